```python
import math
import jax
import jax.numpy as jnp
from jax import lax
import numpy as np

D_MODEL = 1024
BATCH = 8
SEQ = 8192
DEPTH = 2

RMS_EPS = 1e-6
GN_EPS = 1e-5
ROPE_THETA = 10000.0

RET_HEADS = D_MODEL // 256
RET_QK_DIM = 128
RET_V_DIM = 2 * RET_QK_DIM
RET_CHUNK = 128

MLA_HEADS = D_MODEL // 128
MLA_Q_LORA = 3 * D_MODEL // 8
MLA_KV_LORA = D_MODEL // 4
MLA_NOPE = 128
MLA_ROPE = 64
MLA_V = 128
ATTN_BLOCK = 128

S5_WIDTH = D_MODEL
S5_GROUP = 16
S5_GROUPS = S5_WIDTH // S5_GROUP
S5_STATE = 64

N_BRANCH = 3
BRANCH_WIDTH = D_MODEL
FFN_HIDDEN = ((8 * D_MODEL + 3 * 256 - 1) // (3 * 256)) * 256

IN_SPLITS = (
    RET_HEADS * RET_QK_DIM,
    RET_HEADS * RET_QK_DIM,
    RET_HEADS * RET_V_DIM,
    RET_HEADS * RET_V_DIM,
    MLA_Q_LORA,
    MLA_KV_LORA,
    MLA_ROPE,
    S5_WIDTH,
    N_BRANCH * D_MODEL,
)
IN_WIDTH = sum(IN_SPLITS)

kernel_name = "hybrid_retention_mla_s5_gated_encoder"


def _rmsnorm(x, g):
    xf = x.astype(jnp.float32)
    y = xf * lax.rsqrt(jnp.mean(xf * xf, axis=-1, keepdims=True) + RMS_EPS)
    return y.astype(x.dtype) * g


def _rope_tables(seq, dim):
    inv = 1.0 / (ROPE_THETA ** (jnp.arange(0, dim, 2, dtype=jnp.float32) / dim))
    ang = jnp.arange(seq, dtype=jnp.float32)[:, None] * inv[None, :]
    return jnp.cos(ang), jnp.sin(ang)


def _apply_rope(x, cos, sin):
    half = x.shape[-1] // 2
    x1, x2 = x[..., :half], x[..., half:]
    return jnp.concatenate([x1 * cos - x2 * sin, x2 * cos + x1 * sin], axis=-1).astype(x.dtype)


def _retention_dir(q, k, v, log_g, strict):
    b, h, s, dk = q.shape
    dv = v.shape[-1]
    c = RET_CHUNK
    n = s // c
    q = q.reshape(b, h, n, c, dk)
    k = k.reshape(b, h, n, c, dk)
    v = v.reshape(b, h, n, c, dv)
    pos = jnp.arange(c, dtype=jnp.float32)
    diff = pos[:, None] - pos[None, :]
    mask = (diff > 0) if strict else (diff >= 0)
    decay_in = jnp.where(mask, jnp.exp(jnp.where(mask, diff, 0.0)[None] * log_g[:, None, None]), 0.0)
    scores = jnp.einsum('bhnid,bhnjd->bhnij', q, k) * decay_in[None, :, None]
    inner = jnp.einsum('bhnij,bhnje->bhnie', scores, v)
    k_w = jnp.exp((c - 1 - pos)[None, :] * log_g[:, None])
    chunk_kv = jnp.einsum('bhncd,bhnce->nbhde', k * k_w[None, :, None, :, None], v)
    chunk_decay = jnp.exp(c * log_g)[None, :, None, None]

    def step(state, kv):
        return chunk_decay * state + kv, state

    init = jnp.zeros(chunk_kv.shape[1:], chunk_kv.dtype)
    _, states = lax.scan(step, init, chunk_kv)
    q_w = jnp.exp((pos + 1)[None, :] * log_g[:, None])
    cross = jnp.einsum('bhncd,nbhde->bhnce', q, states) * q_w[None, :, None, :, None]
    return (inner + cross).reshape(b, h, s, dv)


def _retention_branch(q_flat, k_flat, v_flat, g_flat, ret_decay, gn_g, cos, sin):
    b, s, _ = q_flat.shape
    q = q_flat.reshape(b, s, RET_HEADS, RET_QK_DIM).transpose(0, 2, 1, 3)
    k = k_flat.reshape(b, s, RET_HEADS, RET_QK_DIM).transpose(0, 2, 1, 3)
    v = v_flat.reshape(b, s, RET_HEADS, RET_V_DIM).transpose(0, 2, 1, 3)
    q = _apply_rope(q, cos, sin) * (RET_QK_DIM ** -0.5)
    k = _apply_rope(k, cos, sin)
    log_g = jax.nn.log_sigmoid(ret_decay.astype(jnp.float32))
    fwd = _retention_dir(q, k, v, log_g[0], strict=False)
    bwd = _retention_dir(q[:, :, ::-1], k[:, :, ::-1], v[:, :, ::-1], log_g[1], strict=True)[:, :, ::-1]
    y = (fwd + bwd).astype(jnp.float32)
    mu = jnp.mean(y, axis=-1, keepdims=True)
    var = jnp.mean(jnp.square(y - mu), axis=-1, keepdims=True)
    y = (y - mu) * lax.rsqrt(var + GN_EPS)
    y = y.transpose(0, 2, 1, 3).reshape(b, s, RET_HEADS * RET_V_DIM).astype(q_flat.dtype) * gn_g
    return jax.nn.silu(g_flat) * y


def _mla_branch(c_q, c_kv, k_rope_flat, q_norm_g, w_uq, kv_norm_g, w_ukv, cos, sin):
    b, s, _ = c_q.shape
    q = (_rmsnorm(c_q, q_norm_g) @ w_uq).reshape(b, s, MLA_HEADS, MLA_NOPE + MLA_ROPE).transpose(0, 2, 1, 3)
    scale = (MLA_NOPE + MLA_ROPE) ** -0.5
    q_nope = q[..., :MLA_NOPE] * scale
    q_rope = _apply_rope(q[..., MLA_NOPE:], cos, sin) * scale
    kv = (_rmsnorm(c_kv, kv_norm_g) @ w_ukv).reshape(b, s, MLA_HEADS, MLA_NOPE + MLA_V).transpose(0, 2, 1, 3)
    k_nope, v = kv[..., :MLA_NOPE], kv[..., MLA_NOPE:]
    k_rope = _apply_rope(k_rope_flat, cos, sin)
    nb = s // ATTN_BLOCK
    qn_blocks = jnp.moveaxis(q_nope.reshape(b, MLA_HEADS, nb, ATTN_BLOCK, MLA_NOPE), 2, 0)
    qr_blocks = jnp.moveaxis(q_rope.reshape(b, MLA_HEADS, nb, ATTN_BLOCK, MLA_ROPE), 2, 0)

    def attend(blk):
        qn, qr = blk
        sc = (jnp.einsum('bhqd,bhkd->bhqk', qn, k_nope)
              + jnp.einsum('bhqd,bkd->bhqk', qr, k_rope))
        p = jax.nn.softmax(sc.astype(jnp.float32), axis=-1)
        return jnp.einsum('bhqk,bhkd->bhqd', p.astype(v.dtype), v)

    out = lax.map(attend, (qn_blocks, qr_blocks))
    return out.transpose(1, 0, 3, 2, 4).reshape(b, s, MLA_HEADS * MLA_V)


def _s5_direction(u_g, a_re, a_im, log_dt, b_re, b_im, c_re, c_im, reverse):
    dt = jnp.exp(log_dt)[:, None]
    ar = jnp.minimum(a_re, -1e-4)
    mag = jnp.exp(dt * ar)
    abar_re = mag * jnp.cos(dt * a_im)
    abar_im = mag * jnp.sin(dt * a_im)
    den = ar * ar + a_im * a_im
    nr = abar_re - 1.0
    ni = abar_im
    coef_re = (nr * ar + ni * a_im) / den
    coef_im = (ni * ar - nr * a_im) / den
    bb_re = coef_re[..., None] * b_re - coef_im[..., None] * b_im
    bb_im = coef_re[..., None] * b_im + coef_im[..., None] * b_re
    bu_re = jnp.einsum('bsgc,gpc->bsgp', u_g, bb_re)
    bu_im = jnp.einsum('bsgc,gpc->bsgp', u_g, bb_im)
    s = u_g.shape[1]
    a_re_t = jnp.broadcast_to(abar_re, (1, s) + abar_re.shape)
    a_im_t = jnp.broadcast_to(abar_im, (1, s) + abar_im.shape)

    def combine(e1, e2):
        a1r, a1i, b1r, b1i = e1
        a2r, a2i, b2r, b2i = e2
        return (a2r * a1r - a2i * a1i,
                a2r * a1i + a2i * a1r,
                a2r * b1r - a2i * b1i + b2r,
                a2r * b1i + a2i * b1r + b2i)

    _, _, xr, xi = lax.associative_scan(combine, (a_re_t, a_im_t, bu_re, bu_im), reverse=reverse, axis=1)
    return jnp.einsum('bsgp,gcp->bsgc', xr, c_re) - jnp.einsum('bsgp,gcp->bsgc', xi, c_im)


def _s5_branch(u, a_re, a_im, log_dt, b_re, b_im, c_re, c_im, d, w_glu):
    b, s, w = u.shape
    u_g = u.reshape(b, s, S5_GROUPS, S5_GROUP)
    y = (_s5_direction(u_g, a_re[0], a_im[0], log_dt[0], b_re[0], b_im[0], c_re[0], c_im[0], reverse=False)
         + _s5_direction(u_g, a_re[1], a_im[1], log_dt[1], b_re[1], b_im[1], c_re[1], c_im[1], reverse=True))
    y = y.reshape(b, s, w).astype(u.dtype) + d * u
    g = jax.nn.gelu(y)
    ga, gb = jnp.split(g @ w_glu, 2, axis=-1)
    return ga * jax.nn.sigmoid(gb)


def _fwd_setup_inputs(seed: int = 0) -> dict:
    key = jax.random.key(seed)
    ks = jax.random.split(key, 24)
    f32 = jnp.float32
    L, D, G, P, C = DEPTH, D_MODEL, S5_GROUPS, S5_STATE, S5_GROUP

    def nrm(k, shape, scale):
        return jax.random.normal(k, shape, f32) * scale

    def gain(k, shape):
        return 1.0 + 0.02 * jax.random.normal(k, shape, f32)

    ret_logit = jnp.log(2.0 ** (5.0 + jnp.arange(RET_HEADS, dtype=f32)) - 1.0)
    ret_decay = ret_logit[None, None, :] + 0.05 * jax.random.normal(ks[3], (L, 2, RET_HEADS), f32)
    a_im_init = math.pi * jnp.arange(P, dtype=f32)

    return {
        "x": jax.random.normal(ks[0], (BATCH, SEQ, D), f32),
        "norm1_g": gain(ks[1], (L, D)),
        "w_in": nrm(ks[2], (L, D, IN_WIDTH), D ** -0.5),
        "ret_decay": ret_decay,
        "ret_gn_g": gain(ks[4], (L, RET_HEADS * RET_V_DIM)),
        "mla_q_norm_g": gain(ks[5], (L, MLA_Q_LORA)),
        "mla_w_uq": nrm(ks[6], (L, MLA_Q_LORA, MLA_HEADS * (MLA_NOPE + MLA_ROPE)), MLA_Q_LORA ** -0.5),
        "mla_kv_norm_g": gain(ks[7], (L, MLA_KV_LORA)),
        "mla_w_ukv": nrm(ks[8], (L, MLA_KV_LORA, MLA_HEADS * (MLA_NOPE + MLA_V)), MLA_KV_LORA ** -0.5),
        "s5_a_re": -0.5 + 0.01 * jax.random.normal(ks[9], (L, 2, G, P), f32),
        "s5_a_im": a_im_init + 0.01 * jax.random.normal(ks[10], (L, 2, G, P), f32),
        "s5_log_dt": jax.random.uniform(ks[11], (L, 2, G), f32, math.log(0.001), math.log(0.1)),
        "s5_b_re": nrm(ks[12], (L, 2, G, P, C), (2 * C) ** -0.5),
        "s5_b_im": nrm(ks[13], (L, 2, G, P, C), (2 * C) ** -0.5),
        "s5_c_re": nrm(ks[14], (L, 2, G, C, P), (2 * P) ** -0.5),
        "s5_c_im": nrm(ks[15], (L, 2, G, C, P), (2 * P) ** -0.5),
        "s5_d": nrm(ks[16], (L, S5_WIDTH), 1.0),
        "s5_w_glu": nrm(ks[17], (L, S5_WIDTH, 2 * S5_WIDTH), S5_WIDTH ** -0.5),
        "w_branch": nrm(ks[18], (L, N_BRANCH, BRANCH_WIDTH, D), BRANCH_WIDTH ** -0.5),
        "w_out": nrm(ks[19], (L, D, D), D ** -0.5),
        "norm2_g": gain(ks[20], (L, D)),
        "ffn_w_gu": nrm(ks[21], (L, D, 2 * FFN_HIDDEN), D ** -0.5),
        "ffn_w_down": nrm(ks[22], (L, FFN_HIDDEN, D), FFN_HIDDEN ** -0.5),
        "final_g": gain(ks[23], (D,)),
    }


def _fwd_reference(x, norm1_g, w_in, ret_decay, ret_gn_g, mla_q_norm_g, mla_w_uq, mla_kv_norm_g, mla_w_ukv,
              s5_a_re, s5_a_im, s5_log_dt, s5_b_re, s5_b_im, s5_c_re, s5_c_im, s5_d, s5_w_glu,
              w_branch, w_out, norm2_g, ffn_w_gu, ffn_w_down, final_g):
    b, s, d = x.shape
    cos_r, sin_r = _rope_tables(s, RET_QK_DIM)
    cos_m, sin_m = _rope_tables(s, MLA_ROPE)
    split_at = list(np.cumsum(IN_SPLITS)[:-1])
    for l in range(DEPTH):
        h = _rmsnorm(x, norm1_g[l])
        (rq, rk, rv, rg, c_q, c_kv, k_rope, s5_u, gate_logits) = jnp.split(h @ w_in[l], split_at, axis=-1)
        y_ret = _retention_branch(rq, rk, rv, rg, ret_decay[l], ret_gn_g[l], cos_r, sin_r)
        y_mla = _mla_branch(c_q, c_kv, k_rope, mla_q_norm_g[l], mla_w_uq[l], mla_kv_norm_g[l], mla_w_ukv[l],
                            cos_m, sin_m)
        y_s5 = _s5_branch(s5_u, s5_a_re[l], s5_a_im[l], s5_log_dt[l], s5_b_re[l], s5_b_im[l],
                          s5_c_re[l], s5_c_im[l], s5_d[l], s5_w_glu[l])
        branches = jnp.stack([y_ret, y_mla, y_s5], axis=2)
        proj = jnp.einsum('bsnk,nkd->bsnd', branches, w_branch[l])
        gates = jax.nn.sigmoid(gate_logits.reshape(b, s, N_BRANCH, d))
        x = x + jnp.sum(gates * proj, axis=2) @ w_out[l]
        h2 = _rmsnorm(x, norm2_g[l])
        f_gate, f_up = jnp.split(h2 @ ffn_w_gu[l], 2, axis=-1)
        x = x + (jax.nn.silu(f_gate) * f_up) @ ffn_w_down[l]
    return _rmsnorm(x, final_g)


import jax as _jax
import jax.numpy as _jnp

TWIN_FORMAT = 'train_step'
FWD_PARAMS = ['x', 'norm1_g', 'w_in', 'ret_decay', 'ret_gn_g', 'mla_q_norm_g', 'mla_w_uq', 'mla_kv_norm_g', 'mla_w_ukv', 's5_a_re', 's5_a_im', 's5_log_dt', 's5_b_re', 's5_b_im', 's5_c_re', 's5_c_im', 's5_d', 's5_w_glu', 'w_branch', 'w_out', 'norm2_g', 'ffn_w_gu', 'ffn_w_down', 'final_g']
TWIN_WEIGHTS = ['norm1_g', 'w_in', 'ret_decay', 'ret_gn_g', 'mla_q_norm_g', 'mla_w_uq', 'mla_kv_norm_g', 'mla_w_ukv', 's5_a_re', 's5_a_im', 's5_log_dt', 's5_b_re', 's5_b_im', 's5_c_re', 's5_c_im', 's5_d', 's5_w_glu', 'w_branch', 'w_out', 'norm2_g', 'ffn_w_gu', 'ffn_w_down', 'final_g']
TWIN_DIFF_INPUT = 'x'
TWIN_INPUTS = ['x', 'norm1_g', 'w_in', 'ret_decay', 'ret_gn_g', 'mla_q_norm_g', 'mla_w_uq', 'mla_kv_norm_g', 'mla_w_ukv', 's5_a_re', 's5_a_im', 's5_log_dt', 's5_b_re', 's5_b_im', 's5_c_re', 's5_c_im', 's5_d', 's5_w_glu', 'w_branch', 'w_out', 'norm2_g', 'ffn_w_gu', 'ffn_w_down', 'final_g', 'loss_target', 'm_norm1_g', 'm_w_in', 'm_ret_decay', 'm_ret_gn_g', 'm_mla_q_norm_g', 'm_mla_w_uq', 'm_mla_kv_norm_g', 'm_mla_w_ukv', 'm_s5_a_re', 'm_s5_a_im', 'm_s5_log_dt', 'm_s5_b_re', 'm_s5_b_im', 'm_s5_c_re', 'm_s5_c_im', 'm_s5_d', 'm_s5_w_glu', 'm_w_branch', 'm_w_out', 'm_norm2_g', 'm_ffn_w_gu', 'm_ffn_w_down', 'm_final_g', 'v_norm1_g', 'v_w_in', 'v_ret_decay', 'v_ret_gn_g', 'v_mla_q_norm_g', 'v_mla_w_uq', 'v_mla_kv_norm_g', 'v_mla_w_ukv', 'v_s5_a_re', 'v_s5_a_im', 'v_s5_log_dt', 'v_s5_b_re', 'v_s5_b_im', 'v_s5_c_re', 'v_s5_c_im', 'v_s5_d', 'v_s5_w_glu', 'v_w_branch', 'v_w_out', 'v_norm2_g', 'v_ffn_w_gu', 'v_ffn_w_down', 'v_final_g']
TWIN_OUTPUTS = ['loss', 'grad_x', 'grad_norm1_g', 'grad_w_in', 'grad_ret_decay', 'grad_ret_gn_g', 'grad_mla_q_norm_g', 'grad_mla_w_uq', 'grad_mla_kv_norm_g', 'grad_mla_w_ukv', 'grad_s5_a_re', 'grad_s5_a_im', 'grad_s5_log_dt', 'grad_s5_b_re', 'grad_s5_b_im', 'grad_s5_c_re', 'grad_s5_c_im', 'grad_s5_d', 'grad_s5_w_glu', 'grad_w_branch', 'grad_w_out', 'grad_norm2_g', 'grad_ffn_w_gu', 'grad_ffn_w_down', 'grad_final_g', 'delta_norm1_g', 'delta_w_in', 'delta_ret_decay', 'delta_ret_gn_g', 'delta_mla_q_norm_g', 'delta_mla_w_uq', 'delta_mla_kv_norm_g', 'delta_mla_w_ukv', 'delta_s5_a_re', 'delta_s5_a_im', 'delta_s5_log_dt', 'delta_s5_b_re', 'delta_s5_b_im', 'delta_s5_c_re', 'delta_s5_c_im', 'delta_s5_d', 'delta_s5_w_glu', 'delta_w_branch', 'delta_w_out', 'delta_norm2_g', 'delta_ffn_w_gu', 'delta_ffn_w_down', 'delta_final_g', 'new_m_norm1_g', 'new_m_w_in', 'new_m_ret_decay', 'new_m_ret_gn_g', 'new_m_mla_q_norm_g', 'new_m_mla_w_uq', 'new_m_mla_kv_norm_g', 'new_m_mla_w_ukv', 'new_m_s5_a_re', 'new_m_s5_a_im', 'new_m_s5_log_dt', 'new_m_s5_b_re', 'new_m_s5_b_im', 'new_m_s5_c_re', 'new_m_s5_c_im', 'new_m_s5_d', 'new_m_s5_w_glu', 'new_m_w_branch', 'new_m_w_out', 'new_m_norm2_g', 'new_m_ffn_w_gu', 'new_m_ffn_w_down', 'new_m_final_g', 'new_v_norm1_g', 'new_v_w_in', 'new_v_ret_decay', 'new_v_ret_gn_g', 'new_v_mla_q_norm_g', 'new_v_mla_w_uq', 'new_v_mla_kv_norm_g', 'new_v_mla_w_ukv', 'new_v_s5_a_re', 'new_v_s5_a_im', 'new_v_s5_log_dt', 'new_v_s5_b_re', 'new_v_s5_b_im', 'new_v_s5_c_re', 'new_v_s5_c_im', 'new_v_s5_d', 'new_v_s5_w_glu', 'new_v_w_branch', 'new_v_w_out', 'new_v_norm2_g', 'new_v_ffn_w_gu', 'new_v_ffn_w_down', 'new_v_final_g']
TWIN_LEAF_KINDS = {'loss': 'loss', 'grad_x': 'grad_x', 'grad_norm1_g': 'grad_w', 'grad_w_in': 'grad_w', 'grad_ret_decay': 'grad_w', 'grad_ret_gn_g': 'grad_w', 'grad_mla_q_norm_g': 'grad_w', 'grad_mla_w_uq': 'grad_w', 'grad_mla_kv_norm_g': 'grad_w', 'grad_mla_w_ukv': 'grad_w', 'grad_s5_a_re': 'grad_w', 'grad_s5_a_im': 'grad_w', 'grad_s5_log_dt': 'grad_w', 'grad_s5_b_re': 'grad_w', 'grad_s5_b_im': 'grad_w', 'grad_s5_c_re': 'grad_w', 'grad_s5_c_im': 'grad_w', 'grad_s5_d': 'grad_w', 'grad_s5_w_glu': 'grad_w', 'grad_w_branch': 'grad_w', 'grad_w_out': 'grad_w', 'grad_norm2_g': 'grad_w', 'grad_ffn_w_gu': 'grad_w', 'grad_ffn_w_down': 'grad_w', 'grad_final_g': 'grad_w', 'delta_norm1_g': 'delta_w', 'delta_w_in': 'delta_w', 'delta_ret_decay': 'delta_w', 'delta_ret_gn_g': 'delta_w', 'delta_mla_q_norm_g': 'delta_w', 'delta_mla_w_uq': 'delta_w', 'delta_mla_kv_norm_g': 'delta_w', 'delta_mla_w_ukv': 'delta_w', 'delta_s5_a_re': 'delta_w', 'delta_s5_a_im': 'delta_w', 'delta_s5_log_dt': 'delta_w', 'delta_s5_b_re': 'delta_w', 'delta_s5_b_im': 'delta_w', 'delta_s5_c_re': 'delta_w', 'delta_s5_c_im': 'delta_w', 'delta_s5_d': 'delta_w', 'delta_s5_w_glu': 'delta_w', 'delta_w_branch': 'delta_w', 'delta_w_out': 'delta_w', 'delta_norm2_g': 'delta_w', 'delta_ffn_w_gu': 'delta_w', 'delta_ffn_w_down': 'delta_w', 'delta_final_g': 'delta_w', 'new_m_norm1_g': 'new_m', 'new_m_w_in': 'new_m', 'new_m_ret_decay': 'new_m', 'new_m_ret_gn_g': 'new_m', 'new_m_mla_q_norm_g': 'new_m', 'new_m_mla_w_uq': 'new_m', 'new_m_mla_kv_norm_g': 'new_m', 'new_m_mla_w_ukv': 'new_m', 'new_m_s5_a_re': 'new_m', 'new_m_s5_a_im': 'new_m', 'new_m_s5_log_dt': 'new_m', 'new_m_s5_b_re': 'new_m', 'new_m_s5_b_im': 'new_m', 'new_m_s5_c_re': 'new_m', 'new_m_s5_c_im': 'new_m', 'new_m_s5_d': 'new_m', 'new_m_s5_w_glu': 'new_m', 'new_m_w_branch': 'new_m', 'new_m_w_out': 'new_m', 'new_m_norm2_g': 'new_m', 'new_m_ffn_w_gu': 'new_m', 'new_m_ffn_w_down': 'new_m', 'new_m_final_g': 'new_m', 'new_v_norm1_g': 'new_v', 'new_v_w_in': 'new_v', 'new_v_ret_decay': 'new_v', 'new_v_ret_gn_g': 'new_v', 'new_v_mla_q_norm_g': 'new_v', 'new_v_mla_w_uq': 'new_v', 'new_v_mla_kv_norm_g': 'new_v', 'new_v_mla_w_ukv': 'new_v', 'new_v_s5_a_re': 'new_v', 'new_v_s5_a_im': 'new_v', 'new_v_s5_log_dt': 'new_v', 'new_v_s5_b_re': 'new_v', 'new_v_s5_b_im': 'new_v', 'new_v_s5_c_re': 'new_v', 'new_v_s5_c_im': 'new_v', 'new_v_s5_d': 'new_v', 'new_v_s5_w_glu': 'new_v', 'new_v_w_branch': 'new_v', 'new_v_w_out': 'new_v', 'new_v_norm2_g': 'new_v', 'new_v_ffn_w_gu': 'new_v', 'new_v_ffn_w_down': 'new_v', 'new_v_final_g': 'new_v'}


def _forward(args):
    return _fwd_reference(*[args[k] for k in FWD_PARAMS])


def _output_shape():
    def fwd():
        inp = _fwd_setup_inputs(0)
        return _fwd_reference(*[inp[k] for k in FWD_PARAMS])
    out = _jax.eval_shape(fwd)
    return out.shape, out.dtype

N_MICROBATCH = 1
ADAM_LR = 0.001
ADAM_B1 = 0.9
ADAM_B2 = 0.999
ADAM_EPS = 1e-08
ADAM_WD = 0.01
ADAM_STEP = 10
PER_EXAMPLE_BATCH_AXIS = {'x': 0, 'loss_target': 0}
SHARED_INPUTS = []
_WEIGHT_DTYPES = {'norm1_g': _jnp.float32, 'w_in': _jnp.float32, 'ret_decay': _jnp.float32, 'ret_gn_g': _jnp.float32, 'mla_q_norm_g': _jnp.float32, 'mla_w_uq': _jnp.float32, 'mla_kv_norm_g': _jnp.float32, 'mla_w_ukv': _jnp.float32, 's5_a_re': _jnp.float32, 's5_a_im': _jnp.float32, 's5_log_dt': _jnp.float32, 's5_b_re': _jnp.float32, 's5_b_im': _jnp.float32, 's5_c_re': _jnp.float32, 's5_c_im': _jnp.float32, 's5_d': _jnp.float32, 's5_w_glu': _jnp.float32, 'w_branch': _jnp.float32, 'w_out': _jnp.float32, 'norm2_g': _jnp.float32, 'ffn_w_gu': _jnp.float32, 'ffn_w_down': _jnp.float32, 'final_g': _jnp.float32}
MOMENT_SCALE = {'norm1_g': 1.982134e-01, 'w_in': 6.845055e-02, 'ret_decay': 8.068808e-01, 'ret_gn_g': 9.386596e-02, 'mla_q_norm_g': 2.707743e-02, 'mla_w_uq': 1.346234e-02, 'mla_kv_norm_g': 4.806166e-02, 'mla_w_ukv': 1.626608e-02, 's5_a_re': 2.940741e-03, 's5_a_im': 3.120256e-03, 's5_log_dt': 2.844790e+00, 's5_b_re': 1.823209e-03, 's5_b_im': 1.821733e-03, 's5_c_re': 3.624459e-03, 's5_c_im': 3.627721e-03, 's5_d': 5.631709e-02, 's5_w_glu': 3.845881e-02, 'w_branch': 5.875155e-02, 'w_out': 1.017171e-01, 'norm2_g': 1.905722e-01, 'ffn_w_gu': 7.810115e-02, 'ffn_w_down': 1.276847e-01, 'final_g': 6.392276e+01}


def _to_microbatches(a, axis):
    t = _jnp.moveaxis(a, axis, 0)
    t = t.reshape((N_MICROBATCH, t.shape[0] // N_MICROBATCH) + t.shape[1:])
    return _jnp.moveaxis(t, 1, axis + 1)


def setup_inputs(seed: int = 0) -> dict:
    inp = _fwd_setup_inputs(seed)
    key = _jax.random.fold_in(_jax.random.key(seed), 7919)
    shape, _ = _output_shape()
    out = dict(inp)
    out["loss_target"] = _jax.random.normal(_jax.random.fold_in(key, 0), shape, _jnp.float32)
    for i, name in enumerate(TWIN_WEIGHTS):
        w = inp[name].astype(_jnp.float32)
        if MOMENT_SCALE is None:
            s = _jnp.sqrt(_jnp.mean(_jnp.square(w)) + 1e-30)
        else:
            s = MOMENT_SCALE[name]
        km, kv = _jax.random.split(_jax.random.fold_in(key, i + 1))
        out[name] = w
        out["m_" + name] = s * _jax.random.normal(km, w.shape, _jnp.float32)
        out["v_" + name] = (s * s) * _jax.random.uniform(kv, w.shape, _jnp.float32, 0.5, 1.5)
    if N_MICROBATCH > 1:
        for name, axis in PER_EXAMPLE_BATCH_AXIS.items():
            out[name] = _to_microbatches(out[name], axis)
    return {'x': out['x'], 'norm1_g': out['norm1_g'], 'w_in': out['w_in'], 'ret_decay': out['ret_decay'], 'ret_gn_g': out['ret_gn_g'], 'mla_q_norm_g': out['mla_q_norm_g'], 'mla_w_uq': out['mla_w_uq'], 'mla_kv_norm_g': out['mla_kv_norm_g'], 'mla_w_ukv': out['mla_w_ukv'], 's5_a_re': out['s5_a_re'], 's5_a_im': out['s5_a_im'], 's5_log_dt': out['s5_log_dt'], 's5_b_re': out['s5_b_re'], 's5_b_im': out['s5_b_im'], 's5_c_re': out['s5_c_re'], 's5_c_im': out['s5_c_im'], 's5_d': out['s5_d'], 's5_w_glu': out['s5_w_glu'], 'w_branch': out['w_branch'], 'w_out': out['w_out'], 'norm2_g': out['norm2_g'], 'ffn_w_gu': out['ffn_w_gu'], 'ffn_w_down': out['ffn_w_down'], 'final_g': out['final_g'], 'loss_target': out['loss_target'], 'm_norm1_g': out['m_norm1_g'], 'm_w_in': out['m_w_in'], 'm_ret_decay': out['m_ret_decay'], 'm_ret_gn_g': out['m_ret_gn_g'], 'm_mla_q_norm_g': out['m_mla_q_norm_g'], 'm_mla_w_uq': out['m_mla_w_uq'], 'm_mla_kv_norm_g': out['m_mla_kv_norm_g'], 'm_mla_w_ukv': out['m_mla_w_ukv'], 'm_s5_a_re': out['m_s5_a_re'], 'm_s5_a_im': out['m_s5_a_im'], 'm_s5_log_dt': out['m_s5_log_dt'], 'm_s5_b_re': out['m_s5_b_re'], 'm_s5_b_im': out['m_s5_b_im'], 'm_s5_c_re': out['m_s5_c_re'], 'm_s5_c_im': out['m_s5_c_im'], 'm_s5_d': out['m_s5_d'], 'm_s5_w_glu': out['m_s5_w_glu'], 'm_w_branch': out['m_w_branch'], 'm_w_out': out['m_w_out'], 'm_norm2_g': out['m_norm2_g'], 'm_ffn_w_gu': out['m_ffn_w_gu'], 'm_ffn_w_down': out['m_ffn_w_down'], 'm_final_g': out['m_final_g'], 'v_norm1_g': out['v_norm1_g'], 'v_w_in': out['v_w_in'], 'v_ret_decay': out['v_ret_decay'], 'v_ret_gn_g': out['v_ret_gn_g'], 'v_mla_q_norm_g': out['v_mla_q_norm_g'], 'v_mla_w_uq': out['v_mla_w_uq'], 'v_mla_kv_norm_g': out['v_mla_kv_norm_g'], 'v_mla_w_ukv': out['v_mla_w_ukv'], 'v_s5_a_re': out['v_s5_a_re'], 'v_s5_a_im': out['v_s5_a_im'], 'v_s5_log_dt': out['v_s5_log_dt'], 'v_s5_b_re': out['v_s5_b_re'], 'v_s5_b_im': out['v_s5_b_im'], 'v_s5_c_re': out['v_s5_c_re'], 'v_s5_c_im': out['v_s5_c_im'], 'v_s5_d': out['v_s5_d'], 'v_s5_w_glu': out['v_s5_w_glu'], 'v_w_branch': out['v_w_branch'], 'v_w_out': out['v_w_out'], 'v_norm2_g': out['v_norm2_g'], 'v_ffn_w_gu': out['v_ffn_w_gu'], 'v_ffn_w_down': out['v_ffn_w_down'], 'v_final_g': out['v_final_g']}


def _loss(weights, diff, rest, loss_target):
    with _jax.named_scope("forward"):
        args = {**rest, TWIN_DIFF_INPUT: diff, **{k: w.astype(_WEIGHT_DTYPES[k]) for k, w in weights.items()}}
        y = _forward(args)
    with _jax.named_scope("loss_head"):
        err = _jnp.square(y.astype(_jnp.float32) - loss_target)
        return 0.5 * _jnp.sum(_jnp.mean(err, axis=-1)) if err.ndim else 0.5 * err


def _adamw(w, g, m, v):
    m = ADAM_B1 * m + (1.0 - ADAM_B1) * g
    v = ADAM_B2 * v + (1.0 - ADAM_B2) * _jnp.square(g)
    m_hat = m / (1.0 - ADAM_B1 ** ADAM_STEP)
    v_hat = v / (1.0 - ADAM_B2 ** ADAM_STEP)
    delta = -ADAM_LR * (m_hat / (_jnp.sqrt(v_hat) + ADAM_EPS) + ADAM_WD * w)
    return delta, m, v


def reference(x, norm1_g, w_in, ret_decay, ret_gn_g, mla_q_norm_g, mla_w_uq, mla_kv_norm_g, mla_w_ukv, s5_a_re, s5_a_im, s5_log_dt, s5_b_re, s5_b_im, s5_c_re, s5_c_im, s5_d, s5_w_glu, w_branch, w_out, norm2_g, ffn_w_gu, ffn_w_down, final_g, loss_target, m_norm1_g, m_w_in, m_ret_decay, m_ret_gn_g, m_mla_q_norm_g, m_mla_w_uq, m_mla_kv_norm_g, m_mla_w_ukv, m_s5_a_re, m_s5_a_im, m_s5_log_dt, m_s5_b_re, m_s5_b_im, m_s5_c_re, m_s5_c_im, m_s5_d, m_s5_w_glu, m_w_branch, m_w_out, m_norm2_g, m_ffn_w_gu, m_ffn_w_down, m_final_g, v_norm1_g, v_w_in, v_ret_decay, v_ret_gn_g, v_mla_q_norm_g, v_mla_w_uq, v_mla_kv_norm_g, v_mla_w_ukv, v_s5_a_re, v_s5_a_im, v_s5_log_dt, v_s5_b_re, v_s5_b_im, v_s5_c_re, v_s5_c_im, v_s5_d, v_s5_w_glu, v_w_branch, v_w_out, v_norm2_g, v_ffn_w_gu, v_ffn_w_down, v_final_g):
    given = dict(x=x, norm1_g=norm1_g, w_in=w_in, ret_decay=ret_decay, ret_gn_g=ret_gn_g, mla_q_norm_g=mla_q_norm_g, mla_w_uq=mla_w_uq, mla_kv_norm_g=mla_kv_norm_g, mla_w_ukv=mla_w_ukv, s5_a_re=s5_a_re, s5_a_im=s5_a_im, s5_log_dt=s5_log_dt, s5_b_re=s5_b_re, s5_b_im=s5_b_im, s5_c_re=s5_c_re, s5_c_im=s5_c_im, s5_d=s5_d, s5_w_glu=s5_w_glu, w_branch=w_branch, w_out=w_out, norm2_g=norm2_g, ffn_w_gu=ffn_w_gu, ffn_w_down=ffn_w_down, final_g=final_g, loss_target=loss_target, m_norm1_g=m_norm1_g, m_w_in=m_w_in, m_ret_decay=m_ret_decay, m_ret_gn_g=m_ret_gn_g, m_mla_q_norm_g=m_mla_q_norm_g, m_mla_w_uq=m_mla_w_uq, m_mla_kv_norm_g=m_mla_kv_norm_g, m_mla_w_ukv=m_mla_w_ukv, m_s5_a_re=m_s5_a_re, m_s5_a_im=m_s5_a_im, m_s5_log_dt=m_s5_log_dt, m_s5_b_re=m_s5_b_re, m_s5_b_im=m_s5_b_im, m_s5_c_re=m_s5_c_re, m_s5_c_im=m_s5_c_im, m_s5_d=m_s5_d, m_s5_w_glu=m_s5_w_glu, m_w_branch=m_w_branch, m_w_out=m_w_out, m_norm2_g=m_norm2_g, m_ffn_w_gu=m_ffn_w_gu, m_ffn_w_down=m_ffn_w_down, m_final_g=m_final_g, v_norm1_g=v_norm1_g, v_w_in=v_w_in, v_ret_decay=v_ret_decay, v_ret_gn_g=v_ret_gn_g, v_mla_q_norm_g=v_mla_q_norm_g, v_mla_w_uq=v_mla_w_uq, v_mla_kv_norm_g=v_mla_kv_norm_g, v_mla_w_ukv=v_mla_w_ukv, v_s5_a_re=v_s5_a_re, v_s5_a_im=v_s5_a_im, v_s5_log_dt=v_s5_log_dt, v_s5_b_re=v_s5_b_re, v_s5_b_im=v_s5_b_im, v_s5_c_re=v_s5_c_re, v_s5_c_im=v_s5_c_im, v_s5_d=v_s5_d, v_s5_w_glu=v_s5_w_glu, v_w_branch=v_w_branch, v_w_out=v_w_out, v_norm2_g=v_norm2_g, v_ffn_w_gu=v_ffn_w_gu, v_ffn_w_down=v_ffn_w_down, v_final_g=v_final_g)
    weights = {n: given[n] for n in TWIN_WEIGHTS}
    shared = {n: given[n] for n in SHARED_INPUTS}
    per_example = {n: given[n] for n in ['x']}
    grad_fn = _jax.value_and_grad(_loss, argnums=(0, 1))

    def one_microbatch(ex, loss_target):
        ex = dict(ex)
        diff = ex.pop(TWIN_DIFF_INPUT)
        return grad_fn(weights, diff, {**shared, **ex}, loss_target)

    if N_MICROBATCH == 1:
        loss, (grad_w, grad_x) = one_microbatch(per_example, given["loss_target"])
    else:
        def body(carry, xs):
            loss_sum, grad_sum = carry
            l_k, (gw_k, gx_k) = one_microbatch(xs[0], xs[1])
            with _jax.named_scope("update"):
                return (loss_sum + l_k, _jax.tree.map(_jnp.add, grad_sum, gw_k)), gx_k

        init = (_jnp.zeros((), _jnp.float32), _jax.tree.map(_jnp.zeros_like, weights))
        (loss, grad_w), grad_x = _jax.lax.scan(body, init, (per_example, given["loss_target"]))
    with _jax.named_scope("update"):
        delta_w, new_m, new_v = {}, {}, {}
        for n in TWIN_WEIGHTS:
            delta_w[n], new_m[n], new_v[n] = _adamw(weights[n], grad_w[n], given["m_" + n], given["v_" + n])
    return (loss, grad_x, *[grad_w[n] for n in TWIN_WEIGHTS], *[delta_w[n] for n in TWIN_WEIGHTS],
            *[new_m[n] for n in TWIN_WEIGHTS], *[new_v[n] for n in TWIN_WEIGHTS])
```

```python
import functools
import math

import jax
import jax.numpy as jnp
from jax import lax
from jax.experimental import pallas as pl
from jax.experimental.pallas import tpu as pltpu

f32 = jnp.float32
bf16 = jnp.bfloat16
SDS = jax.ShapeDtypeStruct
MESH = pl.DeviceIdType.MESH

D = 1024
DEPTH = 2
RMS_EPS = 1e-6
GN_EPS = 1e-5
ROPE_THETA = 10000.0
RET_HEADS = 4
RET_DK = 128
RET_DV = 256
RET_CHUNK = 128
MLA_HEADS = 8
MLA_Q_LORA = 384
MLA_KV_LORA = 256
MLA_NOPE = 128
MLA_ROPE = 64
MLA_V = 128
MLA_QW = 256
S5_G = 64
S5_P = 64
S5_C = 16
S5_NJ = 8
S5_SEG = 8
FFN_H = 2816
ADAM_LR = 0.001
ADAM_B1 = 0.9
ADAM_B2 = 0.999
ADAM_EPS = 1e-08
ADAM_WD = 0.01
ADAM_STEP = 10
VMEM_BIG = 56 * 1024 * 1024

W_NAMES = ['norm1_g', 'w_in', 'ret_decay', 'ret_gn_g', 'mla_q_norm_g', 'mla_w_uq', 'mla_kv_norm_g', 'mla_w_ukv',
           's5_a_re', 's5_a_im', 's5_log_dt', 's5_b_re', 's5_b_im', 's5_c_re', 's5_c_im', 's5_d', 's5_w_glu',
           'w_branch', 'w_out', 'norm2_g', 'ffn_w_gu', 'ffn_w_down', 'final_g']
BIG = {'w_in': 2, 'mla_w_uq': 2, 'mla_w_ukv': 2, 's5_w_glu': 2, 'w_branch': 2, 'w_out': 1, 'ffn_w_gu': 2, 'ffn_w_down': 1}
SMALL = [n for n in W_NAMES if n not in BIG]


def _pick(n, cands=(512, 384, 256, 128)):
    if n <= 1024:
        return n
    for c in cands:
        if n % c == 0:
            return c
    raise ValueError(n)


def _params(sem, vmem=None):
    return pltpu.CompilerParams(dimension_semantics=sem, vmem_limit_bytes=vmem)


def _mm(a, b, *, tb=False, res=None, out_dtype=f32, name):
    M, K = a.shape
    N = b.shape[0] if tb else b.shape[1]
    tm = _pick(M)
    tn = _pick(N)
    tk = K if K <= 3072 else _pick(K, (1408, 1024, 512))
    nk = K // tk
    assert M % tm == 0 and N % tn == 0 and K % tk == 0

    def body(*refs):
        if res is None:
            a_ref, b_ref, o_ref, acc = refs
        else:
            a_ref, b_ref, r_ref, o_ref, acc = refs
        k = pl.program_id(2)
        dn = (((1,), (1 if tb else 0,)), ((), ()))
        part = lax.dot_general(a_ref[...].astype(bf16), b_ref[...].astype(bf16), dn, preferred_element_type=f32)

        @pl.when(k == 0)
        def _():
            acc[...] = part

        @pl.when(k > 0)
        def _():
            acc[...] += part

        @pl.when(k == nk - 1)
        def _():
            v = acc[...]
            if res is not None:
                v = v + r_ref[...]
            o_ref[...] = v.astype(out_dtype)

    in_specs = [pl.BlockSpec((tm, tk), lambda i, j, k: (i, k)),
                pl.BlockSpec((tn, tk), lambda i, j, k: (j, k)) if tb else pl.BlockSpec((tk, tn), lambda i, j, k: (k, j))]
    args = [a, b]
    if res is not None:
        in_specs.append(pl.BlockSpec((tm, tn), lambda i, j, k: (i, j)))
        args.append(res)
    return pl.pallas_call(
        body, grid=(M // tm, N // tn, nk), in_specs=in_specs,
        out_specs=pl.BlockSpec((tm, tn), lambda i, j, k: (i, j)),
        out_shape=SDS((M, N), out_dtype), scratch_shapes=[pltpu.VMEM((tm, tn), f32)],
        compiler_params=_params(("parallel", "parallel", "arbitrary"), VMEM_BIG), name=name)(*args)


def _mmT(a, b, *, name):
    S, M = a.shape
    N = b.shape[1]
    tm = _pick(M)
    tn = _pick(N)
    tk = min(S, 1024)
    nk = S // tk

    def body(a_ref, b_ref, o_ref):
        k = pl.program_id(2)
        part = lax.dot_general(a_ref[...].astype(bf16), b_ref[...].astype(bf16), (((0,), (0,)), ((), ())),
                               preferred_element_type=f32)

        @pl.when(k == 0)
        def _():
            o_ref[...] = part

        @pl.when(k > 0)
        def _():
            o_ref[...] += part

    return pl.pallas_call(
        body, grid=(M // tm, N // tn, nk),
        in_specs=[pl.BlockSpec((tk, tm), lambda i, j, k: (k, i)), pl.BlockSpec((tk, tn), lambda i, j, k: (k, j))],
        out_specs=pl.BlockSpec((tm, tn), lambda i, j, k: (i, j)),
        out_shape=SDS((M, N), f32),
        compiler_params=_params(("parallel", "parallel", "arbitrary"), VMEM_BIG), name=name)(a, b)


def _pw(fn, ins, in_specs, outs, out_specs, grid, *, n_acc=0, name):
    n_in = len(ins)
    n_out = len(outs)

    def body(*refs):
        vals = fn(*[r[...] for r in refs[:n_in]])
        if not isinstance(vals, (tuple, list)):
            vals = (vals,)
        orefs = refs[n_in:]
        for r, v in zip(orefs[:n_out - n_acc], vals[:n_out - n_acc]):
            r[...] = v.astype(r.dtype)
        if n_acc:
            i = pl.program_id(1)

            @pl.when(i == 0)
            def _():
                for r, v in zip(orefs[n_out - n_acc:], vals[n_out - n_acc:]):
                    r[...] = v.astype(r.dtype)

            @pl.when(i > 0)
            def _():
                for r, v in zip(orefs[n_out - n_acc:], vals[n_out - n_acc:]):
                    r[...] += v.astype(r.dtype)

    res = pl.pallas_call(
        body, grid=grid, in_specs=in_specs, out_specs=out_specs, out_shape=outs,
        compiler_params=_params(("parallel", "arbitrary"), VMEM_BIG), name=name)(*ins)
    return res


def _row(T, w, col=None):
    if col is None:
        return pl.BlockSpec((T, w), lambda j, i: (i, 0))
    return pl.BlockSpec((T, w), lambda j, i: (i, col(j)))


def _par(w, col=None):
    if col is None:
        return pl.BlockSpec((1, w), lambda j, i: (0, 0))
    return pl.BlockSpec((1, w), lambda j, i: (0, col(j)))


def _rms(x, g):
    return x * lax.rsqrt(jnp.mean(x * x, axis=-1, keepdims=True) + RMS_EPS) * g


def _rope(x, cos, sinm, half):
    if half == 64:
        partner = pltpu.roll(x, 64, axis=1)
    else:
        lane = lax.broadcasted_iota(jnp.int32, x.shape, 1)
        partner = jnp.where((lane % (2 * half)) < half, pltpu.roll(x, 128 - half, axis=1), pltpu.roll(x, half, axis=1))
    return x * cos + partner * sinm


def _rope_t(x, cos, sinm, half):
    return _rope(x, cos, -sinm, half)


def _rmsnorm_fwd(x, g, *, name):
    S, W = x.shape
    T = min(S, 512)
    return _pw(lambda xv, gv: _rms(xv, gv), [x, g], [_row(T, W), _par(W)], [SDS((S, W), bf16)], [_row(T, W)],
               (1, S // T), name=name)[0]


def _rmsnorm_bwd(x, g, dh, dres, *, name):
    S, W = x.shape
    T = min(S, 512)

    def fn(xv, gv, dhv, drv):
        _, vjp = jax.vjp(_rms, xv, gv)
        dx, dg = vjp(dhv)
        return dx + drv, dg

    return _pw(fn, [x, g, dh, dres], [_row(T, W), _par(W), _row(T, W), _row(T, W)],
               [SDS((S, W), f32), SDS((1, W), f32)], [_row(T, W), _par(W)], (1, S // T), n_acc=1, name=name)


def _ret_tables(lg, reverse):
    C = RET_CHUNK
    ii = lax.broadcasted_iota(jnp.int32, (C, C), 0).astype(f32)
    jj = lax.broadcasted_iota(jnp.int32, (C, C), 1).astype(f32)
    if not reverse:
        E = ii - jj
        mask = E >= 0
        eq = ii + 1.0
        ek = (C - 1.0) - ii
    else:
        E = jj - ii
        mask = E > 0
        eq = C - ii
        ek = ii
    Dm = jnp.where(mask, jnp.exp(jnp.where(mask, E, 0.0) * lg), 0.0)
    Em = jnp.where(mask, E, 0.0)
    qw = jnp.exp(eq * lg)
    kw = jnp.exp(ek * lg)
    qw2 = jnp.concatenate([qw, qw], axis=1)
    return Dm, Em, eq, ek, qw, kw, qw2, jnp.exp(C * lg)


def _dot(a, b, dims):
    return lax.dot_general(a.astype(bf16), b.astype(bf16), (dims, ((), ())), preferred_element_type=f32)


NN = ((1,), (0,))
NT = ((1,), (1,))
TN = ((0,), (0,))


def _ret_dir_fwd(zr, lg, cos, sinm, *, reverse, name):
    S = zr.shape[0]
    C = RET_CHUNK
    TB = min(S, 512)
    nc = TB // C
    NB = S // TB
    d = 1 if reverse else 0
    scale = RET_DK ** -0.5

    def tb(b):
        return (NB - 1 - b) if reverse else b

    def body(lg_ref, q_ref, k_ref, v_ref, cos_ref, sin_ref, y_ref, st_ref, state):
        h = pl.program_id(0)
        b = pl.program_id(1)

        @pl.when(b == 0)
        def _():
            state[...] = jnp.zeros_like(state)

        Dm, _, _, _, _, kw, qw2, gC = _ret_tables(lg_ref[d, h], reverse)
        order = range(nc - 1, -1, -1) if reverse else range(nc)
        for c in order:
            rows = pl.ds(c * C, C)
            q = _rope(q_ref[rows, :], cos_ref[rows, :], sin_ref[rows, :], 64) * scale
            k = _rope(k_ref[rows, :], cos_ref[rows, :], sin_ref[rows, :], 64)
            v = v_ref[rows, :]
            st = state[...]
            st_ref[0, c] = st
            s = _dot(q, k, NT) * Dm
            o = _dot(s, v, NN) + _dot(q, st, NN) * qw2
            y_ref[rows, :] = o
            state[...] = gC * st + _dot(k * kw, v, TN)

    return pl.pallas_call(
        body, grid=(RET_HEADS, NB),
        in_specs=[pl.BlockSpec(memory_space=pltpu.SMEM),
                  pl.BlockSpec((TB, 128), lambda h, b: (tb(b), h)),
                  pl.BlockSpec((TB, 128), lambda h, b: (tb(b), 4 + h)),
                  pl.BlockSpec((TB, 256), lambda h, b: (tb(b), 4 + h)),
                  pl.BlockSpec((TB, 128), lambda h, b: (tb(b), 0)),
                  pl.BlockSpec((TB, 128), lambda h, b: (tb(b), 0))],
        out_specs=[pl.BlockSpec((TB, 256), lambda h, b: (tb(b), h)),
                   pl.BlockSpec((1, nc, 128, 256), lambda h, b: (h, tb(b), 0, 0))],
        out_shape=[SDS((S, 1024), f32), SDS((RET_HEADS, S // C, 128, 256), f32)],
        scratch_shapes=[pltpu.VMEM((128, 256), f32)],
        compiler_params=_params(("parallel", "arbitrary")), name=name)(lg, zr, zr, zr, cos, sinm)


def _ret_dir_bwd(zr, lg, cos, sinm, dy, states, *, reverse, name):
    S = zr.shape[0]
    C = RET_CHUNK
    TB = min(S, 512)
    nc = TB // C
    NB = S // TB
    d = 1 if reverse else 0
    scale = RET_DK ** -0.5

    def tb(b):
        return b if reverse else (NB - 1 - b)

    def body(lg_ref, q_ref, k_ref, v_ref, cos_ref, sin_ref, dy_ref, st_ref, dq_ref, dk_ref, dv_ref, dlg_ref, dstate):
        h = pl.program_id(0)
        b = pl.program_id(1)

        @pl.when(b == 0)
        def _():
            dstate[...] = jnp.zeros_like(dstate)
            dlg_ref[...] = jnp.zeros_like(dlg_ref)

        Dm, Em, eq, ek, qw, kw, qw2, gC = _ret_tables(lg_ref[d, h], reverse)
        order = range(nc) if reverse else range(nc - 1, -1, -1)
        dlg = jnp.zeros((), f32)
        for c in order:
            rows = pl.ds(c * C, C)
            cs, sn = cos_ref[rows, :], sin_ref[rows, :]
            q = _rope(q_ref[rows, :], cs, sn, 64) * scale
            k = _rope(k_ref[rows, :], cs, sn, 64)
            v = v_ref[rows, :]
            do = dy_ref[rows, :]
            st = st_ref[0, c]
            ds = dstate[...]
            p = _dot(q, k, NT)
            a = p * Dm
            dp = _dot(do, v, NT) * Dm
            dq_cross = _dot(do, st, NT) * qw
            dk_cross = _dot(v, ds, NT) * kw
            dq = _dot(dp, k, NN) + dq_cross
            dk = _dot(dp, q, TN) + dk_cross
            dv = _dot(a, do, TN) + _dot(k * kw, ds, NN)
            dlg = dlg + jnp.sum(dp * p * Em) + jnp.sum(dq_cross * q * eq) + jnp.sum(dk_cross * k * ek) \
                + C * gC * jnp.sum(ds * st)
            dstate[...] = gC * ds + _dot(q * qw, do, TN)
            dq_ref[rows, :] = _rope_t(dq, cs, sn, 64) * scale
            dk_ref[rows, :] = _rope_t(dk, cs, sn, 64)
            dv_ref[rows, :] = dv
        dlg_ref[...] += jnp.full(dlg_ref.shape, dlg, f32)

    return pl.pallas_call(
        body, grid=(RET_HEADS, NB),
        in_specs=[pl.BlockSpec(memory_space=pltpu.SMEM),
                  pl.BlockSpec((TB, 128), lambda h, b: (tb(b), h)),
                  pl.BlockSpec((TB, 128), lambda h, b: (tb(b), 4 + h)),
                  pl.BlockSpec((TB, 256), lambda h, b: (tb(b), 4 + h)),
                  pl.BlockSpec((TB, 128), lambda h, b: (tb(b), 0)),
                  pl.BlockSpec((TB, 128), lambda h, b: (tb(b), 0)),
                  pl.BlockSpec((TB, 256), lambda h, b: (tb(b), h)),
                  pl.BlockSpec((1, nc, 128, 256), lambda h, b: (h, tb(b), 0, 0))],
        out_specs=[pl.BlockSpec((TB, 128), lambda h, b: (tb(b), h)),
                   pl.BlockSpec((TB, 128), lambda h, b: (tb(b), h)),
                   pl.BlockSpec((TB, 256), lambda h, b: (tb(b), h)),
                   pl.BlockSpec((1, 1, 128), lambda h, b: (h, 0, 0))],
        out_shape=[SDS((S, 512), f32), SDS((S, 512), f32), SDS((S, 1024), f32), SDS((RET_HEADS, 1, 128), f32)],
        scratch_shapes=[pltpu.VMEM((128, 256), f32)],
        compiler_params=_params(("parallel", "arbitrary")), name=name)(lg, zr, zr, zr, cos, sinm, dy, states)


def _gn_gate(yf, yb, g, gn):
    y = yf + yb
    mu = jnp.mean(y, axis=-1, keepdims=True)
    var = jnp.mean(jnp.square(y - mu), axis=-1, keepdims=True)
    yn = (y - mu) * lax.rsqrt(var + GN_EPS)
    return jax.nn.silu(g) * (yn * gn)


def _flash_fwd(Q, K, kv, *, name):
    S = Q.shape[0]
    tq = min(S, 512)
    tk = min(S, 512)
    nk = S // tk

    def body(q_ref, k_ref, v_ref, o_ref, l_ref, m_s, l_s, acc):
        kk = pl.program_id(2)

        @pl.when(kk == 0)
        def _():
            m_s[...] = jnp.full_like(m_s, -jnp.inf)
            l_s[...] = jnp.zeros_like(l_s)
            acc[...] = jnp.zeros_like(acc)

        s = lax.dot_general(q_ref[...], k_ref[...], (NT, ((), ())), preferred_element_type=f32)
        m_prev = m_s[...]
        m_new = jnp.maximum(m_prev, jnp.max(s, axis=-1, keepdims=True))
        p = jnp.exp(s - m_new[:, :1])
        alpha = jnp.exp(m_prev - m_new)
        l_s[...] = alpha * l_s[...] + jnp.sum(p, axis=-1, keepdims=True)
        acc[...] = alpha * acc[...] + lax.dot_general(p.astype(bf16), v_ref[...], (NN, ((), ())), preferred_element_type=f32)
        m_s[...] = m_new

        @pl.when(kk == nk - 1)
        def _():
            o_ref[...] = (acc[...] / l_s[...]).astype(bf16)
            l_ref[...] = m_s[...] + jnp.log(l_s[...])

    return pl.pallas_call(
        body, grid=(MLA_HEADS, S // tq, nk),
        in_specs=[pl.BlockSpec((tq, 256), lambda h, i, k: (i, h)),
                  pl.BlockSpec((tk, 256), lambda h, i, k: (k, h)),
                  pl.BlockSpec((tk, 128), lambda h, i, k: (k, 2 * h + 1))],
        out_specs=[pl.BlockSpec((tq, 128), lambda h, i, k: (i, h)), pl.BlockSpec((tq, 128), lambda h, i, k: (i, h))],
        out_shape=[SDS((S, 1024), bf16), SDS((S, 1024), f32)],
        scratch_shapes=[pltpu.VMEM((tq, 128), f32), pltpu.VMEM((tq, 128), f32), pltpu.VMEM((tq, 128), f32)],
        compiler_params=_params(("parallel", "parallel", "arbitrary")), name=name)(Q, K, kv)


def _flash_bwd(Q, K, kv, O, L, dO, *, name):
    S = Q.shape[0]
    tq = min(S, 512)
    tk = min(S, 512)
    nq = S // tq

    def body(q_ref, k_ref, v_ref, o_ref, l_ref, do_ref, dq_ref, dk_ref, dv_ref, dk_acc, dv_acc):
        kk = pl.program_id(1)
        i = pl.program_id(2)

        @pl.when((kk == 0) & (i == 0))
        def _():
            dq_ref[...] = jnp.zeros_like(dq_ref)

        @pl.when(i == 0)
        def _():
            dk_acc[...] = jnp.zeros_like(dk_acc)
            dv_acc[...] = jnp.zeros_like(dv_acc)

        q = q_ref[...]
        k = k_ref[...]
        do = do_ref[...]
        s = lax.dot_general(q, k, (NT, ((), ())), preferred_element_type=f32)
        p = jnp.exp(s - l_ref[...][:, :1])
        delta = jnp.sum(do * o_ref[...].astype(f32), axis=-1, keepdims=True)
        dob = do.astype(bf16)
        dv_acc[...] += lax.dot_general(p.astype(bf16), dob, (TN, ((), ())), preferred_element_type=f32)
        dp = lax.dot_general(dob, v_ref[...], (NT, ((), ())), preferred_element_type=f32)
        ds = (p * (dp - delta)).astype(bf16)
        dk_acc[...] += lax.dot_general(ds, q, (TN, ((), ())), preferred_element_type=f32)
        rows = pl.ds(pl.multiple_of(i * tq, tq), tq)
        dq_ref[rows, :] += lax.dot_general(ds, k, (NN, ((), ())), preferred_element_type=f32)

        @pl.when(i == nq - 1)
        def _():
            dk_ref[...] = dk_acc[...]
            dv_ref[...] = dv_acc[...]

    return pl.pallas_call(
        body, grid=(MLA_HEADS, S // tk, nq),
        in_specs=[pl.BlockSpec((tq, 256), lambda h, k, i: (i, h)),
                  pl.BlockSpec((tk, 256), lambda h, k, i: (k, h)),
                  pl.BlockSpec((tk, 128), lambda h, k, i: (k, 2 * h + 1)),
                  pl.BlockSpec((tq, 128), lambda h, k, i: (i, h)),
                  pl.BlockSpec((tq, 128), lambda h, k, i: (i, h)),
                  pl.BlockSpec((tq, 128), lambda h, k, i: (i, h))],
        out_specs=[pl.BlockSpec((S, 256), lambda h, k, i: (0, h)),
                   pl.BlockSpec((tk, 256), lambda h, k, i: (k, h)),
                   pl.BlockSpec((tk, 128), lambda h, k, i: (k, h))],
        out_shape=[SDS((S, 2048), f32), SDS((S, 2048), f32), SDS((S, 1024), f32)],
        scratch_shapes=[pltpu.VMEM((tk, 256), f32), pltpu.VMEM((tk, 128), f32)],
        compiler_params=_params(("parallel", "arbitrary", "arbitrary"), VMEM_BIG), name=name)(Q, K, kv, O, L, dO)


def _mla_bwd_prep(dQ, dK, dV, cosm, sinm, *, name):
    S = dQ.shape[0]
    T = min(S, 256)
    scale = (MLA_NOPE + MLA_ROPE) ** -0.5

    def body(dq_ref, dk_ref, dv_ref, cos_ref, sin_ref, oq_ref, okv_ref, okr_ref):
        cs, sn = cos_ref[...], sin_ref[...]
        kr = jnp.zeros((T, 128), f32)
        for h in range(MLA_HEADS):
            a = 256 * h
            oq_ref[:, a:a + 128] = (dq_ref[:, a:a + 128] * scale).astype(bf16)
            oq_ref[:, a + 128:a + 256] = (_rope_t(dq_ref[:, a + 128:a + 256], cs, sn, 32) * scale).astype(bf16)
            okv_ref[:, a:a + 128] = dk_ref[:, a:a + 128].astype(bf16)
            okv_ref[:, a + 128:a + 256] = dv_ref[:, 128 * h:128 * h + 128].astype(bf16)
            kr = kr + dk_ref[:, a + 128:a + 256]
        okr_ref[...] = _rope_t(kr, cs, sn, 32)

    return pl.pallas_call(
        body, grid=(S // T,),
        in_specs=[pl.BlockSpec((T, 2048), lambda i: (i, 0)), pl.BlockSpec((T, 2048), lambda i: (i, 0)),
                  pl.BlockSpec((T, 1024), lambda i: (i, 0)), pl.BlockSpec((T, 128), lambda i: (i, 0)),
                  pl.BlockSpec((T, 128), lambda i: (i, 0))],
        out_specs=[pl.BlockSpec((T, 2048), lambda i: (i, 0)), pl.BlockSpec((T, 2048), lambda i: (i, 0)),
                   pl.BlockSpec((T, 128), lambda i: (i, 0))],
        out_shape=[SDS((S, 2048), bf16), SDS((S, 2048), bf16), SDS((S, 128), f32)],
        compiler_params=_params(("parallel",), VMEM_BIG), name=name)(dQ, dK, dV, cosm, sinm)


def _mla_norm_bwd(zm, qg, kvg, dcqn, dckvn, dkr, *, name):
    S = zm.shape[0]
    T = min(S, 512)

    def body(cq_ref, ckv_ref, qg_ref, kvg_ref, dcq_ref, dckv_ref, dkr_ref, o_ref, dqg_ref, dkvg_ref):
        i = pl.program_id(0)
        _, vjp = jax.vjp(_rms, cq_ref[...], qg_ref[...])
        dcq, dqg = vjp(dcq_ref[...])
        _, vjp2 = jax.vjp(_rms, ckv_ref[...], kvg_ref[...])
        dckv, dkvg = vjp2(dckv_ref[...])
        o_ref[:, 0:384] = dcq.astype(bf16)
        o_ref[:, 384:512] = jnp.zeros((T, 128), bf16)
        o_ref[:, 512:768] = dckv.astype(bf16)
        o_ref[:, 768:896] = dkr_ref[...].astype(bf16)

        @pl.when(i == 0)
        def _():
            dqg_ref[...] = dqg
            dkvg_ref[...] = dkvg

        @pl.when(i > 0)
        def _():
            dqg_ref[...] += dqg
            dkvg_ref[...] += dkvg

    return pl.pallas_call(
        body, grid=(S // T,),
        in_specs=[pl.BlockSpec((T, 384), lambda i: (i, 0)), pl.BlockSpec((T, 256), lambda i: (i, 2)),
                  pl.BlockSpec((1, 384), lambda i: (0, 0)), pl.BlockSpec((1, 256), lambda i: (0, 0)),
                  pl.BlockSpec((T, 384), lambda i: (i, 0)), pl.BlockSpec((T, 256), lambda i: (i, 0)),
                  pl.BlockSpec((T, 128), lambda i: (i, 0))],
        out_specs=[pl.BlockSpec((T, 896), lambda i: (i, 0)), pl.BlockSpec((1, 384), lambda i: (0, 0)),
                   pl.BlockSpec((1, 256), lambda i: (0, 0))],
        out_shape=[SDS((S, 896), bf16), SDS((1, 384), f32), SDS((1, 256), f32)],
        compiler_params=_params(("arbitrary",)), name=name)(zm, zm, qg, kvg, dcqn, dckvn, dkr)


def _s5_disc(a_re, a_im, ldt, b_re, b_im):
    dt = jnp.exp(ldt)
    ar = jnp.minimum(a_re, -1e-4)
    mag = jnp.exp(dt * ar)
    abr = mag * jnp.cos(dt * a_im)
    abi = mag * jnp.sin(dt * a_im)
    den = ar * ar + a_im * a_im
    nr = abr - 1.0
    ni = abi
    cr = (nr * ar + ni * a_im) / den
    ci = (ni * ar - nr * a_im) / den
    return abr, abi, cr * b_re - ci * b_im, cr * b_im + ci * b_re


def _s5_param_fwd(a_re, a_im, ldt, b_re, b_im, *, name):
    R = SDS((1, 8192), f32)
    M = SDS((16, 8192), f32)
    Pw = SDS((64, 8192), f32)

    def body(a_re_r, a_im_r, ldt_r, b_re_r, b_im_r, o1, o2, o3, o4, p_re, p_im):
        abr, abi, bbr, bbi = _s5_disc(a_re_r[...], a_im_r[...], ldt_r[...], b_re_r[...], b_im_r[...])
        o1[...] = abr
        o2[...] = abi
        o3[...] = bbr
        o4[...] = bbi
        dt = jnp.exp(ldt_r[...])
        ar = jnp.minimum(a_re_r[...], -1e-4)
        n = lax.broadcasted_iota(jnp.int32, (64, 8192), 0).astype(f32) + 1.0
        mag = jnp.exp(n * (dt * ar))
        ang = n * (dt * a_im_r[...])
        p_re[...] = mag * jnp.cos(ang)
        p_im[...] = mag * jnp.sin(ang)

    return pl.pallas_call(body, out_shape=[R, R, M, M, Pw, Pw], name=name)(a_re, a_im, ldt, b_re, b_im)


def _s5_param_bwd(a_re, a_im, ldt, b_re, b_im, d_abr, d_abi, d_bbr, d_bbi, *, name):
    R = SDS((1, 8192), f32)
    M = SDS((16, 8192), f32)

    def body(a_re_r, a_im_r, ldt_r, b_re_r, b_im_r, c1, c2, c3, c4, o1, o2, o3, o4, o5):
        _, vjp = jax.vjp(_s5_disc, a_re_r[...], a_im_r[...], ldt_r[...], b_re_r[...], b_im_r[...])
        g = vjp((c1[...], c2[...], c3[...], c4[...]))
        for o, v in zip((o1, o2, o3, o4, o5), g):
            o[...] = v

    return pl.pallas_call(body, out_shape=[R, R, R, M, M], name=name)(a_re, a_im, ldt, b_re, b_im, d_abr, d_abi, d_bbr, d_bbi)


def _scan_core(sc, ar, ai, pwr_ref, pwi_ref, a64r, a64i, carry, *, reverse, T):
    L = T // S5_SEG
    arb = [jnp.broadcast_to(ar[:, 128 * k:128 * k + 128], (8, 128)) for k in range(4)]
    aib = [jnp.broadcast_to(ai[:, 128 * k:128 * k + 128], (8, 128)) for k in range(4)]

    def step(r0, c):
        r = (L - 1 - r0) if reverse else r0
        new_r, new_i = [], []
        for k in range(4):
            idx = pl.ds(r, S5_SEG, stride=L)
            br = sc[k, idx, :]
            bi = sc[4 + k, idx, :]
            cr, ci = c[k], c[4 + k]
            nr = arb[k] * cr - aib[k] * ci + br
            ni = arb[k] * ci + aib[k] * cr + bi
            sc[k, idx, :] = nr
            sc[4 + k, idx, :] = ni
            new_r.append(nr)
            new_i.append(ni)
        return tuple(new_r + new_i)

    c = lax.fori_loop(0, L, step, tuple(jnp.zeros((8, 128), f32) for _ in range(8)))
    segs = range(S5_SEG - 1, -1, -1) if reverse else range(S5_SEG)
    for k in range(4):
        lr, li = c[k], c[4 + k]
        cr = carry[k, 0:1, :]
        ci = carry[4 + k, 0:1, :]
        a6r = a64r[:, 128 * k:128 * k + 128]
        a6i = a64i[:, 128 * k:128 * k + 128]
        pr = pwr_ref[:, 128 * k:128 * k + 128]
        pi = pwi_ref[:, 128 * k:128 * k + 128]
        for seg in segs:
            sl = pl.ds(seg * L, L)
            sc[k, sl, :] += pr * cr - pi * ci
            sc[4 + k, sl, :] += pr * ci + pi * cr
            ncr = lr[seg:seg + 1, :] + a6r * cr - a6i * ci
            nci = li[seg:seg + 1, :] + a6r * ci + a6i * cr
            cr, ci = ncr, nci
        carry[k, 0:1, :] = cr
        carry[4 + k, 0:1, :] = ci


def _s5_scan_fwd(u, BBr, BBi, CCr, CCi, abr, abi, pwr, pwi, *, reverse, name):
    S = u.shape[0]
    T = min(S, 512)
    NB = S // T
    L = T // S5_SEG
    d = 1 if reverse else 0

    def tb(b):
        return (NB - 1 - b) if reverse else b

    def body(u_ref, bbr_ref, bbi_ref, ccr_ref, cci_ref, ar_ref, ai_ref, pwr_ref, pwi_ref, y_ref, xr_ref, xi_ref, sc, carry):
        b = pl.program_id(1)

        @pl.when(b == 0)
        def _():
            carry[...] = jnp.zeros_like(carry)

        ub = u_ref[...].astype(bf16)
        bur = lax.dot_general(ub, bbr_ref[0, 0], (NN, ((), ())), preferred_element_type=f32)
        bui = lax.dot_general(ub, bbi_ref[0, 0], (NN, ((), ())), preferred_element_type=f32)
        for k in range(4):
            sc[k] = bur[:, 128 * k:128 * k + 128]
            sc[4 + k] = bui[:, 128 * k:128 * k + 128]
        a6 = (0 if reverse else L - 1)
        _scan_core(sc, ar_ref[...], ai_ref[...], pwr_ref, pwi_ref, pwr_ref[a6:a6 + 1, :], pwi_ref[a6:a6 + 1, :], carry,
                   reverse=reverse, T=T)
        y = jnp.zeros((T, 128), f32)
        for k in range(4):
            xr = sc[k]
            xi = sc[4 + k]
            xr_ref[:, 128 * k:128 * k + 128] = xr
            xi_ref[:, 128 * k:128 * k + 128] = xi
            y = y + _dot(xr, ccr_ref[0, 0, 128 * k:128 * k + 128, :], NN) - _dot(xi, cci_ref[0, 0, 128 * k:128 * k + 128, :], NN)
        y_ref[...] = y

    mat = lambda shp: pl.BlockSpec((1, 1) + shp, lambda j, b: (d, j, 0, 0))
    vec = lambda r: pl.BlockSpec((r, 512), lambda j, b: (0, d * S5_NJ + j))
    return pl.pallas_call(
        body, grid=(S5_NJ, NB),
        in_specs=[pl.BlockSpec((T, 128), lambda j, b: (tb(b), j)), mat((128, 512)), mat((128, 512)), mat((512, 128)),
                  mat((512, 128)), vec(1), vec(1), vec(L), vec(L)],
        out_specs=[pl.BlockSpec((T, 128), lambda j, b: (tb(b), j)), pl.BlockSpec((T, 512), lambda j, b: (tb(b), j)),
                   pl.BlockSpec((T, 512), lambda j, b: (tb(b), j))],
        out_shape=[SDS((S, 1024), f32), SDS((S, 4096), f32), SDS((S, 4096), f32)],
        scratch_shapes=[pltpu.VMEM((8, T, 128), f32), pltpu.VMEM((8, 8, 128), f32)],
        compiler_params=_params(("parallel", "arbitrary")), name=name)(u, BBr, BBi, CCr, CCi, abr, abi, pwr, pwi)


def _s5_scan_bwd(u, dy, xr, xi, BBr, BBi, CCr, CCi, abr, abi, pwr, pwi, *, reverse, name):
    S = u.shape[0]
    T = min(S, 512)
    NB = S // T
    L = T // S5_SEG
    d = 1 if reverse else 0
    adj_rev = not reverse

    def tb(b):
        return b if reverse else (NB - 1 - b)

    def bnd(b):
        t = tb(b)
        if reverse:
            return jnp.minimum((t + 1) * (T // 8), S // 8 - 1)
        return jnp.maximum(t * (T // 8) - 1, 0)

    def body(u_ref, dy_ref, xr_ref, xi_ref, xbr_ref, xbi_ref, bbr_ref, bbi_ref, ccr_ref, cci_ref, ar_ref, ai_ref,
             pwr_ref, pwi_ref, du_ref, dbbr_ref, dbbi_ref, dccr_ref, dcci_ref, dar_ref, dai_ref, sc, carry):
        b = pl.program_id(1)

        @pl.when(b == 0)
        def _():
            carry[...] = jnp.zeros_like(carry)
            for r in (dbbr_ref, dbbi_ref, dccr_ref, dcci_ref, dar_ref, dai_ref):
                r[...] = jnp.zeros_like(r)

        dyb = dy_ref[...]
        gxr = lax.dot_general(dyb, ccr_ref[0, 0], (NT, ((), ())), preferred_element_type=f32)
        gxi = -lax.dot_general(dyb, cci_ref[0, 0], (NT, ((), ())), preferred_element_type=f32)
        for k in range(4):
            sc[k] = gxr[:, 128 * k:128 * k + 128]
            sc[4 + k] = gxi[:, 128 * k:128 * k + 128]
        a6 = (0 if adj_rev else L - 1)
        nai = -ai_ref[...]
        _scan_core(sc, ar_ref[...], nai, pwr_ref, _Neg(pwi_ref), pwr_ref[a6:a6 + 1, :], -pwi_ref[a6:a6 + 1, :], carry,
                   reverse=adj_rev, T=T)
        ub = u_ref[...].astype(bf16)
        first = (b == NB - 1)
        row = lax.broadcasted_iota(jnp.int32, (T, 128), 0)
        du = jnp.zeros((T, 128), f32)
        for k in range(4):
            cols = slice(128 * k, 128 * k + 128)
            lr = sc[k]
            li = sc[4 + k]
            lrb = lr.astype(bf16)
            lib = li.astype(bf16)
            du = du + lax.dot_general(lrb, bbr_ref[0, 0, :, cols], (NT, ((), ())), preferred_element_type=f32) \
                + lax.dot_general(lib, bbi_ref[0, 0, :, cols], (NT, ((), ())), preferred_element_type=f32)
            dbbr_ref[0, 0, :, cols] += lax.dot_general(ub, lrb, (TN, ((), ())), preferred_element_type=f32)
            dbbi_ref[0, 0, :, cols] += lax.dot_general(ub, lib, (TN, ((), ())), preferred_element_type=f32)
            xrk = xr_ref[:, cols]
            xik = xi_ref[:, cols]
            dccr_ref[0, 0, cols, :] += lax.dot_general(xrk.astype(bf16), dyb, (TN, ((), ())), preferred_element_type=f32)
            dcci_ref[0, 0, cols, :] -= lax.dot_general(xik.astype(bf16), dyb, (TN, ((), ())), preferred_element_type=f32)
            if reverse:
                edge_r = jnp.where(first, 0.0, xbr_ref[0:1, cols])
                edge_i = jnp.where(first, 0.0, xbi_ref[0:1, cols])
                xpr = jnp.where(row == T - 1, edge_r, pltpu.roll(xrk, T - 1, axis=0))
                xpi = jnp.where(row == T - 1, edge_i, pltpu.roll(xik, T - 1, axis=0))
            else:
                edge_r = jnp.where(first, 0.0, xbr_ref[7:8, cols])
                edge_i = jnp.where(first, 0.0, xbi_ref[7:8, cols])
                xpr = jnp.where(row == 0, edge_r, pltpu.roll(xrk, 1, axis=0))
                xpi = jnp.where(row == 0, edge_i, pltpu.roll(xik, 1, axis=0))
            dar_ref[:, cols] += jnp.sum(xpr * lr + xpi * li, axis=0, keepdims=True)
            dai_ref[:, cols] += jnp.sum(xpr * li - xpi * lr, axis=0, keepdims=True)
        du_ref[...] = du

    mat = lambda shp: pl.BlockSpec((1, 1) + shp, lambda j, b: (d, j, 0, 0))
    omat = lambda shp: pl.BlockSpec((1, 1) + shp, lambda j, b: (0, j, 0, 0))
    vec = lambda r: pl.BlockSpec((r, 512), lambda j, b: (0, d * S5_NJ + j))
    blk = lambda w: pl.BlockSpec((T, w), lambda j, b: (tb(b), j))
    return pl.pallas_call(
        body, grid=(S5_NJ, NB),
        in_specs=[blk(128), blk(128), blk(512), blk(512),
                  pl.BlockSpec((8, 512), lambda j, b: (bnd(b), j)), pl.BlockSpec((8, 512), lambda j, b: (bnd(b), j)),
                  mat((128, 512)), mat((128, 512)), mat((512, 128)), mat((512, 128)), vec(1), vec(1), vec(L), vec(L)],
        out_specs=[blk(128), omat((128, 512)), omat((128, 512)), omat((512, 128)), omat((512, 128)),
                   pl.BlockSpec((1, 512), lambda j, b: (0, j)), pl.BlockSpec((1, 512), lambda j, b: (0, j))],
        out_shape=[SDS((S, 1024), f32), SDS((1, 8, 128, 512), f32), SDS((1, 8, 128, 512), f32), SDS((1, 8, 512, 128), f32),
                   SDS((1, 8, 512, 128), f32), SDS((1, 4096), f32), SDS((1, 4096), f32)],
        scratch_shapes=[pltpu.VMEM((8, T, 128), f32), pltpu.VMEM((8, 8, 128), f32)],
        compiler_params=_params(("parallel", "arbitrary"), VMEM_BIG), name=name)(
            u, dy, xr, xi, xr, xi, BBr, BBi, CCr, CCi, abr, abi, pwr, pwi)


class _Neg:
    def __init__(self, ref):
        self.ref = ref

    def __getitem__(self, idx):
        return -self.ref[idx]


def _silu_mul(g, u):
    return jax.nn.silu(g) * u


def _mixf(p0, p1, p2, z0, z1, z2):
    return jax.nn.sigmoid(z0) * p0 + jax.nn.sigmoid(z1) * p1 + jax.nn.sigmoid(z2) * p2


def _s5_act(yf, yb, u, dd):
    return jax.nn.gelu(yf + yb + dd * u)


def _glu(a, b):
    return a * jax.nn.sigmoid(b)


def _layer_fwd(x, w, tabs, l):
    S = x.shape[0]
    T = min(S, 512)
    I = S // T
    nm = lambda s: f"L{l}_{s}"
    sv = {'x': x}
    h = _rmsnorm_fwd(x, w['norm1_g'], name=nm("norm1"))
    zr = _mm(h, w['W_ret'], name=nm("in_ret"))
    zm = _mm(h, w['W_mla'], name=nm("in_mla"))
    zs = _mm(h, w['W_s5'], name=nm("in_s5"))
    zg = _mm(h, w['W_gate'], name=nm("in_gate"))
    sv.update(h=h, zr=zr, zm=zm, zs=zs, zg=zg)

    yf, stf = _ret_dir_fwd(zr, w['lg'], tabs['cos_r'], tabs['sin_r'], reverse=False, name=nm("ret_f"))
    yb, stb = _ret_dir_fwd(zr, w['lg'], tabs['cos_r'], tabs['sin_r'], reverse=True, name=nm("ret_b"))
    hd = lambda j: j
    y_ret = _pw(_gn_gate, [yf, yb, zr, w['ret_gn_g']],
                [_row(T, 256, hd), _row(T, 256, hd), _row(T, 256, lambda j: 8 + j), _par(256, hd)],
                [SDS((S, 1024), bf16)], [_row(T, 256, hd)], (RET_HEADS, I), name=nm("ret_gn"))[0]
    sv.update(yf=yf, yb=yb, stf=stf, stb=stb, y_ret=y_ret)

    cqn, ckvn = _pw(lambda a, b, g1, g2: (_rms(a, g1), _rms(b, g2)), [zm, zm, w['mla_q_norm_g'], w['mla_kv_norm_g']],
                    [_row(T, 384), _row(T, 256, lambda j: 2), _par(384), _par(256)],
                    [SDS((S, 384), bf16), SDS((S, 256), bf16)], [_row(T, 384), _row(T, 256)], (1, I), name=nm("mla_norm"))
    q = _mm(cqn, w['W_uq'], name=nm("mla_uq"))
    kv = _mm(ckvn, w['W_ukv'], out_dtype=bf16, name=nm("mla_ukv"))
    sc = (MLA_NOPE + MLA_ROPE) ** -0.5
    Q = _pw(lambda xq, cs, sn: jnp.concatenate([xq[:, :128] * sc, _rope(xq[:, 128:], cs, sn, 32) * sc], axis=1),
            [q, tabs['cos_m'], tabs['sin_m']], [_row(T, 256, hd), _row(T, 128), _row(T, 128)],
            [SDS((S, 2048), bf16)], [_row(T, 256, hd)], (MLA_HEADS, I), name=nm("mla_qprep"))[0]
    K = _pw(lambda kn, kr, cs, sn: jnp.concatenate([kn.astype(f32), _rope(kr, cs, sn, 32)], axis=1),
            [kv, zm, tabs['cos_m'], tabs['sin_m']],
            [_row(T, 128, lambda j: 2 * j), _row(T, 128, lambda j: 6), _row(T, 128), _row(T, 128)],
            [SDS((S, 2048), bf16)], [_row(T, 256, hd)], (MLA_HEADS, I), name=nm("mla_kprep"))[0]
    O, Lse = _flash_fwd(Q, K, kv, name=nm("mla_attn"))
    sv.update(cqn=cqn, ckvn=ckvn, kv=kv, Q=Q, K=K, O=O, Lse=Lse)

    s5 = w['s5']
    ysf, xrf, xif = _s5_scan_fwd(zs, s5['BBr'], s5['BBi'], s5['CCr'], s5['CCi'], s5['abr'], s5['abi'], s5['pwr_f'], s5['pwi_f'],
                                 reverse=False, name=nm("s5_f"))
    ysb, xrb, xib = _s5_scan_fwd(zs, s5['BBr'], s5['BBi'], s5['CCr'], s5['CCi'], s5['abr'], s5['abi'], s5['pwr_f'], s5['pwi_f'],
                                 reverse=True, name=nm("s5_b"))
    gact = _pw(_s5_act, [ysf, ysb, zs, w['s5_d']], [_row(T, D), _row(T, D), _row(T, D), _par(D)],
               [SDS((S, D), bf16)], [_row(T, D)], (1, I), name=nm("s5_act"))[0]
    gg = _mm(gact, w['W_glu'], name=nm("s5_glu_mm"))
    y_s5 = _pw(_glu, [gg, gg], [_row(T, D), _row(T, D, lambda j: 1)], [SDS((S, D), bf16)], [_row(T, D)], (1, I),
               name=nm("s5_glu"))[0]
    sv.update(ysf=ysf, ysb=ysb, xrf=xrf, xif=xif, xrb=xrb, xib=xib, gact=gact, gg=gg, y_s5=y_s5)

    ys = [y_ret, O, y_s5]
    pr = [_mm(ys[i], w['W_br'][i], name=nm(f"branch{i}")) for i in range(3)]
    mix = _pw(_mixf, pr + [zg, zg, zg],
              [_row(T, D)] * 3 + [_row(T, D), _row(T, D, lambda j: 1), _row(T, D, lambda j: 2)],
              [SDS((S, D), bf16)], [_row(T, D)], (1, I), name=nm("mix"))[0]
    x1 = _mm(mix, w['W_out'], res=x, name=nm("out_proj"))
    h2 = _rmsnorm_fwd(x1, w['norm2_g'], name=nm("norm2"))
    fgu = _mm(h2, w['W_gu'], name=nm("ffn_gu"))
    act = _pw(_silu_mul, [fgu, fgu], [_row(T, 1408, lambda j: j), _row(T, 1408, lambda j: 2 + j)],
              [SDS((S, FFN_H), bf16)], [_row(T, 1408, lambda j: j)], (2, I), name=nm("ffn_act"))[0]
    x2 = _mm(act, w['W_down'], res=x1, name=nm("ffn_down"))
    sv.update(pr=pr, mix=mix, x1=x1, h2=h2, fgu=fgu, act=act)
    return x2, sv


def _vjp_fn(fn, n_primal, cast=None):
    def g(*args):
        _, vjp = jax.vjp(fn, *args[:n_primal])
        return vjp(args[n_primal].astype(f32))
    return g


def _layer_bwd(dx2, w, tabs, sv, l):
    S = dx2.shape[0]
    T = min(S, 512)
    I = S // T
    nm = lambda s: f"L{l}_b_{s}"
    g = {}
    hd = lambda j: j

    dact = _mm(dx2, w['W_down'], tb=True, name=nm("ffn_down_dx"))
    g['W_down'] = _mmT(sv['act'], dx2, name=nm("ffn_down_dw"))
    dfg, dfu = _pw(_vjp_fn(_silu_mul, 2), [sv['fgu'], sv['fgu'], dact],
                   [_row(T, 1408, lambda j: j), _row(T, 1408, lambda j: 2 + j), _row(T, 1408, lambda j: j)],
                   [SDS((S, FFN_H), bf16), SDS((S, FFN_H), bf16)], [_row(T, 1408, lambda j: j)] * 2, (2, I), name=nm("ffn_act"))
    dfgu = jnp.concatenate([dfg, dfu], axis=1)
    g['W_gu'] = _mmT(sv['h2'], dfgu, name=nm("ffn_gu_dw"))
    dh2 = _mm(dfgu, w['W_gu'], tb=True, name=nm("ffn_gu_dx"))
    dx1, g['norm2_g'] = _rmsnorm_bwd(sv['x1'], w['norm2_g'], dh2, dx2, name=nm("norm2"))

    dmix = _mm(dx1, w['W_out'], tb=True, name=nm("out_dx"))
    g['W_out'] = _mmT(sv['mix'], dx1, name=nm("out_dw"))
    zg = sv['zg']
    outs = _pw(_vjp_fn(_mixf, 6), sv['pr'] + [zg, zg, zg, dmix],
               [_row(T, D)] * 3 + [_row(T, D), _row(T, D, lambda j: 1), _row(T, D, lambda j: 2), _row(T, D)],
               [SDS((S, D), bf16)] * 6, [_row(T, D)] * 6, (1, I), name=nm("mix"))
    dpr, dzg = outs[:3], jnp.concatenate(outs[3:], axis=1)
    ys = [sv['y_ret'], sv['O'], sv['y_s5']]
    g['W_br'] = [_mmT(ys[i], dpr[i], name=nm(f"branch{i}_dw")) for i in range(3)]
    dys = [_mm(dpr[i], w['W_br'][i], tb=True, name=nm(f"branch{i}_dx")) for i in range(3)]

    gg = sv['gg']
    dga, dgb = _pw(_vjp_fn(_glu, 2), [gg, gg, dys[2]], [_row(T, D), _row(T, D, lambda j: 1), _row(T, D)],
                   [SDS((S, D), bf16)] * 2, [_row(T, D)] * 2, (1, I), name=nm("s5_glu"))
    dgg = jnp.concatenate([dga, dgb], axis=1)
    g['W_glu'] = _mmT(sv['gact'], dgg, name=nm("s5_glu_dw"))
    dgact = _mm(dgg, w['W_glu'], tb=True, name=nm("s5_glu_dx"))

    def act_bwd(yf, yb, u, dd, ct):
        _, vjp = jax.vjp(_s5_act, yf, yb, u, dd)
        dyf, _, du, ddd = vjp(ct)
        return dyf, du, ddd

    dys5, du_direct, g['s5_d'] = _pw(act_bwd, [sv['ysf'], sv['ysb'], sv['zs'], w['s5_d'], dgact],
                                     [_row(T, D)] * 3 + [_par(D), _row(T, D)],
                                     [SDS((S, D), bf16), SDS((S, D), f32), SDS((1, D), f32)],
                                     [_row(T, D), _row(T, D), _par(D)], (1, I), n_acc=1, name=nm("s5_act"))
    s5 = w['s5']
    rf = _s5_scan_bwd(sv['zs'], dys5, sv['xrf'], sv['xif'], s5['BBr'], s5['BBi'], s5['CCr'], s5['CCi'], s5['abr'], s5['abi'],
                      s5['pwr_a'], s5['pwi_a'], reverse=False, name=nm("s5_f"))
    rb = _s5_scan_bwd(sv['zs'], dys5, sv['xrb'], sv['xib'], s5['BBr'], s5['BBi'], s5['CCr'], s5['CCi'], s5['abr'], s5['abi'],
                      s5['pwr_a'], s5['pwi_a'], reverse=True, name=nm("s5_b"))
    g['s5'] = (rf[1:], rb[1:])
    dzs = _pw(lambda a, b, c: a + b + c, [du_direct, rf[0], rb[0]], [_row(T, D)] * 3, [SDS((S, D), bf16)], [_row(T, D)],
              (1, I), name=nm("s5_du"))[0]

    dQ, dK, dV = _flash_bwd(sv['Q'], sv['K'], sv['kv'], sv['O'], sv['Lse'], dys[1], name=nm("mla_attn"))
    dq_lin, dkv, dkr = _mla_bwd_prep(dQ, dK, dV, tabs['cos_m'], tabs['sin_m'], name=nm("mla_prep"))
    g['W_uq'] = _mmT(sv['cqn'], dq_lin, name=nm("mla_uq_dw"))
    dcqn = _mm(dq_lin, w['W_uq'], tb=True, name=nm("mla_uq_dx"))
    g['W_ukv'] = _mmT(sv['ckvn'], dkv, name=nm("mla_ukv_dw"))
    dckvn = _mm(dkv, w['W_ukv'], tb=True, name=nm("mla_ukv_dx"))
    dzm, g['mla_q_norm_g'], g['mla_kv_norm_g'] = _mla_norm_bwd(sv['zm'], w['mla_q_norm_g'], w['mla_kv_norm_g'], dcqn, dckvn, dkr,
                                                               name=nm("mla_norm"))

    zr = sv['zr']

    def gn_bwd(yf, yb, gt, gn, ct):
        _, vjp = jax.vjp(_gn_gate, yf, yb, gt, gn)
        dyf, _, dgt, dgn = vjp(ct)
        return dyf, dgt, dgn

    dyr, dgate, g['ret_gn_g'] = _pw(gn_bwd, [sv['yf'], sv['yb'], zr, w['ret_gn_g'], dys[0]],
                                    [_row(T, 256, hd), _row(T, 256, hd), _row(T, 256, lambda j: 8 + j), _par(256, hd),
                                     _row(T, 256, hd)],
                                    [SDS((S, 1024), bf16), SDS((S, 1024), bf16), SDS((1, 1024), f32)],
                                    [_row(T, 256, hd), _row(T, 256, hd), _par(256, hd)], (RET_HEADS, I), n_acc=1, name=nm("ret_gn"))
    qf, kf, vf, lgf = _ret_dir_bwd(zr, w['lg'], tabs['cos_r'], tabs['sin_r'], dyr, sv['stf'], reverse=False, name=nm("ret_f"))
    qb, kb, vb, lgb = _ret_dir_bwd(zr, w['lg'], tabs['cos_r'], tabs['sin_r'], dyr, sv['stb'], reverse=True, name=nm("ret_b"))
    g['lg'] = jnp.stack([lgf[:, 0, 0], lgb[:, 0, 0]])
    add2 = lambda a, b: a + b
    dq = _pw(add2, [qf, qb], [_row(T, 512)] * 2, [SDS((S, 512), bf16)], [_row(T, 512)], (1, I), name=nm("ret_dq"))[0]
    dk = _pw(add2, [kf, kb], [_row(T, 512)] * 2, [SDS((S, 512), bf16)], [_row(T, 512)], (1, I), name=nm("ret_dk"))[0]
    dv = _pw(add2, [vf, vb], [_row(T, D)] * 2, [SDS((S, D), bf16)], [_row(T, D)], (1, I), name=nm("ret_dv"))[0]
    dzr = jnp.concatenate([dq, dk, dv, dgate], axis=1)

    h = sv['h']
    g['W_ret'] = _mmT(h, dzr, name=nm("in_ret_dw"))
    g['W_mla'] = _mmT(h, dzm, name=nm("in_mla_dw"))
    g['W_s5'] = _mmT(h, dzs, name=nm("in_s5_dw"))
    g['W_gate'] = _mmT(h, dzg, name=nm("in_gate_dw"))
    dh = _mm(dzr, w['W_ret'], tb=True, name=nm("in_ret_dx"))
    dh = _mm(dzm, w['W_mla'], tb=True, res=dh, name=nm("in_mla_dx"))
    dh = _mm(dzs, w['W_s5'], tb=True, res=dh, name=nm("in_s5_dx"))
    dh = _mm(dzg, w['W_gate'], tb=True, res=dh, name=nm("in_gate_dx"))
    dx, g['norm1_g'] = _rmsnorm_bwd(sv['x'], w['norm1_g'], dh, dx1, name=nm("norm1"))
    return dx, g


def _loss_head(x, tgt, gain, *, name):
    S, W = x.shape
    T = min(S, 512)

    def loss_fn(xv, gv, tv):
        return 0.5 * jnp.sum(jnp.mean(jnp.square(_rms(xv, gv) - tv), axis=-1, keepdims=True), axis=0, keepdims=True)

    def fn(xv, gv, tv):
        lv, vjp = jax.vjp(lambda a, b: loss_fn(a, b, tv), xv, gv)
        dx, dg = vjp(jnp.ones((1, 1), f32))
        return dx, jnp.broadcast_to(lv, (1, 128)), dg

    return _pw(fn, [x, gain, tgt], [_row(T, W), _par(W), _row(T, W)],
               [SDS((S, W), f32), SDS((1, 128), f32), SDS((1, W), f32)], [_row(T, W), _par(128), _par(W)],
               (1, S // T), n_acc=2, name=name)


def _rope_tabs(S):
    def tab(dim):
        inv = 1.0 / (ROPE_THETA ** (jnp.arange(0, dim, 2, dtype=f32) / dim))
        ang = jnp.arange(S, dtype=f32)[:, None] * inv[None, :]
        return jnp.cos(ang), jnp.sin(ang)

    cr, sr = tab(RET_DK)
    cm, sm = tab(MLA_ROPE)
    z = jnp.zeros((S, 64), f32)
    return {'cos_r': jnp.concatenate([cr, cr], axis=1), 'sin_r': jnp.concatenate([-sr, sr], axis=1),
            'cos_m': jnp.concatenate([cm, cm, z], axis=1), 'sin_m': jnp.concatenate([-sm, sm, z], axis=1)}


def _bd_B(bb):
    b5 = bb.reshape(16, 2, 8, 8, 64)
    return jnp.einsum('cdjgp,gh->djgchp', b5, jnp.eye(8, dtype=bb.dtype)).reshape(2, 8, 128, 512)


def _bd_B_t(dBB):
    return jnp.einsum('djgcgp->cdjgp', dBB.reshape(2, 8, 8, 16, 8, 64)).reshape(16, 8192)


def _bd_C(c):
    c5 = c.reshape(2, 8, 8, 16, 64)
    return jnp.einsum('djgcp,gh->djgphc', c5, jnp.eye(8, dtype=c.dtype)).reshape(2, 8, 512, 128)


def _bd_C_t(dCC):
    return jnp.einsum('djgpgc->djgcp', dCC.reshape(2, 8, 8, 64, 8, 16)).reshape(2, 64, 16, 64)


def _s5_rows(p, l):
    a_re = p['s5_a_re'][l].reshape(1, 8192)
    a_im = p['s5_a_im'][l].reshape(1, 8192)
    ldt = jnp.broadcast_to(p['s5_log_dt'][l][:, :, None], (2, S5_G, S5_P)).reshape(1, 8192)
    b_re = p['s5_b_re'][l].transpose(3, 0, 1, 2).reshape(16, 8192)
    b_im = p['s5_b_im'][l].transpose(3, 0, 1, 2).reshape(16, 8192)
    return a_re, a_im, ldt, b_re, b_im


def _layer_weights(big, p, l):
    w_in = big['w_in'][l]
    z = lambda n: jnp.zeros((D, n), w_in.dtype)
    w = {
        'W_ret': w_in[:, 0:3072],
        'W_mla': jnp.concatenate([w_in[:, 3072:3456], z(128), w_in[:, 3456:3712], w_in[:, 3712:3776], z(64)], axis=1),
        'W_s5': w_in[:, 3776:4800],
        'W_gate': w_in[:, 4800:7872],
        'W_uq': jnp.pad(big['mla_w_uq'][l].reshape(MLA_Q_LORA, MLA_HEADS, 192), ((0, 0), (0, 0), (0, 64))).reshape(MLA_Q_LORA, 2048),
        'W_ukv': big['mla_w_ukv'][l],
        'W_glu': big['s5_w_glu'][l],
        'W_br': [big['w_branch'][l, i] for i in range(3)],
        'W_out': big['w_out'][l],
        'W_gu': big['ffn_w_gu'][l],
        'W_down': big['ffn_w_down'][l],
    }
    for n in ('norm1_g', 'ret_gn_g', 'mla_q_norm_g', 'mla_kv_norm_g', 's5_d', 'norm2_g'):
        w[n] = p[n][l][None, :]
    w['lg'] = jax.nn.log_sigmoid(p['ret_decay'][l])
    rows = _s5_rows(p, l)
    abr, abi, bbr, bbi, pwr, pwi = _s5_param_fwd(*rows, name=f"L{l}_s5_param")
    flip = lambda t, first: jnp.concatenate([t[::-1, :4096], t[:, 4096:]] if first else [t[:, :4096], t[::-1, 4096:]], axis=1)
    w['s5'] = {'abr': abr, 'abi': abi, 'BBr': _bd_B(bbr).astype(bf16), 'BBi': _bd_B(bbi).astype(bf16),
               'CCr': _bd_C(p['s5_c_re'][l]).astype(bf16), 'CCi': _bd_C(p['s5_c_im'][l]).astype(bf16),
               'pwr_f': flip(pwr, False), 'pwi_f': flip(pwi, False), 'pwr_a': flip(pwr, True), 'pwi_a': flip(pwi, True),
               'rows': rows}
    return w


def _layer_grads(g, w, p, l):
    out = {}
    m = g['W_mla']
    out['w_in'] = jnp.concatenate([g['W_ret'], m[:, 0:384], m[:, 512:768], m[:, 768:832], g['W_s5'], g['W_gate']], axis=1)
    out['mla_w_uq'] = g['W_uq'].reshape(MLA_Q_LORA, MLA_HEADS, 256)[:, :, :192].reshape(MLA_Q_LORA, 1536)
    out['mla_w_ukv'] = g['W_ukv']
    out['s5_w_glu'] = g['W_glu']
    out['w_branch'] = jnp.stack(g['W_br'])
    out['w_out'] = g['W_out']
    out['ffn_w_gu'] = g['W_gu']
    out['ffn_w_down'] = g['W_down']
    for n in ('norm1_g', 'ret_gn_g', 'mla_q_norm_g', 'mla_kv_norm_g', 's5_d', 'norm2_g'):
        out[n] = g[n][0]
    out['ret_decay'] = g['lg'] * jax.nn.sigmoid(-p['ret_decay'][l])
    (fB_r, fB_i, fC_r, fC_i, fa_r, fa_i), (bB_r, bB_i, bC_r, bC_i, ba_r, ba_i) = g['s5']
    cat = lambda a, b: jnp.concatenate([a, b], axis=0)
    d_bbr = _bd_B_t(cat(fB_r, bB_r))
    d_bbi = _bd_B_t(cat(fB_i, bB_i))
    out['s5_c_re'] = _bd_C_t(cat(fC_r, bC_r))
    out['s5_c_im'] = _bd_C_t(cat(fC_i, bC_i))
    d_abr = jnp.concatenate([fa_r, ba_r], axis=1)
    d_abi = jnp.concatenate([fa_i, ba_i], axis=1)
    da_re, da_im, dldt, db_re, db_im = _s5_param_bwd(*w['s5']['rows'], d_abr, d_abi, d_bbr, d_bbi, name=f"L{l}_b_s5_param")
    out['s5_a_re'] = da_re.reshape(2, S5_G, S5_P)
    out['s5_a_im'] = da_im.reshape(2, S5_G, S5_P)
    out['s5_log_dt'] = dldt.reshape(2, S5_G, S5_P).sum(axis=-1)
    out['s5_b_re'] = db_re.reshape(16, 2, S5_G, S5_P).transpose(1, 2, 3, 0)
    out['s5_b_im'] = db_im.reshape(16, 2, S5_G, S5_P).transpose(1, 2, 3, 0)
    return out


def _local_step(x, tgt, big, p):
    S = x.shape[0]
    assert S % 512 == 0
    tabs = _rope_tabs(S)
    ws, svs = [], []
    h = x
    for l in range(DEPTH):
        w = _layer_weights(big, p, l)
        h, sv = _layer_fwd(h, w, tabs, l)
        ws.append(w)
        svs.append(sv)
    dx, lossv, dfinal = _loss_head(h, tgt, p['final_g'][None, :], name="loss_head")
    per_layer = [None] * DEPTH
    for l in reversed(range(DEPTH)):
        dx, g = _layer_bwd(dx, ws[l], tabs, svs[l], l)
        per_layer[l] = _layer_grads(g, ws[l], p, l)
    grads = {n: jnp.stack([per_layer[l][n] for l in range(DEPTH)]) for n in per_layer[0]}
    grads['final_g'] = dfinal[0]
    return lossv[0, 0], dx, grads


_ANY = pl.BlockSpec(memory_space=pl.ANY)


def _place():
    x, y, c = lax.axis_index("x"), lax.axis_index("y"), lax.axis_index("c")
    return x, y, c, [(1 - x, y), (x, 1 - y), (1 - x, 1 - y)]


def _allgather4(arrs, *, name):
    n = len(arrs)

    def body(*refs):
        ins, outs = refs[:n], refs[n:2 * n]
        send, recv, loc = refs[2 * n:]
        x, y, c, chips = _place()
        me = 2 * x + y

        def remote(a, k, slot):
            px, py = chips[k]
            return pltpu.make_async_remote_copy(src_ref=ins[a], dst_ref=outs[a].at[slot], send_sem=send.at[a, k],
                                                recv_sem=recv.at[a, k], device_id=(px, py, c), device_id_type=MESH)

        mine = [pltpu.make_async_copy(ins[a], outs[a].at[me], loc.at[a]) for a in range(n)]
        for cp in mine:
            cp.start()
        sends = [remote(a, k, me) for a in range(n) for k in range(3)]
        for cp in sends:
            cp.start()
        for a in range(n):
            for k, (px, py) in enumerate(chips):
                remote(a, k, 2 * px + py).wait_recv()
        for cp in sends:
            cp.wait_send()
        for cp in mine:
            cp.wait()

    return pl.pallas_call(
        body, in_specs=[_ANY] * n, out_specs=[_ANY] * n, out_shape=[SDS((4,) + a.shape, a.dtype) for a in arrs],
        scratch_shapes=[pltpu.SemaphoreType.DMA((n, 3)), pltpu.SemaphoreType.DMA((n, 3)), pltpu.SemaphoreType.DMA((n,))],
        name=name)(*arrs)


def _rs_exchange(parts, *, name):
    n = len(parts)

    def body(*refs):
        ins, owns, gots = refs[:n], refs[n:2 * n], refs[2 * n:3 * n]
        send, recv, loc = refs[3 * n:]
        x, y, c, chips = _place()
        me = 2 * x + y

        def remote(a, k):
            px, py = chips[k]
            return pltpu.make_async_remote_copy(src_ref=ins[a].at[2 * px + py], dst_ref=gots[a].at[k], send_sem=send.at[a, k],
                                                recv_sem=recv.at[a, k], device_id=(px, py, c), device_id_type=MESH)

        mine = [pltpu.make_async_copy(ins[a].at[me], owns[a], loc.at[a]) for a in range(n)]
        for cp in mine:
            cp.start()
        sends = [remote(a, k) for a in range(n) for k in range(3)]
        for cp in sends:
            cp.start()
        for cp in sends:
            cp.wait_recv()
        for cp in sends:
            cp.wait_send()
        for cp in mine:
            cp.wait()

    return pl.pallas_call(
        body, in_specs=[_ANY] * n, out_specs=[_ANY] * (2 * n),
        out_shape=[SDS(a.shape[1:], a.dtype) for a in parts] + [SDS((3,) + a.shape[1:], a.dtype) for a in parts],
        scratch_shapes=[pltpu.SemaphoreType.DMA((n, 3)), pltpu.SemaphoreType.DMA((n, 3)), pltpu.SemaphoreType.DMA((n,))],
        name=name)(*parts)


def _sib_exchange(arrs, *, name):
    n = len(arrs)

    def body(*refs):
        ins, outs = refs[:n], refs[n:2 * n]
        send, recv = refs[2 * n:]
        x, y, c, _ = _place()
        cps = [pltpu.make_async_remote_copy(src_ref=ins[a], dst_ref=outs[a], send_sem=send.at[a], recv_sem=recv.at[a],
                                            device_id=(x, y, 1 - c), device_id_type=MESH) for a in range(n)]
        for cp in cps:
            cp.start()
        for cp in cps:
            cp.wait_recv()
        for cp in cps:
            cp.wait_send()

    return pl.pallas_call(
        body, in_specs=[_ANY] * n, out_specs=[_ANY] * n, out_shape=[SDS(a.shape, a.dtype) for a in arrs],
        scratch_shapes=[pltpu.SemaphoreType.DMA((n,)), pltpu.SemaphoreType.DMA((n,))], name=name)(*arrs)


def _sum4(own, got, *, name):
    R, W = own.shape
    tr = R if R <= 256 else 128
    g3 = lambda k: pl.BlockSpec((None, tr, W), lambda j, i: (k, i, 0))
    return _pw(lambda a, b, c, d: ((a + b) + c) + d, [own, got, got, got], [_row(tr, W), g3(0), g3(1), g3(2)],
               [SDS((R, W), f32)], [_row(tr, W)], (1, R // tr), name=name)[0]


def _adamw(po, ps, w, m, v, *, name):
    R, W = w.shape
    tr = R if R <= 256 else 128

    def fn(a, b, wv, mv, vv):
        g = a + b
        m2 = ADAM_B1 * mv + (1.0 - ADAM_B1) * g
        v2 = ADAM_B2 * vv + (1.0 - ADAM_B2) * jnp.square(g)
        m_hat = m2 / (1.0 - ADAM_B1 ** ADAM_STEP)
        v_hat = v2 / (1.0 - ADAM_B2 ** ADAM_STEP)
        return g, -ADAM_LR * (m_hat / (jnp.sqrt(v_hat) + ADAM_EPS) + ADAM_WD * wv), m2, v2

    return _pw(fn, [po, ps, w, m, v], [_row(tr, W)] * 5, [SDS((R, W), f32)] * 4, [_row(tr, W)] * 4, (1, R // tr), name=name)


def _to_parts(g, axis):
    shp = g.shape
    g = g.reshape(shp[:axis] + (4, shp[axis] // 4) + shp[axis + 1:])
    return jnp.moveaxis(g, axis, 0)


def _from_parts(pt, axis):
    g = jnp.moveaxis(pt, 0, axis)
    shp = g.shape
    return g.reshape(shp[:axis] + (4 * shp[axis + 1],) + shp[axis + 2:])


def kernel(x, norm1_g, w_in, ret_decay, ret_gn_g, mla_q_norm_g, mla_w_uq, mla_kv_norm_g, mla_w_ukv, s5_a_re, s5_a_im, s5_log_dt, s5_b_re, s5_b_im, s5_c_re, s5_c_im, s5_d, s5_w_glu, w_branch, w_out, norm2_g, ffn_w_gu, ffn_w_down, final_g, loss_target, m_norm1_g, m_w_in, m_ret_decay, m_ret_gn_g, m_mla_q_norm_g, m_mla_w_uq, m_mla_kv_norm_g, m_mla_w_ukv, m_s5_a_re, m_s5_a_im, m_s5_log_dt, m_s5_b_re, m_s5_b_im, m_s5_c_re, m_s5_c_im, m_s5_d, m_s5_w_glu, m_w_branch, m_w_out, m_norm2_g, m_ffn_w_gu, m_ffn_w_down, m_final_g, v_norm1_g, v_w_in, v_ret_decay, v_ret_gn_g, v_mla_q_norm_g, v_mla_w_uq, v_mla_kv_norm_g, v_mla_w_ukv, v_s5_a_re, v_s5_a_im, v_s5_log_dt, v_s5_b_re, v_s5_b_im, v_s5_c_re, v_s5_c_im, v_s5_d, v_s5_w_glu, v_w_branch, v_w_out, v_norm2_g, v_ffn_w_gu, v_ffn_w_down, v_final_g):
    wv = dict(zip(W_NAMES, (norm1_g, w_in, ret_decay, ret_gn_g, mla_q_norm_g, mla_w_uq, mla_kv_norm_g, mla_w_ukv, s5_a_re, s5_a_im,
                            s5_log_dt, s5_b_re, s5_b_im, s5_c_re, s5_c_im, s5_d, s5_w_glu, w_branch, w_out, norm2_g, ffn_w_gu,
                            ffn_w_down, final_g)))
    mv = dict(zip(W_NAMES, (m_norm1_g, m_w_in, m_ret_decay, m_ret_gn_g, m_mla_q_norm_g, m_mla_w_uq, m_mla_kv_norm_g, m_mla_w_ukv,
                            m_s5_a_re, m_s5_a_im, m_s5_log_dt, m_s5_b_re, m_s5_b_im, m_s5_c_re, m_s5_c_im, m_s5_d, m_s5_w_glu,
                            m_w_branch, m_w_out, m_norm2_g, m_ffn_w_gu, m_ffn_w_down, m_final_g)))
    vv = dict(zip(W_NAMES, (v_norm1_g, v_w_in, v_ret_decay, v_ret_gn_g, v_mla_q_norm_g, v_mla_w_uq, v_mla_kv_norm_g, v_mla_w_ukv,
                            v_s5_a_re, v_s5_a_im, v_s5_log_dt, v_s5_b_re, v_s5_b_im, v_s5_c_re, v_s5_c_im, v_s5_d, v_s5_w_glu,
                            v_w_branch, v_w_out, v_norm2_g, v_ffn_w_gu, v_ffn_w_down, v_final_g)))
    big_names = list(BIG)

    gathered = _allgather4([wv[n].astype(bf16) for n in big_names], name="gather_weights")
    big = {n: _from_parts(gt, BIG[n]) for n, gt in zip(big_names, gathered)}
    small = {n: wv[n] for n in SMALL}

    loss_local, dx, grads = _local_step(x[0], loss_target[0], big, small)
    loss = lax.psum(loss_local, ("x", "y", "c"))

    flat = lambda d: jnp.concatenate([d[n].reshape(-1) for n in SMALL])
    n_small = sum(math.prod(wv[n].shape) for n in SMALL)
    quarter = -(-n_small // (4 * 128 * 128)) * 128 * 128
    pad = lambda f: jnp.pad(f, (0, 4 * quarter - n_small))
    parts = [_to_parts(grads[n], BIG[n]) for n in big_names] + [pad(flat(grads)).reshape(4, quarter // 128, 128)]
    got = _rs_exchange(parts, name="grad_exchange")
    n_arr = len(parts)
    two_d = lambda a: a.reshape(-1, a.shape[-1])
    sums = [_sum4(two_d(got[a]), got[n_arr + a].reshape(3, -1, got[a].shape[-1]), name=f"grad_sum4_{a}") for a in range(n_arr)]
    sib = _sib_exchange(sums, name="grad_sibling")

    out_g, out_d, out_m, out_v = {}, {}, {}, {}
    for a, n in enumerate(big_names):
        shp = wv[n].shape
        res = _adamw(sums[a], sib[a], two_d(wv[n]), two_d(mv[n]), two_d(vv[n]), name=f"adamw_{n}")
        out_g[n], out_d[n], out_m[n], out_v[n] = [r.reshape(shp) for r in res]
    g_quarter = _pw(lambda p, q: p + q, [sums[-1], sib[-1]], [_row(quarter // 128, 128)] * 2, [SDS((quarter // 128, 128), f32)],
                    [_row(quarter // 128, 128)], (1, 1), name="small_grad_sum")[0]
    g_small = _allgather4([g_quarter], name="gather_small_grads")[0].reshape(4 * quarter // 128, 128)
    zero = jnp.zeros_like(g_small)
    as_rows = lambda d: pad(flat(d)).reshape(4 * quarter // 128, 128)
    res = _adamw(g_small, zero, as_rows(wv), as_rows(mv), as_rows(vv), name="adamw_small")
    off = 0
    for n in SMALL:
        k = math.prod(wv[n].shape)
        for dst, r in zip((out_g, out_d, out_m, out_v), res):
            dst[n] = r.reshape(-1)[off:off + k].reshape(wv[n].shape)
        off += k
    return (loss, dx[None], *[out_g[n] for n in W_NAMES], *[out_d[n] for n in W_NAMES], *[out_m[n] for n in W_NAMES],
            *[out_v[n] for n in W_NAMES])
```

```python
import functools
import math

import jax
import jax.numpy as jnp
from jax import lax
from jax.experimental import pallas as pl
from jax.experimental.pallas import tpu as pltpu

f32 = jnp.float32
bf16 = jnp.bfloat16
SDS = jax.ShapeDtypeStruct
MESH = pl.DeviceIdType.MESH

D = 1024
DEPTH = 2
RMS_EPS = 1e-6
GN_EPS = 1e-5
ROPE_THETA = 10000.0
RET_HEADS = 4
RET_DK = 128
RET_DV = 256
RET_CHUNK = 128
MLA_HEADS = 8
MLA_Q_LORA = 384
MLA_KV_LORA = 256
MLA_NOPE = 128
MLA_ROPE = 64
MLA_V = 128
MLA_QW = 256
S5_G = 64
S5_P = 64
S5_C = 16
S5_NJ = 8
S5_SEG = 8
FFN_H = 2816
ADAM_LR = 0.001
ADAM_B1 = 0.9
ADAM_B2 = 0.999
ADAM_EPS = 1e-08
ADAM_WD = 0.01
ADAM_STEP = 10
VMEM_BIG = 56 * 1024 * 1024

W_NAMES = ['norm1_g', 'w_in', 'ret_decay', 'ret_gn_g', 'mla_q_norm_g', 'mla_w_uq', 'mla_kv_norm_g', 'mla_w_ukv',
           's5_a_re', 's5_a_im', 's5_log_dt', 's5_b_re', 's5_b_im', 's5_c_re', 's5_c_im', 's5_d', 's5_w_glu',
           'w_branch', 'w_out', 'norm2_g', 'ffn_w_gu', 'ffn_w_down', 'final_g']
BIG = {'w_in': 2, 'mla_w_uq': 2, 'mla_w_ukv': 2, 's5_w_glu': 2, 'w_branch': 2, 'w_out': 1, 'ffn_w_gu': 2, 'ffn_w_down': 1}
SMALL = [n for n in W_NAMES if n not in BIG]


def _pick(n, cands=(512, 384, 256, 128)):
    if n <= 1024:
        return n
    for c in cands:
        if n % c == 0:
            return c
    raise ValueError(n)


def _params(sem, vmem=None):
    return pltpu.CompilerParams(dimension_semantics=sem, vmem_limit_bytes=vmem)


def _mm(a, b, *, tb=False, res=None, out_dtype=f32, name):
    M, K = a.shape
    N = b.shape[0] if tb else b.shape[1]
    tn = _pick(N)
    tk = K if K <= 3072 else _pick(K, (1408, 1024, 512))
    nk = K // tk
    tm = _pick(M)
    if M % 1024 == 0 and 1024 * tk * a.dtype.itemsize <= 4 * 1024 * 1024:
        tm = 1024
    assert M % tm == 0 and N % tn == 0 and K % tk == 0

    def body(*refs):
        if res is None:
            a_ref, b_ref, o_ref, acc = refs
        else:
            a_ref, b_ref, r_ref, o_ref, acc = refs
        k = pl.program_id(2)
        dn = (((1,), (1 if tb else 0,)), ((), ()))
        part = lax.dot_general(a_ref[...].astype(bf16), b_ref[...].astype(bf16), dn, preferred_element_type=f32)

        @pl.when(k == 0)
        def _():
            acc[...] = part

        @pl.when(k > 0)
        def _():
            acc[...] += part

        @pl.when(k == nk - 1)
        def _():
            v = acc[...]
            if res is not None:
                v = v + r_ref[...]
            o_ref[...] = v.astype(out_dtype)

    in_specs = [pl.BlockSpec((tm, tk), lambda i, j, k: (i, k)),
                pl.BlockSpec((tn, tk), lambda i, j, k: (j, k)) if tb else pl.BlockSpec((tk, tn), lambda i, j, k: (k, j))]
    args = [a, b]
    if res is not None:
        in_specs.append(pl.BlockSpec((tm, tn), lambda i, j, k: (i, j)))
        args.append(res)
    return pl.pallas_call(
        body, grid=(M // tm, N // tn, nk), in_specs=in_specs,
        out_specs=pl.BlockSpec((tm, tn), lambda i, j, k: (i, j)),
        out_shape=SDS((M, N), out_dtype), scratch_shapes=[pltpu.VMEM((tm, tn), f32)],
        compiler_params=_params(("parallel", "parallel", "arbitrary"), VMEM_BIG), name=name)(*args)


def _mmT(a, b, *, name):
    S, M = a.shape
    N = b.shape[1]
    tm = _pick(M)
    tn = _pick(N)
    tk = min(S, 1024)
    nk = S // tk

    def body(a_ref, b_ref, o_ref):
        k = pl.program_id(2)
        part = lax.dot_general(a_ref[...].astype(bf16), b_ref[...].astype(bf16), (((0,), (0,)), ((), ())),
                               preferred_element_type=f32)

        @pl.when(k == 0)
        def _():
            o_ref[...] = part

        @pl.when(k > 0)
        def _():
            o_ref[...] += part

    return pl.pallas_call(
        body, grid=(M // tm, N // tn, nk),
        in_specs=[pl.BlockSpec((tk, tm), lambda i, j, k: (k, i)), pl.BlockSpec((tk, tn), lambda i, j, k: (k, j))],
        out_specs=pl.BlockSpec((tm, tn), lambda i, j, k: (i, j)),
        out_shape=SDS((M, N), f32),
        compiler_params=_params(("parallel", "parallel", "arbitrary"), VMEM_BIG), name=name)(a, b)


def _pw(fn, ins, in_specs, outs, out_specs, grid, *, n_acc=0, name):
    n_in = len(ins)
    n_out = len(outs)

    def body(*refs):
        vals = fn(*[r[...] for r in refs[:n_in]])
        if not isinstance(vals, (tuple, list)):
            vals = (vals,)
        orefs = refs[n_in:]
        for r, v in zip(orefs[:n_out - n_acc], vals[:n_out - n_acc]):
            r[...] = v.astype(r.dtype)
        if n_acc:
            i = pl.program_id(1)

            @pl.when(i == 0)
            def _():
                for r, v in zip(orefs[n_out - n_acc:], vals[n_out - n_acc:]):
                    r[...] = v.astype(r.dtype)

            @pl.when(i > 0)
            def _():
                for r, v in zip(orefs[n_out - n_acc:], vals[n_out - n_acc:]):
                    r[...] += v.astype(r.dtype)

    res = pl.pallas_call(
        body, grid=grid, in_specs=in_specs, out_specs=out_specs, out_shape=outs,
        compiler_params=_params(("parallel", "arbitrary"), VMEM_BIG), name=name)(*ins)
    return res


def _row(T, w, col=None):
    if col is None:
        return pl.BlockSpec((T, w), lambda j, i: (i, 0))
    return pl.BlockSpec((T, w), lambda j, i: (i, col(j)))


def _par(w, col=None):
    if col is None:
        return pl.BlockSpec((1, w), lambda j, i: (0, 0))
    return pl.BlockSpec((1, w), lambda j, i: (0, col(j)))


def _rms(x, g):
    return x * lax.rsqrt(jnp.mean(x * x, axis=-1, keepdims=True) + RMS_EPS) * g


def _rope(x, cos, sinm, half):
    if half == 64:
        partner = pltpu.roll(x, 64, axis=1)
    else:
        lane = lax.broadcasted_iota(jnp.int32, x.shape, 1)
        partner = jnp.where((lane % (2 * half)) < half, pltpu.roll(x, 128 - half, axis=1), pltpu.roll(x, half, axis=1))
    return x * cos + partner * sinm


def _rope_t(x, cos, sinm, half):
    return _rope(x, cos, -sinm, half)


def _rmsnorm_fwd(x, g, *, name):
    S, W = x.shape
    T = min(S, 512)
    return _pw(lambda xv, gv: _rms(xv, gv), [x, g], [_row(T, W), _par(W)], [SDS((S, W), bf16)], [_row(T, W)],
               (1, S // T), name=name)[0]


def _rmsnorm_bwd(x, g, dh, dres, *, name):
    S, W = x.shape
    T = min(S, 512)

    def fn(xv, gv, dhv, drv):
        _, vjp = jax.vjp(_rms, xv, gv)
        dx, dg = vjp(dhv)
        return dx + drv, dg

    return _pw(fn, [x, g, dh, dres], [_row(T, W), _par(W), _row(T, W), _row(T, W)],
               [SDS((S, W), f32), SDS((1, W), f32)], [_row(T, W), _par(W)], (1, S // T), n_acc=1, name=name)


def _ret_tables(lg, reverse):
    C = RET_CHUNK
    ii = lax.broadcasted_iota(jnp.int32, (C, C), 0).astype(f32)
    jj = lax.broadcasted_iota(jnp.int32, (C, C), 1).astype(f32)
    if not reverse:
        E = ii - jj
        mask = E >= 0
        eq = ii + 1.0
        ek = (C - 1.0) - ii
    else:
        E = jj - ii
        mask = E > 0
        eq = C - ii
        ek = ii
    Dm = jnp.where(mask, jnp.exp(jnp.where(mask, E, 0.0) * lg), 0.0)
    Em = jnp.where(mask, E, 0.0)
    qw = jnp.exp(eq * lg)
    kw = jnp.exp(ek * lg)
    qw2 = jnp.concatenate([qw, qw], axis=1)
    return Dm, Em, eq, ek, qw, kw, qw2, jnp.exp(C * lg)


def _dot(a, b, dims):
    return lax.dot_general(a.astype(bf16), b.astype(bf16), (dims, ((), ())), preferred_element_type=f32)


NN = ((1,), (0,))
NT = ((1,), (1,))
TN = ((0,), (0,))


def _ret_dir_fwd(zr, lg, cos, sinm, *, reverse, name):
    S = zr.shape[0]
    C = RET_CHUNK
    TB = min(S, 512)
    nc = TB // C
    NB = S // TB
    d = 1 if reverse else 0
    scale = RET_DK ** -0.5

    def tb(b):
        return (NB - 1 - b) if reverse else b

    def body(lg_ref, q_ref, k_ref, v_ref, cos_ref, sin_ref, y_ref, st_ref, state):
        h = pl.program_id(0)
        b = pl.program_id(1)

        @pl.when(b == 0)
        def _():
            state[...] = jnp.zeros_like(state)

        Dm, _, _, _, _, kw, qw2, gC = _ret_tables(lg_ref[d, h], reverse)
        order = range(nc - 1, -1, -1) if reverse else range(nc)
        for c in order:
            rows = pl.ds(c * C, C)
            q = _rope(q_ref[rows, :], cos_ref[rows, :], sin_ref[rows, :], 64) * scale
            k = _rope(k_ref[rows, :], cos_ref[rows, :], sin_ref[rows, :], 64)
            v = v_ref[rows, :]
            st = state[...]
            st_ref[0, c] = st
            s = _dot(q, k, NT) * Dm
            o = _dot(s, v, NN) + _dot(q, st, NN) * qw2
            y_ref[rows, :] = o
            state[...] = gC * st + _dot(k * kw, v, TN)

    return pl.pallas_call(
        body, grid=(RET_HEADS, NB),
        in_specs=[pl.BlockSpec(memory_space=pltpu.SMEM),
                  pl.BlockSpec((TB, 128), lambda h, b: (tb(b), h)),
                  pl.BlockSpec((TB, 128), lambda h, b: (tb(b), 4 + h)),
                  pl.BlockSpec((TB, 256), lambda h, b: (tb(b), 4 + h)),
                  pl.BlockSpec((TB, 128), lambda h, b: (tb(b), 0)),
                  pl.BlockSpec((TB, 128), lambda h, b: (tb(b), 0))],
        out_specs=[pl.BlockSpec((TB, 256), lambda h, b: (tb(b), h)),
                   pl.BlockSpec((1, nc, 128, 256), lambda h, b: (h, tb(b), 0, 0))],
        out_shape=[SDS((S, 1024), f32), SDS((RET_HEADS, S // C, 128, 256), f32)],
        scratch_shapes=[pltpu.VMEM((128, 256), f32)],
        compiler_params=_params(("parallel", "arbitrary")), name=name)(lg, zr, zr, zr, cos, sinm)


def _ret_dir_bwd(zr, lg, cos, sinm, dy, states, *, reverse, name):
    S = zr.shape[0]
    C = RET_CHUNK
    TB = min(S, 512)
    nc = TB // C
    NB = S // TB
    d = 1 if reverse else 0
    scale = RET_DK ** -0.5

    def tb(b):
        return b if reverse else (NB - 1 - b)

    def body(lg_ref, q_ref, k_ref, v_ref, cos_ref, sin_ref, dy_ref, st_ref, dq_ref, dk_ref, dv_ref, dlg_ref, dstate):
        h = pl.program_id(0)
        b = pl.program_id(1)

        @pl.when(b == 0)
        def _():
            dstate[...] = jnp.zeros_like(dstate)
            dlg_ref[...] = jnp.zeros_like(dlg_ref)

        Dm, Em, eq, ek, qw, kw, qw2, gC = _ret_tables(lg_ref[d, h], reverse)
        order = range(nc) if reverse else range(nc - 1, -1, -1)
        dlg = jnp.zeros((), f32)
        for c in order:
            rows = pl.ds(c * C, C)
            cs, sn = cos_ref[rows, :], sin_ref[rows, :]
            q = _rope(q_ref[rows, :], cs, sn, 64) * scale
            k = _rope(k_ref[rows, :], cs, sn, 64)
            v = v_ref[rows, :]
            do = dy_ref[rows, :]
            st = st_ref[0, c]
            ds = dstate[...]
            p = _dot(q, k, NT)
            a = p * Dm
            dp = _dot(do, v, NT) * Dm
            dq_cross = _dot(do, st, NT) * qw
            dk_cross = _dot(v, ds, NT) * kw
            dq = _dot(dp, k, NN) + dq_cross
            dk = _dot(dp, q, TN) + dk_cross
            dv = _dot(a, do, TN) + _dot(k * kw, ds, NN)
            dlg = dlg + jnp.sum(dp * p * Em) + jnp.sum(dq_cross * q * eq) + jnp.sum(dk_cross * k * ek) \
                + C * gC * jnp.sum(ds * st)
            dstate[...] = gC * ds + _dot(q * qw, do, TN)
            dq_ref[rows, :] = _rope_t(dq, cs, sn, 64) * scale
            dk_ref[rows, :] = _rope_t(dk, cs, sn, 64)
            dv_ref[rows, :] = dv
        dlg_ref[...] += jnp.full(dlg_ref.shape, dlg, f32)

    return pl.pallas_call(
        body, grid=(RET_HEADS, NB),
        in_specs=[pl.BlockSpec(memory_space=pltpu.SMEM),
                  pl.BlockSpec((TB, 128), lambda h, b: (tb(b), h)),
                  pl.BlockSpec((TB, 128), lambda h, b: (tb(b), 4 + h)),
                  pl.BlockSpec((TB, 256), lambda h, b: (tb(b), 4 + h)),
                  pl.BlockSpec((TB, 128), lambda h, b: (tb(b), 0)),
                  pl.BlockSpec((TB, 128), lambda h, b: (tb(b), 0)),
                  pl.BlockSpec((TB, 256), lambda h, b: (tb(b), h)),
                  pl.BlockSpec((1, nc, 128, 256), lambda h, b: (h, tb(b), 0, 0))],
        out_specs=[pl.BlockSpec((TB, 128), lambda h, b: (tb(b), h)),
                   pl.BlockSpec((TB, 128), lambda h, b: (tb(b), h)),
                   pl.BlockSpec((TB, 256), lambda h, b: (tb(b), h)),
                   pl.BlockSpec((1, 1, 128), lambda h, b: (h, 0, 0))],
        out_shape=[SDS((S, 512), f32), SDS((S, 512), f32), SDS((S, 1024), f32), SDS((RET_HEADS, 1, 128), f32)],
        scratch_shapes=[pltpu.VMEM((128, 256), f32)],
        compiler_params=_params(("parallel", "arbitrary")), name=name)(lg, zr, zr, zr, cos, sinm, dy, states)


def _gn_gate(yf, yb, g, gn):
    y = yf + yb
    mu = jnp.mean(y, axis=-1, keepdims=True)
    var = jnp.mean(jnp.square(y - mu), axis=-1, keepdims=True)
    yn = (y - mu) * lax.rsqrt(var + GN_EPS)
    return jax.nn.silu(g) * (yn * gn)


def _flash_fwd(Q, K, kv, *, name):
    S = Q.shape[0]
    tq = min(S, 512)
    tk = min(S, 512)
    nk = S // tk

    def body(q_ref, k_ref, v_ref, o_ref, l_ref, m_s, l_s, acc):
        kk = pl.program_id(2)

        @pl.when(kk == 0)
        def _():
            m_s[...] = jnp.full_like(m_s, -jnp.inf)
            l_s[...] = jnp.zeros_like(l_s)
            acc[...] = jnp.zeros_like(acc)

        st = lax.dot_general(k_ref[...], q_ref[...], (NT, ((), ())), preferred_element_type=f32)
        m_prev = m_s[...]
        m_new = jnp.maximum(m_prev, jnp.max(st, axis=0, keepdims=True))
        pt = jnp.exp2(st - m_new)
        alpha = jnp.exp2(m_prev - m_new)
        l_s[...] = alpha * l_s[...] + jnp.sum(pt, axis=0, keepdims=True)
        acc[...] = alpha * acc[...] + lax.dot_general(v_ref[...], pt.astype(bf16), (TN, ((), ())), preferred_element_type=f32)
        m_s[...] = m_new

        @pl.when(kk == nk - 1)
        def _():
            o_ref[...] = jnp.transpose(acc[...] / l_s[...]).astype(bf16)
            l_ref[0] = m_s[...] + jnp.log2(l_s[...])

    return pl.pallas_call(
        body, grid=(MLA_HEADS, S // tq, nk),
        in_specs=[pl.BlockSpec((tq, 256), lambda h, i, k: (i, h)),
                  pl.BlockSpec((tk, 256), lambda h, i, k: (k, h)),
                  pl.BlockSpec((tk, 128), lambda h, i, k: (k, 2 * h + 1))],
        out_specs=[pl.BlockSpec((tq, 128), lambda h, i, k: (i, h)), pl.BlockSpec((1, 1, tq), lambda h, i, k: (h, 0, i))],
        out_shape=[SDS((S, 1024), bf16), SDS((MLA_HEADS, 1, S), f32)],
        scratch_shapes=[pltpu.VMEM((1, tq), f32), pltpu.VMEM((1, tq), f32), pltpu.VMEM((128, tq), f32)],
        compiler_params=_params(("parallel", "parallel", "arbitrary")), name=name)(Q, K, kv)


def _flash_bwd(Q, K, kv, O, L, dO, *, name):
    S = Q.shape[0]
    tq = min(S, 512)
    tk = min(S, 512)
    nq = S // tq
    ln2 = math.log(2.0)

    def body(q_ref, k_ref, v_ref, o_ref, l_ref, do_ref, dq_ref, dk_ref, dv_ref, dk_acc, dv_acc):
        kk = pl.program_id(1)
        i = pl.program_id(2)

        @pl.when((kk == 0) & (i == 0))
        def _():
            dq_ref[...] = jnp.zeros_like(dq_ref)

        @pl.when(i == 0)
        def _():
            dk_acc[...] = jnp.zeros_like(dk_acc)
            dv_acc[...] = jnp.zeros_like(dv_acc)

        q = q_ref[...]
        k = k_ref[...]
        do = do_ref[...]
        st = lax.dot_general(k, q, (NT, ((), ())), preferred_element_type=f32)
        pt = jnp.exp2(st - l_ref[0])
        delta = lax.dot_general(jnp.ones((8, 128), f32), do * o_ref[...].astype(f32), (NT, ((), ())),
                                preferred_element_type=f32, precision=lax.Precision.HIGHEST)[0:1, :]
        dob = do.astype(bf16)
        dv_acc[...] += lax.dot_general(pt.astype(bf16), dob, (NN, ((), ())), preferred_element_type=f32)
        dpt = lax.dot_general(v_ref[...], dob, (NT, ((), ())), preferred_element_type=f32)
        dst = (pt * (dpt - delta)).astype(bf16)
        dk_acc[...] += lax.dot_general(dst, q, (NN, ((), ())), preferred_element_type=f32)
        rows = pl.ds(pl.multiple_of(i * tq, tq), tq)
        dq_ref[rows, :] += lax.dot_general(dst, k, (TN, ((), ())), preferred_element_type=f32)

        @pl.when(i == nq - 1)
        def _():
            dk_ref[...] = dk_acc[...] * ln2
            dv_ref[...] = dv_acc[...]

    return pl.pallas_call(
        body, grid=(MLA_HEADS, S // tk, nq),
        in_specs=[pl.BlockSpec((tq, 256), lambda h, k, i: (i, h)),
                  pl.BlockSpec((tk, 256), lambda h, k, i: (k, h)),
                  pl.BlockSpec((tk, 128), lambda h, k, i: (k, 2 * h + 1)),
                  pl.BlockSpec((tq, 128), lambda h, k, i: (i, h)),
                  pl.BlockSpec((1, 1, tq), lambda h, k, i: (h, 0, i)),
                  pl.BlockSpec((tq, 128), lambda h, k, i: (i, h))],
        out_specs=[pl.BlockSpec((S, 256), lambda h, k, i: (0, h)),
                   pl.BlockSpec((tk, 256), lambda h, k, i: (k, h)),
                   pl.BlockSpec((tk, 128), lambda h, k, i: (k, h))],
        out_shape=[SDS((S, 2048), f32), SDS((S, 2048), f32), SDS((S, 1024), f32)],
        scratch_shapes=[pltpu.VMEM((tk, 256), f32), pltpu.VMEM((tk, 128), f32)],
        compiler_params=_params(("parallel", "arbitrary", "arbitrary"), VMEM_BIG), name=name)(Q, K, kv, O, L, dO)


def _mla_bwd_prep(dQ, dK, dV, cosm, sinm, *, name):
    S = dQ.shape[0]
    T = min(S, 256)
    scale = (MLA_NOPE + MLA_ROPE) ** -0.5

    def body(dq_ref, dk_ref, dv_ref, cos_ref, sin_ref, oq_ref, okv_ref, okr_ref):
        cs, sn = cos_ref[...], sin_ref[...]
        kr = jnp.zeros((T, 128), f32)
        for h in range(MLA_HEADS):
            a = 256 * h
            oq_ref[:, a:a + 128] = (dq_ref[:, a:a + 128] * scale).astype(bf16)
            oq_ref[:, a + 128:a + 256] = (_rope_t(dq_ref[:, a + 128:a + 256], cs, sn, 32) * scale).astype(bf16)
            okv_ref[:, a:a + 128] = dk_ref[:, a:a + 128].astype(bf16)
            okv_ref[:, a + 128:a + 256] = dv_ref[:, 128 * h:128 * h + 128].astype(bf16)
            kr = kr + dk_ref[:, a + 128:a + 256]
        okr_ref[...] = _rope_t(kr, cs, sn, 32)

    return pl.pallas_call(
        body, grid=(S // T,),
        in_specs=[pl.BlockSpec((T, 2048), lambda i: (i, 0)), pl.BlockSpec((T, 2048), lambda i: (i, 0)),
                  pl.BlockSpec((T, 1024), lambda i: (i, 0)), pl.BlockSpec((T, 128), lambda i: (i, 0)),
                  pl.BlockSpec((T, 128), lambda i: (i, 0))],
        out_specs=[pl.BlockSpec((T, 2048), lambda i: (i, 0)), pl.BlockSpec((T, 2048), lambda i: (i, 0)),
                   pl.BlockSpec((T, 128), lambda i: (i, 0))],
        out_shape=[SDS((S, 2048), bf16), SDS((S, 2048), bf16), SDS((S, 128), f32)],
        compiler_params=_params(("parallel",), VMEM_BIG), name=name)(dQ, dK, dV, cosm, sinm)


def _mla_norm_bwd(zm, qg, kvg, dcqn, dckvn, dkr, *, name):
    S = zm.shape[0]
    T = min(S, 512)

    def body(cq_ref, ckv_ref, qg_ref, kvg_ref, dcq_ref, dckv_ref, dkr_ref, o_ref, dqg_ref, dkvg_ref):
        i = pl.program_id(0)
        _, vjp = jax.vjp(_rms, cq_ref[...], qg_ref[...])
        dcq, dqg = vjp(dcq_ref[...])
        _, vjp2 = jax.vjp(_rms, ckv_ref[...], kvg_ref[...])
        dckv, dkvg = vjp2(dckv_ref[...])
        o_ref[:, 0:384] = dcq.astype(bf16)
        o_ref[:, 384:512] = jnp.zeros((T, 128), bf16)
        o_ref[:, 512:768] = dckv.astype(bf16)
        o_ref[:, 768:896] = dkr_ref[...].astype(bf16)

        @pl.when(i == 0)
        def _():
            dqg_ref[...] = dqg
            dkvg_ref[...] = dkvg

        @pl.when(i > 0)
        def _():
            dqg_ref[...] += dqg
            dkvg_ref[...] += dkvg

    return pl.pallas_call(
        body, grid=(S // T,),
        in_specs=[pl.BlockSpec((T, 384), lambda i: (i, 0)), pl.BlockSpec((T, 256), lambda i: (i, 2)),
                  pl.BlockSpec((1, 384), lambda i: (0, 0)), pl.BlockSpec((1, 256), lambda i: (0, 0)),
                  pl.BlockSpec((T, 384), lambda i: (i, 0)), pl.BlockSpec((T, 256), lambda i: (i, 0)),
                  pl.BlockSpec((T, 128), lambda i: (i, 0))],
        out_specs=[pl.BlockSpec((T, 896), lambda i: (i, 0)), pl.BlockSpec((1, 384), lambda i: (0, 0)),
                   pl.BlockSpec((1, 256), lambda i: (0, 0))],
        out_shape=[SDS((S, 896), bf16), SDS((1, 384), f32), SDS((1, 256), f32)],
        compiler_params=_params(("arbitrary",)), name=name)(zm, zm, qg, kvg, dcqn, dckvn, dkr)


def _s5_disc(a_re, a_im, ldt, b_re, b_im):
    dt = jnp.exp(ldt)
    ar = jnp.minimum(a_re, -1e-4)
    mag = jnp.exp(dt * ar)
    abr = mag * jnp.cos(dt * a_im)
    abi = mag * jnp.sin(dt * a_im)
    den = ar * ar + a_im * a_im
    nr = abr - 1.0
    ni = abi
    cr = (nr * ar + ni * a_im) / den
    ci = (ni * ar - nr * a_im) / den
    return abr, abi, cr * b_re - ci * b_im, cr * b_im + ci * b_re


def _s5_param_fwd(a_re, a_im, ldt, b_re, b_im, *, name):
    R = SDS((1, 8192), f32)
    M = SDS((16, 8192), f32)
    Pw = SDS((64, 8192), f32)

    def body(a_re_r, a_im_r, ldt_r, b_re_r, b_im_r, o1, o2, o3, o4, p_re, p_im):
        abr, abi, bbr, bbi = _s5_disc(a_re_r[...], a_im_r[...], ldt_r[...], b_re_r[...], b_im_r[...])
        o1[...] = abr
        o2[...] = abi
        o3[...] = bbr
        o4[...] = bbi
        dt = jnp.exp(ldt_r[...])
        ar = jnp.minimum(a_re_r[...], -1e-4)
        n = lax.broadcasted_iota(jnp.int32, (64, 8192), 0).astype(f32) + 1.0
        mag = jnp.exp(n * (dt * ar))
        ang = n * (dt * a_im_r[...])
        p_re[...] = mag * jnp.cos(ang)
        p_im[...] = mag * jnp.sin(ang)

    return pl.pallas_call(body, out_shape=[R, R, M, M, Pw, Pw], name=name)(a_re, a_im, ldt, b_re, b_im)


def _s5_param_bwd(a_re, a_im, ldt, b_re, b_im, d_abr, d_abi, d_bbr, d_bbi, *, name):
    R = SDS((1, 8192), f32)
    M = SDS((16, 8192), f32)

    def body(a_re_r, a_im_r, ldt_r, b_re_r, b_im_r, c1, c2, c3, c4, o1, o2, o3, o4, o5):
        _, vjp = jax.vjp(_s5_disc, a_re_r[...], a_im_r[...], ldt_r[...], b_re_r[...], b_im_r[...])
        g = vjp((c1[...], c2[...], c3[...], c4[...]))
        for o, v in zip((o1, o2, o3, o4, o5), g):
            o[...] = v

    return pl.pallas_call(body, out_shape=[R, R, R, M, M], name=name)(a_re, a_im, ldt, b_re, b_im, d_abr, d_abi, d_bbr, d_bbi)


def _scan_core(sc, ar, ai, pwr_ref, pwi_ref, a64r, a64i, carry, *, reverse, T):
    L = T // S5_SEG
    arb = [jnp.broadcast_to(ar[:, 128 * k:128 * k + 128], (8, 128)) for k in range(4)]
    aib = [jnp.broadcast_to(ai[:, 128 * k:128 * k + 128], (8, 128)) for k in range(4)]

    def step(r0, c):
        r = (L - 1 - r0) if reverse else r0
        new_r, new_i = [], []
        for k in range(4):
            idx = pl.ds(r, S5_SEG, stride=L)
            br = sc[k, idx, :]
            bi = sc[4 + k, idx, :]
            cr, ci = c[k], c[4 + k]
            nr = arb[k] * cr - aib[k] * ci + br
            ni = arb[k] * ci + aib[k] * cr + bi
            sc[k, idx, :] = nr
            sc[4 + k, idx, :] = ni
            new_r.append(nr)
            new_i.append(ni)
        return tuple(new_r + new_i)

    c = lax.fori_loop(0, L, step, tuple(jnp.zeros((8, 128), f32) for _ in range(8)), unroll=4)
    segs = range(S5_SEG - 1, -1, -1) if reverse else range(S5_SEG)
    for k in range(4):
        lr, li = c[k], c[4 + k]
        cr = carry[k, 0:1, :]
        ci = carry[4 + k, 0:1, :]
        a6r = a64r[:, 128 * k:128 * k + 128]
        a6i = a64i[:, 128 * k:128 * k + 128]
        pr = pwr_ref[:, 128 * k:128 * k + 128]
        pi = pwi_ref[:, 128 * k:128 * k + 128]
        for seg in segs:
            sl = pl.ds(seg * L, L)
            sc[k, sl, :] += pr * cr - pi * ci
            sc[4 + k, sl, :] += pr * ci + pi * cr
            ncr = lr[seg:seg + 1, :] + a6r * cr - a6i * ci
            nci = li[seg:seg + 1, :] + a6r * ci + a6i * cr
            cr, ci = ncr, nci
        carry[k, 0:1, :] = cr
        carry[4 + k, 0:1, :] = ci


def _s5_scan_fwd(u, BBr, BBi, CCr, CCi, abr, abi, pwr, pwi, *, reverse, name):
    S = u.shape[0]
    T = min(S, 512)
    NB = S // T
    L = T // S5_SEG
    d = 1 if reverse else 0

    def tb(b):
        return (NB - 1 - b) if reverse else b

    def body(u_ref, bbr_ref, bbi_ref, ccr_ref, cci_ref, ar_ref, ai_ref, pwr_ref, pwi_ref, y_ref, xr_ref, xi_ref, sc, carry):
        b = pl.program_id(1)

        @pl.when(b == 0)
        def _():
            carry[...] = jnp.zeros_like(carry)

        ub = u_ref[...].astype(bf16)
        bur = lax.dot_general(ub, bbr_ref[0, 0], (NN, ((), ())), preferred_element_type=f32)
        bui = lax.dot_general(ub, bbi_ref[0, 0], (NN, ((), ())), preferred_element_type=f32)
        for k in range(4):
            sc[k] = bur[:, 128 * k:128 * k + 128]
            sc[4 + k] = bui[:, 128 * k:128 * k + 128]
        a6 = (0 if reverse else L - 1)
        _scan_core(sc, ar_ref[...], ai_ref[...], pwr_ref, pwi_ref, pwr_ref[a6:a6 + 1, :], pwi_ref[a6:a6 + 1, :], carry,
                   reverse=reverse, T=T)
        y = jnp.zeros((T, 128), f32)
        for k in range(4):
            xr = sc[k]
            xi = sc[4 + k]
            xr_ref[:, 128 * k:128 * k + 128] = xr
            xi_ref[:, 128 * k:128 * k + 128] = xi
            y = y + _dot(xr, ccr_ref[0, 0, 128 * k:128 * k + 128, :], NN) - _dot(xi, cci_ref[0, 0, 128 * k:128 * k + 128, :], NN)
        y_ref[...] = y

    mat = lambda shp: pl.BlockSpec((1, 1) + shp, lambda j, b: (d, j, 0, 0))
    vec = lambda r: pl.BlockSpec((r, 512), lambda j, b: (0, d * S5_NJ + j))
    return pl.pallas_call(
        body, grid=(S5_NJ, NB),
        in_specs=[pl.BlockSpec((T, 128), lambda j, b: (tb(b), j)), mat((128, 512)), mat((128, 512)), mat((512, 128)),
                  mat((512, 128)), vec(1), vec(1), vec(L), vec(L)],
        out_specs=[pl.BlockSpec((T, 128), lambda j, b: (tb(b), j)), pl.BlockSpec((T, 512), lambda j, b: (tb(b), j)),
                   pl.BlockSpec((T, 512), lambda j, b: (tb(b), j))],
        out_shape=[SDS((S, 1024), f32), SDS((S, 4096), f32), SDS((S, 4096), f32)],
        scratch_shapes=[pltpu.VMEM((8, T, 128), f32), pltpu.VMEM((8, 8, 128), f32)],
        compiler_params=_params(("parallel", "arbitrary")), name=name)(u, BBr, BBi, CCr, CCi, abr, abi, pwr, pwi)


def _s5_scan_bwd(u, dy, xr, xi, BBr, BBi, CCr, CCi, abr, abi, pwr, pwi, *, reverse, name):
    S = u.shape[0]
    T = min(S, 512)
    NB = S // T
    L = T // S5_SEG
    d = 1 if reverse else 0
    adj_rev = not reverse

    def tb(b):
        return b if reverse else (NB - 1 - b)

    def bnd(b):
        t = tb(b)
        if reverse:
            return jnp.minimum((t + 1) * (T // 8), S // 8 - 1)
        return jnp.maximum(t * (T // 8) - 1, 0)

    def body(u_ref, dy_ref, xr_ref, xi_ref, xbr_ref, xbi_ref, bbr_ref, bbi_ref, ccr_ref, cci_ref, ar_ref, ai_ref,
             pwr_ref, pwi_ref, du_ref, dbbr_ref, dbbi_ref, dccr_ref, dcci_ref, dar_ref, dai_ref, sc, carry):
        b = pl.program_id(1)

        @pl.when(b == 0)
        def _():
            carry[...] = jnp.zeros_like(carry)
            for r in (dbbr_ref, dbbi_ref, dccr_ref, dcci_ref, dar_ref, dai_ref):
                r[...] = jnp.zeros_like(r)

        dyb = dy_ref[...]
        gxr = lax.dot_general(dyb, ccr_ref[0, 0], (NT, ((), ())), preferred_element_type=f32)
        gxi = -lax.dot_general(dyb, cci_ref[0, 0], (NT, ((), ())), preferred_element_type=f32)
        for k in range(4):
            sc[k] = gxr[:, 128 * k:128 * k + 128]
            sc[4 + k] = gxi[:, 128 * k:128 * k + 128]
        a6 = (0 if adj_rev else L - 1)
        nai = -ai_ref[...]
        _scan_core(sc, ar_ref[...], nai, pwr_ref, _Neg(pwi_ref), pwr_ref[a6:a6 + 1, :], -pwi_ref[a6:a6 + 1, :], carry,
                   reverse=adj_rev, T=T)
        ub = u_ref[...].astype(bf16)
        first = (b == NB - 1)
        row = lax.broadcasted_iota(jnp.int32, (T, 128), 0)
        du = jnp.zeros((T, 128), f32)
        for k in range(4):
            cols = slice(128 * k, 128 * k + 128)
            lr = sc[k]
            li = sc[4 + k]
            lrb = lr.astype(bf16)
            lib = li.astype(bf16)
            du = du + lax.dot_general(lrb, bbr_ref[0, 0, :, cols], (NT, ((), ())), preferred_element_type=f32) \
                + lax.dot_general(lib, bbi_ref[0, 0, :, cols], (NT, ((), ())), preferred_element_type=f32)
            dbbr_ref[0, 0, :, cols] += lax.dot_general(ub, lrb, (TN, ((), ())), preferred_element_type=f32)
            dbbi_ref[0, 0, :, cols] += lax.dot_general(ub, lib, (TN, ((), ())), preferred_element_type=f32)
            xrk = xr_ref[:, cols]
            xik = xi_ref[:, cols]
            dccr_ref[0, 0, cols, :] += lax.dot_general(xrk.astype(bf16), dyb, (TN, ((), ())), preferred_element_type=f32)
            dcci_ref[0, 0, cols, :] -= lax.dot_general(xik.astype(bf16), dyb, (TN, ((), ())), preferred_element_type=f32)
            if reverse:
                edge_r = jnp.where(first, 0.0, xbr_ref[0:1, cols])
                edge_i = jnp.where(first, 0.0, xbi_ref[0:1, cols])
                xpr = jnp.where(row == T - 1, edge_r, pltpu.roll(xrk, T - 1, axis=0))
                xpi = jnp.where(row == T - 1, edge_i, pltpu.roll(xik, T - 1, axis=0))
            else:
                edge_r = jnp.where(first, 0.0, xbr_ref[7:8, cols])
                edge_i = jnp.where(first, 0.0, xbi_ref[7:8, cols])
                xpr = jnp.where(row == 0, edge_r, pltpu.roll(xrk, 1, axis=0))
                xpi = jnp.where(row == 0, edge_i, pltpu.roll(xik, 1, axis=0))
            dar_ref[:, cols] += jnp.sum(xpr * lr + xpi * li, axis=0, keepdims=True)
            dai_ref[:, cols] += jnp.sum(xpr * li - xpi * lr, axis=0, keepdims=True)
        du_ref[...] = du

    mat = lambda shp: pl.BlockSpec((1, 1) + shp, lambda j, b: (d, j, 0, 0))
    omat = lambda shp: pl.BlockSpec((1, 1) + shp, lambda j, b: (0, j, 0, 0))
    vec = lambda r: pl.BlockSpec((r, 512), lambda j, b: (0, d * S5_NJ + j))
    blk = lambda w: pl.BlockSpec((T, w), lambda j, b: (tb(b), j))
    return pl.pallas_call(
        body, grid=(S5_NJ, NB),
        in_specs=[blk(128), blk(128), blk(512), blk(512),
                  pl.BlockSpec((8, 512), lambda j, b: (bnd(b), j)), pl.BlockSpec((8, 512), lambda j, b: (bnd(b), j)),
                  mat((128, 512)), mat((128, 512)), mat((512, 128)), mat((512, 128)), vec(1), vec(1), vec(L), vec(L)],
        out_specs=[blk(128), omat((128, 512)), omat((128, 512)), omat((512, 128)), omat((512, 128)),
                   pl.BlockSpec((1, 512), lambda j, b: (0, j)), pl.BlockSpec((1, 512), lambda j, b: (0, j))],
        out_shape=[SDS((S, 1024), f32), SDS((1, 8, 128, 512), f32), SDS((1, 8, 128, 512), f32), SDS((1, 8, 512, 128), f32),
                   SDS((1, 8, 512, 128), f32), SDS((1, 4096), f32), SDS((1, 4096), f32)],
        scratch_shapes=[pltpu.VMEM((8, T, 128), f32), pltpu.VMEM((8, 8, 128), f32)],
        compiler_params=_params(("parallel", "arbitrary"), VMEM_BIG), name=name)(
            u, dy, xr, xi, xr, xi, BBr, BBi, CCr, CCi, abr, abi, pwr, pwi)


class _Neg:
    def __init__(self, ref):
        self.ref = ref

    def __getitem__(self, idx):
        return -self.ref[idx]


def _silu_mul(g, u):
    return jax.nn.silu(g) * u


def _mixf(p0, p1, p2, z0, z1, z2):
    return jax.nn.sigmoid(z0) * p0 + jax.nn.sigmoid(z1) * p1 + jax.nn.sigmoid(z2) * p2


def _s5_act(yf, yb, u, dd):
    return jax.nn.gelu(yf + yb + dd * u)


def _glu(a, b):
    return a * jax.nn.sigmoid(b)


def _layer_fwd(x, w, tabs, l):
    S = x.shape[0]
    T = min(S, 512)
    I = S // T
    nm = lambda s: f"L{l}_{s}"
    sv = {'x': x}
    h = _rmsnorm_fwd(x, w['norm1_g'], name=nm("norm1"))
    zr = _mm(h, w['W_ret'], name=nm("in_ret"))
    zm = _mm(h, w['W_mla'], name=nm("in_mla"))
    zs = _mm(h, w['W_s5'], name=nm("in_s5"))
    zg = _mm(h, w['W_gate'], name=nm("in_gate"))
    sv.update(h=h, zr=zr, zm=zm, zs=zs, zg=zg)

    yf, stf = _ret_dir_fwd(zr, w['lg'], tabs['cos_r'], tabs['sin_r'], reverse=False, name=nm("ret_f"))
    yb, stb = _ret_dir_fwd(zr, w['lg'], tabs['cos_r'], tabs['sin_r'], reverse=True, name=nm("ret_b"))
    hd = lambda j: j
    y_ret = _pw(_gn_gate, [yf, yb, zr, w['ret_gn_g']],
                [_row(T, 256, hd), _row(T, 256, hd), _row(T, 256, lambda j: 8 + j), _par(256, hd)],
                [SDS((S, 1024), bf16)], [_row(T, 256, hd)], (RET_HEADS, I), name=nm("ret_gn"))[0]
    sv.update(yf=yf, yb=yb, stf=stf, stb=stb, y_ret=y_ret)

    cqn, ckvn = _pw(lambda a, b, g1, g2: (_rms(a, g1), _rms(b, g2)), [zm, zm, w['mla_q_norm_g'], w['mla_kv_norm_g']],
                    [_row(T, 384), _row(T, 256, lambda j: 2), _par(384), _par(256)],
                    [SDS((S, 384), bf16), SDS((S, 256), bf16)], [_row(T, 384), _row(T, 256)], (1, I), name=nm("mla_norm"))
    q = _mm(cqn, w['W_uq'], name=nm("mla_uq"))
    kv = _mm(ckvn, w['W_ukv'], out_dtype=bf16, name=nm("mla_ukv"))
    sc = (MLA_NOPE + MLA_ROPE) ** -0.5 * math.log2(math.e)
    Q = _pw(lambda xq, cs, sn: jnp.concatenate([xq[:, :128] * sc, _rope(xq[:, 128:], cs, sn, 32) * sc], axis=1),
            [q, tabs['cos_m'], tabs['sin_m']], [_row(T, 256, hd), _row(T, 128), _row(T, 128)],
            [SDS((S, 2048), bf16)], [_row(T, 256, hd)], (MLA_HEADS, I), name=nm("mla_qprep"))[0]
    K = _pw(lambda kn, kr, cs, sn: jnp.concatenate([kn.astype(f32), _rope(kr, cs, sn, 32)], axis=1),
            [kv, zm, tabs['cos_m'], tabs['sin_m']],
            [_row(T, 128, lambda j: 2 * j), _row(T, 128, lambda j: 6), _row(T, 128), _row(T, 128)],
            [SDS((S, 2048), bf16)], [_row(T, 256, hd)], (MLA_HEADS, I), name=nm("mla_kprep"))[0]
    O, Lse = _flash_fwd(Q, K, kv, name=nm("mla_attn"))
    sv.update(cqn=cqn, ckvn=ckvn, kv=kv, Q=Q, K=K, O=O, Lse=Lse)

    s5 = w['s5']
    ysf, xrf, xif = _s5_scan_fwd(zs, s5['BBr'], s5['BBi'], s5['CCr'], s5['CCi'], s5['abr'], s5['abi'], s5['pwr_f'], s5['pwi_f'],
                                 reverse=False, name=nm("s5_f"))
    ysb, xrb, xib = _s5_scan_fwd(zs, s5['BBr'], s5['BBi'], s5['CCr'], s5['CCi'], s5['abr'], s5['abi'], s5['pwr_f'], s5['pwi_f'],
                                 reverse=True, name=nm("s5_b"))
    gact = _pw(_s5_act, [ysf, ysb, zs, w['s5_d']], [_row(T, D), _row(T, D), _row(T, D), _par(D)],
               [SDS((S, D), bf16)], [_row(T, D)], (1, I), name=nm("s5_act"))[0]
    gg = _mm(gact, w['W_glu'], name=nm("s5_glu_mm"))
    y_s5 = _pw(_glu, [gg, gg], [_row(T, D), _row(T, D, lambda j: 1)], [SDS((S, D), bf16)], [_row(T, D)], (1, I),
               name=nm("s5_glu"))[0]
    sv.update(ysf=ysf, ysb=ysb, xrf=xrf, xif=xif, xrb=xrb, xib=xib, gact=gact, gg=gg, y_s5=y_s5)

    ys = [y_ret, O, y_s5]
    pr = [_mm(ys[i], w['W_br'][i], name=nm(f"branch{i}")) for i in range(3)]
    mix = _pw(_mixf, pr + [zg, zg, zg],
              [_row(T, D)] * 3 + [_row(T, D), _row(T, D, lambda j: 1), _row(T, D, lambda j: 2)],
              [SDS((S, D), bf16)], [_row(T, D)], (1, I), name=nm("mix"))[0]
    x1 = _mm(mix, w['W_out'], res=x, name=nm("out_proj"))
    h2 = _rmsnorm_fwd(x1, w['norm2_g'], name=nm("norm2"))
    fgu = _mm(h2, w['W_gu'], name=nm("ffn_gu"))
    act = _pw(_silu_mul, [fgu, fgu], [_row(T, 1408, lambda j: j), _row(T, 1408, lambda j: 2 + j)],
              [SDS((S, FFN_H), bf16)], [_row(T, 1408, lambda j: j)], (2, I), name=nm("ffn_act"))[0]
    x2 = _mm(act, w['W_down'], res=x1, name=nm("ffn_down"))
    sv.update(pr=pr, mix=mix, x1=x1, h2=h2, fgu=fgu, act=act)
    return x2, sv


def _vjp_fn(fn, n_primal, cast=None):
    def g(*args):
        _, vjp = jax.vjp(fn, *args[:n_primal])
        return vjp(args[n_primal].astype(f32))
    return g


def _layer_bwd(dx2, w, tabs, sv, l):
    S = dx2.shape[0]
    T = min(S, 512)
    I = S // T
    nm = lambda s: f"L{l}_b_{s}"
    g = {}
    hd = lambda j: j

    dact = _mm(dx2, w['W_down'], tb=True, name=nm("ffn_down_dx"))
    g['W_down'] = _mmT(sv['act'], dx2, name=nm("ffn_down_dw"))
    dfg, dfu = _pw(_vjp_fn(_silu_mul, 2), [sv['fgu'], sv['fgu'], dact],
                   [_row(T, 1408, lambda j: j), _row(T, 1408, lambda j: 2 + j), _row(T, 1408, lambda j: j)],
                   [SDS((S, FFN_H), bf16), SDS((S, FFN_H), bf16)], [_row(T, 1408, lambda j: j)] * 2, (2, I), name=nm("ffn_act"))
    dfgu = jnp.concatenate([dfg, dfu], axis=1)
    g['W_gu'] = _mmT(sv['h2'], dfgu, name=nm("ffn_gu_dw"))
    dh2 = _mm(dfgu, w['W_gu'], tb=True, name=nm("ffn_gu_dx"))
    dx1, g['norm2_g'] = _rmsnorm_bwd(sv['x1'], w['norm2_g'], dh2, dx2, name=nm("norm2"))

    dmix = _mm(dx1, w['W_out'], tb=True, name=nm("out_dx"))
    g['W_out'] = _mmT(sv['mix'], dx1, name=nm("out_dw"))
    zg = sv['zg']
    outs = _pw(_vjp_fn(_mixf, 6), sv['pr'] + [zg, zg, zg, dmix],
               [_row(T, D)] * 3 + [_row(T, D), _row(T, D, lambda j: 1), _row(T, D, lambda j: 2), _row(T, D)],
               [SDS((S, D), bf16)] * 6, [_row(T, D)] * 6, (1, I), name=nm("mix"))
    dpr, dzg = outs[:3], jnp.concatenate(outs[3:], axis=1)
    ys = [sv['y_ret'], sv['O'], sv['y_s5']]
    g['W_br'] = [_mmT(ys[i], dpr[i], name=nm(f"branch{i}_dw")) for i in range(3)]
    dys = [_mm(dpr[i], w['W_br'][i], tb=True, name=nm(f"branch{i}_dx")) for i in range(3)]

    gg = sv['gg']
    dga, dgb = _pw(_vjp_fn(_glu, 2), [gg, gg, dys[2]], [_row(T, D), _row(T, D, lambda j: 1), _row(T, D)],
                   [SDS((S, D), bf16)] * 2, [_row(T, D)] * 2, (1, I), name=nm("s5_glu"))
    dgg = jnp.concatenate([dga, dgb], axis=1)
    g['W_glu'] = _mmT(sv['gact'], dgg, name=nm("s5_glu_dw"))
    dgact = _mm(dgg, w['W_glu'], tb=True, name=nm("s5_glu_dx"))

    def act_bwd(yf, yb, u, dd, ct):
        _, vjp = jax.vjp(_s5_act, yf, yb, u, dd)
        dyf, _, du, ddd = vjp(ct)
        return dyf, du, ddd

    dys5, du_direct, g['s5_d'] = _pw(act_bwd, [sv['ysf'], sv['ysb'], sv['zs'], w['s5_d'], dgact],
                                     [_row(T, D)] * 3 + [_par(D), _row(T, D)],
                                     [SDS((S, D), bf16), SDS((S, D), f32), SDS((1, D), f32)],
                                     [_row(T, D), _row(T, D), _par(D)], (1, I), n_acc=1, name=nm("s5_act"))
    s5 = w['s5']
    rf = _s5_scan_bwd(sv['zs'], dys5, sv['xrf'], sv['xif'], s5['BBr'], s5['BBi'], s5['CCr'], s5['CCi'], s5['abr'], s5['abi'],
                      s5['pwr_a'], s5['pwi_a'], reverse=False, name=nm("s5_f"))
    rb = _s5_scan_bwd(sv['zs'], dys5, sv['xrb'], sv['xib'], s5['BBr'], s5['BBi'], s5['CCr'], s5['CCi'], s5['abr'], s5['abi'],
                      s5['pwr_a'], s5['pwi_a'], reverse=True, name=nm("s5_b"))
    g['s5'] = (rf[1:], rb[1:])
    dzs = _pw(lambda a, b, c: a + b + c, [du_direct, rf[0], rb[0]], [_row(T, D)] * 3, [SDS((S, D), bf16)], [_row(T, D)],
              (1, I), name=nm("s5_du"))[0]

    dQ, dK, dV = _flash_bwd(sv['Q'], sv['K'], sv['kv'], sv['O'], sv['Lse'], dys[1], name=nm("mla_attn"))
    dq_lin, dkv, dkr = _mla_bwd_prep(dQ, dK, dV, tabs['cos_m'], tabs['sin_m'], name=nm("mla_prep"))
    g['W_uq'] = _mmT(sv['cqn'], dq_lin, name=nm("mla_uq_dw"))
    dcqn = _mm(dq_lin, w['W_uq'], tb=True, name=nm("mla_uq_dx"))
    g['W_ukv'] = _mmT(sv['ckvn'], dkv, name=nm("mla_ukv_dw"))
    dckvn = _mm(dkv, w['W_ukv'], tb=True, name=nm("mla_ukv_dx"))
    dzm, g['mla_q_norm_g'], g['mla_kv_norm_g'] = _mla_norm_bwd(sv['zm'], w['mla_q_norm_g'], w['mla_kv_norm_g'], dcqn, dckvn, dkr,
                                                               name=nm("mla_norm"))

    zr = sv['zr']

    def gn_bwd(yf, yb, gt, gn, ct):
        _, vjp = jax.vjp(_gn_gate, yf, yb, gt, gn)
        dyf, _, dgt, dgn = vjp(ct)
        return dyf, dgt, dgn

    dyr, dgate, g['ret_gn_g'] = _pw(gn_bwd, [sv['yf'], sv['yb'], zr, w['ret_gn_g'], dys[0]],
                                    [_row(T, 256, hd), _row(T, 256, hd), _row(T, 256, lambda j: 8 + j), _par(256, hd),
                                     _row(T, 256, hd)],
                                    [SDS((S, 1024), bf16), SDS((S, 1024), bf16), SDS((1, 1024), f32)],
                                    [_row(T, 256, hd), _row(T, 256, hd), _par(256, hd)], (RET_HEADS, I), n_acc=1, name=nm("ret_gn"))
    qf, kf, vf, lgf = _ret_dir_bwd(zr, w['lg'], tabs['cos_r'], tabs['sin_r'], dyr, sv['stf'], reverse=False, name=nm("ret_f"))
    qb, kb, vb, lgb = _ret_dir_bwd(zr, w['lg'], tabs['cos_r'], tabs['sin_r'], dyr, sv['stb'], reverse=True, name=nm("ret_b"))
    g['lg'] = jnp.stack([lgf[:, 0, 0], lgb[:, 0, 0]])
    add2 = lambda a, b: a + b
    dq = _pw(add2, [qf, qb], [_row(T, 512)] * 2, [SDS((S, 512), bf16)], [_row(T, 512)], (1, I), name=nm("ret_dq"))[0]
    dk = _pw(add2, [kf, kb], [_row(T, 512)] * 2, [SDS((S, 512), bf16)], [_row(T, 512)], (1, I), name=nm("ret_dk"))[0]
    dv = _pw(add2, [vf, vb], [_row(T, D)] * 2, [SDS((S, D), bf16)], [_row(T, D)], (1, I), name=nm("ret_dv"))[0]
    dzr = jnp.concatenate([dq, dk, dv, dgate], axis=1)

    h = sv['h']
    g['W_ret'] = _mmT(h, dzr, name=nm("in_ret_dw"))
    g['W_mla'] = _mmT(h, dzm, name=nm("in_mla_dw"))
    g['W_s5'] = _mmT(h, dzs, name=nm("in_s5_dw"))
    g['W_gate'] = _mmT(h, dzg, name=nm("in_gate_dw"))
    dh = _mm(dzr, w['W_ret'], tb=True, name=nm("in_ret_dx"))
    dh = _mm(dzm, w['W_mla'], tb=True, res=dh, name=nm("in_mla_dx"))
    dh = _mm(dzs, w['W_s5'], tb=True, res=dh, name=nm("in_s5_dx"))
    dh = _mm(dzg, w['W_gate'], tb=True, res=dh, name=nm("in_gate_dx"))
    dx, g['norm1_g'] = _rmsnorm_bwd(sv['x'], w['norm1_g'], dh, dx1, name=nm("norm1"))
    return dx, g


def _loss_head(x, tgt, gain, *, name):
    S, W = x.shape
    T = min(S, 512)

    def loss_fn(xv, gv, tv):
        return 0.5 * jnp.sum(jnp.mean(jnp.square(_rms(xv, gv) - tv), axis=-1, keepdims=True), axis=0, keepdims=True)

    def fn(xv, gv, tv):
        lv, vjp = jax.vjp(lambda a, b: loss_fn(a, b, tv), xv, gv)
        dx, dg = vjp(jnp.ones((1, 1), f32))
        return dx, jnp.broadcast_to(lv, (1, 128)), dg

    return _pw(fn, [x, gain, tgt], [_row(T, W), _par(W), _row(T, W)],
               [SDS((S, W), f32), SDS((1, 128), f32), SDS((1, W), f32)], [_row(T, W), _par(128), _par(W)],
               (1, S // T), n_acc=2, name=name)


def _rope_tabs(S):
    def tab(dim):
        inv = 1.0 / (ROPE_THETA ** (jnp.arange(0, dim, 2, dtype=f32) / dim))
        ang = jnp.arange(S, dtype=f32)[:, None] * inv[None, :]
        return jnp.cos(ang), jnp.sin(ang)

    cr, sr = tab(RET_DK)
    cm, sm = tab(MLA_ROPE)
    z = jnp.zeros((S, 64), f32)
    return {'cos_r': jnp.concatenate([cr, cr], axis=1), 'sin_r': jnp.concatenate([-sr, sr], axis=1),
            'cos_m': jnp.concatenate([cm, cm, z], axis=1), 'sin_m': jnp.concatenate([-sm, sm, z], axis=1)}


def _bd_B(bb):
    b5 = bb.reshape(16, 2, 8, 8, 64)
    return jnp.einsum('cdjgp,gh->djgchp', b5, jnp.eye(8, dtype=bb.dtype)).reshape(2, 8, 128, 512)


def _bd_B_t(dBB):
    return jnp.einsum('djgcgp->cdjgp', dBB.reshape(2, 8, 8, 16, 8, 64)).reshape(16, 8192)


def _bd_C(c):
    c5 = c.reshape(2, 8, 8, 16, 64)
    return jnp.einsum('djgcp,gh->djgphc', c5, jnp.eye(8, dtype=c.dtype)).reshape(2, 8, 512, 128)


def _bd_C_t(dCC):
    return jnp.einsum('djgpgc->djgcp', dCC.reshape(2, 8, 8, 64, 8, 16)).reshape(2, 64, 16, 64)


def _s5_rows(p, l):
    a_re = p['s5_a_re'][l].reshape(1, 8192)
    a_im = p['s5_a_im'][l].reshape(1, 8192)
    ldt = jnp.broadcast_to(p['s5_log_dt'][l][:, :, None], (2, S5_G, S5_P)).reshape(1, 8192)
    b_re = p['s5_b_re'][l].transpose(3, 0, 1, 2).reshape(16, 8192)
    b_im = p['s5_b_im'][l].transpose(3, 0, 1, 2).reshape(16, 8192)
    return a_re, a_im, ldt, b_re, b_im


def _layer_weights(big, p, l):
    w_in = big['w_in'][l]
    z = lambda n: jnp.zeros((D, n), w_in.dtype)
    w = {
        'W_ret': w_in[:, 0:3072],
        'W_mla': jnp.concatenate([w_in[:, 3072:3456], z(128), w_in[:, 3456:3712], w_in[:, 3712:3776], z(64)], axis=1),
        'W_s5': w_in[:, 3776:4800],
        'W_gate': w_in[:, 4800:7872],
        'W_uq': jnp.pad(big['mla_w_uq'][l].reshape(MLA_Q_LORA, MLA_HEADS, 192), ((0, 0), (0, 0), (0, 64))).reshape(MLA_Q_LORA, 2048),
        'W_ukv': big['mla_w_ukv'][l],
        'W_glu': big['s5_w_glu'][l],
        'W_br': [big['w_branch'][l, i] for i in range(3)],
        'W_out': big['w_out'][l],
        'W_gu': big['ffn_w_gu'][l],
        'W_down': big['ffn_w_down'][l],
    }
    for n in ('norm1_g', 'ret_gn_g', 'mla_q_norm_g', 'mla_kv_norm_g', 's5_d', 'norm2_g'):
        w[n] = p[n][l][None, :]
    w['lg'] = jax.nn.log_sigmoid(p['ret_decay'][l])
    rows = _s5_rows(p, l)
    abr, abi, bbr, bbi, pwr, pwi = _s5_param_fwd(*rows, name=f"L{l}_s5_param")
    flip = lambda t, first: jnp.concatenate([t[::-1, :4096], t[:, 4096:]] if first else [t[:, :4096], t[::-1, 4096:]], axis=1)
    w['s5'] = {'abr': abr, 'abi': abi, 'BBr': _bd_B(bbr).astype(bf16), 'BBi': _bd_B(bbi).astype(bf16),
               'CCr': _bd_C(p['s5_c_re'][l]).astype(bf16), 'CCi': _bd_C(p['s5_c_im'][l]).astype(bf16),
               'pwr_f': flip(pwr, False), 'pwi_f': flip(pwi, False), 'pwr_a': flip(pwr, True), 'pwi_a': flip(pwi, True),
               'rows': rows}
    return w


def _layer_grads(g, w, p, l):
    out = {}
    m = g['W_mla']
    out['w_in'] = jnp.concatenate([g['W_ret'], m[:, 0:384], m[:, 512:768], m[:, 768:832], g['W_s5'], g['W_gate']], axis=1)
    out['mla_w_uq'] = g['W_uq'].reshape(MLA_Q_LORA, MLA_HEADS, 256)[:, :, :192].reshape(MLA_Q_LORA, 1536)
    out['mla_w_ukv'] = g['W_ukv']
    out['s5_w_glu'] = g['W_glu']
    out['w_branch'] = jnp.stack(g['W_br'])
    out['w_out'] = g['W_out']
    out['ffn_w_gu'] = g['W_gu']
    out['ffn_w_down'] = g['W_down']
    for n in ('norm1_g', 'ret_gn_g', 'mla_q_norm_g', 'mla_kv_norm_g', 's5_d', 'norm2_g'):
        out[n] = g[n][0]
    out['ret_decay'] = g['lg'] * jax.nn.sigmoid(-p['ret_decay'][l])
    (fB_r, fB_i, fC_r, fC_i, fa_r, fa_i), (bB_r, bB_i, bC_r, bC_i, ba_r, ba_i) = g['s5']
    cat = lambda a, b: jnp.concatenate([a, b], axis=0)
    d_bbr = _bd_B_t(cat(fB_r, bB_r))
    d_bbi = _bd_B_t(cat(fB_i, bB_i))
    out['s5_c_re'] = _bd_C_t(cat(fC_r, bC_r))
    out['s5_c_im'] = _bd_C_t(cat(fC_i, bC_i))
    d_abr = jnp.concatenate([fa_r, ba_r], axis=1)
    d_abi = jnp.concatenate([fa_i, ba_i], axis=1)
    da_re, da_im, dldt, db_re, db_im = _s5_param_bwd(*w['s5']['rows'], d_abr, d_abi, d_bbr, d_bbi, name=f"L{l}_b_s5_param")
    out['s5_a_re'] = da_re.reshape(2, S5_G, S5_P)
    out['s5_a_im'] = da_im.reshape(2, S5_G, S5_P)
    out['s5_log_dt'] = dldt.reshape(2, S5_G, S5_P).sum(axis=-1)
    out['s5_b_re'] = db_re.reshape(16, 2, S5_G, S5_P).transpose(1, 2, 3, 0)
    out['s5_b_im'] = db_im.reshape(16, 2, S5_G, S5_P).transpose(1, 2, 3, 0)
    return out


def _local_step(x, tgt, big, p):
    S = x.shape[0]
    assert S % 512 == 0
    tabs = _rope_tabs(S)
    ws, svs = [], []
    h = x
    for l in range(DEPTH):
        w = _layer_weights(big, p, l)
        h, sv = _layer_fwd(h, w, tabs, l)
        ws.append(w)
        svs.append(sv)
    dx, lossv, dfinal = _loss_head(h, tgt, p['final_g'][None, :], name="loss_head")
    per_layer = [None] * DEPTH
    for l in reversed(range(DEPTH)):
        dx, g = _layer_bwd(dx, ws[l], tabs, svs[l], l)
        per_layer[l] = _layer_grads(g, ws[l], p, l)
    grads = {n: jnp.stack([per_layer[l][n] for l in range(DEPTH)]) for n in per_layer[0]}
    grads['final_g'] = dfinal[0]
    return lossv[0, 0], dx, grads


_ANY = pl.BlockSpec(memory_space=pl.ANY)


def _place():
    x, y, c = lax.axis_index("x"), lax.axis_index("y"), lax.axis_index("c")
    return x, y, c, [(1 - x, y), (x, 1 - y), (1 - x, 1 - y)]


def _allgather4(arrs, *, name):
    n = len(arrs)

    def body(*refs):
        ins, outs = refs[:n], refs[n:2 * n]
        send, recv, loc = refs[2 * n:]
        x, y, c, chips = _place()
        me = 2 * x + y

        def remote(a, k, slot):
            px, py = chips[k]
            return pltpu.make_async_remote_copy(src_ref=ins[a], dst_ref=outs[a].at[slot], send_sem=send.at[a, k],
                                                recv_sem=recv.at[a, k], device_id=(px, py, c), device_id_type=MESH)

        mine = [pltpu.make_async_copy(ins[a], outs[a].at[me], loc.at[a]) for a in range(n)]
        for cp in mine:
            cp.start()
        sends = [remote(a, k, me) for a in range(n) for k in range(3)]
        for cp in sends:
            cp.start()
        for a in range(n):
            for k, (px, py) in enumerate(chips):
                remote(a, k, 2 * px + py).wait_recv()
        for cp in sends:
            cp.wait_send()
        for cp in mine:
            cp.wait()

    return pl.pallas_call(
        body, in_specs=[_ANY] * n, out_specs=[_ANY] * n, out_shape=[SDS((4,) + a.shape, a.dtype) for a in arrs],
        scratch_shapes=[pltpu.SemaphoreType.DMA((n, 3)), pltpu.SemaphoreType.DMA((n, 3)), pltpu.SemaphoreType.DMA((n,))],
        name=name)(*arrs)


def _rs_exchange(parts, *, name):
    n = len(parts)

    def body(*refs):
        ins, owns, gots = refs[:n], refs[n:2 * n], refs[2 * n:3 * n]
        send, recv, loc = refs[3 * n:]
        x, y, c, chips = _place()
        me = 2 * x + y

        def remote(a, k):
            px, py = chips[k]
            return pltpu.make_async_remote_copy(src_ref=ins[a].at[2 * px + py], dst_ref=gots[a].at[k], send_sem=send.at[a, k],
                                                recv_sem=recv.at[a, k], device_id=(px, py, c), device_id_type=MESH)

        mine = [pltpu.make_async_copy(ins[a].at[me], owns[a], loc.at[a]) for a in range(n)]
        for cp in mine:
            cp.start()
        sends = [remote(a, k) for a in range(n) for k in range(3)]
        for cp in sends:
            cp.start()
        for cp in sends:
            cp.wait_recv()
        for cp in sends:
            cp.wait_send()
        for cp in mine:
            cp.wait()

    return pl.pallas_call(
        body, in_specs=[_ANY] * n, out_specs=[_ANY] * (2 * n),
        out_shape=[SDS(a.shape[1:], a.dtype) for a in parts] + [SDS((3,) + a.shape[1:], a.dtype) for a in parts],
        scratch_shapes=[pltpu.SemaphoreType.DMA((n, 3)), pltpu.SemaphoreType.DMA((n, 3)), pltpu.SemaphoreType.DMA((n,))],
        name=name)(*parts)


def _sib_exchange(arrs, *, name):
    n = len(arrs)

    def body(*refs):
        ins, outs = refs[:n], refs[n:2 * n]
        send, recv = refs[2 * n:]
        x, y, c, _ = _place()
        cps = [pltpu.make_async_remote_copy(src_ref=ins[a], dst_ref=outs[a], send_sem=send.at[a], recv_sem=recv.at[a],
                                            device_id=(x, y, 1 - c), device_id_type=MESH) for a in range(n)]
        for cp in cps:
            cp.start()
        for cp in cps:
            cp.wait_recv()
        for cp in cps:
            cp.wait_send()

    return pl.pallas_call(
        body, in_specs=[_ANY] * n, out_specs=[_ANY] * n, out_shape=[SDS(a.shape, a.dtype) for a in arrs],
        scratch_shapes=[pltpu.SemaphoreType.DMA((n,)), pltpu.SemaphoreType.DMA((n,))], name=name)(*arrs)


def _sum4(own, got, *, name):
    R, W = own.shape
    tr = R if R <= 256 else 128
    g3 = lambda k: pl.BlockSpec((None, tr, W), lambda j, i: (k, i, 0))
    up = lambda t: t.astype(f32)
    return _pw(lambda a, b, c, d: ((up(a) + up(b)) + up(c)) + up(d), [own, got, got, got], [_row(tr, W), g3(0), g3(1), g3(2)],
               [SDS((R, W), f32)], [_row(tr, W)], (1, R // tr), name=name)[0]


def _adamw(po, ps, w, m, v, *, name):
    R, W = w.shape
    tr = R if R <= 256 else 128

    def fn(a, b, wv, mv, vv):
        g = a + b
        m2 = ADAM_B1 * mv + (1.0 - ADAM_B1) * g
        v2 = ADAM_B2 * vv + (1.0 - ADAM_B2) * jnp.square(g)
        m_hat = m2 / (1.0 - ADAM_B1 ** ADAM_STEP)
        v_hat = v2 / (1.0 - ADAM_B2 ** ADAM_STEP)
        return g, -ADAM_LR * (m_hat / (jnp.sqrt(v_hat) + ADAM_EPS) + ADAM_WD * wv), m2, v2

    return _pw(fn, [po, ps, w, m, v], [_row(tr, W)] * 5, [SDS((R, W), f32)] * 4, [_row(tr, W)] * 4, (1, R // tr), name=name)


def _to_parts(g, axis):
    shp = g.shape
    g = g.reshape(shp[:axis] + (4, shp[axis] // 4) + shp[axis + 1:])
    return jnp.moveaxis(g, axis, 0)


def _from_parts(pt, axis):
    g = jnp.moveaxis(pt, 0, axis)
    shp = g.shape
    return g.reshape(shp[:axis] + (4 * shp[axis + 1],) + shp[axis + 2:])


def kernel(x, norm1_g, w_in, ret_decay, ret_gn_g, mla_q_norm_g, mla_w_uq, mla_kv_norm_g, mla_w_ukv, s5_a_re, s5_a_im, s5_log_dt, s5_b_re, s5_b_im, s5_c_re, s5_c_im, s5_d, s5_w_glu, w_branch, w_out, norm2_g, ffn_w_gu, ffn_w_down, final_g, loss_target, m_norm1_g, m_w_in, m_ret_decay, m_ret_gn_g, m_mla_q_norm_g, m_mla_w_uq, m_mla_kv_norm_g, m_mla_w_ukv, m_s5_a_re, m_s5_a_im, m_s5_log_dt, m_s5_b_re, m_s5_b_im, m_s5_c_re, m_s5_c_im, m_s5_d, m_s5_w_glu, m_w_branch, m_w_out, m_norm2_g, m_ffn_w_gu, m_ffn_w_down, m_final_g, v_norm1_g, v_w_in, v_ret_decay, v_ret_gn_g, v_mla_q_norm_g, v_mla_w_uq, v_mla_kv_norm_g, v_mla_w_ukv, v_s5_a_re, v_s5_a_im, v_s5_log_dt, v_s5_b_re, v_s5_b_im, v_s5_c_re, v_s5_c_im, v_s5_d, v_s5_w_glu, v_w_branch, v_w_out, v_norm2_g, v_ffn_w_gu, v_ffn_w_down, v_final_g):
    wv = dict(zip(W_NAMES, (norm1_g, w_in, ret_decay, ret_gn_g, mla_q_norm_g, mla_w_uq, mla_kv_norm_g, mla_w_ukv, s5_a_re, s5_a_im,
                            s5_log_dt, s5_b_re, s5_b_im, s5_c_re, s5_c_im, s5_d, s5_w_glu, w_branch, w_out, norm2_g, ffn_w_gu,
                            ffn_w_down, final_g)))
    mv = dict(zip(W_NAMES, (m_norm1_g, m_w_in, m_ret_decay, m_ret_gn_g, m_mla_q_norm_g, m_mla_w_uq, m_mla_kv_norm_g, m_mla_w_ukv,
                            m_s5_a_re, m_s5_a_im, m_s5_log_dt, m_s5_b_re, m_s5_b_im, m_s5_c_re, m_s5_c_im, m_s5_d, m_s5_w_glu,
                            m_w_branch, m_w_out, m_norm2_g, m_ffn_w_gu, m_ffn_w_down, m_final_g)))
    vv = dict(zip(W_NAMES, (v_norm1_g, v_w_in, v_ret_decay, v_ret_gn_g, v_mla_q_norm_g, v_mla_w_uq, v_mla_kv_norm_g, v_mla_w_ukv,
                            v_s5_a_re, v_s5_a_im, v_s5_log_dt, v_s5_b_re, v_s5_b_im, v_s5_c_re, v_s5_c_im, v_s5_d, v_s5_w_glu,
                            v_w_branch, v_w_out, v_norm2_g, v_ffn_w_gu, v_ffn_w_down, v_final_g)))
    big_names = list(BIG)

    gathered = _allgather4([wv[n].astype(bf16) for n in big_names], name="gather_weights")
    big = {n: _from_parts(gt, BIG[n]) for n, gt in zip(big_names, gathered)}
    small = {n: wv[n] for n in SMALL}

    loss_local, dx, grads = _local_step(x[0], loss_target[0], big, small)
    loss = lax.psum(loss_local, ("x", "y", "c"))

    n_rows = {n: -(-math.prod(wv[n].shape) // 1024) * 8 for n in SMALL}
    rows_q = -(-sum(n_rows.values()) // (4 * 128)) * 128

    def as_rows(d):
        blocks = [jnp.pad(d[n].reshape(-1), (0, n_rows[n] * 128 - math.prod(wv[n].shape))).reshape(n_rows[n], 128) for n in SMALL]
        blocks.append(jnp.zeros((4 * rows_q - sum(n_rows.values()), 128), f32))
        return jnp.concatenate(blocks, axis=0)

    parts = [_to_parts(grads[n].astype(bf16), BIG[n]) for n in big_names] + [as_rows(grads).reshape(4, rows_q, 128)]
    got = _rs_exchange(parts, name="grad_exchange")
    n_arr = len(parts)
    two_d = lambda a: a.reshape(-1, a.shape[-1])
    sums = [_sum4(two_d(got[a]), got[n_arr + a].reshape(3, -1, got[a].shape[-1]), name=f"grad_sum4_{a}") for a in range(n_arr)]
    sib = _sib_exchange(sums, name="grad_sibling")

    out_g, out_d, out_m, out_v = {}, {}, {}, {}
    for a, n in enumerate(big_names):
        shp = wv[n].shape
        res = _adamw(sums[a], sib[a], two_d(wv[n]), two_d(mv[n]), two_d(vv[n]), name=f"adamw_{n}")
        out_g[n], out_d[n], out_m[n], out_v[n] = [r.reshape(shp) for r in res]
    g_quarter = _pw(lambda p, q: p + q, [sums[-1], sib[-1]], [_row(rows_q, 128)] * 2, [SDS((rows_q, 128), f32)],
                    [_row(rows_q, 128)], (1, 1), name="small_grad_sum")[0]
    g_small = _allgather4([g_quarter], name="gather_small_grads")[0].reshape(4 * rows_q, 128)
    zero = jnp.zeros_like(g_small)
    res = _adamw(g_small, zero, as_rows(wv), as_rows(mv), as_rows(vv), name="adamw_small")
    off = 0
    for n in SMALL:
        k = math.prod(wv[n].shape)
        for dst, r in zip((out_g, out_d, out_m, out_v), res):
            dst[n] = r[off:off + n_rows[n]].reshape(-1)[:k].reshape(wv[n].shape)
        off += n_rows[n]
    return (loss, dx[None], *[out_g[n] for n in W_NAMES], *[out_d[n] for n in W_NAMES], *[out_m[n] for n in W_NAMES],
            *[out_v[n] for n in W_NAMES])
```

```python
import functools
import math

import jax
import jax.numpy as jnp
from jax import lax
from jax.experimental import pallas as pl
from jax.experimental.pallas import tpu as pltpu

f32 = jnp.float32
bf16 = jnp.bfloat16
SDS = jax.ShapeDtypeStruct
MESH = pl.DeviceIdType.MESH

D = 1024
DEPTH = 2
RMS_EPS = 1e-6
GN_EPS = 1e-5
ROPE_THETA = 10000.0
RET_HEADS = 4
RET_DK = 128
RET_DV = 256
RET_CHUNK = 128
MLA_HEADS = 8
MLA_Q_LORA = 384
MLA_KV_LORA = 256
MLA_NOPE = 128
MLA_ROPE = 64
MLA_V = 128
MLA_QW = 256
S5_G = 64
S5_P = 64
S5_C = 16
S5_NJ = 8
S5_SEG = 8
FFN_H = 2816
ADAM_LR = 0.001
ADAM_B1 = 0.9
ADAM_B2 = 0.999
ADAM_EPS = 1e-08
ADAM_WD = 0.01
ADAM_STEP = 10
VMEM_BIG = 56 * 1024 * 1024

W_NAMES = ['norm1_g', 'w_in', 'ret_decay', 'ret_gn_g', 'mla_q_norm_g', 'mla_w_uq', 'mla_kv_norm_g', 'mla_w_ukv',
           's5_a_re', 's5_a_im', 's5_log_dt', 's5_b_re', 's5_b_im', 's5_c_re', 's5_c_im', 's5_d', 's5_w_glu',
           'w_branch', 'w_out', 'norm2_g', 'ffn_w_gu', 'ffn_w_down', 'final_g']
BIG = {'w_in': 2, 'mla_w_uq': 2, 'mla_w_ukv': 2, 's5_w_glu': 2, 'w_branch': 2, 'w_out': 1, 'ffn_w_gu': 2, 'ffn_w_down': 1}
SMALL = [n for n in W_NAMES if n not in BIG]


def _pick(n, cands=(512, 384, 256, 128)):
    if n <= 1024:
        return n
    for c in cands:
        if n % c == 0:
            return c
    raise ValueError(n)


def _params(sem, vmem=None):
    return pltpu.CompilerParams(dimension_semantics=sem, vmem_limit_bytes=vmem)


def _mm(a, b, *, tb=False, res=None, out_dtype=f32, name):
    M, K = a.shape
    N = b.shape[0] if tb else b.shape[1]
    tn = _pick(N)
    tk = K if K <= 3072 else _pick(K, (1408, 1024, 512))
    nk = K // tk
    tm = _pick(M)
    if M % 1024 == 0 and 1024 * tk * a.dtype.itemsize <= 4 * 1024 * 1024:
        tm = 1024
    assert M % tm == 0 and N % tn == 0 and K % tk == 0

    def body(*refs):
        if res is None:
            a_ref, b_ref, o_ref, acc = refs
        else:
            a_ref, b_ref, r_ref, o_ref, acc = refs
        k = pl.program_id(2)
        dn = (((1,), (1 if tb else 0,)), ((), ()))
        part = lax.dot_general(a_ref[...].astype(bf16), b_ref[...].astype(bf16), dn, preferred_element_type=f32)

        @pl.when(k == 0)
        def _():
            acc[...] = part

        @pl.when(k > 0)
        def _():
            acc[...] += part

        @pl.when(k == nk - 1)
        def _():
            v = acc[...]
            if res is not None:
                v = v + r_ref[...]
            o_ref[...] = v.astype(out_dtype)

    in_specs = [pl.BlockSpec((tm, tk), lambda i, j, k: (i, k)),
                pl.BlockSpec((tn, tk), lambda i, j, k: (j, k)) if tb else pl.BlockSpec((tk, tn), lambda i, j, k: (k, j))]
    args = [a, b]
    if res is not None:
        in_specs.append(pl.BlockSpec((tm, tn), lambda i, j, k: (i, j)))
        args.append(res)
    return pl.pallas_call(
        body, grid=(M // tm, N // tn, nk), in_specs=in_specs,
        out_specs=pl.BlockSpec((tm, tn), lambda i, j, k: (i, j)),
        out_shape=SDS((M, N), out_dtype), scratch_shapes=[pltpu.VMEM((tm, tn), f32)],
        compiler_params=_params(("parallel", "parallel", "arbitrary"), VMEM_BIG), name=name)(*args)


def _mmT(a, b, *, name):
    S, M = a.shape
    N = b.shape[1]
    tm = _pick(M)
    tn = _pick(N)
    tk = min(S, 1024)
    nk = S // tk

    def body(a_ref, b_ref, o_ref):
        k = pl.program_id(2)
        part = lax.dot_general(a_ref[...].astype(bf16), b_ref[...].astype(bf16), (((0,), (0,)), ((), ())),
                               preferred_element_type=f32)

        @pl.when(k == 0)
        def _():
            o_ref[...] = part

        @pl.when(k > 0)
        def _():
            o_ref[...] += part

    return pl.pallas_call(
        body, grid=(M // tm, N // tn, nk),
        in_specs=[pl.BlockSpec((tk, tm), lambda i, j, k: (k, i)), pl.BlockSpec((tk, tn), lambda i, j, k: (k, j))],
        out_specs=pl.BlockSpec((tm, tn), lambda i, j, k: (i, j)),
        out_shape=SDS((M, N), f32),
        compiler_params=_params(("parallel", "parallel", "arbitrary"), VMEM_BIG), name=name)(a, b)


def _pw(fn, ins, in_specs, outs, out_specs, grid, *, n_acc=0, name):
    n_in = len(ins)
    n_out = len(outs)

    def body(*refs):
        vals = fn(*[r[...] for r in refs[:n_in]])
        if not isinstance(vals, (tuple, list)):
            vals = (vals,)
        orefs = refs[n_in:]
        for r, v in zip(orefs[:n_out - n_acc], vals[:n_out - n_acc]):
            r[...] = v.astype(r.dtype)
        if n_acc:
            i = pl.program_id(1)

            @pl.when(i == 0)
            def _():
                for r, v in zip(orefs[n_out - n_acc:], vals[n_out - n_acc:]):
                    r[...] = v.astype(r.dtype)

            @pl.when(i > 0)
            def _():
                for r, v in zip(orefs[n_out - n_acc:], vals[n_out - n_acc:]):
                    r[...] += v.astype(r.dtype)

    res = pl.pallas_call(
        body, grid=grid, in_specs=in_specs, out_specs=out_specs, out_shape=outs,
        compiler_params=_params(("parallel", "arbitrary"), VMEM_BIG), name=name)(*ins)
    return res


def _row(T, w, col=None):
    if col is None:
        return pl.BlockSpec((T, w), lambda j, i: (i, 0))
    return pl.BlockSpec((T, w), lambda j, i: (i, col(j)))


def _par(w, col=None):
    if col is None:
        return pl.BlockSpec((1, w), lambda j, i: (0, 0))
    return pl.BlockSpec((1, w), lambda j, i: (0, col(j)))


def _rms(x, g):
    return x * lax.rsqrt(jnp.mean(x * x, axis=-1, keepdims=True) + RMS_EPS) * g


def _rope(x, cos, sinm, half):
    if half == 64:
        partner = pltpu.roll(x, 64, axis=1)
    else:
        lane = lax.broadcasted_iota(jnp.int32, x.shape, 1)
        partner = jnp.where((lane % (2 * half)) < half, pltpu.roll(x, 128 - half, axis=1), pltpu.roll(x, half, axis=1))
    return x * cos + partner * sinm


def _rope_t(x, cos, sinm, half):
    return _rope(x, cos, -sinm, half)


def _rmsnorm_fwd(x, g, *, name):
    S, W = x.shape
    T = min(S, 512)
    return _pw(lambda xv, gv: _rms(xv, gv), [x, g], [_row(T, W), _par(W)], [SDS((S, W), bf16)], [_row(T, W)],
               (1, S // T), name=name)[0]


def _rmsnorm_bwd(x, g, dh, dres, *, name):
    S, W = x.shape
    T = min(S, 512)

    def fn(xv, gv, dhv, drv):
        _, vjp = jax.vjp(_rms, xv, gv)
        dx, dg = vjp(dhv)
        return dx + drv, dg

    return _pw(fn, [x, g, dh, dres], [_row(T, W), _par(W), _row(T, W), _row(T, W)],
               [SDS((S, W), f32), SDS((1, W), f32)], [_row(T, W), _par(W)], (1, S // T), n_acc=1, name=name)


def _ret_tables(lg, reverse):
    C = RET_CHUNK
    ii = lax.broadcasted_iota(jnp.int32, (C, C), 0).astype(f32)
    jj = lax.broadcasted_iota(jnp.int32, (C, C), 1).astype(f32)
    if not reverse:
        E = ii - jj
        mask = E >= 0
        eq = ii + 1.0
        ek = (C - 1.0) - ii
    else:
        E = jj - ii
        mask = E > 0
        eq = C - ii
        ek = ii
    Dm = jnp.where(mask, jnp.exp(jnp.where(mask, E, 0.0) * lg), 0.0)
    Em = jnp.where(mask, E, 0.0)
    qw = jnp.exp(eq * lg)
    kw = jnp.exp(ek * lg)
    qw2 = jnp.concatenate([qw, qw], axis=1)
    return Dm, Em, eq, ek, qw, kw, qw2, jnp.exp(C * lg)


def _dot(a, b, dims):
    return lax.dot_general(a.astype(bf16), b.astype(bf16), (dims, ((), ())), preferred_element_type=f32)


NN = ((1,), (0,))
NT = ((1,), (1,))
TN = ((0,), (0,))


def _ret_dir_fwd(zr, lg, cos, sinm, *, reverse, name):
    S = zr.shape[0]
    C = RET_CHUNK
    TB = min(S, 512)
    nc = TB // C
    NB = S // TB
    d = 1 if reverse else 0
    scale = RET_DK ** -0.5

    def tb(b):
        return (NB - 1 - b) if reverse else b

    def body(lg_ref, q_ref, k_ref, v_ref, cos_ref, sin_ref, y_ref, st_ref, state):
        h = pl.program_id(0)
        b = pl.program_id(1)

        @pl.when(b == 0)
        def _():
            state[...] = jnp.zeros_like(state)

        Dm, _, _, _, _, kw, qw2, gC = _ret_tables(lg_ref[d, h], reverse)
        order = range(nc - 1, -1, -1) if reverse else range(nc)
        for c in order:
            rows = pl.ds(c * C, C)
            q = _rope(q_ref[rows, :], cos_ref[rows, :], sin_ref[rows, :], 64) * scale
            k = _rope(k_ref[rows, :], cos_ref[rows, :], sin_ref[rows, :], 64)
            v = v_ref[rows, :]
            st = state[...]
            st_ref[0, c] = st
            s = _dot(q, k, NT) * Dm
            o = _dot(s, v, NN) + _dot(q, st, NN) * qw2
            y_ref[rows, :] = o
            state[...] = gC * st + _dot(k * kw, v, TN)

    return pl.pallas_call(
        body, grid=(RET_HEADS, NB),
        in_specs=[pl.BlockSpec(memory_space=pltpu.SMEM),
                  pl.BlockSpec((TB, 128), lambda h, b: (tb(b), h)),
                  pl.BlockSpec((TB, 128), lambda h, b: (tb(b), 4 + h)),
                  pl.BlockSpec((TB, 256), lambda h, b: (tb(b), 4 + h)),
                  pl.BlockSpec((TB, 128), lambda h, b: (tb(b), 0)),
                  pl.BlockSpec((TB, 128), lambda h, b: (tb(b), 0))],
        out_specs=[pl.BlockSpec((TB, 256), lambda h, b: (tb(b), h)),
                   pl.BlockSpec((1, nc, 128, 256), lambda h, b: (h, tb(b), 0, 0))],
        out_shape=[SDS((S, 1024), f32), SDS((RET_HEADS, S // C, 128, 256), f32)],
        scratch_shapes=[pltpu.VMEM((128, 256), f32)],
        compiler_params=_params(("parallel", "arbitrary")), name=name)(lg, zr, zr, zr, cos, sinm)


def _ret_dir_bwd(zr, lg, cos, sinm, dy, states, *, reverse, name):
    S = zr.shape[0]
    C = RET_CHUNK
    TB = min(S, 512)
    nc = TB // C
    NB = S // TB
    d = 1 if reverse else 0
    scale = RET_DK ** -0.5

    def tb(b):
        return b if reverse else (NB - 1 - b)

    def body(lg_ref, q_ref, k_ref, v_ref, cos_ref, sin_ref, dy_ref, st_ref, dq_ref, dk_ref, dv_ref, dlg_ref, dstate):
        h = pl.program_id(0)
        b = pl.program_id(1)

        @pl.when(b == 0)
        def _():
            dstate[...] = jnp.zeros_like(dstate)
            dlg_ref[...] = jnp.zeros_like(dlg_ref)

        Dm, Em, eq, ek, qw, kw, qw2, gC = _ret_tables(lg_ref[d, h], reverse)
        order = range(nc) if reverse else range(nc - 1, -1, -1)
        dlg = jnp.zeros((), f32)
        for c in order:
            rows = pl.ds(c * C, C)
            cs, sn = cos_ref[rows, :], sin_ref[rows, :]
            q = _rope(q_ref[rows, :], cs, sn, 64) * scale
            k = _rope(k_ref[rows, :], cs, sn, 64)
            v = v_ref[rows, :]
            do = dy_ref[rows, :]
            st = st_ref[0, c]
            ds = dstate[...]
            p = _dot(q, k, NT)
            a = p * Dm
            dp = _dot(do, v, NT) * Dm
            dq_cross = _dot(do, st, NT) * qw
            dk_cross = _dot(v, ds, NT) * kw
            dq = _dot(dp, k, NN) + dq_cross
            dk = _dot(dp, q, TN) + dk_cross
            dv = _dot(a, do, TN) + _dot(k * kw, ds, NN)
            dlg = dlg + jnp.sum(dp * p * Em) + jnp.sum(dq_cross * q * eq) + jnp.sum(dk_cross * k * ek) \
                + C * gC * jnp.sum(ds * st)
            dstate[...] = gC * ds + _dot(q * qw, do, TN)
            dq_ref[rows, :] = _rope_t(dq, cs, sn, 64) * scale
            dk_ref[rows, :] = _rope_t(dk, cs, sn, 64)
            dv_ref[rows, :] = dv
        dlg_ref[...] += jnp.full(dlg_ref.shape, dlg, f32)

    return pl.pallas_call(
        body, grid=(RET_HEADS, NB),
        in_specs=[pl.BlockSpec(memory_space=pltpu.SMEM),
                  pl.BlockSpec((TB, 128), lambda h, b: (tb(b), h)),
                  pl.BlockSpec((TB, 128), lambda h, b: (tb(b), 4 + h)),
                  pl.BlockSpec((TB, 256), lambda h, b: (tb(b), 4 + h)),
                  pl.BlockSpec((TB, 128), lambda h, b: (tb(b), 0)),
                  pl.BlockSpec((TB, 128), lambda h, b: (tb(b), 0)),
                  pl.BlockSpec((TB, 256), lambda h, b: (tb(b), h)),
                  pl.BlockSpec((1, nc, 128, 256), lambda h, b: (h, tb(b), 0, 0))],
        out_specs=[pl.BlockSpec((TB, 128), lambda h, b: (tb(b), h)),
                   pl.BlockSpec((TB, 128), lambda h, b: (tb(b), h)),
                   pl.BlockSpec((TB, 256), lambda h, b: (tb(b), h)),
                   pl.BlockSpec((1, 1, 128), lambda h, b: (h, 0, 0))],
        out_shape=[SDS((S, 512), f32), SDS((S, 512), f32), SDS((S, 1024), f32), SDS((RET_HEADS, 1, 128), f32)],
        scratch_shapes=[pltpu.VMEM((128, 256), f32)],
        compiler_params=_params(("parallel", "arbitrary")), name=name)(lg, zr, zr, zr, cos, sinm, dy, states)


def _gn_gate(yf, yb, g, gn):
    y = yf + yb
    mu = jnp.mean(y, axis=-1, keepdims=True)
    var = jnp.mean(jnp.square(y - mu), axis=-1, keepdims=True)
    yn = (y - mu) * lax.rsqrt(var + GN_EPS)
    return jax.nn.silu(g) * (yn * gn)


def _flash_fwd(Q, K, kv, *, name):
    S = Q.shape[0]
    hq = min(S, 512)
    nh = 2 if S % 1024 == 0 else 1
    tq = nh * hq
    tk = min(S, 512)
    nk = S // tk

    def body(q_ref, k_ref, v_ref, o_ref, l_ref, m_s, l_s, acc):
        kk = pl.program_id(2)

        @pl.when(kk == 0)
        def _():
            m_s[...] = jnp.full_like(m_s, -jnp.inf)
            l_s[...] = jnp.zeros_like(l_s)
            acc[...] = jnp.zeros_like(acc)

        k = k_ref[...]
        v = v_ref[...]
        sts = [lax.dot_general(k, q_ref[hf * hq:(hf + 1) * hq, :], (NT, ((), ())), preferred_element_type=f32)
               for hf in range(nh)]
        for hf in range(nh):
            st = sts[hf]
            m_prev = m_s[hf]
            m_new = jnp.maximum(m_prev, jnp.max(st, axis=0, keepdims=True))
            pt = jnp.exp2(st - m_new)
            alpha = jnp.exp2(m_prev - m_new)
            l_s[hf] = alpha * l_s[hf] + jnp.sum(pt, axis=0, keepdims=True)
            acc[hf] = alpha * acc[hf] + lax.dot_general(v, pt.astype(bf16), (TN, ((), ())), preferred_element_type=f32)
            m_s[hf] = m_new

        @pl.when(kk == nk - 1)
        def _():
            for hf in range(nh):
                o_ref[hf * hq:(hf + 1) * hq, :] = jnp.transpose(acc[hf] / l_s[hf]).astype(bf16)
                l_ref[0, :, hf * hq:(hf + 1) * hq] = m_s[hf] + jnp.log2(l_s[hf])

    return pl.pallas_call(
        body, grid=(MLA_HEADS, S // tq, nk),
        in_specs=[pl.BlockSpec((tq, 256), lambda h, i, k: (i, h)),
                  pl.BlockSpec((tk, 256), lambda h, i, k: (k, h)),
                  pl.BlockSpec((tk, 128), lambda h, i, k: (k, 2 * h + 1))],
        out_specs=[pl.BlockSpec((tq, 128), lambda h, i, k: (i, h)), pl.BlockSpec((1, 1, tq), lambda h, i, k: (h, 0, i))],
        out_shape=[SDS((S, 1024), bf16), SDS((MLA_HEADS, 1, S), f32)],
        scratch_shapes=[pltpu.VMEM((nh, 1, hq), f32), pltpu.VMEM((nh, 1, hq), f32), pltpu.VMEM((nh, 128, hq), f32)],
        compiler_params=_params(("parallel", "parallel", "arbitrary")), name=name)(Q, K, kv)


def _flash_bwd(Q, K, kv, O, L, dO, *, name):
    S = Q.shape[0]
    hq = min(S, 512)
    nh = 2 if S % 1024 == 0 else 1
    tq = nh * hq
    tk = min(S, 512)
    nq = S // tq
    ln2 = math.log(2.0)

    def body(q_ref, k_ref, v_ref, o_ref, l_ref, do_ref, dq_ref, dk_ref, dv_ref, dk_acc, dv_acc):
        kk = pl.program_id(1)
        i = pl.program_id(2)

        @pl.when((kk == 0) & (i == 0))
        def _():
            dq_ref[...] = jnp.zeros_like(dq_ref)

        @pl.when(i == 0)
        def _():
            dk_acc[...] = jnp.zeros_like(dk_acc)
            dv_acc[...] = jnp.zeros_like(dv_acc)

        k = k_ref[...]
        v = v_ref[...]
        ones = jnp.ones((8, 128), f32)
        dk_new = dk_acc[...]
        dv_new = dv_acc[...]
        for hf in range(nh):
            sl = slice(hf * hq, (hf + 1) * hq)
            q = q_ref[sl, :]
            do = do_ref[sl, :]
            st = lax.dot_general(k, q, (NT, ((), ())), preferred_element_type=f32)
            pt = jnp.exp2(st - l_ref[0, :, sl])
            delta = lax.dot_general(ones, do * o_ref[sl, :].astype(f32), (NT, ((), ())),
                                    preferred_element_type=f32, precision=lax.Precision.HIGHEST)[0:1, :]
            dob = do.astype(bf16)
            dv_new = dv_new + lax.dot_general(pt.astype(bf16), dob, (NN, ((), ())), preferred_element_type=f32)
            dpt = lax.dot_general(v, dob, (NT, ((), ())), preferred_element_type=f32)
            dst = (pt * (dpt - delta)).astype(bf16)
            dk_new = dk_new + lax.dot_general(dst, q, (NN, ((), ())), preferred_element_type=f32)
            rows = pl.ds(pl.multiple_of(i * tq + hf * hq, hq), hq)
            dq_ref[rows, :] += lax.dot_general(dst, k, (TN, ((), ())), preferred_element_type=f32)
        dk_acc[...] = dk_new
        dv_acc[...] = dv_new

        @pl.when(i == nq - 1)
        def _():
            dk_ref[...] = dk_acc[...] * ln2
            dv_ref[...] = dv_acc[...]

    return pl.pallas_call(
        body, grid=(MLA_HEADS, S // tk, nq),
        in_specs=[pl.BlockSpec((tq, 256), lambda h, k, i: (i, h)),
                  pl.BlockSpec((tk, 256), lambda h, k, i: (k, h)),
                  pl.BlockSpec((tk, 128), lambda h, k, i: (k, 2 * h + 1)),
                  pl.BlockSpec((tq, 128), lambda h, k, i: (i, h)),
                  pl.BlockSpec((1, 1, tq), lambda h, k, i: (h, 0, i)),
                  pl.BlockSpec((tq, 128), lambda h, k, i: (i, h))],
        out_specs=[pl.BlockSpec((S, 256), lambda h, k, i: (0, h)),
                   pl.BlockSpec((tk, 256), lambda h, k, i: (k, h)),
                   pl.BlockSpec((tk, 128), lambda h, k, i: (k, h))],
        out_shape=[SDS((S, 2048), f32), SDS((S, 2048), f32), SDS((S, 1024), f32)],
        scratch_shapes=[pltpu.VMEM((tk, 256), f32), pltpu.VMEM((tk, 128), f32)],
        compiler_params=_params(("parallel", "arbitrary", "arbitrary"), VMEM_BIG), name=name)(Q, K, kv, O, L, dO)


def _mla_bwd_prep(dQ, dK, dV, cosm, sinm, *, name):
    S = dQ.shape[0]
    T = min(S, 256)
    scale = (MLA_NOPE + MLA_ROPE) ** -0.5

    def body(dq_ref, dk_ref, dv_ref, cos_ref, sin_ref, oq_ref, okv_ref, okr_ref):
        cs, sn = cos_ref[...], sin_ref[...]
        kr = jnp.zeros((T, 128), f32)
        for h in range(MLA_HEADS):
            a = 256 * h
            oq_ref[:, a:a + 128] = (dq_ref[:, a:a + 128] * scale).astype(bf16)
            oq_ref[:, a + 128:a + 256] = (_rope_t(dq_ref[:, a + 128:a + 256], cs, sn, 32) * scale).astype(bf16)
            okv_ref[:, a:a + 128] = dk_ref[:, a:a + 128].astype(bf16)
            okv_ref[:, a + 128:a + 256] = dv_ref[:, 128 * h:128 * h + 128].astype(bf16)
            kr = kr + dk_ref[:, a + 128:a + 256]
        okr_ref[...] = _rope_t(kr, cs, sn, 32)

    return pl.pallas_call(
        body, grid=(S // T,),
        in_specs=[pl.BlockSpec((T, 2048), lambda i: (i, 0)), pl.BlockSpec((T, 2048), lambda i: (i, 0)),
                  pl.BlockSpec((T, 1024), lambda i: (i, 0)), pl.BlockSpec((T, 128), lambda i: (i, 0)),
                  pl.BlockSpec((T, 128), lambda i: (i, 0))],
        out_specs=[pl.BlockSpec((T, 2048), lambda i: (i, 0)), pl.BlockSpec((T, 2048), lambda i: (i, 0)),
                   pl.BlockSpec((T, 128), lambda i: (i, 0))],
        out_shape=[SDS((S, 2048), bf16), SDS((S, 2048), bf16), SDS((S, 128), f32)],
        compiler_params=_params(("parallel",), VMEM_BIG), name=name)(dQ, dK, dV, cosm, sinm)


def _mla_norm_bwd(zm, qg, kvg, dcqn, dckvn, dkr, *, name):
    S = zm.shape[0]
    T = min(S, 512)

    def body(cq_ref, ckv_ref, qg_ref, kvg_ref, dcq_ref, dckv_ref, dkr_ref, o_ref, dqg_ref, dkvg_ref):
        i = pl.program_id(0)
        _, vjp = jax.vjp(_rms, cq_ref[...], qg_ref[...])
        dcq, dqg = vjp(dcq_ref[...])
        _, vjp2 = jax.vjp(_rms, ckv_ref[...], kvg_ref[...])
        dckv, dkvg = vjp2(dckv_ref[...])
        o_ref[:, 0:384] = dcq.astype(bf16)
        o_ref[:, 384:512] = jnp.zeros((T, 128), bf16)
        o_ref[:, 512:768] = dckv.astype(bf16)
        o_ref[:, 768:896] = dkr_ref[...].astype(bf16)

        @pl.when(i == 0)
        def _():
            dqg_ref[...] = dqg
            dkvg_ref[...] = dkvg

        @pl.when(i > 0)
        def _():
            dqg_ref[...] += dqg
            dkvg_ref[...] += dkvg

    return pl.pallas_call(
        body, grid=(S // T,),
        in_specs=[pl.BlockSpec((T, 384), lambda i: (i, 0)), pl.BlockSpec((T, 256), lambda i: (i, 2)),
                  pl.BlockSpec((1, 384), lambda i: (0, 0)), pl.BlockSpec((1, 256), lambda i: (0, 0)),
                  pl.BlockSpec((T, 384), lambda i: (i, 0)), pl.BlockSpec((T, 256), lambda i: (i, 0)),
                  pl.BlockSpec((T, 128), lambda i: (i, 0))],
        out_specs=[pl.BlockSpec((T, 896), lambda i: (i, 0)), pl.BlockSpec((1, 384), lambda i: (0, 0)),
                   pl.BlockSpec((1, 256), lambda i: (0, 0))],
        out_shape=[SDS((S, 896), bf16), SDS((1, 384), f32), SDS((1, 256), f32)],
        compiler_params=_params(("arbitrary",)), name=name)(zm, zm, qg, kvg, dcqn, dckvn, dkr)


def _s5_disc(a_re, a_im, ldt, b_re, b_im):
    dt = jnp.exp(ldt)
    ar = jnp.minimum(a_re, -1e-4)
    mag = jnp.exp(dt * ar)
    abr = mag * jnp.cos(dt * a_im)
    abi = mag * jnp.sin(dt * a_im)
    den = ar * ar + a_im * a_im
    nr = abr - 1.0
    ni = abi
    cr = (nr * ar + ni * a_im) / den
    ci = (ni * ar - nr * a_im) / den
    return abr, abi, cr * b_re - ci * b_im, cr * b_im + ci * b_re


def _s5_param_fwd(a_re, a_im, ldt, b_re, b_im, *, name):
    R = SDS((1, 8192), f32)
    M = SDS((16, 8192), f32)
    Pw = SDS((64, 8192), f32)

    def body(a_re_r, a_im_r, ldt_r, b_re_r, b_im_r, o1, o2, o3, o4, p_re, p_im):
        abr, abi, bbr, bbi = _s5_disc(a_re_r[...], a_im_r[...], ldt_r[...], b_re_r[...], b_im_r[...])
        o1[...] = abr
        o2[...] = abi
        o3[...] = bbr
        o4[...] = bbi
        dt = jnp.exp(ldt_r[...])
        ar = jnp.minimum(a_re_r[...], -1e-4)
        n = lax.broadcasted_iota(jnp.int32, (64, 8192), 0).astype(f32) + 1.0
        mag = jnp.exp(n * (dt * ar))
        ang = n * (dt * a_im_r[...])
        p_re[...] = mag * jnp.cos(ang)
        p_im[...] = mag * jnp.sin(ang)

    return pl.pallas_call(body, out_shape=[R, R, M, M, Pw, Pw], name=name)(a_re, a_im, ldt, b_re, b_im)


def _s5_param_bwd(a_re, a_im, ldt, b_re, b_im, d_abr, d_abi, d_bbr, d_bbi, *, name):
    R = SDS((1, 8192), f32)
    M = SDS((16, 8192), f32)

    def body(a_re_r, a_im_r, ldt_r, b_re_r, b_im_r, c1, c2, c3, c4, o1, o2, o3, o4, o5):
        _, vjp = jax.vjp(_s5_disc, a_re_r[...], a_im_r[...], ldt_r[...], b_re_r[...], b_im_r[...])
        g = vjp((c1[...], c2[...], c3[...], c4[...]))
        for o, v in zip((o1, o2, o3, o4, o5), g):
            o[...] = v

    return pl.pallas_call(body, out_shape=[R, R, R, M, M], name=name)(a_re, a_im, ldt, b_re, b_im, d_abr, d_abi, d_bbr, d_bbi)


def _scan_core(sc, ar, ai, pwr_ref, pwi_ref, a64r, a64i, carry, *, reverse, T):
    L = T // S5_SEG
    arb = [jnp.broadcast_to(ar[:, 128 * k:128 * k + 128], (8, 128)) for k in range(4)]
    aib = [jnp.broadcast_to(ai[:, 128 * k:128 * k + 128], (8, 128)) for k in range(4)]

    UN = 4

    def step(r4, c):
        c = list(c)
        for u in range(UN):
            r0 = r4 * UN + u
            r = (L - 1 - r0) if reverse else r0
            for k in range(4):
                idx = pl.ds(r, S5_SEG, stride=L)
                br = sc[k, idx, :]
                bi = sc[4 + k, idx, :]
                cr, ci = c[k], c[4 + k]
                nr = arb[k] * cr - aib[k] * ci + br
                ni = arb[k] * ci + aib[k] * cr + bi
                sc[k, idx, :] = nr
                sc[4 + k, idx, :] = ni
                c[k], c[4 + k] = nr, ni
        return tuple(c)

    c = lax.fori_loop(0, L // UN, step, tuple(jnp.zeros((8, 128), f32) for _ in range(8)))
    segs = range(S5_SEG - 1, -1, -1) if reverse else range(S5_SEG)
    for k in range(4):
        lr, li = c[k], c[4 + k]
        cr = carry[k, 0:1, :]
        ci = carry[4 + k, 0:1, :]
        a6r = a64r[:, 128 * k:128 * k + 128]
        a6i = a64i[:, 128 * k:128 * k + 128]
        pr = pwr_ref[:, 128 * k:128 * k + 128]
        pi = pwi_ref[:, 128 * k:128 * k + 128]
        for seg in segs:
            sl = pl.ds(seg * L, L)
            sc[k, sl, :] += pr * cr - pi * ci
            sc[4 + k, sl, :] += pr * ci + pi * cr
            ncr = lr[seg:seg + 1, :] + a6r * cr - a6i * ci
            nci = li[seg:seg + 1, :] + a6r * ci + a6i * cr
            cr, ci = ncr, nci
        carry[k, 0:1, :] = cr
        carry[4 + k, 0:1, :] = ci


def _s5_scan_fwd(u, BBr, BBi, CCr, CCi, abr, abi, pwr, pwi, *, reverse, name):
    S = u.shape[0]
    T = min(S, 512)
    NB = S // T
    L = T // S5_SEG
    d = 1 if reverse else 0

    def tb(b):
        return (NB - 1 - b) if reverse else b

    def body(u_ref, bbr_ref, bbi_ref, ccr_ref, cci_ref, ar_ref, ai_ref, pwr_ref, pwi_ref, y_ref, xr_ref, xi_ref, sc, carry):
        b = pl.program_id(1)

        @pl.when(b == 0)
        def _():
            carry[...] = jnp.zeros_like(carry)

        ub = u_ref[...].astype(bf16)
        bur = lax.dot_general(ub, bbr_ref[0, 0], (NN, ((), ())), preferred_element_type=f32)
        bui = lax.dot_general(ub, bbi_ref[0, 0], (NN, ((), ())), preferred_element_type=f32)
        for k in range(4):
            sc[k] = bur[:, 128 * k:128 * k + 128]
            sc[4 + k] = bui[:, 128 * k:128 * k + 128]
        a6 = (0 if reverse else L - 1)
        _scan_core(sc, ar_ref[...], ai_ref[...], pwr_ref, pwi_ref, pwr_ref[a6:a6 + 1, :], pwi_ref[a6:a6 + 1, :], carry,
                   reverse=reverse, T=T)
        for k in range(4):
            xr_ref[:, 128 * k:128 * k + 128] = sc[k]
            xi_ref[:, 128 * k:128 * k + 128] = sc[4 + k]
        y_ref[...] = _dot(xr_ref[...], ccr_ref[0, 0], NN) - _dot(xi_ref[...], cci_ref[0, 0], NN)

    mat = lambda shp: pl.BlockSpec((1, 1) + shp, lambda j, b: (d, j, 0, 0))
    vec = lambda r: pl.BlockSpec((r, 512), lambda j, b: (0, d * S5_NJ + j))
    return pl.pallas_call(
        body, grid=(S5_NJ, NB),
        in_specs=[pl.BlockSpec((T, 128), lambda j, b: (tb(b), j)), mat((128, 512)), mat((128, 512)), mat((512, 128)),
                  mat((512, 128)), vec(1), vec(1), vec(L), vec(L)],
        out_specs=[pl.BlockSpec((T, 128), lambda j, b: (tb(b), j)), pl.BlockSpec((T, 512), lambda j, b: (tb(b), j)),
                   pl.BlockSpec((T, 512), lambda j, b: (tb(b), j))],
        out_shape=[SDS((S, 1024), f32), SDS((S, 4096), f32), SDS((S, 4096), f32)],
        scratch_shapes=[pltpu.VMEM((8, T, 128), f32), pltpu.VMEM((8, 8, 128), f32)],
        compiler_params=_params(("parallel", "arbitrary")), name=name)(u, BBr, BBi, CCr, CCi, abr, abi, pwr, pwi)


def _s5_scan_bwd(u, dy, xr, xi, BBr, BBi, CCr, CCi, abr, abi, pwr, pwi, *, reverse, name):
    S = u.shape[0]
    T = min(S, 512)
    NB = S // T
    L = T // S5_SEG
    d = 1 if reverse else 0
    adj_rev = not reverse

    def tb(b):
        return b if reverse else (NB - 1 - b)

    def bnd(b):
        t = tb(b)
        if reverse:
            return jnp.minimum((t + 1) * (T // 8), S // 8 - 1)
        return jnp.maximum(t * (T // 8) - 1, 0)

    def body(u_ref, dy_ref, xr_ref, xi_ref, xbr_ref, xbi_ref, bbr_ref, bbi_ref, ccr_ref, cci_ref, ar_ref, ai_ref,
             pwr_ref, pwi_ref, du_ref, dbbr_ref, dbbi_ref, dccr_ref, dcci_ref, dar_ref, dai_ref, sc, carry, lam):
        b = pl.program_id(1)

        @pl.when(b == 0)
        def _():
            carry[...] = jnp.zeros_like(carry)
            for r in (dbbr_ref, dbbi_ref, dccr_ref, dcci_ref, dar_ref, dai_ref):
                r[...] = jnp.zeros_like(r)

        dyb = dy_ref[...]
        gxr = lax.dot_general(dyb, ccr_ref[0, 0], (NT, ((), ())), preferred_element_type=f32)
        gxi = -lax.dot_general(dyb, cci_ref[0, 0], (NT, ((), ())), preferred_element_type=f32)
        for k in range(4):
            sc[k] = gxr[:, 128 * k:128 * k + 128]
            sc[4 + k] = gxi[:, 128 * k:128 * k + 128]
        a6 = (0 if adj_rev else L - 1)
        nai = -ai_ref[...]
        _scan_core(sc, ar_ref[...], nai, pwr_ref, _Neg(pwi_ref), pwr_ref[a6:a6 + 1, :], -pwi_ref[a6:a6 + 1, :], carry,
                   reverse=adj_rev, T=T)
        ub = u_ref[...].astype(bf16)
        first = (b == NB - 1)
        for k in range(4):
            lam[0, :, 128 * k:128 * k + 128] = sc[k]
            lam[1, :, 128 * k:128 * k + 128] = sc[4 + k]
        lr = lam[0]
        li = lam[1]
        lrb = lr.astype(bf16)
        lib = li.astype(bf16)
        du_ref[...] = lax.dot_general(lrb, bbr_ref[0, 0], (NT, ((), ())), preferred_element_type=f32) \
            + lax.dot_general(lib, bbi_ref[0, 0], (NT, ((), ())), preferred_element_type=f32)
        dbbr_ref[0, 0] += lax.dot_general(ub, lrb, (TN, ((), ())), preferred_element_type=f32)
        dbbi_ref[0, 0] += lax.dot_general(ub, lib, (TN, ((), ())), preferred_element_type=f32)
        xr = xr_ref[...]
        xi = xi_ref[...]
        dccr_ref[0, 0] += lax.dot_general(xr.astype(bf16), dyb, (TN, ((), ())), preferred_element_type=f32)
        dcci_ref[0, 0] -= lax.dot_general(xi.astype(bf16), dyb, (TN, ((), ())), preferred_element_type=f32)
        row = lax.broadcasted_iota(jnp.int32, (T, 512), 0)
        if reverse:
            edge_r = jnp.where(first, 0.0, xbr_ref[0:1, :])
            edge_i = jnp.where(first, 0.0, xbi_ref[0:1, :])
            xpr = jnp.where(row == T - 1, edge_r, pltpu.roll(xr, T - 1, axis=0))
            xpi = jnp.where(row == T - 1, edge_i, pltpu.roll(xi, T - 1, axis=0))
        else:
            edge_r = jnp.where(first, 0.0, xbr_ref[7:8, :])
            edge_i = jnp.where(first, 0.0, xbi_ref[7:8, :])
            xpr = jnp.where(row == 0, edge_r, pltpu.roll(xr, 1, axis=0))
            xpi = jnp.where(row == 0, edge_i, pltpu.roll(xi, 1, axis=0))
        dar_ref[...] += jnp.sum(xpr * lr + xpi * li, axis=0, keepdims=True)
        dai_ref[...] += jnp.sum(xpr * li - xpi * lr, axis=0, keepdims=True)

    mat = lambda shp: pl.BlockSpec((1, 1) + shp, lambda j, b: (d, j, 0, 0))
    omat = lambda shp: pl.BlockSpec((1, 1) + shp, lambda j, b: (0, j, 0, 0))
    vec = lambda r: pl.BlockSpec((r, 512), lambda j, b: (0, d * S5_NJ + j))
    blk = lambda w: pl.BlockSpec((T, w), lambda j, b: (tb(b), j))
    return pl.pallas_call(
        body, grid=(S5_NJ, NB),
        in_specs=[blk(128), blk(128), blk(512), blk(512),
                  pl.BlockSpec((8, 512), lambda j, b: (bnd(b), j)), pl.BlockSpec((8, 512), lambda j, b: (bnd(b), j)),
                  mat((128, 512)), mat((128, 512)), mat((512, 128)), mat((512, 128)), vec(1), vec(1), vec(L), vec(L)],
        out_specs=[blk(128), omat((128, 512)), omat((128, 512)), omat((512, 128)), omat((512, 128)),
                   pl.BlockSpec((1, 512), lambda j, b: (0, j)), pl.BlockSpec((1, 512), lambda j, b: (0, j))],
        out_shape=[SDS((S, 1024), f32), SDS((1, 8, 128, 512), f32), SDS((1, 8, 128, 512), f32), SDS((1, 8, 512, 128), f32),
                   SDS((1, 8, 512, 128), f32), SDS((1, 4096), f32), SDS((1, 4096), f32)],
        scratch_shapes=[pltpu.VMEM((8, T, 128), f32), pltpu.VMEM((8, 8, 128), f32), pltpu.VMEM((2, T, 512), f32)],
        compiler_params=_params(("parallel", "arbitrary"), VMEM_BIG), name=name)(
            u, dy, xr, xi, xr, xi, BBr, BBi, CCr, CCi, abr, abi, pwr, pwi)


class _Neg:
    def __init__(self, ref):
        self.ref = ref

    def __getitem__(self, idx):
        return -self.ref[idx]


def _silu_mul(g, u):
    return jax.nn.silu(g) * u


def _mixf(p0, p1, p2, z0, z1, z2):
    return jax.nn.sigmoid(z0) * p0 + jax.nn.sigmoid(z1) * p1 + jax.nn.sigmoid(z2) * p2


def _s5_act(yf, yb, u, dd):
    return jax.nn.gelu(yf + yb + dd * u)


def _glu(a, b):
    return a * jax.nn.sigmoid(b)


def _layer_fwd(x, w, tabs, l):
    S = x.shape[0]
    T = min(S, 512)
    I = S // T
    nm = lambda s: f"L{l}_{s}"
    sv = {'x': x}
    h = _rmsnorm_fwd(x, w['norm1_g'], name=nm("norm1"))
    zr = _mm(h, w['W_ret'], name=nm("in_ret"))
    zm = _mm(h, w['W_mla'], name=nm("in_mla"))
    zs = _mm(h, w['W_s5'], name=nm("in_s5"))
    zg = _mm(h, w['W_gate'], name=nm("in_gate"))
    sv.update(h=h, zr=zr, zm=zm, zs=zs, zg=zg)

    yf, stf = _ret_dir_fwd(zr, w['lg'], tabs['cos_r'], tabs['sin_r'], reverse=False, name=nm("ret_f"))
    yb, stb = _ret_dir_fwd(zr, w['lg'], tabs['cos_r'], tabs['sin_r'], reverse=True, name=nm("ret_b"))
    hd = lambda j: j
    y_ret = _pw(_gn_gate, [yf, yb, zr, w['ret_gn_g']],
                [_row(T, 256, hd), _row(T, 256, hd), _row(T, 256, lambda j: 8 + j), _par(256, hd)],
                [SDS((S, 1024), bf16)], [_row(T, 256, hd)], (RET_HEADS, I), name=nm("ret_gn"))[0]
    sv.update(yf=yf, yb=yb, stf=stf, stb=stb, y_ret=y_ret)

    cqn, ckvn = _pw(lambda a, b, g1, g2: (_rms(a, g1), _rms(b, g2)), [zm, zm, w['mla_q_norm_g'], w['mla_kv_norm_g']],
                    [_row(T, 384), _row(T, 256, lambda j: 2), _par(384), _par(256)],
                    [SDS((S, 384), bf16), SDS((S, 256), bf16)], [_row(T, 384), _row(T, 256)], (1, I), name=nm("mla_norm"))
    q = _mm(cqn, w['W_uq'], name=nm("mla_uq"))
    kv = _mm(ckvn, w['W_ukv'], out_dtype=bf16, name=nm("mla_ukv"))
    sc = (MLA_NOPE + MLA_ROPE) ** -0.5 * math.log2(math.e)
    Q = _pw(lambda xq, cs, sn: jnp.concatenate([xq[:, :128] * sc, _rope(xq[:, 128:], cs, sn, 32) * sc], axis=1),
            [q, tabs['cos_m'], tabs['sin_m']], [_row(T, 256, hd), _row(T, 128), _row(T, 128)],
            [SDS((S, 2048), bf16)], [_row(T, 256, hd)], (MLA_HEADS, I), name=nm("mla_qprep"))[0]
    K = _pw(lambda kn, kr, cs, sn: jnp.concatenate([kn.astype(f32), _rope(kr, cs, sn, 32)], axis=1),
            [kv, zm, tabs['cos_m'], tabs['sin_m']],
            [_row(T, 128, lambda j: 2 * j), _row(T, 128, lambda j: 6), _row(T, 128), _row(T, 128)],
            [SDS((S, 2048), bf16)], [_row(T, 256, hd)], (MLA_HEADS, I), name=nm("mla_kprep"))[0]
    O, Lse = _flash_fwd(Q, K, kv, name=nm("mla_attn"))
    sv.update(cqn=cqn, ckvn=ckvn, kv=kv, Q=Q, K=K, O=O, Lse=Lse)

    s5 = w['s5']
    ysf, xrf, xif = _s5_scan_fwd(zs, s5['BBr'], s5['BBi'], s5['CCr'], s5['CCi'], s5['abr'], s5['abi'], s5['pwr_f'], s5['pwi_f'],
                                 reverse=False, name=nm("s5_f"))
    ysb, xrb, xib = _s5_scan_fwd(zs, s5['BBr'], s5['BBi'], s5['CCr'], s5['CCi'], s5['abr'], s5['abi'], s5['pwr_f'], s5['pwi_f'],
                                 reverse=True, name=nm("s5_b"))
    gact = _pw(_s5_act, [ysf, ysb, zs, w['s5_d']], [_row(T, D), _row(T, D), _row(T, D), _par(D)],
               [SDS((S, D), bf16)], [_row(T, D)], (1, I), name=nm("s5_act"))[0]
    gg = _mm(gact, w['W_glu'], name=nm("s5_glu_mm"))
    y_s5 = _pw(_glu, [gg, gg], [_row(T, D), _row(T, D, lambda j: 1)], [SDS((S, D), bf16)], [_row(T, D)], (1, I),
               name=nm("s5_glu"))[0]
    sv.update(ysf=ysf, ysb=ysb, xrf=xrf, xif=xif, xrb=xrb, xib=xib, gact=gact, gg=gg, y_s5=y_s5)

    ys = [y_ret, O, y_s5]
    pr = [_mm(ys[i], w['W_br'][i], name=nm(f"branch{i}")) for i in range(3)]
    mix = _pw(_mixf, pr + [zg, zg, zg],
              [_row(T, D)] * 3 + [_row(T, D), _row(T, D, lambda j: 1), _row(T, D, lambda j: 2)],
              [SDS((S, D), bf16)], [_row(T, D)], (1, I), name=nm("mix"))[0]
    x1 = _mm(mix, w['W_out'], res=x, name=nm("out_proj"))
    h2 = _rmsnorm_fwd(x1, w['norm2_g'], name=nm("norm2"))
    fgu = _mm(h2, w['W_gu'], name=nm("ffn_gu"))
    act = _pw(_silu_mul, [fgu, fgu], [_row(T, 1408, lambda j: j), _row(T, 1408, lambda j: 2 + j)],
              [SDS((S, FFN_H), bf16)], [_row(T, 1408, lambda j: j)], (2, I), name=nm("ffn_act"))[0]
    x2 = _mm(act, w['W_down'], res=x1, name=nm("ffn_down"))
    sv.update(pr=pr, mix=mix, x1=x1, h2=h2, fgu=fgu, act=act)
    return x2, sv


def _vjp_fn(fn, n_primal, cast=None):
    def g(*args):
        _, vjp = jax.vjp(fn, *args[:n_primal])
        return vjp(args[n_primal].astype(f32))
    return g


def _layer_bwd(dx2, w, tabs, sv, l):
    S = dx2.shape[0]
    T = min(S, 512)
    I = S // T
    nm = lambda s: f"L{l}_b_{s}"
    g = {}
    hd = lambda j: j

    dact = _mm(dx2, w['W_down'], tb=True, name=nm("ffn_down_dx"))
    g['W_down'] = _mmT(sv['act'], dx2, name=nm("ffn_down_dw"))
    dfg, dfu = _pw(_vjp_fn(_silu_mul, 2), [sv['fgu'], sv['fgu'], dact],
                   [_row(T, 1408, lambda j: j), _row(T, 1408, lambda j: 2 + j), _row(T, 1408, lambda j: j)],
                   [SDS((S, FFN_H), bf16), SDS((S, FFN_H), bf16)], [_row(T, 1408, lambda j: j)] * 2, (2, I), name=nm("ffn_act"))
    dfgu = jnp.concatenate([dfg, dfu], axis=1)
    g['W_gu'] = _mmT(sv['h2'], dfgu, name=nm("ffn_gu_dw"))
    dh2 = _mm(dfgu, w['W_gu'], tb=True, name=nm("ffn_gu_dx"))
    dx1, g['norm2_g'] = _rmsnorm_bwd(sv['x1'], w['norm2_g'], dh2, dx2, name=nm("norm2"))

    dmix = _mm(dx1, w['W_out'], tb=True, name=nm("out_dx"))
    g['W_out'] = _mmT(sv['mix'], dx1, name=nm("out_dw"))
    zg = sv['zg']
    outs = _pw(_vjp_fn(_mixf, 6), sv['pr'] + [zg, zg, zg, dmix],
               [_row(T, D)] * 3 + [_row(T, D), _row(T, D, lambda j: 1), _row(T, D, lambda j: 2), _row(T, D)],
               [SDS((S, D), bf16)] * 6, [_row(T, D)] * 6, (1, I), name=nm("mix"))
    dpr, dzg = outs[:3], jnp.concatenate(outs[3:], axis=1)
    ys = [sv['y_ret'], sv['O'], sv['y_s5']]
    g['W_br'] = [_mmT(ys[i], dpr[i], name=nm(f"branch{i}_dw")) for i in range(3)]
    dys = [_mm(dpr[i], w['W_br'][i], tb=True, name=nm(f"branch{i}_dx")) for i in range(3)]

    gg = sv['gg']
    dga, dgb = _pw(_vjp_fn(_glu, 2), [gg, gg, dys[2]], [_row(T, D), _row(T, D, lambda j: 1), _row(T, D)],
                   [SDS((S, D), bf16)] * 2, [_row(T, D)] * 2, (1, I), name=nm("s5_glu"))
    dgg = jnp.concatenate([dga, dgb], axis=1)
    g['W_glu'] = _mmT(sv['gact'], dgg, name=nm("s5_glu_dw"))
    dgact = _mm(dgg, w['W_glu'], tb=True, name=nm("s5_glu_dx"))

    def act_bwd(yf, yb, u, dd, ct):
        _, vjp = jax.vjp(_s5_act, yf, yb, u, dd)
        dyf, _, du, ddd = vjp(ct)
        return dyf, du, ddd

    dys5, du_direct, g['s5_d'] = _pw(act_bwd, [sv['ysf'], sv['ysb'], sv['zs'], w['s5_d'], dgact],
                                     [_row(T, D)] * 3 + [_par(D), _row(T, D)],
                                     [SDS((S, D), bf16), SDS((S, D), f32), SDS((1, D), f32)],
                                     [_row(T, D), _row(T, D), _par(D)], (1, I), n_acc=1, name=nm("s5_act"))
    s5 = w['s5']
    rf = _s5_scan_bwd(sv['zs'], dys5, sv['xrf'], sv['xif'], s5['BBr'], s5['BBi'], s5['CCr'], s5['CCi'], s5['abr'], s5['abi'],
                      s5['pwr_a'], s5['pwi_a'], reverse=False, name=nm("s5_f"))
    rb = _s5_scan_bwd(sv['zs'], dys5, sv['xrb'], sv['xib'], s5['BBr'], s5['BBi'], s5['CCr'], s5['CCi'], s5['abr'], s5['abi'],
                      s5['pwr_a'], s5['pwi_a'], reverse=True, name=nm("s5_b"))
    g['s5'] = (rf[1:], rb[1:])
    dzs = _pw(lambda a, b, c: a + b + c, [du_direct, rf[0], rb[0]], [_row(T, D)] * 3, [SDS((S, D), bf16)], [_row(T, D)],
              (1, I), name=nm("s5_du"))[0]

    dQ, dK, dV = _flash_bwd(sv['Q'], sv['K'], sv['kv'], sv['O'], sv['Lse'], dys[1], name=nm("mla_attn"))
    dq_lin, dkv, dkr = _mla_bwd_prep(dQ, dK, dV, tabs['cos_m'], tabs['sin_m'], name=nm("mla_prep"))
    g['W_uq'] = _mmT(sv['cqn'], dq_lin, name=nm("mla_uq_dw"))
    dcqn = _mm(dq_lin, w['W_uq'], tb=True, name=nm("mla_uq_dx"))
    g['W_ukv'] = _mmT(sv['ckvn'], dkv, name=nm("mla_ukv_dw"))
    dckvn = _mm(dkv, w['W_ukv'], tb=True, name=nm("mla_ukv_dx"))
    dzm, g['mla_q_norm_g'], g['mla_kv_norm_g'] = _mla_norm_bwd(sv['zm'], w['mla_q_norm_g'], w['mla_kv_norm_g'], dcqn, dckvn, dkr,
                                                               name=nm("mla_norm"))

    zr = sv['zr']

    def gn_bwd(yf, yb, gt, gn, ct):
        _, vjp = jax.vjp(_gn_gate, yf, yb, gt, gn)
        dyf, _, dgt, dgn = vjp(ct)
        return dyf, dgt, dgn

    dyr, dgate, g['ret_gn_g'] = _pw(gn_bwd, [sv['yf'], sv['yb'], zr, w['ret_gn_g'], dys[0]],
                                    [_row(T, 256, hd), _row(T, 256, hd), _row(T, 256, lambda j: 8 + j), _par(256, hd),
                                     _row(T, 256, hd)],
                                    [SDS((S, 1024), bf16), SDS((S, 1024), bf16), SDS((1, 1024), f32)],
                                    [_row(T, 256, hd), _row(T, 256, hd), _par(256, hd)], (RET_HEADS, I), n_acc=1, name=nm("ret_gn"))
    qf, kf, vf, lgf = _ret_dir_bwd(zr, w['lg'], tabs['cos_r'], tabs['sin_r'], dyr, sv['stf'], reverse=False, name=nm("ret_f"))
    qb, kb, vb, lgb = _ret_dir_bwd(zr, w['lg'], tabs['cos_r'], tabs['sin_r'], dyr, sv['stb'], reverse=True, name=nm("ret_b"))
    g['lg'] = jnp.stack([lgf[:, 0, 0], lgb[:, 0, 0]])
    add2 = lambda a, b: a + b
    dq = _pw(add2, [qf, qb], [_row(T, 512)] * 2, [SDS((S, 512), bf16)], [_row(T, 512)], (1, I), name=nm("ret_dq"))[0]
    dk = _pw(add2, [kf, kb], [_row(T, 512)] * 2, [SDS((S, 512), bf16)], [_row(T, 512)], (1, I), name=nm("ret_dk"))[0]
    dv = _pw(add2, [vf, vb], [_row(T, D)] * 2, [SDS((S, D), bf16)], [_row(T, D)], (1, I), name=nm("ret_dv"))[0]
    dzr = jnp.concatenate([dq, dk, dv, dgate], axis=1)

    h = sv['h']
    g['W_ret'] = _mmT(h, dzr, name=nm("in_ret_dw"))
    g['W_mla'] = _mmT(h, dzm, name=nm("in_mla_dw"))
    g['W_s5'] = _mmT(h, dzs, name=nm("in_s5_dw"))
    g['W_gate'] = _mmT(h, dzg, name=nm("in_gate_dw"))
    dh = _mm(dzr, w['W_ret'], tb=True, name=nm("in_ret_dx"))
    dh = _mm(dzm, w['W_mla'], tb=True, res=dh, name=nm("in_mla_dx"))
    dh = _mm(dzs, w['W_s5'], tb=True, res=dh, name=nm("in_s5_dx"))
    dh = _mm(dzg, w['W_gate'], tb=True, res=dh, name=nm("in_gate_dx"))
    dx, g['norm1_g'] = _rmsnorm_bwd(sv['x'], w['norm1_g'], dh, dx1, name=nm("norm1"))
    return dx, g


def _loss_head(x, tgt, gain, *, name):
    S, W = x.shape
    T = min(S, 512)

    def loss_fn(xv, gv, tv):
        return 0.5 * jnp.sum(jnp.mean(jnp.square(_rms(xv, gv) - tv), axis=-1, keepdims=True), axis=0, keepdims=True)

    def fn(xv, gv, tv):
        lv, vjp = jax.vjp(lambda a, b: loss_fn(a, b, tv), xv, gv)
        dx, dg = vjp(jnp.ones((1, 1), f32))
        return dx, jnp.broadcast_to(lv, (1, 128)), dg

    return _pw(fn, [x, gain, tgt], [_row(T, W), _par(W), _row(T, W)],
               [SDS((S, W), f32), SDS((1, 128), f32), SDS((1, W), f32)], [_row(T, W), _par(128), _par(W)],
               (1, S // T), n_acc=2, name=name)


def _rope_tabs(S):
    def tab(dim):
        inv = 1.0 / (ROPE_THETA ** (jnp.arange(0, dim, 2, dtype=f32) / dim))
        ang = jnp.arange(S, dtype=f32)[:, None] * inv[None, :]
        return jnp.cos(ang), jnp.sin(ang)

    cr, sr = tab(RET_DK)
    cm, sm = tab(MLA_ROPE)
    z = jnp.zeros((S, 64), f32)
    return {'cos_r': jnp.concatenate([cr, cr], axis=1), 'sin_r': jnp.concatenate([-sr, sr], axis=1),
            'cos_m': jnp.concatenate([cm, cm, z], axis=1), 'sin_m': jnp.concatenate([-sm, sm, z], axis=1)}


def _bd_B(bb):
    b5 = bb.reshape(16, 2, 8, 8, 64)
    return jnp.einsum('cdjgp,gh->djgchp', b5, jnp.eye(8, dtype=bb.dtype)).reshape(2, 8, 128, 512)


def _bd_B_t(dBB):
    return jnp.einsum('djgcgp->cdjgp', dBB.reshape(2, 8, 8, 16, 8, 64)).reshape(16, 8192)


def _bd_C(c):
    c5 = c.reshape(2, 8, 8, 16, 64)
    return jnp.einsum('djgcp,gh->djgphc', c5, jnp.eye(8, dtype=c.dtype)).reshape(2, 8, 512, 128)


def _bd_C_t(dCC):
    return jnp.einsum('djgpgc->djgcp', dCC.reshape(2, 8, 8, 64, 8, 16)).reshape(2, 64, 16, 64)


def _s5_rows(p, l):
    a_re = p['s5_a_re'][l].reshape(1, 8192)
    a_im = p['s5_a_im'][l].reshape(1, 8192)
    ldt = jnp.broadcast_to(p['s5_log_dt'][l][:, :, None], (2, S5_G, S5_P)).reshape(1, 8192)
    b_re = p['s5_b_re'][l].transpose(3, 0, 1, 2).reshape(16, 8192)
    b_im = p['s5_b_im'][l].transpose(3, 0, 1, 2).reshape(16, 8192)
    return a_re, a_im, ldt, b_re, b_im


def _layer_weights(big, p, l):
    w_in = big['w_in'][l]
    z = lambda n: jnp.zeros((D, n), w_in.dtype)
    w = {
        'W_ret': w_in[:, 0:3072],
        'W_mla': jnp.concatenate([w_in[:, 3072:3456], z(128), w_in[:, 3456:3712], w_in[:, 3712:3776], z(64)], axis=1),
        'W_s5': w_in[:, 3776:4800],
        'W_gate': w_in[:, 4800:7872],
        'W_uq': jnp.pad(big['mla_w_uq'][l].reshape(MLA_Q_LORA, MLA_HEADS, 192), ((0, 0), (0, 0), (0, 64))).reshape(MLA_Q_LORA, 2048),
        'W_ukv': big['mla_w_ukv'][l],
        'W_glu': big['s5_w_glu'][l],
        'W_br': [big['w_branch'][l, i] for i in range(3)],
        'W_out': big['w_out'][l],
        'W_gu': big['ffn_w_gu'][l],
        'W_down': big['ffn_w_down'][l],
    }
    for n in ('norm1_g', 'ret_gn_g', 'mla_q_norm_g', 'mla_kv_norm_g', 's5_d', 'norm2_g'):
        w[n] = p[n][l][None, :]
    w['lg'] = jax.nn.log_sigmoid(p['ret_decay'][l])
    rows = _s5_rows(p, l)
    abr, abi, bbr, bbi, pwr, pwi = _s5_param_fwd(*rows, name=f"L{l}_s5_param")
    flip = lambda t, first: jnp.concatenate([t[::-1, :4096], t[:, 4096:]] if first else [t[:, :4096], t[::-1, 4096:]], axis=1)
    w['s5'] = {'abr': abr, 'abi': abi, 'BBr': _bd_B(bbr).astype(bf16), 'BBi': _bd_B(bbi).astype(bf16),
               'CCr': _bd_C(p['s5_c_re'][l]).astype(bf16), 'CCi': _bd_C(p['s5_c_im'][l]).astype(bf16),
               'pwr_f': flip(pwr, False), 'pwi_f': flip(pwi, False), 'pwr_a': flip(pwr, True), 'pwi_a': flip(pwi, True),
               'rows': rows}
    return w


def _layer_grads(g, w, p, l):
    out = {}
    m = g['W_mla']
    out['w_in'] = jnp.concatenate([g['W_ret'], m[:, 0:384], m[:, 512:768], m[:, 768:832], g['W_s5'], g['W_gate']], axis=1)
    out['mla_w_uq'] = g['W_uq'].reshape(MLA_Q_LORA, MLA_HEADS, 256)[:, :, :192].reshape(MLA_Q_LORA, 1536)
    out['mla_w_ukv'] = g['W_ukv']
    out['s5_w_glu'] = g['W_glu']
    out['w_branch'] = jnp.stack(g['W_br'])
    out['w_out'] = g['W_out']
    out['ffn_w_gu'] = g['W_gu']
    out['ffn_w_down'] = g['W_down']
    for n in ('norm1_g', 'ret_gn_g', 'mla_q_norm_g', 'mla_kv_norm_g', 's5_d', 'norm2_g'):
        out[n] = g[n][0]
    out['ret_decay'] = g['lg'] * jax.nn.sigmoid(-p['ret_decay'][l])
    (fB_r, fB_i, fC_r, fC_i, fa_r, fa_i), (bB_r, bB_i, bC_r, bC_i, ba_r, ba_i) = g['s5']
    cat = lambda a, b: jnp.concatenate([a, b], axis=0)
    d_bbr = _bd_B_t(cat(fB_r, bB_r))
    d_bbi = _bd_B_t(cat(fB_i, bB_i))
    out['s5_c_re'] = _bd_C_t(cat(fC_r, bC_r))
    out['s5_c_im'] = _bd_C_t(cat(fC_i, bC_i))
    d_abr = jnp.concatenate([fa_r, ba_r], axis=1)
    d_abi = jnp.concatenate([fa_i, ba_i], axis=1)
    da_re, da_im, dldt, db_re, db_im = _s5_param_bwd(*w['s5']['rows'], d_abr, d_abi, d_bbr, d_bbi, name=f"L{l}_b_s5_param")
    out['s5_a_re'] = da_re.reshape(2, S5_G, S5_P)
    out['s5_a_im'] = da_im.reshape(2, S5_G, S5_P)
    out['s5_log_dt'] = dldt.reshape(2, S5_G, S5_P).sum(axis=-1)
    out['s5_b_re'] = db_re.reshape(16, 2, S5_G, S5_P).transpose(1, 2, 3, 0)
    out['s5_b_im'] = db_im.reshape(16, 2, S5_G, S5_P).transpose(1, 2, 3, 0)
    return out


def _local_step(x, tgt, big, p):
    S = x.shape[0]
    assert S % 512 == 0
    tabs = _rope_tabs(S)
    ws, svs = [], []
    h = x
    for l in range(DEPTH):
        w = _layer_weights(big, p, l)
        h, sv = _layer_fwd(h, w, tabs, l)
        ws.append(w)
        svs.append(sv)
    dx, lossv, dfinal = _loss_head(h, tgt, p['final_g'][None, :], name="loss_head")
    per_layer = [None] * DEPTH
    for l in reversed(range(DEPTH)):
        dx, g = _layer_bwd(dx, ws[l], tabs, svs[l], l)
        per_layer[l] = _layer_grads(g, ws[l], p, l)
    grads = {n: jnp.stack([per_layer[l][n] for l in range(DEPTH)]) for n in per_layer[0]}
    grads['final_g'] = dfinal[0]
    return lossv[0, 0], dx, grads


_ANY = pl.BlockSpec(memory_space=pl.ANY)


def _place():
    x, y, c = lax.axis_index("x"), lax.axis_index("y"), lax.axis_index("c")
    return x, y, c, [(1 - x, y), (x, 1 - y), (1 - x, 1 - y)]


def _allgather4(arrs, *, name):
    n = len(arrs)

    def body(*refs):
        ins, outs = refs[:n], refs[n:2 * n]
        send, recv, loc = refs[2 * n:]
        x, y, c, chips = _place()
        me = 2 * x + y

        def remote(a, k, slot):
            px, py = chips[k]
            return pltpu.make_async_remote_copy(src_ref=ins[a], dst_ref=outs[a].at[slot], send_sem=send.at[a, k],
                                                recv_sem=recv.at[a, k], device_id=(px, py, c), device_id_type=MESH)

        mine = [pltpu.make_async_copy(ins[a], outs[a].at[me], loc.at[a]) for a in range(n)]
        for cp in mine:
            cp.start()
        sends = [remote(a, k, me) for a in range(n) for k in range(3)]
        for cp in sends:
            cp.start()
        for a in range(n):
            for k, (px, py) in enumerate(chips):
                remote(a, k, 2 * px + py).wait_recv()
        for cp in sends:
            cp.wait_send()
        for cp in mine:
            cp.wait()

    return pl.pallas_call(
        body, in_specs=[_ANY] * n, out_specs=[_ANY] * n, out_shape=[SDS((4,) + a.shape, a.dtype) for a in arrs],
        scratch_shapes=[pltpu.SemaphoreType.DMA((n, 3)), pltpu.SemaphoreType.DMA((n, 3)), pltpu.SemaphoreType.DMA((n,))],
        name=name)(*arrs)


def _rs_exchange(parts, *, name):
    n = len(parts)

    def body(*refs):
        ins, owns, gots = refs[:n], refs[n:2 * n], refs[2 * n:3 * n]
        send, recv, loc = refs[3 * n:]
        x, y, c, chips = _place()
        me = 2 * x + y

        def remote(a, k):
            px, py = chips[k]
            return pltpu.make_async_remote_copy(src_ref=ins[a].at[2 * px + py], dst_ref=gots[a].at[k], send_sem=send.at[a, k],
                                                recv_sem=recv.at[a, k], device_id=(px, py, c), device_id_type=MESH)

        mine = [pltpu.make_async_copy(ins[a].at[me], owns[a], loc.at[a]) for a in range(n)]
        for cp in mine:
            cp.start()
        sends = [remote(a, k) for a in range(n) for k in range(3)]
        for cp in sends:
            cp.start()
        for cp in sends:
            cp.wait_recv()
        for cp in sends:
            cp.wait_send()
        for cp in mine:
            cp.wait()

    return pl.pallas_call(
        body, in_specs=[_ANY] * n, out_specs=[_ANY] * (2 * n),
        out_shape=[SDS(a.shape[1:], a.dtype) for a in parts] + [SDS((3,) + a.shape[1:], a.dtype) for a in parts],
        scratch_shapes=[pltpu.SemaphoreType.DMA((n, 3)), pltpu.SemaphoreType.DMA((n, 3)), pltpu.SemaphoreType.DMA((n,))],
        name=name)(*parts)


def _sib_exchange(arrs, *, name):
    n = len(arrs)

    def body(*refs):
        ins, outs = refs[:n], refs[n:2 * n]
        send, recv = refs[2 * n:]
        x, y, c, _ = _place()
        cps = [pltpu.make_async_remote_copy(src_ref=ins[a], dst_ref=outs[a], send_sem=send.at[a], recv_sem=recv.at[a],
                                            device_id=(x, y, 1 - c), device_id_type=MESH) for a in range(n)]
        for cp in cps:
            cp.start()
        for cp in cps:
            cp.wait_recv()
        for cp in cps:
            cp.wait_send()

    return pl.pallas_call(
        body, in_specs=[_ANY] * n, out_specs=[_ANY] * n, out_shape=[SDS(a.shape, a.dtype) for a in arrs],
        scratch_shapes=[pltpu.SemaphoreType.DMA((n,)), pltpu.SemaphoreType.DMA((n,))], name=name)(*arrs)


def _sum4(own, got, *, name):
    R, W = own.shape
    tr = R if R <= 256 else 128
    g3 = lambda k: pl.BlockSpec((None, tr, W), lambda j, i: (k, i, 0))
    up = lambda t: t.astype(f32)
    return _pw(lambda a, b, c, d: ((up(a) + up(b)) + up(c)) + up(d), [own, got, got, got], [_row(tr, W), g3(0), g3(1), g3(2)],
               [SDS((R, W), f32)], [_row(tr, W)], (1, R // tr), name=name)[0]


def _adamw(po, ps, w, m, v, *, name):
    R, W = w.shape
    tr = R if R <= 256 else 128

    def fn(a, b, wv, mv, vv):
        g = a + b
        m2 = ADAM_B1 * mv + (1.0 - ADAM_B1) * g
        v2 = ADAM_B2 * vv + (1.0 - ADAM_B2) * jnp.square(g)
        m_hat = m2 / (1.0 - ADAM_B1 ** ADAM_STEP)
        v_hat = v2 / (1.0 - ADAM_B2 ** ADAM_STEP)
        return g, -ADAM_LR * (m_hat / (jnp.sqrt(v_hat) + ADAM_EPS) + ADAM_WD * wv), m2, v2

    return _pw(fn, [po, ps, w, m, v], [_row(tr, W)] * 5, [SDS((R, W), f32)] * 4, [_row(tr, W)] * 4, (1, R // tr), name=name)


def _to_parts(g, axis):
    shp = g.shape
    g = g.reshape(shp[:axis] + (4, shp[axis] // 4) + shp[axis + 1:])
    return jnp.moveaxis(g, axis, 0)


def _from_parts(pt, axis):
    g = jnp.moveaxis(pt, 0, axis)
    shp = g.shape
    return g.reshape(shp[:axis] + (4 * shp[axis + 1],) + shp[axis + 2:])


def kernel(x, norm1_g, w_in, ret_decay, ret_gn_g, mla_q_norm_g, mla_w_uq, mla_kv_norm_g, mla_w_ukv, s5_a_re, s5_a_im, s5_log_dt, s5_b_re, s5_b_im, s5_c_re, s5_c_im, s5_d, s5_w_glu, w_branch, w_out, norm2_g, ffn_w_gu, ffn_w_down, final_g, loss_target, m_norm1_g, m_w_in, m_ret_decay, m_ret_gn_g, m_mla_q_norm_g, m_mla_w_uq, m_mla_kv_norm_g, m_mla_w_ukv, m_s5_a_re, m_s5_a_im, m_s5_log_dt, m_s5_b_re, m_s5_b_im, m_s5_c_re, m_s5_c_im, m_s5_d, m_s5_w_glu, m_w_branch, m_w_out, m_norm2_g, m_ffn_w_gu, m_ffn_w_down, m_final_g, v_norm1_g, v_w_in, v_ret_decay, v_ret_gn_g, v_mla_q_norm_g, v_mla_w_uq, v_mla_kv_norm_g, v_mla_w_ukv, v_s5_a_re, v_s5_a_im, v_s5_log_dt, v_s5_b_re, v_s5_b_im, v_s5_c_re, v_s5_c_im, v_s5_d, v_s5_w_glu, v_w_branch, v_w_out, v_norm2_g, v_ffn_w_gu, v_ffn_w_down, v_final_g):
    wv = dict(zip(W_NAMES, (norm1_g, w_in, ret_decay, ret_gn_g, mla_q_norm_g, mla_w_uq, mla_kv_norm_g, mla_w_ukv, s5_a_re, s5_a_im,
                            s5_log_dt, s5_b_re, s5_b_im, s5_c_re, s5_c_im, s5_d, s5_w_glu, w_branch, w_out, norm2_g, ffn_w_gu,
                            ffn_w_down, final_g)))
    mv = dict(zip(W_NAMES, (m_norm1_g, m_w_in, m_ret_decay, m_ret_gn_g, m_mla_q_norm_g, m_mla_w_uq, m_mla_kv_norm_g, m_mla_w_ukv,
                            m_s5_a_re, m_s5_a_im, m_s5_log_dt, m_s5_b_re, m_s5_b_im, m_s5_c_re, m_s5_c_im, m_s5_d, m_s5_w_glu,
                            m_w_branch, m_w_out, m_norm2_g, m_ffn_w_gu, m_ffn_w_down, m_final_g)))
    vv = dict(zip(W_NAMES, (v_norm1_g, v_w_in, v_ret_decay, v_ret_gn_g, v_mla_q_norm_g, v_mla_w_uq, v_mla_kv_norm_g, v_mla_w_ukv,
                            v_s5_a_re, v_s5_a_im, v_s5_log_dt, v_s5_b_re, v_s5_b_im, v_s5_c_re, v_s5_c_im, v_s5_d, v_s5_w_glu,
                            v_w_branch, v_w_out, v_norm2_g, v_ffn_w_gu, v_ffn_w_down, v_final_g)))
    big_names = list(BIG)

    gathered = _allgather4([wv[n].astype(bf16) for n in big_names], name="gather_weights")
    big = {n: _from_parts(gt, BIG[n]) for n, gt in zip(big_names, gathered)}
    small = {n: wv[n] for n in SMALL}

    loss_local, dx, grads = _local_step(x[0], loss_target[0], big, small)

    n_rows = {n: -(-math.prod(wv[n].shape) // 1024) * 8 for n in SMALL}
    used = sum(n_rows.values())
    rows_q = -(-(used + 8) // (4 * 128)) * 128

    def as_rows(d, tail=None):
        blocks = [jnp.pad(d[n].reshape(-1), (0, n_rows[n] * 128 - math.prod(wv[n].shape))).reshape(n_rows[n], 128) for n in SMALL]
        blocks.append(jnp.zeros((8, 128), f32) if tail is None else tail)
        blocks.append(jnp.zeros((4 * rows_q - used - 8, 128), f32))
        return jnp.concatenate(blocks, axis=0)

    loss_rows = jnp.full((8, 128), loss_local, f32)
    parts = [_to_parts(grads[n].astype(bf16), BIG[n]) for n in big_names] + [as_rows(grads, loss_rows).reshape(4, rows_q, 128)]
    got = _rs_exchange(parts, name="grad_exchange")
    n_arr = len(parts)
    two_d = lambda a: a.reshape(-1, a.shape[-1])
    sums = [_sum4(two_d(got[a]), got[n_arr + a].reshape(3, -1, got[a].shape[-1]), name=f"grad_sum4_{a}") for a in range(n_arr)]
    sib = _sib_exchange(sums, name="grad_sibling")

    out_g, out_d, out_m, out_v = {}, {}, {}, {}
    for a, n in enumerate(big_names):
        shp = wv[n].shape
        res = _adamw(sums[a], sib[a], two_d(wv[n]), two_d(mv[n]), two_d(vv[n]), name=f"adamw_{n}")
        out_g[n], out_d[n], out_m[n], out_v[n] = [r.reshape(shp) for r in res]
    g_quarter = _pw(lambda p, q: p + q, [sums[-1], sib[-1]], [_row(rows_q, 128)] * 2, [SDS((rows_q, 128), f32)],
                    [_row(rows_q, 128)], (1, 1), name="small_grad_sum")[0]
    g_small = _allgather4([g_quarter], name="gather_small_grads")[0].reshape(4 * rows_q, 128)
    loss = g_small[used, 0]
    zero = jnp.zeros_like(g_small)
    res = _adamw(g_small, zero, as_rows(wv), as_rows(mv), as_rows(vv), name="adamw_small")
    off = 0
    for n in SMALL:
        k = math.prod(wv[n].shape)
        for dst, r in zip((out_g, out_d, out_m, out_v), res):
            dst[n] = r[off:off + n_rows[n]].reshape(-1)[:k].reshape(wv[n].shape)
        off += n_rows[n]
    return (loss, dx[None], *[out_g[n] for n in W_NAMES], *[out_d[n] for n in W_NAMES], *[out_m[n] for n in W_NAMES],
            *[out_v[n] for n in W_NAMES])
```

```python
import functools
import math

import jax
import jax.numpy as jnp
from jax import lax
from jax.experimental import pallas as pl
from jax.experimental.pallas import tpu as pltpu

f32 = jnp.float32
bf16 = jnp.bfloat16
SDS = jax.ShapeDtypeStruct
MESH = pl.DeviceIdType.MESH

D = 1024
DEPTH = 2
RMS_EPS = 1e-6
GN_EPS = 1e-5
ROPE_THETA = 10000.0
RET_HEADS = 4
RET_DK = 128
RET_DV = 256
RET_CHUNK = 128
MLA_HEADS = 8
MLA_Q_LORA = 384
MLA_KV_LORA = 256
MLA_NOPE = 128
MLA_ROPE = 64
MLA_V = 128
MLA_QW = 256
S5_G = 64
S5_P = 64
S5_C = 16
S5_NJ = 8
S5_SEG = 8
FFN_H = 2816
ADAM_LR = 0.001
ADAM_B1 = 0.9
ADAM_B2 = 0.999
ADAM_EPS = 1e-08
ADAM_WD = 0.01
ADAM_STEP = 10
VMEM_BIG = 56 * 1024 * 1024

W_NAMES = ['norm1_g', 'w_in', 'ret_decay', 'ret_gn_g', 'mla_q_norm_g', 'mla_w_uq', 'mla_kv_norm_g', 'mla_w_ukv',
           's5_a_re', 's5_a_im', 's5_log_dt', 's5_b_re', 's5_b_im', 's5_c_re', 's5_c_im', 's5_d', 's5_w_glu',
           'w_branch', 'w_out', 'norm2_g', 'ffn_w_gu', 'ffn_w_down', 'final_g']
BIG = {'w_in': 2, 'mla_w_uq': 2, 'mla_w_ukv': 2, 's5_w_glu': 2, 'w_branch': 2, 'w_out': 1, 'ffn_w_gu': 2, 'ffn_w_down': 1}
SMALL = [n for n in W_NAMES if n not in BIG]


def _pick(n, cands=(512, 384, 256, 128)):
    if n <= 1024:
        return n
    for c in cands:
        if n % c == 0:
            return c
    raise ValueError(n)


def _params(sem, vmem=None):
    return pltpu.CompilerParams(dimension_semantics=sem, vmem_limit_bytes=vmem)


def _mm(a, b, *, tb=False, res=None, out_dtype=f32, name):
    M, K = a.shape
    N = b.shape[0] if tb else b.shape[1]
    tn = _pick(N)
    tk = K if K <= 3072 else _pick(K, (1408, 1024, 512))
    nk = K // tk
    tm = _pick(M)
    if M % 1024 == 0 and 1024 * tk * a.dtype.itemsize <= 4 * 1024 * 1024:
        tm = 1024
    assert M % tm == 0 and N % tn == 0 and K % tk == 0

    def body(*refs):
        if res is None:
            a_ref, b_ref, o_ref, acc = refs
        else:
            a_ref, b_ref, r_ref, o_ref, acc = refs
        k = pl.program_id(2)
        dn = (((1,), (1 if tb else 0,)), ((), ()))
        part = lax.dot_general(a_ref[...].astype(bf16), b_ref[...].astype(bf16), dn, preferred_element_type=f32)

        @pl.when(k == 0)
        def _():
            acc[...] = part

        @pl.when(k > 0)
        def _():
            acc[...] += part

        @pl.when(k == nk - 1)
        def _():
            v = acc[...]
            if res is not None:
                v = v + r_ref[...]
            o_ref[...] = v.astype(out_dtype)

    in_specs = [pl.BlockSpec((tm, tk), lambda i, j, k: (i, k)),
                pl.BlockSpec((tn, tk), lambda i, j, k: (j, k)) if tb else pl.BlockSpec((tk, tn), lambda i, j, k: (k, j))]
    args = [a, b]
    if res is not None:
        in_specs.append(pl.BlockSpec((tm, tn), lambda i, j, k: (i, j)))
        args.append(res)
    return pl.pallas_call(
        body, grid=(M // tm, N // tn, nk), in_specs=in_specs,
        out_specs=pl.BlockSpec((tm, tn), lambda i, j, k: (i, j)),
        out_shape=SDS((M, N), out_dtype), scratch_shapes=[pltpu.VMEM((tm, tn), f32)],
        compiler_params=_params(("parallel", "parallel", "arbitrary"), VMEM_BIG), name=name)(*args)


def _mmT(a, b, *, name):
    S, M = a.shape
    N = b.shape[1]
    tm = _pick(M)
    tn = _pick(N)
    tk = min(S, 1024)
    nk = S // tk

    def body(a_ref, b_ref, o_ref):
        k = pl.program_id(2)
        part = lax.dot_general(a_ref[...].astype(bf16), b_ref[...].astype(bf16), (((0,), (0,)), ((), ())),
                               preferred_element_type=f32)

        @pl.when(k == 0)
        def _():
            o_ref[...] = part

        @pl.when(k > 0)
        def _():
            o_ref[...] += part

    return pl.pallas_call(
        body, grid=(M // tm, N // tn, nk),
        in_specs=[pl.BlockSpec((tk, tm), lambda i, j, k: (k, i)), pl.BlockSpec((tk, tn), lambda i, j, k: (k, j))],
        out_specs=pl.BlockSpec((tm, tn), lambda i, j, k: (i, j)),
        out_shape=SDS((M, N), f32),
        compiler_params=_params(("parallel", "parallel", "arbitrary"), VMEM_BIG), name=name)(a, b)


def _pw(fn, ins, in_specs, outs, out_specs, grid, *, n_acc=0, name):
    n_in = len(ins)
    n_out = len(outs)

    def body(*refs):
        vals = fn(*[r[...] for r in refs[:n_in]])
        if not isinstance(vals, (tuple, list)):
            vals = (vals,)
        orefs = refs[n_in:]
        for r, v in zip(orefs[:n_out - n_acc], vals[:n_out - n_acc]):
            r[...] = v.astype(r.dtype)
        if n_acc:
            i = pl.program_id(1)

            @pl.when(i == 0)
            def _():
                for r, v in zip(orefs[n_out - n_acc:], vals[n_out - n_acc:]):
                    r[...] = v.astype(r.dtype)

            @pl.when(i > 0)
            def _():
                for r, v in zip(orefs[n_out - n_acc:], vals[n_out - n_acc:]):
                    r[...] += v.astype(r.dtype)

    res = pl.pallas_call(
        body, grid=grid, in_specs=in_specs, out_specs=out_specs, out_shape=outs,
        compiler_params=_params(("parallel", "arbitrary"), VMEM_BIG), name=name)(*ins)
    return res


def _row(T, w, col=None):
    if col is None:
        return pl.BlockSpec((T, w), lambda j, i: (i, 0))
    return pl.BlockSpec((T, w), lambda j, i: (i, col(j)))


def _par(w, col=None):
    if col is None:
        return pl.BlockSpec((1, w), lambda j, i: (0, 0))
    return pl.BlockSpec((1, w), lambda j, i: (0, col(j)))


def _rms(x, g):
    return x * lax.rsqrt(jnp.mean(x * x, axis=-1, keepdims=True) + RMS_EPS) * g


def _rope(x, cos, sinm, half):
    if half == 64:
        partner = pltpu.roll(x, 64, axis=1)
    else:
        lane = lax.broadcasted_iota(jnp.int32, x.shape, 1)
        partner = jnp.where((lane % (2 * half)) < half, pltpu.roll(x, 128 - half, axis=1), pltpu.roll(x, half, axis=1))
    return x * cos + partner * sinm


def _rope_t(x, cos, sinm, half):
    return _rope(x, cos, -sinm, half)


def _rmsnorm_fwd(x, g, *, name):
    S, W = x.shape
    T = min(S, 512)
    return _pw(lambda xv, gv: _rms(xv, gv), [x, g], [_row(T, W), _par(W)], [SDS((S, W), bf16)], [_row(T, W)],
               (1, S // T), name=name)[0]


def _rmsnorm_bwd(x, g, dh, dres, *, name):
    S, W = x.shape
    T = min(S, 512)

    def fn(xv, gv, dhv, drv):
        _, vjp = jax.vjp(_rms, xv, gv)
        dx, dg = vjp(dhv)
        return dx + drv, dg

    return _pw(fn, [x, g, dh, dres], [_row(T, W), _par(W), _row(T, W), _row(T, W)],
               [SDS((S, W), f32), SDS((1, W), f32)], [_row(T, W), _par(W)], (1, S // T), n_acc=1, name=name)


def _ret_tables(lg, reverse):
    C = RET_CHUNK
    ii = lax.broadcasted_iota(jnp.int32, (C, C), 0).astype(f32)
    jj = lax.broadcasted_iota(jnp.int32, (C, C), 1).astype(f32)
    if not reverse:
        E = ii - jj
        mask = E >= 0
        eq = ii + 1.0
        ek = (C - 1.0) - ii
    else:
        E = jj - ii
        mask = E > 0
        eq = C - ii
        ek = ii
    Dm = jnp.where(mask, jnp.exp(jnp.where(mask, E, 0.0) * lg), 0.0)
    Em = jnp.where(mask, E, 0.0)
    qw = jnp.exp(eq * lg)
    kw = jnp.exp(ek * lg)
    qw2 = jnp.concatenate([qw, qw], axis=1)
    return Dm, Em, eq, ek, qw, kw, qw2, jnp.exp(C * lg)


def _dot(a, b, dims):
    return lax.dot_general(a.astype(bf16), b.astype(bf16), (dims, ((), ())), preferred_element_type=f32)


NN = ((1,), (0,))
NT = ((1,), (1,))
TN = ((0,), (0,))


def _ret_dir_fwd(zr, lg, cos, sinm, *, reverse, name):
    S = zr.shape[0]
    C = RET_CHUNK
    TB = min(S, 512)
    nc = TB // C
    NB = S // TB
    d = 1 if reverse else 0
    scale = RET_DK ** -0.5

    def tb(b):
        return (NB - 1 - b) if reverse else b

    def body(lg_ref, q_ref, k_ref, v_ref, cos_ref, sin_ref, y_ref, st_ref, state):
        h = pl.program_id(0)
        b = pl.program_id(1)

        @pl.when(b == 0)
        def _():
            state[...] = jnp.zeros_like(state)

        Dm, _, _, _, _, kw, qw2, gC = _ret_tables(lg_ref[d, h], reverse)
        order = range(nc - 1, -1, -1) if reverse else range(nc)
        for c in order:
            rows = pl.ds(c * C, C)
            q = _rope(q_ref[rows, :], cos_ref[rows, :], sin_ref[rows, :], 64) * scale
            k = _rope(k_ref[rows, :], cos_ref[rows, :], sin_ref[rows, :], 64)
            v = v_ref[rows, :]
            st = state[...]
            st_ref[0, c] = st
            s = _dot(q, k, NT) * Dm
            o = _dot(s, v, NN) + _dot(q, st, NN) * qw2
            y_ref[rows, :] = o
            state[...] = gC * st + _dot(k * kw, v, TN)

    return pl.pallas_call(
        body, grid=(RET_HEADS, NB),
        in_specs=[pl.BlockSpec(memory_space=pltpu.SMEM),
                  pl.BlockSpec((TB, 128), lambda h, b: (tb(b), h)),
                  pl.BlockSpec((TB, 128), lambda h, b: (tb(b), 4 + h)),
                  pl.BlockSpec((TB, 256), lambda h, b: (tb(b), 4 + h)),
                  pl.BlockSpec((TB, 128), lambda h, b: (tb(b), 0)),
                  pl.BlockSpec((TB, 128), lambda h, b: (tb(b), 0))],
        out_specs=[pl.BlockSpec((TB, 256), lambda h, b: (tb(b), h)),
                   pl.BlockSpec((1, nc, 128, 256), lambda h, b: (h, tb(b), 0, 0))],
        out_shape=[SDS((S, 1024), f32), SDS((RET_HEADS, S // C, 128, 256), f32)],
        scratch_shapes=[pltpu.VMEM((128, 256), f32)],
        compiler_params=_params(("parallel", "arbitrary")), name=name)(lg, zr, zr, zr, cos, sinm)


def _ret_dir_bwd(zr, lg, cos, sinm, dy, states, *, reverse, name):
    S = zr.shape[0]
    C = RET_CHUNK
    TB = min(S, 512)
    nc = TB // C
    NB = S // TB
    d = 1 if reverse else 0
    scale = RET_DK ** -0.5

    def tb(b):
        return b if reverse else (NB - 1 - b)

    def body(lg_ref, q_ref, k_ref, v_ref, cos_ref, sin_ref, dy_ref, st_ref, dq_ref, dk_ref, dv_ref, dlg_ref, dstate):
        h = pl.program_id(0)
        b = pl.program_id(1)

        @pl.when(b == 0)
        def _():
            dstate[...] = jnp.zeros_like(dstate)
            dlg_ref[...] = jnp.zeros_like(dlg_ref)

        Dm, Em, eq, ek, qw, kw, qw2, gC = _ret_tables(lg_ref[d, h], reverse)
        order = range(nc) if reverse else range(nc - 1, -1, -1)
        dlg = jnp.zeros((), f32)
        for c in order:
            rows = pl.ds(c * C, C)
            cs, sn = cos_ref[rows, :], sin_ref[rows, :]
            q = _rope(q_ref[rows, :], cs, sn, 64) * scale
            k = _rope(k_ref[rows, :], cs, sn, 64)
            v = v_ref[rows, :]
            do = dy_ref[rows, :]
            st = st_ref[0, c]
            ds = dstate[...]
            p = _dot(q, k, NT)
            a = p * Dm
            dp = _dot(do, v, NT) * Dm
            dq_cross = _dot(do, st, NT) * qw
            dk_cross = _dot(v, ds, NT) * kw
            dq = _dot(dp, k, NN) + dq_cross
            dk = _dot(dp, q, TN) + dk_cross
            dv = _dot(a, do, TN) + _dot(k * kw, ds, NN)
            dlg = dlg + jnp.sum(dp * p * Em) + jnp.sum(dq_cross * q * eq) + jnp.sum(dk_cross * k * ek) \
                + C * gC * jnp.sum(ds * st)
            dstate[...] = gC * ds + _dot(q * qw, do, TN)
            dq_ref[rows, :] = _rope_t(dq, cs, sn, 64) * scale
            dk_ref[rows, :] = _rope_t(dk, cs, sn, 64)
            dv_ref[rows, :] = dv
        dlg_ref[...] += jnp.full(dlg_ref.shape, dlg, f32)

    return pl.pallas_call(
        body, grid=(RET_HEADS, NB),
        in_specs=[pl.BlockSpec(memory_space=pltpu.SMEM),
                  pl.BlockSpec((TB, 128), lambda h, b: (tb(b), h)),
                  pl.BlockSpec((TB, 128), lambda h, b: (tb(b), 4 + h)),
                  pl.BlockSpec((TB, 256), lambda h, b: (tb(b), 4 + h)),
                  pl.BlockSpec((TB, 128), lambda h, b: (tb(b), 0)),
                  pl.BlockSpec((TB, 128), lambda h, b: (tb(b), 0)),
                  pl.BlockSpec((TB, 256), lambda h, b: (tb(b), h)),
                  pl.BlockSpec((1, nc, 128, 256), lambda h, b: (h, tb(b), 0, 0))],
        out_specs=[pl.BlockSpec((TB, 128), lambda h, b: (tb(b), h)),
                   pl.BlockSpec((TB, 128), lambda h, b: (tb(b), h)),
                   pl.BlockSpec((TB, 256), lambda h, b: (tb(b), h)),
                   pl.BlockSpec((1, 1, 128), lambda h, b: (h, 0, 0))],
        out_shape=[SDS((S, 512), f32), SDS((S, 512), f32), SDS((S, 1024), f32), SDS((RET_HEADS, 1, 128), f32)],
        scratch_shapes=[pltpu.VMEM((128, 256), f32)],
        compiler_params=_params(("parallel", "arbitrary")), name=name)(lg, zr, zr, zr, cos, sinm, dy, states)


def _gn_gate(yf, yb, g, gn):
    y = yf + yb
    mu = jnp.mean(y, axis=-1, keepdims=True)
    var = jnp.mean(jnp.square(y - mu), axis=-1, keepdims=True)
    yn = (y - mu) * lax.rsqrt(var + GN_EPS)
    return jax.nn.silu(g) * (yn * gn)


def _flash_fwd(Q, K, kv, *, name):
    S = Q.shape[0]
    hq = min(S, 512)
    nh = 2 if S % 1024 == 0 else 1
    tq = nh * hq
    tk = min(S, 512)
    nk = S // tk

    def body(q_ref, k_ref, v_ref, o_ref, l_ref, m_s, l_s, acc):
        kk = pl.program_id(2)

        @pl.when(kk == 0)
        def _():
            m_s[...] = jnp.full_like(m_s, -jnp.inf)
            l_s[...] = jnp.zeros_like(l_s)
            acc[...] = jnp.zeros_like(acc)

        k = k_ref[...]
        v = v_ref[...]
        sts = [lax.dot_general(k, q_ref[hf * hq:(hf + 1) * hq, :], (NT, ((), ())), preferred_element_type=f32)
               for hf in range(nh)]
        for hf in range(nh):
            st = sts[hf]
            m_prev = m_s[hf]
            m_new = jnp.maximum(m_prev, jnp.max(st, axis=0, keepdims=True))
            pt = jnp.exp2(st - m_new)
            alpha = jnp.exp2(m_prev - m_new)
            l_s[hf] = alpha * l_s[hf] + jnp.sum(pt, axis=0, keepdims=True)
            acc[hf] = alpha * acc[hf] + lax.dot_general(v, pt.astype(bf16), (TN, ((), ())), preferred_element_type=f32)
            m_s[hf] = m_new

        @pl.when(kk == nk - 1)
        def _():
            for hf in range(nh):
                o_ref[hf * hq:(hf + 1) * hq, :] = jnp.transpose(acc[hf] / l_s[hf]).astype(bf16)
                l_ref[0, :, hf * hq:(hf + 1) * hq] = m_s[hf] + jnp.log2(l_s[hf])

    return pl.pallas_call(
        body, grid=(MLA_HEADS, S // tq, nk),
        in_specs=[pl.BlockSpec((tq, 256), lambda h, i, k: (i, h)),
                  pl.BlockSpec((tk, 256), lambda h, i, k: (k, h)),
                  pl.BlockSpec((tk, 128), lambda h, i, k: (k, 2 * h + 1))],
        out_specs=[pl.BlockSpec((tq, 128), lambda h, i, k: (i, h)), pl.BlockSpec((1, 1, tq), lambda h, i, k: (h, 0, i))],
        out_shape=[SDS((S, 1024), bf16), SDS((MLA_HEADS, 1, S), f32)],
        scratch_shapes=[pltpu.VMEM((nh, 1, hq), f32), pltpu.VMEM((nh, 1, hq), f32), pltpu.VMEM((nh, 128, hq), f32)],
        compiler_params=_params(("parallel", "parallel", "arbitrary")), name=name)(Q, K, kv)


def _flash_bwd(Q, K, kv, O, L, dO, *, name):
    S = Q.shape[0]
    hq = min(S, 512)
    nh = 2 if S % 1024 == 0 else 1
    tq = nh * hq
    tk = min(S, 512)
    nq = S // tq
    ln2 = math.log(2.0)

    def body(q_ref, k_ref, v_ref, o_ref, l_ref, do_ref, dq_ref, dk_ref, dv_ref, dk_acc, dv_acc):
        kk = pl.program_id(1)
        i = pl.program_id(2)

        @pl.when((kk == 0) & (i == 0))
        def _():
            dq_ref[...] = jnp.zeros_like(dq_ref)

        @pl.when(i == 0)
        def _():
            dk_acc[...] = jnp.zeros_like(dk_acc)
            dv_acc[...] = jnp.zeros_like(dv_acc)

        k = k_ref[...]
        v = v_ref[...]
        ones = jnp.ones((8, 128), f32)
        dk_new = dk_acc[...]
        dv_new = dv_acc[...]
        for hf in range(nh):
            sl = slice(hf * hq, (hf + 1) * hq)
            q = q_ref[sl, :]
            do = do_ref[sl, :]
            st = lax.dot_general(k, q, (NT, ((), ())), preferred_element_type=f32)
            pt = jnp.exp2(st - l_ref[0, :, sl])
            delta = lax.dot_general(ones, do * o_ref[sl, :].astype(f32), (NT, ((), ())),
                                    preferred_element_type=f32, precision=lax.Precision.HIGHEST)[0:1, :]
            dob = do.astype(bf16)
            dv_new = dv_new + lax.dot_general(pt.astype(bf16), dob, (NN, ((), ())), preferred_element_type=f32)
            dpt = lax.dot_general(v, dob, (NT, ((), ())), preferred_element_type=f32)
            dst = (pt * (dpt - delta)).astype(bf16)
            dk_new = dk_new + lax.dot_general(dst, q, (NN, ((), ())), preferred_element_type=f32)
            rows = pl.ds(pl.multiple_of(i * tq + hf * hq, hq), hq)
            dq_ref[rows, :] += lax.dot_general(dst, k, (TN, ((), ())), preferred_element_type=f32)
        dk_acc[...] = dk_new
        dv_acc[...] = dv_new

        @pl.when(i == nq - 1)
        def _():
            dk_ref[...] = dk_acc[...] * ln2
            dv_ref[...] = dv_acc[...]

    return pl.pallas_call(
        body, grid=(MLA_HEADS, S // tk, nq),
        in_specs=[pl.BlockSpec((tq, 256), lambda h, k, i: (i, h)),
                  pl.BlockSpec((tk, 256), lambda h, k, i: (k, h)),
                  pl.BlockSpec((tk, 128), lambda h, k, i: (k, 2 * h + 1)),
                  pl.BlockSpec((tq, 128), lambda h, k, i: (i, h)),
                  pl.BlockSpec((1, 1, tq), lambda h, k, i: (h, 0, i)),
                  pl.BlockSpec((tq, 128), lambda h, k, i: (i, h))],
        out_specs=[pl.BlockSpec((S, 256), lambda h, k, i: (0, h)),
                   pl.BlockSpec((tk, 256), lambda h, k, i: (k, h)),
                   pl.BlockSpec((tk, 128), lambda h, k, i: (k, h))],
        out_shape=[SDS((S, 2048), f32), SDS((S, 2048), f32), SDS((S, 1024), f32)],
        scratch_shapes=[pltpu.VMEM((tk, 256), f32), pltpu.VMEM((tk, 128), f32)],
        compiler_params=_params(("parallel", "arbitrary", "arbitrary"), VMEM_BIG), name=name)(Q, K, kv, O, L, dO)


def _mla_bwd_prep(dQ, dK, dV, cosm, sinm, *, name):
    S = dQ.shape[0]
    T = min(S, 256)
    scale = (MLA_NOPE + MLA_ROPE) ** -0.5

    def body(dq_ref, dk_ref, dv_ref, cos_ref, sin_ref, oq_ref, okv_ref, okr_ref):
        cs, sn = cos_ref[...], sin_ref[...]
        kr = jnp.zeros((T, 128), f32)
        for h in range(MLA_HEADS):
            a = 256 * h
            oq_ref[:, a:a + 128] = (dq_ref[:, a:a + 128] * scale).astype(bf16)
            oq_ref[:, a + 128:a + 256] = (_rope_t(dq_ref[:, a + 128:a + 256], cs, sn, 32) * scale).astype(bf16)
            okv_ref[:, a:a + 128] = dk_ref[:, a:a + 128].astype(bf16)
            okv_ref[:, a + 128:a + 256] = dv_ref[:, 128 * h:128 * h + 128].astype(bf16)
            kr = kr + dk_ref[:, a + 128:a + 256]
        okr_ref[...] = _rope_t(kr, cs, sn, 32)

    return pl.pallas_call(
        body, grid=(S // T,),
        in_specs=[pl.BlockSpec((T, 2048), lambda i: (i, 0)), pl.BlockSpec((T, 2048), lambda i: (i, 0)),
                  pl.BlockSpec((T, 1024), lambda i: (i, 0)), pl.BlockSpec((T, 128), lambda i: (i, 0)),
                  pl.BlockSpec((T, 128), lambda i: (i, 0))],
        out_specs=[pl.BlockSpec((T, 2048), lambda i: (i, 0)), pl.BlockSpec((T, 2048), lambda i: (i, 0)),
                   pl.BlockSpec((T, 128), lambda i: (i, 0))],
        out_shape=[SDS((S, 2048), bf16), SDS((S, 2048), bf16), SDS((S, 128), f32)],
        compiler_params=_params(("parallel",), VMEM_BIG), name=name)(dQ, dK, dV, cosm, sinm)


def _mla_norm_bwd(zm, qg, kvg, dcqn, dckvn, dkr, *, name):
    S = zm.shape[0]
    T = min(S, 512)

    def body(cq_ref, ckv_ref, qg_ref, kvg_ref, dcq_ref, dckv_ref, dkr_ref, o_ref, dqg_ref, dkvg_ref):
        i = pl.program_id(0)
        _, vjp = jax.vjp(_rms, cq_ref[...], qg_ref[...])
        dcq, dqg = vjp(dcq_ref[...])
        _, vjp2 = jax.vjp(_rms, ckv_ref[...], kvg_ref[...])
        dckv, dkvg = vjp2(dckv_ref[...])
        o_ref[:, 0:384] = dcq.astype(bf16)
        o_ref[:, 384:512] = jnp.zeros((T, 128), bf16)
        o_ref[:, 512:768] = dckv.astype(bf16)
        o_ref[:, 768:896] = dkr_ref[...].astype(bf16)

        @pl.when(i == 0)
        def _():
            dqg_ref[...] = dqg
            dkvg_ref[...] = dkvg

        @pl.when(i > 0)
        def _():
            dqg_ref[...] += dqg
            dkvg_ref[...] += dkvg

    return pl.pallas_call(
        body, grid=(S // T,),
        in_specs=[pl.BlockSpec((T, 384), lambda i: (i, 0)), pl.BlockSpec((T, 256), lambda i: (i, 2)),
                  pl.BlockSpec((1, 384), lambda i: (0, 0)), pl.BlockSpec((1, 256), lambda i: (0, 0)),
                  pl.BlockSpec((T, 384), lambda i: (i, 0)), pl.BlockSpec((T, 256), lambda i: (i, 0)),
                  pl.BlockSpec((T, 128), lambda i: (i, 0))],
        out_specs=[pl.BlockSpec((T, 896), lambda i: (i, 0)), pl.BlockSpec((1, 384), lambda i: (0, 0)),
                   pl.BlockSpec((1, 256), lambda i: (0, 0))],
        out_shape=[SDS((S, 896), bf16), SDS((1, 384), f32), SDS((1, 256), f32)],
        compiler_params=_params(("arbitrary",)), name=name)(zm, zm, qg, kvg, dcqn, dckvn, dkr)


def _s5_disc(a_re, a_im, ldt, b_re, b_im):
    dt = jnp.exp(ldt)
    ar = jnp.minimum(a_re, -1e-4)
    mag = jnp.exp(dt * ar)
    abr = mag * jnp.cos(dt * a_im)
    abi = mag * jnp.sin(dt * a_im)
    den = ar * ar + a_im * a_im
    nr = abr - 1.0
    ni = abi
    cr = (nr * ar + ni * a_im) / den
    ci = (ni * ar - nr * a_im) / den
    return abr, abi, cr * b_re - ci * b_im, cr * b_im + ci * b_re


def _s5_param_fwd(a_re, a_im, ldt, b_re, b_im, *, name):
    R = SDS((1, 8192), f32)
    M = SDS((16, 8192), f32)
    Pw = SDS((64, 8192), f32)

    def body(a_re_r, a_im_r, ldt_r, b_re_r, b_im_r, o1, o2, o3, o4, p_re, p_im):
        abr, abi, bbr, bbi = _s5_disc(a_re_r[...], a_im_r[...], ldt_r[...], b_re_r[...], b_im_r[...])
        o1[...] = abr
        o2[...] = abi
        o3[...] = bbr
        o4[...] = bbi
        dt = jnp.exp(ldt_r[...])
        ar = jnp.minimum(a_re_r[...], -1e-4)
        n = lax.broadcasted_iota(jnp.int32, (64, 8192), 0).astype(f32) + 1.0
        mag = jnp.exp(n * (dt * ar))
        ang = n * (dt * a_im_r[...])
        p_re[...] = mag * jnp.cos(ang)
        p_im[...] = mag * jnp.sin(ang)

    return pl.pallas_call(body, out_shape=[R, R, M, M, Pw, Pw], name=name)(a_re, a_im, ldt, b_re, b_im)


def _s5_param_bwd(a_re, a_im, ldt, b_re, b_im, d_abr, d_abi, d_bbr, d_bbi, *, name):
    R = SDS((1, 8192), f32)
    M = SDS((16, 8192), f32)

    def body(a_re_r, a_im_r, ldt_r, b_re_r, b_im_r, c1, c2, c3, c4, o1, o2, o3, o4, o5):
        _, vjp = jax.vjp(_s5_disc, a_re_r[...], a_im_r[...], ldt_r[...], b_re_r[...], b_im_r[...])
        g = vjp((c1[...], c2[...], c3[...], c4[...]))
        for o, v in zip((o1, o2, o3, o4, o5), g):
            o[...] = v

    return pl.pallas_call(body, out_shape=[R, R, R, M, M], name=name)(a_re, a_im, ldt, b_re, b_im, d_abr, d_abi, d_bbr, d_bbi)


def _seg_perm(T, inverse):
    L = T // S5_SEG
    i = jnp.arange(T)
    src = (i % S5_SEG) * L + i // S5_SEG
    P = (src[:, None] == jnp.arange(T)[None, :]).astype(bf16)
    return P.T if inverse else P


def _perm_rows(a, P, *, name):
    S, W = a.shape
    T = P.shape[0]

    def body(p_ref, a_ref, o_ref):
        o_ref[...] = lax.dot_general(p_ref[...], a_ref[...], (NN, ((), ())), preferred_element_type=f32).astype(o_ref.dtype)

    return pl.pallas_call(
        body, grid=(S // T,), in_specs=[pl.BlockSpec((T, T), lambda i: (0, 0)), pl.BlockSpec((T, W), lambda i: (i, 0))],
        out_specs=pl.BlockSpec((T, W), lambda i: (i, 0)), out_shape=SDS((S, W), a.dtype),
        compiler_params=_params(("parallel",)), name=name)(P, a)


def _scan_core(xr, xi, ar, ai, pwr_ref, pwi_ref, a64r, a64i, carry, *, reverse, T, conj):
    L = T // S5_SEG
    sg = -1.0 if conj else 1.0
    arb = jnp.broadcast_to(ar, (8, 512))
    aib = jnp.broadcast_to(ai, (8, 512))
    UN = 4

    def step(r4, c):
        cr, ci = c
        for u in range(UN):
            r0 = r4 * UN + u
            r = (L - 1 - r0) if reverse else r0
            rows = pl.ds(pl.multiple_of(r * 8, 8), 8)
            nr = arb * cr - aib * ci + xr[rows, :]
            ni = arb * ci + aib * cr + xi[rows, :]
            xr[rows, :] = nr
            xi[rows, :] = ni
            cr, ci = nr, ni
        return cr, ci

    lr, li = lax.fori_loop(0, L // UN, step, (jnp.zeros((8, 512), f32), jnp.zeros((8, 512), f32)))
    row8 = lax.broadcasted_iota(jnp.int32, (8, 512), 0)
    cr = carry[0, 0:1, :]
    ci = carry[1, 0:1, :]
    a6i = sg * a64i
    cin_r = jnp.zeros((8, 512), f32)
    cin_i = jnp.zeros((8, 512), f32)
    for seg in (range(S5_SEG - 1, -1, -1) if reverse else range(S5_SEG)):
        cin_r = jnp.where(row8 == seg, cr, cin_r)
        cin_i = jnp.where(row8 == seg, ci, cin_i)
        ncr = lr[seg:seg + 1, :] + a64r * cr - a6i * ci
        nci = li[seg:seg + 1, :] + a64r * ci + a6i * cr
        cr, ci = ncr, nci
    carry[0, 0:1, :] = cr
    carry[1, 0:1, :] = ci

    def fix(r4, _):
        for u in range(UN):
            r = r4 * UN + u
            rows = pl.ds(pl.multiple_of(r * 8, 8), 8)
            pr = pwr_ref[pl.ds(r, 1), :]
            pi = sg * pwi_ref[pl.ds(r, 1), :]
            xr[rows, :] += pr * cin_r - pi * cin_i
            xi[rows, :] += pr * cin_i + pi * cin_r
        return 0

    lax.fori_loop(0, L // UN, fix, 0)


def _s5_scan_fwd(u, BBr, BBi, CCr, CCi, abr, abi, pwr, pwi, *, reverse, name):
    S = u.shape[0]
    T = min(S, 512)
    NB = S // T
    L = T // S5_SEG
    d = 1 if reverse else 0

    def tb(b):
        return (NB - 1 - b) if reverse else b

    def body(u_ref, bbr_ref, bbi_ref, ccr_ref, cci_ref, ar_ref, ai_ref, pwr_ref, pwi_ref, y_ref, xr_ref, xi_ref, carry):
        b = pl.program_id(1)

        @pl.when(b == 0)
        def _():
            carry[...] = jnp.zeros_like(carry)

        ub = u_ref[...].astype(bf16)
        xr_ref[...] = lax.dot_general(ub, bbr_ref[0, 0], (NN, ((), ())), preferred_element_type=f32)
        xi_ref[...] = lax.dot_general(ub, bbi_ref[0, 0], (NN, ((), ())), preferred_element_type=f32)
        a6 = (0 if reverse else L - 1)
        _scan_core(xr_ref, xi_ref, ar_ref[...], ai_ref[...], pwr_ref, pwi_ref, pwr_ref[a6:a6 + 1, :], pwi_ref[a6:a6 + 1, :],
                   carry, reverse=reverse, T=T, conj=False)
        y_ref[...] = _dot(xr_ref[...], ccr_ref[0, 0], NN) - _dot(xi_ref[...], cci_ref[0, 0], NN)

    mat = lambda shp: pl.BlockSpec((1, 1) + shp, lambda j, b: (d, j, 0, 0))
    vec = lambda r: pl.BlockSpec((r, 512), lambda j, b: (0, d * S5_NJ + j))
    return pl.pallas_call(
        body, grid=(S5_NJ, NB),
        in_specs=[pl.BlockSpec((T, 128), lambda j, b: (tb(b), j)), mat((128, 512)), mat((128, 512)), mat((512, 128)),
                  mat((512, 128)), vec(1), vec(1), vec(L), vec(L)],
        out_specs=[pl.BlockSpec((T, 128), lambda j, b: (tb(b), j)), pl.BlockSpec((T, 512), lambda j, b: (tb(b), j)),
                   pl.BlockSpec((T, 512), lambda j, b: (tb(b), j))],
        out_shape=[SDS((S, 1024), f32), SDS((S, 4096), f32), SDS((S, 4096), f32)],
        scratch_shapes=[pltpu.VMEM((2, 8, 512), f32)],
        compiler_params=_params(("parallel", "arbitrary")), name=name)(u, BBr, BBi, CCr, CCi, abr, abi, pwr, pwi)


def _s5_scan_bwd(u, dy, xr, xi, BBr, BBi, CCr, CCi, abr, abi, pwr, pwi, *, reverse, name):
    S = u.shape[0]
    T = min(S, 512)
    NB = S // T
    L = T // S5_SEG
    d = 1 if reverse else 0
    adj_rev = not reverse

    def tb(b):
        return b if reverse else (NB - 1 - b)

    def bnd(b):
        t = tb(b)
        if reverse:
            return jnp.minimum((t + 1) * (T // 8), S // 8 - 1)
        return jnp.maximum(t * (T // 8) - 1, 0)

    def body(u_ref, dy_ref, xr_ref, xi_ref, xbr_ref, xbi_ref, bbr_ref, bbi_ref, ccr_ref, cci_ref, ar_ref, ai_ref,
             pwr_ref, pwi_ref, du_ref, dbbr_ref, dbbi_ref, dccr_ref, dcci_ref, dar_ref, dai_ref, carry, lam):
        b = pl.program_id(1)

        @pl.when(b == 0)
        def _():
            carry[...] = jnp.zeros_like(carry)
            for r in (dbbr_ref, dbbi_ref, dccr_ref, dcci_ref, dar_ref, dai_ref):
                r[...] = jnp.zeros_like(r)

        dyb = dy_ref[...]
        lam[0] = lax.dot_general(dyb, ccr_ref[0, 0], (NT, ((), ())), preferred_element_type=f32)
        lam[1] = -lax.dot_general(dyb, cci_ref[0, 0], (NT, ((), ())), preferred_element_type=f32)
        a6 = (0 if adj_rev else L - 1)
        _scan_core(lam.at[0], lam.at[1], ar_ref[...], -ai_ref[...], pwr_ref, pwi_ref, pwr_ref[a6:a6 + 1, :],
                   pwi_ref[a6:a6 + 1, :], carry, reverse=adj_rev, T=T, conj=True)
        ub = u_ref[...].astype(bf16)
        first = (b == NB - 1)
        lrb = lam[0].astype(bf16)
        lib = lam[1].astype(bf16)
        du_ref[...] = lax.dot_general(lrb, bbr_ref[0, 0], (NT, ((), ())), preferred_element_type=f32) \
            + lax.dot_general(lib, bbi_ref[0, 0], (NT, ((), ())), preferred_element_type=f32)
        dbbr_ref[0, 0] += lax.dot_general(ub, lrb, (TN, ((), ())), preferred_element_type=f32)
        dbbi_ref[0, 0] += lax.dot_general(ub, lib, (TN, ((), ())), preferred_element_type=f32)
        dccr_ref[0, 0] += lax.dot_general(dyb, xr_ref[...].astype(bf16), (TN, ((), ())), preferred_element_type=f32)
        dcci_ref[0, 0] -= lax.dot_general(dyb, xi_ref[...].astype(bf16), (TN, ((), ())), preferred_element_type=f32)
        row8 = lax.broadcasted_iota(jnp.int32, (8, 512), 0)
        if reverse:
            body_x, body_l, edge_l = slice(8, T), slice(0, T - 8), slice(T - 8, T)
            sp_r = jnp.where(row8 == 7, jnp.where(first, 0.0, xbr_ref[0:1, :]), pltpu.roll(xr_ref[0:8, :], 7, axis=0))
            sp_i = jnp.where(row8 == 7, jnp.where(first, 0.0, xbi_ref[0:1, :]), pltpu.roll(xi_ref[0:8, :], 7, axis=0))
        else:
            body_x, body_l, edge_l = slice(0, T - 8), slice(8, T), slice(0, 8)
            sp_r = jnp.where(row8 == 0, jnp.where(first, 0.0, xbr_ref[7:8, :]), pltpu.roll(xr_ref[T - 8:T, :], 1, axis=0))
            sp_i = jnp.where(row8 == 0, jnp.where(first, 0.0, xbi_ref[7:8, :]), pltpu.roll(xi_ref[T - 8:T, :], 1, axis=0))
        xpr, xpi = xr_ref[body_x, :], xi_ref[body_x, :]
        lr, li = lam[0, body_l, :], lam[1, body_l, :]
        er, ei = lam[0, edge_l, :], lam[1, edge_l, :]
        dar_ref[...] += jnp.sum(xpr * lr + xpi * li, axis=0, keepdims=True) + jnp.sum(sp_r * er + sp_i * ei, axis=0, keepdims=True)
        dai_ref[...] += jnp.sum(xpr * li - xpi * lr, axis=0, keepdims=True) + jnp.sum(sp_r * ei - sp_i * er, axis=0, keepdims=True)

    mat = lambda shp: pl.BlockSpec((1, 1) + shp, lambda j, b: (d, j, 0, 0))
    omat = lambda shp: pl.BlockSpec((1, 1) + shp, lambda j, b: (0, j, 0, 0))
    vec = lambda r: pl.BlockSpec((r, 512), lambda j, b: (0, d * S5_NJ + j))
    blk = lambda w: pl.BlockSpec((T, w), lambda j, b: (tb(b), j))
    return pl.pallas_call(
        body, grid=(S5_NJ, NB),
        in_specs=[blk(128), blk(128), blk(512), blk(512),
                  pl.BlockSpec((8, 512), lambda j, b: (bnd(b), j)), pl.BlockSpec((8, 512), lambda j, b: (bnd(b), j)),
                  mat((128, 512)), mat((128, 512)), mat((512, 128)), mat((512, 128)), vec(1), vec(1), vec(L), vec(L)],
        out_specs=[blk(128), omat((128, 512)), omat((128, 512)), omat((128, 512)), omat((128, 512)),
                   pl.BlockSpec((1, 512), lambda j, b: (0, j)), pl.BlockSpec((1, 512), lambda j, b: (0, j))],
        out_shape=[SDS((S, 1024), f32), SDS((1, 8, 128, 512), f32), SDS((1, 8, 128, 512), f32), SDS((1, 8, 128, 512), f32),
                   SDS((1, 8, 128, 512), f32), SDS((1, 4096), f32), SDS((1, 4096), f32)],
        scratch_shapes=[pltpu.VMEM((2, 8, 512), f32), pltpu.VMEM((2, T, 512), f32)],
        compiler_params=_params(("parallel", "arbitrary"), VMEM_BIG), name=name)(
            u, dy, xr, xi, xr, xi, BBr, BBi, CCr, CCi, abr, abi, pwr, pwi)


def _silu_mul(g, u):
    return jax.nn.silu(g) * u


def _mixf(p0, p1, p2, z0, z1, z2):
    return jax.nn.sigmoid(z0) * p0 + jax.nn.sigmoid(z1) * p1 + jax.nn.sigmoid(z2) * p2


def _s5_act(yf, yb, u, dd):
    return jax.nn.gelu(yf + yb + dd * u)


def _glu(a, b):
    return a * jax.nn.sigmoid(b)


def _layer_fwd(x, w, tabs, l):
    S = x.shape[0]
    T = min(S, 512)
    I = S // T
    nm = lambda s: f"L{l}_{s}"
    sv = {'x': x}
    h = _rmsnorm_fwd(x, w['norm1_g'], name=nm("norm1"))
    zr = _mm(h, w['W_ret'], name=nm("in_ret"))
    zm = _mm(h, w['W_mla'], name=nm("in_mla"))
    h_seg = _perm_rows(h, tabs['seg_perm'], name=nm("s5_perm_h"))
    zs = _mm(h_seg, w['W_s5'], name=nm("in_s5"))
    zg = _mm(h, w['W_gate'], name=nm("in_gate"))
    sv.update(h=h, h_seg=h_seg, zr=zr, zm=zm, zs=zs, zg=zg)

    yf, stf = _ret_dir_fwd(zr, w['lg'], tabs['cos_r'], tabs['sin_r'], reverse=False, name=nm("ret_f"))
    yb, stb = _ret_dir_fwd(zr, w['lg'], tabs['cos_r'], tabs['sin_r'], reverse=True, name=nm("ret_b"))
    hd = lambda j: j
    y_ret = _pw(_gn_gate, [yf, yb, zr, w['ret_gn_g']],
                [_row(T, 256, hd), _row(T, 256, hd), _row(T, 256, lambda j: 8 + j), _par(256, hd)],
                [SDS((S, 1024), bf16)], [_row(T, 256, hd)], (RET_HEADS, I), name=nm("ret_gn"))[0]
    sv.update(yf=yf, yb=yb, stf=stf, stb=stb, y_ret=y_ret)

    cqn, ckvn = _pw(lambda a, b, g1, g2: (_rms(a, g1), _rms(b, g2)), [zm, zm, w['mla_q_norm_g'], w['mla_kv_norm_g']],
                    [_row(T, 384), _row(T, 256, lambda j: 2), _par(384), _par(256)],
                    [SDS((S, 384), bf16), SDS((S, 256), bf16)], [_row(T, 384), _row(T, 256)], (1, I), name=nm("mla_norm"))
    q = _mm(cqn, w['W_uq'], name=nm("mla_uq"))
    kv = _mm(ckvn, w['W_ukv'], out_dtype=bf16, name=nm("mla_ukv"))
    sc = (MLA_NOPE + MLA_ROPE) ** -0.5 * math.log2(math.e)
    Q = _pw(lambda xq, cs, sn: jnp.concatenate([xq[:, :128] * sc, _rope(xq[:, 128:], cs, sn, 32) * sc], axis=1),
            [q, tabs['cos_m'], tabs['sin_m']], [_row(T, 256, hd), _row(T, 128), _row(T, 128)],
            [SDS((S, 2048), bf16)], [_row(T, 256, hd)], (MLA_HEADS, I), name=nm("mla_qprep"))[0]
    K = _pw(lambda kn, kr, cs, sn: jnp.concatenate([kn.astype(f32), _rope(kr, cs, sn, 32)], axis=1),
            [kv, zm, tabs['cos_m'], tabs['sin_m']],
            [_row(T, 128, lambda j: 2 * j), _row(T, 128, lambda j: 6), _row(T, 128), _row(T, 128)],
            [SDS((S, 2048), bf16)], [_row(T, 256, hd)], (MLA_HEADS, I), name=nm("mla_kprep"))[0]
    O, Lse = _flash_fwd(Q, K, kv, name=nm("mla_attn"))
    sv.update(cqn=cqn, ckvn=ckvn, kv=kv, Q=Q, K=K, O=O, Lse=Lse)

    s5 = w['s5']
    ysf, xrf, xif = _s5_scan_fwd(zs, s5['BBr'], s5['BBi'], s5['CCr'], s5['CCi'], s5['abr'], s5['abi'], s5['pwr_f'], s5['pwi_f'],
                                 reverse=False, name=nm("s5_f"))
    ysb, xrb, xib = _s5_scan_fwd(zs, s5['BBr'], s5['BBi'], s5['CCr'], s5['CCi'], s5['abr'], s5['abi'], s5['pwr_f'], s5['pwi_f'],
                                 reverse=True, name=nm("s5_b"))
    gact = _pw(_s5_act, [ysf, ysb, zs, w['s5_d']], [_row(T, D), _row(T, D), _row(T, D), _par(D)],
               [SDS((S, D), bf16)], [_row(T, D)], (1, I), name=nm("s5_act"))[0]
    gg = _mm(gact, w['W_glu'], name=nm("s5_glu_mm"))
    y_s5 = _pw(_glu, [gg, gg], [_row(T, D), _row(T, D, lambda j: 1)], [SDS((S, D), bf16)], [_row(T, D)], (1, I),
               name=nm("s5_glu"))[0]
    y_s5 = _perm_rows(y_s5, tabs['seg_unperm'], name=nm("s5_unperm_y"))
    sv.update(ysf=ysf, ysb=ysb, xrf=xrf, xif=xif, xrb=xrb, xib=xib, gact=gact, gg=gg, y_s5=y_s5)

    ys = [y_ret, O, y_s5]
    pr = [_mm(ys[i], w['W_br'][i], name=nm(f"branch{i}")) for i in range(3)]
    mix = _pw(_mixf, pr + [zg, zg, zg],
              [_row(T, D)] * 3 + [_row(T, D), _row(T, D, lambda j: 1), _row(T, D, lambda j: 2)],
              [SDS((S, D), bf16)], [_row(T, D)], (1, I), name=nm("mix"))[0]
    x1 = _mm(mix, w['W_out'], res=x, name=nm("out_proj"))
    h2 = _rmsnorm_fwd(x1, w['norm2_g'], name=nm("norm2"))
    fgu = _mm(h2, w['W_gu'], name=nm("ffn_gu"))
    act = _pw(_silu_mul, [fgu, fgu], [_row(T, 1408, lambda j: j), _row(T, 1408, lambda j: 2 + j)],
              [SDS((S, FFN_H), bf16)], [_row(T, 1408, lambda j: j)], (2, I), name=nm("ffn_act"))[0]
    x2 = _mm(act, w['W_down'], res=x1, name=nm("ffn_down"))
    sv.update(pr=pr, mix=mix, x1=x1, h2=h2, fgu=fgu, act=act)
    return x2, sv


def _vjp_fn(fn, n_primal, cast=None):
    def g(*args):
        _, vjp = jax.vjp(fn, *args[:n_primal])
        return vjp(args[n_primal].astype(f32))
    return g


def _layer_bwd(dx2, w, tabs, sv, l):
    S = dx2.shape[0]
    T = min(S, 512)
    I = S // T
    nm = lambda s: f"L{l}_b_{s}"
    g = {}
    hd = lambda j: j

    dact = _mm(dx2, w['W_down'], tb=True, name=nm("ffn_down_dx"))
    g['W_down'] = _mmT(sv['act'], dx2, name=nm("ffn_down_dw"))
    dfg, dfu = _pw(_vjp_fn(_silu_mul, 2), [sv['fgu'], sv['fgu'], dact],
                   [_row(T, 1408, lambda j: j), _row(T, 1408, lambda j: 2 + j), _row(T, 1408, lambda j: j)],
                   [SDS((S, FFN_H), bf16), SDS((S, FFN_H), bf16)], [_row(T, 1408, lambda j: j)] * 2, (2, I), name=nm("ffn_act"))
    dfgu = jnp.concatenate([dfg, dfu], axis=1)
    g['W_gu'] = _mmT(sv['h2'], dfgu, name=nm("ffn_gu_dw"))
    dh2 = _mm(dfgu, w['W_gu'], tb=True, name=nm("ffn_gu_dx"))
    dx1, g['norm2_g'] = _rmsnorm_bwd(sv['x1'], w['norm2_g'], dh2, dx2, name=nm("norm2"))

    dmix = _mm(dx1, w['W_out'], tb=True, name=nm("out_dx"))
    g['W_out'] = _mmT(sv['mix'], dx1, name=nm("out_dw"))
    zg = sv['zg']
    outs = _pw(_vjp_fn(_mixf, 6), sv['pr'] + [zg, zg, zg, dmix],
               [_row(T, D)] * 3 + [_row(T, D), _row(T, D, lambda j: 1), _row(T, D, lambda j: 2), _row(T, D)],
               [SDS((S, D), bf16)] * 6, [_row(T, D)] * 6, (1, I), name=nm("mix"))
    dpr, dzg = outs[:3], jnp.concatenate(outs[3:], axis=1)
    ys = [sv['y_ret'], sv['O'], sv['y_s5']]
    g['W_br'] = [_mmT(ys[i], dpr[i], name=nm(f"branch{i}_dw")) for i in range(3)]
    dpr_seg = _perm_rows(dpr[2], tabs['seg_perm'], name=nm("s5_perm_dy"))
    dys = [_mm(dpr[i] if i < 2 else dpr_seg, w['W_br'][i], tb=True, name=nm(f"branch{i}_dx")) for i in range(3)]

    gg = sv['gg']
    dga, dgb = _pw(_vjp_fn(_glu, 2), [gg, gg, dys[2]], [_row(T, D), _row(T, D, lambda j: 1), _row(T, D)],
                   [SDS((S, D), bf16)] * 2, [_row(T, D)] * 2, (1, I), name=nm("s5_glu"))
    dgg = jnp.concatenate([dga, dgb], axis=1)
    g['W_glu'] = _mmT(sv['gact'], dgg, name=nm("s5_glu_dw"))
    dgact = _mm(dgg, w['W_glu'], tb=True, name=nm("s5_glu_dx"))

    def act_bwd(yf, yb, u, dd, ct):
        _, vjp = jax.vjp(_s5_act, yf, yb, u, dd)
        dyf, _, du, ddd = vjp(ct)
        return dyf, du, ddd

    dys5, du_direct, g['s5_d'] = _pw(act_bwd, [sv['ysf'], sv['ysb'], sv['zs'], w['s5_d'], dgact],
                                     [_row(T, D)] * 3 + [_par(D), _row(T, D)],
                                     [SDS((S, D), bf16), SDS((S, D), f32), SDS((1, D), f32)],
                                     [_row(T, D), _row(T, D), _par(D)], (1, I), n_acc=1, name=nm("s5_act"))
    s5 = w['s5']
    rf = _s5_scan_bwd(sv['zs'], dys5, sv['xrf'], sv['xif'], s5['BBr'], s5['BBi'], s5['CCr'], s5['CCi'], s5['abr'], s5['abi'],
                      s5['pwr_a'], s5['pwi_a'], reverse=False, name=nm("s5_f"))
    rb = _s5_scan_bwd(sv['zs'], dys5, sv['xrb'], sv['xib'], s5['BBr'], s5['BBi'], s5['CCr'], s5['CCi'], s5['abr'], s5['abi'],
                      s5['pwr_a'], s5['pwi_a'], reverse=True, name=nm("s5_b"))
    g['s5'] = (rf[1:], rb[1:])
    dzs_seg = _pw(lambda a, b, c: a + b + c, [du_direct, rf[0], rb[0]], [_row(T, D)] * 3, [SDS((S, D), bf16)], [_row(T, D)],
                  (1, I), name=nm("s5_du"))[0]
    dzs = _perm_rows(dzs_seg, tabs['seg_unperm'], name=nm("s5_unperm_dz"))

    dQ, dK, dV = _flash_bwd(sv['Q'], sv['K'], sv['kv'], sv['O'], sv['Lse'], dys[1], name=nm("mla_attn"))
    dq_lin, dkv, dkr = _mla_bwd_prep(dQ, dK, dV, tabs['cos_m'], tabs['sin_m'], name=nm("mla_prep"))
    g['W_uq'] = _mmT(sv['cqn'], dq_lin, name=nm("mla_uq_dw"))
    dcqn = _mm(dq_lin, w['W_uq'], tb=True, name=nm("mla_uq_dx"))
    g['W_ukv'] = _mmT(sv['ckvn'], dkv, name=nm("mla_ukv_dw"))
    dckvn = _mm(dkv, w['W_ukv'], tb=True, name=nm("mla_ukv_dx"))
    dzm, g['mla_q_norm_g'], g['mla_kv_norm_g'] = _mla_norm_bwd(sv['zm'], w['mla_q_norm_g'], w['mla_kv_norm_g'], dcqn, dckvn, dkr,
                                                               name=nm("mla_norm"))

    zr = sv['zr']

    def gn_bwd(yf, yb, gt, gn, ct):
        _, vjp = jax.vjp(_gn_gate, yf, yb, gt, gn)
        dyf, _, dgt, dgn = vjp(ct)
        return dyf, dgt, dgn

    dyr, dgate, g['ret_gn_g'] = _pw(gn_bwd, [sv['yf'], sv['yb'], zr, w['ret_gn_g'], dys[0]],
                                    [_row(T, 256, hd), _row(T, 256, hd), _row(T, 256, lambda j: 8 + j), _par(256, hd),
                                     _row(T, 256, hd)],
                                    [SDS((S, 1024), bf16), SDS((S, 1024), bf16), SDS((1, 1024), f32)],
                                    [_row(T, 256, hd), _row(T, 256, hd), _par(256, hd)], (RET_HEADS, I), n_acc=1, name=nm("ret_gn"))
    qf, kf, vf, lgf = _ret_dir_bwd(zr, w['lg'], tabs['cos_r'], tabs['sin_r'], dyr, sv['stf'], reverse=False, name=nm("ret_f"))
    qb, kb, vb, lgb = _ret_dir_bwd(zr, w['lg'], tabs['cos_r'], tabs['sin_r'], dyr, sv['stb'], reverse=True, name=nm("ret_b"))
    g['lg'] = jnp.stack([lgf[:, 0, 0], lgb[:, 0, 0]])
    add2 = lambda a, b: a + b
    dq = _pw(add2, [qf, qb], [_row(T, 512)] * 2, [SDS((S, 512), bf16)], [_row(T, 512)], (1, I), name=nm("ret_dq"))[0]
    dk = _pw(add2, [kf, kb], [_row(T, 512)] * 2, [SDS((S, 512), bf16)], [_row(T, 512)], (1, I), name=nm("ret_dk"))[0]
    dv = _pw(add2, [vf, vb], [_row(T, D)] * 2, [SDS((S, D), bf16)], [_row(T, D)], (1, I), name=nm("ret_dv"))[0]
    dzr = jnp.concatenate([dq, dk, dv, dgate], axis=1)

    h = sv['h']
    g['W_ret'] = _mmT(h, dzr, name=nm("in_ret_dw"))
    g['W_mla'] = _mmT(h, dzm, name=nm("in_mla_dw"))
    g['W_s5'] = _mmT(sv['h_seg'], dzs_seg, name=nm("in_s5_dw"))
    g['W_gate'] = _mmT(h, dzg, name=nm("in_gate_dw"))
    dh = _mm(dzr, w['W_ret'], tb=True, name=nm("in_ret_dx"))
    dh = _mm(dzm, w['W_mla'], tb=True, res=dh, name=nm("in_mla_dx"))
    dh = _mm(dzs, w['W_s5'], tb=True, res=dh, name=nm("in_s5_dx"))
    dh = _mm(dzg, w['W_gate'], tb=True, res=dh, name=nm("in_gate_dx"))
    dx, g['norm1_g'] = _rmsnorm_bwd(sv['x'], w['norm1_g'], dh, dx1, name=nm("norm1"))
    return dx, g


def _loss_head(x, tgt, gain, *, name):
    S, W = x.shape
    T = min(S, 512)

    def loss_fn(xv, gv, tv):
        return 0.5 * jnp.sum(jnp.mean(jnp.square(_rms(xv, gv) - tv), axis=-1, keepdims=True), axis=0, keepdims=True)

    def fn(xv, gv, tv):
        lv, vjp = jax.vjp(lambda a, b: loss_fn(a, b, tv), xv, gv)
        dx, dg = vjp(jnp.ones((1, 1), f32))
        return dx, jnp.broadcast_to(lv, (1, 128)), dg

    return _pw(fn, [x, gain, tgt], [_row(T, W), _par(W), _row(T, W)],
               [SDS((S, W), f32), SDS((1, 128), f32), SDS((1, W), f32)], [_row(T, W), _par(128), _par(W)],
               (1, S // T), n_acc=2, name=name)


def _rope_tabs(S):
    def tab(dim):
        inv = 1.0 / (ROPE_THETA ** (jnp.arange(0, dim, 2, dtype=f32) / dim))
        ang = jnp.arange(S, dtype=f32)[:, None] * inv[None, :]
        return jnp.cos(ang), jnp.sin(ang)

    cr, sr = tab(RET_DK)
    cm, sm = tab(MLA_ROPE)
    z = jnp.zeros((S, 64), f32)
    return {'cos_r': jnp.concatenate([cr, cr], axis=1), 'sin_r': jnp.concatenate([-sr, sr], axis=1),
            'cos_m': jnp.concatenate([cm, cm, z], axis=1), 'sin_m': jnp.concatenate([-sm, sm, z], axis=1),
            'seg_perm': _seg_perm(512, False), 'seg_unperm': _seg_perm(512, True)}


def _bd_B(bb):
    b5 = bb.reshape(16, 2, 8, 8, 64)
    return jnp.einsum('cdjgp,gh->djgchp', b5, jnp.eye(8, dtype=bb.dtype)).reshape(2, 8, 128, 512)


def _bd_B_t(dBB):
    return jnp.einsum('djgcgp->cdjgp', dBB.reshape(2, 8, 8, 16, 8, 64)).reshape(16, 8192)


def _bd_C(c):
    c5 = c.reshape(2, 8, 8, 16, 64)
    return jnp.einsum('djgcp,gh->djgphc', c5, jnp.eye(8, dtype=c.dtype)).reshape(2, 8, 512, 128)


def _s5_rows(p, l):
    a_re = p['s5_a_re'][l].reshape(1, 8192)
    a_im = p['s5_a_im'][l].reshape(1, 8192)
    ldt = jnp.broadcast_to(p['s5_log_dt'][l][:, :, None], (2, S5_G, S5_P)).reshape(1, 8192)
    b_re = p['s5_b_re'][l].transpose(3, 0, 1, 2).reshape(16, 8192)
    b_im = p['s5_b_im'][l].transpose(3, 0, 1, 2).reshape(16, 8192)
    return a_re, a_im, ldt, b_re, b_im


def _layer_weights(big, p, l):
    w_in = big['w_in'][l]
    z = lambda n: jnp.zeros((D, n), w_in.dtype)
    w = {
        'W_ret': w_in[:, 0:3072],
        'W_mla': jnp.concatenate([w_in[:, 3072:3456], z(128), w_in[:, 3456:3712], w_in[:, 3712:3776], z(64)], axis=1),
        'W_s5': w_in[:, 3776:4800],
        'W_gate': w_in[:, 4800:7872],
        'W_uq': jnp.pad(big['mla_w_uq'][l].reshape(MLA_Q_LORA, MLA_HEADS, 192), ((0, 0), (0, 0), (0, 64))).reshape(MLA_Q_LORA, 2048),
        'W_ukv': big['mla_w_ukv'][l],
        'W_glu': big['s5_w_glu'][l],
        'W_br': [big['w_branch'][l, i] for i in range(3)],
        'W_out': big['w_out'][l],
        'W_gu': big['ffn_w_gu'][l],
        'W_down': big['ffn_w_down'][l],
    }
    for n in ('norm1_g', 'ret_gn_g', 'mla_q_norm_g', 'mla_kv_norm_g', 's5_d', 'norm2_g'):
        w[n] = p[n][l][None, :]
    w['lg'] = jax.nn.log_sigmoid(p['ret_decay'][l])
    rows = _s5_rows(p, l)
    abr, abi, bbr, bbi, pwr, pwi = _s5_param_fwd(*rows, name=f"L{l}_s5_param")
    flip = lambda t, first: jnp.concatenate([t[::-1, :4096], t[:, 4096:]] if first else [t[:, :4096], t[::-1, 4096:]], axis=1)
    w['s5'] = {'abr': abr, 'abi': abi, 'BBr': _bd_B(bbr).astype(bf16), 'BBi': _bd_B(bbi).astype(bf16),
               'CCr': _bd_C(p['s5_c_re'][l]).astype(bf16), 'CCi': _bd_C(p['s5_c_im'][l]).astype(bf16),
               'pwr_f': flip(pwr, False), 'pwi_f': flip(pwi, False), 'pwr_a': flip(pwr, True), 'pwi_a': flip(pwi, True),
               'rows': rows}
    return w


def _layer_grads(g, w, p, l):
    out = {}
    m = g['W_mla']
    out['w_in'] = jnp.concatenate([g['W_ret'], m[:, 0:384], m[:, 512:768], m[:, 768:832], g['W_s5'], g['W_gate']], axis=1)
    out['mla_w_uq'] = g['W_uq'].reshape(MLA_Q_LORA, MLA_HEADS, 256)[:, :, :192].reshape(MLA_Q_LORA, 1536)
    out['mla_w_ukv'] = g['W_ukv']
    out['s5_w_glu'] = g['W_glu']
    out['w_branch'] = jnp.stack(g['W_br'])
    out['w_out'] = g['W_out']
    out['ffn_w_gu'] = g['W_gu']
    out['ffn_w_down'] = g['W_down']
    for n in ('norm1_g', 'ret_gn_g', 'mla_q_norm_g', 'mla_kv_norm_g', 's5_d', 'norm2_g'):
        out[n] = g[n][0]
    out['ret_decay'] = g['lg'] * jax.nn.sigmoid(-p['ret_decay'][l])
    (fB_r, fB_i, fC_r, fC_i, fa_r, fa_i), (bB_r, bB_i, bC_r, bC_i, ba_r, ba_i) = g['s5']
    cat = lambda a, b: jnp.concatenate([a, b], axis=0)
    d_bbr = _bd_B_t(cat(fB_r, bB_r))
    d_bbi = _bd_B_t(cat(fB_i, bB_i))
    to_c = lambda t: _bd_B_t(t).reshape(16, 2, S5_G, S5_P).transpose(1, 2, 0, 3)
    out['s5_c_re'] = to_c(cat(fC_r, bC_r))
    out['s5_c_im'] = to_c(cat(fC_i, bC_i))
    d_abr = jnp.concatenate([fa_r, ba_r], axis=1)
    d_abi = jnp.concatenate([fa_i, ba_i], axis=1)
    da_re, da_im, dldt, db_re, db_im = _s5_param_bwd(*w['s5']['rows'], d_abr, d_abi, d_bbr, d_bbi, name=f"L{l}_b_s5_param")
    out['s5_a_re'] = da_re.reshape(2, S5_G, S5_P)
    out['s5_a_im'] = da_im.reshape(2, S5_G, S5_P)
    out['s5_log_dt'] = dldt.reshape(2, S5_G, S5_P).sum(axis=-1)
    out['s5_b_re'] = db_re.reshape(16, 2, S5_G, S5_P).transpose(1, 2, 3, 0)
    out['s5_b_im'] = db_im.reshape(16, 2, S5_G, S5_P).transpose(1, 2, 3, 0)
    return out


def _local_step(x, tgt, big, p):
    S = x.shape[0]
    assert S % 512 == 0
    tabs = _rope_tabs(S)
    ws, svs = [], []
    h = x
    for l in range(DEPTH):
        w = _layer_weights(big, p, l)
        h, sv = _layer_fwd(h, w, tabs, l)
        ws.append(w)
        svs.append(sv)
    dx, lossv, dfinal = _loss_head(h, tgt, p['final_g'][None, :], name="loss_head")
    per_layer = [None] * DEPTH
    for l in reversed(range(DEPTH)):
        dx, g = _layer_bwd(dx, ws[l], tabs, svs[l], l)
        per_layer[l] = _layer_grads(g, ws[l], p, l)
    grads = {n: jnp.stack([per_layer[l][n] for l in range(DEPTH)]) for n in per_layer[0]}
    grads['final_g'] = dfinal[0]
    return lossv[0, 0], dx, grads


_ANY = pl.BlockSpec(memory_space=pl.ANY)


def _place():
    x, y, c = lax.axis_index("x"), lax.axis_index("y"), lax.axis_index("c")
    return x, y, c, [(1 - x, y), (x, 1 - y), (1 - x, 1 - y)]


def _allgather4(arrs, *, name):
    n = len(arrs)

    def body(*refs):
        ins, outs = refs[:n], refs[n:2 * n]
        send, recv, loc = refs[2 * n:]
        x, y, c, chips = _place()
        me = 2 * x + y

        def remote(a, k, slot):
            px, py = chips[k]
            return pltpu.make_async_remote_copy(src_ref=ins[a], dst_ref=outs[a].at[slot], send_sem=send.at[a, k],
                                                recv_sem=recv.at[a, k], device_id=(px, py, c), device_id_type=MESH)

        mine = [pltpu.make_async_copy(ins[a], outs[a].at[me], loc.at[a]) for a in range(n)]
        for cp in mine:
            cp.start()
        sends = [remote(a, k, me) for a in range(n) for k in range(3)]
        for cp in sends:
            cp.start()
        for a in range(n):
            for k, (px, py) in enumerate(chips):
                remote(a, k, 2 * px + py).wait_recv()
        for cp in sends:
            cp.wait_send()
        for cp in mine:
            cp.wait()

    return pl.pallas_call(
        body, in_specs=[_ANY] * n, out_specs=[_ANY] * n, out_shape=[SDS((4,) + a.shape, a.dtype) for a in arrs],
        scratch_shapes=[pltpu.SemaphoreType.DMA((n, 3)), pltpu.SemaphoreType.DMA((n, 3)), pltpu.SemaphoreType.DMA((n,))],
        name=name)(*arrs)


def _rs_exchange(parts, *, name):
    n = len(parts)

    def body(*refs):
        ins, owns, gots = refs[:n], refs[n:2 * n], refs[2 * n:3 * n]
        send, recv, loc = refs[3 * n:]
        x, y, c, chips = _place()
        me = 2 * x + y

        def remote(a, k):
            px, py = chips[k]
            return pltpu.make_async_remote_copy(src_ref=ins[a].at[2 * px + py], dst_ref=gots[a].at[k], send_sem=send.at[a, k],
                                                recv_sem=recv.at[a, k], device_id=(px, py, c), device_id_type=MESH)

        mine = [pltpu.make_async_copy(ins[a].at[me], owns[a], loc.at[a]) for a in range(n)]
        for cp in mine:
            cp.start()
        sends = [remote(a, k) for a in range(n) for k in range(3)]
        for cp in sends:
            cp.start()
        for cp in sends:
            cp.wait_recv()
        for cp in sends:
            cp.wait_send()
        for cp in mine:
            cp.wait()

    return pl.pallas_call(
        body, in_specs=[_ANY] * n, out_specs=[_ANY] * (2 * n),
        out_shape=[SDS(a.shape[1:], a.dtype) for a in parts] + [SDS((3,) + a.shape[1:], a.dtype) for a in parts],
        scratch_shapes=[pltpu.SemaphoreType.DMA((n, 3)), pltpu.SemaphoreType.DMA((n, 3)), pltpu.SemaphoreType.DMA((n,))],
        name=name)(*parts)


def _sib_exchange(arrs, *, name):
    n = len(arrs)

    def body(*refs):
        ins, outs = refs[:n], refs[n:2 * n]
        send, recv = refs[2 * n:]
        x, y, c, _ = _place()
        cps = [pltpu.make_async_remote_copy(src_ref=ins[a], dst_ref=outs[a], send_sem=send.at[a], recv_sem=recv.at[a],
                                            device_id=(x, y, 1 - c), device_id_type=MESH) for a in range(n)]
        for cp in cps:
            cp.start()
        for cp in cps:
            cp.wait_recv()
        for cp in cps:
            cp.wait_send()

    return pl.pallas_call(
        body, in_specs=[_ANY] * n, out_specs=[_ANY] * n, out_shape=[SDS(a.shape, a.dtype) for a in arrs],
        scratch_shapes=[pltpu.SemaphoreType.DMA((n,)), pltpu.SemaphoreType.DMA((n,))], name=name)(*arrs)


def _sum4(own, got, *, name):
    R, W = own.shape
    tr = R if R <= 256 else 128
    g3 = lambda k: pl.BlockSpec((None, tr, W), lambda j, i: (k, i, 0))
    up = lambda t: t.astype(f32)
    return _pw(lambda a, b, c, d: ((up(a) + up(b)) + up(c)) + up(d), [own, got, got, got], [_row(tr, W), g3(0), g3(1), g3(2)],
               [SDS((R, W), f32)], [_row(tr, W)], (1, R // tr), name=name)[0]


def _adamw(po, ps, w, m, v, *, name):
    R, W = w.shape
    tr = R if R <= 256 else 128

    def fn(a, b, wv, mv, vv):
        g = a + b
        m2 = ADAM_B1 * mv + (1.0 - ADAM_B1) * g
        v2 = ADAM_B2 * vv + (1.0 - ADAM_B2) * jnp.square(g)
        m_hat = m2 / (1.0 - ADAM_B1 ** ADAM_STEP)
        v_hat = v2 / (1.0 - ADAM_B2 ** ADAM_STEP)
        return g, -ADAM_LR * (m_hat / (jnp.sqrt(v_hat) + ADAM_EPS) + ADAM_WD * wv), m2, v2

    return _pw(fn, [po, ps, w, m, v], [_row(tr, W)] * 5, [SDS((R, W), f32)] * 4, [_row(tr, W)] * 4, (1, R // tr), name=name)


def _to_parts(g, axis):
    shp = g.shape
    g = g.reshape(shp[:axis] + (4, shp[axis] // 4) + shp[axis + 1:])
    return jnp.moveaxis(g, axis, 0)


def _from_parts(pt, axis):
    g = jnp.moveaxis(pt, 0, axis)
    shp = g.shape
    return g.reshape(shp[:axis] + (4 * shp[axis + 1],) + shp[axis + 2:])


def kernel(x, norm1_g, w_in, ret_decay, ret_gn_g, mla_q_norm_g, mla_w_uq, mla_kv_norm_g, mla_w_ukv, s5_a_re, s5_a_im, s5_log_dt, s5_b_re, s5_b_im, s5_c_re, s5_c_im, s5_d, s5_w_glu, w_branch, w_out, norm2_g, ffn_w_gu, ffn_w_down, final_g, loss_target, m_norm1_g, m_w_in, m_ret_decay, m_ret_gn_g, m_mla_q_norm_g, m_mla_w_uq, m_mla_kv_norm_g, m_mla_w_ukv, m_s5_a_re, m_s5_a_im, m_s5_log_dt, m_s5_b_re, m_s5_b_im, m_s5_c_re, m_s5_c_im, m_s5_d, m_s5_w_glu, m_w_branch, m_w_out, m_norm2_g, m_ffn_w_gu, m_ffn_w_down, m_final_g, v_norm1_g, v_w_in, v_ret_decay, v_ret_gn_g, v_mla_q_norm_g, v_mla_w_uq, v_mla_kv_norm_g, v_mla_w_ukv, v_s5_a_re, v_s5_a_im, v_s5_log_dt, v_s5_b_re, v_s5_b_im, v_s5_c_re, v_s5_c_im, v_s5_d, v_s5_w_glu, v_w_branch, v_w_out, v_norm2_g, v_ffn_w_gu, v_ffn_w_down, v_final_g):
    wv = dict(zip(W_NAMES, (norm1_g, w_in, ret_decay, ret_gn_g, mla_q_norm_g, mla_w_uq, mla_kv_norm_g, mla_w_ukv, s5_a_re, s5_a_im,
                            s5_log_dt, s5_b_re, s5_b_im, s5_c_re, s5_c_im, s5_d, s5_w_glu, w_branch, w_out, norm2_g, ffn_w_gu,
                            ffn_w_down, final_g)))
    mv = dict(zip(W_NAMES, (m_norm1_g, m_w_in, m_ret_decay, m_ret_gn_g, m_mla_q_norm_g, m_mla_w_uq, m_mla_kv_norm_g, m_mla_w_ukv,
                            m_s5_a_re, m_s5_a_im, m_s5_log_dt, m_s5_b_re, m_s5_b_im, m_s5_c_re, m_s5_c_im, m_s5_d, m_s5_w_glu,
                            m_w_branch, m_w_out, m_norm2_g, m_ffn_w_gu, m_ffn_w_down, m_final_g)))
    vv = dict(zip(W_NAMES, (v_norm1_g, v_w_in, v_ret_decay, v_ret_gn_g, v_mla_q_norm_g, v_mla_w_uq, v_mla_kv_norm_g, v_mla_w_ukv,
                            v_s5_a_re, v_s5_a_im, v_s5_log_dt, v_s5_b_re, v_s5_b_im, v_s5_c_re, v_s5_c_im, v_s5_d, v_s5_w_glu,
                            v_w_branch, v_w_out, v_norm2_g, v_ffn_w_gu, v_ffn_w_down, v_final_g)))
    big_names = list(BIG)

    gathered = _allgather4([wv[n].astype(bf16) for n in big_names], name="gather_weights")
    big = {n: _from_parts(gt, BIG[n]) for n, gt in zip(big_names, gathered)}
    small = {n: wv[n] for n in SMALL}

    loss_local, dx, grads = _local_step(x[0], loss_target[0], big, small)

    n_rows = {n: -(-math.prod(wv[n].shape) // 1024) * 8 for n in SMALL}
    used = sum(n_rows.values())
    rows_q = -(-(used + 8) // (4 * 128)) * 128

    def as_rows(d, tail=None):
        blocks = [jnp.pad(d[n].reshape(-1), (0, n_rows[n] * 128 - math.prod(wv[n].shape))).reshape(n_rows[n], 128) for n in SMALL]
        blocks.append(jnp.zeros((8, 128), f32) if tail is None else tail)
        blocks.append(jnp.zeros((4 * rows_q - used - 8, 128), f32))
        return jnp.concatenate(blocks, axis=0)

    loss_rows = jnp.full((8, 128), loss_local, f32)
    parts = [_to_parts(grads[n].astype(bf16), BIG[n]) for n in big_names] + [as_rows(grads, loss_rows).reshape(4, rows_q, 128)]
    got = _rs_exchange(parts, name="grad_exchange")
    n_arr = len(parts)
    two_d = lambda a: a.reshape(-1, a.shape[-1])
    sums = [_sum4(two_d(got[a]), got[n_arr + a].reshape(3, -1, got[a].shape[-1]), name=f"grad_sum4_{a}") for a in range(n_arr)]
    sib = _sib_exchange(sums, name="grad_sibling")

    out_g, out_d, out_m, out_v = {}, {}, {}, {}
    for a, n in enumerate(big_names):
        shp = wv[n].shape
        res = _adamw(sums[a], sib[a], two_d(wv[n]), two_d(mv[n]), two_d(vv[n]), name=f"adamw_{n}")
        out_g[n], out_d[n], out_m[n], out_v[n] = [r.reshape(shp) for r in res]
    g_quarter = _pw(lambda p, q: p + q, [sums[-1], sib[-1]], [_row(rows_q, 128)] * 2, [SDS((rows_q, 128), f32)],
                    [_row(rows_q, 128)], (1, 1), name="small_grad_sum")[0]
    g_small = _allgather4([g_quarter], name="gather_small_grads")[0].reshape(4 * rows_q, 128)
    loss = g_small[used, 0]
    zero = jnp.zeros_like(g_small)
    res = _adamw(g_small, zero, as_rows(wv), as_rows(mv), as_rows(vv), name="adamw_small")
    off = 0
    for n in SMALL:
        k = math.prod(wv[n].shape)
        for dst, r in zip((out_g, out_d, out_m, out_v), res):
            dst[n] = r[off:off + n_rows[n]].reshape(-1)[:k].reshape(wv[n].shape)
        off += n_rows[n]
    return (loss, dx[None], *[out_g[n] for n in W_NAMES], *[out_d[n] for n in W_NAMES], *[out_m[n] for n in W_NAMES],
            *[out_v[n] for n in W_NAMES])
```

```python
import functools
import math

import jax
import jax.numpy as jnp
from jax import lax
from jax.experimental import pallas as pl
from jax.experimental.pallas import tpu as pltpu

f32 = jnp.float32
bf16 = jnp.bfloat16
SDS = jax.ShapeDtypeStruct
MESH = pl.DeviceIdType.MESH

D = 1024
DEPTH = 2
RMS_EPS = 1e-6
GN_EPS = 1e-5
ROPE_THETA = 10000.0
RET_HEADS = 4
RET_DK = 128
RET_DV = 256
RET_CHUNK = 128
MLA_HEADS = 8
MLA_Q_LORA = 384
MLA_KV_LORA = 256
MLA_NOPE = 128
MLA_ROPE = 64
MLA_V = 128
MLA_QW = 256
S5_G = 64
S5_P = 64
S5_C = 16
S5_NJ = 8
S5_SEG = 8
FFN_H = 2816
ADAM_LR = 0.001
ADAM_B1 = 0.9
ADAM_B2 = 0.999
ADAM_EPS = 1e-08
ADAM_WD = 0.01
ADAM_STEP = 10
VMEM_BIG = 56 * 1024 * 1024

W_NAMES = ['norm1_g', 'w_in', 'ret_decay', 'ret_gn_g', 'mla_q_norm_g', 'mla_w_uq', 'mla_kv_norm_g', 'mla_w_ukv',
           's5_a_re', 's5_a_im', 's5_log_dt', 's5_b_re', 's5_b_im', 's5_c_re', 's5_c_im', 's5_d', 's5_w_glu',
           'w_branch', 'w_out', 'norm2_g', 'ffn_w_gu', 'ffn_w_down', 'final_g']
BIG = {'w_in': 2, 'mla_w_uq': 2, 'mla_w_ukv': 2, 's5_w_glu': 2, 'w_branch': 2, 'w_out': 1, 'ffn_w_gu': 2, 'ffn_w_down': 1}
SMALL = [n for n in W_NAMES if n not in BIG]


def _pick(n, cands=(512, 384, 256, 128)):
    if n <= 1024:
        return n
    for c in cands:
        if n % c == 0:
            return c
    raise ValueError(n)


def _params(sem, vmem=None):
    return pltpu.CompilerParams(dimension_semantics=sem, vmem_limit_bytes=vmem)


def _mm(a, b, *, tb=False, res=None, out_dtype=f32, name):
    M, K = a.shape
    N = b.shape[0] if tb else b.shape[1]
    tn = _pick(N)
    tk = K if K <= 3072 else _pick(K, (1408, 1024, 512))
    nk = K // tk
    tm = _pick(M)
    if M % 1024 == 0 and 1024 * tk * a.dtype.itemsize <= 4 * 1024 * 1024:
        tm = 1024
    assert M % tm == 0 and N % tn == 0 and K % tk == 0

    def body(*refs):
        if res is None:
            a_ref, b_ref, o_ref, acc = refs
        else:
            a_ref, b_ref, r_ref, o_ref, acc = refs
        k = pl.program_id(2)
        dn = (((1,), (1 if tb else 0,)), ((), ()))
        part = lax.dot_general(a_ref[...].astype(bf16), b_ref[...].astype(bf16), dn, preferred_element_type=f32)

        @pl.when(k == 0)
        def _():
            acc[...] = part

        @pl.when(k > 0)
        def _():
            acc[...] += part

        @pl.when(k == nk - 1)
        def _():
            v = acc[...]
            if res is not None:
                v = v + r_ref[...]
            o_ref[...] = v.astype(out_dtype)

    in_specs = [pl.BlockSpec((tm, tk), lambda i, j, k: (i, k)),
                pl.BlockSpec((tn, tk), lambda i, j, k: (j, k)) if tb else pl.BlockSpec((tk, tn), lambda i, j, k: (k, j))]
    args = [a, b]
    if res is not None:
        in_specs.append(pl.BlockSpec((tm, tn), lambda i, j, k: (i, j)))
        args.append(res)
    return pl.pallas_call(
        body, grid=(M // tm, N // tn, nk), in_specs=in_specs,
        out_specs=pl.BlockSpec((tm, tn), lambda i, j, k: (i, j)),
        out_shape=SDS((M, N), out_dtype), scratch_shapes=[pltpu.VMEM((tm, tn), f32)],
        compiler_params=_params(("parallel", "parallel", "arbitrary"), VMEM_BIG), name=name)(*args)


def _mmT(a, b, *, name):
    S, M = a.shape
    N = b.shape[1]
    tm = _pick(M)
    tn = _pick(N)
    tk = min(S, 1024)
    nk = S // tk

    def body(a_ref, b_ref, o_ref):
        k = pl.program_id(2)
        part = lax.dot_general(a_ref[...].astype(bf16), b_ref[...].astype(bf16), (((0,), (0,)), ((), ())),
                               preferred_element_type=f32)

        @pl.when(k == 0)
        def _():
            o_ref[...] = part

        @pl.when(k > 0)
        def _():
            o_ref[...] += part

    return pl.pallas_call(
        body, grid=(M // tm, N // tn, nk),
        in_specs=[pl.BlockSpec((tk, tm), lambda i, j, k: (k, i)), pl.BlockSpec((tk, tn), lambda i, j, k: (k, j))],
        out_specs=pl.BlockSpec((tm, tn), lambda i, j, k: (i, j)),
        out_shape=SDS((M, N), f32),
        compiler_params=_params(("parallel", "parallel", "arbitrary"), VMEM_BIG), name=name)(a, b)


def _pw(fn, ins, in_specs, outs, out_specs, grid, *, n_acc=0, name):
    n_in = len(ins)
    n_out = len(outs)

    def body(*refs):
        vals = fn(*[r[...] for r in refs[:n_in]])
        if not isinstance(vals, (tuple, list)):
            vals = (vals,)
        orefs = refs[n_in:]
        for r, v in zip(orefs[:n_out - n_acc], vals[:n_out - n_acc]):
            r[...] = v.astype(r.dtype)
        if n_acc:
            i = pl.program_id(1)

            @pl.when(i == 0)
            def _():
                for r, v in zip(orefs[n_out - n_acc:], vals[n_out - n_acc:]):
                    r[...] = v.astype(r.dtype)

            @pl.when(i > 0)
            def _():
                for r, v in zip(orefs[n_out - n_acc:], vals[n_out - n_acc:]):
                    r[...] += v.astype(r.dtype)

    res = pl.pallas_call(
        body, grid=grid, in_specs=in_specs, out_specs=out_specs, out_shape=outs,
        compiler_params=_params(("parallel", "arbitrary"), VMEM_BIG), name=name)(*ins)
    return res


def _row(T, w, col=None):
    if col is None:
        return pl.BlockSpec((T, w), lambda j, i: (i, 0))
    return pl.BlockSpec((T, w), lambda j, i: (i, col(j)))


def _par(w, col=None):
    if col is None:
        return pl.BlockSpec((1, w), lambda j, i: (0, 0))
    return pl.BlockSpec((1, w), lambda j, i: (0, col(j)))


def _rms(x, g):
    return x * lax.rsqrt(jnp.mean(x * x, axis=-1, keepdims=True) + RMS_EPS) * g


def _rope(x, cos, sinm, half):
    if half == 64:
        partner = pltpu.roll(x, 64, axis=1)
    else:
        lane = lax.broadcasted_iota(jnp.int32, x.shape, 1)
        partner = jnp.where((lane % (2 * half)) < half, pltpu.roll(x, 128 - half, axis=1), pltpu.roll(x, half, axis=1))
    return x * cos + partner * sinm


def _rope_t(x, cos, sinm, half):
    return _rope(x, cos, -sinm, half)


def _rmsnorm_fwd(x, g, *, name):
    S, W = x.shape
    T = min(S, 512)
    return _pw(lambda xv, gv: _rms(xv, gv), [x, g], [_row(T, W), _par(W)], [SDS((S, W), bf16)], [_row(T, W)],
               (1, S // T), name=name)[0]


def _rmsnorm_bwd(x, g, dh, dres, *, name):
    S, W = x.shape
    T = min(S, 512)

    def fn(xv, gv, dhv, drv):
        _, vjp = jax.vjp(_rms, xv, gv)
        dx, dg = vjp(dhv)
        return dx + drv, dg

    return _pw(fn, [x, g, dh, dres], [_row(T, W), _par(W), _row(T, W), _row(T, W)],
               [SDS((S, W), f32), SDS((1, W), f32)], [_row(T, W), _par(W)], (1, S // T), n_acc=1, name=name)


def _ret_tables(lg, reverse):
    C = RET_CHUNK
    ii = lax.broadcasted_iota(jnp.int32, (C, C), 0).astype(f32)
    jj = lax.broadcasted_iota(jnp.int32, (C, C), 1).astype(f32)
    if not reverse:
        E = ii - jj
        mask = E >= 0
        eq = ii + 1.0
        ek = (C - 1.0) - ii
    else:
        E = jj - ii
        mask = E > 0
        eq = C - ii
        ek = ii
    Dm = jnp.where(mask, jnp.exp(jnp.where(mask, E, 0.0) * lg), 0.0)
    Em = jnp.where(mask, E, 0.0)
    qw = jnp.exp(eq * lg)
    kw = jnp.exp(ek * lg)
    qw2 = jnp.concatenate([qw, qw], axis=1)
    return Dm, Em, eq, ek, qw, kw, qw2, jnp.exp(C * lg)


def _dot(a, b, dims):
    return lax.dot_general(a.astype(bf16), b.astype(bf16), (dims, ((), ())), preferred_element_type=f32)


NN = ((1,), (0,))
NT = ((1,), (1,))
TN = ((0,), (0,))


def _ret_dir_fwd(zr, lg, cos, sinm, *, reverse, name):
    S = zr.shape[0]
    C = RET_CHUNK
    TB = min(S, 512)
    nc = TB // C
    NB = S // TB
    d = 1 if reverse else 0
    scale = RET_DK ** -0.5

    def tb(b):
        return (NB - 1 - b) if reverse else b

    def body(lg_ref, q_ref, k_ref, v_ref, cos_ref, sin_ref, y_ref, st_ref, state):
        h = pl.program_id(0)
        b = pl.program_id(1)

        @pl.when(b == 0)
        def _():
            state[...] = jnp.zeros_like(state)

        Dm, _, _, _, _, kw, qw2, gC = _ret_tables(lg_ref[d, h], reverse)
        order = range(nc - 1, -1, -1) if reverse else range(nc)
        for c in order:
            rows = pl.ds(c * C, C)
            q = _rope(q_ref[rows, :], cos_ref[rows, :], sin_ref[rows, :], 64) * scale
            k = _rope(k_ref[rows, :], cos_ref[rows, :], sin_ref[rows, :], 64)
            v = v_ref[rows, :]
            st = state[...]
            st_ref[0, c] = st
            s = _dot(q, k, NT) * Dm
            o = _dot(s, v, NN) + _dot(q, st, NN) * qw2
            y_ref[rows, :] = o
            state[...] = gC * st + _dot(k * kw, v, TN)

    return pl.pallas_call(
        body, grid=(RET_HEADS, NB),
        in_specs=[pl.BlockSpec(memory_space=pltpu.SMEM),
                  pl.BlockSpec((TB, 128), lambda h, b: (tb(b), h)),
                  pl.BlockSpec((TB, 128), lambda h, b: (tb(b), 4 + h)),
                  pl.BlockSpec((TB, 256), lambda h, b: (tb(b), 4 + h)),
                  pl.BlockSpec((TB, 128), lambda h, b: (tb(b), 0)),
                  pl.BlockSpec((TB, 128), lambda h, b: (tb(b), 0))],
        out_specs=[pl.BlockSpec((TB, 256), lambda h, b: (tb(b), h)),
                   pl.BlockSpec((1, nc, 128, 256), lambda h, b: (h, tb(b), 0, 0))],
        out_shape=[SDS((S, 1024), f32), SDS((RET_HEADS, S // C, 128, 256), f32)],
        scratch_shapes=[pltpu.VMEM((128, 256), f32)],
        compiler_params=_params(("parallel", "arbitrary")), name=name)(lg, zr, zr, zr, cos, sinm)


def _ret_dir_bwd(zr, lg, cos, sinm, dy, states, *, reverse, name):
    S = zr.shape[0]
    C = RET_CHUNK
    TB = min(S, 512)
    nc = TB // C
    NB = S // TB
    d = 1 if reverse else 0
    scale = RET_DK ** -0.5

    def tb(b):
        return b if reverse else (NB - 1 - b)

    def body(lg_ref, q_ref, k_ref, v_ref, cos_ref, sin_ref, dy_ref, st_ref, dq_ref, dk_ref, dv_ref, dlg_ref, dstate):
        h = pl.program_id(0)
        b = pl.program_id(1)

        @pl.when(b == 0)
        def _():
            dstate[...] = jnp.zeros_like(dstate)
            dlg_ref[...] = jnp.zeros_like(dlg_ref)

        Dm, Em, eq, ek, qw, kw, qw2, gC = _ret_tables(lg_ref[d, h], reverse)
        order = range(nc) if reverse else range(nc - 1, -1, -1)
        dlg = jnp.zeros((), f32)
        for c in order:
            rows = pl.ds(c * C, C)
            cs, sn = cos_ref[rows, :], sin_ref[rows, :]
            q = _rope(q_ref[rows, :], cs, sn, 64) * scale
            k = _rope(k_ref[rows, :], cs, sn, 64)
            v = v_ref[rows, :]
            do = dy_ref[rows, :]
            st = st_ref[0, c]
            ds = dstate[...]
            p = _dot(q, k, NT)
            a = p * Dm
            dp = _dot(do, v, NT) * Dm
            dq_cross = _dot(do, st, NT) * qw
            dk_cross = _dot(v, ds, NT) * kw
            dq = _dot(dp, k, NN) + dq_cross
            dk = _dot(dp, q, TN) + dk_cross
            dv = _dot(a, do, TN) + _dot(k * kw, ds, NN)
            dlg = dlg + jnp.sum(dp * p * Em) + jnp.sum(dq_cross * q * eq) + jnp.sum(dk_cross * k * ek) \
                + C * gC * jnp.sum(ds * st)
            dstate[...] = gC * ds + _dot(q * qw, do, TN)
            dq_ref[rows, :] = _rope_t(dq, cs, sn, 64) * scale
            dk_ref[rows, :] = _rope_t(dk, cs, sn, 64)
            dv_ref[rows, :] = dv
        dlg_ref[...] += jnp.full(dlg_ref.shape, dlg, f32)

    return pl.pallas_call(
        body, grid=(RET_HEADS, NB),
        in_specs=[pl.BlockSpec(memory_space=pltpu.SMEM),
                  pl.BlockSpec((TB, 128), lambda h, b: (tb(b), h)),
                  pl.BlockSpec((TB, 128), lambda h, b: (tb(b), 4 + h)),
                  pl.BlockSpec((TB, 256), lambda h, b: (tb(b), 4 + h)),
                  pl.BlockSpec((TB, 128), lambda h, b: (tb(b), 0)),
                  pl.BlockSpec((TB, 128), lambda h, b: (tb(b), 0)),
                  pl.BlockSpec((TB, 256), lambda h, b: (tb(b), h)),
                  pl.BlockSpec((1, nc, 128, 256), lambda h, b: (h, tb(b), 0, 0))],
        out_specs=[pl.BlockSpec((TB, 128), lambda h, b: (tb(b), h)),
                   pl.BlockSpec((TB, 128), lambda h, b: (tb(b), h)),
                   pl.BlockSpec((TB, 256), lambda h, b: (tb(b), h)),
                   pl.BlockSpec((1, 1, 128), lambda h, b: (h, 0, 0))],
        out_shape=[SDS((S, 512), f32), SDS((S, 512), f32), SDS((S, 1024), f32), SDS((RET_HEADS, 1, 128), f32)],
        scratch_shapes=[pltpu.VMEM((128, 256), f32)],
        compiler_params=_params(("parallel", "arbitrary")), name=name)(lg, zr, zr, zr, cos, sinm, dy, states)


def _gn_gate(yf, yb, g, gn):
    y = yf + yb
    mu = jnp.mean(y, axis=-1, keepdims=True)
    var = jnp.mean(jnp.square(y - mu), axis=-1, keepdims=True)
    yn = (y - mu) * lax.rsqrt(var + GN_EPS)
    return jax.nn.silu(g) * (yn * gn)


def _flash_fwd(Q, K, kv, *, name):
    S = Q.shape[0]
    hq = min(S, 512)
    nh = 2 if S % 1024 == 0 else 1
    tq = nh * hq
    tk = min(S, 512)
    nk = S // tk

    def body(q_ref, k_ref, v_ref, o_ref, l_ref, m_s, l_s, acc):
        kk = pl.program_id(2)

        @pl.when(kk == 0)
        def _():
            m_s[...] = jnp.full_like(m_s, -jnp.inf)
            l_s[...] = jnp.zeros_like(l_s)
            acc[...] = jnp.zeros_like(acc)

        k = k_ref[...]
        v = v_ref[...]
        sts = [lax.dot_general(k, q_ref[hf * hq:(hf + 1) * hq, :], (NT, ((), ())), preferred_element_type=f32)
               for hf in range(nh)]
        for hf in range(nh):
            st = sts[hf]
            m_prev = m_s[hf]
            m_new = jnp.maximum(m_prev, jnp.max(st, axis=0, keepdims=True))
            pt = jnp.exp2(st - m_new)
            alpha = jnp.exp2(m_prev - m_new)
            l_s[hf] = alpha * l_s[hf] + jnp.sum(pt, axis=0, keepdims=True)
            acc[hf] = alpha * acc[hf] + lax.dot_general(v, pt.astype(bf16), (TN, ((), ())), preferred_element_type=f32)
            m_s[hf] = m_new

        @pl.when(kk == nk - 1)
        def _():
            for hf in range(nh):
                o_ref[hf * hq:(hf + 1) * hq, :] = jnp.transpose(acc[hf] / l_s[hf]).astype(bf16)
                l_ref[0, :, hf * hq:(hf + 1) * hq] = m_s[hf] + jnp.log2(l_s[hf])

    return pl.pallas_call(
        body, grid=(MLA_HEADS, S // tq, nk),
        in_specs=[pl.BlockSpec((tq, 256), lambda h, i, k: (i, h)),
                  pl.BlockSpec((tk, 256), lambda h, i, k: (k, h)),
                  pl.BlockSpec((tk, 128), lambda h, i, k: (k, 2 * h + 1))],
        out_specs=[pl.BlockSpec((tq, 128), lambda h, i, k: (i, h)), pl.BlockSpec((1, 1, tq), lambda h, i, k: (h, 0, i))],
        out_shape=[SDS((S, 1024), bf16), SDS((MLA_HEADS, 1, S), f32)],
        scratch_shapes=[pltpu.VMEM((nh, 1, hq), f32), pltpu.VMEM((nh, 1, hq), f32), pltpu.VMEM((nh, 128, hq), f32)],
        compiler_params=_params(("parallel", "parallel", "arbitrary")), name=name)(Q, K, kv)


def _attn_delta(dO, O, *, name):
    S = dO.shape[0]
    T = min(S, 512)

    def body(do_ref, o_ref, d_ref):
        prod = do_ref[...] * o_ref[...].astype(f32)
        d_ref[0] = lax.dot_general(jnp.ones((8, 128), f32), prod, (NT, ((), ())), preferred_element_type=f32,
                                   precision=lax.Precision.HIGHEST)[0:1, :]

    return pl.pallas_call(
        body, grid=(MLA_HEADS, S // T),
        in_specs=[pl.BlockSpec((T, 128), lambda h, i: (i, h)), pl.BlockSpec((T, 128), lambda h, i: (i, h))],
        out_specs=pl.BlockSpec((1, 1, T), lambda h, i: (h, 0, i)), out_shape=SDS((MLA_HEADS, 1, S), f32),
        compiler_params=_params(("parallel", "parallel")), name=name)(dO, O)


def _flash_bwd(Q, K, kv, delta, L, dO, *, name):
    S = Q.shape[0]
    hq = min(S, 512)
    nh = 2 if S % 1024 == 0 else 1
    tq = nh * hq
    tk = min(S, 512)
    nq = S // tq
    ln2 = math.log(2.0)

    def body(q_ref, k_ref, v_ref, dl_ref, l_ref, do_ref, dq_ref, dk_ref, dv_ref, dk_acc, dv_acc):
        kk = pl.program_id(1)
        i = pl.program_id(2)

        @pl.when((kk == 0) & (i == 0))
        def _():
            dq_ref[...] = jnp.zeros_like(dq_ref)

        @pl.when(i == 0)
        def _():
            dk_acc[...] = jnp.zeros_like(dk_acc)
            dv_acc[...] = jnp.zeros_like(dv_acc)

        k = k_ref[...]
        v = v_ref[...]
        dk_new = dk_acc[...]
        dv_new = dv_acc[...]
        for hf in range(nh):
            sl = slice(hf * hq, (hf + 1) * hq)
            q = q_ref[sl, :]
            st = lax.dot_general(k, q, (NT, ((), ())), preferred_element_type=f32)
            pt = jnp.exp2(st - l_ref[0, :, sl])
            delta = dl_ref[0, :, sl]
            dob = do_ref[sl, :].astype(bf16)
            dv_new = dv_new + lax.dot_general(pt.astype(bf16), dob, (NN, ((), ())), preferred_element_type=f32)
            dpt = lax.dot_general(v, dob, (NT, ((), ())), preferred_element_type=f32)
            dst = (pt * (dpt - delta)).astype(bf16)
            dk_new = dk_new + lax.dot_general(dst, q, (NN, ((), ())), preferred_element_type=f32)
            rows = pl.ds(pl.multiple_of(i * tq + hf * hq, hq), hq)
            dq_ref[rows, :] += lax.dot_general(dst, k, (TN, ((), ())), preferred_element_type=f32)
        dk_acc[...] = dk_new
        dv_acc[...] = dv_new

        @pl.when(i == nq - 1)
        def _():
            dk_ref[...] = dk_acc[...] * ln2
            dv_ref[...] = dv_acc[...]

    return pl.pallas_call(
        body, grid=(MLA_HEADS, S // tk, nq),
        in_specs=[pl.BlockSpec((tq, 256), lambda h, k, i: (i, h)),
                  pl.BlockSpec((tk, 256), lambda h, k, i: (k, h)),
                  pl.BlockSpec((tk, 128), lambda h, k, i: (k, 2 * h + 1)),
                  pl.BlockSpec((1, 1, tq), lambda h, k, i: (h, 0, i)),
                  pl.BlockSpec((1, 1, tq), lambda h, k, i: (h, 0, i)),
                  pl.BlockSpec((tq, 128), lambda h, k, i: (i, h))],
        out_specs=[pl.BlockSpec((S, 256), lambda h, k, i: (0, h)),
                   pl.BlockSpec((tk, 256), lambda h, k, i: (k, h)),
                   pl.BlockSpec((tk, 128), lambda h, k, i: (k, h))],
        out_shape=[SDS((S, 2048), f32), SDS((S, 2048), f32), SDS((S, 1024), f32)],
        scratch_shapes=[pltpu.VMEM((tk, 256), f32), pltpu.VMEM((tk, 128), f32)],
        compiler_params=_params(("parallel", "arbitrary", "arbitrary"), VMEM_BIG), name=name)(Q, K, kv, delta, L, dO)


def _mla_bwd_prep(dQ, dK, dV, cosm, sinm, *, name):
    S = dQ.shape[0]
    T = min(S, 256)
    scale = (MLA_NOPE + MLA_ROPE) ** -0.5

    def body(dq_ref, dk_ref, dv_ref, cos_ref, sin_ref, oq_ref, okv_ref, okr_ref):
        cs, sn = cos_ref[...], sin_ref[...]
        kr = jnp.zeros((T, 128), f32)
        for h in range(MLA_HEADS):
            a = 256 * h
            oq_ref[:, a:a + 128] = (dq_ref[:, a:a + 128] * scale).astype(bf16)
            oq_ref[:, a + 128:a + 256] = (_rope_t(dq_ref[:, a + 128:a + 256], cs, sn, 32) * scale).astype(bf16)
            okv_ref[:, a:a + 128] = dk_ref[:, a:a + 128].astype(bf16)
            okv_ref[:, a + 128:a + 256] = dv_ref[:, 128 * h:128 * h + 128].astype(bf16)
            kr = kr + dk_ref[:, a + 128:a + 256]
        okr_ref[...] = _rope_t(kr, cs, sn, 32)

    return pl.pallas_call(
        body, grid=(S // T,),
        in_specs=[pl.BlockSpec((T, 2048), lambda i: (i, 0)), pl.BlockSpec((T, 2048), lambda i: (i, 0)),
                  pl.BlockSpec((T, 1024), lambda i: (i, 0)), pl.BlockSpec((T, 128), lambda i: (i, 0)),
                  pl.BlockSpec((T, 128), lambda i: (i, 0))],
        out_specs=[pl.BlockSpec((T, 2048), lambda i: (i, 0)), pl.BlockSpec((T, 2048), lambda i: (i, 0)),
                   pl.BlockSpec((T, 128), lambda i: (i, 0))],
        out_shape=[SDS((S, 2048), bf16), SDS((S, 2048), bf16), SDS((S, 128), f32)],
        compiler_params=_params(("parallel",), VMEM_BIG), name=name)(dQ, dK, dV, cosm, sinm)


def _mla_norm_bwd(zm, qg, kvg, dcqn, dckvn, dkr, *, name):
    S = zm.shape[0]
    T = min(S, 512)

    def body(cq_ref, ckv_ref, qg_ref, kvg_ref, dcq_ref, dckv_ref, dkr_ref, o_ref, dqg_ref, dkvg_ref):
        i = pl.program_id(0)
        _, vjp = jax.vjp(_rms, cq_ref[...], qg_ref[...])
        dcq, dqg = vjp(dcq_ref[...])
        _, vjp2 = jax.vjp(_rms, ckv_ref[...], kvg_ref[...])
        dckv, dkvg = vjp2(dckv_ref[...])
        o_ref[:, 0:384] = dcq.astype(bf16)
        o_ref[:, 384:512] = jnp.zeros((T, 128), bf16)
        o_ref[:, 512:768] = dckv.astype(bf16)
        o_ref[:, 768:896] = dkr_ref[...].astype(bf16)

        @pl.when(i == 0)
        def _():
            dqg_ref[...] = dqg
            dkvg_ref[...] = dkvg

        @pl.when(i > 0)
        def _():
            dqg_ref[...] += dqg
            dkvg_ref[...] += dkvg

    return pl.pallas_call(
        body, grid=(S // T,),
        in_specs=[pl.BlockSpec((T, 384), lambda i: (i, 0)), pl.BlockSpec((T, 256), lambda i: (i, 2)),
                  pl.BlockSpec((1, 384), lambda i: (0, 0)), pl.BlockSpec((1, 256), lambda i: (0, 0)),
                  pl.BlockSpec((T, 384), lambda i: (i, 0)), pl.BlockSpec((T, 256), lambda i: (i, 0)),
                  pl.BlockSpec((T, 128), lambda i: (i, 0))],
        out_specs=[pl.BlockSpec((T, 896), lambda i: (i, 0)), pl.BlockSpec((1, 384), lambda i: (0, 0)),
                   pl.BlockSpec((1, 256), lambda i: (0, 0))],
        out_shape=[SDS((S, 896), bf16), SDS((1, 384), f32), SDS((1, 256), f32)],
        compiler_params=_params(("arbitrary",)), name=name)(zm, zm, qg, kvg, dcqn, dckvn, dkr)


def _s5_disc(a_re, a_im, ldt, b_re, b_im):
    dt = jnp.exp(ldt)
    ar = jnp.minimum(a_re, -1e-4)
    mag = jnp.exp(dt * ar)
    abr = mag * jnp.cos(dt * a_im)
    abi = mag * jnp.sin(dt * a_im)
    den = ar * ar + a_im * a_im
    nr = abr - 1.0
    ni = abi
    cr = (nr * ar + ni * a_im) / den
    ci = (ni * ar - nr * a_im) / den
    return abr, abi, cr * b_re - ci * b_im, cr * b_im + ci * b_re


def _s5_param_fwd(a_re, a_im, ldt, b_re, b_im, *, name):
    R = SDS((1, 8192), f32)
    M = SDS((16, 8192), f32)
    Pw = SDS((64, 8192), f32)

    def body(a_re_r, a_im_r, ldt_r, b_re_r, b_im_r, o1, o2, o3, o4, p_re, p_im):
        abr, abi, bbr, bbi = _s5_disc(a_re_r[...], a_im_r[...], ldt_r[...], b_re_r[...], b_im_r[...])
        o1[...] = abr
        o2[...] = abi
        o3[...] = bbr
        o4[...] = bbi
        dt = jnp.exp(ldt_r[...])
        ar = jnp.minimum(a_re_r[...], -1e-4)
        n = lax.broadcasted_iota(jnp.int32, (64, 8192), 0).astype(f32) + 1.0
        mag = jnp.exp(n * (dt * ar))
        ang = n * (dt * a_im_r[...])
        p_re[...] = mag * jnp.cos(ang)
        p_im[...] = mag * jnp.sin(ang)

    return pl.pallas_call(body, out_shape=[R, R, M, M, Pw, Pw], name=name)(a_re, a_im, ldt, b_re, b_im)


def _s5_param_bwd(a_re, a_im, ldt, b_re, b_im, d_abr, d_abi, d_bbr, d_bbi, *, name):
    R = SDS((1, 8192), f32)
    M = SDS((16, 8192), f32)

    def body(a_re_r, a_im_r, ldt_r, b_re_r, b_im_r, c1, c2, c3, c4, o1, o2, o3, o4, o5):
        _, vjp = jax.vjp(_s5_disc, a_re_r[...], a_im_r[...], ldt_r[...], b_re_r[...], b_im_r[...])
        g = vjp((c1[...], c2[...], c3[...], c4[...]))
        for o, v in zip((o1, o2, o3, o4, o5), g):
            o[...] = v

    return pl.pallas_call(body, out_shape=[R, R, R, M, M], name=name)(a_re, a_im, ldt, b_re, b_im, d_abr, d_abi, d_bbr, d_bbi)


def _seg_perm(T, inverse):
    L = T // S5_SEG
    i = jnp.arange(T)
    src = (i % S5_SEG) * L + i // S5_SEG
    P = (src[:, None] == jnp.arange(T)[None, :]).astype(bf16)
    return P.T if inverse else P


def _perm_rows(a, P, *, name):
    S, W = a.shape
    T = P.shape[0]

    def body(p_ref, a_ref, o_ref):
        o_ref[...] = lax.dot_general(p_ref[...], a_ref[...], (NN, ((), ())), preferred_element_type=f32).astype(o_ref.dtype)

    return pl.pallas_call(
        body, grid=(S // T,), in_specs=[pl.BlockSpec((T, T), lambda i: (0, 0)), pl.BlockSpec((T, W), lambda i: (i, 0))],
        out_specs=pl.BlockSpec((T, W), lambda i: (i, 0)), out_shape=SDS((S, W), a.dtype),
        compiler_params=_params(("parallel",)), name=name)(P, a)


def _scan_core(xr, xi, ar, ai, pwr_ref, pwi_ref, a64r, a64i, carry, *, reverse, T, conj):
    L = T // S5_SEG
    sg = -1.0 if conj else 1.0
    arb = jnp.broadcast_to(ar, (8, 512))
    aib = jnp.broadcast_to(ai, (8, 512))
    UN = 4

    def step(r4, c):
        cr, ci = c
        for u in range(UN):
            r0 = r4 * UN + u
            r = (L - 1 - r0) if reverse else r0
            rows = pl.ds(pl.multiple_of(r * 8, 8), 8)
            nr = arb * cr - aib * ci + xr[rows, :]
            ni = arb * ci + aib * cr + xi[rows, :]
            xr[rows, :] = nr
            xi[rows, :] = ni
            cr, ci = nr, ni
        return cr, ci

    lr, li = lax.fori_loop(0, L // UN, step, (jnp.zeros((8, 512), f32), jnp.zeros((8, 512), f32)))
    row8 = lax.broadcasted_iota(jnp.int32, (8, 512), 0)
    cr = carry[0, 0:1, :]
    ci = carry[1, 0:1, :]
    a6i = sg * a64i
    cin_r = jnp.zeros((8, 512), f32)
    cin_i = jnp.zeros((8, 512), f32)
    for seg in (range(S5_SEG - 1, -1, -1) if reverse else range(S5_SEG)):
        cin_r = jnp.where(row8 == seg, cr, cin_r)
        cin_i = jnp.where(row8 == seg, ci, cin_i)
        ncr = lr[seg:seg + 1, :] + a64r * cr - a6i * ci
        nci = li[seg:seg + 1, :] + a64r * ci + a6i * cr
        cr, ci = ncr, nci
    carry[0, 0:1, :] = cr
    carry[1, 0:1, :] = ci

    def fix(r4, _):
        for u in range(UN):
            r = r4 * UN + u
            rows = pl.ds(pl.multiple_of(r * 8, 8), 8)
            pr = pwr_ref[pl.ds(r, 1), :]
            pi = sg * pwi_ref[pl.ds(r, 1), :]
            xr[rows, :] += pr * cin_r - pi * cin_i
            xi[rows, :] += pr * cin_i + pi * cin_r
        return 0

    lax.fori_loop(0, L // UN, fix, 0)


def _s5_scan_fwd(u, BBr, BBi, CCr, CCi, abr, abi, pwr, pwi, *, reverse, name):
    S = u.shape[0]
    T = min(S, 512)
    NB = S // T
    L = T // S5_SEG
    d = 1 if reverse else 0

    def tb(b):
        return (NB - 1 - b) if reverse else b

    def body(u_ref, bbr_ref, bbi_ref, ccr_ref, cci_ref, ar_ref, ai_ref, pwr_ref, pwi_ref, y_ref, xr_ref, xi_ref, carry):
        b = pl.program_id(1)

        @pl.when(b == 0)
        def _():
            carry[...] = jnp.zeros_like(carry)

        ub = u_ref[...].astype(bf16)
        xr_ref[...] = lax.dot_general(ub, bbr_ref[0, 0], (NN, ((), ())), preferred_element_type=f32)
        xi_ref[...] = lax.dot_general(ub, bbi_ref[0, 0], (NN, ((), ())), preferred_element_type=f32)
        a6 = (0 if reverse else L - 1)
        _scan_core(xr_ref, xi_ref, ar_ref[...], ai_ref[...], pwr_ref, pwi_ref, pwr_ref[a6:a6 + 1, :], pwi_ref[a6:a6 + 1, :],
                   carry, reverse=reverse, T=T, conj=False)
        y_ref[...] = _dot(xr_ref[...], ccr_ref[0, 0], NN) - _dot(xi_ref[...], cci_ref[0, 0], NN)

    mat = lambda shp: pl.BlockSpec((1, 1) + shp, lambda j, b: (d, j, 0, 0))
    vec = lambda r: pl.BlockSpec((r, 512), lambda j, b: (0, d * S5_NJ + j))
    return pl.pallas_call(
        body, grid=(S5_NJ, NB),
        in_specs=[pl.BlockSpec((T, 128), lambda j, b: (tb(b), j)), mat((128, 512)), mat((128, 512)), mat((512, 128)),
                  mat((512, 128)), vec(1), vec(1), vec(L), vec(L)],
        out_specs=[pl.BlockSpec((T, 128), lambda j, b: (tb(b), j)), pl.BlockSpec((T, 512), lambda j, b: (tb(b), j)),
                   pl.BlockSpec((T, 512), lambda j, b: (tb(b), j))],
        out_shape=[SDS((S, 1024), f32), SDS((S, 4096), f32), SDS((S, 4096), f32)],
        scratch_shapes=[pltpu.VMEM((2, 8, 512), f32)],
        compiler_params=_params(("parallel", "arbitrary")), name=name)(u, BBr, BBi, CCr, CCi, abr, abi, pwr, pwi)


def _s5_scan_bwd(u, dy, xr, xi, BBr, BBi, CCr, CCi, abr, abi, pwr, pwi, *, reverse, name):
    S = u.shape[0]
    T = min(S, 512)
    NB = S // T
    L = T // S5_SEG
    d = 1 if reverse else 0
    adj_rev = not reverse

    def tb(b):
        return b if reverse else (NB - 1 - b)

    def bnd(b):
        t = tb(b)
        if reverse:
            return jnp.minimum((t + 1) * (T // 8), S // 8 - 1)
        return jnp.maximum(t * (T // 8) - 1, 0)

    def body(u_ref, dy_ref, xr_ref, xi_ref, xbr_ref, xbi_ref, bbr_ref, bbi_ref, ccr_ref, cci_ref, ar_ref, ai_ref,
             pwr_ref, pwi_ref, du_ref, dbbr_ref, dbbi_ref, dccr_ref, dcci_ref, dar_ref, dai_ref, carry, lam):
        b = pl.program_id(1)

        @pl.when(b == 0)
        def _():
            carry[...] = jnp.zeros_like(carry)
            for r in (dbbr_ref, dbbi_ref, dccr_ref, dcci_ref, dar_ref, dai_ref):
                r[...] = jnp.zeros_like(r)

        dyb = dy_ref[...]
        lam[0] = lax.dot_general(dyb, ccr_ref[0, 0], (NT, ((), ())), preferred_element_type=f32)
        lam[1] = -lax.dot_general(dyb, cci_ref[0, 0], (NT, ((), ())), preferred_element_type=f32)
        a6 = (0 if adj_rev else L - 1)
        _scan_core(lam.at[0], lam.at[1], ar_ref[...], -ai_ref[...], pwr_ref, pwi_ref, pwr_ref[a6:a6 + 1, :],
                   pwi_ref[a6:a6 + 1, :], carry, reverse=adj_rev, T=T, conj=True)
        ub = u_ref[...].astype(bf16)
        first = (b == NB - 1)
        lrb = lam[0].astype(bf16)
        lib = lam[1].astype(bf16)
        du_ref[...] = lax.dot_general(lrb, bbr_ref[0, 0], (NT, ((), ())), preferred_element_type=f32) \
            + lax.dot_general(lib, bbi_ref[0, 0], (NT, ((), ())), preferred_element_type=f32)
        dbbr_ref[0, 0] += lax.dot_general(ub, lrb, (TN, ((), ())), preferred_element_type=f32)
        dbbi_ref[0, 0] += lax.dot_general(ub, lib, (TN, ((), ())), preferred_element_type=f32)
        dccr_ref[0, 0] += lax.dot_general(dyb, xr_ref[...].astype(bf16), (TN, ((), ())), preferred_element_type=f32)
        dcci_ref[0, 0] -= lax.dot_general(dyb, xi_ref[...].astype(bf16), (TN, ((), ())), preferred_element_type=f32)
        row8 = lax.broadcasted_iota(jnp.int32, (8, 512), 0)
        if reverse:
            body_x, body_l, edge_l = slice(8, T), slice(0, T - 8), slice(T - 8, T)
            sp_r = jnp.where(row8 == 7, jnp.where(first, 0.0, xbr_ref[0:1, :]), pltpu.roll(xr_ref[0:8, :], 7, axis=0))
            sp_i = jnp.where(row8 == 7, jnp.where(first, 0.0, xbi_ref[0:1, :]), pltpu.roll(xi_ref[0:8, :], 7, axis=0))
        else:
            body_x, body_l, edge_l = slice(0, T - 8), slice(8, T), slice(0, 8)
            sp_r = jnp.where(row8 == 0, jnp.where(first, 0.0, xbr_ref[7:8, :]), pltpu.roll(xr_ref[T - 8:T, :], 1, axis=0))
            sp_i = jnp.where(row8 == 0, jnp.where(first, 0.0, xbi_ref[7:8, :]), pltpu.roll(xi_ref[T - 8:T, :], 1, axis=0))
        xpr, xpi = xr_ref[body_x, :], xi_ref[body_x, :]
        lr, li = lam[0, body_l, :], lam[1, body_l, :]
        er, ei = lam[0, edge_l, :], lam[1, edge_l, :]
        dar_ref[...] += jnp.sum(xpr * lr + xpi * li, axis=0, keepdims=True) + jnp.sum(sp_r * er + sp_i * ei, axis=0, keepdims=True)
        dai_ref[...] += jnp.sum(xpr * li - xpi * lr, axis=0, keepdims=True) + jnp.sum(sp_r * ei - sp_i * er, axis=0, keepdims=True)

    mat = lambda shp: pl.BlockSpec((1, 1) + shp, lambda j, b: (d, j, 0, 0))
    omat = lambda shp: pl.BlockSpec((1, 1) + shp, lambda j, b: (0, j, 0, 0))
    vec = lambda r: pl.BlockSpec((r, 512), lambda j, b: (0, d * S5_NJ + j))
    blk = lambda w: pl.BlockSpec((T, w), lambda j, b: (tb(b), j))
    return pl.pallas_call(
        body, grid=(S5_NJ, NB),
        in_specs=[blk(128), blk(128), blk(512), blk(512),
                  pl.BlockSpec((8, 512), lambda j, b: (bnd(b), j)), pl.BlockSpec((8, 512), lambda j, b: (bnd(b), j)),
                  mat((128, 512)), mat((128, 512)), mat((512, 128)), mat((512, 128)), vec(1), vec(1), vec(L), vec(L)],
        out_specs=[blk(128), omat((128, 512)), omat((128, 512)), omat((128, 512)), omat((128, 512)),
                   pl.BlockSpec((1, 512), lambda j, b: (0, j)), pl.BlockSpec((1, 512), lambda j, b: (0, j))],
        out_shape=[SDS((S, 1024), f32), SDS((1, 8, 128, 512), f32), SDS((1, 8, 128, 512), f32), SDS((1, 8, 128, 512), f32),
                   SDS((1, 8, 128, 512), f32), SDS((1, 4096), f32), SDS((1, 4096), f32)],
        scratch_shapes=[pltpu.VMEM((2, 8, 512), f32), pltpu.VMEM((2, T, 512), f32)],
        compiler_params=_params(("parallel", "arbitrary"), VMEM_BIG), name=name)(
            u, dy, xr, xi, xr, xi, BBr, BBi, CCr, CCi, abr, abi, pwr, pwi)


def _silu_mul(g, u):
    return jax.nn.silu(g) * u


def _mixf(p0, p1, p2, z0, z1, z2):
    return jax.nn.sigmoid(z0) * p0 + jax.nn.sigmoid(z1) * p1 + jax.nn.sigmoid(z2) * p2


def _s5_act(yf, yb, u, dd):
    return jax.nn.gelu(yf + yb + dd * u)


def _glu(a, b):
    return a * jax.nn.sigmoid(b)


def _layer_fwd(x, w, tabs, l):
    S = x.shape[0]
    T = min(S, 512)
    I = S // T
    nm = lambda s: f"L{l}_{s}"
    sv = {'x': x}
    h = _rmsnorm_fwd(x, w['norm1_g'], name=nm("norm1"))
    zr = _mm(h, w['W_ret'], name=nm("in_ret"))
    zm = _mm(h, w['W_mla'], name=nm("in_mla"))
    h_seg = _perm_rows(h, tabs['seg_perm'], name=nm("s5_perm_h"))
    zs = _mm(h_seg, w['W_s5'], name=nm("in_s5"))
    zg = _mm(h, w['W_gate'], name=nm("in_gate"))
    sv.update(h=h, h_seg=h_seg, zr=zr, zm=zm, zs=zs, zg=zg)

    yf, stf = _ret_dir_fwd(zr, w['lg'], tabs['cos_r'], tabs['sin_r'], reverse=False, name=nm("ret_f"))
    yb, stb = _ret_dir_fwd(zr, w['lg'], tabs['cos_r'], tabs['sin_r'], reverse=True, name=nm("ret_b"))
    hd = lambda j: j
    y_ret = _pw(_gn_gate, [yf, yb, zr, w['ret_gn_g']],
                [_row(T, 256, hd), _row(T, 256, hd), _row(T, 256, lambda j: 8 + j), _par(256, hd)],
                [SDS((S, 1024), bf16)], [_row(T, 256, hd)], (RET_HEADS, I), name=nm("ret_gn"))[0]
    sv.update(yf=yf, yb=yb, stf=stf, stb=stb, y_ret=y_ret)

    cqn, ckvn = _pw(lambda a, b, g1, g2: (_rms(a, g1), _rms(b, g2)), [zm, zm, w['mla_q_norm_g'], w['mla_kv_norm_g']],
                    [_row(T, 384), _row(T, 256, lambda j: 2), _par(384), _par(256)],
                    [SDS((S, 384), bf16), SDS((S, 256), bf16)], [_row(T, 384), _row(T, 256)], (1, I), name=nm("mla_norm"))
    q = _mm(cqn, w['W_uq'], name=nm("mla_uq"))
    kv = _mm(ckvn, w['W_ukv'], out_dtype=bf16, name=nm("mla_ukv"))
    sc = (MLA_NOPE + MLA_ROPE) ** -0.5 * math.log2(math.e)
    Q = _pw(lambda xq, cs, sn: jnp.concatenate([xq[:, :128] * sc, _rope(xq[:, 128:], cs, sn, 32) * sc], axis=1),
            [q, tabs['cos_m'], tabs['sin_m']], [_row(T, 256, hd), _row(T, 128), _row(T, 128)],
            [SDS((S, 2048), bf16)], [_row(T, 256, hd)], (MLA_HEADS, I), name=nm("mla_qprep"))[0]
    K = _pw(lambda kn, kr, cs, sn: jnp.concatenate([kn.astype(f32), _rope(kr, cs, sn, 32)], axis=1),
            [kv, zm, tabs['cos_m'], tabs['sin_m']],
            [_row(T, 128, lambda j: 2 * j), _row(T, 128, lambda j: 6), _row(T, 128), _row(T, 128)],
            [SDS((S, 2048), bf16)], [_row(T, 256, hd)], (MLA_HEADS, I), name=nm("mla_kprep"))[0]
    O, Lse = _flash_fwd(Q, K, kv, name=nm("mla_attn"))
    sv.update(cqn=cqn, ckvn=ckvn, kv=kv, Q=Q, K=K, O=O, Lse=Lse)

    s5 = w['s5']
    ysf, xrf, xif = _s5_scan_fwd(zs, s5['BBr'], s5['BBi'], s5['CCr'], s5['CCi'], s5['abr'], s5['abi'], s5['pwr_f'], s5['pwi_f'],
                                 reverse=False, name=nm("s5_f"))
    ysb, xrb, xib = _s5_scan_fwd(zs, s5['BBr'], s5['BBi'], s5['CCr'], s5['CCi'], s5['abr'], s5['abi'], s5['pwr_f'], s5['pwi_f'],
                                 reverse=True, name=nm("s5_b"))
    gact = _pw(_s5_act, [ysf, ysb, zs, w['s5_d']], [_row(T, D), _row(T, D), _row(T, D), _par(D)],
               [SDS((S, D), bf16)], [_row(T, D)], (1, I), name=nm("s5_act"))[0]
    gg = _mm(gact, w['W_glu'], name=nm("s5_glu_mm"))
    y_s5 = _pw(_glu, [gg, gg], [_row(T, D), _row(T, D, lambda j: 1)], [SDS((S, D), bf16)], [_row(T, D)], (1, I),
               name=nm("s5_glu"))[0]
    y_s5 = _perm_rows(y_s5, tabs['seg_unperm'], name=nm("s5_unperm_y"))
    sv.update(ysf=ysf, ysb=ysb, xrf=xrf, xif=xif, xrb=xrb, xib=xib, gact=gact, gg=gg, y_s5=y_s5)

    ys = [y_ret, O, y_s5]
    pr = [_mm(ys[i], w['W_br'][i], name=nm(f"branch{i}")) for i in range(3)]
    mix = _pw(_mixf, pr + [zg, zg, zg],
              [_row(T, D)] * 3 + [_row(T, D), _row(T, D, lambda j: 1), _row(T, D, lambda j: 2)],
              [SDS((S, D), bf16)], [_row(T, D)], (1, I), name=nm("mix"))[0]
    x1 = _mm(mix, w['W_out'], res=x, name=nm("out_proj"))
    h2 = _rmsnorm_fwd(x1, w['norm2_g'], name=nm("norm2"))
    fgu = _mm(h2, w['W_gu'], name=nm("ffn_gu"))
    act = _pw(_silu_mul, [fgu, fgu], [_row(T, 1408, lambda j: j), _row(T, 1408, lambda j: 2 + j)],
              [SDS((S, FFN_H), bf16)], [_row(T, 1408, lambda j: j)], (2, I), name=nm("ffn_act"))[0]
    x2 = _mm(act, w['W_down'], res=x1, name=nm("ffn_down"))
    sv.update(pr=pr, mix=mix, x1=x1, h2=h2, fgu=fgu, act=act)
    return x2, sv


def _vjp_fn(fn, n_primal, cast=None):
    def g(*args):
        _, vjp = jax.vjp(fn, *args[:n_primal])
        return vjp(args[n_primal].astype(f32))
    return g


def _layer_bwd(dx2, w, tabs, sv, l):
    S = dx2.shape[0]
    T = min(S, 512)
    I = S // T
    nm = lambda s: f"L{l}_b_{s}"
    g = {}
    hd = lambda j: j

    dact = _mm(dx2, w['W_down'], tb=True, name=nm("ffn_down_dx"))
    g['W_down'] = _mmT(sv['act'], dx2, name=nm("ffn_down_dw"))
    dfg, dfu = _pw(_vjp_fn(_silu_mul, 2), [sv['fgu'], sv['fgu'], dact],
                   [_row(T, 1408, lambda j: j), _row(T, 1408, lambda j: 2 + j), _row(T, 1408, lambda j: j)],
                   [SDS((S, FFN_H), bf16), SDS((S, FFN_H), bf16)], [_row(T, 1408, lambda j: j)] * 2, (2, I), name=nm("ffn_act"))
    dfgu = jnp.concatenate([dfg, dfu], axis=1)
    g['W_gu'] = _mmT(sv['h2'], dfgu, name=nm("ffn_gu_dw"))
    dh2 = _mm(dfgu, w['W_gu'], tb=True, name=nm("ffn_gu_dx"))
    dx1, g['norm2_g'] = _rmsnorm_bwd(sv['x1'], w['norm2_g'], dh2, dx2, name=nm("norm2"))

    dmix = _mm(dx1, w['W_out'], tb=True, name=nm("out_dx"))
    g['W_out'] = _mmT(sv['mix'], dx1, name=nm("out_dw"))
    zg = sv['zg']
    outs = _pw(_vjp_fn(_mixf, 6), sv['pr'] + [zg, zg, zg, dmix],
               [_row(T, D)] * 3 + [_row(T, D), _row(T, D, lambda j: 1), _row(T, D, lambda j: 2), _row(T, D)],
               [SDS((S, D), bf16)] * 6, [_row(T, D)] * 6, (1, I), name=nm("mix"))
    dpr, dzg = outs[:3], jnp.concatenate(outs[3:], axis=1)
    ys = [sv['y_ret'], sv['O'], sv['y_s5']]
    g['W_br'] = [_mmT(ys[i], dpr[i], name=nm(f"branch{i}_dw")) for i in range(3)]
    dpr_seg = _perm_rows(dpr[2], tabs['seg_perm'], name=nm("s5_perm_dy"))
    dys = [_mm(dpr[i] if i < 2 else dpr_seg, w['W_br'][i], tb=True, out_dtype=bf16 if i == 1 else f32,
               name=nm(f"branch{i}_dx")) for i in range(3)]

    gg = sv['gg']
    dga, dgb = _pw(_vjp_fn(_glu, 2), [gg, gg, dys[2]], [_row(T, D), _row(T, D, lambda j: 1), _row(T, D)],
                   [SDS((S, D), bf16)] * 2, [_row(T, D)] * 2, (1, I), name=nm("s5_glu"))
    dgg = jnp.concatenate([dga, dgb], axis=1)
    g['W_glu'] = _mmT(sv['gact'], dgg, name=nm("s5_glu_dw"))
    dgact = _mm(dgg, w['W_glu'], tb=True, name=nm("s5_glu_dx"))

    def act_bwd(yf, yb, u, dd, ct):
        _, vjp = jax.vjp(_s5_act, yf, yb, u, dd)
        dyf, _, du, ddd = vjp(ct)
        return dyf, du, ddd

    dys5, du_direct, g['s5_d'] = _pw(act_bwd, [sv['ysf'], sv['ysb'], sv['zs'], w['s5_d'], dgact],
                                     [_row(T, D)] * 3 + [_par(D), _row(T, D)],
                                     [SDS((S, D), bf16), SDS((S, D), f32), SDS((1, D), f32)],
                                     [_row(T, D), _row(T, D), _par(D)], (1, I), n_acc=1, name=nm("s5_act"))
    s5 = w['s5']
    rf = _s5_scan_bwd(sv['zs'], dys5, sv['xrf'], sv['xif'], s5['BBr'], s5['BBi'], s5['CCr'], s5['CCi'], s5['abr'], s5['abi'],
                      s5['pwr_a'], s5['pwi_a'], reverse=False, name=nm("s5_f"))
    rb = _s5_scan_bwd(sv['zs'], dys5, sv['xrb'], sv['xib'], s5['BBr'], s5['BBi'], s5['CCr'], s5['CCi'], s5['abr'], s5['abi'],
                      s5['pwr_a'], s5['pwi_a'], reverse=True, name=nm("s5_b"))
    g['s5'] = (rf[1:], rb[1:])
    dzs_seg = _pw(lambda a, b, c: a + b + c, [du_direct, rf[0], rb[0]], [_row(T, D)] * 3, [SDS((S, D), bf16)], [_row(T, D)],
                  (1, I), name=nm("s5_du"))[0]
    dzs = _perm_rows(dzs_seg, tabs['seg_unperm'], name=nm("s5_unperm_dz"))

    delta = _attn_delta(dys[1], sv['O'], name=nm("mla_delta"))
    dQ, dK, dV = _flash_bwd(sv['Q'], sv['K'], sv['kv'], delta, sv['Lse'], dys[1], name=nm("mla_attn"))
    dq_lin, dkv, dkr = _mla_bwd_prep(dQ, dK, dV, tabs['cos_m'], tabs['sin_m'], name=nm("mla_prep"))
    g['W_uq'] = _mmT(sv['cqn'], dq_lin, name=nm("mla_uq_dw"))
    dcqn = _mm(dq_lin, w['W_uq'], tb=True, name=nm("mla_uq_dx"))
    g['W_ukv'] = _mmT(sv['ckvn'], dkv, name=nm("mla_ukv_dw"))
    dckvn = _mm(dkv, w['W_ukv'], tb=True, name=nm("mla_ukv_dx"))
    dzm, g['mla_q_norm_g'], g['mla_kv_norm_g'] = _mla_norm_bwd(sv['zm'], w['mla_q_norm_g'], w['mla_kv_norm_g'], dcqn, dckvn, dkr,
                                                               name=nm("mla_norm"))

    zr = sv['zr']

    def gn_bwd(yf, yb, gt, gn, ct):
        _, vjp = jax.vjp(_gn_gate, yf, yb, gt, gn)
        dyf, _, dgt, dgn = vjp(ct)
        return dyf, dgt, dgn

    dyr, dgate, g['ret_gn_g'] = _pw(gn_bwd, [sv['yf'], sv['yb'], zr, w['ret_gn_g'], dys[0]],
                                    [_row(T, 256, hd), _row(T, 256, hd), _row(T, 256, lambda j: 8 + j), _par(256, hd),
                                     _row(T, 256, hd)],
                                    [SDS((S, 1024), bf16), SDS((S, 1024), bf16), SDS((1, 1024), f32)],
                                    [_row(T, 256, hd), _row(T, 256, hd), _par(256, hd)], (RET_HEADS, I), n_acc=1, name=nm("ret_gn"))
    qf, kf, vf, lgf = _ret_dir_bwd(zr, w['lg'], tabs['cos_r'], tabs['sin_r'], dyr, sv['stf'], reverse=False, name=nm("ret_f"))
    qb, kb, vb, lgb = _ret_dir_bwd(zr, w['lg'], tabs['cos_r'], tabs['sin_r'], dyr, sv['stb'], reverse=True, name=nm("ret_b"))
    g['lg'] = jnp.stack([lgf[:, 0, 0], lgb[:, 0, 0]])
    add2 = lambda a, b: a + b
    dq = _pw(add2, [qf, qb], [_row(T, 512)] * 2, [SDS((S, 512), bf16)], [_row(T, 512)], (1, I), name=nm("ret_dq"))[0]
    dk = _pw(add2, [kf, kb], [_row(T, 512)] * 2, [SDS((S, 512), bf16)], [_row(T, 512)], (1, I), name=nm("ret_dk"))[0]
    dv = _pw(add2, [vf, vb], [_row(T, D)] * 2, [SDS((S, D), bf16)], [_row(T, D)], (1, I), name=nm("ret_dv"))[0]
    dzr = jnp.concatenate([dq, dk, dv, dgate], axis=1)

    h = sv['h']
    g['W_ret'] = _mmT(h, dzr, name=nm("in_ret_dw"))
    g['W_mla'] = _mmT(h, dzm, name=nm("in_mla_dw"))
    g['W_s5'] = _mmT(sv['h_seg'], dzs_seg, name=nm("in_s5_dw"))
    g['W_gate'] = _mmT(h, dzg, name=nm("in_gate_dw"))
    dh = _mm(dzr, w['W_ret'], tb=True, name=nm("in_ret_dx"))
    dh = _mm(dzm, w['W_mla'], tb=True, res=dh, name=nm("in_mla_dx"))
    dh = _mm(dzs, w['W_s5'], tb=True, res=dh, name=nm("in_s5_dx"))
    dh = _mm(dzg, w['W_gate'], tb=True, res=dh, name=nm("in_gate_dx"))
    dx, g['norm1_g'] = _rmsnorm_bwd(sv['x'], w['norm1_g'], dh, dx1, name=nm("norm1"))
    return dx, g


def _loss_head(x, tgt, gain, *, name):
    S, W = x.shape
    T = min(S, 512)

    def loss_fn(xv, gv, tv):
        return 0.5 * jnp.sum(jnp.mean(jnp.square(_rms(xv, gv) - tv), axis=-1, keepdims=True), axis=0, keepdims=True)

    def fn(xv, gv, tv):
        lv, vjp = jax.vjp(lambda a, b: loss_fn(a, b, tv), xv, gv)
        dx, dg = vjp(jnp.ones((1, 1), f32))
        return dx, jnp.broadcast_to(lv, (1, 128)), dg

    return _pw(fn, [x, gain, tgt], [_row(T, W), _par(W), _row(T, W)],
               [SDS((S, W), f32), SDS((1, 128), f32), SDS((1, W), f32)], [_row(T, W), _par(128), _par(W)],
               (1, S // T), n_acc=2, name=name)


def _rope_tabs(S):
    def tab(dim):
        inv = 1.0 / (ROPE_THETA ** (jnp.arange(0, dim, 2, dtype=f32) / dim))
        ang = jnp.arange(S, dtype=f32)[:, None] * inv[None, :]
        return jnp.cos(ang), jnp.sin(ang)

    cr, sr = tab(RET_DK)
    cm, sm = tab(MLA_ROPE)
    z = jnp.zeros((S, 64), f32)
    return {'cos_r': jnp.concatenate([cr, cr], axis=1), 'sin_r': jnp.concatenate([-sr, sr], axis=1),
            'cos_m': jnp.concatenate([cm, cm, z], axis=1), 'sin_m': jnp.concatenate([-sm, sm, z], axis=1),
            'seg_perm': _seg_perm(512, False), 'seg_unperm': _seg_perm(512, True)}


def _bd_B(bb):
    b5 = bb.reshape(16, 2, 8, 8, 64)
    return jnp.einsum('cdjgp,gh->djgchp', b5, jnp.eye(8, dtype=bb.dtype)).reshape(2, 8, 128, 512)


def _bd_B_t(dBB):
    return jnp.einsum('djgcgp->cdjgp', dBB.reshape(2, 8, 8, 16, 8, 64)).reshape(16, 8192)


def _bd_C(c):
    c5 = c.reshape(2, 8, 8, 16, 64)
    return jnp.einsum('djgcp,gh->djgphc', c5, jnp.eye(8, dtype=c.dtype)).reshape(2, 8, 512, 128)


def _s5_rows(p, l):
    a_re = p['s5_a_re'][l].reshape(1, 8192)
    a_im = p['s5_a_im'][l].reshape(1, 8192)
    ldt = jnp.broadcast_to(p['s5_log_dt'][l][:, :, None], (2, S5_G, S5_P)).reshape(1, 8192)
    b_re = p['s5_b_re'][l].transpose(3, 0, 1, 2).reshape(16, 8192)
    b_im = p['s5_b_im'][l].transpose(3, 0, 1, 2).reshape(16, 8192)
    return a_re, a_im, ldt, b_re, b_im


def _layer_weights(big, p, l):
    w_in = big['w_in'][l]
    z = lambda n: jnp.zeros((D, n), w_in.dtype)
    w = {
        'W_ret': w_in[:, 0:3072],
        'W_mla': jnp.concatenate([w_in[:, 3072:3456], z(128), w_in[:, 3456:3712], w_in[:, 3712:3776], z(64)], axis=1),
        'W_s5': w_in[:, 3776:4800],
        'W_gate': w_in[:, 4800:7872],
        'W_uq': jnp.pad(big['mla_w_uq'][l].reshape(MLA_Q_LORA, MLA_HEADS, 192), ((0, 0), (0, 0), (0, 64))).reshape(MLA_Q_LORA, 2048),
        'W_ukv': big['mla_w_ukv'][l],
        'W_glu': big['s5_w_glu'][l],
        'W_br': [big['w_branch'][l, i] for i in range(3)],
        'W_out': big['w_out'][l],
        'W_gu': big['ffn_w_gu'][l],
        'W_down': big['ffn_w_down'][l],
    }
    for n in ('norm1_g', 'ret_gn_g', 'mla_q_norm_g', 'mla_kv_norm_g', 's5_d', 'norm2_g'):
        w[n] = p[n][l][None, :]
    w['lg'] = jax.nn.log_sigmoid(p['ret_decay'][l])
    rows = _s5_rows(p, l)
    abr, abi, bbr, bbi, pwr, pwi = _s5_param_fwd(*rows, name=f"L{l}_s5_param")
    flip = lambda t, first: jnp.concatenate([t[::-1, :4096], t[:, 4096:]] if first else [t[:, :4096], t[::-1, 4096:]], axis=1)
    w['s5'] = {'abr': abr, 'abi': abi, 'BBr': _bd_B(bbr).astype(bf16), 'BBi': _bd_B(bbi).astype(bf16),
               'CCr': _bd_C(p['s5_c_re'][l]).astype(bf16), 'CCi': _bd_C(p['s5_c_im'][l]).astype(bf16),
               'pwr_f': flip(pwr, False), 'pwi_f': flip(pwi, False), 'pwr_a': flip(pwr, True), 'pwi_a': flip(pwi, True),
               'rows': rows}
    return w


def _layer_grads(g, w, p, l):
    out = {}
    m = g['W_mla']
    out['w_in'] = jnp.concatenate([g['W_ret'], m[:, 0:384], m[:, 512:768], m[:, 768:832], g['W_s5'], g['W_gate']], axis=1)
    out['mla_w_uq'] = g['W_uq'].reshape(MLA_Q_LORA, MLA_HEADS, 256)[:, :, :192].reshape(MLA_Q_LORA, 1536)
    out['mla_w_ukv'] = g['W_ukv']
    out['s5_w_glu'] = g['W_glu']
    out['w_branch'] = jnp.stack(g['W_br'])
    out['w_out'] = g['W_out']
    out['ffn_w_gu'] = g['W_gu']
    out['ffn_w_down'] = g['W_down']
    for n in ('norm1_g', 'ret_gn_g', 'mla_q_norm_g', 'mla_kv_norm_g', 's5_d', 'norm2_g'):
        out[n] = g[n][0]
    out['ret_decay'] = g['lg'] * jax.nn.sigmoid(-p['ret_decay'][l])
    (fB_r, fB_i, fC_r, fC_i, fa_r, fa_i), (bB_r, bB_i, bC_r, bC_i, ba_r, ba_i) = g['s5']
    cat = lambda a, b: jnp.concatenate([a, b], axis=0)
    d_bbr = _bd_B_t(cat(fB_r, bB_r))
    d_bbi = _bd_B_t(cat(fB_i, bB_i))
    to_c = lambda t: _bd_B_t(t).reshape(16, 2, S5_G, S5_P).transpose(1, 2, 0, 3)
    out['s5_c_re'] = to_c(cat(fC_r, bC_r))
    out['s5_c_im'] = to_c(cat(fC_i, bC_i))
    d_abr = jnp.concatenate([fa_r, ba_r], axis=1)
    d_abi = jnp.concatenate([fa_i, ba_i], axis=1)
    da_re, da_im, dldt, db_re, db_im = _s5_param_bwd(*w['s5']['rows'], d_abr, d_abi, d_bbr, d_bbi, name=f"L{l}_b_s5_param")
    out['s5_a_re'] = da_re.reshape(2, S5_G, S5_P)
    out['s5_a_im'] = da_im.reshape(2, S5_G, S5_P)
    out['s5_log_dt'] = dldt.reshape(2, S5_G, S5_P).sum(axis=-1)
    out['s5_b_re'] = db_re.reshape(16, 2, S5_G, S5_P).transpose(1, 2, 3, 0)
    out['s5_b_im'] = db_im.reshape(16, 2, S5_G, S5_P).transpose(1, 2, 3, 0)
    return out


def _local_step(x, tgt, big, p):
    S = x.shape[0]
    assert S % 512 == 0
    tabs = _rope_tabs(S)
    ws, svs = [], []
    h = x
    for l in range(DEPTH):
        w = _layer_weights(big, p, l)
        h, sv = _layer_fwd(h, w, tabs, l)
        ws.append(w)
        svs.append(sv)
    dx, lossv, dfinal = _loss_head(h, tgt, p['final_g'][None, :], name="loss_head")
    per_layer = [None] * DEPTH
    for l in reversed(range(DEPTH)):
        dx, g = _layer_bwd(dx, ws[l], tabs, svs[l], l)
        per_layer[l] = _layer_grads(g, ws[l], p, l)
    grads = {n: jnp.stack([per_layer[l][n] for l in range(DEPTH)]) for n in per_layer[0]}
    grads['final_g'] = dfinal[0]
    return lossv[0, 0], dx, grads


_ANY = pl.BlockSpec(memory_space=pl.ANY)


def _place():
    x, y, c = lax.axis_index("x"), lax.axis_index("y"), lax.axis_index("c")
    return x, y, c, [(1 - x, y), (x, 1 - y), (1 - x, 1 - y)]


def _allgather4(arrs, *, name):
    n = len(arrs)

    def body(*refs):
        ins, outs = refs[:n], refs[n:2 * n]
        send, recv, loc = refs[2 * n:]
        x, y, c, chips = _place()
        me = 2 * x + y

        def remote(a, k, slot):
            px, py = chips[k]
            return pltpu.make_async_remote_copy(src_ref=ins[a], dst_ref=outs[a].at[slot], send_sem=send.at[a, k],
                                                recv_sem=recv.at[a, k], device_id=(px, py, c), device_id_type=MESH)

        mine = [pltpu.make_async_copy(ins[a], outs[a].at[me], loc.at[a]) for a in range(n)]
        for cp in mine:
            cp.start()
        sends = [remote(a, k, me) for a in range(n) for k in range(3)]
        for cp in sends:
            cp.start()
        for a in range(n):
            for k, (px, py) in enumerate(chips):
                remote(a, k, 2 * px + py).wait_recv()
        for cp in sends:
            cp.wait_send()
        for cp in mine:
            cp.wait()

    return pl.pallas_call(
        body, in_specs=[_ANY] * n, out_specs=[_ANY] * n, out_shape=[SDS((4,) + a.shape, a.dtype) for a in arrs],
        scratch_shapes=[pltpu.SemaphoreType.DMA((n, 3)), pltpu.SemaphoreType.DMA((n, 3)), pltpu.SemaphoreType.DMA((n,))],
        name=name)(*arrs)


def _rs_exchange(parts, *, name):
    n = len(parts)

    def body(*refs):
        ins, owns, gots = refs[:n], refs[n:2 * n], refs[2 * n:3 * n]
        send, recv, loc = refs[3 * n:]
        x, y, c, chips = _place()
        me = 2 * x + y

        def remote(a, k):
            px, py = chips[k]
            return pltpu.make_async_remote_copy(src_ref=ins[a].at[2 * px + py], dst_ref=gots[a].at[k], send_sem=send.at[a, k],
                                                recv_sem=recv.at[a, k], device_id=(px, py, c), device_id_type=MESH)

        mine = [pltpu.make_async_copy(ins[a].at[me], owns[a], loc.at[a]) for a in range(n)]
        for cp in mine:
            cp.start()
        sends = [remote(a, k) for a in range(n) for k in range(3)]
        for cp in sends:
            cp.start()
        for cp in sends:
            cp.wait_recv()
        for cp in sends:
            cp.wait_send()
        for cp in mine:
            cp.wait()

    return pl.pallas_call(
        body, in_specs=[_ANY] * n, out_specs=[_ANY] * (2 * n),
        out_shape=[SDS(a.shape[1:], a.dtype) for a in parts] + [SDS((3,) + a.shape[1:], a.dtype) for a in parts],
        scratch_shapes=[pltpu.SemaphoreType.DMA((n, 3)), pltpu.SemaphoreType.DMA((n, 3)), pltpu.SemaphoreType.DMA((n,))],
        name=name)(*parts)


def _gather_split(arrs, *, name):
    n = len(arrs)

    def body(*refs):
        ins, outs = refs[:n], refs[n:2 * n]
        s_ici, r_ici, s_sib, r_sib, loc = refs[2 * n:]
        x, y, c, chips = _place()
        me = 2 * x + y
        ids = [2 * px + py for px, py in chips] + [me]

        def over_ici(a, k, slot):
            px, py = chips[k]
            return pltpu.make_async_remote_copy(src_ref=ins[a].at[c], dst_ref=outs[a].at[slot, c], send_sem=s_ici.at[a, k],
                                                recv_sem=r_ici.at[a, k], device_id=(px, py, c), device_id_type=MESH)

        def to_sibling(a, k, half, src=None):
            blk = outs[a].at[ids[k], half]
            return pltpu.make_async_remote_copy(src_ref=blk if src is None else src, dst_ref=blk, send_sem=s_sib.at[a, k],
                                                recv_sem=r_sib.at[a, k], device_id=(x, y, 1 - c), device_id_type=MESH)

        mine = [pltpu.make_async_copy(ins[a].at[c], outs[a].at[me, c], loc.at[a]) for a in range(n)]
        for cp in mine:
            cp.start()
        sends = [over_ici(a, k, me) for a in range(n) for k in range(3)]
        sends += [to_sibling(a, 3, c, src=ins[a].at[c]) for a in range(n)]
        for cp in sends:
            cp.start()
        for a in range(n):
            for k in range(3):
                over_ici(a, k, ids[k]).wait_recv()
                fwd = to_sibling(a, k, c)
                fwd.start()
                sends.append(fwd)
        for a in range(n):
            for k in range(4):
                to_sibling(a, k, 1 - c).wait_recv()
        for cp in sends:
            cp.wait_send()
        for cp in mine:
            cp.wait()

    dma = pltpu.SemaphoreType.DMA
    return pl.pallas_call(
        body, in_specs=[_ANY] * n, out_specs=[_ANY] * n, out_shape=[SDS((4,) + a.shape, a.dtype) for a in arrs],
        scratch_shapes=[dma((n, 3)), dma((n, 3)), dma((n, 4)), dma((n, 4)), dma((n,))], name=name)(*arrs)


def _swap_halves(parts, *, name):
    n = len(parts)

    def body(*refs):
        ins, mines, gots = refs[:n], refs[n:2 * n], refs[2 * n:3 * n]
        send, recv, loc = refs[3 * n:]
        x, y, c, _ = _place()
        local = [pltpu.make_async_copy(ins[a].at[q, c], mines[a].at[q], loc.at[a, q]) for a in range(n) for q in range(4)]
        for cp in local:
            cp.start()
        cps = [pltpu.make_async_remote_copy(src_ref=ins[a].at[q, 1 - c], dst_ref=gots[a].at[q], send_sem=send.at[a, q],
                                            recv_sem=recv.at[a, q], device_id=(x, y, 1 - c), device_id_type=MESH)
               for a in range(n) for q in range(4)]
        for cp in cps:
            cp.start()
        for cp in cps:
            cp.wait_recv()
        for cp in cps:
            cp.wait_send()
        for cp in local:
            cp.wait()

    dma = pltpu.SemaphoreType.DMA
    half = lambda a: SDS((4,) + a.shape[2:], a.dtype)
    return pl.pallas_call(
        body, in_specs=[_ANY] * n, out_specs=[_ANY] * (2 * n), out_shape=[half(a) for a in parts] * 2,
        scratch_shapes=[dma((n, 4)), dma((n, 4)), dma((n, 4))], name=name)(*parts)


def _pair_halves(arrs, *, name):
    n = len(arrs)

    def body(*refs):
        ins, outs = refs[:n], refs[n:2 * n]
        send, recv, loc = refs[2 * n:]
        x, y, c, _ = _place()
        local = [pltpu.make_async_copy(ins[a], outs[a].at[c], loc.at[a]) for a in range(n)]
        for cp in local:
            cp.start()
        cps = [pltpu.make_async_remote_copy(src_ref=ins[a], dst_ref=outs[a].at[c], send_sem=send.at[a], recv_sem=recv.at[a],
                                            device_id=(x, y, 1 - c), device_id_type=MESH) for a in range(n)]
        for cp in cps:
            cp.start()
        for a in range(n):
            pltpu.make_async_remote_copy(src_ref=ins[a], dst_ref=outs[a].at[1 - c], send_sem=send.at[a], recv_sem=recv.at[a],
                                         device_id=(x, y, 1 - c), device_id_type=MESH).wait_recv()
        for cp in cps:
            cp.wait_send()
        for cp in local:
            cp.wait()

    dma = pltpu.SemaphoreType.DMA
    return pl.pallas_call(
        body, in_specs=[_ANY] * n, out_specs=[_ANY] * n, out_shape=[SDS((2,) + a.shape, a.dtype) for a in arrs],
        scratch_shapes=[dma((n,)), dma((n,)), dma((n,))], name=name)(*arrs)


def _row_tile(R):
    return R if R <= 256 else next(t for t in (256, 128, 64, 32, 16) if R % t == 0)


def _add2(a, b, *, name):
    R, W = a.shape
    tr = _row_tile(R)
    return _pw(lambda p, q: p.astype(f32) + q.astype(f32), [a, b], [_row(tr, W)] * 2, [SDS((R, W), a.dtype)], [_row(tr, W)],
               (1, R // tr), name=name)[0]


def _sum4(own, got, *, name):
    R, W = own.shape
    tr = _row_tile(R)
    g3 = lambda k: pl.BlockSpec((None, tr, W), lambda j, i: (k, i, 0))
    up = lambda t: t.astype(f32)
    return _pw(lambda a, b, c, d: ((up(a) + up(b)) + up(c)) + up(d), [own, got, got, got], [_row(tr, W), g3(0), g3(1), g3(2)],
               [SDS((R, W), f32)], [_row(tr, W)], (1, R // tr), name=name)[0]


def _adamw(g, w, m, v, *, name):
    R, W = w.shape
    tr = _row_tile(R)

    def fn(gv, wv, mv, vv):
        m2 = ADAM_B1 * mv + (1.0 - ADAM_B1) * gv
        v2 = ADAM_B2 * vv + (1.0 - ADAM_B2) * jnp.square(gv)
        m_hat = m2 / (1.0 - ADAM_B1 ** ADAM_STEP)
        v_hat = v2 / (1.0 - ADAM_B2 ** ADAM_STEP)
        return -ADAM_LR * (m_hat / (jnp.sqrt(v_hat) + ADAM_EPS) + ADAM_WD * wv), m2, v2

    return _pw(fn, [g, w, m, v], [_row(tr, W)] * 4, [SDS((R, W), f32)] * 3, [_row(tr, W)] * 3, (1, R // tr), name=name)


def _to_parts(g, axis):
    shp = g.shape
    g = g.reshape(shp[:axis] + (4, shp[axis] // 4) + shp[axis + 1:])
    return jnp.moveaxis(g, axis, 0)


def _from_parts(pt, axis):
    g = jnp.moveaxis(pt, 0, axis)
    shp = g.shape
    return g.reshape(shp[:axis] + (4 * shp[axis + 1],) + shp[axis + 2:])


def kernel(x, norm1_g, w_in, ret_decay, ret_gn_g, mla_q_norm_g, mla_w_uq, mla_kv_norm_g, mla_w_ukv, s5_a_re, s5_a_im, s5_log_dt, s5_b_re, s5_b_im, s5_c_re, s5_c_im, s5_d, s5_w_glu, w_branch, w_out, norm2_g, ffn_w_gu, ffn_w_down, final_g, loss_target, m_norm1_g, m_w_in, m_ret_decay, m_ret_gn_g, m_mla_q_norm_g, m_mla_w_uq, m_mla_kv_norm_g, m_mla_w_ukv, m_s5_a_re, m_s5_a_im, m_s5_log_dt, m_s5_b_re, m_s5_b_im, m_s5_c_re, m_s5_c_im, m_s5_d, m_s5_w_glu, m_w_branch, m_w_out, m_norm2_g, m_ffn_w_gu, m_ffn_w_down, m_final_g, v_norm1_g, v_w_in, v_ret_decay, v_ret_gn_g, v_mla_q_norm_g, v_mla_w_uq, v_mla_kv_norm_g, v_mla_w_ukv, v_s5_a_re, v_s5_a_im, v_s5_log_dt, v_s5_b_re, v_s5_b_im, v_s5_c_re, v_s5_c_im, v_s5_d, v_s5_w_glu, v_w_branch, v_w_out, v_norm2_g, v_ffn_w_gu, v_ffn_w_down, v_final_g):
    wv = dict(zip(W_NAMES, (norm1_g, w_in, ret_decay, ret_gn_g, mla_q_norm_g, mla_w_uq, mla_kv_norm_g, mla_w_ukv, s5_a_re, s5_a_im,
                            s5_log_dt, s5_b_re, s5_b_im, s5_c_re, s5_c_im, s5_d, s5_w_glu, w_branch, w_out, norm2_g, ffn_w_gu,
                            ffn_w_down, final_g)))
    mv = dict(zip(W_NAMES, (m_norm1_g, m_w_in, m_ret_decay, m_ret_gn_g, m_mla_q_norm_g, m_mla_w_uq, m_mla_kv_norm_g, m_mla_w_ukv,
                            m_s5_a_re, m_s5_a_im, m_s5_log_dt, m_s5_b_re, m_s5_b_im, m_s5_c_re, m_s5_c_im, m_s5_d, m_s5_w_glu,
                            m_w_branch, m_w_out, m_norm2_g, m_ffn_w_gu, m_ffn_w_down, m_final_g)))
    vv = dict(zip(W_NAMES, (v_norm1_g, v_w_in, v_ret_decay, v_ret_gn_g, v_mla_q_norm_g, v_mla_w_uq, v_mla_kv_norm_g, v_mla_w_ukv,
                            v_s5_a_re, v_s5_a_im, v_s5_log_dt, v_s5_b_re, v_s5_b_im, v_s5_c_re, v_s5_c_im, v_s5_d, v_s5_w_glu,
                            v_w_branch, v_w_out, v_norm2_g, v_ffn_w_gu, v_ffn_w_down, v_final_g)))
    big_names = list(BIG)

    gathered = _gather_split([wv[n].astype(bf16) for n in big_names], name="gather_weights")
    big = {n: _from_parts(gt, BIG[n]) for n, gt in zip(big_names, gathered)}
    small = {n: wv[n] for n in SMALL}

    loss_local, dx, grads = _local_step(x[0], loss_target[0], big, small)

    n_rows = {n: -(-math.prod(wv[n].shape) // 1024) * 8 for n in SMALL}
    used = sum(n_rows.values())
    rows_q = -(-(used + 8) // (4 * 128)) * 128

    def as_rows(d, tail=None):
        blocks = [jnp.pad(d[n].reshape(-1), (0, n_rows[n] * 128 - math.prod(wv[n].shape))).reshape(n_rows[n], 128) for n in SMALL]
        blocks.append(jnp.zeros((8, 128), f32) if tail is None else tail)
        blocks.append(jnp.zeros((4 * rows_q - used - 8, 128), f32))
        return jnp.concatenate(blocks, axis=0)

    loss_rows = jnp.full((8, 128), loss_local, f32)
    parts = [_to_parts(grads[n].astype(bf16), BIG[n]) for n in big_names]
    parts.append(as_rows(grads, loss_rows).reshape(4, 2, rows_q // 2, 128))
    n_arr = len(parts)
    two_d = lambda a: a.reshape(-1, a.shape[-1])
    swapped = _swap_halves(parts, name="grad_swap_halves")
    chip_sums = [_add2(two_d(swapped[a]), two_d(swapped[n_arr + a]), name=f"grad_add2_{a}").reshape(swapped[a].shape)
                 for a in range(n_arr)]
    got = _rs_exchange(chip_sums, name="grad_exchange")
    sums = [_sum4(two_d(got[a]), got[n_arr + a].reshape(3, -1, got[a].shape[-1]), name=f"grad_sum4_{a}") for a in range(n_arr)]
    full = _pair_halves(sums, name="grad_pair_halves")

    out_g, out_d, out_m, out_v = {}, {}, {}, {}
    for a, n in enumerate(big_names):
        shp = wv[n].shape
        res = _adamw(two_d(full[a]), two_d(wv[n]), two_d(mv[n]), two_d(vv[n]), name=f"adamw_{n}")
        out_g[n] = full[a].reshape(shp)
        out_d[n], out_m[n], out_v[n] = [r.reshape(shp) for r in res]
    g_small = _allgather4([full[-1].reshape(rows_q, 128)], name="gather_small_grads")[0].reshape(4 * rows_q, 128)
    loss = g_small[used, 0]
    res = (g_small,) + tuple(_adamw(g_small, as_rows(wv), as_rows(mv), as_rows(vv), name="adamw_small"))
    off = 0
    for n in SMALL:
        k = math.prod(wv[n].shape)
        for dst, r in zip((out_g, out_d, out_m, out_v), res):
            dst[n] = r[off:off + n_rows[n]].reshape(-1)[:k].reshape(wv[n].shape)
        off += n_rows[n]
    return (loss, dx[None], *[out_g[n] for n in W_NAMES], *[out_d[n] for n in W_NAMES], *[out_m[n] for n in W_NAMES],
            *[out_v[n] for n in W_NAMES])
```

```python
import functools
import math

import jax
import jax.numpy as jnp
from jax import lax
from jax.experimental import pallas as pl
from jax.experimental.pallas import tpu as pltpu

f32 = jnp.float32
bf16 = jnp.bfloat16
SDS = jax.ShapeDtypeStruct
MESH = pl.DeviceIdType.MESH

D = 1024
DEPTH = 2
RMS_EPS = 1e-6
GN_EPS = 1e-5
ROPE_THETA = 10000.0
RET_HEADS = 4
RET_DK = 128
RET_DV = 256
RET_CHUNK = 128
MLA_HEADS = 8
MLA_Q_LORA = 384
MLA_KV_LORA = 256
MLA_NOPE = 128
MLA_ROPE = 64
MLA_V = 128
MLA_QW = 256
S5_G = 64
S5_P = 64
S5_C = 16
S5_NJ = 8
S5_SEG = 8
FFN_H = 2816
ADAM_LR = 0.001
ADAM_B1 = 0.9
ADAM_B2 = 0.999
ADAM_EPS = 1e-08
ADAM_WD = 0.01
ADAM_STEP = 10
VMEM_BIG = 56 * 1024 * 1024

W_NAMES = ['norm1_g', 'w_in', 'ret_decay', 'ret_gn_g', 'mla_q_norm_g', 'mla_w_uq', 'mla_kv_norm_g', 'mla_w_ukv',
           's5_a_re', 's5_a_im', 's5_log_dt', 's5_b_re', 's5_b_im', 's5_c_re', 's5_c_im', 's5_d', 's5_w_glu',
           'w_branch', 'w_out', 'norm2_g', 'ffn_w_gu', 'ffn_w_down', 'final_g']
BIG = {'w_in': 2, 'mla_w_uq': 2, 'mla_w_ukv': 2, 's5_w_glu': 2, 'w_branch': 2, 'w_out': 1, 'ffn_w_gu': 2, 'ffn_w_down': 1}
SMALL = [n for n in W_NAMES if n not in BIG]


def _pick(n, cands=(512, 384, 256, 128)):
    if n <= 1024:
        return n
    for c in cands:
        if n % c == 0:
            return c
    raise ValueError(n)


def _params(sem, vmem=None):
    return pltpu.CompilerParams(dimension_semantics=sem, vmem_limit_bytes=vmem)


def _mm(a, b, *, tb=False, res=None, out_dtype=f32, name):
    M, K = a.shape
    N = b.shape[0] if tb else b.shape[1]
    tn = _pick(N)
    tk = K if K <= 3072 else _pick(K, (1408, 1024, 512))
    nk = K // tk
    tm = _pick(M)
    if M % 1024 == 0 and 1024 * tk * a.dtype.itemsize <= 4 * 1024 * 1024:
        tm = 1024
    assert M % tm == 0 and N % tn == 0 and K % tk == 0

    def body(*refs):
        if res is None:
            a_ref, b_ref, o_ref, acc = refs
        else:
            a_ref, b_ref, r_ref, o_ref, acc = refs
        k = pl.program_id(2)
        dn = (((1,), (1 if tb else 0,)), ((), ()))
        part = lax.dot_general(a_ref[...].astype(bf16), b_ref[...].astype(bf16), dn, preferred_element_type=f32)

        @pl.when(k == 0)
        def _():
            acc[...] = part

        @pl.when(k > 0)
        def _():
            acc[...] += part

        @pl.when(k == nk - 1)
        def _():
            v = acc[...]
            if res is not None:
                v = v + r_ref[...]
            o_ref[...] = v.astype(out_dtype)

    in_specs = [pl.BlockSpec((tm, tk), lambda i, j, k: (i, k)),
                pl.BlockSpec((tn, tk), lambda i, j, k: (j, k)) if tb else pl.BlockSpec((tk, tn), lambda i, j, k: (k, j))]
    args = [a, b]
    if res is not None:
        in_specs.append(pl.BlockSpec((tm, tn), lambda i, j, k: (i, j)))
        args.append(res)
    return pl.pallas_call(
        body, grid=(M // tm, N // tn, nk), in_specs=in_specs,
        out_specs=pl.BlockSpec((tm, tn), lambda i, j, k: (i, j)),
        out_shape=SDS((M, N), out_dtype), scratch_shapes=[pltpu.VMEM((tm, tn), f32)],
        compiler_params=_params(("parallel", "parallel", "arbitrary"), VMEM_BIG), name=name)(*args)


def _mmT(a, b, *, name):
    S, M = a.shape
    N = b.shape[1]
    tm = _pick(M)
    tn = _pick(N)
    tk = min(S, 1024)
    nk = S // tk

    def body(a_ref, b_ref, o_ref):
        k = pl.program_id(2)
        part = lax.dot_general(a_ref[...].astype(bf16), b_ref[...].astype(bf16), (((0,), (0,)), ((), ())),
                               preferred_element_type=f32)

        @pl.when(k == 0)
        def _():
            o_ref[...] = part

        @pl.when(k > 0)
        def _():
            o_ref[...] += part

    return pl.pallas_call(
        body, grid=(M // tm, N // tn, nk),
        in_specs=[pl.BlockSpec((tk, tm), lambda i, j, k: (k, i)), pl.BlockSpec((tk, tn), lambda i, j, k: (k, j))],
        out_specs=pl.BlockSpec((tm, tn), lambda i, j, k: (i, j)),
        out_shape=SDS((M, N), f32),
        compiler_params=_params(("parallel", "parallel", "arbitrary"), VMEM_BIG), name=name)(a, b)


def _pw(fn, ins, in_specs, outs, out_specs, grid, *, n_acc=0, name):
    n_in = len(ins)
    n_out = len(outs)

    def body(*refs):
        vals = fn(*[r[...] for r in refs[:n_in]])
        if not isinstance(vals, (tuple, list)):
            vals = (vals,)
        orefs = refs[n_in:]
        for r, v in zip(orefs[:n_out - n_acc], vals[:n_out - n_acc]):
            r[...] = v.astype(r.dtype)
        if n_acc:
            i = pl.program_id(1)

            @pl.when(i == 0)
            def _():
                for r, v in zip(orefs[n_out - n_acc:], vals[n_out - n_acc:]):
                    r[...] = v.astype(r.dtype)

            @pl.when(i > 0)
            def _():
                for r, v in zip(orefs[n_out - n_acc:], vals[n_out - n_acc:]):
                    r[...] += v.astype(r.dtype)

    res = pl.pallas_call(
        body, grid=grid, in_specs=in_specs, out_specs=out_specs, out_shape=outs,
        compiler_params=_params(("parallel", "arbitrary"), VMEM_BIG), name=name)(*ins)
    return res


def _row(T, w, col=None):
    if col is None:
        return pl.BlockSpec((T, w), lambda j, i: (i, 0))
    return pl.BlockSpec((T, w), lambda j, i: (i, col(j)))


def _par(w, col=None):
    if col is None:
        return pl.BlockSpec((1, w), lambda j, i: (0, 0))
    return pl.BlockSpec((1, w), lambda j, i: (0, col(j)))


def _rms(x, g):
    return x * lax.rsqrt(jnp.mean(x * x, axis=-1, keepdims=True) + RMS_EPS) * g


def _rope(x, cos, sinm, half):
    if half == 64:
        partner = pltpu.roll(x, 64, axis=1)
    else:
        lane = lax.broadcasted_iota(jnp.int32, x.shape, 1)
        partner = jnp.where((lane % (2 * half)) < half, pltpu.roll(x, 128 - half, axis=1), pltpu.roll(x, half, axis=1))
    return x * cos + partner * sinm


def _rope_t(x, cos, sinm, half):
    return _rope(x, cos, -sinm, half)


def _rmsnorm_fwd(x, g, *, name):
    S, W = x.shape
    T = min(S, 512)
    return _pw(lambda xv, gv: _rms(xv, gv), [x, g], [_row(T, W), _par(W)], [SDS((S, W), bf16)], [_row(T, W)],
               (1, S // T), name=name)[0]


def _rmsnorm_bwd(x, g, dh, dres, *, name):
    S, W = x.shape
    T = min(S, 512)

    def fn(xv, gv, dhv, drv):
        _, vjp = jax.vjp(_rms, xv, gv)
        dx, dg = vjp(dhv)
        return dx + drv, dg

    return _pw(fn, [x, g, dh, dres], [_row(T, W), _par(W), _row(T, W), _row(T, W)],
               [SDS((S, W), f32), SDS((1, W), f32)], [_row(T, W), _par(W)], (1, S // T), n_acc=1, name=name)


def _ret_tables(lg, reverse):
    C = RET_CHUNK
    ii = lax.broadcasted_iota(jnp.int32, (C, C), 0).astype(f32)
    jj = lax.broadcasted_iota(jnp.int32, (C, C), 1).astype(f32)
    if not reverse:
        E = ii - jj
        mask = E >= 0
        eq = ii + 1.0
        ek = (C - 1.0) - ii
    else:
        E = jj - ii
        mask = E > 0
        eq = C - ii
        ek = ii
    Dm = jnp.where(mask, jnp.exp(jnp.where(mask, E, 0.0) * lg), 0.0)
    Em = jnp.where(mask, E, 0.0)
    qw = jnp.exp(eq * lg)
    kw = jnp.exp(ek * lg)
    qw2 = jnp.concatenate([qw, qw], axis=1)
    return Dm, Em, eq, ek, qw, kw, qw2, jnp.exp(C * lg)


def _dot(a, b, dims):
    return lax.dot_general(a.astype(bf16), b.astype(bf16), (dims, ((), ())), preferred_element_type=f32)


NN = ((1,), (0,))
NT = ((1,), (1,))
TN = ((0,), (0,))


def _ret_dir_fwd(zr, lg, cos, sinm, *, reverse, name):
    S = zr.shape[0]
    C = RET_CHUNK
    TB = min(S, 512)
    nc = TB // C
    NB = S // TB
    d = 1 if reverse else 0
    scale = RET_DK ** -0.5

    def tb(b):
        return (NB - 1 - b) if reverse else b

    def body(lg_ref, q_ref, k_ref, v_ref, cos_ref, sin_ref, y_ref, st_ref, state):
        h = pl.program_id(0)
        b = pl.program_id(1)

        @pl.when(b == 0)
        def _():
            state[...] = jnp.zeros_like(state)

        Dm, _, _, _, _, kw, qw2, gC = _ret_tables(lg_ref[d, h], reverse)
        order = range(nc - 1, -1, -1) if reverse else range(nc)
        for c in order:
            rows = pl.ds(c * C, C)
            q = _rope(q_ref[rows, :], cos_ref[rows, :], sin_ref[rows, :], 64) * scale
            k = _rope(k_ref[rows, :], cos_ref[rows, :], sin_ref[rows, :], 64)
            v = v_ref[rows, :]
            st = state[...]
            st_ref[0, c] = st
            s = _dot(q, k, NT) * Dm
            o = _dot(s, v, NN) + _dot(q, st, NN) * qw2
            y_ref[rows, :] = o
            state[...] = gC * st + _dot(k * kw, v, TN)

    return pl.pallas_call(
        body, grid=(RET_HEADS, NB),
        in_specs=[pl.BlockSpec(memory_space=pltpu.SMEM),
                  pl.BlockSpec((TB, 128), lambda h, b: (tb(b), h)),
                  pl.BlockSpec((TB, 128), lambda h, b: (tb(b), 4 + h)),
                  pl.BlockSpec((TB, 256), lambda h, b: (tb(b), 4 + h)),
                  pl.BlockSpec((TB, 128), lambda h, b: (tb(b), 0)),
                  pl.BlockSpec((TB, 128), lambda h, b: (tb(b), 0))],
        out_specs=[pl.BlockSpec((TB, 256), lambda h, b: (tb(b), h)),
                   pl.BlockSpec((1, nc, 128, 256), lambda h, b: (h, tb(b), 0, 0))],
        out_shape=[SDS((S, 1024), f32), SDS((RET_HEADS, S // C, 128, 256), f32)],
        scratch_shapes=[pltpu.VMEM((128, 256), f32)],
        compiler_params=_params(("parallel", "arbitrary")), name=name)(lg, zr, zr, zr, cos, sinm)


def _ret_dir_bwd(zr, lg, cos, sinm, dy, states, *, reverse, name):
    S = zr.shape[0]
    C = RET_CHUNK
    TB = min(S, 512)
    nc = TB // C
    NB = S // TB
    d = 1 if reverse else 0
    scale = RET_DK ** -0.5

    def tb(b):
        return b if reverse else (NB - 1 - b)

    def body(lg_ref, q_ref, k_ref, v_ref, cos_ref, sin_ref, dy_ref, st_ref, dq_ref, dk_ref, dv_ref, dlg_ref, dstate):
        h = pl.program_id(0)
        b = pl.program_id(1)

        @pl.when(b == 0)
        def _():
            dstate[...] = jnp.zeros_like(dstate)
            dlg_ref[...] = jnp.zeros_like(dlg_ref)

        Dm, Em, eq, ek, qw, kw, qw2, gC = _ret_tables(lg_ref[d, h], reverse)
        order = range(nc) if reverse else range(nc - 1, -1, -1)
        dlg = jnp.zeros((), f32)
        for c in order:
            rows = pl.ds(c * C, C)
            cs, sn = cos_ref[rows, :], sin_ref[rows, :]
            q = _rope(q_ref[rows, :], cs, sn, 64) * scale
            k = _rope(k_ref[rows, :], cs, sn, 64)
            v = v_ref[rows, :]
            do = dy_ref[rows, :]
            st = st_ref[0, c]
            ds = dstate[...]
            p = _dot(q, k, NT)
            a = p * Dm
            dp = _dot(do, v, NT) * Dm
            dq_cross = _dot(do, st, NT) * qw
            dk_cross = _dot(v, ds, NT) * kw
            dq = _dot(dp, k, NN) + dq_cross
            dk = _dot(dp, q, TN) + dk_cross
            dv = _dot(a, do, TN) + _dot(k * kw, ds, NN)
            dlg = dlg + jnp.sum(dp * p * Em) + jnp.sum(dq_cross * q * eq) + jnp.sum(dk_cross * k * ek) \
                + C * gC * jnp.sum(ds * st)
            dstate[...] = gC * ds + _dot(q * qw, do, TN)
            dq_ref[rows, :] = _rope_t(dq, cs, sn, 64) * scale
            dk_ref[rows, :] = _rope_t(dk, cs, sn, 64)
            dv_ref[rows, :] = dv
        dlg_ref[...] += jnp.full(dlg_ref.shape, dlg, f32)

    return pl.pallas_call(
        body, grid=(RET_HEADS, NB),
        in_specs=[pl.BlockSpec(memory_space=pltpu.SMEM),
                  pl.BlockSpec((TB, 128), lambda h, b: (tb(b), h)),
                  pl.BlockSpec((TB, 128), lambda h, b: (tb(b), 4 + h)),
                  pl.BlockSpec((TB, 256), lambda h, b: (tb(b), 4 + h)),
                  pl.BlockSpec((TB, 128), lambda h, b: (tb(b), 0)),
                  pl.BlockSpec((TB, 128), lambda h, b: (tb(b), 0)),
                  pl.BlockSpec((TB, 256), lambda h, b: (tb(b), h)),
                  pl.BlockSpec((1, nc, 128, 256), lambda h, b: (h, tb(b), 0, 0))],
        out_specs=[pl.BlockSpec((TB, 128), lambda h, b: (tb(b), h)),
                   pl.BlockSpec((TB, 128), lambda h, b: (tb(b), h)),
                   pl.BlockSpec((TB, 256), lambda h, b: (tb(b), h)),
                   pl.BlockSpec((1, 1, 128), lambda h, b: (h, 0, 0))],
        out_shape=[SDS((S, 512), f32), SDS((S, 512), f32), SDS((S, 1024), f32), SDS((RET_HEADS, 1, 128), f32)],
        scratch_shapes=[pltpu.VMEM((128, 256), f32)],
        compiler_params=_params(("parallel", "arbitrary")), name=name)(lg, zr, zr, zr, cos, sinm, dy, states)


def _gn_gate(yf, yb, g, gn):
    y = yf + yb
    mu = jnp.mean(y, axis=-1, keepdims=True)
    var = jnp.mean(jnp.square(y - mu), axis=-1, keepdims=True)
    yn = (y - mu) * lax.rsqrt(var + GN_EPS)
    return jax.nn.silu(g) * (yn * gn)


def _flash_fwd(Q, K, kv, *, name):
    S = Q.shape[0]
    hq = min(S, 512)
    nh = 2 if S % 1024 == 0 else 1
    tq = nh * hq
    tk = min(S, 512)
    nk = S // tk

    def body(q_ref, k_ref, v_ref, o_ref, l_ref, m_s, l_s, acc):
        kk = pl.program_id(2)

        @pl.when(kk == 0)
        def _():
            m_s[...] = jnp.full_like(m_s, -jnp.inf)
            l_s[...] = jnp.zeros_like(l_s)
            acc[...] = jnp.zeros_like(acc)

        k = k_ref[...]
        v = v_ref[...]
        sts = [lax.dot_general(k, q_ref[hf * hq:(hf + 1) * hq, :], (NT, ((), ())), preferred_element_type=f32)
               for hf in range(nh)]
        for hf in range(nh):
            st = sts[hf]
            m_prev = m_s[hf]
            m_new = jnp.maximum(m_prev, jnp.max(st, axis=0, keepdims=True))
            pt = jnp.exp2(st - m_new)
            alpha = jnp.exp2(m_prev - m_new)
            l_s[hf] = alpha * l_s[hf] + jnp.sum(pt, axis=0, keepdims=True)
            acc[hf] = alpha * acc[hf] + lax.dot_general(v, pt.astype(bf16), (TN, ((), ())), preferred_element_type=f32)
            m_s[hf] = m_new

        @pl.when(kk == nk - 1)
        def _():
            for hf in range(nh):
                o_ref[hf * hq:(hf + 1) * hq, :] = jnp.transpose(acc[hf] / l_s[hf]).astype(bf16)
                l_ref[0, :, hf * hq:(hf + 1) * hq] = m_s[hf] + jnp.log2(l_s[hf])

    return pl.pallas_call(
        body, grid=(MLA_HEADS, S // tq, nk),
        in_specs=[pl.BlockSpec((tq, 256), lambda h, i, k: (i, h)),
                  pl.BlockSpec((tk, 256), lambda h, i, k: (k, h)),
                  pl.BlockSpec((tk, 128), lambda h, i, k: (k, 2 * h + 1))],
        out_specs=[pl.BlockSpec((tq, 128), lambda h, i, k: (i, h)), pl.BlockSpec((1, 1, tq), lambda h, i, k: (h, 0, i))],
        out_shape=[SDS((S, 1024), bf16), SDS((MLA_HEADS, 1, S), f32)],
        scratch_shapes=[pltpu.VMEM((nh, 1, hq), f32), pltpu.VMEM((nh, 1, hq), f32), pltpu.VMEM((nh, 128, hq), f32)],
        compiler_params=_params(("parallel", "parallel", "arbitrary")), name=name)(Q, K, kv)


def _attn_delta(dO, O, *, name):
    S = dO.shape[0]
    T = min(S, 512)

    def body(do_ref, o_ref, d_ref):
        prod = do_ref[...] * o_ref[...].astype(f32)
        d_ref[0] = lax.dot_general(jnp.ones((8, 128), f32), prod, (NT, ((), ())), preferred_element_type=f32,
                                   precision=lax.Precision.HIGHEST)[0:1, :]

    return pl.pallas_call(
        body, grid=(MLA_HEADS, S // T),
        in_specs=[pl.BlockSpec((T, 128), lambda h, i: (i, h)), pl.BlockSpec((T, 128), lambda h, i: (i, h))],
        out_specs=pl.BlockSpec((1, 1, T), lambda h, i: (h, 0, i)), out_shape=SDS((MLA_HEADS, 1, S), f32),
        compiler_params=_params(("parallel", "parallel")), name=name)(dO, O)


def _flash_bwd(Q, K, kv, delta, L, dO, *, name):
    S = Q.shape[0]
    hq = min(S, 512)
    nh = 2 if S % 1024 == 0 else 1
    tq = nh * hq
    tk = min(S, 512)
    nq = S // tq
    ln2 = math.log(2.0)

    def body(q_ref, k_ref, v_ref, dl_ref, l_ref, do_ref, dq_ref, dk_ref, dv_ref, dk_acc, dv_acc):
        kk = pl.program_id(1)
        i = pl.program_id(2)

        @pl.when((kk == 0) & (i == 0))
        def _():
            dq_ref[...] = jnp.zeros_like(dq_ref)

        @pl.when(i == 0)
        def _():
            dk_acc[...] = jnp.zeros_like(dk_acc)
            dv_acc[...] = jnp.zeros_like(dv_acc)

        k = k_ref[...]
        v = v_ref[...]
        dk_new = dk_acc[...]
        dv_new = dv_acc[...]
        for hf in range(nh):
            sl = slice(hf * hq, (hf + 1) * hq)
            q = q_ref[sl, :]
            st = lax.dot_general(k, q, (NT, ((), ())), preferred_element_type=f32)
            pt = jnp.exp2(st - l_ref[0, :, sl])
            delta = dl_ref[0, :, sl]
            dob = do_ref[sl, :].astype(bf16)
            dv_new = dv_new + lax.dot_general(pt.astype(bf16), dob, (NN, ((), ())), preferred_element_type=f32)
            dpt = lax.dot_general(v, dob, (NT, ((), ())), preferred_element_type=f32)
            dst = (pt * (dpt - delta)).astype(bf16)
            dk_new = dk_new + lax.dot_general(dst, q, (NN, ((), ())), preferred_element_type=f32)
            rows = pl.ds(pl.multiple_of(i * tq + hf * hq, hq), hq)
            dq_ref[rows, :] += lax.dot_general(dst, k, (TN, ((), ())), preferred_element_type=f32)
        dk_acc[...] = dk_new
        dv_acc[...] = dv_new

        @pl.when(i == nq - 1)
        def _():
            dk_ref[...] = dk_acc[...] * ln2
            dv_ref[...] = dv_acc[...]

    return pl.pallas_call(
        body, grid=(MLA_HEADS, S // tk, nq),
        in_specs=[pl.BlockSpec((tq, 256), lambda h, k, i: (i, h)),
                  pl.BlockSpec((tk, 256), lambda h, k, i: (k, h)),
                  pl.BlockSpec((tk, 128), lambda h, k, i: (k, 2 * h + 1)),
                  pl.BlockSpec((1, 1, tq), lambda h, k, i: (h, 0, i)),
                  pl.BlockSpec((1, 1, tq), lambda h, k, i: (h, 0, i)),
                  pl.BlockSpec((tq, 128), lambda h, k, i: (i, h))],
        out_specs=[pl.BlockSpec((S, 256), lambda h, k, i: (0, h)),
                   pl.BlockSpec((tk, 256), lambda h, k, i: (k, h)),
                   pl.BlockSpec((tk, 128), lambda h, k, i: (k, h))],
        out_shape=[SDS((S, 2048), f32), SDS((S, 2048), f32), SDS((S, 1024), f32)],
        scratch_shapes=[pltpu.VMEM((tk, 256), f32), pltpu.VMEM((tk, 128), f32)],
        compiler_params=_params(("parallel", "arbitrary", "arbitrary"), VMEM_BIG), name=name)(Q, K, kv, delta, L, dO)


def _mla_bwd_prep(dQ, dK, dV, cosm, sinm, *, name):
    S = dQ.shape[0]
    T = min(S, 256)
    scale = (MLA_NOPE + MLA_ROPE) ** -0.5

    def body(dq_ref, dk_ref, dv_ref, cos_ref, sin_ref, oq_ref, okv_ref, okr_ref):
        cs, sn = cos_ref[...], sin_ref[...]
        kr = jnp.zeros((T, 128), f32)
        for h in range(MLA_HEADS):
            a = 256 * h
            oq_ref[:, a:a + 128] = (dq_ref[:, a:a + 128] * scale).astype(bf16)
            oq_ref[:, a + 128:a + 256] = (_rope_t(dq_ref[:, a + 128:a + 256], cs, sn, 32) * scale).astype(bf16)
            okv_ref[:, a:a + 128] = dk_ref[:, a:a + 128].astype(bf16)
            okv_ref[:, a + 128:a + 256] = dv_ref[:, 128 * h:128 * h + 128].astype(bf16)
            kr = kr + dk_ref[:, a + 128:a + 256]
        okr_ref[...] = _rope_t(kr, cs, sn, 32)

    return pl.pallas_call(
        body, grid=(S // T,),
        in_specs=[pl.BlockSpec((T, 2048), lambda i: (i, 0)), pl.BlockSpec((T, 2048), lambda i: (i, 0)),
                  pl.BlockSpec((T, 1024), lambda i: (i, 0)), pl.BlockSpec((T, 128), lambda i: (i, 0)),
                  pl.BlockSpec((T, 128), lambda i: (i, 0))],
        out_specs=[pl.BlockSpec((T, 2048), lambda i: (i, 0)), pl.BlockSpec((T, 2048), lambda i: (i, 0)),
                   pl.BlockSpec((T, 128), lambda i: (i, 0))],
        out_shape=[SDS((S, 2048), bf16), SDS((S, 2048), bf16), SDS((S, 128), f32)],
        compiler_params=_params(("parallel",), VMEM_BIG), name=name)(dQ, dK, dV, cosm, sinm)


def _mla_norm_bwd(zm, qg, kvg, dcqn, dckvn, dkr, *, name):
    S = zm.shape[0]
    T = min(S, 512)

    def body(cq_ref, ckv_ref, qg_ref, kvg_ref, dcq_ref, dckv_ref, dkr_ref, o_ref, dqg_ref, dkvg_ref):
        i = pl.program_id(0)
        _, vjp = jax.vjp(_rms, cq_ref[...], qg_ref[...])
        dcq, dqg = vjp(dcq_ref[...])
        _, vjp2 = jax.vjp(_rms, ckv_ref[...], kvg_ref[...])
        dckv, dkvg = vjp2(dckv_ref[...])
        o_ref[:, 0:384] = dcq.astype(bf16)
        o_ref[:, 384:512] = jnp.zeros((T, 128), bf16)
        o_ref[:, 512:768] = dckv.astype(bf16)
        o_ref[:, 768:896] = dkr_ref[...].astype(bf16)

        @pl.when(i == 0)
        def _():
            dqg_ref[...] = dqg
            dkvg_ref[...] = dkvg

        @pl.when(i > 0)
        def _():
            dqg_ref[...] += dqg
            dkvg_ref[...] += dkvg

    return pl.pallas_call(
        body, grid=(S // T,),
        in_specs=[pl.BlockSpec((T, 384), lambda i: (i, 0)), pl.BlockSpec((T, 256), lambda i: (i, 2)),
                  pl.BlockSpec((1, 384), lambda i: (0, 0)), pl.BlockSpec((1, 256), lambda i: (0, 0)),
                  pl.BlockSpec((T, 384), lambda i: (i, 0)), pl.BlockSpec((T, 256), lambda i: (i, 0)),
                  pl.BlockSpec((T, 128), lambda i: (i, 0))],
        out_specs=[pl.BlockSpec((T, 896), lambda i: (i, 0)), pl.BlockSpec((1, 384), lambda i: (0, 0)),
                   pl.BlockSpec((1, 256), lambda i: (0, 0))],
        out_shape=[SDS((S, 896), bf16), SDS((1, 384), f32), SDS((1, 256), f32)],
        compiler_params=_params(("arbitrary",)), name=name)(zm, zm, qg, kvg, dcqn, dckvn, dkr)


def _s5_disc(a_re, a_im, ldt, b_re, b_im):
    dt = jnp.exp(ldt)
    ar = jnp.minimum(a_re, -1e-4)
    mag = jnp.exp(dt * ar)
    abr = mag * jnp.cos(dt * a_im)
    abi = mag * jnp.sin(dt * a_im)
    den = ar * ar + a_im * a_im
    nr = abr - 1.0
    ni = abi
    cr = (nr * ar + ni * a_im) / den
    ci = (ni * ar - nr * a_im) / den
    return abr, abi, cr * b_re - ci * b_im, cr * b_im + ci * b_re


def _s5_param_fwd(a_re, a_im, ldt, b_re, b_im, *, name):
    R = SDS((1, 8192), f32)
    M = SDS((16, 8192), f32)
    Pw = SDS((64, 8192), f32)

    def body(a_re_r, a_im_r, ldt_r, b_re_r, b_im_r, o1, o2, o3, o4, p_re, p_im):
        abr, abi, bbr, bbi = _s5_disc(a_re_r[...], a_im_r[...], ldt_r[...], b_re_r[...], b_im_r[...])
        o1[...] = abr
        o2[...] = abi
        o3[...] = bbr
        o4[...] = bbi
        dt = jnp.exp(ldt_r[...])
        ar = jnp.minimum(a_re_r[...], -1e-4)
        n = lax.broadcasted_iota(jnp.int32, (64, 8192), 0).astype(f32) + 1.0
        mag = jnp.exp(n * (dt * ar))
        ang = n * (dt * a_im_r[...])
        p_re[...] = mag * jnp.cos(ang)
        p_im[...] = mag * jnp.sin(ang)

    return pl.pallas_call(body, out_shape=[R, R, M, M, Pw, Pw], name=name)(a_re, a_im, ldt, b_re, b_im)


def _s5_param_bwd(a_re, a_im, ldt, b_re, b_im, d_abr, d_abi, d_bbr, d_bbi, *, name):
    R = SDS((1, 8192), f32)
    M = SDS((16, 8192), f32)

    def body(a_re_r, a_im_r, ldt_r, b_re_r, b_im_r, c1, c2, c3, c4, o1, o2, o3, o4, o5):
        _, vjp = jax.vjp(_s5_disc, a_re_r[...], a_im_r[...], ldt_r[...], b_re_r[...], b_im_r[...])
        g = vjp((c1[...], c2[...], c3[...], c4[...]))
        for o, v in zip((o1, o2, o3, o4, o5), g):
            o[...] = v

    return pl.pallas_call(body, out_shape=[R, R, R, M, M], name=name)(a_re, a_im, ldt, b_re, b_im, d_abr, d_abi, d_bbr, d_bbi)


def _seg_perm(T, inverse):
    L = T // S5_SEG
    i = jnp.arange(T)
    src = (i % S5_SEG) * L + i // S5_SEG
    P = (src[:, None] == jnp.arange(T)[None, :]).astype(bf16)
    return P.T if inverse else P


def _perm_rows(a, P, *, name):
    S, W = a.shape
    T = P.shape[0]

    def body(p_ref, a_ref, o_ref):
        o_ref[...] = lax.dot_general(p_ref[...], a_ref[...], (NN, ((), ())), preferred_element_type=f32).astype(o_ref.dtype)

    return pl.pallas_call(
        body, grid=(S // T,), in_specs=[pl.BlockSpec((T, T), lambda i: (0, 0)), pl.BlockSpec((T, W), lambda i: (i, 0))],
        out_specs=pl.BlockSpec((T, W), lambda i: (i, 0)), out_shape=SDS((S, W), a.dtype),
        compiler_params=_params(("parallel",)), name=name)(P, a)


def _scan_core(xr, xi, ar, ai, pwr_ref, pwi_ref, a64r, a64i, carry, *, reverse, T, conj):
    L = T // S5_SEG
    sg = -1.0 if conj else 1.0
    arb = jnp.broadcast_to(ar, (8, 512))
    aib = jnp.broadcast_to(ai, (8, 512))
    UN = 4

    def step(r4, c):
        cr, ci = c
        for u in range(UN):
            r0 = r4 * UN + u
            r = (L - 1 - r0) if reverse else r0
            rows = pl.ds(pl.multiple_of(r * 8, 8), 8)
            nr = arb * cr - aib * ci + xr[rows, :]
            ni = arb * ci + aib * cr + xi[rows, :]
            xr[rows, :] = nr
            xi[rows, :] = ni
            cr, ci = nr, ni
        return cr, ci

    lr, li = lax.fori_loop(0, L // UN, step, (jnp.zeros((8, 512), f32), jnp.zeros((8, 512), f32)))
    row8 = lax.broadcasted_iota(jnp.int32, (8, 512), 0)
    cr = carry[0, 0:1, :]
    ci = carry[1, 0:1, :]
    a6i = sg * a64i
    cin_r = jnp.zeros((8, 512), f32)
    cin_i = jnp.zeros((8, 512), f32)
    for seg in (range(S5_SEG - 1, -1, -1) if reverse else range(S5_SEG)):
        cin_r = jnp.where(row8 == seg, cr, cin_r)
        cin_i = jnp.where(row8 == seg, ci, cin_i)
        ncr = lr[seg:seg + 1, :] + a64r * cr - a6i * ci
        nci = li[seg:seg + 1, :] + a64r * ci + a6i * cr
        cr, ci = ncr, nci
    carry[0, 0:1, :] = cr
    carry[1, 0:1, :] = ci

    def fix(r4, _):
        for u in range(UN):
            r = r4 * UN + u
            rows = pl.ds(pl.multiple_of(r * 8, 8), 8)
            pr = pwr_ref[pl.ds(r, 1), :]
            pi = sg * pwi_ref[pl.ds(r, 1), :]
            xr[rows, :] += pr * cin_r - pi * cin_i
            xi[rows, :] += pr * cin_i + pi * cin_r
        return 0

    lax.fori_loop(0, L // UN, fix, 0)


def _s5_scan_fwd(u, BBr, BBi, CCr, CCi, abr, abi, pwr, pwi, *, reverse, name):
    S = u.shape[0]
    T = min(S, 512)
    NB = S // T
    L = T // S5_SEG
    d = 1 if reverse else 0

    def tb(b):
        return (NB - 1 - b) if reverse else b

    def body(u_ref, bbr_ref, bbi_ref, ccr_ref, cci_ref, ar_ref, ai_ref, pwr_ref, pwi_ref, y_ref, xr_ref, xi_ref, carry):
        b = pl.program_id(1)

        @pl.when(b == 0)
        def _():
            carry[...] = jnp.zeros_like(carry)

        ub = u_ref[...].astype(bf16)
        xr_ref[...] = lax.dot_general(ub, bbr_ref[0, 0], (NN, ((), ())), preferred_element_type=f32)
        xi_ref[...] = lax.dot_general(ub, bbi_ref[0, 0], (NN, ((), ())), preferred_element_type=f32)
        a6 = (0 if reverse else L - 1)
        _scan_core(xr_ref, xi_ref, ar_ref[...], ai_ref[...], pwr_ref, pwi_ref, pwr_ref[a6:a6 + 1, :], pwi_ref[a6:a6 + 1, :],
                   carry, reverse=reverse, T=T, conj=False)
        y_ref[...] = _dot(xr_ref[...], ccr_ref[0, 0], NN) - _dot(xi_ref[...], cci_ref[0, 0], NN)

    mat = lambda shp: pl.BlockSpec((1, 1) + shp, lambda j, b: (d, j, 0, 0))
    vec = lambda r: pl.BlockSpec((r, 512), lambda j, b: (0, d * S5_NJ + j))
    return pl.pallas_call(
        body, grid=(S5_NJ, NB),
        in_specs=[pl.BlockSpec((T, 128), lambda j, b: (tb(b), j)), mat((128, 512)), mat((128, 512)), mat((512, 128)),
                  mat((512, 128)), vec(1), vec(1), vec(L), vec(L)],
        out_specs=[pl.BlockSpec((T, 128), lambda j, b: (tb(b), j)), pl.BlockSpec((T, 512), lambda j, b: (tb(b), j)),
                   pl.BlockSpec((T, 512), lambda j, b: (tb(b), j))],
        out_shape=[SDS((S, 1024), f32), SDS((S, 4096), f32), SDS((S, 4096), f32)],
        scratch_shapes=[pltpu.VMEM((2, 8, 512), f32)],
        compiler_params=_params(("parallel", "arbitrary")), name=name)(u, BBr, BBi, CCr, CCi, abr, abi, pwr, pwi)


def _s5_scan_bwd(u, dy, xr, xi, BBr, BBi, CCr, CCi, abr, abi, pwr, pwi, *, reverse, name):
    S = u.shape[0]
    T = min(S, 512)
    NB = S // T
    L = T // S5_SEG
    d = 1 if reverse else 0
    adj_rev = not reverse

    def tb(b):
        return b if reverse else (NB - 1 - b)

    def bnd(b):
        t = tb(b)
        if reverse:
            return jnp.minimum((t + 1) * (T // 8), S // 8 - 1)
        return jnp.maximum(t * (T // 8) - 1, 0)

    def body(u_ref, dy_ref, xr_ref, xi_ref, xbr_ref, xbi_ref, bbr_ref, bbi_ref, ccr_ref, cci_ref, ar_ref, ai_ref,
             pwr_ref, pwi_ref, du_ref, dbbr_ref, dbbi_ref, dccr_ref, dcci_ref, dar_ref, dai_ref, carry, lam):
        b = pl.program_id(1)

        @pl.when(b == 0)
        def _():
            carry[...] = jnp.zeros_like(carry)
            for r in (dbbr_ref, dbbi_ref, dccr_ref, dcci_ref, dar_ref, dai_ref):
                r[...] = jnp.zeros_like(r)

        dyb = dy_ref[...]
        lam[0] = lax.dot_general(dyb, ccr_ref[0, 0], (NT, ((), ())), preferred_element_type=f32)
        lam[1] = -lax.dot_general(dyb, cci_ref[0, 0], (NT, ((), ())), preferred_element_type=f32)
        a6 = (0 if adj_rev else L - 1)
        _scan_core(lam.at[0], lam.at[1], ar_ref[...], -ai_ref[...], pwr_ref, pwi_ref, pwr_ref[a6:a6 + 1, :],
                   pwi_ref[a6:a6 + 1, :], carry, reverse=adj_rev, T=T, conj=True)
        ub = u_ref[...].astype(bf16)
        first = (b == NB - 1)
        lrb = lam[0].astype(bf16)
        lib = lam[1].astype(bf16)
        du_ref[...] = lax.dot_general(lrb, bbr_ref[0, 0], (NT, ((), ())), preferred_element_type=f32) \
            + lax.dot_general(lib, bbi_ref[0, 0], (NT, ((), ())), preferred_element_type=f32)
        dbbr_ref[0, 0] += lax.dot_general(ub, lrb, (TN, ((), ())), preferred_element_type=f32)
        dbbi_ref[0, 0] += lax.dot_general(ub, lib, (TN, ((), ())), preferred_element_type=f32)
        dccr_ref[0, 0] += lax.dot_general(dyb, xr_ref[...].astype(bf16), (TN, ((), ())), preferred_element_type=f32)
        dcci_ref[0, 0] -= lax.dot_general(dyb, xi_ref[...].astype(bf16), (TN, ((), ())), preferred_element_type=f32)
        row8 = lax.broadcasted_iota(jnp.int32, (8, 512), 0)
        if reverse:
            body_x, body_l, edge_l = slice(8, T), slice(0, T - 8), slice(T - 8, T)
            sp_r = jnp.where(row8 == 7, jnp.where(first, 0.0, xbr_ref[0:1, :]), pltpu.roll(xr_ref[0:8, :], 7, axis=0))
            sp_i = jnp.where(row8 == 7, jnp.where(first, 0.0, xbi_ref[0:1, :]), pltpu.roll(xi_ref[0:8, :], 7, axis=0))
        else:
            body_x, body_l, edge_l = slice(0, T - 8), slice(8, T), slice(0, 8)
            sp_r = jnp.where(row8 == 0, jnp.where(first, 0.0, xbr_ref[7:8, :]), pltpu.roll(xr_ref[T - 8:T, :], 1, axis=0))
            sp_i = jnp.where(row8 == 0, jnp.where(first, 0.0, xbi_ref[7:8, :]), pltpu.roll(xi_ref[T - 8:T, :], 1, axis=0))
        xpr, xpi = xr_ref[body_x, :], xi_ref[body_x, :]
        lr, li = lam[0, body_l, :], lam[1, body_l, :]
        er, ei = lam[0, edge_l, :], lam[1, edge_l, :]
        dar_ref[...] += jnp.sum(xpr * lr + xpi * li, axis=0, keepdims=True) + jnp.sum(sp_r * er + sp_i * ei, axis=0, keepdims=True)
        dai_ref[...] += jnp.sum(xpr * li - xpi * lr, axis=0, keepdims=True) + jnp.sum(sp_r * ei - sp_i * er, axis=0, keepdims=True)

    mat = lambda shp: pl.BlockSpec((1, 1) + shp, lambda j, b: (d, j, 0, 0))
    omat = lambda shp: pl.BlockSpec((1, 1) + shp, lambda j, b: (0, j, 0, 0))
    vec = lambda r: pl.BlockSpec((r, 512), lambda j, b: (0, d * S5_NJ + j))
    blk = lambda w: pl.BlockSpec((T, w), lambda j, b: (tb(b), j))
    return pl.pallas_call(
        body, grid=(S5_NJ, NB),
        in_specs=[blk(128), blk(128), blk(512), blk(512),
                  pl.BlockSpec((8, 512), lambda j, b: (bnd(b), j)), pl.BlockSpec((8, 512), lambda j, b: (bnd(b), j)),
                  mat((128, 512)), mat((128, 512)), mat((512, 128)), mat((512, 128)), vec(1), vec(1), vec(L), vec(L)],
        out_specs=[blk(128), omat((128, 512)), omat((128, 512)), omat((128, 512)), omat((128, 512)),
                   pl.BlockSpec((1, 512), lambda j, b: (0, j)), pl.BlockSpec((1, 512), lambda j, b: (0, j))],
        out_shape=[SDS((S, 1024), f32), SDS((1, 8, 128, 512), f32), SDS((1, 8, 128, 512), f32), SDS((1, 8, 128, 512), f32),
                   SDS((1, 8, 128, 512), f32), SDS((1, 4096), f32), SDS((1, 4096), f32)],
        scratch_shapes=[pltpu.VMEM((2, 8, 512), f32), pltpu.VMEM((2, T, 512), f32)],
        compiler_params=_params(("parallel", "arbitrary"), VMEM_BIG), name=name)(
            u, dy, xr, xi, xr, xi, BBr, BBi, CCr, CCi, abr, abi, pwr, pwi)


def _silu_mul(g, u):
    return jax.nn.silu(g) * u


def _mixf(p0, p1, p2, z0, z1, z2):
    return jax.nn.sigmoid(z0) * p0 + jax.nn.sigmoid(z1) * p1 + jax.nn.sigmoid(z2) * p2


def _s5_act(yf, yb, u, dd):
    return jax.nn.gelu(yf + yb + dd * u)


def _glu(a, b):
    return a * jax.nn.sigmoid(b)


def _layer_fwd(x, w, tabs, l):
    S = x.shape[0]
    T = min(S, 512)
    I = S // T
    nm = lambda s: f"L{l}_{s}"
    sv = {'x': x}
    h = _rmsnorm_fwd(x, w['norm1_g'], name=nm("norm1"))
    zr = _mm(h, w['W_ret'], name=nm("in_ret"))
    zm = _mm(h, w['W_mla'], name=nm("in_mla"))
    h_seg = _perm_rows(h, tabs['seg_perm'], name=nm("s5_perm_h"))
    zs = _mm(h_seg, w['W_s5'], name=nm("in_s5"))
    zg = _mm(h, w['W_gate'], name=nm("in_gate"))
    sv.update(h=h, h_seg=h_seg, zr=zr, zm=zm, zs=zs, zg=zg)

    yf, stf = _ret_dir_fwd(zr, w['lg'], tabs['cos_r'], tabs['sin_r'], reverse=False, name=nm("ret_f"))
    yb, stb = _ret_dir_fwd(zr, w['lg'], tabs['cos_r'], tabs['sin_r'], reverse=True, name=nm("ret_b"))
    hd = lambda j: j
    y_ret = _pw(_gn_gate, [yf, yb, zr, w['ret_gn_g']],
                [_row(T, 256, hd), _row(T, 256, hd), _row(T, 256, lambda j: 8 + j), _par(256, hd)],
                [SDS((S, 1024), bf16)], [_row(T, 256, hd)], (RET_HEADS, I), name=nm("ret_gn"))[0]
    sv.update(yf=yf, yb=yb, stf=stf, stb=stb, y_ret=y_ret)

    cqn, ckvn = _pw(lambda a, b, g1, g2: (_rms(a, g1), _rms(b, g2)), [zm, zm, w['mla_q_norm_g'], w['mla_kv_norm_g']],
                    [_row(T, 384), _row(T, 256, lambda j: 2), _par(384), _par(256)],
                    [SDS((S, 384), bf16), SDS((S, 256), bf16)], [_row(T, 384), _row(T, 256)], (1, I), name=nm("mla_norm"))
    q = _mm(cqn, w['W_uq'], name=nm("mla_uq"))
    kv = _mm(ckvn, w['W_ukv'], out_dtype=bf16, name=nm("mla_ukv"))
    sc = (MLA_NOPE + MLA_ROPE) ** -0.5 * math.log2(math.e)
    Q = _pw(lambda xq, cs, sn: jnp.concatenate([xq[:, :128] * sc, _rope(xq[:, 128:], cs, sn, 32) * sc], axis=1),
            [q, tabs['cos_m'], tabs['sin_m']], [_row(T, 256, hd), _row(T, 128), _row(T, 128)],
            [SDS((S, 2048), bf16)], [_row(T, 256, hd)], (MLA_HEADS, I), name=nm("mla_qprep"))[0]
    K = _pw(lambda kn, kr, cs, sn: jnp.concatenate([kn.astype(f32), _rope(kr, cs, sn, 32)], axis=1),
            [kv, zm, tabs['cos_m'], tabs['sin_m']],
            [_row(T, 128, lambda j: 2 * j), _row(T, 128, lambda j: 6), _row(T, 128), _row(T, 128)],
            [SDS((S, 2048), bf16)], [_row(T, 256, hd)], (MLA_HEADS, I), name=nm("mla_kprep"))[0]
    O, Lse = _flash_fwd(Q, K, kv, name=nm("mla_attn"))
    sv.update(cqn=cqn, ckvn=ckvn, kv=kv, Q=Q, K=K, O=O, Lse=Lse)

    s5 = w['s5']
    ysf, xrf, xif = _s5_scan_fwd(zs, s5['BBr'], s5['BBi'], s5['CCr'], s5['CCi'], s5['abr'], s5['abi'], s5['pwr_f'], s5['pwi_f'],
                                 reverse=False, name=nm("s5_f"))
    ysb, xrb, xib = _s5_scan_fwd(zs, s5['BBr'], s5['BBi'], s5['CCr'], s5['CCi'], s5['abr'], s5['abi'], s5['pwr_f'], s5['pwi_f'],
                                 reverse=True, name=nm("s5_b"))
    gact = _pw(_s5_act, [ysf, ysb, zs, w['s5_d']], [_row(T, D), _row(T, D), _row(T, D), _par(D)],
               [SDS((S, D), bf16)], [_row(T, D)], (1, I), name=nm("s5_act"))[0]
    gg = _mm(gact, w['W_glu'], name=nm("s5_glu_mm"))
    y_s5 = _pw(_glu, [gg, gg], [_row(T, D), _row(T, D, lambda j: 1)], [SDS((S, D), bf16)], [_row(T, D)], (1, I),
               name=nm("s5_glu"))[0]
    y_s5 = _perm_rows(y_s5, tabs['seg_unperm'], name=nm("s5_unperm_y"))
    sv.update(ysf=ysf, ysb=ysb, xrf=xrf, xif=xif, xrb=xrb, xib=xib, gact=gact, gg=gg, y_s5=y_s5)

    ys = [y_ret, O, y_s5]
    pr = [_mm(ys[i], w['W_br'][i], name=nm(f"branch{i}")) for i in range(3)]
    mix = _pw(_mixf, pr + [zg, zg, zg],
              [_row(T, D)] * 3 + [_row(T, D), _row(T, D, lambda j: 1), _row(T, D, lambda j: 2)],
              [SDS((S, D), bf16)], [_row(T, D)], (1, I), name=nm("mix"))[0]
    x1 = _mm(mix, w['W_out'], res=x, name=nm("out_proj"))
    h2 = _rmsnorm_fwd(x1, w['norm2_g'], name=nm("norm2"))
    fgu = _mm(h2, w['W_gu'], name=nm("ffn_gu"))
    act = _pw(_silu_mul, [fgu, fgu], [_row(T, 1408, lambda j: j), _row(T, 1408, lambda j: 2 + j)],
              [SDS((S, FFN_H), bf16)], [_row(T, 1408, lambda j: j)], (2, I), name=nm("ffn_act"))[0]
    x2 = _mm(act, w['W_down'], res=x1, name=nm("ffn_down"))
    sv.update(pr=pr, mix=mix, x1=x1, h2=h2, fgu=fgu, act=act)
    return x2, sv


def _vjp_fn(fn, n_primal, cast=None):
    def g(*args):
        _, vjp = jax.vjp(fn, *args[:n_primal])
        return vjp(args[n_primal].astype(f32))
    return g


def _layer_bwd(dx2, w, tabs, sv, l):
    S = dx2.shape[0]
    T = min(S, 512)
    I = S // T
    nm = lambda s: f"L{l}_b_{s}"
    g = {}
    hd = lambda j: j

    dact = _mm(dx2, w['W_down'], tb=True, name=nm("ffn_down_dx"))
    g['W_down'] = _mmT(sv['act'], dx2, name=nm("ffn_down_dw"))
    dfg, dfu = _pw(_vjp_fn(_silu_mul, 2), [sv['fgu'], sv['fgu'], dact],
                   [_row(T, 1408, lambda j: j), _row(T, 1408, lambda j: 2 + j), _row(T, 1408, lambda j: j)],
                   [SDS((S, FFN_H), bf16), SDS((S, FFN_H), bf16)], [_row(T, 1408, lambda j: j)] * 2, (2, I), name=nm("ffn_act"))
    dfgu = jnp.concatenate([dfg, dfu], axis=1)
    g['W_gu'] = _mmT(sv['h2'], dfgu, name=nm("ffn_gu_dw"))
    dh2 = _mm(dfgu, w['W_gu'], tb=True, name=nm("ffn_gu_dx"))
    dx1, g['norm2_g'] = _rmsnorm_bwd(sv['x1'], w['norm2_g'], dh2, dx2, name=nm("norm2"))

    dmix = _mm(dx1, w['W_out'], tb=True, name=nm("out_dx"))
    g['W_out'] = _mmT(sv['mix'], dx1, name=nm("out_dw"))
    zg = sv['zg']
    outs = _pw(_vjp_fn(_mixf, 6), sv['pr'] + [zg, zg, zg, dmix],
               [_row(T, D)] * 3 + [_row(T, D), _row(T, D, lambda j: 1), _row(T, D, lambda j: 2), _row(T, D)],
               [SDS((S, D), bf16)] * 6, [_row(T, D)] * 6, (1, I), name=nm("mix"))
    dpr, dzg = outs[:3], jnp.concatenate(outs[3:], axis=1)
    ys = [sv['y_ret'], sv['O'], sv['y_s5']]
    g['W_br'] = [_mmT(ys[i], dpr[i], name=nm(f"branch{i}_dw")) for i in range(3)]
    dpr_seg = _perm_rows(dpr[2], tabs['seg_perm'], name=nm("s5_perm_dy"))
    dys = [_mm(dpr[i] if i < 2 else dpr_seg, w['W_br'][i], tb=True, out_dtype=bf16 if i == 1 else f32,
               name=nm(f"branch{i}_dx")) for i in range(3)]

    gg = sv['gg']
    dga, dgb = _pw(_vjp_fn(_glu, 2), [gg, gg, dys[2]], [_row(T, D), _row(T, D, lambda j: 1), _row(T, D)],
                   [SDS((S, D), bf16)] * 2, [_row(T, D)] * 2, (1, I), name=nm("s5_glu"))
    dgg = jnp.concatenate([dga, dgb], axis=1)
    g['W_glu'] = _mmT(sv['gact'], dgg, name=nm("s5_glu_dw"))
    dgact = _mm(dgg, w['W_glu'], tb=True, name=nm("s5_glu_dx"))

    def act_bwd(yf, yb, u, dd, ct):
        _, vjp = jax.vjp(_s5_act, yf, yb, u, dd)
        dyf, _, du, ddd = vjp(ct)
        return dyf, du, ddd

    dys5, du_direct, g['s5_d'] = _pw(act_bwd, [sv['ysf'], sv['ysb'], sv['zs'], w['s5_d'], dgact],
                                     [_row(T, D)] * 3 + [_par(D), _row(T, D)],
                                     [SDS((S, D), bf16), SDS((S, D), f32), SDS((1, D), f32)],
                                     [_row(T, D), _row(T, D), _par(D)], (1, I), n_acc=1, name=nm("s5_act"))
    s5 = w['s5']
    rf = _s5_scan_bwd(sv['zs'], dys5, sv['xrf'], sv['xif'], s5['BBr'], s5['BBi'], s5['CCr'], s5['CCi'], s5['abr'], s5['abi'],
                      s5['pwr_a'], s5['pwi_a'], reverse=False, name=nm("s5_f"))
    rb = _s5_scan_bwd(sv['zs'], dys5, sv['xrb'], sv['xib'], s5['BBr'], s5['BBi'], s5['CCr'], s5['CCi'], s5['abr'], s5['abi'],
                      s5['pwr_a'], s5['pwi_a'], reverse=True, name=nm("s5_b"))
    g['s5'] = (rf[1:], rb[1:])
    dzs_seg = _pw(lambda a, b, c: a + b + c, [du_direct, rf[0], rb[0]], [_row(T, D)] * 3, [SDS((S, D), bf16)], [_row(T, D)],
                  (1, I), name=nm("s5_du"))[0]
    dzs = _perm_rows(dzs_seg, tabs['seg_unperm'], name=nm("s5_unperm_dz"))

    delta = _attn_delta(dys[1], sv['O'], name=nm("mla_delta"))
    dQ, dK, dV = _flash_bwd(sv['Q'], sv['K'], sv['kv'], delta, sv['Lse'], dys[1], name=nm("mla_attn"))
    dq_lin, dkv, dkr = _mla_bwd_prep(dQ, dK, dV, tabs['cos_m'], tabs['sin_m'], name=nm("mla_prep"))
    g['W_uq'] = _mmT(sv['cqn'], dq_lin, name=nm("mla_uq_dw"))
    dcqn = _mm(dq_lin, w['W_uq'], tb=True, name=nm("mla_uq_dx"))
    g['W_ukv'] = _mmT(sv['ckvn'], dkv, name=nm("mla_ukv_dw"))
    dckvn = _mm(dkv, w['W_ukv'], tb=True, name=nm("mla_ukv_dx"))
    dzm, g['mla_q_norm_g'], g['mla_kv_norm_g'] = _mla_norm_bwd(sv['zm'], w['mla_q_norm_g'], w['mla_kv_norm_g'], dcqn, dckvn, dkr,
                                                               name=nm("mla_norm"))

    zr = sv['zr']

    def gn_bwd(yf, yb, gt, gn, ct):
        _, vjp = jax.vjp(_gn_gate, yf, yb, gt, gn)
        dyf, _, dgt, dgn = vjp(ct)
        return dyf, dgt, dgn

    dyr, dgate, g['ret_gn_g'] = _pw(gn_bwd, [sv['yf'], sv['yb'], zr, w['ret_gn_g'], dys[0]],
                                    [_row(T, 256, hd), _row(T, 256, hd), _row(T, 256, lambda j: 8 + j), _par(256, hd),
                                     _row(T, 256, hd)],
                                    [SDS((S, 1024), bf16), SDS((S, 1024), bf16), SDS((1, 1024), f32)],
                                    [_row(T, 256, hd), _row(T, 256, hd), _par(256, hd)], (RET_HEADS, I), n_acc=1, name=nm("ret_gn"))
    qf, kf, vf, lgf = _ret_dir_bwd(zr, w['lg'], tabs['cos_r'], tabs['sin_r'], dyr, sv['stf'], reverse=False, name=nm("ret_f"))
    qb, kb, vb, lgb = _ret_dir_bwd(zr, w['lg'], tabs['cos_r'], tabs['sin_r'], dyr, sv['stb'], reverse=True, name=nm("ret_b"))
    g['lg'] = jnp.stack([lgf[:, 0, 0], lgb[:, 0, 0]])
    add2 = lambda a, b: a + b
    dq = _pw(add2, [qf, qb], [_row(T, 512)] * 2, [SDS((S, 512), bf16)], [_row(T, 512)], (1, I), name=nm("ret_dq"))[0]
    dk = _pw(add2, [kf, kb], [_row(T, 512)] * 2, [SDS((S, 512), bf16)], [_row(T, 512)], (1, I), name=nm("ret_dk"))[0]
    dv = _pw(add2, [vf, vb], [_row(T, D)] * 2, [SDS((S, D), bf16)], [_row(T, D)], (1, I), name=nm("ret_dv"))[0]
    dzr = jnp.concatenate([dq, dk, dv, dgate], axis=1)

    h = sv['h']
    g['W_ret'] = _mmT(h, dzr, name=nm("in_ret_dw"))
    g['W_mla'] = _mmT(h, dzm, name=nm("in_mla_dw"))
    g['W_s5'] = _mmT(sv['h_seg'], dzs_seg, name=nm("in_s5_dw"))
    g['W_gate'] = _mmT(h, dzg, name=nm("in_gate_dw"))
    dh = _mm(dzr, w['W_ret'], tb=True, name=nm("in_ret_dx"))
    dh = _mm(dzm, w['W_mla'], tb=True, res=dh, name=nm("in_mla_dx"))
    dh = _mm(dzs, w['W_s5'], tb=True, res=dh, name=nm("in_s5_dx"))
    dh = _mm(dzg, w['W_gate'], tb=True, res=dh, name=nm("in_gate_dx"))
    dx, g['norm1_g'] = _rmsnorm_bwd(sv['x'], w['norm1_g'], dh, dx1, name=nm("norm1"))
    return dx, g


def _loss_head(x, tgt, gain, *, name):
    S, W = x.shape
    T = min(S, 512)

    def loss_fn(xv, gv, tv):
        return 0.5 * jnp.sum(jnp.mean(jnp.square(_rms(xv, gv) - tv), axis=-1, keepdims=True), axis=0, keepdims=True)

    def fn(xv, gv, tv):
        lv, vjp = jax.vjp(lambda a, b: loss_fn(a, b, tv), xv, gv)
        dx, dg = vjp(jnp.ones((1, 1), f32))
        return dx, jnp.broadcast_to(lv, (1, 128)), dg

    return _pw(fn, [x, gain, tgt], [_row(T, W), _par(W), _row(T, W)],
               [SDS((S, W), f32), SDS((1, 128), f32), SDS((1, W), f32)], [_row(T, W), _par(128), _par(W)],
               (1, S // T), n_acc=2, name=name)


def _rope_tabs(S):
    def tab(dim):
        inv = 1.0 / (ROPE_THETA ** (jnp.arange(0, dim, 2, dtype=f32) / dim))
        ang = jnp.arange(S, dtype=f32)[:, None] * inv[None, :]
        return jnp.cos(ang), jnp.sin(ang)

    cr, sr = tab(RET_DK)
    cm, sm = tab(MLA_ROPE)
    z = jnp.zeros((S, 64), f32)
    return {'cos_r': jnp.concatenate([cr, cr], axis=1), 'sin_r': jnp.concatenate([-sr, sr], axis=1),
            'cos_m': jnp.concatenate([cm, cm, z], axis=1), 'sin_m': jnp.concatenate([-sm, sm, z], axis=1),
            'seg_perm': _seg_perm(512, False), 'seg_unperm': _seg_perm(512, True)}


def _bd_B(bb):
    b5 = bb.reshape(16, 2, 8, 8, 64)
    return jnp.einsum('cdjgp,gh->djgchp', b5, jnp.eye(8, dtype=bb.dtype)).reshape(2, 8, 128, 512)


def _bd_B_t(dBB):
    return jnp.einsum('djgcgp->cdjgp', dBB.reshape(2, 8, 8, 16, 8, 64)).reshape(16, 8192)


def _bd_C(c):
    c5 = c.reshape(2, 8, 8, 16, 64)
    return jnp.einsum('djgcp,gh->djgphc', c5, jnp.eye(8, dtype=c.dtype)).reshape(2, 8, 512, 128)


def _s5_rows(p, l):
    a_re = p['s5_a_re'][l].reshape(1, 8192)
    a_im = p['s5_a_im'][l].reshape(1, 8192)
    ldt = jnp.broadcast_to(p['s5_log_dt'][l][:, :, None], (2, S5_G, S5_P)).reshape(1, 8192)
    b_re = p['s5_b_re'][l].transpose(3, 0, 1, 2).reshape(16, 8192)
    b_im = p['s5_b_im'][l].transpose(3, 0, 1, 2).reshape(16, 8192)
    return a_re, a_im, ldt, b_re, b_im


def _layer_weights(big, p, l):
    w_in = big['w_in'][l]
    z = lambda n: jnp.zeros((D, n), w_in.dtype)
    w = {
        'W_ret': w_in[:, 0:3072],
        'W_mla': jnp.concatenate([w_in[:, 3072:3456], z(128), w_in[:, 3456:3712], w_in[:, 3712:3776], z(64)], axis=1),
        'W_s5': w_in[:, 3776:4800],
        'W_gate': w_in[:, 4800:7872],
        'W_uq': jnp.pad(big['mla_w_uq'][l].reshape(MLA_Q_LORA, MLA_HEADS, 192), ((0, 0), (0, 0), (0, 64))).reshape(MLA_Q_LORA, 2048),
        'W_ukv': big['mla_w_ukv'][l],
        'W_glu': big['s5_w_glu'][l],
        'W_br': [big['w_branch'][l, i] for i in range(3)],
        'W_out': big['w_out'][l],
        'W_gu': big['ffn_w_gu'][l],
        'W_down': big['ffn_w_down'][l],
    }
    for n in ('norm1_g', 'ret_gn_g', 'mla_q_norm_g', 'mla_kv_norm_g', 's5_d', 'norm2_g'):
        w[n] = p[n][l][None, :]
    w['lg'] = jax.nn.log_sigmoid(p['ret_decay'][l])
    rows = _s5_rows(p, l)
    abr, abi, bbr, bbi, pwr, pwi = _s5_param_fwd(*rows, name=f"L{l}_s5_param")
    flip = lambda t, first: jnp.concatenate([t[::-1, :4096], t[:, 4096:]] if first else [t[:, :4096], t[::-1, 4096:]], axis=1)
    w['s5'] = {'abr': abr, 'abi': abi, 'BBr': _bd_B(bbr).astype(bf16), 'BBi': _bd_B(bbi).astype(bf16),
               'CCr': _bd_C(p['s5_c_re'][l]).astype(bf16), 'CCi': _bd_C(p['s5_c_im'][l]).astype(bf16),
               'pwr_f': flip(pwr, False), 'pwi_f': flip(pwi, False), 'pwr_a': flip(pwr, True), 'pwi_a': flip(pwi, True),
               'rows': rows}
    return w


def _layer_grads(g, w, p, l):
    out = {}
    m = g['W_mla']
    out['w_in'] = jnp.concatenate([g['W_ret'], m[:, 0:384], m[:, 512:768], m[:, 768:832], g['W_s5'], g['W_gate']], axis=1)
    out['mla_w_uq'] = g['W_uq'].reshape(MLA_Q_LORA, MLA_HEADS, 256)[:, :, :192].reshape(MLA_Q_LORA, 1536)
    out['mla_w_ukv'] = g['W_ukv']
    out['s5_w_glu'] = g['W_glu']
    out['w_branch'] = jnp.stack(g['W_br'])
    out['w_out'] = g['W_out']
    out['ffn_w_gu'] = g['W_gu']
    out['ffn_w_down'] = g['W_down']
    for n in ('norm1_g', 'ret_gn_g', 'mla_q_norm_g', 'mla_kv_norm_g', 's5_d', 'norm2_g'):
        out[n] = g[n][0]
    out['ret_decay'] = g['lg'] * jax.nn.sigmoid(-p['ret_decay'][l])
    (fB_r, fB_i, fC_r, fC_i, fa_r, fa_i), (bB_r, bB_i, bC_r, bC_i, ba_r, ba_i) = g['s5']
    cat = lambda a, b: jnp.concatenate([a, b], axis=0)
    d_bbr = _bd_B_t(cat(fB_r, bB_r))
    d_bbi = _bd_B_t(cat(fB_i, bB_i))
    to_c = lambda t: _bd_B_t(t).reshape(16, 2, S5_G, S5_P).transpose(1, 2, 0, 3)
    out['s5_c_re'] = to_c(cat(fC_r, bC_r))
    out['s5_c_im'] = to_c(cat(fC_i, bC_i))
    d_abr = jnp.concatenate([fa_r, ba_r], axis=1)
    d_abi = jnp.concatenate([fa_i, ba_i], axis=1)
    da_re, da_im, dldt, db_re, db_im = _s5_param_bwd(*w['s5']['rows'], d_abr, d_abi, d_bbr, d_bbi, name=f"L{l}_b_s5_param")
    out['s5_a_re'] = da_re.reshape(2, S5_G, S5_P)
    out['s5_a_im'] = da_im.reshape(2, S5_G, S5_P)
    out['s5_log_dt'] = dldt.reshape(2, S5_G, S5_P).sum(axis=-1)
    out['s5_b_re'] = db_re.reshape(16, 2, S5_G, S5_P).transpose(1, 2, 3, 0)
    out['s5_b_im'] = db_im.reshape(16, 2, S5_G, S5_P).transpose(1, 2, 3, 0)
    return out


def _local_step(x, tgt, big, p):
    S = x.shape[0]
    assert S % 512 == 0
    tabs = _rope_tabs(S)
    ws, svs = [], []
    h = x
    for l in range(DEPTH):
        w = _layer_weights(big, p, l)
        h, sv = _layer_fwd(h, w, tabs, l)
        ws.append(w)
        svs.append(sv)
    dx, lossv, dfinal = _loss_head(h, tgt, p['final_g'][None, :], name="loss_head")
    per_layer = [None] * DEPTH
    for l in reversed(range(DEPTH)):
        dx, g = _layer_bwd(dx, ws[l], tabs, svs[l], l)
        per_layer[l] = _layer_grads(g, ws[l], p, l)
    grads = {n: jnp.stack([per_layer[l][n] for l in range(DEPTH)]) for n in per_layer[0]}
    grads['final_g'] = dfinal[0]
    return lossv[0, 0], dx, grads


_ANY = pl.BlockSpec(memory_space=pl.ANY)


def _place():
    x, y, c = lax.axis_index("x"), lax.axis_index("y"), lax.axis_index("c")
    return x, y, c, [(1 - x, y), (x, 1 - y), (1 - x, 1 - y)]


def _allgather4(arrs, *, name):
    n = len(arrs)

    def body(*refs):
        ins, outs = refs[:n], refs[n:2 * n]
        send, recv, loc = refs[2 * n:]
        x, y, c, chips = _place()
        me = 2 * x + y

        def remote(a, k, slot):
            px, py = chips[k]
            return pltpu.make_async_remote_copy(src_ref=ins[a], dst_ref=outs[a].at[slot], send_sem=send.at[a, k],
                                                recv_sem=recv.at[a, k], device_id=(px, py, c), device_id_type=MESH)

        mine = [pltpu.make_async_copy(ins[a], outs[a].at[me], loc.at[a]) for a in range(n)]
        for cp in mine:
            cp.start()
        sends = [remote(a, k, me) for a in range(n) for k in range(3)]
        for cp in sends:
            cp.start()
        for a in range(n):
            for k, (px, py) in enumerate(chips):
                remote(a, k, 2 * px + py).wait_recv()
        for cp in sends:
            cp.wait_send()
        for cp in mine:
            cp.wait()

    return pl.pallas_call(
        body, in_specs=[_ANY] * n, out_specs=[_ANY] * n, out_shape=[SDS((4,) + a.shape, a.dtype) for a in arrs],
        scratch_shapes=[pltpu.SemaphoreType.DMA((n, 3)), pltpu.SemaphoreType.DMA((n, 3)), pltpu.SemaphoreType.DMA((n,))],
        name=name)(*arrs)


def _rs_exchange(parts, *, name):
    n = len(parts)

    def body(*refs):
        ins, gots = refs[:n], refs[n:2 * n]
        send, recv = refs[2 * n:]
        x, y, c, chips = _place()

        def remote(a, k):
            px, py = chips[k]
            return pltpu.make_async_remote_copy(src_ref=ins[a].at[2 * px + py], dst_ref=gots[a].at[k], send_sem=send.at[a, k],
                                                recv_sem=recv.at[a, k], device_id=(px, py, c), device_id_type=MESH)

        sends = [remote(a, k) for a in range(n) for k in range(3)]
        for cp in sends:
            cp.start()
        for cp in sends:
            cp.wait_recv()
        for cp in sends:
            cp.wait_send()

    return pl.pallas_call(
        body, in_specs=[_ANY] * n, out_specs=[_ANY] * n, out_shape=[SDS((3,) + a.shape[1:], a.dtype) for a in parts],
        scratch_shapes=[pltpu.SemaphoreType.DMA((n, 3)), pltpu.SemaphoreType.DMA((n, 3))], name=name)(*parts)


def _gather_split(arrs, *, name):
    n = len(arrs)

    def body(*refs):
        ins, outs = refs[:n], refs[n:2 * n]
        s_ici, r_ici, s_sib, r_sib = refs[2 * n:]
        x, y, c, chips = _place()
        me = 2 * x + y
        ids = [2 * px + py for px, py in chips] + [me]

        def over_ici(a, k, slot):
            px, py = chips[k]
            return pltpu.make_async_remote_copy(src_ref=ins[a].at[c], dst_ref=outs[a].at[slot, c], send_sem=s_ici.at[a, k],
                                                recv_sem=r_ici.at[a, k], device_id=(px, py, c), device_id_type=MESH)

        def to_sibling(a, k, half, src=None):
            blk = outs[a].at[ids[k], half]
            return pltpu.make_async_remote_copy(src_ref=blk if src is None else src, dst_ref=blk, send_sem=s_sib.at[a, k],
                                                recv_sem=r_sib.at[a, k], device_id=(x, y, 1 - c), device_id_type=MESH)

        sends = [over_ici(a, k, me) for a in range(n) for k in range(3)]
        sends += [to_sibling(a, 3, c, src=ins[a].at[c]) for a in range(n)]
        for cp in sends:
            cp.start()
        for a in range(n):
            for k in range(3):
                over_ici(a, k, ids[k]).wait_recv()
                fwd = to_sibling(a, k, c)
                fwd.start()
                sends.append(fwd)
        for a in range(n):
            for k in range(4):
                to_sibling(a, k, 1 - c).wait_recv()
        for cp in sends:
            cp.wait_send()

    dma = pltpu.SemaphoreType.DMA
    return pl.pallas_call(
        body, in_specs=[_ANY] * n, out_specs=[_ANY] * n, out_shape=[SDS((4,) + a.shape, a.dtype) for a in arrs],
        scratch_shapes=[dma((n, 3)), dma((n, 3)), dma((n, 4)), dma((n, 4))], name=name)(*arrs)


def _swap_halves(parts, *, name):
    n = len(parts)

    def body(*refs):
        ins, gots = refs[:n], refs[n:2 * n]
        send, recv = refs[2 * n:]
        x, y, c, _ = _place()
        cps = [pltpu.make_async_remote_copy(src_ref=ins[a].at[q, 1 - c], dst_ref=gots[a].at[q], send_sem=send.at[a, q],
                                            recv_sem=recv.at[a, q], device_id=(x, y, 1 - c), device_id_type=MESH)
               for a in range(n) for q in range(4)]
        for cp in cps:
            cp.start()
        for cp in cps:
            cp.wait_recv()
        for cp in cps:
            cp.wait_send()

    dma = pltpu.SemaphoreType.DMA
    return pl.pallas_call(
        body, in_specs=[_ANY] * n, out_specs=[_ANY] * n, out_shape=[SDS((4,) + a.shape[2:], a.dtype) for a in parts],
        scratch_shapes=[dma((n, 4)), dma((n, 4))], name=name)(*parts)


def _sibling_copy(arrs, *, name):
    n = len(arrs)

    def body(*refs):
        ins, outs = refs[:n], refs[n:2 * n]
        send, recv = refs[2 * n:]
        x, y, c, _ = _place()
        cps = [pltpu.make_async_remote_copy(src_ref=ins[a], dst_ref=outs[a], send_sem=send.at[a], recv_sem=recv.at[a],
                                            device_id=(x, y, 1 - c), device_id_type=MESH) for a in range(n)]
        for cp in cps:
            cp.start()
        for cp in cps:
            cp.wait_recv()
        for cp in cps:
            cp.wait_send()

    dma = pltpu.SemaphoreType.DMA
    return pl.pallas_call(
        body, in_specs=[_ANY] * n, out_specs=[_ANY] * n, out_shape=[SDS(a.shape, a.dtype) for a in arrs],
        scratch_shapes=[dma((n,)), dma((n,))], name=name)(*arrs)


def _row_tile(R):
    return R if R <= 256 else next(t for t in (256, 128, 64, 32, 16) if R % t == 0)


def _add2(a, b, *, name):
    R, W = a.shape
    tr = _row_tile(R)
    return _pw(lambda p, q: p.astype(f32) + q.astype(f32), [a, b], [_row(tr, W)] * 2, [SDS((R, W), a.dtype)], [_row(tr, W)],
               (1, R // tr), name=name)[0]


def _sum4(own, got, *, name):
    R, W = own.shape
    tr = _row_tile(R)
    g3 = lambda k: pl.BlockSpec((None, tr, W), lambda j, i: (k, i, 0))
    up = lambda t: t.astype(f32)
    return _pw(lambda a, b, c, d: ((up(a) + up(b)) + up(c)) + up(d), [own, got, got, got], [_row(tr, W), g3(0), g3(1), g3(2)],
               [SDS((R, W), f32)], [_row(tr, W)], (1, R // tr), name=name)[0]


def _adamw(g, w, m, v, *, name):
    R, W = w.shape
    tr = _row_tile(R)

    def fn(gv, wv, mv, vv):
        m2 = ADAM_B1 * mv + (1.0 - ADAM_B1) * gv
        v2 = ADAM_B2 * vv + (1.0 - ADAM_B2) * jnp.square(gv)
        m_hat = m2 / (1.0 - ADAM_B1 ** ADAM_STEP)
        v_hat = v2 / (1.0 - ADAM_B2 ** ADAM_STEP)
        return -ADAM_LR * (m_hat / (jnp.sqrt(v_hat) + ADAM_EPS) + ADAM_WD * wv), m2, v2

    return _pw(fn, [g, w, m, v], [_row(tr, W)] * 4, [SDS((R, W), f32)] * 3, [_row(tr, W)] * 3, (1, R // tr), name=name)


def _to_parts(g, axis):
    shp = g.shape
    g = g.reshape(shp[:axis] + (4, shp[axis] // 4) + shp[axis + 1:])
    return jnp.moveaxis(g, axis, 0)


def _from_parts(pt, axis):
    g = jnp.moveaxis(pt, 0, axis)
    shp = g.shape
    return g.reshape(shp[:axis] + (4 * shp[axis + 1],) + shp[axis + 2:])


def kernel(x, norm1_g, w_in, ret_decay, ret_gn_g, mla_q_norm_g, mla_w_uq, mla_kv_norm_g, mla_w_ukv, s5_a_re, s5_a_im, s5_log_dt, s5_b_re, s5_b_im, s5_c_re, s5_c_im, s5_d, s5_w_glu, w_branch, w_out, norm2_g, ffn_w_gu, ffn_w_down, final_g, loss_target, m_norm1_g, m_w_in, m_ret_decay, m_ret_gn_g, m_mla_q_norm_g, m_mla_w_uq, m_mla_kv_norm_g, m_mla_w_ukv, m_s5_a_re, m_s5_a_im, m_s5_log_dt, m_s5_b_re, m_s5_b_im, m_s5_c_re, m_s5_c_im, m_s5_d, m_s5_w_glu, m_w_branch, m_w_out, m_norm2_g, m_ffn_w_gu, m_ffn_w_down, m_final_g, v_norm1_g, v_w_in, v_ret_decay, v_ret_gn_g, v_mla_q_norm_g, v_mla_w_uq, v_mla_kv_norm_g, v_mla_w_ukv, v_s5_a_re, v_s5_a_im, v_s5_log_dt, v_s5_b_re, v_s5_b_im, v_s5_c_re, v_s5_c_im, v_s5_d, v_s5_w_glu, v_w_branch, v_w_out, v_norm2_g, v_ffn_w_gu, v_ffn_w_down, v_final_g):
    wv = dict(zip(W_NAMES, (norm1_g, w_in, ret_decay, ret_gn_g, mla_q_norm_g, mla_w_uq, mla_kv_norm_g, mla_w_ukv, s5_a_re, s5_a_im,
                            s5_log_dt, s5_b_re, s5_b_im, s5_c_re, s5_c_im, s5_d, s5_w_glu, w_branch, w_out, norm2_g, ffn_w_gu,
                            ffn_w_down, final_g)))
    mv = dict(zip(W_NAMES, (m_norm1_g, m_w_in, m_ret_decay, m_ret_gn_g, m_mla_q_norm_g, m_mla_w_uq, m_mla_kv_norm_g, m_mla_w_ukv,
                            m_s5_a_re, m_s5_a_im, m_s5_log_dt, m_s5_b_re, m_s5_b_im, m_s5_c_re, m_s5_c_im, m_s5_d, m_s5_w_glu,
                            m_w_branch, m_w_out, m_norm2_g, m_ffn_w_gu, m_ffn_w_down, m_final_g)))
    vv = dict(zip(W_NAMES, (v_norm1_g, v_w_in, v_ret_decay, v_ret_gn_g, v_mla_q_norm_g, v_mla_w_uq, v_mla_kv_norm_g, v_mla_w_ukv,
                            v_s5_a_re, v_s5_a_im, v_s5_log_dt, v_s5_b_re, v_s5_b_im, v_s5_c_re, v_s5_c_im, v_s5_d, v_s5_w_glu,
                            v_w_branch, v_w_out, v_norm2_g, v_ffn_w_gu, v_ffn_w_down, v_final_g)))
    big_names = list(BIG)

    my_c = lax.axis_index("c")
    my_chip = 2 * lax.axis_index("x") + lax.axis_index("y")
    shards = [wv[n].astype(bf16) for n in big_names]
    gathered = _gather_split(shards, name="gather_weights")
    gathered = [lax.dynamic_update_slice(gt, lax.dynamic_index_in_dim(sh, my_c, 0)[None],
                                         (my_chip, my_c) + (0,) * (sh.ndim - 1)) for gt, sh in zip(gathered, shards)]
    big = {n: _from_parts(gt, BIG[n]) for n, gt in zip(big_names, gathered)}
    small = {n: wv[n] for n in SMALL}

    loss_local, dx, grads = _local_step(x[0], loss_target[0], big, small)

    n_rows = {n: -(-math.prod(wv[n].shape) // 1024) * 8 for n in SMALL}
    used = sum(n_rows.values())
    rows_q = -(-(used + 8) // (4 * 128)) * 128

    def as_rows(d, tail=None):
        blocks = [jnp.pad(d[n].reshape(-1), (0, n_rows[n] * 128 - math.prod(wv[n].shape))).reshape(n_rows[n], 128) for n in SMALL]
        blocks.append(jnp.zeros((8, 128), f32) if tail is None else tail)
        blocks.append(jnp.zeros((4 * rows_q - used - 8, 128), f32))
        return jnp.concatenate(blocks, axis=0)

    loss_rows = jnp.full((8, 128), loss_local, f32)
    parts = [_to_parts(grads[n].astype(bf16), BIG[n]) for n in big_names]
    parts.append(as_rows(grads, loss_rows).reshape(4, 2, rows_q // 2, 128))
    n_arr = len(parts)
    two_d = lambda a: a.reshape(-1, a.shape[-1])
    theirs = _swap_halves(parts, name="grad_swap_halves")
    mine = [lax.dynamic_index_in_dim(p, my_c, 1, keepdims=False) for p in parts]
    chip_sums = [_add2(two_d(mine[a]), two_d(theirs[a]), name=f"grad_add2_{a}").reshape(theirs[a].shape) for a in range(n_arr)]
    got = _rs_exchange(chip_sums, name="grad_exchange")
    own = [lax.dynamic_index_in_dim(s, my_chip, 0, keepdims=False) for s in chip_sums]
    sums = [_sum4(two_d(own[a]), got[a].reshape(3, -1, got[a].shape[-1]), name=f"grad_sum4_{a}") for a in range(n_arr)]
    other = _sibling_copy(sums, name="grad_sibling")
    full = [jnp.stack([jnp.where(my_c == 0, sums[a], other[a]), jnp.where(my_c == 0, other[a], sums[a])]) for a in range(n_arr)]

    out_g, out_d, out_m, out_v = {}, {}, {}, {}
    for a, n in enumerate(big_names):
        shp = wv[n].shape
        res = _adamw(two_d(full[a]), two_d(wv[n]), two_d(mv[n]), two_d(vv[n]), name=f"adamw_{n}")
        out_g[n] = full[a].reshape(shp)
        out_d[n], out_m[n], out_v[n] = [r.reshape(shp) for r in res]
    g_small = _allgather4([full[-1].reshape(rows_q, 128)], name="gather_small_grads")[0].reshape(4 * rows_q, 128)
    loss = g_small[used, 0]
    res = (g_small,) + tuple(_adamw(g_small, as_rows(wv), as_rows(mv), as_rows(vv), name="adamw_small"))
    off = 0
    for n in SMALL:
        k = math.prod(wv[n].shape)
        for dst, r in zip((out_g, out_d, out_m, out_v), res):
            dst[n] = r[off:off + n_rows[n]].reshape(-1)[:k].reshape(wv[n].shape)
        off += n_rows[n]
    return (loss, dx[None], *[out_g[n] for n in W_NAMES], *[out_d[n] for n in W_NAMES], *[out_m[n] for n in W_NAMES],
            *[out_v[n] for n in W_NAMES])
```

```python
import functools
import math

import jax
import jax.numpy as jnp
from jax import lax
from jax.experimental import pallas as pl
from jax.experimental.pallas import tpu as pltpu

f32 = jnp.float32
bf16 = jnp.bfloat16
SDS = jax.ShapeDtypeStruct
MESH = pl.DeviceIdType.MESH

D = 1024
DEPTH = 2
RMS_EPS = 1e-6
GN_EPS = 1e-5
ROPE_THETA = 10000.0
RET_HEADS = 4
RET_DK = 128
RET_DV = 256
RET_CHUNK = 128
MLA_HEADS = 8
MLA_Q_LORA = 384
MLA_KV_LORA = 256
MLA_NOPE = 128
MLA_ROPE = 64
MLA_V = 128
MLA_QW = 256
S5_G = 64
S5_P = 64
S5_C = 16
S5_NJ = 8
S5_SEG = 8
FFN_H = 2816
ADAM_LR = 0.001
ADAM_B1 = 0.9
ADAM_B2 = 0.999
ADAM_EPS = 1e-08
ADAM_WD = 0.01
ADAM_STEP = 10
VMEM_BIG = 56 * 1024 * 1024

W_NAMES = ['norm1_g', 'w_in', 'ret_decay', 'ret_gn_g', 'mla_q_norm_g', 'mla_w_uq', 'mla_kv_norm_g', 'mla_w_ukv',
           's5_a_re', 's5_a_im', 's5_log_dt', 's5_b_re', 's5_b_im', 's5_c_re', 's5_c_im', 's5_d', 's5_w_glu',
           'w_branch', 'w_out', 'norm2_g', 'ffn_w_gu', 'ffn_w_down', 'final_g']
BIG = {'w_in': 2, 'mla_w_uq': 2, 'mla_w_ukv': 2, 's5_w_glu': 2, 'w_branch': 2, 'w_out': 1, 'ffn_w_gu': 2, 'ffn_w_down': 1}
SMALL = [n for n in W_NAMES if n not in BIG]


TILE_BYTES = 6 * 1024 * 1024


def _pick(n, cands=(512, 384, 256, 128), cap=None):
    if n <= 1024 and (cap is None or n <= cap):
        return n
    for c in cands:
        if n % c == 0 and (cap is None or c <= cap):
            return c
    raise ValueError(n)


WIDE = (1408, 1024, 768, 512, 384, 256, 128)


def _params(sem, vmem=None):
    return pltpu.CompilerParams(dimension_semantics=sem, vmem_limit_bytes=vmem)


def _mm(a, b, *, tb=False, res=None, out_dtype=f32, name):
    M, K = a.shape
    N = b.shape[0] if tb else b.shape[1]
    tk = K if K <= 3072 else _pick(K, (1408, 1024, 512))
    nk = K // tk
    tn = _pick(N, WIDE, cap=TILE_BYTES // (tk * b.dtype.itemsize))
    tm = _pick(M)
    if M % 1024 == 0 and 1024 * tk * a.dtype.itemsize <= 4 * 1024 * 1024 and 1024 * tn * 4 <= TILE_BYTES:
        tm = 1024
    assert M % tm == 0 and N % tn == 0 and K % tk == 0

    def body(*refs):
        if res is None:
            a_ref, b_ref, o_ref, acc = refs
        else:
            a_ref, b_ref, r_ref, o_ref, acc = refs
        k = pl.program_id(2)
        dn = (((1,), (1 if tb else 0,)), ((), ()))
        part = lax.dot_general(a_ref[...].astype(bf16), b_ref[...].astype(bf16), dn, preferred_element_type=f32)

        @pl.when(k == 0)
        def _():
            acc[...] = part

        @pl.when(k > 0)
        def _():
            acc[...] += part

        @pl.when(k == nk - 1)
        def _():
            v = acc[...]
            if res is not None:
                v = v + r_ref[...]
            o_ref[...] = v.astype(out_dtype)

    in_specs = [pl.BlockSpec((tm, tk), lambda i, j, k: (i, k)),
                pl.BlockSpec((tn, tk), lambda i, j, k: (j, k)) if tb else pl.BlockSpec((tk, tn), lambda i, j, k: (k, j))]
    args = [a, b]
    if res is not None:
        in_specs.append(pl.BlockSpec((tm, tn), lambda i, j, k: (i, j)))
        args.append(res)
    return pl.pallas_call(
        body, grid=(M // tm, N // tn, nk), in_specs=in_specs,
        out_specs=pl.BlockSpec((tm, tn), lambda i, j, k: (i, j)),
        out_shape=SDS((M, N), out_dtype), scratch_shapes=[pltpu.VMEM((tm, tn), f32)],
        compiler_params=_params(("parallel", "parallel", "arbitrary"), VMEM_BIG), name=name)(*args)


def _mmT(a, b, *, name):
    S, M = a.shape
    N = b.shape[1]
    tn = _pick(N, WIDE)
    tm = _pick(M, WIDE, cap=TILE_BYTES // (tn * 4))
    tk = min(S, 1024)
    nk = S // tk

    def body(a_ref, b_ref, o_ref):
        k = pl.program_id(2)
        part = lax.dot_general(a_ref[...].astype(bf16), b_ref[...].astype(bf16), (((0,), (0,)), ((), ())),
                               preferred_element_type=f32)

        @pl.when(k == 0)
        def _():
            o_ref[...] = part

        @pl.when(k > 0)
        def _():
            o_ref[...] += part

    return pl.pallas_call(
        body, grid=(M // tm, N // tn, nk),
        in_specs=[pl.BlockSpec((tk, tm), lambda i, j, k: (k, i)), pl.BlockSpec((tk, tn), lambda i, j, k: (k, j))],
        out_specs=pl.BlockSpec((tm, tn), lambda i, j, k: (i, j)),
        out_shape=SDS((M, N), f32),
        compiler_params=_params(("parallel", "parallel", "arbitrary"), VMEM_BIG), name=name)(a, b)


def _pw(fn, ins, in_specs, outs, out_specs, grid, *, n_acc=0, name):
    n_in = len(ins)
    n_out = len(outs)

    def body(*refs):
        vals = fn(*[r[...] for r in refs[:n_in]])
        if not isinstance(vals, (tuple, list)):
            vals = (vals,)
        orefs = refs[n_in:]
        for r, v in zip(orefs[:n_out - n_acc], vals[:n_out - n_acc]):
            r[...] = v.astype(r.dtype)
        if n_acc:
            i = pl.program_id(1)

            @pl.when(i == 0)
            def _():
                for r, v in zip(orefs[n_out - n_acc:], vals[n_out - n_acc:]):
                    r[...] = v.astype(r.dtype)

            @pl.when(i > 0)
            def _():
                for r, v in zip(orefs[n_out - n_acc:], vals[n_out - n_acc:]):
                    r[...] += v.astype(r.dtype)

    res = pl.pallas_call(
        body, grid=grid, in_specs=in_specs, out_specs=out_specs, out_shape=outs,
        compiler_params=_params(("parallel", "arbitrary"), VMEM_BIG), name=name)(*ins)
    return res


def _row(T, w, col=None):
    if col is None:
        return pl.BlockSpec((T, w), lambda j, i: (i, 0))
    return pl.BlockSpec((T, w), lambda j, i: (i, col(j)))


def _par(w, col=None):
    if col is None:
        return pl.BlockSpec((1, w), lambda j, i: (0, 0))
    return pl.BlockSpec((1, w), lambda j, i: (0, col(j)))


def _rms(x, g):
    return x * lax.rsqrt(jnp.mean(x * x, axis=-1, keepdims=True) + RMS_EPS) * g


def _rope(x, cos, sinm, half):
    if half == 64:
        partner = pltpu.roll(x, 64, axis=1)
    else:
        lane = lax.broadcasted_iota(jnp.int32, x.shape, 1)
        partner = jnp.where((lane % (2 * half)) < half, pltpu.roll(x, 128 - half, axis=1), pltpu.roll(x, half, axis=1))
    return x * cos + partner * sinm


def _rope_t(x, cos, sinm, half):
    return _rope(x, cos, -sinm, half)


def _rmsnorm_fwd(x, g, *, name):
    S, W = x.shape
    T = min(S, 512)
    return _pw(lambda xv, gv: _rms(xv, gv), [x, g], [_row(T, W), _par(W)], [SDS((S, W), bf16)], [_row(T, W)],
               (1, S // T), name=name)[0]


def _rmsnorm_bwd(x, g, dh, dres, *, name):
    S, W = x.shape
    T = min(S, 512)

    def fn(xv, gv, dhv, drv):
        _, vjp = jax.vjp(_rms, xv, gv)
        dx, dg = vjp(dhv)
        return dx + drv, dg

    return _pw(fn, [x, g, dh, dres], [_row(T, W), _par(W), _row(T, W), _row(T, W)],
               [SDS((S, W), f32), SDS((1, W), f32)], [_row(T, W), _par(W)], (1, S // T), n_acc=1, name=name)


def _ret_tables(lg, reverse):
    C = RET_CHUNK
    ii = lax.broadcasted_iota(jnp.int32, (C, C), 0).astype(f32)
    jj = lax.broadcasted_iota(jnp.int32, (C, C), 1).astype(f32)
    if not reverse:
        E = ii - jj
        mask = E >= 0
        eq = ii + 1.0
        ek = (C - 1.0) - ii
    else:
        E = jj - ii
        mask = E > 0
        eq = C - ii
        ek = ii
    Dm = jnp.where(mask, jnp.exp(jnp.where(mask, E, 0.0) * lg), 0.0)
    Em = jnp.where(mask, E, 0.0)
    qw = jnp.exp(eq * lg)
    kw = jnp.exp(ek * lg)
    qw2 = jnp.concatenate([qw, qw], axis=1)
    return Dm, Em, eq, ek, qw, kw, qw2, jnp.exp(C * lg)


def _dot(a, b, dims):
    return lax.dot_general(a.astype(bf16), b.astype(bf16), (dims, ((), ())), preferred_element_type=f32)


NN = ((1,), (0,))
NT = ((1,), (1,))
TN = ((0,), (0,))


def _ret_dir_fwd(zr, lg, cos, sinm, *, reverse, name):
    S = zr.shape[0]
    C = RET_CHUNK
    TB = min(S, 512)
    nc = TB // C
    NB = S // TB
    d = 1 if reverse else 0
    scale = RET_DK ** -0.5

    def tb(b):
        return (NB - 1 - b) if reverse else b

    def body(lg_ref, q_ref, k_ref, v_ref, cos_ref, sin_ref, y_ref, st_ref, state):
        h = pl.program_id(0)
        b = pl.program_id(1)

        @pl.when(b == 0)
        def _():
            state[...] = jnp.zeros_like(state)

        Dm, _, _, _, _, kw, qw2, gC = _ret_tables(lg_ref[d, h], reverse)
        order = range(nc - 1, -1, -1) if reverse else range(nc)
        for c in order:
            rows = pl.ds(c * C, C)
            q = _rope(q_ref[rows, :], cos_ref[rows, :], sin_ref[rows, :], 64) * scale
            k = _rope(k_ref[rows, :], cos_ref[rows, :], sin_ref[rows, :], 64)
            v = v_ref[rows, :]
            st = state[...]
            st_ref[0, c] = st
            s = _dot(q, k, NT) * Dm
            o = _dot(s, v, NN) + _dot(q, st, NN) * qw2
            y_ref[rows, :] = o
            state[...] = gC * st + _dot(k * kw, v, TN)

    return pl.pallas_call(
        body, grid=(RET_HEADS, NB),
        in_specs=[pl.BlockSpec(memory_space=pltpu.SMEM),
                  pl.BlockSpec((TB, 128), lambda h, b: (tb(b), h)),
                  pl.BlockSpec((TB, 128), lambda h, b: (tb(b), 4 + h)),
                  pl.BlockSpec((TB, 256), lambda h, b: (tb(b), 4 + h)),
                  pl.BlockSpec((TB, 128), lambda h, b: (tb(b), 0)),
                  pl.BlockSpec((TB, 128), lambda h, b: (tb(b), 0))],
        out_specs=[pl.BlockSpec((TB, 256), lambda h, b: (tb(b), h)),
                   pl.BlockSpec((1, nc, 128, 256), lambda h, b: (h, tb(b), 0, 0))],
        out_shape=[SDS((S, 1024), f32), SDS((RET_HEADS, S // C, 128, 256), f32)],
        scratch_shapes=[pltpu.VMEM((128, 256), f32)],
        compiler_params=_params(("parallel", "arbitrary")), name=name)(lg, zr, zr, zr, cos, sinm)


def _ret_dir_bwd(zr, lg, cos, sinm, dy, states, *, reverse, name):
    S = zr.shape[0]
    C = RET_CHUNK
    TB = min(S, 512)
    nc = TB // C
    NB = S // TB
    d = 1 if reverse else 0
    scale = RET_DK ** -0.5

    def tb(b):
        return b if reverse else (NB - 1 - b)

    def body(lg_ref, q_ref, k_ref, v_ref, cos_ref, sin_ref, dy_ref, st_ref, dq_ref, dk_ref, dv_ref, dlg_ref, dstate):
        h = pl.program_id(0)
        b = pl.program_id(1)

        @pl.when(b == 0)
        def _():
            dstate[...] = jnp.zeros_like(dstate)
            dlg_ref[...] = jnp.zeros_like(dlg_ref)

        Dm, Em, eq, ek, qw, kw, qw2, gC = _ret_tables(lg_ref[d, h], reverse)
        order = range(nc) if reverse else range(nc - 1, -1, -1)
        dlg = jnp.zeros((), f32)
        for c in order:
            rows = pl.ds(c * C, C)
            cs, sn = cos_ref[rows, :], sin_ref[rows, :]
            q = _rope(q_ref[rows, :], cs, sn, 64) * scale
            k = _rope(k_ref[rows, :], cs, sn, 64)
            v = v_ref[rows, :]
            do = dy_ref[rows, :]
            st = st_ref[0, c]
            ds = dstate[...]
            p = _dot(q, k, NT)
            a = p * Dm
            dp = _dot(do, v, NT) * Dm
            dq_cross = _dot(do, st, NT) * qw
            dk_cross = _dot(v, ds, NT) * kw
            dq = _dot(dp, k, NN) + dq_cross
            dk = _dot(dp, q, TN) + dk_cross
            dv = _dot(a, do, TN) + _dot(k * kw, ds, NN)
            dlg = dlg + jnp.sum(dp * p * Em) + jnp.sum(dq_cross * q * eq) + jnp.sum(dk_cross * k * ek) \
                + C * gC * jnp.sum(ds * st)
            dstate[...] = gC * ds + _dot(q * qw, do, TN)
            dq_ref[rows, :] = _rope_t(dq, cs, sn, 64) * scale
            dk_ref[rows, :] = _rope_t(dk, cs, sn, 64)
            dv_ref[rows, :] = dv
        dlg_ref[...] += jnp.full(dlg_ref.shape, dlg, f32)

    return pl.pallas_call(
        body, grid=(RET_HEADS, NB),
        in_specs=[pl.BlockSpec(memory_space=pltpu.SMEM),
                  pl.BlockSpec((TB, 128), lambda h, b: (tb(b), h)),
                  pl.BlockSpec((TB, 128), lambda h, b: (tb(b), 4 + h)),
                  pl.BlockSpec((TB, 256), lambda h, b: (tb(b), 4 + h)),
                  pl.BlockSpec((TB, 128), lambda h, b: (tb(b), 0)),
                  pl.BlockSpec((TB, 128), lambda h, b: (tb(b), 0)),
                  pl.BlockSpec((TB, 256), lambda h, b: (tb(b), h)),
                  pl.BlockSpec((1, nc, 128, 256), lambda h, b: (h, tb(b), 0, 0))],
        out_specs=[pl.BlockSpec((TB, 128), lambda h, b: (tb(b), h)),
                   pl.BlockSpec((TB, 128), lambda h, b: (tb(b), h)),
                   pl.BlockSpec((TB, 256), lambda h, b: (tb(b), h)),
                   pl.BlockSpec((1, 1, 128), lambda h, b: (h, 0, 0))],
        out_shape=[SDS((S, 512), f32), SDS((S, 512), f32), SDS((S, 1024), f32), SDS((RET_HEADS, 1, 128), f32)],
        scratch_shapes=[pltpu.VMEM((128, 256), f32)],
        compiler_params=_params(("parallel", "arbitrary")), name=name)(lg, zr, zr, zr, cos, sinm, dy, states)


def _gn_gate(yf, yb, g, gn):
    y = yf + yb
    mu = jnp.mean(y, axis=-1, keepdims=True)
    var = jnp.mean(jnp.square(y - mu), axis=-1, keepdims=True)
    yn = (y - mu) * lax.rsqrt(var + GN_EPS)
    return jax.nn.silu(g) * (yn * gn)


def _flash_fwd(Q, K, kv, *, name):
    S = Q.shape[0]
    hq = min(S, 512)
    nh = 2 if S % 1024 == 0 else 1
    tq = nh * hq
    tk = min(S, 512)
    nk = S // tk

    def body(q_ref, k_ref, v_ref, o_ref, l_ref, m_s, l_s, acc):
        kk = pl.program_id(2)

        @pl.when(kk == 0)
        def _():
            m_s[...] = jnp.full_like(m_s, -jnp.inf)
            l_s[...] = jnp.zeros_like(l_s)
            acc[...] = jnp.zeros_like(acc)

        k = k_ref[...]
        v = v_ref[...]
        sts = [lax.dot_general(k, q_ref[hf * hq:(hf + 1) * hq, :], (NT, ((), ())), preferred_element_type=f32)
               for hf in range(nh)]
        for hf in range(nh):
            st = sts[hf]
            m_prev = m_s[hf]
            m_new = jnp.maximum(m_prev, jnp.max(st, axis=0, keepdims=True))
            pt = jnp.exp2(st - m_new)
            alpha = jnp.exp2(m_prev - m_new)
            l_s[hf] = alpha * l_s[hf] + jnp.sum(pt, axis=0, keepdims=True)
            acc[hf] = alpha * acc[hf] + lax.dot_general(v, pt.astype(bf16), (TN, ((), ())), preferred_element_type=f32)
            m_s[hf] = m_new

        @pl.when(kk == nk - 1)
        def _():
            for hf in range(nh):
                o_ref[hf * hq:(hf + 1) * hq, :] = jnp.transpose(acc[hf] / l_s[hf]).astype(bf16)
                l_ref[0, :, hf * hq:(hf + 1) * hq] = m_s[hf] + jnp.log2(l_s[hf])

    return pl.pallas_call(
        body, grid=(MLA_HEADS, S // tq, nk),
        in_specs=[pl.BlockSpec((tq, 256), lambda h, i, k: (i, h)),
                  pl.BlockSpec((tk, 256), lambda h, i, k: (k, h)),
                  pl.BlockSpec((tk, 128), lambda h, i, k: (k, 2 * h + 1))],
        out_specs=[pl.BlockSpec((tq, 128), lambda h, i, k: (i, h)), pl.BlockSpec((1, 1, tq), lambda h, i, k: (h, 0, i))],
        out_shape=[SDS((S, 1024), bf16), SDS((MLA_HEADS, 1, S), f32)],
        scratch_shapes=[pltpu.VMEM((nh, 1, hq), f32), pltpu.VMEM((nh, 1, hq), f32), pltpu.VMEM((nh, 128, hq), f32)],
        compiler_params=_params(("parallel", "parallel", "arbitrary")), name=name)(Q, K, kv)


def _attn_delta(dO, O, *, name):
    S = dO.shape[0]
    T = min(S, 512)

    def body(do_ref, o_ref, d_ref):
        prod = do_ref[...] * o_ref[...].astype(f32)
        d_ref[0] = lax.dot_general(jnp.ones((8, 128), f32), prod, (NT, ((), ())), preferred_element_type=f32,
                                   precision=lax.Precision.HIGHEST)[0:1, :]

    return pl.pallas_call(
        body, grid=(MLA_HEADS, S // T),
        in_specs=[pl.BlockSpec((T, 128), lambda h, i: (i, h)), pl.BlockSpec((T, 128), lambda h, i: (i, h))],
        out_specs=pl.BlockSpec((1, 1, T), lambda h, i: (h, 0, i)), out_shape=SDS((MLA_HEADS, 1, S), f32),
        compiler_params=_params(("parallel", "parallel")), name=name)(dO, O)


def _flash_bwd(Q, K, kv, delta, L, dO, *, name):
    S = Q.shape[0]
    hq = min(S, 512)
    nh = 2 if S % 1024 == 0 else 1
    tq = nh * hq
    tk = min(S, 512)
    nq = S // tq
    ln2 = math.log(2.0)

    def body(q_ref, k_ref, v_ref, dl_ref, l_ref, do_ref, dq_ref, dk_ref, dv_ref, dk_acc, dv_acc):
        kk = pl.program_id(1)
        i = pl.program_id(2)

        @pl.when((kk == 0) & (i == 0))
        def _():
            dq_ref[...] = jnp.zeros_like(dq_ref)

        @pl.when(i == 0)
        def _():
            dk_acc[...] = jnp.zeros_like(dk_acc)
            dv_acc[...] = jnp.zeros_like(dv_acc)

        k = k_ref[...]
        v = v_ref[...]
        dk_new = dk_acc[...]
        dv_new = dv_acc[...]
        for hf in range(nh):
            sl = slice(hf * hq, (hf + 1) * hq)
            q = q_ref[sl, :]
            st = lax.dot_general(k, q, (NT, ((), ())), preferred_element_type=f32)
            pt = jnp.exp2(st - l_ref[0, :, sl])
            delta = dl_ref[0, :, sl]
            dob = do_ref[sl, :].astype(bf16)
            dv_new = dv_new + lax.dot_general(pt.astype(bf16), dob, (NN, ((), ())), preferred_element_type=f32)
            dpt = lax.dot_general(v, dob, (NT, ((), ())), preferred_element_type=f32)
            dst = (pt * (dpt - delta)).astype(bf16)
            dk_new = dk_new + lax.dot_general(dst, q, (NN, ((), ())), preferred_element_type=f32)
            rows = pl.ds(pl.multiple_of(i * tq + hf * hq, hq), hq)
            dq_ref[rows, :] += lax.dot_general(dst, k, (TN, ((), ())), preferred_element_type=f32)
        dk_acc[...] = dk_new
        dv_acc[...] = dv_new

        @pl.when(i == nq - 1)
        def _():
            dk_ref[...] = dk_acc[...] * ln2
            dv_ref[...] = dv_acc[...]

    return pl.pallas_call(
        body, grid=(MLA_HEADS, S // tk, nq),
        in_specs=[pl.BlockSpec((tq, 256), lambda h, k, i: (i, h)),
                  pl.BlockSpec((tk, 256), lambda h, k, i: (k, h)),
                  pl.BlockSpec((tk, 128), lambda h, k, i: (k, 2 * h + 1)),
                  pl.BlockSpec((1, 1, tq), lambda h, k, i: (h, 0, i)),
                  pl.BlockSpec((1, 1, tq), lambda h, k, i: (h, 0, i)),
                  pl.BlockSpec((tq, 128), lambda h, k, i: (i, h))],
        out_specs=[pl.BlockSpec((S, 256), lambda h, k, i: (0, h)),
                   pl.BlockSpec((tk, 256), lambda h, k, i: (k, h)),
                   pl.BlockSpec((tk, 128), lambda h, k, i: (k, h))],
        out_shape=[SDS((S, 2048), f32), SDS((S, 2048), f32), SDS((S, 1024), f32)],
        scratch_shapes=[pltpu.VMEM((tk, 256), f32), pltpu.VMEM((tk, 128), f32)],
        compiler_params=_params(("parallel", "arbitrary", "arbitrary"), VMEM_BIG), name=name)(Q, K, kv, delta, L, dO)


def _mla_qk_prep(q, kv, zm, cosm, sinm, *, name):
    S = q.shape[0]
    T = min(S, 256)
    scale = (MLA_NOPE + MLA_ROPE) ** -0.5 * math.log2(math.e)

    def body(q_ref, kv_ref, kr_ref, cos_ref, sin_ref, oq_ref, ok_ref):
        cs, sn = cos_ref[...], sin_ref[...]
        kr = _rope(kr_ref[...], cs, sn, 32).astype(bf16)
        for h in range(MLA_HEADS):
            a = 256 * h
            oq_ref[:, a:a + 128] = (q_ref[:, a:a + 128] * scale).astype(bf16)
            oq_ref[:, a + 128:a + 256] = (_rope(q_ref[:, a + 128:a + 256], cs, sn, 32) * scale).astype(bf16)
            ok_ref[:, a:a + 128] = kv_ref[:, a:a + 128]
            ok_ref[:, a + 128:a + 256] = kr

    row = lambda w, col=0: pl.BlockSpec((T, w), lambda i: (i, col))
    return pl.pallas_call(
        body, grid=(S // T,), in_specs=[row(2048), row(2048), row(128, 6), row(128), row(128)],
        out_specs=[row(2048), row(2048)], out_shape=[SDS((S, 2048), bf16), SDS((S, 2048), bf16)],
        compiler_params=_params(("parallel",), VMEM_BIG), name=name)(q, kv, zm, cosm, sinm)


def _mla_bwd_prep(dQ, dK, dV, cosm, sinm, *, name):
    S = dQ.shape[0]
    T = min(S, 256)
    scale = (MLA_NOPE + MLA_ROPE) ** -0.5

    def body(dq_ref, dk_ref, dv_ref, cos_ref, sin_ref, oq_ref, okv_ref, okr_ref):
        cs, sn = cos_ref[...], sin_ref[...]
        kr = jnp.zeros((T, 128), f32)
        for h in range(MLA_HEADS):
            a = 256 * h
            oq_ref[:, a:a + 128] = (dq_ref[:, a:a + 128] * scale).astype(bf16)
            oq_ref[:, a + 128:a + 256] = (_rope_t(dq_ref[:, a + 128:a + 256], cs, sn, 32) * scale).astype(bf16)
            okv_ref[:, a:a + 128] = dk_ref[:, a:a + 128].astype(bf16)
            okv_ref[:, a + 128:a + 256] = dv_ref[:, 128 * h:128 * h + 128].astype(bf16)
            kr = kr + dk_ref[:, a + 128:a + 256]
        okr_ref[...] = _rope_t(kr, cs, sn, 32)

    return pl.pallas_call(
        body, grid=(S // T,),
        in_specs=[pl.BlockSpec((T, 2048), lambda i: (i, 0)), pl.BlockSpec((T, 2048), lambda i: (i, 0)),
                  pl.BlockSpec((T, 1024), lambda i: (i, 0)), pl.BlockSpec((T, 128), lambda i: (i, 0)),
                  pl.BlockSpec((T, 128), lambda i: (i, 0))],
        out_specs=[pl.BlockSpec((T, 2048), lambda i: (i, 0)), pl.BlockSpec((T, 2048), lambda i: (i, 0)),
                   pl.BlockSpec((T, 128), lambda i: (i, 0))],
        out_shape=[SDS((S, 2048), bf16), SDS((S, 2048), bf16), SDS((S, 128), f32)],
        compiler_params=_params(("parallel",), VMEM_BIG), name=name)(dQ, dK, dV, cosm, sinm)


def _mla_norm_bwd(zm, qg, kvg, dcqn, dckvn, dkr, *, name):
    S = zm.shape[0]
    T = min(S, 512)

    def body(cq_ref, ckv_ref, qg_ref, kvg_ref, dcq_ref, dckv_ref, dkr_ref, o_ref, dqg_ref, dkvg_ref):
        i = pl.program_id(0)
        _, vjp = jax.vjp(_rms, cq_ref[...], qg_ref[...])
        dcq, dqg = vjp(dcq_ref[...])
        _, vjp2 = jax.vjp(_rms, ckv_ref[...], kvg_ref[...])
        dckv, dkvg = vjp2(dckv_ref[...])
        o_ref[:, 0:384] = dcq.astype(bf16)
        o_ref[:, 384:512] = jnp.zeros((T, 128), bf16)
        o_ref[:, 512:768] = dckv.astype(bf16)
        o_ref[:, 768:896] = dkr_ref[...].astype(bf16)

        @pl.when(i == 0)
        def _():
            dqg_ref[...] = dqg
            dkvg_ref[...] = dkvg

        @pl.when(i > 0)
        def _():
            dqg_ref[...] += dqg
            dkvg_ref[...] += dkvg

    return pl.pallas_call(
        body, grid=(S // T,),
        in_specs=[pl.BlockSpec((T, 384), lambda i: (i, 0)), pl.BlockSpec((T, 256), lambda i: (i, 2)),
                  pl.BlockSpec((1, 384), lambda i: (0, 0)), pl.BlockSpec((1, 256), lambda i: (0, 0)),
                  pl.BlockSpec((T, 384), lambda i: (i, 0)), pl.BlockSpec((T, 256), lambda i: (i, 0)),
                  pl.BlockSpec((T, 128), lambda i: (i, 0))],
        out_specs=[pl.BlockSpec((T, 896), lambda i: (i, 0)), pl.BlockSpec((1, 384), lambda i: (0, 0)),
                   pl.BlockSpec((1, 256), lambda i: (0, 0))],
        out_shape=[SDS((S, 896), bf16), SDS((1, 384), f32), SDS((1, 256), f32)],
        compiler_params=_params(("arbitrary",)), name=name)(zm, zm, qg, kvg, dcqn, dckvn, dkr)


def _s5_disc(a_re, a_im, ldt, b_re, b_im):
    dt = jnp.exp(ldt)
    ar = jnp.minimum(a_re, -1e-4)
    mag = jnp.exp(dt * ar)
    abr = mag * jnp.cos(dt * a_im)
    abi = mag * jnp.sin(dt * a_im)
    den = ar * ar + a_im * a_im
    nr = abr - 1.0
    ni = abi
    cr = (nr * ar + ni * a_im) / den
    ci = (ni * ar - nr * a_im) / den
    return abr, abi, cr * b_re - ci * b_im, cr * b_im + ci * b_re


def _s5_param_fwd(a_re, a_im, ldt, b_re, b_im, *, name):
    R = SDS((1, 8192), f32)
    M = SDS((16, 8192), f32)
    Pw = SDS((64, 8192), f32)

    def body(a_re_r, a_im_r, ldt_r, b_re_r, b_im_r, o1, o2, o3, o4, p_re, p_im):
        abr, abi, bbr, bbi = _s5_disc(a_re_r[...], a_im_r[...], ldt_r[...], b_re_r[...], b_im_r[...])
        o1[...] = abr
        o2[...] = abi
        o3[...] = bbr
        o4[...] = bbi
        dt = jnp.exp(ldt_r[...])
        ar = jnp.minimum(a_re_r[...], -1e-4)
        n = lax.broadcasted_iota(jnp.int32, (64, 8192), 0).astype(f32) + 1.0
        mag = jnp.exp(n * (dt * ar))
        ang = n * (dt * a_im_r[...])
        p_re[...] = mag * jnp.cos(ang)
        p_im[...] = mag * jnp.sin(ang)

    return pl.pallas_call(body, out_shape=[R, R, M, M, Pw, Pw], name=name)(a_re, a_im, ldt, b_re, b_im)


def _s5_param_bwd(a_re, a_im, ldt, b_re, b_im, d_abr, d_abi, d_bbr, d_bbi, *, name):
    R = SDS((1, 8192), f32)
    M = SDS((16, 8192), f32)

    def body(a_re_r, a_im_r, ldt_r, b_re_r, b_im_r, c1, c2, c3, c4, o1, o2, o3, o4, o5):
        _, vjp = jax.vjp(_s5_disc, a_re_r[...], a_im_r[...], ldt_r[...], b_re_r[...], b_im_r[...])
        g = vjp((c1[...], c2[...], c3[...], c4[...]))
        for o, v in zip((o1, o2, o3, o4, o5), g):
            o[...] = v

    return pl.pallas_call(body, out_shape=[R, R, R, M, M], name=name)(a_re, a_im, ldt, b_re, b_im, d_abr, d_abi, d_bbr, d_bbi)


def _seg_perm(T, inverse):
    L = T // S5_SEG
    i = jnp.arange(T)
    src = (i % S5_SEG) * L + i // S5_SEG
    P = (src[:, None] == jnp.arange(T)[None, :]).astype(bf16)
    return P.T if inverse else P


def _perm_rows(a, P, *, name):
    S, W = a.shape
    T = P.shape[0]

    def body(p_ref, a_ref, o_ref):
        o_ref[...] = lax.dot_general(p_ref[...], a_ref[...], (NN, ((), ())), preferred_element_type=f32).astype(o_ref.dtype)

    return pl.pallas_call(
        body, grid=(S // T,), in_specs=[pl.BlockSpec((T, T), lambda i: (0, 0)), pl.BlockSpec((T, W), lambda i: (i, 0))],
        out_specs=pl.BlockSpec((T, W), lambda i: (i, 0)), out_shape=SDS((S, W), a.dtype),
        compiler_params=_params(("parallel",)), name=name)(P, a)


def _scan_core(xr, xi, ar, ai, pwr_ref, pwi_ref, a64r, a64i, carry, *, reverse, T, conj):
    L = T // S5_SEG
    sg = -1.0 if conj else 1.0
    arb = jnp.broadcast_to(ar, (8, 512))
    aib = jnp.broadcast_to(ai, (8, 512))
    UN = 4

    def step(r4, c):
        cr, ci = c
        for u in range(UN):
            r0 = r4 * UN + u
            r = (L - 1 - r0) if reverse else r0
            rows = pl.ds(pl.multiple_of(r * 8, 8), 8)
            nr = arb * cr - aib * ci + xr[rows, :]
            ni = arb * ci + aib * cr + xi[rows, :]
            xr[rows, :] = nr
            xi[rows, :] = ni
            cr, ci = nr, ni
        return cr, ci

    lr, li = lax.fori_loop(0, L // UN, step, (jnp.zeros((8, 512), f32), jnp.zeros((8, 512), f32)))
    row8 = lax.broadcasted_iota(jnp.int32, (8, 512), 0)
    cr = carry[0, 0:1, :]
    ci = carry[1, 0:1, :]
    a6i = sg * a64i
    cin_r = jnp.zeros((8, 512), f32)
    cin_i = jnp.zeros((8, 512), f32)
    for seg in (range(S5_SEG - 1, -1, -1) if reverse else range(S5_SEG)):
        cin_r = jnp.where(row8 == seg, cr, cin_r)
        cin_i = jnp.where(row8 == seg, ci, cin_i)
        ncr = lr[seg:seg + 1, :] + a64r * cr - a6i * ci
        nci = li[seg:seg + 1, :] + a64r * ci + a6i * cr
        cr, ci = ncr, nci
    carry[0, 0:1, :] = cr
    carry[1, 0:1, :] = ci

    def fix(r4, _):
        for u in range(UN):
            r = r4 * UN + u
            rows = pl.ds(pl.multiple_of(r * 8, 8), 8)
            pr = pwr_ref[pl.ds(r, 1), :]
            pi = sg * pwi_ref[pl.ds(r, 1), :]
            xr[rows, :] += pr * cin_r - pi * cin_i
            xi[rows, :] += pr * cin_i + pi * cin_r
        return 0

    lax.fori_loop(0, L // UN, fix, 0)


def _s5_scan_fwd(u, BBr, BBi, CCr, CCi, abr, abi, pwr, pwi, *, reverse, name):
    S = u.shape[0]
    T = min(S, 512)
    NB = S // T
    L = T // S5_SEG
    d = 1 if reverse else 0

    def tb(b):
        return (NB - 1 - b) if reverse else b

    def body(u_ref, bbr_ref, bbi_ref, ccr_ref, cci_ref, ar_ref, ai_ref, pwr_ref, pwi_ref, y_ref, xr_ref, xi_ref, carry):
        b = pl.program_id(1)

        @pl.when(b == 0)
        def _():
            carry[...] = jnp.zeros_like(carry)

        ub = u_ref[...].astype(bf16)
        xr_ref[...] = lax.dot_general(ub, bbr_ref[0, 0], (NN, ((), ())), preferred_element_type=f32)
        xi_ref[...] = lax.dot_general(ub, bbi_ref[0, 0], (NN, ((), ())), preferred_element_type=f32)
        a6 = (0 if reverse else L - 1)
        _scan_core(xr_ref, xi_ref, ar_ref[...], ai_ref[...], pwr_ref, pwi_ref, pwr_ref[a6:a6 + 1, :], pwi_ref[a6:a6 + 1, :],
                   carry, reverse=reverse, T=T, conj=False)
        y_ref[...] = _dot(xr_ref[...], ccr_ref[0, 0], NN) - _dot(xi_ref[...], cci_ref[0, 0], NN)

    mat = lambda shp: pl.BlockSpec((1, 1) + shp, lambda j, b: (d, j, 0, 0))
    vec = lambda r: pl.BlockSpec((r, 512), lambda j, b: (0, d * S5_NJ + j))
    return pl.pallas_call(
        body, grid=(S5_NJ, NB),
        in_specs=[pl.BlockSpec((T, 128), lambda j, b: (tb(b), j)), mat((128, 512)), mat((128, 512)), mat((512, 128)),
                  mat((512, 128)), vec(1), vec(1), vec(L), vec(L)],
        out_specs=[pl.BlockSpec((T, 128), lambda j, b: (tb(b), j)), pl.BlockSpec((T, 512), lambda j, b: (tb(b), j)),
                   pl.BlockSpec((T, 512), lambda j, b: (tb(b), j))],
        out_shape=[SDS((S, 1024), f32), SDS((S, 4096), f32), SDS((S, 4096), f32)],
        scratch_shapes=[pltpu.VMEM((2, 8, 512), f32)],
        compiler_params=_params(("parallel", "arbitrary")), name=name)(u, BBr, BBi, CCr, CCi, abr, abi, pwr, pwi)


def _s5_scan_bwd(u, dy, xr, xi, BBr, BBi, CCr, CCi, abr, abi, pwr, pwi, *, reverse, name):
    S = u.shape[0]
    T = min(S, 512)
    NB = S // T
    L = T // S5_SEG
    d = 1 if reverse else 0
    adj_rev = not reverse

    def tb(b):
        return b if reverse else (NB - 1 - b)

    def bnd(b):
        t = tb(b)
        if reverse:
            return jnp.minimum((t + 1) * (T // 8), S // 8 - 1)
        return jnp.maximum(t * (T // 8) - 1, 0)

    def body(u_ref, dy_ref, xr_ref, xi_ref, xbr_ref, xbi_ref, bbr_ref, bbi_ref, ccr_ref, cci_ref, ar_ref, ai_ref,
             pwr_ref, pwi_ref, du_ref, dbbr_ref, dbbi_ref, dccr_ref, dcci_ref, dar_ref, dai_ref, carry, lam):
        b = pl.program_id(1)

        @pl.when(b == 0)
        def _():
            carry[...] = jnp.zeros_like(carry)
            for r in (dbbr_ref, dbbi_ref, dccr_ref, dcci_ref, dar_ref, dai_ref):
                r[...] = jnp.zeros_like(r)

        dyb = dy_ref[...]
        lam[0] = lax.dot_general(dyb, ccr_ref[0, 0], (NT, ((), ())), preferred_element_type=f32)
        lam[1] = -lax.dot_general(dyb, cci_ref[0, 0], (NT, ((), ())), preferred_element_type=f32)
        a6 = (0 if adj_rev else L - 1)
        _scan_core(lam.at[0], lam.at[1], ar_ref[...], -ai_ref[...], pwr_ref, pwi_ref, pwr_ref[a6:a6 + 1, :],
                   pwi_ref[a6:a6 + 1, :], carry, reverse=adj_rev, T=T, conj=True)
        ub = u_ref[...].astype(bf16)
        first = (b == NB - 1)
        lrb = lam[0].astype(bf16)
        lib = lam[1].astype(bf16)
        du_ref[...] = lax.dot_general(lrb, bbr_ref[0, 0], (NT, ((), ())), preferred_element_type=f32) \
            + lax.dot_general(lib, bbi_ref[0, 0], (NT, ((), ())), preferred_element_type=f32)
        dbbr_ref[0, 0] += lax.dot_general(ub, lrb, (TN, ((), ())), preferred_element_type=f32)
        dbbi_ref[0, 0] += lax.dot_general(ub, lib, (TN, ((), ())), preferred_element_type=f32)
        dccr_ref[0, 0] += lax.dot_general(dyb, xr_ref[...].astype(bf16), (TN, ((), ())), preferred_element_type=f32)
        dcci_ref[0, 0] -= lax.dot_general(dyb, xi_ref[...].astype(bf16), (TN, ((), ())), preferred_element_type=f32)
        row8 = lax.broadcasted_iota(jnp.int32, (8, 512), 0)
        if reverse:
            body_x, body_l, edge_l = slice(8, T), slice(0, T - 8), slice(T - 8, T)
            sp_r = jnp.where(row8 == 7, jnp.where(first, 0.0, xbr_ref[0:1, :]), pltpu.roll(xr_ref[0:8, :], 7, axis=0))
            sp_i = jnp.where(row8 == 7, jnp.where(first, 0.0, xbi_ref[0:1, :]), pltpu.roll(xi_ref[0:8, :], 7, axis=0))
        else:
            body_x, body_l, edge_l = slice(0, T - 8), slice(8, T), slice(0, 8)
            sp_r = jnp.where(row8 == 0, jnp.where(first, 0.0, xbr_ref[7:8, :]), pltpu.roll(xr_ref[T - 8:T, :], 1, axis=0))
            sp_i = jnp.where(row8 == 0, jnp.where(first, 0.0, xbi_ref[7:8, :]), pltpu.roll(xi_ref[T - 8:T, :], 1, axis=0))
        xpr, xpi = xr_ref[body_x, :], xi_ref[body_x, :]
        lr, li = lam[0, body_l, :], lam[1, body_l, :]
        er, ei = lam[0, edge_l, :], lam[1, edge_l, :]
        dar_ref[...] += jnp.sum(xpr * lr + xpi * li, axis=0, keepdims=True) + jnp.sum(sp_r * er + sp_i * ei, axis=0, keepdims=True)
        dai_ref[...] += jnp.sum(xpr * li - xpi * lr, axis=0, keepdims=True) + jnp.sum(sp_r * ei - sp_i * er, axis=0, keepdims=True)

    mat = lambda shp: pl.BlockSpec((1, 1) + shp, lambda j, b: (d, j, 0, 0))
    omat = lambda shp: pl.BlockSpec((1, 1) + shp, lambda j, b: (0, j, 0, 0))
    vec = lambda r: pl.BlockSpec((r, 512), lambda j, b: (0, d * S5_NJ + j))
    blk = lambda w: pl.BlockSpec((T, w), lambda j, b: (tb(b), j))
    return pl.pallas_call(
        body, grid=(S5_NJ, NB),
        in_specs=[blk(128), blk(128), blk(512), blk(512),
                  pl.BlockSpec((8, 512), lambda j, b: (bnd(b), j)), pl.BlockSpec((8, 512), lambda j, b: (bnd(b), j)),
                  mat((128, 512)), mat((128, 512)), mat((512, 128)), mat((512, 128)), vec(1), vec(1), vec(L), vec(L)],
        out_specs=[blk(128), omat((128, 512)), omat((128, 512)), omat((128, 512)), omat((128, 512)),
                   pl.BlockSpec((1, 512), lambda j, b: (0, j)), pl.BlockSpec((1, 512), lambda j, b: (0, j))],
        out_shape=[SDS((S, 1024), f32), SDS((1, 8, 128, 512), f32), SDS((1, 8, 128, 512), f32), SDS((1, 8, 128, 512), f32),
                   SDS((1, 8, 128, 512), f32), SDS((1, 4096), f32), SDS((1, 4096), f32)],
        scratch_shapes=[pltpu.VMEM((2, 8, 512), f32), pltpu.VMEM((2, T, 512), f32)],
        compiler_params=_params(("parallel", "arbitrary"), VMEM_BIG), name=name)(
            u, dy, xr, xi, xr, xi, BBr, BBi, CCr, CCi, abr, abi, pwr, pwi)


def _silu_mul(g, u):
    return jax.nn.silu(g) * u


def _mixf(p0, p1, p2, z0, z1, z2):
    return jax.nn.sigmoid(z0) * p0 + jax.nn.sigmoid(z1) * p1 + jax.nn.sigmoid(z2) * p2


def _s5_act(yf, yb, u, dd):
    return jax.nn.gelu(yf + yb + dd * u)


def _glu(a, b):
    return a * jax.nn.sigmoid(b)


def _layer_fwd(x, w, tabs, l):
    S = x.shape[0]
    T = min(S, 512)
    I = S // T
    nm = lambda s: f"L{l}_{s}"
    sv = {'x': x}
    h = _rmsnorm_fwd(x, w['norm1_g'], name=nm("norm1"))
    zr = _mm(h, w['W_ret'], name=nm("in_ret"))
    zm = _mm(h, w['W_mla'], name=nm("in_mla"))
    h_seg = _perm_rows(h, tabs['seg_perm'], name=nm("s5_perm_h"))
    zs = _mm(h_seg, w['W_s5'], name=nm("in_s5"))
    zg = _mm(h, w['W_gate'], name=nm("in_gate"))
    sv.update(h=h, h_seg=h_seg, zr=zr, zm=zm, zs=zs, zg=zg)

    yf, stf = _ret_dir_fwd(zr, w['lg'], tabs['cos_r'], tabs['sin_r'], reverse=False, name=nm("ret_f"))
    yb, stb = _ret_dir_fwd(zr, w['lg'], tabs['cos_r'], tabs['sin_r'], reverse=True, name=nm("ret_b"))
    hd = lambda j: j
    y_ret = _pw(_gn_gate, [yf, yb, zr, w['ret_gn_g']],
                [_row(T, 256, hd), _row(T, 256, hd), _row(T, 256, lambda j: 8 + j), _par(256, hd)],
                [SDS((S, 1024), bf16)], [_row(T, 256, hd)], (RET_HEADS, I), name=nm("ret_gn"))[0]
    sv.update(yf=yf, yb=yb, stf=stf, stb=stb, y_ret=y_ret)

    cqn, ckvn = _pw(lambda a, b, g1, g2: (_rms(a, g1), _rms(b, g2)), [zm, zm, w['mla_q_norm_g'], w['mla_kv_norm_g']],
                    [_row(T, 384), _row(T, 256, lambda j: 2), _par(384), _par(256)],
                    [SDS((S, 384), bf16), SDS((S, 256), bf16)], [_row(T, 384), _row(T, 256)], (1, I), name=nm("mla_norm"))
    q = _mm(cqn, w['W_uq'], name=nm("mla_uq"))
    kv = _mm(ckvn, w['W_ukv'], out_dtype=bf16, name=nm("mla_ukv"))
    Q, K = _mla_qk_prep(q, kv, zm, tabs['cos_m'], tabs['sin_m'], name=nm("mla_qkprep"))
    O, Lse = _flash_fwd(Q, K, kv, name=nm("mla_attn"))
    sv.update(cqn=cqn, ckvn=ckvn, kv=kv, Q=Q, K=K, O=O, Lse=Lse)

    s5 = w['s5']
    ysf, xrf, xif = _s5_scan_fwd(zs, s5['BBr'], s5['BBi'], s5['CCr'], s5['CCi'], s5['abr'], s5['abi'], s5['pwr_f'], s5['pwi_f'],
                                 reverse=False, name=nm("s5_f"))
    ysb, xrb, xib = _s5_scan_fwd(zs, s5['BBr'], s5['BBi'], s5['CCr'], s5['CCi'], s5['abr'], s5['abi'], s5['pwr_f'], s5['pwi_f'],
                                 reverse=True, name=nm("s5_b"))
    gact = _pw(_s5_act, [ysf, ysb, zs, w['s5_d']], [_row(T, D), _row(T, D), _row(T, D), _par(D)],
               [SDS((S, D), bf16)], [_row(T, D)], (1, I), name=nm("s5_act"))[0]
    gg = _mm(gact, w['W_glu'], name=nm("s5_glu_mm"))
    y_s5 = _pw(_glu, [gg, gg], [_row(T, D), _row(T, D, lambda j: 1)], [SDS((S, D), bf16)], [_row(T, D)], (1, I),
               name=nm("s5_glu"))[0]
    y_s5 = _perm_rows(y_s5, tabs['seg_unperm'], name=nm("s5_unperm_y"))
    sv.update(ysf=ysf, ysb=ysb, xrf=xrf, xif=xif, xrb=xrb, xib=xib, gact=gact, gg=gg, y_s5=y_s5)

    ys = [y_ret, O, y_s5]
    pr = [_mm(ys[i], w['W_br'][i], name=nm(f"branch{i}")) for i in range(3)]
    mix = _pw(_mixf, pr + [zg, zg, zg],
              [_row(T, D)] * 3 + [_row(T, D), _row(T, D, lambda j: 1), _row(T, D, lambda j: 2)],
              [SDS((S, D), bf16)], [_row(T, D)], (1, I), name=nm("mix"))[0]
    x1 = _mm(mix, w['W_out'], res=x, name=nm("out_proj"))
    h2 = _rmsnorm_fwd(x1, w['norm2_g'], name=nm("norm2"))
    fgu = _mm(h2, w['W_gu'], name=nm("ffn_gu"))
    act = _pw(_silu_mul, [fgu, fgu], [_row(T, 1408, lambda j: j), _row(T, 1408, lambda j: 2 + j)],
              [SDS((S, FFN_H), bf16)], [_row(T, 1408, lambda j: j)], (2, I), name=nm("ffn_act"))[0]
    x2 = _mm(act, w['W_down'], res=x1, name=nm("ffn_down"))
    sv.update(pr=pr, mix=mix, x1=x1, h2=h2, fgu=fgu, act=act)
    return x2, sv


def _vjp_fn(fn, n_primal, cast=None):
    def g(*args):
        _, vjp = jax.vjp(fn, *args[:n_primal])
        return vjp(args[n_primal].astype(f32))
    return g


def _layer_bwd(dx2, w, tabs, sv, l):
    S = dx2.shape[0]
    T = min(S, 512)
    I = S // T
    nm = lambda s: f"L{l}_b_{s}"
    g = {}
    hd = lambda j: j

    dact = _mm(dx2, w['W_down'], tb=True, name=nm("ffn_down_dx"))
    g['W_down'] = _mmT(sv['act'], dx2, name=nm("ffn_down_dw"))
    dfg, dfu = _pw(_vjp_fn(_silu_mul, 2), [sv['fgu'], sv['fgu'], dact],
                   [_row(T, 1408, lambda j: j), _row(T, 1408, lambda j: 2 + j), _row(T, 1408, lambda j: j)],
                   [SDS((S, FFN_H), bf16), SDS((S, FFN_H), bf16)], [_row(T, 1408, lambda j: j)] * 2, (2, I), name=nm("ffn_act"))
    dfgu = jnp.concatenate([dfg, dfu], axis=1)
    g['W_gu'] = _mmT(sv['h2'], dfgu, name=nm("ffn_gu_dw"))
    dh2 = _mm(dfgu, w['W_gu'], tb=True, name=nm("ffn_gu_dx"))
    dx1, g['norm2_g'] = _rmsnorm_bwd(sv['x1'], w['norm2_g'], dh2, dx2, name=nm("norm2"))

    dmix = _mm(dx1, w['W_out'], tb=True, name=nm("out_dx"))
    g['W_out'] = _mmT(sv['mix'], dx1, name=nm("out_dw"))
    zg = sv['zg']
    outs = _pw(_vjp_fn(_mixf, 6), sv['pr'] + [zg, zg, zg, dmix],
               [_row(T, D)] * 3 + [_row(T, D), _row(T, D, lambda j: 1), _row(T, D, lambda j: 2), _row(T, D)],
               [SDS((S, D), bf16)] * 6, [_row(T, D)] * 6, (1, I), name=nm("mix"))
    dpr, dzg = outs[:3], jnp.concatenate(outs[3:], axis=1)
    ys = [sv['y_ret'], sv['O'], sv['y_s5']]
    g['W_br'] = [_mmT(ys[i], dpr[i], name=nm(f"branch{i}_dw")) for i in range(3)]
    dpr_seg = _perm_rows(dpr[2], tabs['seg_perm'], name=nm("s5_perm_dy"))
    dys = [_mm(dpr[i] if i < 2 else dpr_seg, w['W_br'][i], tb=True, out_dtype=bf16 if i == 1 else f32,
               name=nm(f"branch{i}_dx")) for i in range(3)]

    gg = sv['gg']
    dga, dgb = _pw(_vjp_fn(_glu, 2), [gg, gg, dys[2]], [_row(T, D), _row(T, D, lambda j: 1), _row(T, D)],
                   [SDS((S, D), bf16)] * 2, [_row(T, D)] * 2, (1, I), name=nm("s5_glu"))
    dgg = jnp.concatenate([dga, dgb], axis=1)
    g['W_glu'] = _mmT(sv['gact'], dgg, name=nm("s5_glu_dw"))
    dgact = _mm(dgg, w['W_glu'], tb=True, name=nm("s5_glu_dx"))

    def act_bwd(yf, yb, u, dd, ct):
        _, vjp = jax.vjp(_s5_act, yf, yb, u, dd)
        dyf, _, du, ddd = vjp(ct)
        return dyf, du, ddd

    dys5, du_direct, g['s5_d'] = _pw(act_bwd, [sv['ysf'], sv['ysb'], sv['zs'], w['s5_d'], dgact],
                                     [_row(T, D)] * 3 + [_par(D), _row(T, D)],
                                     [SDS((S, D), bf16), SDS((S, D), f32), SDS((1, D), f32)],
                                     [_row(T, D), _row(T, D), _par(D)], (1, I), n_acc=1, name=nm("s5_act"))
    s5 = w['s5']
    rf = _s5_scan_bwd(sv['zs'], dys5, sv['xrf'], sv['xif'], s5['BBr'], s5['BBi'], s5['CCr'], s5['CCi'], s5['abr'], s5['abi'],
                      s5['pwr_a'], s5['pwi_a'], reverse=False, name=nm("s5_f"))
    rb = _s5_scan_bwd(sv['zs'], dys5, sv['xrb'], sv['xib'], s5['BBr'], s5['BBi'], s5['CCr'], s5['CCi'], s5['abr'], s5['abi'],
                      s5['pwr_a'], s5['pwi_a'], reverse=True, name=nm("s5_b"))
    g['s5'] = (rf[1:], rb[1:])
    dzs_seg = _pw(lambda a, b, c: a + b + c, [du_direct, rf[0], rb[0]], [_row(T, D)] * 3, [SDS((S, D), bf16)], [_row(T, D)],
                  (1, I), name=nm("s5_du"))[0]
    dzs = _perm_rows(dzs_seg, tabs['seg_unperm'], name=nm("s5_unperm_dz"))

    delta = _attn_delta(dys[1], sv['O'], name=nm("mla_delta"))
    dQ, dK, dV = _flash_bwd(sv['Q'], sv['K'], sv['kv'], delta, sv['Lse'], dys[1], name=nm("mla_attn"))
    dq_lin, dkv, dkr = _mla_bwd_prep(dQ, dK, dV, tabs['cos_m'], tabs['sin_m'], name=nm("mla_prep"))
    g['W_uq'] = _mmT(sv['cqn'], dq_lin, name=nm("mla_uq_dw"))
    dcqn = _mm(dq_lin, w['W_uq'], tb=True, name=nm("mla_uq_dx"))
    g['W_ukv'] = _mmT(sv['ckvn'], dkv, name=nm("mla_ukv_dw"))
    dckvn = _mm(dkv, w['W_ukv'], tb=True, name=nm("mla_ukv_dx"))
    dzm, g['mla_q_norm_g'], g['mla_kv_norm_g'] = _mla_norm_bwd(sv['zm'], w['mla_q_norm_g'], w['mla_kv_norm_g'], dcqn, dckvn, dkr,
                                                               name=nm("mla_norm"))

    zr = sv['zr']

    def gn_bwd(yf, yb, gt, gn, ct):
        _, vjp = jax.vjp(_gn_gate, yf, yb, gt, gn)
        dyf, _, dgt, dgn = vjp(ct)
        return dyf, dgt, dgn

    dyr, dgate, g['ret_gn_g'] = _pw(gn_bwd, [sv['yf'], sv['yb'], zr, w['ret_gn_g'], dys[0]],
                                    [_row(T, 256, hd), _row(T, 256, hd), _row(T, 256, lambda j: 8 + j), _par(256, hd),
                                     _row(T, 256, hd)],
                                    [SDS((S, 1024), bf16), SDS((S, 1024), bf16), SDS((1, 1024), f32)],
                                    [_row(T, 256, hd), _row(T, 256, hd), _par(256, hd)], (RET_HEADS, I), n_acc=1, name=nm("ret_gn"))
    qf, kf, vf, lgf = _ret_dir_bwd(zr, w['lg'], tabs['cos_r'], tabs['sin_r'], dyr, sv['stf'], reverse=False, name=nm("ret_f"))
    qb, kb, vb, lgb = _ret_dir_bwd(zr, w['lg'], tabs['cos_r'], tabs['sin_r'], dyr, sv['stb'], reverse=True, name=nm("ret_b"))
    g['lg'] = jnp.stack([lgf[:, 0, 0], lgb[:, 0, 0]])
    add2 = lambda a, b: a + b
    dq = _pw(add2, [qf, qb], [_row(T, 512)] * 2, [SDS((S, 512), bf16)], [_row(T, 512)], (1, I), name=nm("ret_dq"))[0]
    dk = _pw(add2, [kf, kb], [_row(T, 512)] * 2, [SDS((S, 512), bf16)], [_row(T, 512)], (1, I), name=nm("ret_dk"))[0]
    dv = _pw(add2, [vf, vb], [_row(T, D)] * 2, [SDS((S, D), bf16)], [_row(T, D)], (1, I), name=nm("ret_dv"))[0]
    dzr = jnp.concatenate([dq, dk, dv, dgate], axis=1)

    h = sv['h']
    g['W_ret'] = _mmT(h, dzr, name=nm("in_ret_dw"))
    g['W_mla'] = _mmT(h, dzm, name=nm("in_mla_dw"))
    g['W_s5'] = _mmT(sv['h_seg'], dzs_seg, name=nm("in_s5_dw"))
    g['W_gate'] = _mmT(h, dzg, name=nm("in_gate_dw"))
    dh = _mm(dzr, w['W_ret'], tb=True, name=nm("in_ret_dx"))
    dh = _mm(dzm, w['W_mla'], tb=True, res=dh, name=nm("in_mla_dx"))
    dh = _mm(dzs, w['W_s5'], tb=True, res=dh, name=nm("in_s5_dx"))
    dh = _mm(dzg, w['W_gate'], tb=True, res=dh, name=nm("in_gate_dx"))
    dx, g['norm1_g'] = _rmsnorm_bwd(sv['x'], w['norm1_g'], dh, dx1, name=nm("norm1"))
    return dx, g


def _loss_head(x, tgt, gain, *, name):
    S, W = x.shape
    T = min(S, 512)

    def loss_fn(xv, gv, tv):
        return 0.5 * jnp.sum(jnp.mean(jnp.square(_rms(xv, gv) - tv), axis=-1, keepdims=True), axis=0, keepdims=True)

    def fn(xv, gv, tv):
        lv, vjp = jax.vjp(lambda a, b: loss_fn(a, b, tv), xv, gv)
        dx, dg = vjp(jnp.ones((1, 1), f32))
        return dx, jnp.broadcast_to(lv, (1, 128)), dg

    return _pw(fn, [x, gain, tgt], [_row(T, W), _par(W), _row(T, W)],
               [SDS((S, W), f32), SDS((1, 128), f32), SDS((1, W), f32)], [_row(T, W), _par(128), _par(W)],
               (1, S // T), n_acc=2, name=name)


def _rope_tabs(S):
    def tab(dim):
        inv = 1.0 / (ROPE_THETA ** (jnp.arange(0, dim, 2, dtype=f32) / dim))
        ang = jnp.arange(S, dtype=f32)[:, None] * inv[None, :]
        return jnp.cos(ang), jnp.sin(ang)

    cr, sr = tab(RET_DK)
    cm, sm = tab(MLA_ROPE)
    z = jnp.zeros((S, 64), f32)
    return {'cos_r': jnp.concatenate([cr, cr], axis=1), 'sin_r': jnp.concatenate([-sr, sr], axis=1),
            'cos_m': jnp.concatenate([cm, cm, z], axis=1), 'sin_m': jnp.concatenate([-sm, sm, z], axis=1),
            'seg_perm': _seg_perm(512, False), 'seg_unperm': _seg_perm(512, True)}


def _bd_B(bb):
    b5 = bb.reshape(16, 2, 8, 8, 64)
    return jnp.einsum('cdjgp,gh->djgchp', b5, jnp.eye(8, dtype=bb.dtype)).reshape(2, 8, 128, 512)


def _bd_B_t(dBB):
    return jnp.einsum('djgcgp->cdjgp', dBB.reshape(2, 8, 8, 16, 8, 64)).reshape(16, 8192)


def _bd_C(c):
    c5 = c.reshape(2, 8, 8, 16, 64)
    return jnp.einsum('djgcp,gh->djgphc', c5, jnp.eye(8, dtype=c.dtype)).reshape(2, 8, 512, 128)


def _s5_rows(p, l):
    a_re = p['s5_a_re'][l].reshape(1, 8192)
    a_im = p['s5_a_im'][l].reshape(1, 8192)
    ldt = jnp.broadcast_to(p['s5_log_dt'][l][:, :, None], (2, S5_G, S5_P)).reshape(1, 8192)
    b_re = p['s5_b_re'][l].transpose(3, 0, 1, 2).reshape(16, 8192)
    b_im = p['s5_b_im'][l].transpose(3, 0, 1, 2).reshape(16, 8192)
    return a_re, a_im, ldt, b_re, b_im


def _layer_weights(big, p, l):
    w_in = big['w_in'][l]
    z = lambda n: jnp.zeros((D, n), w_in.dtype)
    w = {
        'W_ret': w_in[:, 0:3072],
        'W_mla': jnp.concatenate([w_in[:, 3072:3456], z(128), w_in[:, 3456:3712], w_in[:, 3712:3776], z(64)], axis=1),
        'W_s5': w_in[:, 3776:4800],
        'W_gate': w_in[:, 4800:7872],
        'W_uq': jnp.pad(big['mla_w_uq'][l].reshape(MLA_Q_LORA, MLA_HEADS, 192), ((0, 0), (0, 0), (0, 64))).reshape(MLA_Q_LORA, 2048),
        'W_ukv': big['mla_w_ukv'][l],
        'W_glu': big['s5_w_glu'][l],
        'W_br': [big['w_branch'][l, i] for i in range(3)],
        'W_out': big['w_out'][l],
        'W_gu': big['ffn_w_gu'][l],
        'W_down': big['ffn_w_down'][l],
    }
    for n in ('norm1_g', 'ret_gn_g', 'mla_q_norm_g', 'mla_kv_norm_g', 's5_d', 'norm2_g'):
        w[n] = p[n][l][None, :]
    w['lg'] = jax.nn.log_sigmoid(p['ret_decay'][l])
    rows = _s5_rows(p, l)
    abr, abi, bbr, bbi, pwr, pwi = _s5_param_fwd(*rows, name=f"L{l}_s5_param")
    flip = lambda t, first: jnp.concatenate([t[::-1, :4096], t[:, 4096:]] if first else [t[:, :4096], t[::-1, 4096:]], axis=1)
    w['s5'] = {'abr': abr, 'abi': abi, 'BBr': _bd_B(bbr).astype(bf16), 'BBi': _bd_B(bbi).astype(bf16),
               'CCr': _bd_C(p['s5_c_re'][l]).astype(bf16), 'CCi': _bd_C(p['s5_c_im'][l]).astype(bf16),
               'pwr_f': flip(pwr, False), 'pwi_f': flip(pwi, False), 'pwr_a': flip(pwr, True), 'pwi_a': flip(pwi, True),
               'rows': rows}
    return w


def _layer_grads(g, w, p, l):
    out = {}
    m = g['W_mla']
    out['w_in'] = jnp.concatenate([g['W_ret'], m[:, 0:384], m[:, 512:768], m[:, 768:832], g['W_s5'], g['W_gate']], axis=1)
    out['mla_w_uq'] = g['W_uq'].reshape(MLA_Q_LORA, MLA_HEADS, 256)[:, :, :192].reshape(MLA_Q_LORA, 1536)
    out['mla_w_ukv'] = g['W_ukv']
    out['s5_w_glu'] = g['W_glu']
    out['w_branch'] = jnp.stack(g['W_br'])
    out['w_out'] = g['W_out']
    out['ffn_w_gu'] = g['W_gu']
    out['ffn_w_down'] = g['W_down']
    for n in ('norm1_g', 'ret_gn_g', 'mla_q_norm_g', 'mla_kv_norm_g', 's5_d', 'norm2_g'):
        out[n] = g[n][0]
    out['ret_decay'] = g['lg'] * jax.nn.sigmoid(-p['ret_decay'][l])
    (fB_r, fB_i, fC_r, fC_i, fa_r, fa_i), (bB_r, bB_i, bC_r, bC_i, ba_r, ba_i) = g['s5']
    cat = lambda a, b: jnp.concatenate([a, b], axis=0)
    d_bbr = _bd_B_t(cat(fB_r, bB_r))
    d_bbi = _bd_B_t(cat(fB_i, bB_i))
    to_c = lambda t: _bd_B_t(t).reshape(16, 2, S5_G, S5_P).transpose(1, 2, 0, 3)
    out['s5_c_re'] = to_c(cat(fC_r, bC_r))
    out['s5_c_im'] = to_c(cat(fC_i, bC_i))
    d_abr = jnp.concatenate([fa_r, ba_r], axis=1)
    d_abi = jnp.concatenate([fa_i, ba_i], axis=1)
    da_re, da_im, dldt, db_re, db_im = _s5_param_bwd(*w['s5']['rows'], d_abr, d_abi, d_bbr, d_bbi, name=f"L{l}_b_s5_param")
    out['s5_a_re'] = da_re.reshape(2, S5_G, S5_P)
    out['s5_a_im'] = da_im.reshape(2, S5_G, S5_P)
    out['s5_log_dt'] = dldt.reshape(2, S5_G, S5_P).sum(axis=-1)
    out['s5_b_re'] = db_re.reshape(16, 2, S5_G, S5_P).transpose(1, 2, 3, 0)
    out['s5_b_im'] = db_im.reshape(16, 2, S5_G, S5_P).transpose(1, 2, 3, 0)
    return out


def _local_step(x, tgt, big, p):
    S = x.shape[0]
    assert S % 512 == 0
    tabs = _rope_tabs(S)
    ws, svs = [], []
    h = x
    for l in range(DEPTH):
        w = _layer_weights(big, p, l)
        h, sv = _layer_fwd(h, w, tabs, l)
        ws.append(w)
        svs.append(sv)
    dx, lossv, dfinal = _loss_head(h, tgt, p['final_g'][None, :], name="loss_head")
    per_layer = [None] * DEPTH
    for l in reversed(range(DEPTH)):
        dx, g = _layer_bwd(dx, ws[l], tabs, svs[l], l)
        per_layer[l] = _layer_grads(g, ws[l], p, l)
    grads = {n: jnp.stack([per_layer[l][n] for l in range(DEPTH)]) for n in per_layer[0]}
    grads['final_g'] = dfinal[0]
    return lossv[0, 0], dx, grads


_ANY = pl.BlockSpec(memory_space=pl.ANY)


def _place():
    x, y, c = lax.axis_index("x"), lax.axis_index("y"), lax.axis_index("c")
    return x, y, c, [(1 - x, y), (x, 1 - y), (1 - x, 1 - y)]


def _allgather4(arrs, *, name):
    n = len(arrs)

    def body(*refs):
        ins, outs = refs[:n], refs[n:2 * n]
        send, recv, loc = refs[2 * n:]
        x, y, c, chips = _place()
        me = 2 * x + y

        def remote(a, k, slot):
            px, py = chips[k]
            return pltpu.make_async_remote_copy(src_ref=ins[a], dst_ref=outs[a].at[slot], send_sem=send.at[a, k],
                                                recv_sem=recv.at[a, k], device_id=(px, py, c), device_id_type=MESH)

        mine = [pltpu.make_async_copy(ins[a], outs[a].at[me], loc.at[a]) for a in range(n)]
        for cp in mine:
            cp.start()
        sends = [remote(a, k, me) for a in range(n) for k in range(3)]
        for cp in sends:
            cp.start()
        for a in range(n):
            for k, (px, py) in enumerate(chips):
                remote(a, k, 2 * px + py).wait_recv()
        for cp in sends:
            cp.wait_send()
        for cp in mine:
            cp.wait()

    return pl.pallas_call(
        body, in_specs=[_ANY] * n, out_specs=[_ANY] * n, out_shape=[SDS((4,) + a.shape, a.dtype) for a in arrs],
        scratch_shapes=[pltpu.SemaphoreType.DMA((n, 3)), pltpu.SemaphoreType.DMA((n, 3)), pltpu.SemaphoreType.DMA((n,))],
        name=name)(*arrs)


def _rs_exchange(parts, *, name):
    n = len(parts)

    def body(*refs):
        ins, gots = refs[:n], refs[n:2 * n]
        send, recv = refs[2 * n:]
        x, y, c, chips = _place()

        def remote(a, k):
            px, py = chips[k]
            return pltpu.make_async_remote_copy(src_ref=ins[a].at[2 * px + py], dst_ref=gots[a].at[k], send_sem=send.at[a, k],
                                                recv_sem=recv.at[a, k], device_id=(px, py, c), device_id_type=MESH)

        sends = [remote(a, k) for a in range(n) for k in range(3)]
        for cp in sends:
            cp.start()
        for cp in sends:
            cp.wait_recv()
        for cp in sends:
            cp.wait_send()

    return pl.pallas_call(
        body, in_specs=[_ANY] * n, out_specs=[_ANY] * n, out_shape=[SDS((3,) + a.shape[1:], a.dtype) for a in parts],
        scratch_shapes=[pltpu.SemaphoreType.DMA((n, 3)), pltpu.SemaphoreType.DMA((n, 3))], name=name)(*parts)


def _gather_split(arrs, *, name):
    n = len(arrs)

    def body(*refs):
        ins, outs = refs[:n], refs[n:2 * n]
        s_ici, r_ici, s_sib, r_sib = refs[2 * n:]
        x, y, c, chips = _place()
        me = 2 * x + y
        ids = [2 * px + py for px, py in chips] + [me]

        def over_ici(a, k, slot):
            px, py = chips[k]
            return pltpu.make_async_remote_copy(src_ref=ins[a].at[c], dst_ref=outs[a].at[slot, c], send_sem=s_ici.at[a, k],
                                                recv_sem=r_ici.at[a, k], device_id=(px, py, c), device_id_type=MESH)

        def to_sibling(a, k, half, src=None):
            blk = outs[a].at[ids[k], half]
            return pltpu.make_async_remote_copy(src_ref=blk if src is None else src, dst_ref=blk, send_sem=s_sib.at[a, k],
                                                recv_sem=r_sib.at[a, k], device_id=(x, y, 1 - c), device_id_type=MESH)

        sends = [over_ici(a, k, me) for a in range(n) for k in range(3)]
        sends += [to_sibling(a, 3, c, src=ins[a].at[c]) for a in range(n)]
        for cp in sends:
            cp.start()
        for a in range(n):
            for k in range(3):
                over_ici(a, k, ids[k]).wait_recv()
                fwd = to_sibling(a, k, c)
                fwd.start()
                sends.append(fwd)
        for a in range(n):
            for k in range(4):
                to_sibling(a, k, 1 - c).wait_recv()
        for cp in sends:
            cp.wait_send()

    dma = pltpu.SemaphoreType.DMA
    return pl.pallas_call(
        body, in_specs=[_ANY] * n, out_specs=[_ANY] * n, out_shape=[SDS((4,) + a.shape, a.dtype) for a in arrs],
        scratch_shapes=[dma((n, 3)), dma((n, 3)), dma((n, 4)), dma((n, 4))], name=name)(*arrs)


def _swap_halves(parts, *, name):
    n = len(parts)

    def body(*refs):
        ins, gots = refs[:n], refs[n:2 * n]
        send, recv = refs[2 * n:]
        x, y, c, _ = _place()
        cps = [pltpu.make_async_remote_copy(src_ref=ins[a].at[q, 1 - c], dst_ref=gots[a].at[q], send_sem=send.at[a, q],
                                            recv_sem=recv.at[a, q], device_id=(x, y, 1 - c), device_id_type=MESH)
               for a in range(n) for q in range(4)]
        for cp in cps:
            cp.start()
        for cp in cps:
            cp.wait_recv()
        for cp in cps:
            cp.wait_send()

    dma = pltpu.SemaphoreType.DMA
    return pl.pallas_call(
        body, in_specs=[_ANY] * n, out_specs=[_ANY] * n, out_shape=[SDS((4,) + a.shape[2:], a.dtype) for a in parts],
        scratch_shapes=[dma((n, 4)), dma((n, 4))], name=name)(*parts)


def _sibling_copy(arrs, *, name):
    n = len(arrs)

    def body(*refs):
        ins, outs = refs[:n], refs[n:2 * n]
        send, recv = refs[2 * n:]
        x, y, c, _ = _place()
        cps = [pltpu.make_async_remote_copy(src_ref=ins[a], dst_ref=outs[a], send_sem=send.at[a], recv_sem=recv.at[a],
                                            device_id=(x, y, 1 - c), device_id_type=MESH) for a in range(n)]
        for cp in cps:
            cp.start()
        for cp in cps:
            cp.wait_recv()
        for cp in cps:
            cp.wait_send()

    dma = pltpu.SemaphoreType.DMA
    return pl.pallas_call(
        body, in_specs=[_ANY] * n, out_specs=[_ANY] * n, out_shape=[SDS(a.shape, a.dtype) for a in arrs],
        scratch_shapes=[dma((n,)), dma((n,))], name=name)(*arrs)


def _row_tile(R):
    return R if R <= 256 else next(t for t in (256, 128, 64, 32, 16) if R % t == 0)


def _add2(a, b, *, name):
    R, W = a.shape
    tr = _row_tile(R)
    return _pw(lambda p, q: p.astype(f32) + q.astype(f32), [a, b], [_row(tr, W)] * 2, [SDS((R, W), a.dtype)], [_row(tr, W)],
               (1, R // tr), name=name)[0]


def _sum4(own, got, *, name):
    R, W = own.shape
    tr = _row_tile(R)
    g3 = lambda k: pl.BlockSpec((None, tr, W), lambda j, i: (k, i, 0))
    up = lambda t: t.astype(f32)
    return _pw(lambda a, b, c, d: ((up(a) + up(b)) + up(c)) + up(d), [own, got, got, got], [_row(tr, W), g3(0), g3(1), g3(2)],
               [SDS((R, W), f32)], [_row(tr, W)], (1, R // tr), name=name)[0]


def _adamw(g, w, m, v, *, name):
    R, W = w.shape
    tr = _row_tile(R)

    def fn(gv, wv, mv, vv):
        m2 = ADAM_B1 * mv + (1.0 - ADAM_B1) * gv
        v2 = ADAM_B2 * vv + (1.0 - ADAM_B2) * jnp.square(gv)
        m_hat = m2 / (1.0 - ADAM_B1 ** ADAM_STEP)
        v_hat = v2 / (1.0 - ADAM_B2 ** ADAM_STEP)
        return -ADAM_LR * (m_hat / (jnp.sqrt(v_hat) + ADAM_EPS) + ADAM_WD * wv), m2, v2

    return _pw(fn, [g, w, m, v], [_row(tr, W)] * 4, [SDS((R, W), f32)] * 3, [_row(tr, W)] * 3, (1, R // tr), name=name)


def _to_parts(g, axis):
    shp = g.shape
    g = g.reshape(shp[:axis] + (4, shp[axis] // 4) + shp[axis + 1:])
    return jnp.moveaxis(g, axis, 0)


def _from_parts(pt, axis):
    g = jnp.moveaxis(pt, 0, axis)
    shp = g.shape
    return g.reshape(shp[:axis] + (4 * shp[axis + 1],) + shp[axis + 2:])


def kernel(x, norm1_g, w_in, ret_decay, ret_gn_g, mla_q_norm_g, mla_w_uq, mla_kv_norm_g, mla_w_ukv, s5_a_re, s5_a_im, s5_log_dt, s5_b_re, s5_b_im, s5_c_re, s5_c_im, s5_d, s5_w_glu, w_branch, w_out, norm2_g, ffn_w_gu, ffn_w_down, final_g, loss_target, m_norm1_g, m_w_in, m_ret_decay, m_ret_gn_g, m_mla_q_norm_g, m_mla_w_uq, m_mla_kv_norm_g, m_mla_w_ukv, m_s5_a_re, m_s5_a_im, m_s5_log_dt, m_s5_b_re, m_s5_b_im, m_s5_c_re, m_s5_c_im, m_s5_d, m_s5_w_glu, m_w_branch, m_w_out, m_norm2_g, m_ffn_w_gu, m_ffn_w_down, m_final_g, v_norm1_g, v_w_in, v_ret_decay, v_ret_gn_g, v_mla_q_norm_g, v_mla_w_uq, v_mla_kv_norm_g, v_mla_w_ukv, v_s5_a_re, v_s5_a_im, v_s5_log_dt, v_s5_b_re, v_s5_b_im, v_s5_c_re, v_s5_c_im, v_s5_d, v_s5_w_glu, v_w_branch, v_w_out, v_norm2_g, v_ffn_w_gu, v_ffn_w_down, v_final_g):
    wv = dict(zip(W_NAMES, (norm1_g, w_in, ret_decay, ret_gn_g, mla_q_norm_g, mla_w_uq, mla_kv_norm_g, mla_w_ukv, s5_a_re, s5_a_im,
                            s5_log_dt, s5_b_re, s5_b_im, s5_c_re, s5_c_im, s5_d, s5_w_glu, w_branch, w_out, norm2_g, ffn_w_gu,
                            ffn_w_down, final_g)))
    mv = dict(zip(W_NAMES, (m_norm1_g, m_w_in, m_ret_decay, m_ret_gn_g, m_mla_q_norm_g, m_mla_w_uq, m_mla_kv_norm_g, m_mla_w_ukv,
                            m_s5_a_re, m_s5_a_im, m_s5_log_dt, m_s5_b_re, m_s5_b_im, m_s5_c_re, m_s5_c_im, m_s5_d, m_s5_w_glu,
                            m_w_branch, m_w_out, m_norm2_g, m_ffn_w_gu, m_ffn_w_down, m_final_g)))
    vv = dict(zip(W_NAMES, (v_norm1_g, v_w_in, v_ret_decay, v_ret_gn_g, v_mla_q_norm_g, v_mla_w_uq, v_mla_kv_norm_g, v_mla_w_ukv,
                            v_s5_a_re, v_s5_a_im, v_s5_log_dt, v_s5_b_re, v_s5_b_im, v_s5_c_re, v_s5_c_im, v_s5_d, v_s5_w_glu,
                            v_w_branch, v_w_out, v_norm2_g, v_ffn_w_gu, v_ffn_w_down, v_final_g)))
    big_names = list(BIG)

    my_c = lax.axis_index("c")
    my_chip = 2 * lax.axis_index("x") + lax.axis_index("y")
    shards = [wv[n].astype(bf16) for n in big_names]
    gathered = _gather_split(shards, name="gather_weights")
    gathered = [lax.dynamic_update_slice(gt, lax.dynamic_index_in_dim(sh, my_c, 0)[None],
                                         (my_chip, my_c) + (0,) * (sh.ndim - 1)) for gt, sh in zip(gathered, shards)]
    big = {n: _from_parts(gt, BIG[n]) for n, gt in zip(big_names, gathered)}
    small = {n: wv[n] for n in SMALL}

    loss_local, dx, grads = _local_step(x[0], loss_target[0], big, small)

    n_rows = {n: -(-math.prod(wv[n].shape) // 1024) * 8 for n in SMALL}
    used = sum(n_rows.values())
    rows_q = -(-(used + 8) // (4 * 128)) * 128

    def as_rows(d, tail=None):
        blocks = [jnp.pad(d[n].reshape(-1), (0, n_rows[n] * 128 - math.prod(wv[n].shape))).reshape(n_rows[n], 128) for n in SMALL]
        blocks.append(jnp.zeros((8, 128), f32) if tail is None else tail)
        blocks.append(jnp.zeros((4 * rows_q - used - 8, 128), f32))
        return jnp.concatenate(blocks, axis=0)

    loss_rows = jnp.full((8, 128), loss_local, f32)
    parts = [_to_parts(grads[n].astype(bf16), BIG[n]) for n in big_names]
    parts.append(as_rows(grads, loss_rows).reshape(4, 2, rows_q // 2, 128))
    n_arr = len(parts)
    two_d = lambda a: a.reshape(-1, a.shape[-1])
    theirs = _swap_halves(parts, name="grad_swap_halves")
    mine = [lax.dynamic_index_in_dim(p, my_c, 1, keepdims=False) for p in parts]
    chip_sums = [_add2(two_d(mine[a]), two_d(theirs[a]), name=f"grad_add2_{a}").reshape(theirs[a].shape) for a in range(n_arr)]
    got = _rs_exchange(chip_sums, name="grad_exchange")
    own = [lax.dynamic_index_in_dim(s, my_chip, 0, keepdims=False) for s in chip_sums]
    sums = [_sum4(two_d(own[a]), got[a].reshape(3, -1, got[a].shape[-1]), name=f"grad_sum4_{a}") for a in range(n_arr)]
    other = _sibling_copy(sums, name="grad_sibling")
    full = [jnp.stack([jnp.where(my_c == 0, sums[a], other[a]), jnp.where(my_c == 0, other[a], sums[a])]) for a in range(n_arr)]

    out_g, out_d, out_m, out_v = {}, {}, {}, {}
    for a, n in enumerate(big_names):
        shp = wv[n].shape
        res = _adamw(two_d(full[a]), two_d(wv[n]), two_d(mv[n]), two_d(vv[n]), name=f"adamw_{n}")
        out_g[n] = full[a].reshape(shp)
        out_d[n], out_m[n], out_v[n] = [r.reshape(shp) for r in res]
    g_small = _allgather4([full[-1].reshape(rows_q, 128)], name="gather_small_grads")[0].reshape(4 * rows_q, 128)
    loss = g_small[used, 0]
    res = (g_small,) + tuple(_adamw(g_small, as_rows(wv), as_rows(mv), as_rows(vv), name="adamw_small"))
    off = 0
    for n in SMALL:
        k = math.prod(wv[n].shape)
        for dst, r in zip((out_g, out_d, out_m, out_v), res):
            dst[n] = r[off:off + n_rows[n]].reshape(-1)[:k].reshape(wv[n].shape)
        off += n_rows[n]
    return (loss, dx[None], *[out_g[n] for n in W_NAMES], *[out_d[n] for n in W_NAMES], *[out_m[n] for n in W_NAMES],
            *[out_v[n] for n in W_NAMES])
```

```python
import functools
import math

import jax
import jax.numpy as jnp
from jax import lax
from jax.experimental import pallas as pl
from jax.experimental.pallas import tpu as pltpu

f32 = jnp.float32
bf16 = jnp.bfloat16
SDS = jax.ShapeDtypeStruct
MESH = pl.DeviceIdType.MESH

D = 1024
DEPTH = 2
RMS_EPS = 1e-6
GN_EPS = 1e-5
ROPE_THETA = 10000.0
RET_HEADS = 4
RET_DK = 128
RET_DV = 256
RET_CHUNK = 128
MLA_HEADS = 8
MLA_Q_LORA = 384
MLA_KV_LORA = 256
MLA_NOPE = 128
MLA_ROPE = 64
MLA_V = 128
MLA_QW = 256
S5_G = 64
S5_P = 64
S5_C = 16
S5_NJ = 8
S5_SEG = 8
FFN_H = 2816
ADAM_LR = 0.001
ADAM_B1 = 0.9
ADAM_B2 = 0.999
ADAM_EPS = 1e-08
ADAM_WD = 0.01
ADAM_STEP = 10
VMEM_BIG = 56 * 1024 * 1024

W_NAMES = ['norm1_g', 'w_in', 'ret_decay', 'ret_gn_g', 'mla_q_norm_g', 'mla_w_uq', 'mla_kv_norm_g', 'mla_w_ukv',
           's5_a_re', 's5_a_im', 's5_log_dt', 's5_b_re', 's5_b_im', 's5_c_re', 's5_c_im', 's5_d', 's5_w_glu',
           'w_branch', 'w_out', 'norm2_g', 'ffn_w_gu', 'ffn_w_down', 'final_g']
BIG = {'w_in': 2, 'mla_w_uq': 2, 'mla_w_ukv': 2, 's5_w_glu': 2, 'w_branch': 2, 'w_out': 1, 'ffn_w_gu': 2, 'ffn_w_down': 1}
SMALL = [n for n in W_NAMES if n not in BIG]


TILE_BYTES = 6 * 1024 * 1024


def _pick(n, cands=(512, 384, 256, 128), cap=None):
    if n <= 1024 and (cap is None or n <= cap):
        return n
    for c in cands:
        if n % c == 0 and (cap is None or c <= cap):
            return c
    raise ValueError(n)


WIDE = (1408, 1024, 768, 512, 384, 256, 128)


def _params(sem, vmem=None):
    return pltpu.CompilerParams(dimension_semantics=sem, vmem_limit_bytes=vmem)


def _mm(a, b, *, tb=False, res=None, out_dtype=f32, name):
    M, K = a.shape
    N = b.shape[0] if tb else b.shape[1]
    tk = K if K <= 3072 else _pick(K, (1408, 1024, 512))
    nk = K // tk
    tn = _pick(N, WIDE, cap=TILE_BYTES // (tk * b.dtype.itemsize))
    tm = _pick(M)
    if M % 1024 == 0 and 1024 * tk * a.dtype.itemsize <= 4 * 1024 * 1024 and 1024 * tn * 4 <= TILE_BYTES:
        tm = 1024
    assert M % tm == 0 and N % tn == 0 and K % tk == 0

    def body(*refs):
        if res is None:
            a_ref, b_ref, o_ref, acc = refs
        else:
            a_ref, b_ref, r_ref, o_ref, acc = refs
        k = pl.program_id(2)
        dn = (((1,), (1 if tb else 0,)), ((), ()))
        part = lax.dot_general(a_ref[...].astype(bf16), b_ref[...].astype(bf16), dn, preferred_element_type=f32)

        @pl.when(k == 0)
        def _():
            acc[...] = part

        @pl.when(k > 0)
        def _():
            acc[...] += part

        @pl.when(k == nk - 1)
        def _():
            v = acc[...]
            if res is not None:
                v = v + r_ref[...]
            o_ref[...] = v.astype(out_dtype)

    in_specs = [pl.BlockSpec((tm, tk), lambda i, j, k: (i, k)),
                pl.BlockSpec((tn, tk), lambda i, j, k: (j, k)) if tb else pl.BlockSpec((tk, tn), lambda i, j, k: (k, j))]
    args = [a, b]
    if res is not None:
        in_specs.append(pl.BlockSpec((tm, tn), lambda i, j, k: (i, j)))
        args.append(res)
    return pl.pallas_call(
        body, grid=(M // tm, N // tn, nk), in_specs=in_specs,
        out_specs=pl.BlockSpec((tm, tn), lambda i, j, k: (i, j)),
        out_shape=SDS((M, N), out_dtype), scratch_shapes=[pltpu.VMEM((tm, tn), f32)],
        compiler_params=_params(("parallel", "parallel", "arbitrary"), VMEM_BIG), name=name)(*args)


def _mmT(a, b, *, name):
    S, M = a.shape
    N = b.shape[1]
    tn = _pick(N, WIDE)
    tm = _pick(M, WIDE, cap=TILE_BYTES // (tn * 4))
    tk = min(S, 1024)
    nk = S // tk

    def body(a_ref, b_ref, o_ref):
        k = pl.program_id(2)
        part = lax.dot_general(a_ref[...].astype(bf16), b_ref[...].astype(bf16), (((0,), (0,)), ((), ())),
                               preferred_element_type=f32)

        @pl.when(k == 0)
        def _():
            o_ref[...] = part

        @pl.when(k > 0)
        def _():
            o_ref[...] += part

    return pl.pallas_call(
        body, grid=(M // tm, N // tn, nk),
        in_specs=[pl.BlockSpec((tk, tm), lambda i, j, k: (k, i)), pl.BlockSpec((tk, tn), lambda i, j, k: (k, j))],
        out_specs=pl.BlockSpec((tm, tn), lambda i, j, k: (i, j)),
        out_shape=SDS((M, N), f32),
        compiler_params=_params(("parallel", "parallel", "arbitrary"), VMEM_BIG), name=name)(a, b)


def _pw(fn, ins, in_specs, outs, out_specs, grid, *, n_acc=0, name):
    n_in = len(ins)
    n_out = len(outs)

    def body(*refs):
        vals = fn(*[r[...].astype(f32) if r.dtype == bf16 else r[...] for r in refs[:n_in]])
        if not isinstance(vals, (tuple, list)):
            vals = (vals,)
        orefs = refs[n_in:]
        for r, v in zip(orefs[:n_out - n_acc], vals[:n_out - n_acc]):
            r[...] = v.astype(r.dtype)
        if n_acc:
            i = pl.program_id(1)

            @pl.when(i == 0)
            def _():
                for r, v in zip(orefs[n_out - n_acc:], vals[n_out - n_acc:]):
                    r[...] = v.astype(r.dtype)

            @pl.when(i > 0)
            def _():
                for r, v in zip(orefs[n_out - n_acc:], vals[n_out - n_acc:]):
                    r[...] += v.astype(r.dtype)

    res = pl.pallas_call(
        body, grid=grid, in_specs=in_specs, out_specs=out_specs, out_shape=outs,
        compiler_params=_params(("parallel", "arbitrary"), VMEM_BIG), name=name)(*ins)
    return res


def _row(T, w, col=None):
    if col is None:
        return pl.BlockSpec((T, w), lambda j, i: (i, 0))
    return pl.BlockSpec((T, w), lambda j, i: (i, col(j)))


def _par(w, col=None):
    if col is None:
        return pl.BlockSpec((1, w), lambda j, i: (0, 0))
    return pl.BlockSpec((1, w), lambda j, i: (0, col(j)))


def _rms(x, g):
    return x * lax.rsqrt(jnp.mean(x * x, axis=-1, keepdims=True) + RMS_EPS) * g


def _rope(x, cos, sinm, half):
    if half == 64:
        partner = pltpu.roll(x, 64, axis=1)
    else:
        lane = lax.broadcasted_iota(jnp.int32, x.shape, 1)
        partner = jnp.where((lane % (2 * half)) < half, pltpu.roll(x, 128 - half, axis=1), pltpu.roll(x, half, axis=1))
    return x * cos + partner * sinm


def _rope_t(x, cos, sinm, half):
    return _rope(x, cos, -sinm, half)


def _rmsnorm_fwd(x, g, *, name):
    S, W = x.shape
    T = min(S, 512)
    return _pw(lambda xv, gv: _rms(xv, gv), [x, g], [_row(T, W), _par(W)], [SDS((S, W), bf16)], [_row(T, W)],
               (1, S // T), name=name)[0]


def _rmsnorm_bwd(x, g, dh, dres, *, name):
    S, W = x.shape
    T = min(S, 512)

    def fn(xv, gv, dhv, drv):
        _, vjp = jax.vjp(_rms, xv, gv)
        dx, dg = vjp(dhv)
        return dx + drv, dg

    return _pw(fn, [x, g, dh, dres], [_row(T, W), _par(W), _row(T, W), _row(T, W)],
               [SDS((S, W), f32), SDS((1, W), f32)], [_row(T, W), _par(W)], (1, S // T), n_acc=1, name=name)


def _ret_tables(lg, reverse):
    C = RET_CHUNK
    ii = lax.broadcasted_iota(jnp.int32, (C, C), 0).astype(f32)
    jj = lax.broadcasted_iota(jnp.int32, (C, C), 1).astype(f32)
    if not reverse:
        E = ii - jj
        mask = E >= 0
        eq = ii + 1.0
        ek = (C - 1.0) - ii
    else:
        E = jj - ii
        mask = E > 0
        eq = C - ii
        ek = ii
    Dm = jnp.where(mask, jnp.exp(jnp.where(mask, E, 0.0) * lg), 0.0)
    Em = jnp.where(mask, E, 0.0)
    qw = jnp.exp(eq * lg)
    kw = jnp.exp(ek * lg)
    qw2 = jnp.concatenate([qw, qw], axis=1)
    return Dm, Em, eq, ek, qw, kw, qw2, jnp.exp(C * lg)


def _dot(a, b, dims):
    return lax.dot_general(a.astype(bf16), b.astype(bf16), (dims, ((), ())), preferred_element_type=f32)


NN = ((1,), (0,))
NT = ((1,), (1,))
TN = ((0,), (0,))


def _ret_dir_fwd(zr, lg, cos, sinm, *, reverse, name):
    S = zr.shape[0]
    C = RET_CHUNK
    TB = min(S, 512)
    nc = TB // C
    NB = S // TB
    d = 1 if reverse else 0
    scale = RET_DK ** -0.5

    def tb(b):
        return (NB - 1 - b) if reverse else b

    def body(lg_ref, q_ref, k_ref, v_ref, cos_ref, sin_ref, y_ref, st_ref, state):
        h = pl.program_id(0)
        b = pl.program_id(1)

        @pl.when(b == 0)
        def _():
            state[...] = jnp.zeros_like(state)

        Dm, _, _, _, _, kw, qw2, gC = _ret_tables(lg_ref[d, h], reverse)
        order = range(nc - 1, -1, -1) if reverse else range(nc)
        for c in order:
            rows = pl.ds(c * C, C)
            q = _rope(q_ref[rows, :], cos_ref[rows, :], sin_ref[rows, :], 64) * scale
            k = _rope(k_ref[rows, :], cos_ref[rows, :], sin_ref[rows, :], 64)
            v = v_ref[rows, :]
            st = state[...]
            st_ref[0, c] = st
            s = _dot(q, k, NT) * Dm
            o = _dot(s, v, NN) + _dot(q, st, NN) * qw2
            y_ref[rows, :] = o
            state[...] = gC * st + _dot(k * kw, v, TN)

    return pl.pallas_call(
        body, grid=(RET_HEADS, NB),
        in_specs=[pl.BlockSpec(memory_space=pltpu.SMEM),
                  pl.BlockSpec((TB, 128), lambda h, b: (tb(b), h)),
                  pl.BlockSpec((TB, 128), lambda h, b: (tb(b), 4 + h)),
                  pl.BlockSpec((TB, 256), lambda h, b: (tb(b), 4 + h)),
                  pl.BlockSpec((TB, 128), lambda h, b: (tb(b), 0)),
                  pl.BlockSpec((TB, 128), lambda h, b: (tb(b), 0))],
        out_specs=[pl.BlockSpec((TB, 256), lambda h, b: (tb(b), h)),
                   pl.BlockSpec((1, nc, 128, 256), lambda h, b: (h, tb(b), 0, 0))],
        out_shape=[SDS((S, 1024), f32), SDS((RET_HEADS, S // C, 128, 256), f32)],
        scratch_shapes=[pltpu.VMEM((128, 256), f32)],
        compiler_params=_params(("parallel", "arbitrary")), name=name)(lg, zr, zr, zr, cos, sinm)


def _ret_dir_bwd(zr, lg, cos, sinm, dy, states, *, reverse, name):
    S = zr.shape[0]
    C = RET_CHUNK
    TB = min(S, 512)
    nc = TB // C
    NB = S // TB
    d = 1 if reverse else 0
    scale = RET_DK ** -0.5

    def tb(b):
        return b if reverse else (NB - 1 - b)

    def body(lg_ref, q_ref, k_ref, v_ref, cos_ref, sin_ref, dy_ref, st_ref, dq_ref, dk_ref, dv_ref, dlg_ref, dstate):
        h = pl.program_id(0)
        b = pl.program_id(1)

        @pl.when(b == 0)
        def _():
            dstate[...] = jnp.zeros_like(dstate)
            dlg_ref[...] = jnp.zeros_like(dlg_ref)

        Dm, Em, eq, ek, qw, kw, qw2, gC = _ret_tables(lg_ref[d, h], reverse)
        order = range(nc) if reverse else range(nc - 1, -1, -1)
        dlg = jnp.zeros((), f32)
        for c in order:
            rows = pl.ds(c * C, C)
            cs, sn = cos_ref[rows, :], sin_ref[rows, :]
            q = _rope(q_ref[rows, :], cs, sn, 64) * scale
            k = _rope(k_ref[rows, :], cs, sn, 64)
            v = v_ref[rows, :]
            do = dy_ref[rows, :]
            st = st_ref[0, c]
            ds = dstate[...]
            p = _dot(q, k, NT)
            a = p * Dm
            dp = _dot(do, v, NT) * Dm
            dq_cross = _dot(do, st, NT) * qw
            dk_cross = _dot(v, ds, NT) * kw
            dq = _dot(dp, k, NN) + dq_cross
            dk = _dot(dp, q, TN) + dk_cross
            dv = _dot(a, do, TN) + _dot(k * kw, ds, NN)
            dlg = dlg + jnp.sum(dp * p * Em) + jnp.sum(dq_cross * q * eq) + jnp.sum(dk_cross * k * ek) \
                + C * gC * jnp.sum(ds * st)
            dstate[...] = gC * ds + _dot(q * qw, do, TN)
            dq_ref[rows, :] = _rope_t(dq, cs, sn, 64) * scale
            dk_ref[rows, :] = _rope_t(dk, cs, sn, 64)
            dv_ref[rows, :] = dv
        dlg_ref[...] += jnp.full(dlg_ref.shape, dlg, f32)

    return pl.pallas_call(
        body, grid=(RET_HEADS, NB),
        in_specs=[pl.BlockSpec(memory_space=pltpu.SMEM),
                  pl.BlockSpec((TB, 128), lambda h, b: (tb(b), h)),
                  pl.BlockSpec((TB, 128), lambda h, b: (tb(b), 4 + h)),
                  pl.BlockSpec((TB, 256), lambda h, b: (tb(b), 4 + h)),
                  pl.BlockSpec((TB, 128), lambda h, b: (tb(b), 0)),
                  pl.BlockSpec((TB, 128), lambda h, b: (tb(b), 0)),
                  pl.BlockSpec((TB, 256), lambda h, b: (tb(b), h)),
                  pl.BlockSpec((1, nc, 128, 256), lambda h, b: (h, tb(b), 0, 0))],
        out_specs=[pl.BlockSpec((TB, 128), lambda h, b: (tb(b), h)),
                   pl.BlockSpec((TB, 128), lambda h, b: (tb(b), h)),
                   pl.BlockSpec((TB, 256), lambda h, b: (tb(b), h)),
                   pl.BlockSpec((1, 1, 128), lambda h, b: (h, 0, 0))],
        out_shape=[SDS((S, 512), f32), SDS((S, 512), f32), SDS((S, 1024), f32), SDS((RET_HEADS, 1, 128), f32)],
        scratch_shapes=[pltpu.VMEM((128, 256), f32)],
        compiler_params=_params(("parallel", "arbitrary")), name=name)(lg, zr, zr, zr, cos, sinm, dy, states)


def _gn_gate(yf, yb, g, gn):
    y = yf + yb
    mu = jnp.mean(y, axis=-1, keepdims=True)
    var = jnp.mean(jnp.square(y - mu), axis=-1, keepdims=True)
    yn = (y - mu) * lax.rsqrt(var + GN_EPS)
    return jax.nn.silu(g) * (yn * gn)


def _flash_fwd(Q, K, kv, *, name):
    S = Q.shape[0]
    hq = min(S, 256)
    nh = 4 if S % 1024 == 0 else 1
    tq = nh * hq
    tk = min(S, 512)
    nk = S // tk

    def body(q_ref, k_ref, v_ref, o_ref, l_ref, m_s, l_s, acc):
        kk = pl.program_id(2)

        @pl.when(kk == 0)
        def _():
            m_s[...] = jnp.full_like(m_s, -jnp.inf)
            l_s[...] = jnp.zeros_like(l_s)
            acc[...] = jnp.zeros_like(acc)

        k = k_ref[...]
        v = v_ref[...]
        sts = [lax.dot_general(k, q_ref[hf * hq:(hf + 1) * hq, :], (NT, ((), ())), preferred_element_type=f32)
               for hf in range(nh)]
        for hf in range(nh):
            st = sts[hf]
            m_prev = m_s[hf]
            m_new = jnp.maximum(m_prev, jnp.max(st, axis=0, keepdims=True))
            pt = jnp.exp2(st - m_new)
            alpha = jnp.exp2(m_prev - m_new)
            l_s[hf] = alpha * l_s[hf] + jnp.sum(pt, axis=0, keepdims=True)
            acc[hf] = alpha * acc[hf] + lax.dot_general(v, pt.astype(bf16), (TN, ((), ())), preferred_element_type=f32)
            m_s[hf] = m_new

        @pl.when(kk == nk - 1)
        def _():
            for hf in range(nh):
                o_ref[hf * hq:(hf + 1) * hq, :] = jnp.transpose(acc[hf] / l_s[hf]).astype(bf16)
                l_ref[0, :, hf * hq:(hf + 1) * hq] = m_s[hf] + jnp.log2(l_s[hf])

    return pl.pallas_call(
        body, grid=(MLA_HEADS, S // tq, nk),
        in_specs=[pl.BlockSpec((tq, 256), lambda h, i, k: (i, h)),
                  pl.BlockSpec((tk, 256), lambda h, i, k: (k, h)),
                  pl.BlockSpec((tk, 128), lambda h, i, k: (k, 2 * h + 1))],
        out_specs=[pl.BlockSpec((tq, 128), lambda h, i, k: (i, h)), pl.BlockSpec((1, 1, tq), lambda h, i, k: (h, 0, i))],
        out_shape=[SDS((S, 1024), bf16), SDS((MLA_HEADS, 1, S), f32)],
        scratch_shapes=[pltpu.VMEM((nh, 1, hq), f32), pltpu.VMEM((nh, 1, hq), f32), pltpu.VMEM((nh, 128, hq), f32)],
        compiler_params=_params(("parallel", "parallel", "arbitrary")), name=name)(Q, K, kv)


def _attn_delta(dO, O, *, name):
    S = dO.shape[0]
    T = min(S, 512)

    def body(do_ref, o_ref, d_ref):
        prod = do_ref[...] * o_ref[...].astype(f32)
        d_ref[0] = lax.dot_general(jnp.ones((8, 128), f32), prod, (NT, ((), ())), preferred_element_type=f32,
                                   precision=lax.Precision.HIGHEST)[0:1, :]

    return pl.pallas_call(
        body, grid=(MLA_HEADS, S // T),
        in_specs=[pl.BlockSpec((T, 128), lambda h, i: (i, h)), pl.BlockSpec((T, 128), lambda h, i: (i, h))],
        out_specs=pl.BlockSpec((1, 1, T), lambda h, i: (h, 0, i)), out_shape=SDS((MLA_HEADS, 1, S), f32),
        compiler_params=_params(("parallel", "parallel")), name=name)(dO, O)


def _flash_bwd(Q, K, kv, delta, L, dO, *, name):
    S = Q.shape[0]
    hq = min(S, 512)
    nh = 2 if S % 1024 == 0 else 1
    tq = nh * hq
    tk = min(S, 512)
    nq = S // tq
    ln2 = math.log(2.0)

    def body(q_ref, k_ref, v_ref, dl_ref, l_ref, do_ref, dq_ref, dk_ref, dv_ref, dk_acc, dv_acc):
        kk = pl.program_id(1)
        i = pl.program_id(2)

        @pl.when((kk == 0) & (i == 0))
        def _():
            dq_ref[...] = jnp.zeros_like(dq_ref)

        @pl.when(i == 0)
        def _():
            dk_acc[...] = jnp.zeros_like(dk_acc)
            dv_acc[...] = jnp.zeros_like(dv_acc)

        k = k_ref[...]
        v = v_ref[...]
        dk_new = dk_acc[...]
        dv_new = dv_acc[...]
        for hf in range(nh):
            sl = slice(hf * hq, (hf + 1) * hq)
            q = q_ref[sl, :]
            st = lax.dot_general(k, q, (NT, ((), ())), preferred_element_type=f32)
            pt = jnp.exp2(st - l_ref[0, :, sl])
            delta = dl_ref[0, :, sl]
            dob = do_ref[sl, :].astype(bf16)
            dv_new = dv_new + lax.dot_general(pt.astype(bf16), dob, (NN, ((), ())), preferred_element_type=f32)
            dpt = lax.dot_general(v, dob, (NT, ((), ())), preferred_element_type=f32)
            dst = (pt * (dpt - delta)).astype(bf16)
            dk_new = dk_new + lax.dot_general(dst, q, (NN, ((), ())), preferred_element_type=f32)
            rows = pl.ds(pl.multiple_of(i * tq + hf * hq, hq), hq)
            dq_ref[rows, :] += lax.dot_general(dst, k, (TN, ((), ())), preferred_element_type=f32)
        dk_acc[...] = dk_new
        dv_acc[...] = dv_new

        @pl.when(i == nq - 1)
        def _():
            dk_ref[...] = dk_acc[...] * ln2
            dv_ref[...] = dv_acc[...]

    return pl.pallas_call(
        body, grid=(MLA_HEADS, S // tk, nq),
        in_specs=[pl.BlockSpec((tq, 256), lambda h, k, i: (i, h)),
                  pl.BlockSpec((tk, 256), lambda h, k, i: (k, h)),
                  pl.BlockSpec((tk, 128), lambda h, k, i: (k, 2 * h + 1)),
                  pl.BlockSpec((1, 1, tq), lambda h, k, i: (h, 0, i)),
                  pl.BlockSpec((1, 1, tq), lambda h, k, i: (h, 0, i)),
                  pl.BlockSpec((tq, 128), lambda h, k, i: (i, h))],
        out_specs=[pl.BlockSpec((S, 256), lambda h, k, i: (0, h)),
                   pl.BlockSpec((tk, 256), lambda h, k, i: (k, h)),
                   pl.BlockSpec((tk, 128), lambda h, k, i: (k, h))],
        out_shape=[SDS((S, 2048), f32), SDS((S, 2048), f32), SDS((S, 1024), f32)],
        scratch_shapes=[pltpu.VMEM((tk, 256), f32), pltpu.VMEM((tk, 128), f32)],
        compiler_params=_params(("parallel", "arbitrary", "arbitrary"), VMEM_BIG), name=name)(Q, K, kv, delta, L, dO)


def _mla_qk_prep(q, kv, zm, cosm, sinm, *, name):
    S = q.shape[0]
    T = min(S, 256)
    scale = (MLA_NOPE + MLA_ROPE) ** -0.5 * math.log2(math.e)

    def body(q_ref, kv_ref, kr_ref, cos_ref, sin_ref, oq_ref, ok_ref):
        cs, sn = cos_ref[...], sin_ref[...]
        kr = _rope(kr_ref[...], cs, sn, 32).astype(bf16)
        for h in range(MLA_HEADS):
            a = 256 * h
            oq_ref[:, a:a + 128] = (q_ref[:, a:a + 128].astype(f32) * scale).astype(bf16)
            oq_ref[:, a + 128:a + 256] = (_rope(q_ref[:, a + 128:a + 256].astype(f32), cs, sn, 32) * scale).astype(bf16)
            ok_ref[:, a:a + 128] = kv_ref[:, a:a + 128]
            ok_ref[:, a + 128:a + 256] = kr

    row = lambda w, col=0: pl.BlockSpec((T, w), lambda i: (i, col))
    return pl.pallas_call(
        body, grid=(S // T,), in_specs=[row(2048), row(2048), row(128, 6), row(128), row(128)],
        out_specs=[row(2048), row(2048)], out_shape=[SDS((S, 2048), bf16), SDS((S, 2048), bf16)],
        compiler_params=_params(("parallel",), VMEM_BIG), name=name)(q, kv, zm, cosm, sinm)


def _mla_bwd_prep(dQ, dK, dV, cosm, sinm, *, name):
    S = dQ.shape[0]
    T = min(S, 256)
    scale = (MLA_NOPE + MLA_ROPE) ** -0.5

    def body(dq_ref, dk_ref, dv_ref, cos_ref, sin_ref, oq_ref, okv_ref, okr_ref):
        cs, sn = cos_ref[...], sin_ref[...]
        kr = jnp.zeros((T, 128), f32)
        for h in range(MLA_HEADS):
            a = 256 * h
            oq_ref[:, a:a + 128] = (dq_ref[:, a:a + 128] * scale).astype(bf16)
            oq_ref[:, a + 128:a + 256] = (_rope_t(dq_ref[:, a + 128:a + 256], cs, sn, 32) * scale).astype(bf16)
            okv_ref[:, a:a + 128] = dk_ref[:, a:a + 128].astype(bf16)
            okv_ref[:, a + 128:a + 256] = dv_ref[:, 128 * h:128 * h + 128].astype(bf16)
            kr = kr + dk_ref[:, a + 128:a + 256]
        okr_ref[...] = _rope_t(kr, cs, sn, 32)

    return pl.pallas_call(
        body, grid=(S // T,),
        in_specs=[pl.BlockSpec((T, 2048), lambda i: (i, 0)), pl.BlockSpec((T, 2048), lambda i: (i, 0)),
                  pl.BlockSpec((T, 1024), lambda i: (i, 0)), pl.BlockSpec((T, 128), lambda i: (i, 0)),
                  pl.BlockSpec((T, 128), lambda i: (i, 0))],
        out_specs=[pl.BlockSpec((T, 2048), lambda i: (i, 0)), pl.BlockSpec((T, 2048), lambda i: (i, 0)),
                   pl.BlockSpec((T, 128), lambda i: (i, 0))],
        out_shape=[SDS((S, 2048), bf16), SDS((S, 2048), bf16), SDS((S, 128), f32)],
        compiler_params=_params(("parallel",), VMEM_BIG), name=name)(dQ, dK, dV, cosm, sinm)


def _mla_norm_bwd(zm, qg, kvg, dcqn, dckvn, dkr, *, name):
    S = zm.shape[0]
    T = min(S, 512)

    def body(cq_ref, ckv_ref, qg_ref, kvg_ref, dcq_ref, dckv_ref, dkr_ref, o_ref, dqg_ref, dkvg_ref):
        i = pl.program_id(0)
        _, vjp = jax.vjp(_rms, cq_ref[...], qg_ref[...])
        dcq, dqg = vjp(dcq_ref[...])
        _, vjp2 = jax.vjp(_rms, ckv_ref[...], kvg_ref[...])
        dckv, dkvg = vjp2(dckv_ref[...])
        o_ref[:, 0:384] = dcq.astype(bf16)
        o_ref[:, 384:512] = jnp.zeros((T, 128), bf16)
        o_ref[:, 512:768] = dckv.astype(bf16)
        o_ref[:, 768:896] = dkr_ref[...].astype(bf16)

        @pl.when(i == 0)
        def _():
            dqg_ref[...] = dqg
            dkvg_ref[...] = dkvg

        @pl.when(i > 0)
        def _():
            dqg_ref[...] += dqg
            dkvg_ref[...] += dkvg

    return pl.pallas_call(
        body, grid=(S // T,),
        in_specs=[pl.BlockSpec((T, 384), lambda i: (i, 0)), pl.BlockSpec((T, 256), lambda i: (i, 2)),
                  pl.BlockSpec((1, 384), lambda i: (0, 0)), pl.BlockSpec((1, 256), lambda i: (0, 0)),
                  pl.BlockSpec((T, 384), lambda i: (i, 0)), pl.BlockSpec((T, 256), lambda i: (i, 0)),
                  pl.BlockSpec((T, 128), lambda i: (i, 0))],
        out_specs=[pl.BlockSpec((T, 896), lambda i: (i, 0)), pl.BlockSpec((1, 384), lambda i: (0, 0)),
                   pl.BlockSpec((1, 256), lambda i: (0, 0))],
        out_shape=[SDS((S, 896), bf16), SDS((1, 384), f32), SDS((1, 256), f32)],
        compiler_params=_params(("arbitrary",)), name=name)(zm, zm, qg, kvg, dcqn, dckvn, dkr)


def _s5_disc(a_re, a_im, ldt, b_re, b_im):
    dt = jnp.exp(ldt)
    ar = jnp.minimum(a_re, -1e-4)
    mag = jnp.exp(dt * ar)
    abr = mag * jnp.cos(dt * a_im)
    abi = mag * jnp.sin(dt * a_im)
    den = ar * ar + a_im * a_im
    nr = abr - 1.0
    ni = abi
    cr = (nr * ar + ni * a_im) / den
    ci = (ni * ar - nr * a_im) / den
    return abr, abi, cr * b_re - ci * b_im, cr * b_im + ci * b_re


def _s5_param_fwd(a_re, a_im, ldt, b_re, b_im, *, name):
    R = SDS((1, 8192), f32)
    M = SDS((16, 8192), f32)
    Pw = SDS((64, 8192), f32)

    def body(a_re_r, a_im_r, ldt_r, b_re_r, b_im_r, o1, o2, o3, o4, p_re, p_im):
        abr, abi, bbr, bbi = _s5_disc(a_re_r[...], a_im_r[...], ldt_r[...], b_re_r[...], b_im_r[...])
        o1[...] = abr
        o2[...] = abi
        o3[...] = bbr
        o4[...] = bbi
        dt = jnp.exp(ldt_r[...])
        ar = jnp.minimum(a_re_r[...], -1e-4)
        n = lax.broadcasted_iota(jnp.int32, (64, 8192), 0).astype(f32) + 1.0
        mag = jnp.exp(n * (dt * ar))
        ang = n * (dt * a_im_r[...])
        p_re[...] = mag * jnp.cos(ang)
        p_im[...] = mag * jnp.sin(ang)

    return pl.pallas_call(body, out_shape=[R, R, M, M, Pw, Pw], name=name)(a_re, a_im, ldt, b_re, b_im)


def _s5_param_bwd(a_re, a_im, ldt, b_re, b_im, d_abr, d_abi, d_bbr, d_bbi, *, name):
    R = SDS((1, 8192), f32)
    M = SDS((16, 8192), f32)

    def body(a_re_r, a_im_r, ldt_r, b_re_r, b_im_r, c1, c2, c3, c4, o1, o2, o3, o4, o5):
        _, vjp = jax.vjp(_s5_disc, a_re_r[...], a_im_r[...], ldt_r[...], b_re_r[...], b_im_r[...])
        g = vjp((c1[...], c2[...], c3[...], c4[...]))
        for o, v in zip((o1, o2, o3, o4, o5), g):
            o[...] = v

    return pl.pallas_call(body, out_shape=[R, R, R, M, M], name=name)(a_re, a_im, ldt, b_re, b_im, d_abr, d_abi, d_bbr, d_bbi)


def _seg_perm(T, inverse):
    L = T // S5_SEG
    i = jnp.arange(T)
    src = (i % S5_SEG) * L + i // S5_SEG
    P = (src[:, None] == jnp.arange(T)[None, :]).astype(bf16)
    return P.T if inverse else P


def _perm_rows(a, P, *, name):
    S, W = a.shape
    T = P.shape[0]

    def body(p_ref, a_ref, o_ref):
        o_ref[...] = lax.dot_general(p_ref[...], a_ref[...], (NN, ((), ())), preferred_element_type=f32).astype(o_ref.dtype)

    return pl.pallas_call(
        body, grid=(S // T,), in_specs=[pl.BlockSpec((T, T), lambda i: (0, 0)), pl.BlockSpec((T, W), lambda i: (i, 0))],
        out_specs=pl.BlockSpec((T, W), lambda i: (i, 0)), out_shape=SDS((S, W), a.dtype),
        compiler_params=_params(("parallel",)), name=name)(P, a)


def _scan_core(xr, xi, ar, ai, pwr_ref, pwi_ref, a64r, a64i, carry, *, reverse, T, conj):
    L = T // S5_SEG
    sg = -1.0 if conj else 1.0
    arb = jnp.broadcast_to(ar, (8, 512))
    aib = jnp.broadcast_to(ai, (8, 512))
    UN = 4

    def step(r4, c):
        cr, ci = c
        for u in range(UN):
            r0 = r4 * UN + u
            r = (L - 1 - r0) if reverse else r0
            rows = pl.ds(pl.multiple_of(r * 8, 8), 8)
            nr = arb * cr - aib * ci + xr[rows, :]
            ni = arb * ci + aib * cr + xi[rows, :]
            xr[rows, :] = nr
            xi[rows, :] = ni
            cr, ci = nr, ni
        return cr, ci

    lr, li = lax.fori_loop(0, L // UN, step, (jnp.zeros((8, 512), f32), jnp.zeros((8, 512), f32)))
    row8 = lax.broadcasted_iota(jnp.int32, (8, 512), 0)
    cr = carry[0, 0:1, :]
    ci = carry[1, 0:1, :]
    a6i = sg * a64i
    cin_r = jnp.zeros((8, 512), f32)
    cin_i = jnp.zeros((8, 512), f32)
    for seg in (range(S5_SEG - 1, -1, -1) if reverse else range(S5_SEG)):
        cin_r = jnp.where(row8 == seg, cr, cin_r)
        cin_i = jnp.where(row8 == seg, ci, cin_i)
        ncr = lr[seg:seg + 1, :] + a64r * cr - a6i * ci
        nci = li[seg:seg + 1, :] + a64r * ci + a6i * cr
        cr, ci = ncr, nci
    carry[0, 0:1, :] = cr
    carry[1, 0:1, :] = ci

    def fix(r4, _):
        for u in range(UN):
            r = r4 * UN + u
            rows = pl.ds(pl.multiple_of(r * 8, 8), 8)
            pr = pwr_ref[pl.ds(r, 1), :]
            pi = sg * pwi_ref[pl.ds(r, 1), :]
            xr[rows, :] += pr * cin_r - pi * cin_i
            xi[rows, :] += pr * cin_i + pi * cin_r
        return 0

    lax.fori_loop(0, L // UN, fix, 0)


def _s5_scan_fwd(u, BBr, BBi, CCr, CCi, abr, abi, pwr, pwi, *, reverse, name):
    S = u.shape[0]
    T = min(S, 512)
    NB = S // T
    L = T // S5_SEG
    d = 1 if reverse else 0

    def tb(b):
        return (NB - 1 - b) if reverse else b

    def body(u_ref, bbr_ref, bbi_ref, ccr_ref, cci_ref, ar_ref, ai_ref, pwr_ref, pwi_ref, y_ref, xr_ref, xi_ref, carry):
        b = pl.program_id(1)

        @pl.when(b == 0)
        def _():
            carry[...] = jnp.zeros_like(carry)

        ub = u_ref[...].astype(bf16)
        xr_ref[...] = lax.dot_general(ub, bbr_ref[0, 0], (NN, ((), ())), preferred_element_type=f32)
        xi_ref[...] = lax.dot_general(ub, bbi_ref[0, 0], (NN, ((), ())), preferred_element_type=f32)
        a6 = (0 if reverse else L - 1)
        _scan_core(xr_ref, xi_ref, ar_ref[...], ai_ref[...], pwr_ref, pwi_ref, pwr_ref[a6:a6 + 1, :], pwi_ref[a6:a6 + 1, :],
                   carry, reverse=reverse, T=T, conj=False)
        y_ref[...] = _dot(xr_ref[...], ccr_ref[0, 0], NN) - _dot(xi_ref[...], cci_ref[0, 0], NN)

    mat = lambda shp: pl.BlockSpec((1, 1) + shp, lambda j, b: (d, j, 0, 0))
    vec = lambda r: pl.BlockSpec((r, 512), lambda j, b: (0, d * S5_NJ + j))
    return pl.pallas_call(
        body, grid=(S5_NJ, NB),
        in_specs=[pl.BlockSpec((T, 128), lambda j, b: (tb(b), j)), mat((128, 512)), mat((128, 512)), mat((512, 128)),
                  mat((512, 128)), vec(1), vec(1), vec(L), vec(L)],
        out_specs=[pl.BlockSpec((T, 128), lambda j, b: (tb(b), j)), pl.BlockSpec((T, 512), lambda j, b: (tb(b), j)),
                   pl.BlockSpec((T, 512), lambda j, b: (tb(b), j))],
        out_shape=[SDS((S, 1024), f32), SDS((S, 4096), f32), SDS((S, 4096), f32)],
        scratch_shapes=[pltpu.VMEM((2, 8, 512), f32)],
        compiler_params=_params(("parallel", "arbitrary")), name=name)(u, BBr, BBi, CCr, CCi, abr, abi, pwr, pwi)


def _s5_scan_bwd(u, dy, xr, xi, BBr, BBi, CCr, CCi, abr, abi, pwr, pwi, *, reverse, name):
    S = u.shape[0]
    T = min(S, 512)
    NB = S // T
    L = T // S5_SEG
    d = 1 if reverse else 0
    adj_rev = not reverse

    def tb(b):
        return b if reverse else (NB - 1 - b)

    def bnd(b):
        t = tb(b)
        if reverse:
            return jnp.minimum((t + 1) * (T // 8), S // 8 - 1)
        return jnp.maximum(t * (T // 8) - 1, 0)

    def body(u_ref, dy_ref, xr_ref, xi_ref, xbr_ref, xbi_ref, bbr_ref, bbi_ref, ccr_ref, cci_ref, ar_ref, ai_ref,
             pwr_ref, pwi_ref, du_ref, dbbr_ref, dbbi_ref, dccr_ref, dcci_ref, dar_ref, dai_ref, carry, lam):
        b = pl.program_id(1)

        @pl.when(b == 0)
        def _():
            carry[...] = jnp.zeros_like(carry)
            for r in (dbbr_ref, dbbi_ref, dccr_ref, dcci_ref, dar_ref, dai_ref):
                r[...] = jnp.zeros_like(r)

        dyb = dy_ref[...]
        lam[0] = lax.dot_general(dyb, ccr_ref[0, 0], (NT, ((), ())), preferred_element_type=f32)
        lam[1] = -lax.dot_general(dyb, cci_ref[0, 0], (NT, ((), ())), preferred_element_type=f32)
        a6 = (0 if adj_rev else L - 1)
        _scan_core(lam.at[0], lam.at[1], ar_ref[...], -ai_ref[...], pwr_ref, pwi_ref, pwr_ref[a6:a6 + 1, :],
                   pwi_ref[a6:a6 + 1, :], carry, reverse=adj_rev, T=T, conj=True)
        ub = u_ref[...].astype(bf16)
        first = (b == NB - 1)
        lrb = lam[0].astype(bf16)
        lib = lam[1].astype(bf16)
        du_ref[...] = lax.dot_general(lrb, bbr_ref[0, 0], (NT, ((), ())), preferred_element_type=f32) \
            + lax.dot_general(lib, bbi_ref[0, 0], (NT, ((), ())), preferred_element_type=f32)
        dbbr_ref[0, 0] += lax.dot_general(ub, lrb, (TN, ((), ())), preferred_element_type=f32)
        dbbi_ref[0, 0] += lax.dot_general(ub, lib, (TN, ((), ())), preferred_element_type=f32)
        dccr_ref[0, 0] += lax.dot_general(dyb, xr_ref[...].astype(bf16), (TN, ((), ())), preferred_element_type=f32)
        dcci_ref[0, 0] -= lax.dot_general(dyb, xi_ref[...].astype(bf16), (TN, ((), ())), preferred_element_type=f32)
        row8 = lax.broadcasted_iota(jnp.int32, (8, 512), 0)
        if reverse:
            body_x, body_l, edge_l = slice(8, T), slice(0, T - 8), slice(T - 8, T)
            sp_r = jnp.where(row8 == 7, jnp.where(first, 0.0, xbr_ref[0:1, :]), pltpu.roll(xr_ref[0:8, :], 7, axis=0))
            sp_i = jnp.where(row8 == 7, jnp.where(first, 0.0, xbi_ref[0:1, :]), pltpu.roll(xi_ref[0:8, :], 7, axis=0))
        else:
            body_x, body_l, edge_l = slice(0, T - 8), slice(8, T), slice(0, 8)
            sp_r = jnp.where(row8 == 0, jnp.where(first, 0.0, xbr_ref[7:8, :]), pltpu.roll(xr_ref[T - 8:T, :], 1, axis=0))
            sp_i = jnp.where(row8 == 0, jnp.where(first, 0.0, xbi_ref[7:8, :]), pltpu.roll(xi_ref[T - 8:T, :], 1, axis=0))
        xpr, xpi = xr_ref[body_x, :], xi_ref[body_x, :]
        lr, li = lam[0, body_l, :], lam[1, body_l, :]
        er, ei = lam[0, edge_l, :], lam[1, edge_l, :]
        dar_ref[...] += jnp.sum(xpr * lr + xpi * li, axis=0, keepdims=True) + jnp.sum(sp_r * er + sp_i * ei, axis=0, keepdims=True)
        dai_ref[...] += jnp.sum(xpr * li - xpi * lr, axis=0, keepdims=True) + jnp.sum(sp_r * ei - sp_i * er, axis=0, keepdims=True)

    mat = lambda shp: pl.BlockSpec((1, 1) + shp, lambda j, b: (d, j, 0, 0))
    omat = lambda shp: pl.BlockSpec((1, 1) + shp, lambda j, b: (0, j, 0, 0))
    vec = lambda r: pl.BlockSpec((r, 512), lambda j, b: (0, d * S5_NJ + j))
    blk = lambda w: pl.BlockSpec((T, w), lambda j, b: (tb(b), j))
    return pl.pallas_call(
        body, grid=(S5_NJ, NB),
        in_specs=[blk(128), blk(128), blk(512), blk(512),
                  pl.BlockSpec((8, 512), lambda j, b: (bnd(b), j)), pl.BlockSpec((8, 512), lambda j, b: (bnd(b), j)),
                  mat((128, 512)), mat((128, 512)), mat((512, 128)), mat((512, 128)), vec(1), vec(1), vec(L), vec(L)],
        out_specs=[blk(128), omat((128, 512)), omat((128, 512)), omat((128, 512)), omat((128, 512)),
                   pl.BlockSpec((1, 512), lambda j, b: (0, j)), pl.BlockSpec((1, 512), lambda j, b: (0, j))],
        out_shape=[SDS((S, 1024), f32), SDS((1, 8, 128, 512), f32), SDS((1, 8, 128, 512), f32), SDS((1, 8, 128, 512), f32),
                   SDS((1, 8, 128, 512), f32), SDS((1, 4096), f32), SDS((1, 4096), f32)],
        scratch_shapes=[pltpu.VMEM((2, 8, 512), f32), pltpu.VMEM((2, T, 512), f32)],
        compiler_params=_params(("parallel", "arbitrary"), VMEM_BIG), name=name)(
            u, dy, xr, xi, xr, xi, BBr, BBi, CCr, CCi, abr, abi, pwr, pwi)


def _silu_mul(g, u):
    return jax.nn.silu(g) * u


def _mixf(p0, p1, p2, z0, z1, z2):
    return jax.nn.sigmoid(z0) * p0 + jax.nn.sigmoid(z1) * p1 + jax.nn.sigmoid(z2) * p2


def _s5_act(yf, yb, u, dd):
    return jax.nn.gelu(yf + yb + dd * u)


def _glu(a, b):
    return a * jax.nn.sigmoid(b)


def _layer_fwd(x, w, tabs, l):
    S = x.shape[0]
    T = min(S, 512)
    I = S // T
    nm = lambda s: f"L{l}_{s}"
    sv = {'x': x}
    h = _rmsnorm_fwd(x, w['norm1_g'], name=nm("norm1"))
    zr = _mm(h, w['W_ret'], name=nm("in_ret"))
    zm = _mm(h, w['W_mla'], name=nm("in_mla"))
    h_seg = _perm_rows(h, tabs['seg_perm'], name=nm("s5_perm_h"))
    zs = _mm(h_seg, w['W_s5'], name=nm("in_s5"))
    zg = _mm(h, w['W_gate'], out_dtype=bf16, name=nm("in_gate"))
    sv.update(h=h, h_seg=h_seg, zr=zr, zm=zm, zs=zs, zg=zg)

    yf, stf = _ret_dir_fwd(zr, w['lg'], tabs['cos_r'], tabs['sin_r'], reverse=False, name=nm("ret_f"))
    yb, stb = _ret_dir_fwd(zr, w['lg'], tabs['cos_r'], tabs['sin_r'], reverse=True, name=nm("ret_b"))
    hd = lambda j: j
    y_ret = _pw(_gn_gate, [yf, yb, zr, w['ret_gn_g']],
                [_row(T, 256, hd), _row(T, 256, hd), _row(T, 256, lambda j: 8 + j), _par(256, hd)],
                [SDS((S, 1024), bf16)], [_row(T, 256, hd)], (RET_HEADS, I), name=nm("ret_gn"))[0]
    sv.update(yf=yf, yb=yb, stf=stf, stb=stb, y_ret=y_ret)

    cqn, ckvn = _pw(lambda a, b, g1, g2: (_rms(a, g1), _rms(b, g2)), [zm, zm, w['mla_q_norm_g'], w['mla_kv_norm_g']],
                    [_row(T, 384), _row(T, 256, lambda j: 2), _par(384), _par(256)],
                    [SDS((S, 384), bf16), SDS((S, 256), bf16)], [_row(T, 384), _row(T, 256)], (1, I), name=nm("mla_norm"))
    q = _mm(cqn, w['W_uq'], out_dtype=bf16, name=nm("mla_uq"))
    kv = _mm(ckvn, w['W_ukv'], out_dtype=bf16, name=nm("mla_ukv"))
    Q, K = _mla_qk_prep(q, kv, zm, tabs['cos_m'], tabs['sin_m'], name=nm("mla_qkprep"))
    O, Lse = _flash_fwd(Q, K, kv, name=nm("mla_attn"))
    sv.update(cqn=cqn, ckvn=ckvn, kv=kv, Q=Q, K=K, O=O, Lse=Lse)

    s5 = w['s5']
    ysf, xrf, xif = _s5_scan_fwd(zs, s5['BBr'], s5['BBi'], s5['CCr'], s5['CCi'], s5['abr'], s5['abi'], s5['pwr_f'], s5['pwi_f'],
                                 reverse=False, name=nm("s5_f"))
    ysb, xrb, xib = _s5_scan_fwd(zs, s5['BBr'], s5['BBi'], s5['CCr'], s5['CCi'], s5['abr'], s5['abi'], s5['pwr_f'], s5['pwi_f'],
                                 reverse=True, name=nm("s5_b"))
    gact = _pw(_s5_act, [ysf, ysb, zs, w['s5_d']], [_row(T, D), _row(T, D), _row(T, D), _par(D)],
               [SDS((S, D), bf16)], [_row(T, D)], (1, I), name=nm("s5_act"))[0]
    gg = _mm(gact, w['W_glu'], out_dtype=bf16, name=nm("s5_glu_mm"))
    y_s5 = _pw(_glu, [gg, gg], [_row(T, D), _row(T, D, lambda j: 1)], [SDS((S, D), bf16)], [_row(T, D)], (1, I),
               name=nm("s5_glu"))[0]
    y_s5 = _perm_rows(y_s5, tabs['seg_unperm'], name=nm("s5_unperm_y"))
    sv.update(ysf=ysf, ysb=ysb, xrf=xrf, xif=xif, xrb=xrb, xib=xib, gact=gact, gg=gg, y_s5=y_s5)

    ys = [y_ret, O, y_s5]
    pr = [_mm(ys[i], w['W_br'][i], out_dtype=bf16, name=nm(f"branch{i}")) for i in range(3)]
    mix = _pw(_mixf, pr + [zg, zg, zg],
              [_row(T, D)] * 3 + [_row(T, D), _row(T, D, lambda j: 1), _row(T, D, lambda j: 2)],
              [SDS((S, D), bf16)], [_row(T, D)], (1, I), name=nm("mix"))[0]
    x1 = _mm(mix, w['W_out'], res=x, name=nm("out_proj"))
    h2 = _rmsnorm_fwd(x1, w['norm2_g'], name=nm("norm2"))
    fgu = _mm(h2, w['W_gu'], out_dtype=bf16, name=nm("ffn_gu"))
    act = _pw(_silu_mul, [fgu, fgu], [_row(T, 1408, lambda j: j), _row(T, 1408, lambda j: 2 + j)],
              [SDS((S, FFN_H), bf16)], [_row(T, 1408, lambda j: j)], (2, I), name=nm("ffn_act"))[0]
    x2 = _mm(act, w['W_down'], res=x1, name=nm("ffn_down"))
    sv.update(pr=pr, mix=mix, x1=x1, h2=h2, fgu=fgu, act=act)
    return x2, sv


def _vjp_fn(fn, n_primal, cast=None):
    def g(*args):
        _, vjp = jax.vjp(fn, *args[:n_primal])
        return vjp(args[n_primal].astype(f32))
    return g


def _layer_bwd(dx2, w, tabs, sv, l):
    S = dx2.shape[0]
    T = min(S, 512)
    I = S // T
    nm = lambda s: f"L{l}_b_{s}"
    g = {}
    hd = lambda j: j

    dact = _mm(dx2, w['W_down'], tb=True, out_dtype=bf16, name=nm("ffn_down_dx"))
    g['W_down'] = _mmT(sv['act'], dx2, name=nm("ffn_down_dw"))
    dfg, dfu = _pw(_vjp_fn(_silu_mul, 2), [sv['fgu'], sv['fgu'], dact],
                   [_row(T, 1408, lambda j: j), _row(T, 1408, lambda j: 2 + j), _row(T, 1408, lambda j: j)],
                   [SDS((S, FFN_H), bf16), SDS((S, FFN_H), bf16)], [_row(T, 1408, lambda j: j)] * 2, (2, I), name=nm("ffn_act"))
    dfgu = jnp.concatenate([dfg, dfu], axis=1)
    g['W_gu'] = _mmT(sv['h2'], dfgu, name=nm("ffn_gu_dw"))
    dh2 = _mm(dfgu, w['W_gu'], tb=True, name=nm("ffn_gu_dx"))
    dx1, g['norm2_g'] = _rmsnorm_bwd(sv['x1'], w['norm2_g'], dh2, dx2, name=nm("norm2"))

    dmix = _mm(dx1, w['W_out'], tb=True, out_dtype=bf16, name=nm("out_dx"))
    g['W_out'] = _mmT(sv['mix'], dx1, name=nm("out_dw"))
    zg = sv['zg']
    outs = _pw(_vjp_fn(_mixf, 6), sv['pr'] + [zg, zg, zg, dmix],
               [_row(T, D)] * 3 + [_row(T, D), _row(T, D, lambda j: 1), _row(T, D, lambda j: 2), _row(T, D)],
               [SDS((S, D), bf16)] * 6, [_row(T, D)] * 6, (1, I), name=nm("mix"))
    dpr, dzg = outs[:3], jnp.concatenate(outs[3:], axis=1)
    ys = [sv['y_ret'], sv['O'], sv['y_s5']]
    g['W_br'] = [_mmT(ys[i], dpr[i], name=nm(f"branch{i}_dw")) for i in range(3)]
    dpr_seg = _perm_rows(dpr[2], tabs['seg_perm'], name=nm("s5_perm_dy"))
    dys = [_mm(dpr[i] if i < 2 else dpr_seg, w['W_br'][i], tb=True, out_dtype=bf16,
               name=nm(f"branch{i}_dx")) for i in range(3)]

    gg = sv['gg']
    dga, dgb = _pw(_vjp_fn(_glu, 2), [gg, gg, dys[2]], [_row(T, D), _row(T, D, lambda j: 1), _row(T, D)],
                   [SDS((S, D), bf16)] * 2, [_row(T, D)] * 2, (1, I), name=nm("s5_glu"))
    dgg = jnp.concatenate([dga, dgb], axis=1)
    g['W_glu'] = _mmT(sv['gact'], dgg, name=nm("s5_glu_dw"))
    dgact = _mm(dgg, w['W_glu'], tb=True, out_dtype=bf16, name=nm("s5_glu_dx"))

    def act_bwd(yf, yb, u, dd, ct):
        _, vjp = jax.vjp(_s5_act, yf, yb, u, dd)
        dyf, _, du, ddd = vjp(ct)
        return dyf, du, ddd

    dys5, du_direct, g['s5_d'] = _pw(act_bwd, [sv['ysf'], sv['ysb'], sv['zs'], w['s5_d'], dgact],
                                     [_row(T, D)] * 3 + [_par(D), _row(T, D)],
                                     [SDS((S, D), bf16), SDS((S, D), f32), SDS((1, D), f32)],
                                     [_row(T, D), _row(T, D), _par(D)], (1, I), n_acc=1, name=nm("s5_act"))
    s5 = w['s5']
    rf = _s5_scan_bwd(sv['zs'], dys5, sv['xrf'], sv['xif'], s5['BBr'], s5['BBi'], s5['CCr'], s5['CCi'], s5['abr'], s5['abi'],
                      s5['pwr_a'], s5['pwi_a'], reverse=False, name=nm("s5_f"))
    rb = _s5_scan_bwd(sv['zs'], dys5, sv['xrb'], sv['xib'], s5['BBr'], s5['BBi'], s5['CCr'], s5['CCi'], s5['abr'], s5['abi'],
                      s5['pwr_a'], s5['pwi_a'], reverse=True, name=nm("s5_b"))
    g['s5'] = (rf[1:], rb[1:])
    dzs_seg = _pw(lambda a, b, c: a + b + c, [du_direct, rf[0], rb[0]], [_row(T, D)] * 3, [SDS((S, D), bf16)], [_row(T, D)],
                  (1, I), name=nm("s5_du"))[0]
    dzs = _perm_rows(dzs_seg, tabs['seg_unperm'], name=nm("s5_unperm_dz"))

    delta = _attn_delta(dys[1], sv['O'], name=nm("mla_delta"))
    dQ, dK, dV = _flash_bwd(sv['Q'], sv['K'], sv['kv'], delta, sv['Lse'], dys[1], name=nm("mla_attn"))
    dq_lin, dkv, dkr = _mla_bwd_prep(dQ, dK, dV, tabs['cos_m'], tabs['sin_m'], name=nm("mla_prep"))
    g['W_uq'] = _mmT(sv['cqn'], dq_lin, name=nm("mla_uq_dw"))
    dcqn = _mm(dq_lin, w['W_uq'], tb=True, name=nm("mla_uq_dx"))
    g['W_ukv'] = _mmT(sv['ckvn'], dkv, name=nm("mla_ukv_dw"))
    dckvn = _mm(dkv, w['W_ukv'], tb=True, name=nm("mla_ukv_dx"))
    dzm, g['mla_q_norm_g'], g['mla_kv_norm_g'] = _mla_norm_bwd(sv['zm'], w['mla_q_norm_g'], w['mla_kv_norm_g'], dcqn, dckvn, dkr,
                                                               name=nm("mla_norm"))

    zr = sv['zr']

    def gn_bwd(yf, yb, gt, gn, ct):
        _, vjp = jax.vjp(_gn_gate, yf, yb, gt, gn)
        dyf, _, dgt, dgn = vjp(ct)
        return dyf, dgt, dgn

    dyr, dgate, g['ret_gn_g'] = _pw(gn_bwd, [sv['yf'], sv['yb'], zr, w['ret_gn_g'], dys[0]],
                                    [_row(T, 256, hd), _row(T, 256, hd), _row(T, 256, lambda j: 8 + j), _par(256, hd),
                                     _row(T, 256, hd)],
                                    [SDS((S, 1024), bf16), SDS((S, 1024), bf16), SDS((1, 1024), f32)],
                                    [_row(T, 256, hd), _row(T, 256, hd), _par(256, hd)], (RET_HEADS, I), n_acc=1, name=nm("ret_gn"))
    qf, kf, vf, lgf = _ret_dir_bwd(zr, w['lg'], tabs['cos_r'], tabs['sin_r'], dyr, sv['stf'], reverse=False, name=nm("ret_f"))
    qb, kb, vb, lgb = _ret_dir_bwd(zr, w['lg'], tabs['cos_r'], tabs['sin_r'], dyr, sv['stb'], reverse=True, name=nm("ret_b"))
    g['lg'] = jnp.stack([lgf[:, 0, 0], lgb[:, 0, 0]])
    add2 = lambda a, b: a + b
    dq = _pw(add2, [qf, qb], [_row(T, 512)] * 2, [SDS((S, 512), bf16)], [_row(T, 512)], (1, I), name=nm("ret_dq"))[0]
    dk = _pw(add2, [kf, kb], [_row(T, 512)] * 2, [SDS((S, 512), bf16)], [_row(T, 512)], (1, I), name=nm("ret_dk"))[0]
    dv = _pw(add2, [vf, vb], [_row(T, D)] * 2, [SDS((S, D), bf16)], [_row(T, D)], (1, I), name=nm("ret_dv"))[0]
    dzr = jnp.concatenate([dq, dk, dv, dgate], axis=1)

    h = sv['h']
    g['W_ret'] = _mmT(h, dzr, name=nm("in_ret_dw"))
    g['W_mla'] = _mmT(h, dzm, name=nm("in_mla_dw"))
    g['W_s5'] = _mmT(sv['h_seg'], dzs_seg, name=nm("in_s5_dw"))
    g['W_gate'] = _mmT(h, dzg, name=nm("in_gate_dw"))
    dh = _mm(dzr, w['W_ret'], tb=True, name=nm("in_ret_dx"))
    dh = _mm(dzm, w['W_mla'], tb=True, res=dh, name=nm("in_mla_dx"))
    dh = _mm(dzs, w['W_s5'], tb=True, res=dh, name=nm("in_s5_dx"))
    dh = _mm(dzg, w['W_gate'], tb=True, res=dh, name=nm("in_gate_dx"))
    dx, g['norm1_g'] = _rmsnorm_bwd(sv['x'], w['norm1_g'], dh, dx1, name=nm("norm1"))
    return dx, g


def _loss_head(x, tgt, gain, *, name):
    S, W = x.shape
    T = min(S, 512)

    def loss_fn(xv, gv, tv):
        return 0.5 * jnp.sum(jnp.mean(jnp.square(_rms(xv, gv) - tv), axis=-1, keepdims=True), axis=0, keepdims=True)

    def fn(xv, gv, tv):
        lv, vjp = jax.vjp(lambda a, b: loss_fn(a, b, tv), xv, gv)
        dx, dg = vjp(jnp.ones((1, 1), f32))
        return dx, jnp.broadcast_to(lv, (1, 128)), dg

    return _pw(fn, [x, gain, tgt], [_row(T, W), _par(W), _row(T, W)],
               [SDS((S, W), f32), SDS((1, 128), f32), SDS((1, W), f32)], [_row(T, W), _par(128), _par(W)],
               (1, S // T), n_acc=2, name=name)


def _rope_tabs(S):
    def tab(dim):
        inv = 1.0 / (ROPE_THETA ** (jnp.arange(0, dim, 2, dtype=f32) / dim))
        ang = jnp.arange(S, dtype=f32)[:, None] * inv[None, :]
        return jnp.cos(ang), jnp.sin(ang)

    cr, sr = tab(RET_DK)
    cm, sm = tab(MLA_ROPE)
    z = jnp.zeros((S, 64), f32)
    return {'cos_r': jnp.concatenate([cr, cr], axis=1), 'sin_r': jnp.concatenate([-sr, sr], axis=1),
            'cos_m': jnp.concatenate([cm, cm, z], axis=1), 'sin_m': jnp.concatenate([-sm, sm, z], axis=1),
            'seg_perm': _seg_perm(512, False), 'seg_unperm': _seg_perm(512, True)}


def _bd_B(bb):
    b5 = bb.reshape(16, 2, 8, 8, 64)
    return jnp.einsum('cdjgp,gh->djgchp', b5, jnp.eye(8, dtype=bb.dtype)).reshape(2, 8, 128, 512)


def _bd_B_t(dBB):
    return jnp.einsum('djgcgp->cdjgp', dBB.reshape(2, 8, 8, 16, 8, 64)).reshape(16, 8192)


def _bd_C(c):
    c5 = c.reshape(2, 8, 8, 16, 64)
    return jnp.einsum('djgcp,gh->djgphc', c5, jnp.eye(8, dtype=c.dtype)).reshape(2, 8, 512, 128)


def _s5_rows(p, l):
    a_re = p['s5_a_re'][l].reshape(1, 8192)
    a_im = p['s5_a_im'][l].reshape(1, 8192)
    ldt = jnp.broadcast_to(p['s5_log_dt'][l][:, :, None], (2, S5_G, S5_P)).reshape(1, 8192)
    b_re = p['s5_b_re'][l].transpose(3, 0, 1, 2).reshape(16, 8192)
    b_im = p['s5_b_im'][l].transpose(3, 0, 1, 2).reshape(16, 8192)
    return a_re, a_im, ldt, b_re, b_im


def _layer_weights(big, p, l):
    w_in = big['w_in'][l]
    z = lambda n: jnp.zeros((D, n), w_in.dtype)
    w = {
        'W_ret': w_in[:, 0:3072],
        'W_mla': jnp.concatenate([w_in[:, 3072:3456], z(128), w_in[:, 3456:3712], w_in[:, 3712:3776], z(64)], axis=1),
        'W_s5': w_in[:, 3776:4800],
        'W_gate': w_in[:, 4800:7872],
        'W_uq': jnp.pad(big['mla_w_uq'][l].reshape(MLA_Q_LORA, MLA_HEADS, 192), ((0, 0), (0, 0), (0, 64))).reshape(MLA_Q_LORA, 2048),
        'W_ukv': big['mla_w_ukv'][l],
        'W_glu': big['s5_w_glu'][l],
        'W_br': [big['w_branch'][l, i] for i in range(3)],
        'W_out': big['w_out'][l],
        'W_gu': big['ffn_w_gu'][l],
        'W_down': big['ffn_w_down'][l],
    }
    for n in ('norm1_g', 'ret_gn_g', 'mla_q_norm_g', 'mla_kv_norm_g', 's5_d', 'norm2_g'):
        w[n] = p[n][l][None, :]
    w['lg'] = jax.nn.log_sigmoid(p['ret_decay'][l])
    rows = _s5_rows(p, l)
    abr, abi, bbr, bbi, pwr, pwi = _s5_param_fwd(*rows, name=f"L{l}_s5_param")
    flip = lambda t, first: jnp.concatenate([t[::-1, :4096], t[:, 4096:]] if first else [t[:, :4096], t[::-1, 4096:]], axis=1)
    w['s5'] = {'abr': abr, 'abi': abi, 'BBr': _bd_B(bbr).astype(bf16), 'BBi': _bd_B(bbi).astype(bf16),
               'CCr': _bd_C(p['s5_c_re'][l]).astype(bf16), 'CCi': _bd_C(p['s5_c_im'][l]).astype(bf16),
               'pwr_f': flip(pwr, False), 'pwi_f': flip(pwi, False), 'pwr_a': flip(pwr, True), 'pwi_a': flip(pwi, True),
               'rows': rows}
    return w


def _layer_grads(g, w, p, l):
    out = {}
    m = g['W_mla']
    out['w_in'] = jnp.concatenate([g['W_ret'], m[:, 0:384], m[:, 512:768], m[:, 768:832], g['W_s5'], g['W_gate']], axis=1)
    out['mla_w_uq'] = g['W_uq'].reshape(MLA_Q_LORA, MLA_HEADS, 256)[:, :, :192].reshape(MLA_Q_LORA, 1536)
    out['mla_w_ukv'] = g['W_ukv']
    out['s5_w_glu'] = g['W_glu']
    out['w_branch'] = jnp.stack(g['W_br'])
    out['w_out'] = g['W_out']
    out['ffn_w_gu'] = g['W_gu']
    out['ffn_w_down'] = g['W_down']
    for n in ('norm1_g', 'ret_gn_g', 'mla_q_norm_g', 'mla_kv_norm_g', 's5_d', 'norm2_g'):
        out[n] = g[n][0]
    out['ret_decay'] = g['lg'] * jax.nn.sigmoid(-p['ret_decay'][l])
    (fB_r, fB_i, fC_r, fC_i, fa_r, fa_i), (bB_r, bB_i, bC_r, bC_i, ba_r, ba_i) = g['s5']
    cat = lambda a, b: jnp.concatenate([a, b], axis=0)
    d_bbr = _bd_B_t(cat(fB_r, bB_r))
    d_bbi = _bd_B_t(cat(fB_i, bB_i))
    to_c = lambda t: _bd_B_t(t).reshape(16, 2, S5_G, S5_P).transpose(1, 2, 0, 3)
    out['s5_c_re'] = to_c(cat(fC_r, bC_r))
    out['s5_c_im'] = to_c(cat(fC_i, bC_i))
    d_abr = jnp.concatenate([fa_r, ba_r], axis=1)
    d_abi = jnp.concatenate([fa_i, ba_i], axis=1)
    da_re, da_im, dldt, db_re, db_im = _s5_param_bwd(*w['s5']['rows'], d_abr, d_abi, d_bbr, d_bbi, name=f"L{l}_b_s5_param")
    out['s5_a_re'] = da_re.reshape(2, S5_G, S5_P)
    out['s5_a_im'] = da_im.reshape(2, S5_G, S5_P)
    out['s5_log_dt'] = dldt.reshape(2, S5_G, S5_P).sum(axis=-1)
    out['s5_b_re'] = db_re.reshape(16, 2, S5_G, S5_P).transpose(1, 2, 3, 0)
    out['s5_b_im'] = db_im.reshape(16, 2, S5_G, S5_P).transpose(1, 2, 3, 0)
    return out


def _local_step(x, tgt, big, p):
    S = x.shape[0]
    assert S % 512 == 0
    tabs = _rope_tabs(S)
    ws, svs = [], []
    h = x
    for l in range(DEPTH):
        w = _layer_weights(big, p, l)
        h, sv = _layer_fwd(h, w, tabs, l)
        ws.append(w)
        svs.append(sv)
    dx, lossv, dfinal = _loss_head(h, tgt, p['final_g'][None, :], name="loss_head")
    per_layer = [None] * DEPTH
    for l in reversed(range(DEPTH)):
        dx, g = _layer_bwd(dx, ws[l], tabs, svs[l], l)
        per_layer[l] = _layer_grads(g, ws[l], p, l)
    return lossv[0, 0], dx, per_layer, dfinal[0]


_ANY = pl.BlockSpec(memory_space=pl.ANY)


def _place():
    x, y, c = lax.axis_index("x"), lax.axis_index("y"), lax.axis_index("c")
    return x, y, c, [(1 - x, y), (x, 1 - y), (1 - x, 1 - y)]


def _allgather4(arrs, *, name):
    n = len(arrs)

    def body(*refs):
        ins, outs = refs[:n], refs[n:2 * n]
        send, recv, loc = refs[2 * n:]
        x, y, c, chips = _place()
        me = 2 * x + y

        def remote(a, k, slot):
            px, py = chips[k]
            return pltpu.make_async_remote_copy(src_ref=ins[a], dst_ref=outs[a].at[slot], send_sem=send.at[a, k],
                                                recv_sem=recv.at[a, k], device_id=(px, py, c), device_id_type=MESH)

        mine = [pltpu.make_async_copy(ins[a], outs[a].at[me], loc.at[a]) for a in range(n)]
        for cp in mine:
            cp.start()
        sends = [remote(a, k, me) for a in range(n) for k in range(3)]
        for cp in sends:
            cp.start()
        for a in range(n):
            for k, (px, py) in enumerate(chips):
                remote(a, k, 2 * px + py).wait_recv()
        for cp in sends:
            cp.wait_send()
        for cp in mine:
            cp.wait()

    return pl.pallas_call(
        body, in_specs=[_ANY] * n, out_specs=[_ANY] * n, out_shape=[SDS((4,) + a.shape, a.dtype) for a in arrs],
        scratch_shapes=[pltpu.SemaphoreType.DMA((n, 3)), pltpu.SemaphoreType.DMA((n, 3)), pltpu.SemaphoreType.DMA((n,))],
        name=name)(*arrs)


def _rs_exchange(parts, *, name):
    n = len(parts)

    def body(*refs):
        ins, gots = refs[:n], refs[n:2 * n]
        send, recv = refs[2 * n:]
        x, y, c, chips = _place()

        def remote(a, k):
            px, py = chips[k]
            return pltpu.make_async_remote_copy(src_ref=ins[a].at[2 * px + py], dst_ref=gots[a].at[k], send_sem=send.at[a, k],
                                                recv_sem=recv.at[a, k], device_id=(px, py, c), device_id_type=MESH)

        sends = [remote(a, k) for a in range(n) for k in range(3)]
        for cp in sends:
            cp.start()
        for cp in sends:
            cp.wait_recv()
        for cp in sends:
            cp.wait_send()

    return pl.pallas_call(
        body, in_specs=[_ANY] * n, out_specs=[_ANY] * n, out_shape=[SDS((3,) + a.shape[1:], a.dtype) for a in parts],
        scratch_shapes=[pltpu.SemaphoreType.DMA((n, 3)), pltpu.SemaphoreType.DMA((n, 3))], name=name)(*parts)


def _gather_split(arrs, *, name):
    n = len(arrs)

    def body(*refs):
        ins, outs = refs[:n], refs[n:2 * n]
        s_ici, r_ici, s_sib, r_sib = refs[2 * n:]
        x, y, c, chips = _place()
        me = 2 * x + y
        ids = [2 * px + py for px, py in chips] + [me]

        def over_ici(a, k, slot):
            px, py = chips[k]
            return pltpu.make_async_remote_copy(src_ref=ins[a].at[c], dst_ref=outs[a].at[slot, c], send_sem=s_ici.at[a, k],
                                                recv_sem=r_ici.at[a, k], device_id=(px, py, c), device_id_type=MESH)

        def to_sibling(a, k, half, src=None):
            blk = outs[a].at[ids[k], half]
            return pltpu.make_async_remote_copy(src_ref=blk if src is None else src, dst_ref=blk, send_sem=s_sib.at[a, k],
                                                recv_sem=r_sib.at[a, k], device_id=(x, y, 1 - c), device_id_type=MESH)

        sends = [over_ici(a, k, me) for a in range(n) for k in range(3)]
        sends += [to_sibling(a, 3, c, src=ins[a].at[c]) for a in range(n)]
        for cp in sends:
            cp.start()
        for a in range(n):
            for k in range(3):
                over_ici(a, k, ids[k]).wait_recv()
                fwd = to_sibling(a, k, c)
                fwd.start()
                sends.append(fwd)
        for a in range(n):
            for k in range(4):
                to_sibling(a, k, 1 - c).wait_recv()
        for cp in sends:
            cp.wait_send()

    dma = pltpu.SemaphoreType.DMA
    return pl.pallas_call(
        body, in_specs=[_ANY] * n, out_specs=[_ANY] * n, out_shape=[SDS((4,) + a.shape, a.dtype) for a in arrs],
        scratch_shapes=[dma((n, 3)), dma((n, 3)), dma((n, 4)), dma((n, 4))], name=name)(*arrs)


def _swap_halves(parts, *, name):
    n = len(parts)

    def body(*refs):
        ins, gots = refs[:n], refs[n:2 * n]
        send, recv = refs[2 * n:]
        x, y, c, _ = _place()
        cps = [pltpu.make_async_remote_copy(src_ref=ins[a].at[q, 1 - c], dst_ref=gots[a].at[q], send_sem=send.at[a, q],
                                            recv_sem=recv.at[a, q], device_id=(x, y, 1 - c), device_id_type=MESH)
               for a in range(n) for q in range(4)]
        for cp in cps:
            cp.start()
        for cp in cps:
            cp.wait_recv()
        for cp in cps:
            cp.wait_send()

    dma = pltpu.SemaphoreType.DMA
    return pl.pallas_call(
        body, in_specs=[_ANY] * n, out_specs=[_ANY] * n, out_shape=[SDS((4,) + a.shape[2:], a.dtype) for a in parts],
        scratch_shapes=[dma((n, 4)), dma((n, 4))], name=name)(*parts)


def _sibling_copy(arrs, *, name):
    n = len(arrs)

    def body(*refs):
        ins, outs = refs[:n], refs[n:2 * n]
        send, recv = refs[2 * n:]
        x, y, c, _ = _place()
        cps = [pltpu.make_async_remote_copy(src_ref=ins[a], dst_ref=outs[a], send_sem=send.at[a], recv_sem=recv.at[a],
                                            device_id=(x, y, 1 - c), device_id_type=MESH) for a in range(n)]
        for cp in cps:
            cp.start()
        for cp in cps:
            cp.wait_recv()
        for cp in cps:
            cp.wait_send()

    dma = pltpu.SemaphoreType.DMA
    return pl.pallas_call(
        body, in_specs=[_ANY] * n, out_specs=[_ANY] * n, out_shape=[SDS(a.shape, a.dtype) for a in arrs],
        scratch_shapes=[dma((n,)), dma((n,))], name=name)(*arrs)


def _row_tile(R):
    return R if R <= 256 else next(t for t in (256, 128, 64, 32, 16) if R % t == 0)


def _add2(a, b, *, name):
    R, W = a.shape
    tr = _row_tile(R)
    return _pw(lambda p, q: p.astype(f32) + q.astype(f32), [a, b], [_row(tr, W)] * 2, [SDS((R, W), a.dtype)], [_row(tr, W)],
               (1, R // tr), name=name)[0]


def _sum4(own, got, *, name):
    R, W = own.shape
    tr = _row_tile(R)
    g3 = lambda k: pl.BlockSpec((None, tr, W), lambda j, i: (k, i, 0))
    up = lambda t: t.astype(f32)
    return _pw(lambda a, b, c, d: ((up(a) + up(b)) + up(c)) + up(d), [own, got, got, got], [_row(tr, W), g3(0), g3(1), g3(2)],
               [SDS((R, W), f32)], [_row(tr, W)], (1, R // tr), name=name)[0]


def _adamw(g, w, m, v, *, name):
    R, W = w.shape
    tr = _row_tile(R)

    def fn(gv, wv, mv, vv):
        m2 = ADAM_B1 * mv + (1.0 - ADAM_B1) * gv
        v2 = ADAM_B2 * vv + (1.0 - ADAM_B2) * jnp.square(gv)
        m_hat = m2 / (1.0 - ADAM_B1 ** ADAM_STEP)
        v_hat = v2 / (1.0 - ADAM_B2 ** ADAM_STEP)
        return -ADAM_LR * (m_hat / (jnp.sqrt(v_hat) + ADAM_EPS) + ADAM_WD * wv), m2, v2

    return _pw(fn, [g, w, m, v], [_row(tr, W)] * 4, [SDS((R, W), f32)] * 3, [_row(tr, W)] * 3, (1, R // tr), name=name)


def _to_parts(g, axis):
    shp = g.shape
    g = g.reshape(shp[:axis] + (4, shp[axis] // 4) + shp[axis + 1:])
    return jnp.moveaxis(g, axis, 0)


def _from_parts(pt, axis):
    g = jnp.moveaxis(pt, 0, axis)
    shp = g.shape
    return g.reshape(shp[:axis] + (4 * shp[axis + 1],) + shp[axis + 2:])


def kernel(x, norm1_g, w_in, ret_decay, ret_gn_g, mla_q_norm_g, mla_w_uq, mla_kv_norm_g, mla_w_ukv, s5_a_re, s5_a_im, s5_log_dt, s5_b_re, s5_b_im, s5_c_re, s5_c_im, s5_d, s5_w_glu, w_branch, w_out, norm2_g, ffn_w_gu, ffn_w_down, final_g, loss_target, m_norm1_g, m_w_in, m_ret_decay, m_ret_gn_g, m_mla_q_norm_g, m_mla_w_uq, m_mla_kv_norm_g, m_mla_w_ukv, m_s5_a_re, m_s5_a_im, m_s5_log_dt, m_s5_b_re, m_s5_b_im, m_s5_c_re, m_s5_c_im, m_s5_d, m_s5_w_glu, m_w_branch, m_w_out, m_norm2_g, m_ffn_w_gu, m_ffn_w_down, m_final_g, v_norm1_g, v_w_in, v_ret_decay, v_ret_gn_g, v_mla_q_norm_g, v_mla_w_uq, v_mla_kv_norm_g, v_mla_w_ukv, v_s5_a_re, v_s5_a_im, v_s5_log_dt, v_s5_b_re, v_s5_b_im, v_s5_c_re, v_s5_c_im, v_s5_d, v_s5_w_glu, v_w_branch, v_w_out, v_norm2_g, v_ffn_w_gu, v_ffn_w_down, v_final_g):
    wv = dict(zip(W_NAMES, (norm1_g, w_in, ret_decay, ret_gn_g, mla_q_norm_g, mla_w_uq, mla_kv_norm_g, mla_w_ukv, s5_a_re, s5_a_im,
                            s5_log_dt, s5_b_re, s5_b_im, s5_c_re, s5_c_im, s5_d, s5_w_glu, w_branch, w_out, norm2_g, ffn_w_gu,
                            ffn_w_down, final_g)))
    mv = dict(zip(W_NAMES, (m_norm1_g, m_w_in, m_ret_decay, m_ret_gn_g, m_mla_q_norm_g, m_mla_w_uq, m_mla_kv_norm_g, m_mla_w_ukv,
                            m_s5_a_re, m_s5_a_im, m_s5_log_dt, m_s5_b_re, m_s5_b_im, m_s5_c_re, m_s5_c_im, m_s5_d, m_s5_w_glu,
                            m_w_branch, m_w_out, m_norm2_g, m_ffn_w_gu, m_ffn_w_down, m_final_g)))
    vv = dict(zip(W_NAMES, (v_norm1_g, v_w_in, v_ret_decay, v_ret_gn_g, v_mla_q_norm_g, v_mla_w_uq, v_mla_kv_norm_g, v_mla_w_ukv,
                            v_s5_a_re, v_s5_a_im, v_s5_log_dt, v_s5_b_re, v_s5_b_im, v_s5_c_re, v_s5_c_im, v_s5_d, v_s5_w_glu,
                            v_w_branch, v_w_out, v_norm2_g, v_ffn_w_gu, v_ffn_w_down, v_final_g)))
    big_names = list(BIG)

    my_c = lax.axis_index("c")
    my_chip = 2 * lax.axis_index("x") + lax.axis_index("y")
    shards = [wv[n].astype(bf16) for n in big_names]
    gathered = _gather_split(shards, name="gather_weights")
    gathered = [lax.dynamic_update_slice(gt, lax.dynamic_index_in_dim(sh, my_c, 0)[None],
                                         (my_chip, my_c) + (0,) * (sh.ndim - 1)) for gt, sh in zip(gathered, shards)]
    big = {n: _from_parts(gt, BIG[n]) for n, gt in zip(big_names, gathered)}
    small = {n: wv[n] for n in SMALL}

    loss_local, dx, layer_grads, d_final = _local_step(x[0], loss_target[0], big, small)
    grads = {n: jnp.stack([layer_grads[l][n] for l in range(DEPTH)]) for n in SMALL if n != 'final_g'}
    grads['final_g'] = d_final

    n_rows = {n: -(-math.prod(wv[n].shape) // 1024) * 8 for n in SMALL}
    used = sum(n_rows.values())
    rows_q = -(-(used + 8) // (4 * 128)) * 128

    def as_rows(d, tail=None):
        blocks = [jnp.pad(d[n].reshape(-1), (0, n_rows[n] * 128 - math.prod(wv[n].shape))).reshape(n_rows[n], 128) for n in SMALL]
        blocks.append(jnp.zeros((8, 128), f32) if tail is None else tail)
        blocks.append(jnp.zeros((4 * rows_q - used - 8, 128), f32))
        return jnp.concatenate(blocks, axis=0)

    loss_rows = jnp.full((8, 128), loss_local, f32)
    parts = [jnp.stack([_to_parts(layer_grads[l][n].astype(bf16), BIG[n] - 1) for l in range(DEPTH)], axis=1) for n in big_names]
    parts.append(as_rows(grads, loss_rows).reshape(4, 2, rows_q // 2, 128))
    n_arr = len(parts)
    two_d = lambda a: a.reshape(-1, a.shape[-1])
    theirs = _swap_halves(parts, name="grad_swap_halves")
    mine = [lax.dynamic_index_in_dim(p, my_c, 1, keepdims=False) for p in parts]
    chip_sums = [_add2(two_d(mine[a]), two_d(theirs[a]), name=f"grad_add2_{a}").reshape(theirs[a].shape) for a in range(n_arr)]
    got = _rs_exchange(chip_sums, name="grad_exchange")
    own = [lax.dynamic_index_in_dim(s, my_chip, 0, keepdims=False) for s in chip_sums]
    sums = [_sum4(two_d(own[a]), got[a].reshape(3, -1, got[a].shape[-1]), name=f"grad_sum4_{a}") for a in range(n_arr)]
    other = _sibling_copy(sums, name="grad_sibling")
    full = [jnp.stack([jnp.where(my_c == 0, sums[a], other[a]), jnp.where(my_c == 0, other[a], sums[a])]) for a in range(n_arr)]

    out_g, out_d, out_m, out_v = {}, {}, {}, {}
    for a, n in enumerate(big_names):
        shp = wv[n].shape
        res = _adamw(two_d(full[a]), two_d(wv[n]), two_d(mv[n]), two_d(vv[n]), name=f"adamw_{n}")
        out_g[n] = full[a].reshape(shp)
        out_d[n], out_m[n], out_v[n] = [r.reshape(shp) for r in res]
    g_small = _allgather4([full[-1].reshape(rows_q, 128)], name="gather_small_grads")[0].reshape(4 * rows_q, 128)
    loss = g_small[used, 0]
    res = (g_small,) + tuple(_adamw(g_small, as_rows(wv), as_rows(mv), as_rows(vv), name="adamw_small"))
    off = 0
    for n in SMALL:
        k = math.prod(wv[n].shape)
        for dst, r in zip((out_g, out_d, out_m, out_v), res):
            dst[n] = r[off:off + n_rows[n]].reshape(-1)[:k].reshape(wv[n].shape)
        off += n_rows[n]
    return (loss, dx[None], *[out_g[n] for n in W_NAMES], *[out_d[n] for n in W_NAMES], *[out_m[n] for n in W_NAMES],
            *[out_v[n] for n in W_NAMES])
```

```python
import functools
import math

import jax
import jax.numpy as jnp
from jax import lax
from jax.experimental import pallas as pl
from jax.experimental.pallas import tpu as pltpu

f32 = jnp.float32
bf16 = jnp.bfloat16
SDS = jax.ShapeDtypeStruct
MESH = pl.DeviceIdType.MESH

D = 1024
DEPTH = 2
RMS_EPS = 1e-6
GN_EPS = 1e-5
ROPE_THETA = 10000.0
RET_HEADS = 4
RET_DK = 128
RET_DV = 256
RET_CHUNK = 128
MLA_HEADS = 8
MLA_Q_LORA = 384
MLA_KV_LORA = 256
MLA_NOPE = 128
MLA_ROPE = 64
MLA_V = 128
MLA_QW = 256
S5_G = 64
S5_P = 64
S5_C = 16
S5_NJ = 8
S5_SEG = 8
FFN_H = 2816
ADAM_LR = 0.001
ADAM_B1 = 0.9
ADAM_B2 = 0.999
ADAM_EPS = 1e-08
ADAM_WD = 0.01
ADAM_STEP = 10
VMEM_BIG = 56 * 1024 * 1024

W_NAMES = ['norm1_g', 'w_in', 'ret_decay', 'ret_gn_g', 'mla_q_norm_g', 'mla_w_uq', 'mla_kv_norm_g', 'mla_w_ukv',
           's5_a_re', 's5_a_im', 's5_log_dt', 's5_b_re', 's5_b_im', 's5_c_re', 's5_c_im', 's5_d', 's5_w_glu',
           'w_branch', 'w_out', 'norm2_g', 'ffn_w_gu', 'ffn_w_down', 'final_g']
BIG = {'w_in': 2, 'mla_w_uq': 2, 'mla_w_ukv': 2, 's5_w_glu': 2, 'w_branch': 2, 'w_out': 1, 'ffn_w_gu': 2, 'ffn_w_down': 1}
SMALL = [n for n in W_NAMES if n not in BIG]


TILE_BYTES = 6 * 1024 * 1024


def _pick(n, cands=(512, 384, 256, 128), cap=None):
    if n <= 1024 and (cap is None or n <= cap):
        return n
    for c in cands:
        if n % c == 0 and (cap is None or c <= cap):
            return c
    raise ValueError(n)


WIDE = (1408, 1024, 768, 512, 384, 256, 128)


def _params(sem, vmem=None):
    return pltpu.CompilerParams(dimension_semantics=sem, vmem_limit_bytes=vmem)


def _mm(a, b, *, tb=False, res=None, out_dtype=f32, name):
    M, K = a.shape
    N = b.shape[0] if tb else b.shape[1]
    tk = K if K <= 3072 else _pick(K, (1408, 1024, 512))
    nk = K // tk
    tn = _pick(N, WIDE, cap=TILE_BYTES // (tk * b.dtype.itemsize))
    tm = _pick(M)
    if M % 1024 == 0 and 1024 * tk * a.dtype.itemsize <= 4 * 1024 * 1024 and 1024 * tn * 4 <= TILE_BYTES:
        tm = 1024
    assert M % tm == 0 and N % tn == 0 and K % tk == 0

    def body(*refs):
        if res is None:
            a_ref, b_ref, o_ref, acc = refs
        else:
            a_ref, b_ref, r_ref, o_ref, acc = refs
        k = pl.program_id(2)
        dn = (((1,), (1 if tb else 0,)), ((), ()))
        part = lax.dot_general(a_ref[...].astype(bf16), b_ref[...].astype(bf16), dn, preferred_element_type=f32)

        @pl.when(k == 0)
        def _():
            acc[...] = part

        @pl.when(k > 0)
        def _():
            acc[...] += part

        @pl.when(k == nk - 1)
        def _():
            v = acc[...]
            if res is not None:
                v = v + r_ref[...]
            o_ref[...] = v.astype(out_dtype)

    in_specs = [pl.BlockSpec((tm, tk), lambda i, j, k: (i, k)),
                pl.BlockSpec((tn, tk), lambda i, j, k: (j, k)) if tb else pl.BlockSpec((tk, tn), lambda i, j, k: (k, j))]
    args = [a, b]
    if res is not None:
        in_specs.append(pl.BlockSpec((tm, tn), lambda i, j, k: (i, j)))
        args.append(res)
    return pl.pallas_call(
        body, grid=(M // tm, N // tn, nk), in_specs=in_specs,
        out_specs=pl.BlockSpec((tm, tn), lambda i, j, k: (i, j)),
        out_shape=SDS((M, N), out_dtype), scratch_shapes=[pltpu.VMEM((tm, tn), f32)],
        compiler_params=_params(("parallel", "parallel", "arbitrary"), VMEM_BIG), name=name)(*args)


def _mmT(a, b, *, name):
    S, M = a.shape
    N = b.shape[1]
    tn = _pick(N, WIDE)
    tm = _pick(M, WIDE, cap=TILE_BYTES // (tn * 4))
    tk = min(S, 1024)
    nk = S // tk

    def body(a_ref, b_ref, o_ref):
        k = pl.program_id(2)
        part = lax.dot_general(a_ref[...].astype(bf16), b_ref[...].astype(bf16), (((0,), (0,)), ((), ())),
                               preferred_element_type=f32)

        @pl.when(k == 0)
        def _():
            o_ref[...] = part

        @pl.when(k > 0)
        def _():
            o_ref[...] += part

    return pl.pallas_call(
        body, grid=(M // tm, N // tn, nk),
        in_specs=[pl.BlockSpec((tk, tm), lambda i, j, k: (k, i)), pl.BlockSpec((tk, tn), lambda i, j, k: (k, j))],
        out_specs=pl.BlockSpec((tm, tn), lambda i, j, k: (i, j)),
        out_shape=SDS((M, N), f32),
        compiler_params=_params(("parallel", "parallel", "arbitrary"), VMEM_BIG), name=name)(a, b)


def _pw(fn, ins, in_specs, outs, out_specs, grid, *, n_acc=0, name):
    n_in = len(ins)
    n_out = len(outs)

    def body(*refs):
        vals = fn(*[r[...].astype(f32) if r.dtype == bf16 else r[...] for r in refs[:n_in]])
        if not isinstance(vals, (tuple, list)):
            vals = (vals,)
        orefs = refs[n_in:]
        for r, v in zip(orefs[:n_out - n_acc], vals[:n_out - n_acc]):
            r[...] = v.astype(r.dtype)
        if n_acc:
            i = pl.program_id(1)

            @pl.when(i == 0)
            def _():
                for r, v in zip(orefs[n_out - n_acc:], vals[n_out - n_acc:]):
                    r[...] = v.astype(r.dtype)

            @pl.when(i > 0)
            def _():
                for r, v in zip(orefs[n_out - n_acc:], vals[n_out - n_acc:]):
                    r[...] += v.astype(r.dtype)

    res = pl.pallas_call(
        body, grid=grid, in_specs=in_specs, out_specs=out_specs, out_shape=outs,
        compiler_params=_params(("parallel", "arbitrary"), VMEM_BIG), name=name)(*ins)
    return res


def _row(T, w, col=None):
    if col is None:
        return pl.BlockSpec((T, w), lambda j, i: (i, 0))
    return pl.BlockSpec((T, w), lambda j, i: (i, col(j)))


def _par(w, col=None):
    if col is None:
        return pl.BlockSpec((1, w), lambda j, i: (0, 0))
    return pl.BlockSpec((1, w), lambda j, i: (0, col(j)))


def _rms(x, g):
    return x * lax.rsqrt(jnp.mean(x * x, axis=-1, keepdims=True) + RMS_EPS) * g


def _rope(x, cos, sinm, half):
    if half == 64:
        partner = pltpu.roll(x, 64, axis=1)
    else:
        lane = lax.broadcasted_iota(jnp.int32, x.shape, 1)
        partner = jnp.where((lane % (2 * half)) < half, pltpu.roll(x, 128 - half, axis=1), pltpu.roll(x, half, axis=1))
    return x * cos + partner * sinm


def _rope_t(x, cos, sinm, half):
    return _rope(x, cos, -sinm, half)


def _rmsnorm_fwd(x, g, *, name):
    S, W = x.shape
    T = min(S, 512)
    return _pw(lambda xv, gv: _rms(xv, gv), [x, g], [_row(T, W), _par(W)], [SDS((S, W), bf16)], [_row(T, W)],
               (1, S // T), name=name)[0]


def _rmsnorm_bwd(x, g, dh, dres, *, name):
    S, W = x.shape
    T = min(S, 512)

    def fn(xv, gv, dhv, drv):
        _, vjp = jax.vjp(_rms, xv, gv)
        dx, dg = vjp(dhv)
        return dx + drv, dg

    return _pw(fn, [x, g, dh, dres], [_row(T, W), _par(W), _row(T, W), _row(T, W)],
               [SDS((S, W), f32), SDS((1, W), f32)], [_row(T, W), _par(W)], (1, S // T), n_acc=1, name=name)


def _ret_tables(lg, reverse):
    C = RET_CHUNK
    ii = lax.broadcasted_iota(jnp.int32, (C, C), 0).astype(f32)
    jj = lax.broadcasted_iota(jnp.int32, (C, C), 1).astype(f32)
    if not reverse:
        E = ii - jj
        mask = E >= 0
        eq = ii + 1.0
        ek = (C - 1.0) - ii
    else:
        E = jj - ii
        mask = E > 0
        eq = C - ii
        ek = ii
    Dm = jnp.where(mask, jnp.exp(jnp.where(mask, E, 0.0) * lg), 0.0)
    Em = jnp.where(mask, E, 0.0)
    qw = jnp.exp(eq * lg)
    kw = jnp.exp(ek * lg)
    qw2 = jnp.concatenate([qw, qw], axis=1)
    return Dm, Em, eq, ek, qw, kw, qw2, jnp.exp(C * lg)


def _dot(a, b, dims):
    return lax.dot_general(a.astype(bf16), b.astype(bf16), (dims, ((), ())), preferred_element_type=f32)


NN = ((1,), (0,))
NT = ((1,), (1,))
TN = ((0,), (0,))


def _ret_dir_fwd(zr, lg, cos, sinm, *, reverse, name):
    S = zr.shape[0]
    C = RET_CHUNK
    TB = min(S, 512)
    nc = TB // C
    NB = S // TB
    d = 1 if reverse else 0
    scale = RET_DK ** -0.5

    def tb(b):
        return (NB - 1 - b) if reverse else b

    def body(lg_ref, q_ref, k_ref, v_ref, cos_ref, sin_ref, y_ref, st_ref, state):
        h = pl.program_id(0)
        b = pl.program_id(1)

        @pl.when(b == 0)
        def _():
            state[...] = jnp.zeros_like(state)

        Dm, _, _, _, _, kw, qw2, gC = _ret_tables(lg_ref[d, h], reverse)
        order = range(nc - 1, -1, -1) if reverse else range(nc)
        for c in order:
            rows = pl.ds(c * C, C)
            q = _rope(q_ref[rows, :], cos_ref[rows, :], sin_ref[rows, :], 64) * scale
            k = _rope(k_ref[rows, :], cos_ref[rows, :], sin_ref[rows, :], 64)
            v = v_ref[rows, :]
            st = state[...]
            st_ref[0, c] = st
            s = _dot(q, k, NT) * Dm
            o = _dot(s, v, NN) + _dot(q, st, NN) * qw2
            y_ref[rows, :] = o
            state[...] = gC * st + _dot(k * kw, v, TN)

    return pl.pallas_call(
        body, grid=(RET_HEADS, NB),
        in_specs=[pl.BlockSpec(memory_space=pltpu.SMEM),
                  pl.BlockSpec((TB, 128), lambda h, b: (tb(b), h)),
                  pl.BlockSpec((TB, 128), lambda h, b: (tb(b), 4 + h)),
                  pl.BlockSpec((TB, 256), lambda h, b: (tb(b), 4 + h)),
                  pl.BlockSpec((TB, 128), lambda h, b: (tb(b), 0)),
                  pl.BlockSpec((TB, 128), lambda h, b: (tb(b), 0))],
        out_specs=[pl.BlockSpec((TB, 256), lambda h, b: (tb(b), h)),
                   pl.BlockSpec((1, nc, 128, 256), lambda h, b: (h, tb(b), 0, 0))],
        out_shape=[SDS((S, 1024), f32), SDS((RET_HEADS, S // C, 128, 256), f32)],
        scratch_shapes=[pltpu.VMEM((128, 256), f32)],
        compiler_params=_params(("parallel", "arbitrary")), name=name)(lg, zr, zr, zr, cos, sinm)


def _ret_dir_bwd(zr, lg, cos, sinm, dy, states, *, reverse, name):
    S = zr.shape[0]
    C = RET_CHUNK
    TB = min(S, 512)
    nc = TB // C
    NB = S // TB
    d = 1 if reverse else 0
    scale = RET_DK ** -0.5

    def tb(b):
        return b if reverse else (NB - 1 - b)

    def body(lg_ref, q_ref, k_ref, v_ref, cos_ref, sin_ref, dy_ref, st_ref, dq_ref, dk_ref, dv_ref, dlg_ref, dstate):
        h = pl.program_id(0)
        b = pl.program_id(1)

        @pl.when(b == 0)
        def _():
            dstate[...] = jnp.zeros_like(dstate)
            dlg_ref[...] = jnp.zeros_like(dlg_ref)

        Dm, Em, eq, ek, qw, kw, qw2, gC = _ret_tables(lg_ref[d, h], reverse)
        order = range(nc) if reverse else range(nc - 1, -1, -1)
        dlg = jnp.zeros((), f32)
        for c in order:
            rows = pl.ds(c * C, C)
            cs, sn = cos_ref[rows, :], sin_ref[rows, :]
            q = _rope(q_ref[rows, :], cs, sn, 64) * scale
            k = _rope(k_ref[rows, :], cs, sn, 64)
            v = v_ref[rows, :]
            do = dy_ref[rows, :]
            st = st_ref[0, c]
            ds = dstate[...]
            p = _dot(q, k, NT)
            a = p * Dm
            dp = _dot(do, v, NT) * Dm
            dq_cross = _dot(do, st, NT) * qw
            dk_cross = _dot(v, ds, NT) * kw
            dq = _dot(dp, k, NN) + dq_cross
            dk = _dot(dp, q, TN) + dk_cross
            dv = _dot(a, do, TN) + _dot(k * kw, ds, NN)
            dlg = dlg + jnp.sum(dp * p * Em) + jnp.sum(dq_cross * q * eq) + jnp.sum(dk_cross * k * ek) \
                + C * gC * jnp.sum(ds * st)
            dstate[...] = gC * ds + _dot(q * qw, do, TN)
            dq_ref[rows, :] = _rope_t(dq, cs, sn, 64) * scale
            dk_ref[rows, :] = _rope_t(dk, cs, sn, 64)
            dv_ref[rows, :] = dv
        dlg_ref[...] += jnp.full(dlg_ref.shape, dlg, f32)

    return pl.pallas_call(
        body, grid=(RET_HEADS, NB),
        in_specs=[pl.BlockSpec(memory_space=pltpu.SMEM),
                  pl.BlockSpec((TB, 128), lambda h, b: (tb(b), h)),
                  pl.BlockSpec((TB, 128), lambda h, b: (tb(b), 4 + h)),
                  pl.BlockSpec((TB, 256), lambda h, b: (tb(b), 4 + h)),
                  pl.BlockSpec((TB, 128), lambda h, b: (tb(b), 0)),
                  pl.BlockSpec((TB, 128), lambda h, b: (tb(b), 0)),
                  pl.BlockSpec((TB, 256), lambda h, b: (tb(b), h)),
                  pl.BlockSpec((1, nc, 128, 256), lambda h, b: (h, tb(b), 0, 0))],
        out_specs=[pl.BlockSpec((TB, 128), lambda h, b: (tb(b), h)),
                   pl.BlockSpec((TB, 128), lambda h, b: (tb(b), h)),
                   pl.BlockSpec((TB, 256), lambda h, b: (tb(b), h)),
                   pl.BlockSpec((1, 1, 128), lambda h, b: (h, 0, 0))],
        out_shape=[SDS((S, 512), f32), SDS((S, 512), f32), SDS((S, 1024), f32), SDS((RET_HEADS, 1, 128), f32)],
        scratch_shapes=[pltpu.VMEM((128, 256), f32)],
        compiler_params=_params(("parallel", "arbitrary")), name=name)(lg, zr, zr, zr, cos, sinm, dy, states)


def _gn_gate(yf, yb, g, gn):
    y = yf + yb
    mu = jnp.mean(y, axis=-1, keepdims=True)
    var = jnp.mean(jnp.square(y - mu), axis=-1, keepdims=True)
    yn = (y - mu) * lax.rsqrt(var + GN_EPS)
    return jax.nn.silu(g) * (yn * gn)


def _flash_fwd(Q, K, kv, *, name):
    S = Q.shape[0]
    hq = min(S, 256)
    nh = 4 if S % 1024 == 0 else 1
    tq = nh * hq
    tk = min(S, 512)
    nk = S // tk

    def body(q_ref, k_ref, v_ref, o_ref, l_ref, m_s, l_s, acc):
        kk = pl.program_id(2)

        @pl.when(kk == 0)
        def _():
            m_s[...] = jnp.full_like(m_s, -jnp.inf)
            l_s[...] = jnp.zeros_like(l_s)
            acc[...] = jnp.zeros_like(acc)

        k = k_ref[...]
        v = v_ref[...]
        sts = [lax.dot_general(k, q_ref[hf * hq:(hf + 1) * hq, :], (NT, ((), ())), preferred_element_type=f32)
               for hf in range(nh)]
        for hf in range(nh):
            st = sts[hf]
            m_prev = m_s[hf]
            m_new = jnp.maximum(m_prev, jnp.max(st, axis=0, keepdims=True))
            pt = jnp.exp2(st - m_new)
            alpha = jnp.exp2(m_prev - m_new)
            l_s[hf] = alpha * l_s[hf] + jnp.sum(pt, axis=0, keepdims=True)
            acc[hf] = alpha * acc[hf] + lax.dot_general(v, pt.astype(bf16), (TN, ((), ())), preferred_element_type=f32)
            m_s[hf] = m_new

        @pl.when(kk == nk - 1)
        def _():
            for hf in range(nh):
                o_ref[hf * hq:(hf + 1) * hq, :] = jnp.transpose(acc[hf] / l_s[hf]).astype(bf16)
                l_ref[0, :, hf * hq:(hf + 1) * hq] = m_s[hf] + jnp.log2(l_s[hf])

    return pl.pallas_call(
        body, grid=(MLA_HEADS, S // tq, nk),
        in_specs=[pl.BlockSpec((tq, 256), lambda h, i, k: (i, h)),
                  pl.BlockSpec((tk, 256), lambda h, i, k: (k, h)),
                  pl.BlockSpec((tk, 128), lambda h, i, k: (k, 2 * h + 1))],
        out_specs=[pl.BlockSpec((tq, 128), lambda h, i, k: (i, h)), pl.BlockSpec((1, 1, tq), lambda h, i, k: (h, 0, i))],
        out_shape=[SDS((S, 1024), bf16), SDS((MLA_HEADS, 1, S), f32)],
        scratch_shapes=[pltpu.VMEM((nh, 1, hq), f32), pltpu.VMEM((nh, 1, hq), f32), pltpu.VMEM((nh, 128, hq), f32)],
        compiler_params=_params(("parallel", "parallel", "arbitrary")), name=name)(Q, K, kv)


def _attn_delta(dO, O, *, name):
    S = dO.shape[0]
    T = min(S, 512)

    def body(do_ref, o_ref, d_ref):
        prod = do_ref[...] * o_ref[...].astype(f32)
        d_ref[0] = lax.dot_general(jnp.ones((8, 128), f32), prod, (NT, ((), ())), preferred_element_type=f32,
                                   precision=lax.Precision.HIGHEST)[0:1, :]

    return pl.pallas_call(
        body, grid=(MLA_HEADS, S // T),
        in_specs=[pl.BlockSpec((T, 128), lambda h, i: (i, h)), pl.BlockSpec((T, 128), lambda h, i: (i, h))],
        out_specs=pl.BlockSpec((1, 1, T), lambda h, i: (h, 0, i)), out_shape=SDS((MLA_HEADS, 1, S), f32),
        compiler_params=_params(("parallel", "parallel")), name=name)(dO, O)


def _flash_bwd(Q, K, kv, delta, L, dO, *, name):
    S = Q.shape[0]
    hq = min(S, 512)
    nh = 2 if S % 1024 == 0 else 1
    tq = nh * hq
    tk = min(S, 512)
    nq = S // tq
    ln2 = math.log(2.0)

    def body(q_ref, k_ref, v_ref, dl_ref, l_ref, do_ref, dq_ref, dk_ref, dv_ref, dk_acc, dv_acc):
        kk = pl.program_id(1)
        i = pl.program_id(2)

        @pl.when((kk == 0) & (i == 0))
        def _():
            dq_ref[...] = jnp.zeros_like(dq_ref)

        @pl.when(i == 0)
        def _():
            dk_acc[...] = jnp.zeros_like(dk_acc)
            dv_acc[...] = jnp.zeros_like(dv_acc)

        k = k_ref[...]
        v = v_ref[...]
        dk_new = dk_acc[...]
        dv_new = dv_acc[...]
        for hf in range(nh):
            sl = slice(hf * hq, (hf + 1) * hq)
            q = q_ref[sl, :]
            st = lax.dot_general(k, q, (NT, ((), ())), preferred_element_type=f32)
            pt = jnp.exp2(st - l_ref[0, :, sl])
            delta = dl_ref[0, :, sl]
            dob = do_ref[sl, :].astype(bf16)
            dv_new = dv_new + lax.dot_general(pt.astype(bf16), dob, (NN, ((), ())), preferred_element_type=f32)
            dpt = lax.dot_general(v, dob, (NT, ((), ())), preferred_element_type=f32)
            dst = (pt * (dpt - delta)).astype(bf16)
            dk_new = dk_new + lax.dot_general(dst, q, (NN, ((), ())), preferred_element_type=f32)
            rows = pl.ds(pl.multiple_of(i * tq + hf * hq, hq), hq)
            dq_ref[rows, :] += lax.dot_general(dst, k, (TN, ((), ())), preferred_element_type=f32)
        dk_acc[...] = dk_new
        dv_acc[...] = dv_new

        @pl.when(i == nq - 1)
        def _():
            dk_ref[...] = dk_acc[...] * ln2
            dv_ref[...] = dv_acc[...]

    return pl.pallas_call(
        body, grid=(MLA_HEADS, S // tk, nq),
        in_specs=[pl.BlockSpec((tq, 256), lambda h, k, i: (i, h)),
                  pl.BlockSpec((tk, 256), lambda h, k, i: (k, h)),
                  pl.BlockSpec((tk, 128), lambda h, k, i: (k, 2 * h + 1)),
                  pl.BlockSpec((1, 1, tq), lambda h, k, i: (h, 0, i)),
                  pl.BlockSpec((1, 1, tq), lambda h, k, i: (h, 0, i)),
                  pl.BlockSpec((tq, 128), lambda h, k, i: (i, h))],
        out_specs=[pl.BlockSpec((S, 256), lambda h, k, i: (0, h)),
                   pl.BlockSpec((tk, 256), lambda h, k, i: (k, h)),
                   pl.BlockSpec((tk, 128), lambda h, k, i: (k, h))],
        out_shape=[SDS((S, 2048), f32), SDS((S, 2048), f32), SDS((S, 1024), f32)],
        scratch_shapes=[pltpu.VMEM((tk, 256), f32), pltpu.VMEM((tk, 128), f32)],
        compiler_params=_params(("parallel", "arbitrary", "arbitrary"), VMEM_BIG), name=name)(Q, K, kv, delta, L, dO)


def _mla_qk_prep(q, kv, zm, cosm, sinm, *, name):
    S = q.shape[0]
    T = min(S, 256)
    scale = (MLA_NOPE + MLA_ROPE) ** -0.5 * math.log2(math.e)

    def body(q_ref, kv_ref, kr_ref, cos_ref, sin_ref, oq_ref, ok_ref):
        cs, sn = cos_ref[...], sin_ref[...]
        kr = _rope(kr_ref[...], cs, sn, 32).astype(bf16)
        for h in range(MLA_HEADS):
            a = 256 * h
            oq_ref[:, a:a + 128] = (q_ref[:, a:a + 128].astype(f32) * scale).astype(bf16)
            oq_ref[:, a + 128:a + 256] = (_rope(q_ref[:, a + 128:a + 256].astype(f32), cs, sn, 32) * scale).astype(bf16)
            ok_ref[:, a:a + 128] = kv_ref[:, a:a + 128]
            ok_ref[:, a + 128:a + 256] = kr

    row = lambda w, col=0: pl.BlockSpec((T, w), lambda i: (i, col))
    return pl.pallas_call(
        body, grid=(S // T,), in_specs=[row(2048), row(2048), row(128, 6), row(128), row(128)],
        out_specs=[row(2048), row(2048)], out_shape=[SDS((S, 2048), bf16), SDS((S, 2048), bf16)],
        compiler_params=_params(("parallel",), VMEM_BIG), name=name)(q, kv, zm, cosm, sinm)


def _mla_bwd_prep(dQ, dK, dV, cosm, sinm, *, name):
    S = dQ.shape[0]
    T = min(S, 256)
    scale = (MLA_NOPE + MLA_ROPE) ** -0.5

    def body(dq_ref, dk_ref, dv_ref, cos_ref, sin_ref, oq_ref, okv_ref, okr_ref):
        cs, sn = cos_ref[...], sin_ref[...]
        kr = jnp.zeros((T, 128), f32)
        for h in range(MLA_HEADS):
            a = 256 * h
            oq_ref[:, a:a + 128] = (dq_ref[:, a:a + 128] * scale).astype(bf16)
            oq_ref[:, a + 128:a + 256] = (_rope_t(dq_ref[:, a + 128:a + 256], cs, sn, 32) * scale).astype(bf16)
            okv_ref[:, a:a + 128] = dk_ref[:, a:a + 128].astype(bf16)
            okv_ref[:, a + 128:a + 256] = dv_ref[:, 128 * h:128 * h + 128].astype(bf16)
            kr = kr + dk_ref[:, a + 128:a + 256]
        okr_ref[...] = _rope_t(kr, cs, sn, 32)

    return pl.pallas_call(
        body, grid=(S // T,),
        in_specs=[pl.BlockSpec((T, 2048), lambda i: (i, 0)), pl.BlockSpec((T, 2048), lambda i: (i, 0)),
                  pl.BlockSpec((T, 1024), lambda i: (i, 0)), pl.BlockSpec((T, 128), lambda i: (i, 0)),
                  pl.BlockSpec((T, 128), lambda i: (i, 0))],
        out_specs=[pl.BlockSpec((T, 2048), lambda i: (i, 0)), pl.BlockSpec((T, 2048), lambda i: (i, 0)),
                   pl.BlockSpec((T, 128), lambda i: (i, 0))],
        out_shape=[SDS((S, 2048), bf16), SDS((S, 2048), bf16), SDS((S, 128), f32)],
        compiler_params=_params(("parallel",), VMEM_BIG), name=name)(dQ, dK, dV, cosm, sinm)


def _mla_norm_bwd(zm, qg, kvg, dcqn, dckvn, dkr, *, name):
    S = zm.shape[0]
    T = min(S, 512)

    def body(cq_ref, ckv_ref, qg_ref, kvg_ref, dcq_ref, dckv_ref, dkr_ref, o_ref, dqg_ref, dkvg_ref):
        i = pl.program_id(0)
        _, vjp = jax.vjp(_rms, cq_ref[...], qg_ref[...])
        dcq, dqg = vjp(dcq_ref[...])
        _, vjp2 = jax.vjp(_rms, ckv_ref[...], kvg_ref[...])
        dckv, dkvg = vjp2(dckv_ref[...])
        o_ref[:, 0:384] = dcq.astype(bf16)
        o_ref[:, 384:512] = jnp.zeros((T, 128), bf16)
        o_ref[:, 512:768] = dckv.astype(bf16)
        o_ref[:, 768:896] = dkr_ref[...].astype(bf16)

        @pl.when(i == 0)
        def _():
            dqg_ref[...] = dqg
            dkvg_ref[...] = dkvg

        @pl.when(i > 0)
        def _():
            dqg_ref[...] += dqg
            dkvg_ref[...] += dkvg

    return pl.pallas_call(
        body, grid=(S // T,),
        in_specs=[pl.BlockSpec((T, 384), lambda i: (i, 0)), pl.BlockSpec((T, 256), lambda i: (i, 2)),
                  pl.BlockSpec((1, 384), lambda i: (0, 0)), pl.BlockSpec((1, 256), lambda i: (0, 0)),
                  pl.BlockSpec((T, 384), lambda i: (i, 0)), pl.BlockSpec((T, 256), lambda i: (i, 0)),
                  pl.BlockSpec((T, 128), lambda i: (i, 0))],
        out_specs=[pl.BlockSpec((T, 896), lambda i: (i, 0)), pl.BlockSpec((1, 384), lambda i: (0, 0)),
                   pl.BlockSpec((1, 256), lambda i: (0, 0))],
        out_shape=[SDS((S, 896), bf16), SDS((1, 384), f32), SDS((1, 256), f32)],
        compiler_params=_params(("arbitrary",)), name=name)(zm, zm, qg, kvg, dcqn, dckvn, dkr)


def _s5_disc(a_re, a_im, ldt, b_re, b_im):
    dt = jnp.exp(ldt)
    ar = jnp.minimum(a_re, -1e-4)
    mag = jnp.exp(dt * ar)
    abr = mag * jnp.cos(dt * a_im)
    abi = mag * jnp.sin(dt * a_im)
    den = ar * ar + a_im * a_im
    nr = abr - 1.0
    ni = abi
    cr = (nr * ar + ni * a_im) / den
    ci = (ni * ar - nr * a_im) / den
    return abr, abi, cr * b_re - ci * b_im, cr * b_im + ci * b_re


def _s5_param_fwd(a_re, a_im, ldt, b_re, b_im, *, name):
    R = SDS((1, 8192), f32)
    M = SDS((16, 8192), f32)
    Pw = SDS((64, 8192), f32)

    def body(a_re_r, a_im_r, ldt_r, b_re_r, b_im_r, o1, o2, o3, o4, p_re, p_im):
        abr, abi, bbr, bbi = _s5_disc(a_re_r[...], a_im_r[...], ldt_r[...], b_re_r[...], b_im_r[...])
        o1[...] = abr
        o2[...] = abi
        o3[...] = bbr
        o4[...] = bbi
        dt = jnp.exp(ldt_r[...])
        ar = jnp.minimum(a_re_r[...], -1e-4)
        n = lax.broadcasted_iota(jnp.int32, (64, 8192), 0).astype(f32) + 1.0
        mag = jnp.exp(n * (dt * ar))
        ang = n * (dt * a_im_r[...])
        p_re[...] = mag * jnp.cos(ang)
        p_im[...] = mag * jnp.sin(ang)

    return pl.pallas_call(body, out_shape=[R, R, M, M, Pw, Pw], name=name)(a_re, a_im, ldt, b_re, b_im)


def _s5_param_bwd(a_re, a_im, ldt, b_re, b_im, d_abr, d_abi, d_bbr, d_bbi, *, name):
    R = SDS((1, 8192), f32)
    M = SDS((16, 8192), f32)

    def body(a_re_r, a_im_r, ldt_r, b_re_r, b_im_r, c1, c2, c3, c4, o1, o2, o3, o4, o5):
        _, vjp = jax.vjp(_s5_disc, a_re_r[...], a_im_r[...], ldt_r[...], b_re_r[...], b_im_r[...])
        g = vjp((c1[...], c2[...], c3[...], c4[...]))
        for o, v in zip((o1, o2, o3, o4, o5), g):
            o[...] = v

    return pl.pallas_call(body, out_shape=[R, R, R, M, M], name=name)(a_re, a_im, ldt, b_re, b_im, d_abr, d_abi, d_bbr, d_bbi)


def _seg_perm(T, inverse):
    L = T // S5_SEG
    i = jnp.arange(T)
    src = (i % S5_SEG) * L + i // S5_SEG
    P = (src[:, None] == jnp.arange(T)[None, :]).astype(bf16)
    return P.T if inverse else P


def _perm_rows(a, P, *, name):
    S, W = a.shape
    T = P.shape[0]

    def body(p_ref, a_ref, o_ref):
        o_ref[...] = lax.dot_general(p_ref[...], a_ref[...], (NN, ((), ())), preferred_element_type=f32).astype(o_ref.dtype)

    return pl.pallas_call(
        body, grid=(S // T,), in_specs=[pl.BlockSpec((T, T), lambda i: (0, 0)), pl.BlockSpec((T, W), lambda i: (i, 0))],
        out_specs=pl.BlockSpec((T, W), lambda i: (i, 0)), out_shape=SDS((S, W), a.dtype),
        compiler_params=_params(("parallel",)), name=name)(P, a)


def _scan_core(xr, xi, ar, ai, pwr_ref, pwi_ref, a64r, a64i, carry, *, reverse, T, conj):
    L = T // S5_SEG
    sg = -1.0 if conj else 1.0
    arb = jnp.broadcast_to(ar, (8, 512))
    aib = jnp.broadcast_to(ai, (8, 512))
    UN = 4

    def step(r4, c):
        cr, ci = c
        for u in range(UN):
            r0 = r4 * UN + u
            r = (L - 1 - r0) if reverse else r0
            rows = pl.ds(pl.multiple_of(r * 8, 8), 8)
            nr = arb * cr - aib * ci + xr[rows, :]
            ni = arb * ci + aib * cr + xi[rows, :]
            xr[rows, :] = nr
            xi[rows, :] = ni
            cr, ci = nr, ni
        return cr, ci

    lr, li = lax.fori_loop(0, L // UN, step, (jnp.zeros((8, 512), f32), jnp.zeros((8, 512), f32)))
    row8 = lax.broadcasted_iota(jnp.int32, (8, 512), 0)
    cr = carry[0, 0:1, :]
    ci = carry[1, 0:1, :]
    a6i = sg * a64i
    cin_r = jnp.zeros((8, 512), f32)
    cin_i = jnp.zeros((8, 512), f32)
    for seg in (range(S5_SEG - 1, -1, -1) if reverse else range(S5_SEG)):
        cin_r = jnp.where(row8 == seg, cr, cin_r)
        cin_i = jnp.where(row8 == seg, ci, cin_i)
        ncr = lr[seg:seg + 1, :] + a64r * cr - a6i * ci
        nci = li[seg:seg + 1, :] + a64r * ci + a6i * cr
        cr, ci = ncr, nci
    carry[0, 0:1, :] = cr
    carry[1, 0:1, :] = ci

    def fix(r4, _):
        for u in range(UN):
            r = r4 * UN + u
            rows = pl.ds(pl.multiple_of(r * 8, 8), 8)
            pr = pwr_ref[pl.ds(r, 1), :]
            pi = sg * pwi_ref[pl.ds(r, 1), :]
            xr[rows, :] += pr * cin_r - pi * cin_i
            xi[rows, :] += pr * cin_i + pi * cin_r
        return 0

    lax.fori_loop(0, L // UN, fix, 0)


def _s5_scan_fwd(u, BBr, BBi, CCr, CCi, abr, abi, pwr, pwi, *, reverse, name):
    S = u.shape[0]
    T = min(S, 512)
    NB = S // T
    L = T // S5_SEG
    d = 1 if reverse else 0

    def tb(b):
        return (NB - 1 - b) if reverse else b

    def body(u_ref, bbr_ref, bbi_ref, ccr_ref, cci_ref, ar_ref, ai_ref, pwr_ref, pwi_ref, y_ref, xr_ref, xi_ref, carry):
        b = pl.program_id(1)

        @pl.when(b == 0)
        def _():
            carry[...] = jnp.zeros_like(carry)

        ub = u_ref[...].astype(bf16)
        xr_ref[...] = lax.dot_general(ub, bbr_ref[0, 0], (NN, ((), ())), preferred_element_type=f32)
        xi_ref[...] = lax.dot_general(ub, bbi_ref[0, 0], (NN, ((), ())), preferred_element_type=f32)
        a6 = (0 if reverse else L - 1)
        _scan_core(xr_ref, xi_ref, ar_ref[...], ai_ref[...], pwr_ref, pwi_ref, pwr_ref[a6:a6 + 1, :], pwi_ref[a6:a6 + 1, :],
                   carry, reverse=reverse, T=T, conj=False)
        y_ref[...] = _dot(xr_ref[...], ccr_ref[0, 0], NN) - _dot(xi_ref[...], cci_ref[0, 0], NN)

    mat = lambda shp: pl.BlockSpec((1, 1) + shp, lambda j, b: (d, j, 0, 0))
    vec = lambda r: pl.BlockSpec((r, 512), lambda j, b: (0, d * S5_NJ + j))
    return pl.pallas_call(
        body, grid=(S5_NJ, NB),
        in_specs=[pl.BlockSpec((T, 128), lambda j, b: (tb(b), j)), mat((128, 512)), mat((128, 512)), mat((512, 128)),
                  mat((512, 128)), vec(1), vec(1), vec(L), vec(L)],
        out_specs=[pl.BlockSpec((T, 128), lambda j, b: (tb(b), j)), pl.BlockSpec((T, 512), lambda j, b: (tb(b), j)),
                   pl.BlockSpec((T, 512), lambda j, b: (tb(b), j))],
        out_shape=[SDS((S, 1024), f32), SDS((S, 4096), f32), SDS((S, 4096), f32)],
        scratch_shapes=[pltpu.VMEM((2, 8, 512), f32)],
        compiler_params=_params(("parallel", "arbitrary")), name=name)(u, BBr, BBi, CCr, CCi, abr, abi, pwr, pwi)


def _s5_scan_bwd(u, dy, xr, xi, BBr, BBi, CCr, CCi, abr, abi, pwr, pwi, *, reverse, name):
    S = u.shape[0]
    T = min(S, 512)
    NB = S // T
    L = T // S5_SEG
    d = 1 if reverse else 0
    adj_rev = not reverse

    def tb(b):
        return b if reverse else (NB - 1 - b)

    def bnd(b):
        t = tb(b)
        if reverse:
            return jnp.minimum((t + 1) * (T // 8), S // 8 - 1)
        return jnp.maximum(t * (T // 8) - 1, 0)

    def body(u_ref, dy_ref, xr_ref, xi_ref, xbr_ref, xbi_ref, bbr_ref, bbi_ref, ccr_ref, cci_ref, ar_ref, ai_ref,
             pwr_ref, pwi_ref, du_ref, dbbr_ref, dbbi_ref, dccr_ref, dcci_ref, dar_ref, dai_ref, carry, lam):
        b = pl.program_id(1)

        @pl.when(b == 0)
        def _():
            carry[...] = jnp.zeros_like(carry)
            for r in (dbbr_ref, dbbi_ref, dccr_ref, dcci_ref, dar_ref, dai_ref):
                r[...] = jnp.zeros_like(r)

        dyb = dy_ref[...]
        lam[0] = lax.dot_general(dyb, ccr_ref[0, 0], (NT, ((), ())), preferred_element_type=f32)
        lam[1] = -lax.dot_general(dyb, cci_ref[0, 0], (NT, ((), ())), preferred_element_type=f32)
        a6 = (0 if adj_rev else L - 1)
        _scan_core(lam.at[0], lam.at[1], ar_ref[...], -ai_ref[...], pwr_ref, pwi_ref, pwr_ref[a6:a6 + 1, :],
                   pwi_ref[a6:a6 + 1, :], carry, reverse=adj_rev, T=T, conj=True)
        ub = u_ref[...].astype(bf16)
        first = (b == NB - 1)
        lrb = lam[0].astype(bf16)
        lib = lam[1].astype(bf16)
        du_ref[...] = lax.dot_general(lrb, bbr_ref[0, 0], (NT, ((), ())), preferred_element_type=f32) \
            + lax.dot_general(lib, bbi_ref[0, 0], (NT, ((), ())), preferred_element_type=f32)
        dbbr_ref[0, 0] += lax.dot_general(ub, lrb, (TN, ((), ())), preferred_element_type=f32)
        dbbi_ref[0, 0] += lax.dot_general(ub, lib, (TN, ((), ())), preferred_element_type=f32)
        dccr_ref[0, 0] += lax.dot_general(dyb, xr_ref[...].astype(bf16), (TN, ((), ())), preferred_element_type=f32)
        dcci_ref[0, 0] -= lax.dot_general(dyb, xi_ref[...].astype(bf16), (TN, ((), ())), preferred_element_type=f32)
        row8 = lax.broadcasted_iota(jnp.int32, (8, 512), 0)
        if reverse:
            body_x, body_l, edge_l = slice(8, T), slice(0, T - 8), slice(T - 8, T)
            sp_r = jnp.where(row8 == 7, jnp.where(first, 0.0, xbr_ref[0:1, :]), pltpu.roll(xr_ref[0:8, :], 7, axis=0))
            sp_i = jnp.where(row8 == 7, jnp.where(first, 0.0, xbi_ref[0:1, :]), pltpu.roll(xi_ref[0:8, :], 7, axis=0))
        else:
            body_x, body_l, edge_l = slice(0, T - 8), slice(8, T), slice(0, 8)
            sp_r = jnp.where(row8 == 0, jnp.where(first, 0.0, xbr_ref[7:8, :]), pltpu.roll(xr_ref[T - 8:T, :], 1, axis=0))
            sp_i = jnp.where(row8 == 0, jnp.where(first, 0.0, xbi_ref[7:8, :]), pltpu.roll(xi_ref[T - 8:T, :], 1, axis=0))
        xpr, xpi = xr_ref[body_x, :], xi_ref[body_x, :]
        lr, li = lam[0, body_l, :], lam[1, body_l, :]
        er, ei = lam[0, edge_l, :], lam[1, edge_l, :]
        dar_ref[...] += jnp.sum(xpr * lr + xpi * li, axis=0, keepdims=True) + jnp.sum(sp_r * er + sp_i * ei, axis=0, keepdims=True)
        dai_ref[...] += jnp.sum(xpr * li - xpi * lr, axis=0, keepdims=True) + jnp.sum(sp_r * ei - sp_i * er, axis=0, keepdims=True)

    mat = lambda shp: pl.BlockSpec((1, 1) + shp, lambda j, b: (d, j, 0, 0))
    omat = lambda shp: pl.BlockSpec((1, 1) + shp, lambda j, b: (0, j, 0, 0))
    vec = lambda r: pl.BlockSpec((r, 512), lambda j, b: (0, d * S5_NJ + j))
    blk = lambda w: pl.BlockSpec((T, w), lambda j, b: (tb(b), j))
    return pl.pallas_call(
        body, grid=(S5_NJ, NB),
        in_specs=[blk(128), blk(128), blk(512), blk(512),
                  pl.BlockSpec((8, 512), lambda j, b: (bnd(b), j)), pl.BlockSpec((8, 512), lambda j, b: (bnd(b), j)),
                  mat((128, 512)), mat((128, 512)), mat((512, 128)), mat((512, 128)), vec(1), vec(1), vec(L), vec(L)],
        out_specs=[blk(128), omat((128, 512)), omat((128, 512)), omat((128, 512)), omat((128, 512)),
                   pl.BlockSpec((1, 512), lambda j, b: (0, j)), pl.BlockSpec((1, 512), lambda j, b: (0, j))],
        out_shape=[SDS((S, 1024), f32), SDS((1, 8, 128, 512), f32), SDS((1, 8, 128, 512), f32), SDS((1, 8, 128, 512), f32),
                   SDS((1, 8, 128, 512), f32), SDS((1, 4096), f32), SDS((1, 4096), f32)],
        scratch_shapes=[pltpu.VMEM((2, 8, 512), f32), pltpu.VMEM((2, T, 512), f32)],
        compiler_params=_params(("parallel", "arbitrary"), VMEM_BIG), name=name)(
            u, dy, xr, xi, xr, xi, BBr, BBi, CCr, CCi, abr, abi, pwr, pwi)


def _silu_mul(g, u):
    return jax.nn.silu(g) * u


def _mixf(p0, p1, p2, z0, z1, z2):
    return jax.nn.sigmoid(z0) * p0 + jax.nn.sigmoid(z1) * p1 + jax.nn.sigmoid(z2) * p2


def _s5_act(yf, yb, u, dd):
    return jax.nn.gelu(yf + yb + dd * u)


def _glu(a, b):
    return a * jax.nn.sigmoid(b)


def _layer_fwd(x, w, tabs, l):
    S = x.shape[0]
    T = min(S, 512)
    I = S // T
    nm = lambda s: f"L{l}_{s}"
    sv = {'x': x}
    h = _rmsnorm_fwd(x, w['norm1_g'], name=nm("norm1"))
    zr = _mm(h, w['W_ret'], name=nm("in_ret"))
    zm = _mm(h, w['W_mla'], name=nm("in_mla"))
    h_seg = _perm_rows(h, tabs['seg_perm'], name=nm("s5_perm_h"))
    zs = _mm(h_seg, w['W_s5'], name=nm("in_s5"))
    zg = _mm(h, w['W_gate'], out_dtype=bf16, name=nm("in_gate"))
    sv.update(h=h, h_seg=h_seg, zr=zr, zm=zm, zs=zs, zg=zg)

    yf, stf = _ret_dir_fwd(zr, w['lg'], tabs['cos_r'], tabs['sin_r'], reverse=False, name=nm("ret_f"))
    yb, stb = _ret_dir_fwd(zr, w['lg'], tabs['cos_r'], tabs['sin_r'], reverse=True, name=nm("ret_b"))
    hd = lambda j: j
    y_ret = _pw(_gn_gate, [yf, yb, zr, w['ret_gn_g']],
                [_row(T, 256, hd), _row(T, 256, hd), _row(T, 256, lambda j: 8 + j), _par(256, hd)],
                [SDS((S, 1024), bf16)], [_row(T, 256, hd)], (RET_HEADS, I), name=nm("ret_gn"))[0]
    sv.update(yf=yf, yb=yb, stf=stf, stb=stb, y_ret=y_ret)

    cqn, ckvn = _pw(lambda a, b, g1, g2: (_rms(a, g1), _rms(b, g2)), [zm, zm, w['mla_q_norm_g'], w['mla_kv_norm_g']],
                    [_row(T, 384), _row(T, 256, lambda j: 2), _par(384), _par(256)],
                    [SDS((S, 384), bf16), SDS((S, 256), bf16)], [_row(T, 384), _row(T, 256)], (1, I), name=nm("mla_norm"))
    q = _mm(cqn, w['W_uq'], out_dtype=bf16, name=nm("mla_uq"))
    kv = _mm(ckvn, w['W_ukv'], out_dtype=bf16, name=nm("mla_ukv"))
    Q, K = _mla_qk_prep(q, kv, zm, tabs['cos_m'], tabs['sin_m'], name=nm("mla_qkprep"))
    O, Lse = _flash_fwd(Q, K, kv, name=nm("mla_attn"))
    sv.update(cqn=cqn, ckvn=ckvn, kv=kv, Q=Q, K=K, O=O, Lse=Lse)

    s5 = w['s5']
    ysf, xrf, xif = _s5_scan_fwd(zs, s5['BBr'], s5['BBi'], s5['CCr'], s5['CCi'], s5['abr'], s5['abi'], s5['pwr_f'], s5['pwi_f'],
                                 reverse=False, name=nm("s5_f"))
    ysb, xrb, xib = _s5_scan_fwd(zs, s5['BBr'], s5['BBi'], s5['CCr'], s5['CCi'], s5['abr'], s5['abi'], s5['pwr_f'], s5['pwi_f'],
                                 reverse=True, name=nm("s5_b"))
    gact = _pw(_s5_act, [ysf, ysb, zs, w['s5_d']], [_row(T, D), _row(T, D), _row(T, D), _par(D)],
               [SDS((S, D), bf16)], [_row(T, D)], (1, I), name=nm("s5_act"))[0]
    gg = _mm(gact, w['W_glu'], out_dtype=bf16, name=nm("s5_glu_mm"))
    y_s5 = _pw(_glu, [gg, gg], [_row(T, D), _row(T, D, lambda j: 1)], [SDS((S, D), bf16)], [_row(T, D)], (1, I),
               name=nm("s5_glu"))[0]
    y_s5 = _perm_rows(y_s5, tabs['seg_unperm'], name=nm("s5_unperm_y"))
    sv.update(ysf=ysf, ysb=ysb, xrf=xrf, xif=xif, xrb=xrb, xib=xib, gact=gact, gg=gg, y_s5=y_s5)

    ys = [y_ret, O, y_s5]
    pr = [_mm(ys[i], w['W_br'][i], out_dtype=bf16, name=nm(f"branch{i}")) for i in range(3)]
    mix = _pw(_mixf, pr + [zg, zg, zg],
              [_row(T, D)] * 3 + [_row(T, D), _row(T, D, lambda j: 1), _row(T, D, lambda j: 2)],
              [SDS((S, D), bf16)], [_row(T, D)], (1, I), name=nm("mix"))[0]
    x1 = _mm(mix, w['W_out'], res=x, name=nm("out_proj"))
    h2 = _rmsnorm_fwd(x1, w['norm2_g'], name=nm("norm2"))
    fgu = _mm(h2, w['W_gu'], out_dtype=bf16, name=nm("ffn_gu"))
    act = _pw(_silu_mul, [fgu, fgu], [_row(T, 1408, lambda j: j), _row(T, 1408, lambda j: 2 + j)],
              [SDS((S, FFN_H), bf16)], [_row(T, 1408, lambda j: j)], (2, I), name=nm("ffn_act"))[0]
    x2 = _mm(act, w['W_down'], res=x1, name=nm("ffn_down"))
    sv.update(pr=pr, mix=mix, x1=x1, h2=h2, fgu=fgu, act=act)
    return x2, sv


def _vjp_fn(fn, n_primal, cast=None):
    def g(*args):
        _, vjp = jax.vjp(fn, *args[:n_primal])
        return vjp(args[n_primal].astype(f32))
    return g


def _layer_bwd(dx2, w, tabs, sv, l):
    S = dx2.shape[0]
    T = min(S, 512)
    I = S // T
    nm = lambda s: f"L{l}_b_{s}"
    g = {}
    hd = lambda j: j

    dact = _mm(dx2, w['W_down'], tb=True, out_dtype=bf16, name=nm("ffn_down_dx"))
    g['W_down'] = _mmT(sv['act'], dx2, name=nm("ffn_down_dw"))
    dfg, dfu = _pw(_vjp_fn(_silu_mul, 2), [sv['fgu'], sv['fgu'], dact],
                   [_row(T, 1408, lambda j: j), _row(T, 1408, lambda j: 2 + j), _row(T, 1408, lambda j: j)],
                   [SDS((S, FFN_H), bf16), SDS((S, FFN_H), bf16)], [_row(T, 1408, lambda j: j)] * 2, (2, I), name=nm("ffn_act"))
    dfgu = jnp.concatenate([dfg, dfu], axis=1)
    g['W_gu'] = _mmT(sv['h2'], dfgu, name=nm("ffn_gu_dw"))
    dh2 = _mm(dfgu, w['W_gu'], tb=True, name=nm("ffn_gu_dx"))
    dx1, g['norm2_g'] = _rmsnorm_bwd(sv['x1'], w['norm2_g'], dh2, dx2, name=nm("norm2"))

    dmix = _mm(dx1, w['W_out'], tb=True, out_dtype=bf16, name=nm("out_dx"))
    g['W_out'] = _mmT(sv['mix'], dx1, name=nm("out_dw"))
    zg = sv['zg']
    outs = _pw(_vjp_fn(_mixf, 6), sv['pr'] + [zg, zg, zg, dmix],
               [_row(T, D)] * 3 + [_row(T, D), _row(T, D, lambda j: 1), _row(T, D, lambda j: 2), _row(T, D)],
               [SDS((S, D), bf16)] * 6, [_row(T, D)] * 6, (1, I), name=nm("mix"))
    dpr, dzg = outs[:3], jnp.concatenate(outs[3:], axis=1)
    ys = [sv['y_ret'], sv['O'], sv['y_s5']]
    g['W_br'] = [_mmT(ys[i], dpr[i], name=nm(f"branch{i}_dw")) for i in range(3)]
    dpr_seg = _perm_rows(dpr[2], tabs['seg_perm'], name=nm("s5_perm_dy"))
    dys = [_mm(dpr[i] if i < 2 else dpr_seg, w['W_br'][i], tb=True, out_dtype=bf16,
               name=nm(f"branch{i}_dx")) for i in range(3)]

    gg = sv['gg']
    dga, dgb = _pw(_vjp_fn(_glu, 2), [gg, gg, dys[2]], [_row(T, D), _row(T, D, lambda j: 1), _row(T, D)],
                   [SDS((S, D), bf16)] * 2, [_row(T, D)] * 2, (1, I), name=nm("s5_glu"))
    dgg = jnp.concatenate([dga, dgb], axis=1)
    g['W_glu'] = _mmT(sv['gact'], dgg, name=nm("s5_glu_dw"))
    dgact = _mm(dgg, w['W_glu'], tb=True, out_dtype=bf16, name=nm("s5_glu_dx"))

    def act_bwd(yf, yb, u, dd, ct):
        _, vjp = jax.vjp(_s5_act, yf, yb, u, dd)
        dyf, _, du, ddd = vjp(ct)
        return dyf, du, ddd

    dys5, du_direct, g['s5_d'] = _pw(act_bwd, [sv['ysf'], sv['ysb'], sv['zs'], w['s5_d'], dgact],
                                     [_row(T, D)] * 3 + [_par(D), _row(T, D)],
                                     [SDS((S, D), bf16), SDS((S, D), f32), SDS((1, D), f32)],
                                     [_row(T, D), _row(T, D), _par(D)], (1, I), n_acc=1, name=nm("s5_act"))
    s5 = w['s5']
    rf = _s5_scan_bwd(sv['zs'], dys5, sv['xrf'], sv['xif'], s5['BBr'], s5['BBi'], s5['CCr'], s5['CCi'], s5['abr'], s5['abi'],
                      s5['pwr_a'], s5['pwi_a'], reverse=False, name=nm("s5_f"))
    rb = _s5_scan_bwd(sv['zs'], dys5, sv['xrb'], sv['xib'], s5['BBr'], s5['BBi'], s5['CCr'], s5['CCi'], s5['abr'], s5['abi'],
                      s5['pwr_a'], s5['pwi_a'], reverse=True, name=nm("s5_b"))
    g['s5'] = (rf[1:], rb[1:])
    dzs_seg = _pw(lambda a, b, c: a + b + c, [du_direct, rf[0], rb[0]], [_row(T, D)] * 3, [SDS((S, D), bf16)], [_row(T, D)],
                  (1, I), name=nm("s5_du"))[0]
    dzs = _perm_rows(dzs_seg, tabs['seg_unperm'], name=nm("s5_unperm_dz"))

    delta = _attn_delta(dys[1], sv['O'], name=nm("mla_delta"))
    dQ, dK, dV = _flash_bwd(sv['Q'], sv['K'], sv['kv'], delta, sv['Lse'], dys[1], name=nm("mla_attn"))
    dq_lin, dkv, dkr = _mla_bwd_prep(dQ, dK, dV, tabs['cos_m'], tabs['sin_m'], name=nm("mla_prep"))
    g['W_uq'] = _mmT(sv['cqn'], dq_lin, name=nm("mla_uq_dw"))
    dcqn = _mm(dq_lin, w['W_uq'], tb=True, name=nm("mla_uq_dx"))
    g['W_ukv'] = _mmT(sv['ckvn'], dkv, name=nm("mla_ukv_dw"))
    dckvn = _mm(dkv, w['W_ukv'], tb=True, name=nm("mla_ukv_dx"))
    dzm, g['mla_q_norm_g'], g['mla_kv_norm_g'] = _mla_norm_bwd(sv['zm'], w['mla_q_norm_g'], w['mla_kv_norm_g'], dcqn, dckvn, dkr,
                                                               name=nm("mla_norm"))

    zr = sv['zr']

    def gn_bwd(yf, yb, gt, gn, ct):
        _, vjp = jax.vjp(_gn_gate, yf, yb, gt, gn)
        dyf, _, dgt, dgn = vjp(ct)
        return dyf, dgt, dgn

    dyr, dgate, g['ret_gn_g'] = _pw(gn_bwd, [sv['yf'], sv['yb'], zr, w['ret_gn_g'], dys[0]],
                                    [_row(T, 256, hd), _row(T, 256, hd), _row(T, 256, lambda j: 8 + j), _par(256, hd),
                                     _row(T, 256, hd)],
                                    [SDS((S, 1024), bf16), SDS((S, 1024), bf16), SDS((1, 1024), f32)],
                                    [_row(T, 256, hd), _row(T, 256, hd), _par(256, hd)], (RET_HEADS, I), n_acc=1, name=nm("ret_gn"))
    qf, kf, vf, lgf = _ret_dir_bwd(zr, w['lg'], tabs['cos_r'], tabs['sin_r'], dyr, sv['stf'], reverse=False, name=nm("ret_f"))
    qb, kb, vb, lgb = _ret_dir_bwd(zr, w['lg'], tabs['cos_r'], tabs['sin_r'], dyr, sv['stb'], reverse=True, name=nm("ret_b"))
    g['lg'] = jnp.stack([lgf[:, 0, 0], lgb[:, 0, 0]])
    dzr = _pw(lambda a, b, c, d, e, f, gt: jnp.concatenate([a + b, c + d, e + f, gt], axis=1),
              [qf, qb, kf, kb, vf, vb, dgate], [_row(256, 512)] * 4 + [_row(256, D)] * 3,
              [SDS((S, 3072), bf16)], [_row(256, 3072)], (1, S // 256), name=nm("ret_dz"))[0]

    h = sv['h']
    g['W_ret'] = _mmT(h, dzr, name=nm("in_ret_dw"))
    g['W_mla'] = _mmT(h, dzm, name=nm("in_mla_dw"))
    g['W_s5'] = _mmT(sv['h_seg'], dzs_seg, name=nm("in_s5_dw"))
    g['W_gate'] = _mmT(h, dzg, name=nm("in_gate_dw"))
    dh = _mm(dzr, w['W_ret'], tb=True, name=nm("in_ret_dx"))
    dh = _mm(dzm, w['W_mla'], tb=True, res=dh, name=nm("in_mla_dx"))
    dh = _mm(dzs, w['W_s5'], tb=True, res=dh, name=nm("in_s5_dx"))
    dh = _mm(dzg, w['W_gate'], tb=True, res=dh, name=nm("in_gate_dx"))
    dx, g['norm1_g'] = _rmsnorm_bwd(sv['x'], w['norm1_g'], dh, dx1, name=nm("norm1"))
    return dx, g


def _loss_head(x, tgt, gain, *, name):
    S, W = x.shape
    T = min(S, 512)

    def loss_fn(xv, gv, tv):
        return 0.5 * jnp.sum(jnp.mean(jnp.square(_rms(xv, gv) - tv), axis=-1, keepdims=True), axis=0, keepdims=True)

    def fn(xv, gv, tv):
        lv, vjp = jax.vjp(lambda a, b: loss_fn(a, b, tv), xv, gv)
        dx, dg = vjp(jnp.ones((1, 1), f32))
        return dx, jnp.broadcast_to(lv, (1, 128)), dg

    return _pw(fn, [x, gain, tgt], [_row(T, W), _par(W), _row(T, W)],
               [SDS((S, W), f32), SDS((1, 128), f32), SDS((1, W), f32)], [_row(T, W), _par(128), _par(W)],
               (1, S // T), n_acc=2, name=name)


def _rope_tabs(S):
    def tab(dim):
        inv = 1.0 / (ROPE_THETA ** (jnp.arange(0, dim, 2, dtype=f32) / dim))
        ang = jnp.arange(S, dtype=f32)[:, None] * inv[None, :]
        return jnp.cos(ang), jnp.sin(ang)

    cr, sr = tab(RET_DK)
    cm, sm = tab(MLA_ROPE)
    z = jnp.zeros((S, 64), f32)
    return {'cos_r': jnp.concatenate([cr, cr], axis=1), 'sin_r': jnp.concatenate([-sr, sr], axis=1),
            'cos_m': jnp.concatenate([cm, cm, z], axis=1), 'sin_m': jnp.concatenate([-sm, sm, z], axis=1),
            'seg_perm': _seg_perm(512, False), 'seg_unperm': _seg_perm(512, True)}


def _bd_B(bb):
    b5 = bb.reshape(16, 2, 8, 8, 64)
    return jnp.einsum('cdjgp,gh->djgchp', b5, jnp.eye(8, dtype=bb.dtype)).reshape(2, 8, 128, 512)


def _bd_B_t(dBB):
    return jnp.einsum('djgcgp->cdjgp', dBB.reshape(2, 8, 8, 16, 8, 64)).reshape(16, 8192)


def _bd_C(c):
    c5 = c.reshape(2, 8, 8, 16, 64)
    return jnp.einsum('djgcp,gh->djgphc', c5, jnp.eye(8, dtype=c.dtype)).reshape(2, 8, 512, 128)


def _s5_rows(p, l):
    a_re = p['s5_a_re'][l].reshape(1, 8192)
    a_im = p['s5_a_im'][l].reshape(1, 8192)
    ldt = jnp.broadcast_to(p['s5_log_dt'][l][:, :, None], (2, S5_G, S5_P)).reshape(1, 8192)
    b_re = p['s5_b_re'][l].transpose(3, 0, 1, 2).reshape(16, 8192)
    b_im = p['s5_b_im'][l].transpose(3, 0, 1, 2).reshape(16, 8192)
    return a_re, a_im, ldt, b_re, b_im


def _layer_weights(big, p, l):
    w_in = big['w_in'][l]
    z = lambda n: jnp.zeros((D, n), w_in.dtype)
    w = {
        'W_ret': w_in[:, 0:3072],
        'W_mla': jnp.concatenate([w_in[:, 3072:3456], z(128), w_in[:, 3456:3712], w_in[:, 3712:3776], z(64)], axis=1),
        'W_s5': w_in[:, 3776:4800],
        'W_gate': w_in[:, 4800:7872],
        'W_uq': jnp.pad(big['mla_w_uq'][l].reshape(MLA_Q_LORA, MLA_HEADS, 192), ((0, 0), (0, 0), (0, 64))).reshape(MLA_Q_LORA, 2048),
        'W_ukv': big['mla_w_ukv'][l],
        'W_glu': big['s5_w_glu'][l],
        'W_br': [big['w_branch'][l, i] for i in range(3)],
        'W_out': big['w_out'][l],
        'W_gu': big['ffn_w_gu'][l],
        'W_down': big['ffn_w_down'][l],
    }
    for n in ('norm1_g', 'ret_gn_g', 'mla_q_norm_g', 'mla_kv_norm_g', 's5_d', 'norm2_g'):
        w[n] = p[n][l][None, :]
    w['lg'] = jax.nn.log_sigmoid(p['ret_decay'][l])
    rows = _s5_rows(p, l)
    abr, abi, bbr, bbi, pwr, pwi = _s5_param_fwd(*rows, name=f"L{l}_s5_param")
    flip = lambda t, first: jnp.concatenate([t[::-1, :4096], t[:, 4096:]] if first else [t[:, :4096], t[::-1, 4096:]], axis=1)
    w['s5'] = {'abr': abr, 'abi': abi, 'BBr': _bd_B(bbr).astype(bf16), 'BBi': _bd_B(bbi).astype(bf16),
               'CCr': _bd_C(p['s5_c_re'][l]).astype(bf16), 'CCi': _bd_C(p['s5_c_im'][l]).astype(bf16),
               'pwr_f': flip(pwr, False), 'pwi_f': flip(pwi, False), 'pwr_a': flip(pwr, True), 'pwi_a': flip(pwi, True),
               'rows': rows}
    return w


def _layer_grads(g, w, p, l):
    out = {}
    m = g['W_mla']
    out['w_in'] = jnp.concatenate([g['W_ret'], m[:, 0:384], m[:, 512:768], m[:, 768:832], g['W_s5'], g['W_gate']], axis=1)
    out['mla_w_uq'] = g['W_uq'].reshape(MLA_Q_LORA, MLA_HEADS, 256)[:, :, :192].reshape(MLA_Q_LORA, 1536)
    out['mla_w_ukv'] = g['W_ukv']
    out['s5_w_glu'] = g['W_glu']
    out['w_branch'] = jnp.stack(g['W_br'])
    out['w_out'] = g['W_out']
    out['ffn_w_gu'] = g['W_gu']
    out['ffn_w_down'] = g['W_down']
    for n in ('norm1_g', 'ret_gn_g', 'mla_q_norm_g', 'mla_kv_norm_g', 's5_d', 'norm2_g'):
        out[n] = g[n][0]
    out['ret_decay'] = g['lg'] * jax.nn.sigmoid(-p['ret_decay'][l])
    (fB_r, fB_i, fC_r, fC_i, fa_r, fa_i), (bB_r, bB_i, bC_r, bC_i, ba_r, ba_i) = g['s5']
    cat = lambda a, b: jnp.concatenate([a, b], axis=0)
    d_bbr = _bd_B_t(cat(fB_r, bB_r))
    d_bbi = _bd_B_t(cat(fB_i, bB_i))
    to_c = lambda t: _bd_B_t(t).reshape(16, 2, S5_G, S5_P).transpose(1, 2, 0, 3)
    out['s5_c_re'] = to_c(cat(fC_r, bC_r))
    out['s5_c_im'] = to_c(cat(fC_i, bC_i))
    d_abr = jnp.concatenate([fa_r, ba_r], axis=1)
    d_abi = jnp.concatenate([fa_i, ba_i], axis=1)
    da_re, da_im, dldt, db_re, db_im = _s5_param_bwd(*w['s5']['rows'], d_abr, d_abi, d_bbr, d_bbi, name=f"L{l}_b_s5_param")
    out['s5_a_re'] = da_re.reshape(2, S5_G, S5_P)
    out['s5_a_im'] = da_im.reshape(2, S5_G, S5_P)
    out['s5_log_dt'] = dldt.reshape(2, S5_G, S5_P).sum(axis=-1)
    out['s5_b_re'] = db_re.reshape(16, 2, S5_G, S5_P).transpose(1, 2, 3, 0)
    out['s5_b_im'] = db_im.reshape(16, 2, S5_G, S5_P).transpose(1, 2, 3, 0)
    return out


def _local_step(x, tgt, big, p):
    S = x.shape[0]
    assert S % 512 == 0
    tabs = _rope_tabs(S)
    ws, svs = [], []
    h = x
    for l in range(DEPTH):
        w = _layer_weights(big, p, l)
        h, sv = _layer_fwd(h, w, tabs, l)
        ws.append(w)
        svs.append(sv)
    dx, lossv, dfinal = _loss_head(h, tgt, p['final_g'][None, :], name="loss_head")
    per_layer = [None] * DEPTH
    for l in reversed(range(DEPTH)):
        dx, g = _layer_bwd(dx, ws[l], tabs, svs[l], l)
        per_layer[l] = _layer_grads(g, ws[l], p, l)
    return lossv[0, 0], dx, per_layer, dfinal[0]


_ANY = pl.BlockSpec(memory_space=pl.ANY)


def _place():
    x, y, c = lax.axis_index("x"), lax.axis_index("y"), lax.axis_index("c")
    return x, y, c, [(1 - x, y), (x, 1 - y), (1 - x, 1 - y)]


def _allgather4(arrs, *, name):
    n = len(arrs)

    def body(*refs):
        ins, outs = refs[:n], refs[n:2 * n]
        send, recv, loc = refs[2 * n:]
        x, y, c, chips = _place()
        me = 2 * x + y

        def remote(a, k, slot):
            px, py = chips[k]
            return pltpu.make_async_remote_copy(src_ref=ins[a], dst_ref=outs[a].at[slot], send_sem=send.at[a, k],
                                                recv_sem=recv.at[a, k], device_id=(px, py, c), device_id_type=MESH)

        mine = [pltpu.make_async_copy(ins[a], outs[a].at[me], loc.at[a]) for a in range(n)]
        for cp in mine:
            cp.start()
        sends = [remote(a, k, me) for a in range(n) for k in range(3)]
        for cp in sends:
            cp.start()
        for a in range(n):
            for k, (px, py) in enumerate(chips):
                remote(a, k, 2 * px + py).wait_recv()
        for cp in sends:
            cp.wait_send()
        for cp in mine:
            cp.wait()

    return pl.pallas_call(
        body, in_specs=[_ANY] * n, out_specs=[_ANY] * n, out_shape=[SDS((4,) + a.shape, a.dtype) for a in arrs],
        scratch_shapes=[pltpu.SemaphoreType.DMA((n, 3)), pltpu.SemaphoreType.DMA((n, 3)), pltpu.SemaphoreType.DMA((n,))],
        name=name)(*arrs)


def _rs_exchange(parts, *, name):
    n = len(parts)

    def body(*refs):
        ins, gots = refs[:n], refs[n:2 * n]
        send, recv = refs[2 * n:]
        x, y, c, chips = _place()

        def remote(a, k):
            px, py = chips[k]
            return pltpu.make_async_remote_copy(src_ref=ins[a].at[2 * px + py], dst_ref=gots[a].at[k], send_sem=send.at[a, k],
                                                recv_sem=recv.at[a, k], device_id=(px, py, c), device_id_type=MESH)

        sends = [remote(a, k) for a in range(n) for k in range(3)]
        for cp in sends:
            cp.start()
        for cp in sends:
            cp.wait_recv()
        for cp in sends:
            cp.wait_send()

    return pl.pallas_call(
        body, in_specs=[_ANY] * n, out_specs=[_ANY] * n, out_shape=[SDS((3,) + a.shape[1:], a.dtype) for a in parts],
        scratch_shapes=[pltpu.SemaphoreType.DMA((n, 3)), pltpu.SemaphoreType.DMA((n, 3))], name=name)(*parts)


def _gather_split(arrs, *, name):
    n = len(arrs)

    def body(*refs):
        ins, outs = refs[:n], refs[n:2 * n]
        s_ici, r_ici, s_sib, r_sib, loc = refs[2 * n:]
        x, y, c, chips = _place()
        me = 2 * x + y
        ids = [2 * px + py for px, py in chips] + [me]

        def over_ici(a, k, slot):
            px, py = chips[k]
            return pltpu.make_async_remote_copy(src_ref=ins[a].at[c], dst_ref=outs[a].at[slot, c], send_sem=s_ici.at[a, k],
                                                recv_sem=r_ici.at[a, k], device_id=(px, py, c), device_id_type=MESH)

        def to_sibling(a, k, half, src=None):
            blk = outs[a].at[ids[k], half]
            return pltpu.make_async_remote_copy(src_ref=blk if src is None else src, dst_ref=blk, send_sem=s_sib.at[a, k],
                                                recv_sem=r_sib.at[a, k], device_id=(x, y, 1 - c), device_id_type=MESH)

        sends = [over_ici(a, k, me) for a in range(n) for k in range(3)]
        sends += [to_sibling(a, 3, c, src=ins[a].at[c]) for a in range(n)]
        for cp in sends:
            cp.start()
        mine = [pltpu.make_async_copy(ins[a].at[c], outs[a].at[me, c], loc.at[a]) for a in range(n)]
        for cp in mine:
            cp.start()
        for a in range(n):
            for k in range(3):
                over_ici(a, k, ids[k]).wait_recv()
                fwd = to_sibling(a, k, c)
                fwd.start()
                sends.append(fwd)
        for a in range(n):
            for k in range(4):
                to_sibling(a, k, 1 - c).wait_recv()
        for cp in sends:
            cp.wait_send()
        for cp in mine:
            cp.wait()

    dma = pltpu.SemaphoreType.DMA
    return pl.pallas_call(
        body, in_specs=[_ANY] * n, out_specs=[_ANY] * n, out_shape=[SDS((4,) + a.shape, a.dtype) for a in arrs],
        scratch_shapes=[dma((n, 3)), dma((n, 3)), dma((n, 4)), dma((n, 4)), dma((n,))], name=name)(*arrs)


def _swap_halves(parts, *, name):
    n = len(parts)

    def body(*refs):
        ins, gots = refs[:n], refs[n:2 * n]
        send, recv = refs[2 * n:]
        x, y, c, _ = _place()
        cps = [pltpu.make_async_remote_copy(src_ref=ins[a].at[q, 1 - c], dst_ref=gots[a].at[q], send_sem=send.at[a, q],
                                            recv_sem=recv.at[a, q], device_id=(x, y, 1 - c), device_id_type=MESH)
               for a in range(n) for q in range(4)]
        for cp in cps:
            cp.start()
        for cp in cps:
            cp.wait_recv()
        for cp in cps:
            cp.wait_send()

    dma = pltpu.SemaphoreType.DMA
    return pl.pallas_call(
        body, in_specs=[_ANY] * n, out_specs=[_ANY] * n, out_shape=[SDS((4,) + a.shape[2:], a.dtype) for a in parts],
        scratch_shapes=[dma((n, 4)), dma((n, 4))], name=name)(*parts)


def _sibling_copy(arrs, *, name):
    n = len(arrs)

    def body(*refs):
        ins, outs = refs[:n], refs[n:2 * n]
        send, recv = refs[2 * n:]
        x, y, c, _ = _place()
        cps = [pltpu.make_async_remote_copy(src_ref=ins[a], dst_ref=outs[a], send_sem=send.at[a], recv_sem=recv.at[a],
                                            device_id=(x, y, 1 - c), device_id_type=MESH) for a in range(n)]
        for cp in cps:
            cp.start()
        for cp in cps:
            cp.wait_recv()
        for cp in cps:
            cp.wait_send()

    dma = pltpu.SemaphoreType.DMA
    return pl.pallas_call(
        body, in_specs=[_ANY] * n, out_specs=[_ANY] * n, out_shape=[SDS(a.shape, a.dtype) for a in arrs],
        scratch_shapes=[dma((n,)), dma((n,))], name=name)(*arrs)


def _row_tile(R):
    return R if R <= 256 else next(t for t in (256, 128, 64, 32, 16) if R % t == 0)


def _add2(a, b, *, name):
    R, W = a.shape
    tr = _row_tile(R)
    return _pw(lambda p, q: p.astype(f32) + q.astype(f32), [a, b], [_row(tr, W)] * 2, [SDS((R, W), a.dtype)], [_row(tr, W)],
               (1, R // tr), name=name)[0]


def _sum4(own, got, *, name):
    R, W = own.shape
    tr = _row_tile(R)
    g3 = lambda k: pl.BlockSpec((None, tr, W), lambda j, i: (k, i, 0))
    up = lambda t: t.astype(f32)
    return _pw(lambda a, b, c, d: ((up(a) + up(b)) + up(c)) + up(d), [own, got, got, got], [_row(tr, W), g3(0), g3(1), g3(2)],
               [SDS((R, W), f32)], [_row(tr, W)], (1, R // tr), name=name)[0]


def _adamw(g, w, m, v, *, name):
    R, W = w.shape
    tr = _row_tile(R)

    def fn(gv, wv, mv, vv):
        m2 = ADAM_B1 * mv + (1.0 - ADAM_B1) * gv
        v2 = ADAM_B2 * vv + (1.0 - ADAM_B2) * jnp.square(gv)
        m_hat = m2 / (1.0 - ADAM_B1 ** ADAM_STEP)
        v_hat = v2 / (1.0 - ADAM_B2 ** ADAM_STEP)
        return -ADAM_LR * (m_hat / (jnp.sqrt(v_hat) + ADAM_EPS) + ADAM_WD * wv), m2, v2

    return _pw(fn, [g, w, m, v], [_row(tr, W)] * 4, [SDS((R, W), f32)] * 3, [_row(tr, W)] * 3, (1, R // tr), name=name)


def _to_parts(g, axis):
    shp = g.shape
    g = g.reshape(shp[:axis] + (4, shp[axis] // 4) + shp[axis + 1:])
    return jnp.moveaxis(g, axis, 0)


def _from_parts(pt, axis):
    g = jnp.moveaxis(pt, 0, axis)
    shp = g.shape
    return g.reshape(shp[:axis] + (4 * shp[axis + 1],) + shp[axis + 2:])


def kernel(x, norm1_g, w_in, ret_decay, ret_gn_g, mla_q_norm_g, mla_w_uq, mla_kv_norm_g, mla_w_ukv, s5_a_re, s5_a_im, s5_log_dt, s5_b_re, s5_b_im, s5_c_re, s5_c_im, s5_d, s5_w_glu, w_branch, w_out, norm2_g, ffn_w_gu, ffn_w_down, final_g, loss_target, m_norm1_g, m_w_in, m_ret_decay, m_ret_gn_g, m_mla_q_norm_g, m_mla_w_uq, m_mla_kv_norm_g, m_mla_w_ukv, m_s5_a_re, m_s5_a_im, m_s5_log_dt, m_s5_b_re, m_s5_b_im, m_s5_c_re, m_s5_c_im, m_s5_d, m_s5_w_glu, m_w_branch, m_w_out, m_norm2_g, m_ffn_w_gu, m_ffn_w_down, m_final_g, v_norm1_g, v_w_in, v_ret_decay, v_ret_gn_g, v_mla_q_norm_g, v_mla_w_uq, v_mla_kv_norm_g, v_mla_w_ukv, v_s5_a_re, v_s5_a_im, v_s5_log_dt, v_s5_b_re, v_s5_b_im, v_s5_c_re, v_s5_c_im, v_s5_d, v_s5_w_glu, v_w_branch, v_w_out, v_norm2_g, v_ffn_w_gu, v_ffn_w_down, v_final_g):
    wv = dict(zip(W_NAMES, (norm1_g, w_in, ret_decay, ret_gn_g, mla_q_norm_g, mla_w_uq, mla_kv_norm_g, mla_w_ukv, s5_a_re, s5_a_im,
                            s5_log_dt, s5_b_re, s5_b_im, s5_c_re, s5_c_im, s5_d, s5_w_glu, w_branch, w_out, norm2_g, ffn_w_gu,
                            ffn_w_down, final_g)))
    mv = dict(zip(W_NAMES, (m_norm1_g, m_w_in, m_ret_decay, m_ret_gn_g, m_mla_q_norm_g, m_mla_w_uq, m_mla_kv_norm_g, m_mla_w_ukv,
                            m_s5_a_re, m_s5_a_im, m_s5_log_dt, m_s5_b_re, m_s5_b_im, m_s5_c_re, m_s5_c_im, m_s5_d, m_s5_w_glu,
                            m_w_branch, m_w_out, m_norm2_g, m_ffn_w_gu, m_ffn_w_down, m_final_g)))
    vv = dict(zip(W_NAMES, (v_norm1_g, v_w_in, v_ret_decay, v_ret_gn_g, v_mla_q_norm_g, v_mla_w_uq, v_mla_kv_norm_g, v_mla_w_ukv,
                            v_s5_a_re, v_s5_a_im, v_s5_log_dt, v_s5_b_re, v_s5_b_im, v_s5_c_re, v_s5_c_im, v_s5_d, v_s5_w_glu,
                            v_w_branch, v_w_out, v_norm2_g, v_ffn_w_gu, v_ffn_w_down, v_final_g)))
    big_names = list(BIG)

    my_c = lax.axis_index("c")
    my_chip = 2 * lax.axis_index("x") + lax.axis_index("y")
    shards = [wv[n].astype(bf16) for n in big_names]
    gathered = _gather_split(shards, name="gather_weights")
    big = {n: _from_parts(gt, BIG[n]) for n, gt in zip(big_names, gathered)}
    small = {n: wv[n] for n in SMALL}

    loss_local, dx, layer_grads, d_final = _local_step(x[0], loss_target[0], big, small)
    grads = {n: jnp.stack([layer_grads[l][n] for l in range(DEPTH)]) for n in SMALL if n != 'final_g'}
    grads['final_g'] = d_final

    n_rows = {n: -(-math.prod(wv[n].shape) // 1024) * 8 for n in SMALL}
    used = sum(n_rows.values())
    rows_q = -(-(used + 8) // (4 * 128)) * 128

    def as_rows(d, tail=None):
        blocks = [jnp.pad(d[n].reshape(-1), (0, n_rows[n] * 128 - math.prod(wv[n].shape))).reshape(n_rows[n], 128) for n in SMALL]
        blocks.append(jnp.zeros((8, 128), f32) if tail is None else tail)
        blocks.append(jnp.zeros((4 * rows_q - used - 8, 128), f32))
        return jnp.concatenate(blocks, axis=0)

    loss_rows = jnp.full((8, 128), loss_local, f32)
    parts = [jnp.stack([_to_parts(layer_grads[l][n].astype(bf16), BIG[n] - 1) for l in range(DEPTH)], axis=1) for n in big_names]
    parts.append(as_rows(grads, loss_rows).reshape(4, 2, rows_q // 2, 128))
    n_arr = len(parts)
    two_d = lambda a: a.reshape(-1, a.shape[-1])
    theirs = _swap_halves(parts, name="grad_swap_halves")
    mine = [jnp.where(my_c == 0, p[:, 0], p[:, 1]) for p in parts]
    chip_sums = [_add2(two_d(mine[a]), two_d(theirs[a]), name=f"grad_add2_{a}").reshape(theirs[a].shape) for a in range(n_arr)]
    got = _rs_exchange(chip_sums, name="grad_exchange")

    def pick_chip(s):
        r = s[0]
        for q in range(1, 4):
            r = jnp.where(my_chip == q, s[q], r)
        return r

    own = [pick_chip(s) for s in chip_sums]
    sums = [_sum4(two_d(own[a]), got[a].reshape(3, -1, got[a].shape[-1]), name=f"grad_sum4_{a}") for a in range(n_arr)]
    other = _sibling_copy(sums, name="grad_sibling")
    full = [jnp.stack([jnp.where(my_c == 0, sums[a], other[a]), jnp.where(my_c == 0, other[a], sums[a])]) for a in range(n_arr)]

    out_g, out_d, out_m, out_v = {}, {}, {}, {}
    for a, n in enumerate(big_names):
        shp = wv[n].shape
        res = _adamw(two_d(full[a]), two_d(wv[n]), two_d(mv[n]), two_d(vv[n]), name=f"adamw_{n}")
        out_g[n] = full[a].reshape(shp)
        out_d[n], out_m[n], out_v[n] = [r.reshape(shp) for r in res]
    g_small = _allgather4([full[-1].reshape(rows_q, 128)], name="gather_small_grads")[0].reshape(4 * rows_q, 128)
    loss = g_small[used, 0]
    off = 0
    for n in SMALL:
        shp = wv[n].shape
        k = math.prod(shp)
        flat2 = (k // 128, 128) if k % 128 == 0 else (1, k)
        g_n = g_small[off:off + n_rows[n]].reshape(-1)[:k].reshape(flat2)
        res = _adamw(g_n, wv[n].reshape(flat2), mv[n].reshape(flat2), vv[n].reshape(flat2), name=f"adamw_{n}")
        out_g[n] = g_n.reshape(shp)
        out_d[n], out_m[n], out_v[n] = [r.reshape(shp) for r in res]
        off += n_rows[n]
    return (loss, dx[None], *[out_g[n] for n in W_NAMES], *[out_d[n] for n in W_NAMES], *[out_m[n] for n in W_NAMES],
            *[out_v[n] for n in W_NAMES])
```

```python
import functools
import math

import jax
import jax.numpy as jnp
from jax import lax
from jax.experimental import pallas as pl
from jax.experimental.pallas import tpu as pltpu

f32 = jnp.float32
bf16 = jnp.bfloat16
SDS = jax.ShapeDtypeStruct
MESH = pl.DeviceIdType.MESH

D = 1024
DEPTH = 2
RMS_EPS = 1e-6
GN_EPS = 1e-5
ROPE_THETA = 10000.0
RET_HEADS = 4
RET_DK = 128
RET_DV = 256
RET_CHUNK = 128
MLA_HEADS = 8
MLA_Q_LORA = 384
MLA_KV_LORA = 256
MLA_NOPE = 128
MLA_ROPE = 64
MLA_V = 128
MLA_QW = 256
S5_G = 64
S5_P = 64
S5_C = 16
S5_NJ = 8
S5_SEG = 8
FFN_H = 2816
ADAM_LR = 0.001
ADAM_B1 = 0.9
ADAM_B2 = 0.999
ADAM_EPS = 1e-08
ADAM_WD = 0.01
ADAM_STEP = 10
VMEM_BIG = 56 * 1024 * 1024

W_NAMES = ['norm1_g', 'w_in', 'ret_decay', 'ret_gn_g', 'mla_q_norm_g', 'mla_w_uq', 'mla_kv_norm_g', 'mla_w_ukv',
           's5_a_re', 's5_a_im', 's5_log_dt', 's5_b_re', 's5_b_im', 's5_c_re', 's5_c_im', 's5_d', 's5_w_glu',
           'w_branch', 'w_out', 'norm2_g', 'ffn_w_gu', 'ffn_w_down', 'final_g']
BIG = {'w_in': 2, 'mla_w_uq': 2, 'mla_w_ukv': 2, 's5_w_glu': 2, 'w_branch': 2, 'w_out': 1, 'ffn_w_gu': 2, 'ffn_w_down': 1}
SMALL = [n for n in W_NAMES if n not in BIG]


TILE_BYTES = 6 * 1024 * 1024


def _pick(n, cands=(512, 384, 256, 128), cap=None):
    if n <= 1024 and (cap is None or n <= cap):
        return n
    for c in cands:
        if n % c == 0 and (cap is None or c <= cap):
            return c
    raise ValueError(n)


WIDE = (1408, 1024, 768, 512, 384, 256, 128)


def _params(sem, vmem=None):
    return pltpu.CompilerParams(dimension_semantics=sem, vmem_limit_bytes=vmem)


def _mm(a, b, *, tb=False, res=None, out_dtype=f32, name):
    M, K = a.shape
    N = b.shape[0] if tb else b.shape[1]
    tk = K if K <= 3072 else _pick(K, (1408, 1024, 512))
    nk = K // tk
    tn = _pick(N, WIDE, cap=TILE_BYTES // (tk * b.dtype.itemsize))
    tm = _pick(M)
    if M % 1024 == 0 and 1024 * tk * a.dtype.itemsize <= 4 * 1024 * 1024 and 1024 * tn * 4 <= TILE_BYTES:
        tm = 1024
    assert M % tm == 0 and N % tn == 0 and K % tk == 0

    def body(*refs):
        if res is None:
            a_ref, b_ref, o_ref, acc = refs
        else:
            a_ref, b_ref, r_ref, o_ref, acc = refs
        k = pl.program_id(2)
        dn = (((1,), (1 if tb else 0,)), ((), ()))
        part = lax.dot_general(a_ref[...].astype(bf16), b_ref[...].astype(bf16), dn, preferred_element_type=f32)

        @pl.when(k == 0)
        def _():
            acc[...] = part

        @pl.when(k > 0)
        def _():
            acc[...] += part

        @pl.when(k == nk - 1)
        def _():
            v = acc[...]
            if res is not None:
                v = v + r_ref[...]
            o_ref[...] = v.astype(out_dtype)

    in_specs = [pl.BlockSpec((tm, tk), lambda i, j, k: (i, k)),
                pl.BlockSpec((tn, tk), lambda i, j, k: (j, k)) if tb else pl.BlockSpec((tk, tn), lambda i, j, k: (k, j))]
    args = [a, b]
    if res is not None:
        in_specs.append(pl.BlockSpec((tm, tn), lambda i, j, k: (i, j)))
        args.append(res)
    return pl.pallas_call(
        body, grid=(M // tm, N // tn, nk), in_specs=in_specs,
        out_specs=pl.BlockSpec((tm, tn), lambda i, j, k: (i, j)),
        out_shape=SDS((M, N), out_dtype), scratch_shapes=[pltpu.VMEM((tm, tn), f32)],
        compiler_params=_params(("parallel", "parallel", "arbitrary"), VMEM_BIG), name=name)(*args)


def _mmT(a, b, *, name):
    S, M = a.shape
    N = b.shape[1]
    tn = _pick(N, WIDE)
    tm = _pick(M, WIDE, cap=TILE_BYTES // (tn * 4))
    tk = min(S, 1024)
    nk = S // tk

    def body(a_ref, b_ref, o_ref):
        k = pl.program_id(2)
        part = lax.dot_general(a_ref[...].astype(bf16), b_ref[...].astype(bf16), (((0,), (0,)), ((), ())),
                               preferred_element_type=f32)

        @pl.when(k == 0)
        def _():
            o_ref[...] = part

        @pl.when(k > 0)
        def _():
            o_ref[...] += part

    return pl.pallas_call(
        body, grid=(M // tm, N // tn, nk),
        in_specs=[pl.BlockSpec((tk, tm), lambda i, j, k: (k, i)), pl.BlockSpec((tk, tn), lambda i, j, k: (k, j))],
        out_specs=pl.BlockSpec((tm, tn), lambda i, j, k: (i, j)),
        out_shape=SDS((M, N), f32),
        compiler_params=_params(("parallel", "parallel", "arbitrary"), VMEM_BIG), name=name)(a, b)


def _pw(fn, ins, in_specs, outs, out_specs, grid, *, n_acc=0, name):
    n_in = len(ins)
    n_out = len(outs)

    def body(*refs):
        vals = fn(*[r[...].astype(f32) if r.dtype == bf16 else r[...] for r in refs[:n_in]])
        if not isinstance(vals, (tuple, list)):
            vals = (vals,)
        orefs = refs[n_in:]
        for r, v in zip(orefs[:n_out - n_acc], vals[:n_out - n_acc]):
            r[...] = v.astype(r.dtype)
        if n_acc:
            i = pl.program_id(1)

            @pl.when(i == 0)
            def _():
                for r, v in zip(orefs[n_out - n_acc:], vals[n_out - n_acc:]):
                    r[...] = v.astype(r.dtype)

            @pl.when(i > 0)
            def _():
                for r, v in zip(orefs[n_out - n_acc:], vals[n_out - n_acc:]):
                    r[...] += v.astype(r.dtype)

    res = pl.pallas_call(
        body, grid=grid, in_specs=in_specs, out_specs=out_specs, out_shape=outs,
        compiler_params=_params(("parallel", "arbitrary"), VMEM_BIG), name=name)(*ins)
    return res


def _row(T, w, col=None):
    if col is None:
        return pl.BlockSpec((T, w), lambda j, i: (i, 0))
    return pl.BlockSpec((T, w), lambda j, i: (i, col(j)))


def _par(w, col=None):
    if col is None:
        return pl.BlockSpec((1, w), lambda j, i: (0, 0))
    return pl.BlockSpec((1, w), lambda j, i: (0, col(j)))


def _rms(x, g):
    return x * lax.rsqrt(jnp.mean(x * x, axis=-1, keepdims=True) + RMS_EPS) * g


def _rope(x, cos, sinm, half):
    if half == 64:
        partner = pltpu.roll(x, 64, axis=1)
    else:
        lane = lax.broadcasted_iota(jnp.int32, x.shape, 1)
        partner = jnp.where((lane % (2 * half)) < half, pltpu.roll(x, 128 - half, axis=1), pltpu.roll(x, half, axis=1))
    return x * cos + partner * sinm


def _rope_t(x, cos, sinm, half):
    return _rope(x, cos, -sinm, half)


def _rmsnorm_fwd(x, g, *, name):
    S, W = x.shape
    T = min(S, 512)
    return _pw(lambda xv, gv: _rms(xv, gv), [x, g], [_row(T, W), _par(W)], [SDS((S, W), bf16)], [_row(T, W)],
               (1, S // T), name=name)[0]


def _rmsnorm_bwd(x, g, dh, dres, *, name):
    S, W = x.shape
    T = min(S, 512)

    def fn(xv, gv, dhv, drv):
        _, vjp = jax.vjp(_rms, xv, gv)
        dx, dg = vjp(dhv)
        return dx + drv, dg

    return _pw(fn, [x, g, dh, dres], [_row(T, W), _par(W), _row(T, W), _row(T, W)],
               [SDS((S, W), f32), SDS((1, W), f32)], [_row(T, W), _par(W)], (1, S // T), n_acc=1, name=name)


def _ret_tables(lg, reverse):
    C = RET_CHUNK
    ii = lax.broadcasted_iota(jnp.int32, (C, C), 0).astype(f32)
    jj = lax.broadcasted_iota(jnp.int32, (C, C), 1).astype(f32)
    if not reverse:
        E = ii - jj
        mask = E >= 0
        eq = ii + 1.0
        ek = (C - 1.0) - ii
    else:
        E = jj - ii
        mask = E > 0
        eq = C - ii
        ek = ii
    Dm = jnp.where(mask, jnp.exp(jnp.where(mask, E, 0.0) * lg), 0.0)
    Em = jnp.where(mask, E, 0.0)
    qw = jnp.exp(eq * lg)
    kw = jnp.exp(ek * lg)
    qw2 = jnp.concatenate([qw, qw], axis=1)
    return Dm, Em, eq, ek, qw, kw, qw2, jnp.exp(C * lg)


def _dot(a, b, dims):
    return lax.dot_general(a.astype(bf16), b.astype(bf16), (dims, ((), ())), preferred_element_type=f32)


NN = ((1,), (0,))
NT = ((1,), (1,))
TN = ((0,), (0,))


def _ret_dir_fwd(zr, lg, cos, sinm, *, reverse, name):
    S = zr.shape[0]
    C = RET_CHUNK
    TB = min(S, 512)
    nc = TB // C
    NB = S // TB
    d = 1 if reverse else 0
    scale = RET_DK ** -0.5

    def tb(b):
        return (NB - 1 - b) if reverse else b

    def body(lg_ref, q_ref, k_ref, v_ref, cos_ref, sin_ref, y_ref, st_ref, state):
        h = pl.program_id(0)
        b = pl.program_id(1)

        @pl.when(b == 0)
        def _():
            state[...] = jnp.zeros_like(state)

        Dm, _, _, _, _, kw, qw2, gC = _ret_tables(lg_ref[d, h], reverse)
        order = range(nc - 1, -1, -1) if reverse else range(nc)
        for c in order:
            rows = pl.ds(c * C, C)
            q = _rope(q_ref[rows, :], cos_ref[rows, :], sin_ref[rows, :], 64) * scale
            k = _rope(k_ref[rows, :], cos_ref[rows, :], sin_ref[rows, :], 64)
            v = v_ref[rows, :]
            st = state[...]
            st_ref[0, c] = st
            s = _dot(q, k, NT) * Dm
            o = _dot(s, v, NN) + _dot(q, st, NN) * qw2
            y_ref[rows, :] = o
            state[...] = gC * st + _dot(k * kw, v, TN)

    return pl.pallas_call(
        body, grid=(RET_HEADS, NB),
        in_specs=[pl.BlockSpec(memory_space=pltpu.SMEM),
                  pl.BlockSpec((TB, 128), lambda h, b: (tb(b), h)),
                  pl.BlockSpec((TB, 128), lambda h, b: (tb(b), 4 + h)),
                  pl.BlockSpec((TB, 256), lambda h, b: (tb(b), 4 + h)),
                  pl.BlockSpec((TB, 128), lambda h, b: (tb(b), 0)),
                  pl.BlockSpec((TB, 128), lambda h, b: (tb(b), 0))],
        out_specs=[pl.BlockSpec((TB, 256), lambda h, b: (tb(b), h)),
                   pl.BlockSpec((1, nc, 128, 256), lambda h, b: (h, tb(b), 0, 0))],
        out_shape=[SDS((S, 1024), f32), SDS((RET_HEADS, S // C, 128, 256), f32)],
        scratch_shapes=[pltpu.VMEM((128, 256), f32)],
        compiler_params=_params(("parallel", "arbitrary")), name=name)(lg, zr, zr, zr, cos, sinm)


def _ret_dir_bwd(zr, lg, cos, sinm, dy, states, *, reverse, name):
    S = zr.shape[0]
    C = RET_CHUNK
    TB = min(S, 512)
    nc = TB // C
    NB = S // TB
    d = 1 if reverse else 0
    scale = RET_DK ** -0.5

    def tb(b):
        return b if reverse else (NB - 1 - b)

    def body(lg_ref, q_ref, k_ref, v_ref, cos_ref, sin_ref, dy_ref, st_ref, dq_ref, dk_ref, dv_ref, dlg_ref, dstate):
        h = pl.program_id(0)
        b = pl.program_id(1)

        @pl.when(b == 0)
        def _():
            dstate[...] = jnp.zeros_like(dstate)
            dlg_ref[...] = jnp.zeros_like(dlg_ref)

        Dm, Em, eq, ek, qw, kw, qw2, gC = _ret_tables(lg_ref[d, h], reverse)
        order = range(nc) if reverse else range(nc - 1, -1, -1)
        dlg = jnp.zeros((), f32)
        for c in order:
            rows = pl.ds(c * C, C)
            cs, sn = cos_ref[rows, :], sin_ref[rows, :]
            q = _rope(q_ref[rows, :], cs, sn, 64) * scale
            k = _rope(k_ref[rows, :], cs, sn, 64)
            v = v_ref[rows, :]
            do = dy_ref[rows, :]
            st = st_ref[0, c]
            ds = dstate[...]
            p = _dot(q, k, NT)
            a = p * Dm
            dp = _dot(do, v, NT) * Dm
            dq_cross = _dot(do, st, NT) * qw
            dk_cross = _dot(v, ds, NT) * kw
            dq = _dot(dp, k, NN) + dq_cross
            dk = _dot(dp, q, TN) + dk_cross
            dv = _dot(a, do, TN) + _dot(k * kw, ds, NN)
            dlg = dlg + jnp.sum(dp * p * Em) + jnp.sum(dq_cross * q * eq) + jnp.sum(dk_cross * k * ek) \
                + C * gC * jnp.sum(ds * st)
            dstate[...] = gC * ds + _dot(q * qw, do, TN)
            dq_ref[rows, :] = _rope_t(dq, cs, sn, 64) * scale
            dk_ref[rows, :] = _rope_t(dk, cs, sn, 64)
            dv_ref[rows, :] = dv
        dlg_ref[...] += jnp.full(dlg_ref.shape, dlg, f32)

    return pl.pallas_call(
        body, grid=(RET_HEADS, NB),
        in_specs=[pl.BlockSpec(memory_space=pltpu.SMEM),
                  pl.BlockSpec((TB, 128), lambda h, b: (tb(b), h)),
                  pl.BlockSpec((TB, 128), lambda h, b: (tb(b), 4 + h)),
                  pl.BlockSpec((TB, 256), lambda h, b: (tb(b), 4 + h)),
                  pl.BlockSpec((TB, 128), lambda h, b: (tb(b), 0)),
                  pl.BlockSpec((TB, 128), lambda h, b: (tb(b), 0)),
                  pl.BlockSpec((TB, 256), lambda h, b: (tb(b), h)),
                  pl.BlockSpec((1, nc, 128, 256), lambda h, b: (h, tb(b), 0, 0))],
        out_specs=[pl.BlockSpec((TB, 128), lambda h, b: (tb(b), h)),
                   pl.BlockSpec((TB, 128), lambda h, b: (tb(b), h)),
                   pl.BlockSpec((TB, 256), lambda h, b: (tb(b), h)),
                   pl.BlockSpec((1, 1, 128), lambda h, b: (h, 0, 0))],
        out_shape=[SDS((S, 512), f32), SDS((S, 512), f32), SDS((S, 1024), f32), SDS((RET_HEADS, 1, 128), f32)],
        scratch_shapes=[pltpu.VMEM((128, 256), f32)],
        compiler_params=_params(("parallel", "arbitrary")), name=name)(lg, zr, zr, zr, cos, sinm, dy, states)


def _gn_gate(yf, yb, g, gn):
    y = yf + yb
    mu = jnp.mean(y, axis=-1, keepdims=True)
    var = jnp.mean(jnp.square(y - mu), axis=-1, keepdims=True)
    yn = (y - mu) * lax.rsqrt(var + GN_EPS)
    return jax.nn.silu(g) * (yn * gn)


def _flash_fwd(Q, K, kv, *, name):
    S = Q.shape[0]
    hq = min(S, 256)
    nh = 4 if S % 1024 == 0 else 1
    tq = nh * hq
    tk = min(S, 512)
    nk = S // tk

    def body(q_ref, k_ref, v_ref, o_ref, l_ref, m_s, l_s, acc):
        kk = pl.program_id(2)

        @pl.when(kk == 0)
        def _():
            m_s[...] = jnp.full_like(m_s, -jnp.inf)
            l_s[...] = jnp.zeros_like(l_s)
            acc[...] = jnp.zeros_like(acc)

        k = k_ref[...]
        v = v_ref[...]
        sts = [lax.dot_general(k, q_ref[hf * hq:(hf + 1) * hq, :], (NT, ((), ())), preferred_element_type=f32)
               for hf in range(nh)]
        for hf in range(nh):
            st = sts[hf]
            m_prev = m_s[hf]
            m_new = jnp.maximum(m_prev, jnp.max(st, axis=0, keepdims=True))
            pt = jnp.exp2(st - m_new)
            alpha = jnp.exp2(m_prev - m_new)
            l_s[hf] = alpha * l_s[hf] + jnp.sum(pt, axis=0, keepdims=True)
            acc[hf] = alpha * acc[hf] + lax.dot_general(v, pt.astype(bf16), (TN, ((), ())), preferred_element_type=f32)
            m_s[hf] = m_new

        @pl.when(kk == nk - 1)
        def _():
            for hf in range(nh):
                o_ref[hf * hq:(hf + 1) * hq, :] = jnp.transpose(acc[hf] / l_s[hf]).astype(bf16)
                l_ref[0, :, hf * hq:(hf + 1) * hq] = m_s[hf] + jnp.log2(l_s[hf])

    return pl.pallas_call(
        body, grid=(MLA_HEADS, S // tq, nk),
        in_specs=[pl.BlockSpec((tq, 256), lambda h, i, k: (i, h)),
                  pl.BlockSpec((tk, 256), lambda h, i, k: (k, h)),
                  pl.BlockSpec((tk, 128), lambda h, i, k: (k, 2 * h + 1))],
        out_specs=[pl.BlockSpec((tq, 128), lambda h, i, k: (i, h)), pl.BlockSpec((1, 1, tq), lambda h, i, k: (h, 0, i))],
        out_shape=[SDS((S, 1024), bf16), SDS((MLA_HEADS, 1, S), f32)],
        scratch_shapes=[pltpu.VMEM((nh, 1, hq), f32), pltpu.VMEM((nh, 1, hq), f32), pltpu.VMEM((nh, 128, hq), f32)],
        compiler_params=_params(("parallel", "parallel", "arbitrary")), name=name)(Q, K, kv)


def _attn_delta(dO, O, *, name):
    S = dO.shape[0]
    T = min(S, 512)

    def body(do_ref, o_ref, d_ref):
        ones = jnp.ones((8, 128), bf16)
        for h in range(MLA_HEADS):
            cols = slice(128 * h, 128 * h + 128)
            prod = do_ref[:, cols].astype(f32) * o_ref[:, cols].astype(f32)
            hi = prod.astype(bf16)
            lo = (prod - hi.astype(f32)).astype(bf16)
            row = lax.dot_general(ones, hi, (NT, ((), ())), preferred_element_type=f32) \
                + lax.dot_general(ones, lo, (NT, ((), ())), preferred_element_type=f32)
            d_ref[h] = row[0:1, :]

    return pl.pallas_call(
        body, grid=(S // T,),
        in_specs=[pl.BlockSpec((T, 1024), lambda i: (i, 0)), pl.BlockSpec((T, 1024), lambda i: (i, 0))],
        out_specs=pl.BlockSpec((MLA_HEADS, 1, T), lambda i: (0, 0, i)), out_shape=SDS((MLA_HEADS, 1, S), f32),
        compiler_params=_params(("parallel",)), name=name)(dO, O)


def _flash_bwd(Q, K, kv, delta, L, dO, *, name):
    S = Q.shape[0]
    hq = min(S, 512)
    nh = 2 if S % 1024 == 0 else 1
    tq = nh * hq
    tk = min(S, 512)
    nq = S // tq
    ln2 = math.log(2.0)

    def body(q_ref, k_ref, v_ref, dl_ref, l_ref, do_ref, dq_ref, dk_ref, dv_ref, dk_acc, dv_acc):
        kk = pl.program_id(1)
        i = pl.program_id(2)

        @pl.when((kk == 0) & (i == 0))
        def _():
            dq_ref[...] = jnp.zeros_like(dq_ref)

        @pl.when(i == 0)
        def _():
            dk_acc[...] = jnp.zeros_like(dk_acc)
            dv_acc[...] = jnp.zeros_like(dv_acc)

        k = k_ref[...]
        v = v_ref[...]
        dk_new = dk_acc[...]
        dv_new = dv_acc[...]
        for hf in range(nh):
            sl = slice(hf * hq, (hf + 1) * hq)
            q = q_ref[sl, :]
            st = lax.dot_general(k, q, (NT, ((), ())), preferred_element_type=f32)
            pt = jnp.exp2(st - l_ref[0, :, sl])
            delta = dl_ref[0, :, sl]
            dob = do_ref[sl, :].astype(bf16)
            dv_new = dv_new + lax.dot_general(pt.astype(bf16), dob, (NN, ((), ())), preferred_element_type=f32)
            dpt = lax.dot_general(v, dob, (NT, ((), ())), preferred_element_type=f32)
            dst = (pt * (dpt - delta)).astype(bf16)
            dk_new = dk_new + lax.dot_general(dst, q, (NN, ((), ())), preferred_element_type=f32)
            dq_ref[0, i * nh + hf] += lax.dot_general(k, dst, (TN, ((), ())), preferred_element_type=f32)
        dk_acc[...] = dk_new
        dv_acc[...] = dv_new

        @pl.when(i == nq - 1)
        def _():
            dk_ref[...] = dk_acc[...] * ln2
            dv_ref[...] = dv_acc[...]

    return pl.pallas_call(
        body, grid=(MLA_HEADS, S // tk, nq),
        in_specs=[pl.BlockSpec((tq, 256), lambda h, k, i: (i, h)),
                  pl.BlockSpec((tk, 256), lambda h, k, i: (k, h)),
                  pl.BlockSpec((tk, 128), lambda h, k, i: (k, 2 * h + 1)),
                  pl.BlockSpec((1, 1, tq), lambda h, k, i: (h, 0, i)),
                  pl.BlockSpec((1, 1, tq), lambda h, k, i: (h, 0, i)),
                  pl.BlockSpec((tq, 128), lambda h, k, i: (i, h))],
        out_specs=[pl.BlockSpec((1, S // hq, 256, hq), lambda h, k, i: (h, 0, 0, 0)),
                   pl.BlockSpec((tk, 256), lambda h, k, i: (k, h)),
                   pl.BlockSpec((tk, 128), lambda h, k, i: (k, h))],
        out_shape=[SDS((MLA_HEADS, S // hq, 256, hq), f32), SDS((S, 2048), f32), SDS((S, 1024), f32)],
        scratch_shapes=[pltpu.VMEM((tk, 256), f32), pltpu.VMEM((tk, 128), f32)],
        compiler_params=_params(("parallel", "arbitrary", "arbitrary"), VMEM_BIG), name=name)(Q, K, kv, delta, L, dO)


def _mla_qk_prep(q, kv, zm, cosm, sinm, *, name):
    S = q.shape[0]
    T = min(S, 256)
    scale = (MLA_NOPE + MLA_ROPE) ** -0.5 * math.log2(math.e)

    def body(q_ref, kv_ref, kr_ref, cos_ref, sin_ref, oq_ref, ok_ref):
        cs, sn = cos_ref[...], sin_ref[...]
        kr = _rope(kr_ref[...], cs, sn, 32).astype(bf16)
        for h in range(MLA_HEADS):
            a = 256 * h
            oq_ref[:, a:a + 128] = (q_ref[:, a:a + 128].astype(f32) * scale).astype(bf16)
            oq_ref[:, a + 128:a + 256] = (_rope(q_ref[:, a + 128:a + 256].astype(f32), cs, sn, 32) * scale).astype(bf16)
            ok_ref[:, a:a + 128] = kv_ref[:, a:a + 128]
            ok_ref[:, a + 128:a + 256] = kr

    row = lambda w, col=0: pl.BlockSpec((T, w), lambda i: (i, col))
    return pl.pallas_call(
        body, grid=(S // T,), in_specs=[row(2048), row(2048), row(128, 6), row(128), row(128)],
        out_specs=[row(2048), row(2048)], out_shape=[SDS((S, 2048), bf16), SDS((S, 2048), bf16)],
        compiler_params=_params(("parallel",), VMEM_BIG), name=name)(q, kv, zm, cosm, sinm)


def _mla_bwd_prep(dQ, dK, dV, cosm, sinm, *, name):
    S = dK.shape[0]
    T = dQ.shape[3]
    scale = (MLA_NOPE + MLA_ROPE) ** -0.5

    def body(dq_ref, dk_ref, dv_ref, cos_ref, sin_ref, oq_ref, okv_ref, okr_ref):
        cs, sn = cos_ref[...], sin_ref[...]
        kr = jnp.zeros((T, 128), f32)
        for h in range(MLA_HEADS):
            a = 256 * h
            dq = jnp.transpose(dq_ref[h, 0])
            oq_ref[:, a:a + 128] = (dq[:, 0:128] * scale).astype(bf16)
            oq_ref[:, a + 128:a + 256] = (_rope_t(dq[:, 128:256], cs, sn, 32) * scale).astype(bf16)
            okv_ref[:, a:a + 128] = dk_ref[:, a:a + 128].astype(bf16)
            okv_ref[:, a + 128:a + 256] = dv_ref[:, 128 * h:128 * h + 128].astype(bf16)
            kr = kr + dk_ref[:, a + 128:a + 256]
        okr_ref[...] = _rope_t(kr, cs, sn, 32)

    return pl.pallas_call(
        body, grid=(S // T,),
        in_specs=[pl.BlockSpec((MLA_HEADS, 1, 256, T), lambda i: (0, i, 0, 0)), pl.BlockSpec((T, 2048), lambda i: (i, 0)),
                  pl.BlockSpec((T, 1024), lambda i: (i, 0)), pl.BlockSpec((T, 128), lambda i: (i, 0)),
                  pl.BlockSpec((T, 128), lambda i: (i, 0))],
        out_specs=[pl.BlockSpec((T, 2048), lambda i: (i, 0)), pl.BlockSpec((T, 2048), lambda i: (i, 0)),
                   pl.BlockSpec((T, 128), lambda i: (i, 0))],
        out_shape=[SDS((S, 2048), bf16), SDS((S, 2048), bf16), SDS((S, 128), f32)],
        compiler_params=_params(("parallel",), VMEM_BIG), name=name)(dQ, dK, dV, cosm, sinm)


def _mla_norm_bwd(zm, qg, kvg, dcqn, dckvn, dkr, *, name):
    S = zm.shape[0]
    T = min(S, 512)

    def body(cq_ref, ckv_ref, qg_ref, kvg_ref, dcq_ref, dckv_ref, dkr_ref, o_ref, dqg_ref, dkvg_ref):
        i = pl.program_id(0)
        _, vjp = jax.vjp(_rms, cq_ref[...], qg_ref[...])
        dcq, dqg = vjp(dcq_ref[...])
        _, vjp2 = jax.vjp(_rms, ckv_ref[...], kvg_ref[...])
        dckv, dkvg = vjp2(dckv_ref[...])
        o_ref[:, 0:384] = dcq.astype(bf16)
        o_ref[:, 384:512] = jnp.zeros((T, 128), bf16)
        o_ref[:, 512:768] = dckv.astype(bf16)
        o_ref[:, 768:896] = dkr_ref[...].astype(bf16)

        @pl.when(i == 0)
        def _():
            dqg_ref[...] = dqg
            dkvg_ref[...] = dkvg

        @pl.when(i > 0)
        def _():
            dqg_ref[...] += dqg
            dkvg_ref[...] += dkvg

    return pl.pallas_call(
        body, grid=(S // T,),
        in_specs=[pl.BlockSpec((T, 384), lambda i: (i, 0)), pl.BlockSpec((T, 256), lambda i: (i, 2)),
                  pl.BlockSpec((1, 384), lambda i: (0, 0)), pl.BlockSpec((1, 256), lambda i: (0, 0)),
                  pl.BlockSpec((T, 384), lambda i: (i, 0)), pl.BlockSpec((T, 256), lambda i: (i, 0)),
                  pl.BlockSpec((T, 128), lambda i: (i, 0))],
        out_specs=[pl.BlockSpec((T, 896), lambda i: (i, 0)), pl.BlockSpec((1, 384), lambda i: (0, 0)),
                   pl.BlockSpec((1, 256), lambda i: (0, 0))],
        out_shape=[SDS((S, 896), bf16), SDS((1, 384), f32), SDS((1, 256), f32)],
        compiler_params=_params(("arbitrary",)), name=name)(zm, zm, qg, kvg, dcqn, dckvn, dkr)


def _s5_disc(a_re, a_im, ldt, b_re, b_im):
    dt = jnp.exp(ldt)
    ar = jnp.minimum(a_re, -1e-4)
    mag = jnp.exp(dt * ar)
    abr = mag * jnp.cos(dt * a_im)
    abi = mag * jnp.sin(dt * a_im)
    den = ar * ar + a_im * a_im
    nr = abr - 1.0
    ni = abi
    cr = (nr * ar + ni * a_im) / den
    ci = (ni * ar - nr * a_im) / den
    return abr, abi, cr * b_re - ci * b_im, cr * b_im + ci * b_re


def _s5_param_fwd(a_re, a_im, ldt, b_re, b_im, *, name):
    R = SDS((1, 8192), f32)
    M = SDS((16, 8192), f32)
    Pw = SDS((64, 8192), f32)

    def body(a_re_r, a_im_r, ldt_r, b_re_r, b_im_r, o1, o2, o3, o4, p_re, p_im):
        abr, abi, bbr, bbi = _s5_disc(a_re_r[...], a_im_r[...], ldt_r[...], b_re_r[...], b_im_r[...])
        o1[...] = abr
        o2[...] = abi
        o3[...] = bbr
        o4[...] = bbi
        dt = jnp.exp(ldt_r[...])
        ar = jnp.minimum(a_re_r[...], -1e-4)
        n = lax.broadcasted_iota(jnp.int32, (64, 8192), 0).astype(f32) + 1.0
        mag = jnp.exp(n * (dt * ar))
        ang = n * (dt * a_im_r[...])
        p_re[...] = mag * jnp.cos(ang)
        p_im[...] = mag * jnp.sin(ang)

    return pl.pallas_call(body, out_shape=[R, R, M, M, Pw, Pw], name=name)(a_re, a_im, ldt, b_re, b_im)


def _s5_param_bwd(a_re, a_im, ldt, b_re, b_im, d_abr, d_abi, d_bbr, d_bbi, *, name):
    R = SDS((1, 8192), f32)
    M = SDS((16, 8192), f32)

    def body(a_re_r, a_im_r, ldt_r, b_re_r, b_im_r, c1, c2, c3, c4, o1, o2, o3, o4, o5):
        _, vjp = jax.vjp(_s5_disc, a_re_r[...], a_im_r[...], ldt_r[...], b_re_r[...], b_im_r[...])
        g = vjp((c1[...], c2[...], c3[...], c4[...]))
        for o, v in zip((o1, o2, o3, o4, o5), g):
            o[...] = v

    return pl.pallas_call(body, out_shape=[R, R, R, M, M], name=name)(a_re, a_im, ldt, b_re, b_im, d_abr, d_abi, d_bbr, d_bbi)


def _seg_perm(T, inverse):
    L = T // S5_SEG
    i = jnp.arange(T)
    src = (i % S5_SEG) * L + i // S5_SEG
    P = (src[:, None] == jnp.arange(T)[None, :]).astype(bf16)
    return P.T if inverse else P


def _perm_rows(a, P, *, name):
    S, W = a.shape
    T = P.shape[0]

    def body(p_ref, a_ref, o_ref):
        o_ref[...] = lax.dot_general(p_ref[...], a_ref[...], (NN, ((), ())), preferred_element_type=f32).astype(o_ref.dtype)

    return pl.pallas_call(
        body, grid=(S // T,), in_specs=[pl.BlockSpec((T, T), lambda i: (0, 0)), pl.BlockSpec((T, W), lambda i: (i, 0))],
        out_specs=pl.BlockSpec((T, W), lambda i: (i, 0)), out_shape=SDS((S, W), a.dtype),
        compiler_params=_params(("parallel",)), name=name)(P, a)


def _scan_core(xr, xi, ar, ai, pwr_ref, pwi_ref, a64r, a64i, carry, *, reverse, T, conj):
    L = T // S5_SEG
    sg = -1.0 if conj else 1.0
    arb = jnp.broadcast_to(ar, (8, 512))
    aib = jnp.broadcast_to(ai, (8, 512))
    UN = 4

    def step(r4, c):
        cr, ci = c
        for u in range(UN):
            r0 = r4 * UN + u
            r = (L - 1 - r0) if reverse else r0
            rows = pl.ds(pl.multiple_of(r * 8, 8), 8)
            nr = arb * cr - aib * ci + xr[rows, :]
            ni = arb * ci + aib * cr + xi[rows, :]
            xr[rows, :] = nr
            xi[rows, :] = ni
            cr, ci = nr, ni
        return cr, ci

    lr, li = lax.fori_loop(0, L // UN, step, (jnp.zeros((8, 512), f32), jnp.zeros((8, 512), f32)))
    row8 = lax.broadcasted_iota(jnp.int32, (8, 512), 0)
    cr = carry[0, 0:1, :]
    ci = carry[1, 0:1, :]
    a6i = sg * a64i
    cin_r = jnp.zeros((8, 512), f32)
    cin_i = jnp.zeros((8, 512), f32)
    for seg in (range(S5_SEG - 1, -1, -1) if reverse else range(S5_SEG)):
        cin_r = jnp.where(row8 == seg, cr, cin_r)
        cin_i = jnp.where(row8 == seg, ci, cin_i)
        ncr = lr[seg:seg + 1, :] + a64r * cr - a6i * ci
        nci = li[seg:seg + 1, :] + a64r * ci + a6i * cr
        cr, ci = ncr, nci
    carry[0, 0:1, :] = cr
    carry[1, 0:1, :] = ci

    def fix(r4, _):
        for u in range(UN):
            r = r4 * UN + u
            rows = pl.ds(pl.multiple_of(r * 8, 8), 8)
            pr = pwr_ref[pl.ds(r, 1), :]
            pi = sg * pwi_ref[pl.ds(r, 1), :]
            xr[rows, :] += pr * cin_r - pi * cin_i
            xi[rows, :] += pr * cin_i + pi * cin_r
        return 0

    lax.fori_loop(0, L // UN, fix, 0)


def _s5_scan_fwd(u, BBr, BBi, CCr, CCi, abr, abi, pwr, pwi, *, reverse, name):
    S = u.shape[0]
    T = min(S, 512)
    NB = S // T
    L = T // S5_SEG
    d = 1 if reverse else 0

    def tb(b):
        return (NB - 1 - b) if reverse else b

    def body(u_ref, bbr_ref, bbi_ref, ccr_ref, cci_ref, ar_ref, ai_ref, pwr_ref, pwi_ref, y_ref, xr_ref, xi_ref, carry):
        b = pl.program_id(1)

        @pl.when(b == 0)
        def _():
            carry[...] = jnp.zeros_like(carry)

        ub = u_ref[...].astype(bf16)
        xr_ref[...] = lax.dot_general(ub, bbr_ref[0, 0], (NN, ((), ())), preferred_element_type=f32)
        xi_ref[...] = lax.dot_general(ub, bbi_ref[0, 0], (NN, ((), ())), preferred_element_type=f32)
        a6 = (0 if reverse else L - 1)
        _scan_core(xr_ref, xi_ref, ar_ref[...], ai_ref[...], pwr_ref, pwi_ref, pwr_ref[a6:a6 + 1, :], pwi_ref[a6:a6 + 1, :],
                   carry, reverse=reverse, T=T, conj=False)
        y_ref[...] = _dot(xr_ref[...], ccr_ref[0, 0], NN) - _dot(xi_ref[...], cci_ref[0, 0], NN)

    mat = lambda shp: pl.BlockSpec((1, 1) + shp, lambda j, b: (d, j, 0, 0))
    vec = lambda r: pl.BlockSpec((r, 512), lambda j, b: (0, d * S5_NJ + j))
    return pl.pallas_call(
        body, grid=(S5_NJ, NB),
        in_specs=[pl.BlockSpec((T, 128), lambda j, b: (tb(b), j)), mat((128, 512)), mat((128, 512)), mat((512, 128)),
                  mat((512, 128)), vec(1), vec(1), vec(L), vec(L)],
        out_specs=[pl.BlockSpec((T, 128), lambda j, b: (tb(b), j)), pl.BlockSpec((T, 512), lambda j, b: (tb(b), j)),
                   pl.BlockSpec((T, 512), lambda j, b: (tb(b), j))],
        out_shape=[SDS((S, 1024), f32), SDS((S, 4096), f32), SDS((S, 4096), f32)],
        scratch_shapes=[pltpu.VMEM((2, 8, 512), f32)],
        compiler_params=_params(("parallel", "arbitrary")), name=name)(u, BBr, BBi, CCr, CCi, abr, abi, pwr, pwi)


def _s5_scan_bwd(u, dy, xr, xi, BBr, BBi, CCr, CCi, abr, abi, pwr, pwi, *, reverse, name):
    S = u.shape[0]
    T = min(S, 512)
    NB = S // T
    L = T // S5_SEG
    d = 1 if reverse else 0
    adj_rev = not reverse

    def tb(b):
        return b if reverse else (NB - 1 - b)

    def bnd(b):
        t = tb(b)
        if reverse:
            return jnp.minimum((t + 1) * (T // 8), S // 8 - 1)
        return jnp.maximum(t * (T // 8) - 1, 0)

    def body(u_ref, dy_ref, xr_ref, xi_ref, xbr_ref, xbi_ref, bbr_ref, bbi_ref, ccr_ref, cci_ref, ar_ref, ai_ref,
             pwr_ref, pwi_ref, du_ref, dbbr_ref, dbbi_ref, dccr_ref, dcci_ref, dar_ref, dai_ref, carry, lam):
        b = pl.program_id(1)

        @pl.when(b == 0)
        def _():
            carry[...] = jnp.zeros_like(carry)
            for r in (dbbr_ref, dbbi_ref, dccr_ref, dcci_ref, dar_ref, dai_ref):
                r[...] = jnp.zeros_like(r)

        dyb = dy_ref[...]
        lam[0] = lax.dot_general(dyb, ccr_ref[0, 0], (NT, ((), ())), preferred_element_type=f32)
        lam[1] = -lax.dot_general(dyb, cci_ref[0, 0], (NT, ((), ())), preferred_element_type=f32)
        a6 = (0 if adj_rev else L - 1)
        _scan_core(lam.at[0], lam.at[1], ar_ref[...], -ai_ref[...], pwr_ref, pwi_ref, pwr_ref[a6:a6 + 1, :],
                   pwi_ref[a6:a6 + 1, :], carry, reverse=adj_rev, T=T, conj=True)
        ub = u_ref[...].astype(bf16)
        first = (b == NB - 1)
        lrb = lam[0].astype(bf16)
        lib = lam[1].astype(bf16)
        du_ref[...] = lax.dot_general(lrb, bbr_ref[0, 0], (NT, ((), ())), preferred_element_type=f32) \
            + lax.dot_general(lib, bbi_ref[0, 0], (NT, ((), ())), preferred_element_type=f32)
        dbbr_ref[0, 0] += lax.dot_general(ub, lrb, (TN, ((), ())), preferred_element_type=f32)
        dbbi_ref[0, 0] += lax.dot_general(ub, lib, (TN, ((), ())), preferred_element_type=f32)
        dccr_ref[0, 0] += lax.dot_general(dyb, xr_ref[...].astype(bf16), (TN, ((), ())), preferred_element_type=f32)
        dcci_ref[0, 0] -= lax.dot_general(dyb, xi_ref[...].astype(bf16), (TN, ((), ())), preferred_element_type=f32)
        row8 = lax.broadcasted_iota(jnp.int32, (8, 512), 0)
        if reverse:
            body_x, body_l, edge_l = slice(8, T), slice(0, T - 8), slice(T - 8, T)
            sp_r = jnp.where(row8 == 7, jnp.where(first, 0.0, xbr_ref[0:1, :]), pltpu.roll(xr_ref[0:8, :], 7, axis=0))
            sp_i = jnp.where(row8 == 7, jnp.where(first, 0.0, xbi_ref[0:1, :]), pltpu.roll(xi_ref[0:8, :], 7, axis=0))
        else:
            body_x, body_l, edge_l = slice(0, T - 8), slice(8, T), slice(0, 8)
            sp_r = jnp.where(row8 == 0, jnp.where(first, 0.0, xbr_ref[7:8, :]), pltpu.roll(xr_ref[T - 8:T, :], 1, axis=0))
            sp_i = jnp.where(row8 == 0, jnp.where(first, 0.0, xbi_ref[7:8, :]), pltpu.roll(xi_ref[T - 8:T, :], 1, axis=0))
        xpr, xpi = xr_ref[body_x, :], xi_ref[body_x, :]
        lr, li = lam[0, body_l, :], lam[1, body_l, :]
        er, ei = lam[0, edge_l, :], lam[1, edge_l, :]
        dar_ref[...] += jnp.sum(xpr * lr + xpi * li, axis=0, keepdims=True) + jnp.sum(sp_r * er + sp_i * ei, axis=0, keepdims=True)
        dai_ref[...] += jnp.sum(xpr * li - xpi * lr, axis=0, keepdims=True) + jnp.sum(sp_r * ei - sp_i * er, axis=0, keepdims=True)

    mat = lambda shp: pl.BlockSpec((1, 1) + shp, lambda j, b: (d, j, 0, 0))
    omat = lambda shp: pl.BlockSpec((1, 1) + shp, lambda j, b: (0, j, 0, 0))
    vec = lambda r: pl.BlockSpec((r, 512), lambda j, b: (0, d * S5_NJ + j))
    blk = lambda w: pl.BlockSpec((T, w), lambda j, b: (tb(b), j))
    return pl.pallas_call(
        body, grid=(S5_NJ, NB),
        in_specs=[blk(128), blk(128), blk(512), blk(512),
                  pl.BlockSpec((8, 512), lambda j, b: (bnd(b), j)), pl.BlockSpec((8, 512), lambda j, b: (bnd(b), j)),
                  mat((128, 512)), mat((128, 512)), mat((512, 128)), mat((512, 128)), vec(1), vec(1), vec(L), vec(L)],
        out_specs=[blk(128), omat((128, 512)), omat((128, 512)), omat((128, 512)), omat((128, 512)),
                   pl.BlockSpec((1, 512), lambda j, b: (0, j)), pl.BlockSpec((1, 512), lambda j, b: (0, j))],
        out_shape=[SDS((S, 1024), f32), SDS((1, 8, 128, 512), f32), SDS((1, 8, 128, 512), f32), SDS((1, 8, 128, 512), f32),
                   SDS((1, 8, 128, 512), f32), SDS((1, 4096), f32), SDS((1, 4096), f32)],
        scratch_shapes=[pltpu.VMEM((2, 8, 512), f32), pltpu.VMEM((2, T, 512), f32)],
        compiler_params=_params(("parallel", "arbitrary"), VMEM_BIG), name=name)(
            u, dy, xr, xi, xr, xi, BBr, BBi, CCr, CCi, abr, abi, pwr, pwi)


def _silu_mul(g, u):
    return jax.nn.silu(g) * u


def _mixf(p0, p1, p2, z0, z1, z2):
    return jax.nn.sigmoid(z0) * p0 + jax.nn.sigmoid(z1) * p1 + jax.nn.sigmoid(z2) * p2


def _s5_act(yf, yb, u, dd):
    return jax.nn.gelu(yf + yb + dd * u)


def _glu(a, b):
    return a * jax.nn.sigmoid(b)


def _layer_fwd(x, w, tabs, l):
    S = x.shape[0]
    T = min(S, 512)
    I = S // T
    nm = lambda s: f"L{l}_{s}"
    sv = {'x': x}
    h = _rmsnorm_fwd(x, w['norm1_g'], name=nm("norm1"))
    zr = _mm(h, w['W_ret'], name=nm("in_ret"))
    zm = _mm(h, w['W_mla'], name=nm("in_mla"))
    h_seg = _perm_rows(h, tabs['seg_perm'], name=nm("s5_perm_h"))
    zs = _mm(h_seg, w['W_s5'], name=nm("in_s5"))
    zg = _mm(h, w['W_gate'], out_dtype=bf16, name=nm("in_gate"))
    sv.update(h=h, h_seg=h_seg, zr=zr, zm=zm, zs=zs, zg=zg)

    yf, stf = _ret_dir_fwd(zr, w['lg'], tabs['cos_r'], tabs['sin_r'], reverse=False, name=nm("ret_f"))
    yb, stb = _ret_dir_fwd(zr, w['lg'], tabs['cos_r'], tabs['sin_r'], reverse=True, name=nm("ret_b"))
    hd = lambda j: j
    y_ret = _pw(_gn_gate, [yf, yb, zr, w['ret_gn_g']],
                [_row(T, 256, hd), _row(T, 256, hd), _row(T, 256, lambda j: 8 + j), _par(256, hd)],
                [SDS((S, 1024), bf16)], [_row(T, 256, hd)], (RET_HEADS, I), name=nm("ret_gn"))[0]
    sv.update(yf=yf, yb=yb, stf=stf, stb=stb, y_ret=y_ret)

    cqn, ckvn = _pw(lambda a, b, g1, g2: (_rms(a, g1), _rms(b, g2)), [zm, zm, w['mla_q_norm_g'], w['mla_kv_norm_g']],
                    [_row(T, 384), _row(T, 256, lambda j: 2), _par(384), _par(256)],
                    [SDS((S, 384), bf16), SDS((S, 256), bf16)], [_row(T, 384), _row(T, 256)], (1, I), name=nm("mla_norm"))
    q = _mm(cqn, w['W_uq'], out_dtype=bf16, name=nm("mla_uq"))
    kv = _mm(ckvn, w['W_ukv'], out_dtype=bf16, name=nm("mla_ukv"))
    Q, K = _mla_qk_prep(q, kv, zm, tabs['cos_m'], tabs['sin_m'], name=nm("mla_qkprep"))
    O, Lse = _flash_fwd(Q, K, kv, name=nm("mla_attn"))
    sv.update(cqn=cqn, ckvn=ckvn, kv=kv, Q=Q, K=K, O=O, Lse=Lse)

    s5 = w['s5']
    ysf, xrf, xif = _s5_scan_fwd(zs, s5['BBr'], s5['BBi'], s5['CCr'], s5['CCi'], s5['abr'], s5['abi'], s5['pwr_f'], s5['pwi_f'],
                                 reverse=False, name=nm("s5_f"))
    ysb, xrb, xib = _s5_scan_fwd(zs, s5['BBr'], s5['BBi'], s5['CCr'], s5['CCi'], s5['abr'], s5['abi'], s5['pwr_f'], s5['pwi_f'],
                                 reverse=True, name=nm("s5_b"))
    gact = _pw(_s5_act, [ysf, ysb, zs, w['s5_d']], [_row(T, D), _row(T, D), _row(T, D), _par(D)],
               [SDS((S, D), bf16)], [_row(T, D)], (1, I), name=nm("s5_act"))[0]
    gg = _mm(gact, w['W_glu'], out_dtype=bf16, name=nm("s5_glu_mm"))
    y_s5 = _pw(_glu, [gg, gg], [_row(T, D), _row(T, D, lambda j: 1)], [SDS((S, D), bf16)], [_row(T, D)], (1, I),
               name=nm("s5_glu"))[0]
    y_s5 = _perm_rows(y_s5, tabs['seg_unperm'], name=nm("s5_unperm_y"))
    sv.update(ysf=ysf, ysb=ysb, xrf=xrf, xif=xif, xrb=xrb, xib=xib, gact=gact, gg=gg, y_s5=y_s5)

    ys = [y_ret, O, y_s5]
    pr = [_mm(ys[i], w['W_br'][i], out_dtype=bf16, name=nm(f"branch{i}")) for i in range(3)]
    mix = _pw(_mixf, pr + [zg, zg, zg],
              [_row(T, D)] * 3 + [_row(T, D), _row(T, D, lambda j: 1), _row(T, D, lambda j: 2)],
              [SDS((S, D), bf16)], [_row(T, D)], (1, I), name=nm("mix"))[0]
    x1 = _mm(mix, w['W_out'], res=x, name=nm("out_proj"))
    h2 = _rmsnorm_fwd(x1, w['norm2_g'], name=nm("norm2"))
    fgu = _mm(h2, w['W_gu'], out_dtype=bf16, name=nm("ffn_gu"))
    act = _pw(_silu_mul, [fgu, fgu], [_row(T, 1408, lambda j: j), _row(T, 1408, lambda j: 2 + j)],
              [SDS((S, FFN_H), bf16)], [_row(T, 1408, lambda j: j)], (2, I), name=nm("ffn_act"))[0]
    x2 = _mm(act, w['W_down'], res=x1, name=nm("ffn_down"))
    sv.update(pr=pr, mix=mix, x1=x1, h2=h2, fgu=fgu, act=act)
    return x2, sv


def _vjp_fn(fn, n_primal, cast=None):
    def g(*args):
        _, vjp = jax.vjp(fn, *args[:n_primal])
        return vjp(args[n_primal].astype(f32))
    return g


def _layer_bwd(dx2, w, tabs, sv, l):
    S = dx2.shape[0]
    T = min(S, 512)
    I = S // T
    nm = lambda s: f"L{l}_b_{s}"
    g = {}
    hd = lambda j: j

    dact = _mm(dx2, w['W_down'], tb=True, out_dtype=bf16, name=nm("ffn_down_dx"))
    g['W_down'] = _mmT(sv['act'], dx2, name=nm("ffn_down_dw"))
    dfg, dfu = _pw(_vjp_fn(_silu_mul, 2), [sv['fgu'], sv['fgu'], dact],
                   [_row(T, 1408, lambda j: j), _row(T, 1408, lambda j: 2 + j), _row(T, 1408, lambda j: j)],
                   [SDS((S, FFN_H), bf16), SDS((S, FFN_H), bf16)], [_row(T, 1408, lambda j: j)] * 2, (2, I), name=nm("ffn_act"))
    dfgu = jnp.concatenate([dfg, dfu], axis=1)
    g['W_gu'] = _mmT(sv['h2'], dfgu, name=nm("ffn_gu_dw"))
    dh2 = _mm(dfgu, w['W_gu'], tb=True, name=nm("ffn_gu_dx"))
    dx1, g['norm2_g'] = _rmsnorm_bwd(sv['x1'], w['norm2_g'], dh2, dx2, name=nm("norm2"))

    dmix = _mm(dx1, w['W_out'], tb=True, out_dtype=bf16, name=nm("out_dx"))
    g['W_out'] = _mmT(sv['mix'], dx1, name=nm("out_dw"))
    zg = sv['zg']
    outs = _pw(_vjp_fn(_mixf, 6), sv['pr'] + [zg, zg, zg, dmix],
               [_row(T, D)] * 3 + [_row(T, D), _row(T, D, lambda j: 1), _row(T, D, lambda j: 2), _row(T, D)],
               [SDS((S, D), bf16)] * 6, [_row(T, D)] * 6, (1, I), name=nm("mix"))
    dpr, dzg = outs[:3], jnp.concatenate(outs[3:], axis=1)
    ys = [sv['y_ret'], sv['O'], sv['y_s5']]
    g['W_br'] = [_mmT(ys[i], dpr[i], name=nm(f"branch{i}_dw")) for i in range(3)]
    dpr_seg = _perm_rows(dpr[2], tabs['seg_perm'], name=nm("s5_perm_dy"))
    dys = [_mm(dpr[i] if i < 2 else dpr_seg, w['W_br'][i], tb=True, out_dtype=bf16,
               name=nm(f"branch{i}_dx")) for i in range(3)]

    gg = sv['gg']
    dga, dgb = _pw(_vjp_fn(_glu, 2), [gg, gg, dys[2]], [_row(T, D), _row(T, D, lambda j: 1), _row(T, D)],
                   [SDS((S, D), bf16)] * 2, [_row(T, D)] * 2, (1, I), name=nm("s5_glu"))
    dgg = jnp.concatenate([dga, dgb], axis=1)
    g['W_glu'] = _mmT(sv['gact'], dgg, name=nm("s5_glu_dw"))
    dgact = _mm(dgg, w['W_glu'], tb=True, out_dtype=bf16, name=nm("s5_glu_dx"))

    def act_bwd(yf, yb, u, dd, ct):
        _, vjp = jax.vjp(_s5_act, yf, yb, u, dd)
        dyf, _, du, ddd = vjp(ct)
        return dyf, du, ddd

    dys5, du_direct, g['s5_d'] = _pw(act_bwd, [sv['ysf'], sv['ysb'], sv['zs'], w['s5_d'], dgact],
                                     [_row(T, D)] * 3 + [_par(D), _row(T, D)],
                                     [SDS((S, D), bf16), SDS((S, D), f32), SDS((1, D), f32)],
                                     [_row(T, D), _row(T, D), _par(D)], (1, I), n_acc=1, name=nm("s5_act"))
    s5 = w['s5']
    rf = _s5_scan_bwd(sv['zs'], dys5, sv['xrf'], sv['xif'], s5['BBr'], s5['BBi'], s5['CCr'], s5['CCi'], s5['abr'], s5['abi'],
                      s5['pwr_a'], s5['pwi_a'], reverse=False, name=nm("s5_f"))
    rb = _s5_scan_bwd(sv['zs'], dys5, sv['xrb'], sv['xib'], s5['BBr'], s5['BBi'], s5['CCr'], s5['CCi'], s5['abr'], s5['abi'],
                      s5['pwr_a'], s5['pwi_a'], reverse=True, name=nm("s5_b"))
    g['s5'] = (rf[1:], rb[1:])
    dzs_seg = _pw(lambda a, b, c: a + b + c, [du_direct, rf[0], rb[0]], [_row(T, D)] * 3, [SDS((S, D), bf16)], [_row(T, D)],
                  (1, I), name=nm("s5_du"))[0]
    dzs = _perm_rows(dzs_seg, tabs['seg_unperm'], name=nm("s5_unperm_dz"))

    delta = _attn_delta(dys[1], sv['O'], name=nm("mla_delta"))
    dQ, dK, dV = _flash_bwd(sv['Q'], sv['K'], sv['kv'], delta, sv['Lse'], dys[1], name=nm("mla_attn"))
    dq_lin, dkv, dkr = _mla_bwd_prep(dQ, dK, dV, tabs['cos_m'], tabs['sin_m'], name=nm("mla_prep"))
    g['W_uq'] = _mmT(sv['cqn'], dq_lin, name=nm("mla_uq_dw"))
    dcqn = _mm(dq_lin, w['W_uq'], tb=True, name=nm("mla_uq_dx"))
    g['W_ukv'] = _mmT(sv['ckvn'], dkv, name=nm("mla_ukv_dw"))
    dckvn = _mm(dkv, w['W_ukv'], tb=True, name=nm("mla_ukv_dx"))
    dzm, g['mla_q_norm_g'], g['mla_kv_norm_g'] = _mla_norm_bwd(sv['zm'], w['mla_q_norm_g'], w['mla_kv_norm_g'], dcqn, dckvn, dkr,
                                                               name=nm("mla_norm"))

    zr = sv['zr']

    def gn_bwd(yf, yb, gt, gn, ct):
        _, vjp = jax.vjp(_gn_gate, yf, yb, gt, gn)
        dyf, _, dgt, dgn = vjp(ct)
        return dyf, dgt, dgn

    dyr, dgate, g['ret_gn_g'] = _pw(gn_bwd, [sv['yf'], sv['yb'], zr, w['ret_gn_g'], dys[0]],
                                    [_row(T, 256, hd), _row(T, 256, hd), _row(T, 256, lambda j: 8 + j), _par(256, hd),
                                     _row(T, 256, hd)],
                                    [SDS((S, 1024), bf16), SDS((S, 1024), bf16), SDS((1, 1024), f32)],
                                    [_row(T, 256, hd), _row(T, 256, hd), _par(256, hd)], (RET_HEADS, I), n_acc=1, name=nm("ret_gn"))
    qf, kf, vf, lgf = _ret_dir_bwd(zr, w['lg'], tabs['cos_r'], tabs['sin_r'], dyr, sv['stf'], reverse=False, name=nm("ret_f"))
    qb, kb, vb, lgb = _ret_dir_bwd(zr, w['lg'], tabs['cos_r'], tabs['sin_r'], dyr, sv['stb'], reverse=True, name=nm("ret_b"))
    g['lg'] = jnp.stack([lgf[:, 0, 0], lgb[:, 0, 0]])
    dzr = _pw(lambda a, b, c, d, e, f, gt: jnp.concatenate([a + b, c + d, e + f, gt], axis=1),
              [qf, qb, kf, kb, vf, vb, dgate], [_row(256, 512)] * 4 + [_row(256, D)] * 3,
              [SDS((S, 3072), bf16)], [_row(256, 3072)], (1, S // 256), name=nm("ret_dz"))[0]

    h = sv['h']
    g['W_ret'] = _mmT(h, dzr, name=nm("in_ret_dw"))
    g['W_mla'] = _mmT(h, dzm, name=nm("in_mla_dw"))
    g['W_s5'] = _mmT(sv['h_seg'], dzs_seg, name=nm("in_s5_dw"))
    g['W_gate'] = _mmT(h, dzg, name=nm("in_gate_dw"))
    dh = _mm(dzr, w['W_ret'], tb=True, name=nm("in_ret_dx"))
    dh = _mm(dzm, w['W_mla'], tb=True, res=dh, name=nm("in_mla_dx"))
    dh = _mm(dzs, w['W_s5'], tb=True, res=dh, name=nm("in_s5_dx"))
    dh = _mm(dzg, w['W_gate'], tb=True, res=dh, name=nm("in_gate_dx"))
    dx, g['norm1_g'] = _rmsnorm_bwd(sv['x'], w['norm1_g'], dh, dx1, name=nm("norm1"))
    return dx, g


def _loss_head(x, tgt, gain, *, name):
    S, W = x.shape
    T = min(S, 512)

    def loss_fn(xv, gv, tv):
        return 0.5 * jnp.sum(jnp.mean(jnp.square(_rms(xv, gv) - tv), axis=-1, keepdims=True), axis=0, keepdims=True)

    def fn(xv, gv, tv):
        lv, vjp = jax.vjp(lambda a, b: loss_fn(a, b, tv), xv, gv)
        dx, dg = vjp(jnp.ones((1, 1), f32))
        return dx, jnp.broadcast_to(lv, (1, 128)), dg

    return _pw(fn, [x, gain, tgt], [_row(T, W), _par(W), _row(T, W)],
               [SDS((S, W), f32), SDS((1, 128), f32), SDS((1, W), f32)], [_row(T, W), _par(128), _par(W)],
               (1, S // T), n_acc=2, name=name)


def _rope_tabs(S):
    def tab(dim):
        inv = 1.0 / (ROPE_THETA ** (jnp.arange(0, dim, 2, dtype=f32) / dim))
        ang = jnp.arange(S, dtype=f32)[:, None] * inv[None, :]
        return jnp.cos(ang), jnp.sin(ang)

    cr, sr = tab(RET_DK)
    cm, sm = tab(MLA_ROPE)
    z = jnp.zeros((S, 64), f32)
    return {'cos_r': jnp.concatenate([cr, cr], axis=1), 'sin_r': jnp.concatenate([-sr, sr], axis=1),
            'cos_m': jnp.concatenate([cm, cm, z], axis=1), 'sin_m': jnp.concatenate([-sm, sm, z], axis=1),
            'seg_perm': _seg_perm(512, False), 'seg_unperm': _seg_perm(512, True)}


def _bd_B(bb):
    b5 = bb.reshape(16, 2, 8, 8, 64)
    return jnp.einsum('cdjgp,gh->djgchp', b5, jnp.eye(8, dtype=bb.dtype)).reshape(2, 8, 128, 512)


def _bd_B_t(dBB):
    return jnp.einsum('djgcgp->cdjgp', dBB.reshape(2, 8, 8, 16, 8, 64)).reshape(16, 8192)


def _bd_C(c):
    c5 = c.reshape(2, 8, 8, 16, 64)
    return jnp.einsum('djgcp,gh->djgphc', c5, jnp.eye(8, dtype=c.dtype)).reshape(2, 8, 512, 128)


def _s5_rows(p, l):
    a_re = p['s5_a_re'][l].reshape(1, 8192)
    a_im = p['s5_a_im'][l].reshape(1, 8192)
    ldt = jnp.broadcast_to(p['s5_log_dt'][l][:, :, None], (2, S5_G, S5_P)).reshape(1, 8192)
    b_re = p['s5_b_re'][l].transpose(3, 0, 1, 2).reshape(16, 8192)
    b_im = p['s5_b_im'][l].transpose(3, 0, 1, 2).reshape(16, 8192)
    return a_re, a_im, ldt, b_re, b_im


def _layer_weights(big, p, l):
    w_in = big['w_in'][l]
    z = lambda n: jnp.zeros((D, n), w_in.dtype)
    w = {
        'W_ret': w_in[:, 0:3072],
        'W_mla': jnp.concatenate([w_in[:, 3072:3456], z(128), w_in[:, 3456:3712], w_in[:, 3712:3776], z(64)], axis=1),
        'W_s5': w_in[:, 3776:4800],
        'W_gate': w_in[:, 4800:7872],
        'W_uq': jnp.pad(big['mla_w_uq'][l].reshape(MLA_Q_LORA, MLA_HEADS, 192), ((0, 0), (0, 0), (0, 64))).reshape(MLA_Q_LORA, 2048),
        'W_ukv': big['mla_w_ukv'][l],
        'W_glu': big['s5_w_glu'][l],
        'W_br': [big['w_branch'][l, i] for i in range(3)],
        'W_out': big['w_out'][l],
        'W_gu': big['ffn_w_gu'][l],
        'W_down': big['ffn_w_down'][l],
    }
    for n in ('norm1_g', 'ret_gn_g', 'mla_q_norm_g', 'mla_kv_norm_g', 's5_d', 'norm2_g'):
        w[n] = p[n][l][None, :]
    w['lg'] = jax.nn.log_sigmoid(p['ret_decay'][l])
    rows = _s5_rows(p, l)
    abr, abi, bbr, bbi, pwr, pwi = _s5_param_fwd(*rows, name=f"L{l}_s5_param")
    flip = lambda t, first: jnp.concatenate([t[::-1, :4096], t[:, 4096:]] if first else [t[:, :4096], t[::-1, 4096:]], axis=1)
    w['s5'] = {'abr': abr, 'abi': abi, 'BBr': _bd_B(bbr).astype(bf16), 'BBi': _bd_B(bbi).astype(bf16),
               'CCr': _bd_C(p['s5_c_re'][l]).astype(bf16), 'CCi': _bd_C(p['s5_c_im'][l]).astype(bf16),
               'pwr_f': flip(pwr, False), 'pwi_f': flip(pwi, False), 'pwr_a': flip(pwr, True), 'pwi_a': flip(pwi, True),
               'rows': rows}
    return w


def _layer_grads(g, w, p, l):
    out = {}
    m = g['W_mla']
    out['w_in'] = jnp.concatenate([g['W_ret'], m[:, 0:384], m[:, 512:768], m[:, 768:832], g['W_s5'], g['W_gate']], axis=1)
    out['mla_w_uq'] = g['W_uq'].reshape(MLA_Q_LORA, MLA_HEADS, 256)[:, :, :192].reshape(MLA_Q_LORA, 1536)
    out['mla_w_ukv'] = g['W_ukv']
    out['s5_w_glu'] = g['W_glu']
    out['w_branch'] = jnp.stack(g['W_br'])
    out['w_out'] = g['W_out']
    out['ffn_w_gu'] = g['W_gu']
    out['ffn_w_down'] = g['W_down']
    for n in ('norm1_g', 'ret_gn_g', 'mla_q_norm_g', 'mla_kv_norm_g', 's5_d', 'norm2_g'):
        out[n] = g[n][0]
    out['ret_decay'] = g['lg'] * jax.nn.sigmoid(-p['ret_decay'][l])
    (fB_r, fB_i, fC_r, fC_i, fa_r, fa_i), (bB_r, bB_i, bC_r, bC_i, ba_r, ba_i) = g['s5']
    cat = lambda a, b: jnp.concatenate([a, b], axis=0)
    d_bbr = _bd_B_t(cat(fB_r, bB_r))
    d_bbi = _bd_B_t(cat(fB_i, bB_i))
    to_c = lambda t: _bd_B_t(t).reshape(16, 2, S5_G, S5_P).transpose(1, 2, 0, 3)
    out['s5_c_re'] = to_c(cat(fC_r, bC_r))
    out['s5_c_im'] = to_c(cat(fC_i, bC_i))
    d_abr = jnp.concatenate([fa_r, ba_r], axis=1)
    d_abi = jnp.concatenate([fa_i, ba_i], axis=1)
    da_re, da_im, dldt, db_re, db_im = _s5_param_bwd(*w['s5']['rows'], d_abr, d_abi, d_bbr, d_bbi, name=f"L{l}_b_s5_param")
    out['s5_a_re'] = da_re.reshape(2, S5_G, S5_P)
    out['s5_a_im'] = da_im.reshape(2, S5_G, S5_P)
    out['s5_log_dt'] = dldt.reshape(2, S5_G, S5_P).sum(axis=-1)
    out['s5_b_re'] = db_re.reshape(16, 2, S5_G, S5_P).transpose(1, 2, 3, 0)
    out['s5_b_im'] = db_im.reshape(16, 2, S5_G, S5_P).transpose(1, 2, 3, 0)
    return out


def _local_step(x, tgt, big, p):
    S = x.shape[0]
    assert S % 512 == 0
    tabs = _rope_tabs(S)
    ws, svs = [], []
    h = x
    for l in range(DEPTH):
        w = _layer_weights(big, p, l)
        h, sv = _layer_fwd(h, w, tabs, l)
        ws.append(w)
        svs.append(sv)
    dx, lossv, dfinal = _loss_head(h, tgt, p['final_g'][None, :], name="loss_head")
    per_layer = [None] * DEPTH
    for l in reversed(range(DEPTH)):
        dx, g = _layer_bwd(dx, ws[l], tabs, svs[l], l)
        per_layer[l] = _layer_grads(g, ws[l], p, l)
    return lossv[0, 0], dx, per_layer, dfinal[0]


_ANY = pl.BlockSpec(memory_space=pl.ANY)


def _place():
    x, y, c = lax.axis_index("x"), lax.axis_index("y"), lax.axis_index("c")
    return x, y, c, [(1 - x, y), (x, 1 - y), (1 - x, 1 - y)]


def _allgather4(arrs, *, name):
    n = len(arrs)

    def body(*refs):
        ins, outs = refs[:n], refs[n:2 * n]
        send, recv, loc = refs[2 * n:]
        x, y, c, chips = _place()
        me = 2 * x + y

        def remote(a, k, slot):
            px, py = chips[k]
            return pltpu.make_async_remote_copy(src_ref=ins[a], dst_ref=outs[a].at[slot], send_sem=send.at[a, k],
                                                recv_sem=recv.at[a, k], device_id=(px, py, c), device_id_type=MESH)

        mine = [pltpu.make_async_copy(ins[a], outs[a].at[me], loc.at[a]) for a in range(n)]
        for cp in mine:
            cp.start()
        sends = [remote(a, k, me) for a in range(n) for k in range(3)]
        for cp in sends:
            cp.start()
        for a in range(n):
            for k, (px, py) in enumerate(chips):
                remote(a, k, 2 * px + py).wait_recv()
        for cp in sends:
            cp.wait_send()
        for cp in mine:
            cp.wait()

    return pl.pallas_call(
        body, in_specs=[_ANY] * n, out_specs=[_ANY] * n, out_shape=[SDS((4,) + a.shape, a.dtype) for a in arrs],
        scratch_shapes=[pltpu.SemaphoreType.DMA((n, 3)), pltpu.SemaphoreType.DMA((n, 3)), pltpu.SemaphoreType.DMA((n,))],
        name=name)(*arrs)


def _rs_exchange(parts, *, name):
    n = len(parts)

    def body(*refs):
        ins, gots = refs[:n], refs[n:2 * n]
        send, recv = refs[2 * n:]
        x, y, c, chips = _place()

        def remote(a, k):
            px, py = chips[k]
            return pltpu.make_async_remote_copy(src_ref=ins[a].at[2 * px + py], dst_ref=gots[a].at[k], send_sem=send.at[a, k],
                                                recv_sem=recv.at[a, k], device_id=(px, py, c), device_id_type=MESH)

        sends = [remote(a, k) for a in range(n) for k in range(3)]
        for cp in sends:
            cp.start()
        for cp in sends:
            cp.wait_recv()
        for cp in sends:
            cp.wait_send()

    return pl.pallas_call(
        body, in_specs=[_ANY] * n, out_specs=[_ANY] * n, out_shape=[SDS((3,) + a.shape[1:], a.dtype) for a in parts],
        scratch_shapes=[pltpu.SemaphoreType.DMA((n, 3)), pltpu.SemaphoreType.DMA((n, 3))], name=name)(*parts)


def _gather_split(arrs, *, name):
    n = len(arrs)

    def body(*refs):
        ins, outs = refs[:n], refs[n:2 * n]
        s_ici, r_ici, s_sib, r_sib, loc = refs[2 * n:]
        x, y, c, chips = _place()
        me = 2 * x + y
        ids = [2 * px + py for px, py in chips] + [me]

        def over_ici(a, k, slot):
            px, py = chips[k]
            return pltpu.make_async_remote_copy(src_ref=ins[a].at[c], dst_ref=outs[a].at[slot, c], send_sem=s_ici.at[a, k],
                                                recv_sem=r_ici.at[a, k], device_id=(px, py, c), device_id_type=MESH)

        def to_sibling(a, k, half, src=None):
            blk = outs[a].at[ids[k], half]
            return pltpu.make_async_remote_copy(src_ref=blk if src is None else src, dst_ref=blk, send_sem=s_sib.at[a, k],
                                                recv_sem=r_sib.at[a, k], device_id=(x, y, 1 - c), device_id_type=MESH)

        sends = [over_ici(a, k, me) for a in range(n) for k in range(3)]
        sends += [to_sibling(a, 3, c, src=ins[a].at[c]) for a in range(n)]
        for cp in sends:
            cp.start()
        mine = [pltpu.make_async_copy(ins[a].at[c], outs[a].at[me, c], loc.at[a]) for a in range(n)]
        for cp in mine:
            cp.start()
        for a in range(n):
            for k in range(3):
                over_ici(a, k, ids[k]).wait_recv()
                fwd = to_sibling(a, k, c)
                fwd.start()
                sends.append(fwd)
        for a in range(n):
            for k in range(4):
                to_sibling(a, k, 1 - c).wait_recv()
        for cp in sends:
            cp.wait_send()
        for cp in mine:
            cp.wait()

    dma = pltpu.SemaphoreType.DMA
    return pl.pallas_call(
        body, in_specs=[_ANY] * n, out_specs=[_ANY] * n, out_shape=[SDS((4,) + a.shape, a.dtype) for a in arrs],
        scratch_shapes=[dma((n, 3)), dma((n, 3)), dma((n, 4)), dma((n, 4)), dma((n,))], name=name)(*arrs)


def _swap_halves(parts, *, name):
    n = len(parts)

    def body(*refs):
        ins, gots = refs[:n], refs[n:2 * n]
        send, recv = refs[2 * n:]
        x, y, c, _ = _place()
        cps = [pltpu.make_async_remote_copy(src_ref=ins[a].at[q, 1 - c], dst_ref=gots[a].at[q], send_sem=send.at[a, q],
                                            recv_sem=recv.at[a, q], device_id=(x, y, 1 - c), device_id_type=MESH)
               for a in range(n) for q in range(4)]
        for cp in cps:
            cp.start()
        for cp in cps:
            cp.wait_recv()
        for cp in cps:
            cp.wait_send()

    dma = pltpu.SemaphoreType.DMA
    return pl.pallas_call(
        body, in_specs=[_ANY] * n, out_specs=[_ANY] * n, out_shape=[SDS((4,) + a.shape[2:], a.dtype) for a in parts],
        scratch_shapes=[dma((n, 4)), dma((n, 4))], name=name)(*parts)


def _sibling_copy(arrs, *, name):
    n = len(arrs)

    def body(*refs):
        ins, outs = refs[:n], refs[n:2 * n]
        send, recv = refs[2 * n:]
        x, y, c, _ = _place()
        cps = [pltpu.make_async_remote_copy(src_ref=ins[a], dst_ref=outs[a], send_sem=send.at[a], recv_sem=recv.at[a],
                                            device_id=(x, y, 1 - c), device_id_type=MESH) for a in range(n)]
        for cp in cps:
            cp.start()
        for cp in cps:
            cp.wait_recv()
        for cp in cps:
            cp.wait_send()

    dma = pltpu.SemaphoreType.DMA
    return pl.pallas_call(
        body, in_specs=[_ANY] * n, out_specs=[_ANY] * n, out_shape=[SDS(a.shape, a.dtype) for a in arrs],
        scratch_shapes=[dma((n,)), dma((n,))], name=name)(*arrs)


def _row_tile(R):
    return R if R <= 256 else next(t for t in (256, 128, 64, 32, 16) if R % t == 0)


def _add2(a, b, *, name):
    R, W = a.shape
    tr = _row_tile(R)
    return _pw(lambda p, q: p.astype(f32) + q.astype(f32), [a, b], [_row(tr, W)] * 2, [SDS((R, W), a.dtype)], [_row(tr, W)],
               (1, R // tr), name=name)[0]


def _sum4(own, got, *, name):
    R, W = own.shape
    tr = _row_tile(R)
    g3 = lambda k: pl.BlockSpec((None, tr, W), lambda j, i: (k, i, 0))
    up = lambda t: t.astype(f32)
    return _pw(lambda a, b, c, d: ((up(a) + up(b)) + up(c)) + up(d), [own, got, got, got], [_row(tr, W), g3(0), g3(1), g3(2)],
               [SDS((R, W), f32)], [_row(tr, W)], (1, R // tr), name=name)[0]


def _adamw(g, w, m, v, *, name):
    R, W = w.shape
    tr = _row_tile(R)

    def fn(gv, wv, mv, vv):
        m2 = ADAM_B1 * mv + (1.0 - ADAM_B1) * gv
        v2 = ADAM_B2 * vv + (1.0 - ADAM_B2) * jnp.square(gv)
        m_hat = m2 / (1.0 - ADAM_B1 ** ADAM_STEP)
        v_hat = v2 / (1.0 - ADAM_B2 ** ADAM_STEP)
        return -ADAM_LR * (m_hat / (jnp.sqrt(v_hat) + ADAM_EPS) + ADAM_WD * wv), m2, v2

    return _pw(fn, [g, w, m, v], [_row(tr, W)] * 4, [SDS((R, W), f32)] * 3, [_row(tr, W)] * 3, (1, R // tr), name=name)


def _to_parts(g, axis):
    shp = g.shape
    g = g.reshape(shp[:axis] + (4, shp[axis] // 4) + shp[axis + 1:])
    return jnp.moveaxis(g, axis, 0)


def _from_parts(pt, axis):
    g = jnp.moveaxis(pt, 0, axis)
    shp = g.shape
    return g.reshape(shp[:axis] + (4 * shp[axis + 1],) + shp[axis + 2:])


def kernel(x, norm1_g, w_in, ret_decay, ret_gn_g, mla_q_norm_g, mla_w_uq, mla_kv_norm_g, mla_w_ukv, s5_a_re, s5_a_im, s5_log_dt, s5_b_re, s5_b_im, s5_c_re, s5_c_im, s5_d, s5_w_glu, w_branch, w_out, norm2_g, ffn_w_gu, ffn_w_down, final_g, loss_target, m_norm1_g, m_w_in, m_ret_decay, m_ret_gn_g, m_mla_q_norm_g, m_mla_w_uq, m_mla_kv_norm_g, m_mla_w_ukv, m_s5_a_re, m_s5_a_im, m_s5_log_dt, m_s5_b_re, m_s5_b_im, m_s5_c_re, m_s5_c_im, m_s5_d, m_s5_w_glu, m_w_branch, m_w_out, m_norm2_g, m_ffn_w_gu, m_ffn_w_down, m_final_g, v_norm1_g, v_w_in, v_ret_decay, v_ret_gn_g, v_mla_q_norm_g, v_mla_w_uq, v_mla_kv_norm_g, v_mla_w_ukv, v_s5_a_re, v_s5_a_im, v_s5_log_dt, v_s5_b_re, v_s5_b_im, v_s5_c_re, v_s5_c_im, v_s5_d, v_s5_w_glu, v_w_branch, v_w_out, v_norm2_g, v_ffn_w_gu, v_ffn_w_down, v_final_g):
    wv = dict(zip(W_NAMES, (norm1_g, w_in, ret_decay, ret_gn_g, mla_q_norm_g, mla_w_uq, mla_kv_norm_g, mla_w_ukv, s5_a_re, s5_a_im,
                            s5_log_dt, s5_b_re, s5_b_im, s5_c_re, s5_c_im, s5_d, s5_w_glu, w_branch, w_out, norm2_g, ffn_w_gu,
                            ffn_w_down, final_g)))
    mv = dict(zip(W_NAMES, (m_norm1_g, m_w_in, m_ret_decay, m_ret_gn_g, m_mla_q_norm_g, m_mla_w_uq, m_mla_kv_norm_g, m_mla_w_ukv,
                            m_s5_a_re, m_s5_a_im, m_s5_log_dt, m_s5_b_re, m_s5_b_im, m_s5_c_re, m_s5_c_im, m_s5_d, m_s5_w_glu,
                            m_w_branch, m_w_out, m_norm2_g, m_ffn_w_gu, m_ffn_w_down, m_final_g)))
    vv = dict(zip(W_NAMES, (v_norm1_g, v_w_in, v_ret_decay, v_ret_gn_g, v_mla_q_norm_g, v_mla_w_uq, v_mla_kv_norm_g, v_mla_w_ukv,
                            v_s5_a_re, v_s5_a_im, v_s5_log_dt, v_s5_b_re, v_s5_b_im, v_s5_c_re, v_s5_c_im, v_s5_d, v_s5_w_glu,
                            v_w_branch, v_w_out, v_norm2_g, v_ffn_w_gu, v_ffn_w_down, v_final_g)))
    big_names = list(BIG)

    my_c = lax.axis_index("c")
    my_chip = 2 * lax.axis_index("x") + lax.axis_index("y")
    shards = [wv[n].astype(bf16) for n in big_names]
    gathered = _gather_split(shards, name="gather_weights")
    big = {n: _from_parts(gt, BIG[n]) for n, gt in zip(big_names, gathered)}
    small = {n: wv[n] for n in SMALL}

    loss_local, dx, layer_grads, d_final = _local_step(x[0], loss_target[0], big, small)
    grads = {n: jnp.stack([layer_grads[l][n] for l in range(DEPTH)]) for n in SMALL if n != 'final_g'}
    grads['final_g'] = d_final

    n_rows = {n: -(-math.prod(wv[n].shape) // 1024) * 8 for n in SMALL}
    used = sum(n_rows.values())
    rows_q = -(-(used + 8) // (4 * 128)) * 128

    def as_rows(d, tail=None):
        blocks = [jnp.pad(d[n].reshape(-1), (0, n_rows[n] * 128 - math.prod(wv[n].shape))).reshape(n_rows[n], 128) for n in SMALL]
        blocks.append(jnp.zeros((8, 128), f32) if tail is None else tail)
        blocks.append(jnp.zeros((4 * rows_q - used - 8, 128), f32))
        return jnp.concatenate(blocks, axis=0)

    loss_rows = jnp.full((8, 128), loss_local, f32)
    parts = [jnp.stack([_to_parts(layer_grads[l][n].astype(bf16), BIG[n] - 1) for l in range(DEPTH)], axis=1) for n in big_names]
    parts.append(as_rows(grads, loss_rows).reshape(4, 2, rows_q // 2, 128))
    n_arr = len(parts)
    two_d = lambda a: a.reshape(-1, a.shape[-1])
    theirs = _swap_halves(parts, name="grad_swap_halves")
    mine = [jnp.where(my_c == 0, p[:, 0], p[:, 1]) for p in parts]
    chip_sums = [_add2(two_d(mine[a]), two_d(theirs[a]), name=f"grad_add2_{a}").reshape(theirs[a].shape) for a in range(n_arr)]
    got = _rs_exchange(chip_sums, name="grad_exchange")

    def pick_chip(s):
        r = s[0]
        for q in range(1, 4):
            r = jnp.where(my_chip == q, s[q], r)
        return r

    own = [pick_chip(s) for s in chip_sums]
    sums = [_sum4(two_d(own[a]), got[a].reshape(3, -1, got[a].shape[-1]), name=f"grad_sum4_{a}") for a in range(n_arr)]
    other = _sibling_copy(sums, name="grad_sibling")
    full = [jnp.stack([jnp.where(my_c == 0, sums[a], other[a]), jnp.where(my_c == 0, other[a], sums[a])]) for a in range(n_arr)]

    out_g, out_d, out_m, out_v = {}, {}, {}, {}
    for a, n in enumerate(big_names):
        shp = wv[n].shape
        res = _adamw(two_d(full[a]), two_d(wv[n]), two_d(mv[n]), two_d(vv[n]), name=f"adamw_{n}")
        out_g[n] = full[a].reshape(shp)
        out_d[n], out_m[n], out_v[n] = [r.reshape(shp) for r in res]
    g_small = _allgather4([full[-1].reshape(rows_q, 128)], name="gather_small_grads")[0].reshape(4 * rows_q, 128)
    loss = g_small[used, 0]
    off = 0
    for n in SMALL:
        shp = wv[n].shape
        k = math.prod(shp)
        flat2 = (k // 128, 128) if k % 128 == 0 else (1, k)
        g_n = g_small[off:off + n_rows[n]].reshape(-1)[:k].reshape(flat2)
        res = _adamw(g_n, wv[n].reshape(flat2), mv[n].reshape(flat2), vv[n].reshape(flat2), name=f"adamw_{n}")
        out_g[n] = g_n.reshape(shp)
        out_d[n], out_m[n], out_v[n] = [r.reshape(shp) for r in res]
        off += n_rows[n]
    return (loss, dx[None], *[out_g[n] for n in W_NAMES], *[out_d[n] for n in W_NAMES], *[out_m[n] for n in W_NAMES],
            *[out_v[n] for n in W_NAMES])
```

```python
import functools
import math

import jax
import jax.numpy as jnp
from jax import lax
from jax.experimental import pallas as pl
from jax.experimental.pallas import tpu as pltpu

f32 = jnp.float32
bf16 = jnp.bfloat16
SDS = jax.ShapeDtypeStruct
MESH = pl.DeviceIdType.MESH

D = 1024
DEPTH = 2
RMS_EPS = 1e-6
GN_EPS = 1e-5
ROPE_THETA = 10000.0
RET_HEADS = 4
RET_DK = 128
RET_DV = 256
RET_CHUNK = 128
MLA_HEADS = 8
MLA_Q_LORA = 384
MLA_KV_LORA = 256
MLA_NOPE = 128
MLA_ROPE = 64
MLA_V = 128
MLA_QW = 256
S5_G = 64
S5_P = 64
S5_C = 16
S5_NJ = 8
S5_SEG = 8
FFN_H = 2816
ADAM_LR = 0.001
ADAM_B1 = 0.9
ADAM_B2 = 0.999
ADAM_EPS = 1e-08
ADAM_WD = 0.01
ADAM_STEP = 10
VMEM_BIG = 56 * 1024 * 1024

W_NAMES = ['norm1_g', 'w_in', 'ret_decay', 'ret_gn_g', 'mla_q_norm_g', 'mla_w_uq', 'mla_kv_norm_g', 'mla_w_ukv',
           's5_a_re', 's5_a_im', 's5_log_dt', 's5_b_re', 's5_b_im', 's5_c_re', 's5_c_im', 's5_d', 's5_w_glu',
           'w_branch', 'w_out', 'norm2_g', 'ffn_w_gu', 'ffn_w_down', 'final_g']
BIG = {'w_in': 2, 'mla_w_uq': 2, 'mla_w_ukv': 2, 's5_w_glu': 2, 'w_branch': 2, 'w_out': 1, 'ffn_w_gu': 2, 'ffn_w_down': 1}
SMALL = [n for n in W_NAMES if n not in BIG]


TILE_BYTES = 6 * 1024 * 1024


def _pick(n, cands=(512, 384, 256, 128), cap=None):
    if n <= 1024 and (cap is None or n <= cap):
        return n
    for c in cands:
        if n % c == 0 and (cap is None or c <= cap):
            return c
    raise ValueError(n)


WIDE = (1408, 1024, 768, 512, 384, 256, 128)


def _params(sem, vmem=None):
    return pltpu.CompilerParams(dimension_semantics=sem, vmem_limit_bytes=vmem)


def _mm(a, b, *, tb=False, res=None, out_dtype=f32, name):
    M, K = a.shape
    N = b.shape[0] if tb else b.shape[1]
    tk = K if K <= 3072 else _pick(K, (1408, 1024, 512))
    nk = K // tk
    tn = _pick(N, WIDE, cap=TILE_BYTES // (tk * b.dtype.itemsize))
    tm = _pick(M)
    if M % 1024 == 0 and 1024 * tk * a.dtype.itemsize <= 4 * 1024 * 1024 and 1024 * tn * 4 <= TILE_BYTES:
        tm = 1024
    assert M % tm == 0 and N % tn == 0 and K % tk == 0

    def body(*refs):
        if res is None:
            a_ref, b_ref, o_ref, acc = refs
        else:
            a_ref, b_ref, r_ref, o_ref, acc = refs
        k = pl.program_id(2)
        dn = (((1,), (1 if tb else 0,)), ((), ()))
        part = lax.dot_general(a_ref[...].astype(bf16), b_ref[...].astype(bf16), dn, preferred_element_type=f32)

        @pl.when(k == 0)
        def _():
            acc[...] = part

        @pl.when(k > 0)
        def _():
            acc[...] += part

        @pl.when(k == nk - 1)
        def _():
            v = acc[...]
            if res is not None:
                v = v + r_ref[...]
            o_ref[...] = v.astype(out_dtype)

    in_specs = [pl.BlockSpec((tm, tk), lambda i, j, k: (i, k)),
                pl.BlockSpec((tn, tk), lambda i, j, k: (j, k)) if tb else pl.BlockSpec((tk, tn), lambda i, j, k: (k, j))]
    args = [a, b]
    if res is not None:
        in_specs.append(pl.BlockSpec((tm, tn), lambda i, j, k: (i, j)))
        args.append(res)
    return pl.pallas_call(
        body, grid=(M // tm, N // tn, nk), in_specs=in_specs,
        out_specs=pl.BlockSpec((tm, tn), lambda i, j, k: (i, j)),
        out_shape=SDS((M, N), out_dtype), scratch_shapes=[pltpu.VMEM((tm, tn), f32)],
        compiler_params=_params(("parallel", "parallel", "arbitrary"), VMEM_BIG), name=name)(*args)


def _mmT(a, b, *, name):
    S, M = a.shape
    N = b.shape[1]
    tn = _pick(N, WIDE)
    tm = _pick(M, WIDE, cap=TILE_BYTES // (tn * 4))
    tk = min(S, 1024)
    nk = S // tk

    def body(a_ref, b_ref, o_ref):
        k = pl.program_id(2)
        part = lax.dot_general(a_ref[...].astype(bf16), b_ref[...].astype(bf16), (((0,), (0,)), ((), ())),
                               preferred_element_type=f32)

        @pl.when(k == 0)
        def _():
            o_ref[...] = part

        @pl.when(k > 0)
        def _():
            o_ref[...] += part

    return pl.pallas_call(
        body, grid=(M // tm, N // tn, nk),
        in_specs=[pl.BlockSpec((tk, tm), lambda i, j, k: (k, i)), pl.BlockSpec((tk, tn), lambda i, j, k: (k, j))],
        out_specs=pl.BlockSpec((tm, tn), lambda i, j, k: (i, j)),
        out_shape=SDS((M, N), f32),
        compiler_params=_params(("parallel", "parallel", "arbitrary"), VMEM_BIG), name=name)(a, b)


def _pw(fn, ins, in_specs, outs, out_specs, grid, *, n_acc=0, name):
    n_in = len(ins)
    n_out = len(outs)

    def body(*refs):
        vals = fn(*[r[...].astype(f32) if r.dtype == bf16 else r[...] for r in refs[:n_in]])
        if not isinstance(vals, (tuple, list)):
            vals = (vals,)
        orefs = refs[n_in:]
        for r, v in zip(orefs[:n_out - n_acc], vals[:n_out - n_acc]):
            r[...] = v.astype(r.dtype)
        if n_acc:
            i = pl.program_id(1)

            @pl.when(i == 0)
            def _():
                for r, v in zip(orefs[n_out - n_acc:], vals[n_out - n_acc:]):
                    r[...] = v.astype(r.dtype)

            @pl.when(i > 0)
            def _():
                for r, v in zip(orefs[n_out - n_acc:], vals[n_out - n_acc:]):
                    r[...] += v.astype(r.dtype)

    res = pl.pallas_call(
        body, grid=grid, in_specs=in_specs, out_specs=out_specs, out_shape=outs,
        compiler_params=_params(("parallel", "arbitrary"), VMEM_BIG), name=name)(*ins)
    return res


def _row(T, w, col=None):
    if col is None:
        return pl.BlockSpec((T, w), lambda j, i: (i, 0))
    return pl.BlockSpec((T, w), lambda j, i: (i, col(j)))


def _par(w, col=None):
    if col is None:
        return pl.BlockSpec((1, w), lambda j, i: (0, 0))
    return pl.BlockSpec((1, w), lambda j, i: (0, col(j)))


def _rms(x, g):
    return x * lax.rsqrt(jnp.mean(x * x, axis=-1, keepdims=True) + RMS_EPS) * g


def _rope(x, cos, sinm, half):
    if half == 64:
        partner = pltpu.roll(x, 64, axis=1)
    else:
        lane = lax.broadcasted_iota(jnp.int32, x.shape, 1)
        partner = jnp.where((lane % (2 * half)) < half, pltpu.roll(x, 128 - half, axis=1), pltpu.roll(x, half, axis=1))
    return x * cos + partner * sinm


def _rope_t(x, cos, sinm, half):
    return _rope(x, cos, -sinm, half)


def _rmsnorm_fwd(x, g, *, name):
    S, W = x.shape
    T = min(S, 512)
    return _pw(lambda xv, gv: _rms(xv, gv), [x, g], [_row(T, W), _par(W)], [SDS((S, W), bf16)], [_row(T, W)],
               (1, S // T), name=name)[0]


def _rmsnorm_bwd(x, g, dh, dres, *, name):
    S, W = x.shape
    T = min(S, 512)

    def fn(xv, gv, dhv, drv):
        _, vjp = jax.vjp(_rms, xv, gv)
        dx, dg = vjp(dhv)
        return dx + drv, dg

    return _pw(fn, [x, g, dh, dres], [_row(T, W), _par(W), _row(T, W), _row(T, W)],
               [SDS((S, W), f32), SDS((1, W), f32)], [_row(T, W), _par(W)], (1, S // T), n_acc=1, name=name)


def _ret_tables(lg, reverse):
    C = RET_CHUNK
    ii = lax.broadcasted_iota(jnp.int32, (C, C), 0).astype(f32)
    jj = lax.broadcasted_iota(jnp.int32, (C, C), 1).astype(f32)
    if not reverse:
        E = ii - jj
        mask = E >= 0
        eq = ii + 1.0
        ek = (C - 1.0) - ii
    else:
        E = jj - ii
        mask = E > 0
        eq = C - ii
        ek = ii
    Dm = jnp.where(mask, jnp.exp(jnp.where(mask, E, 0.0) * lg), 0.0)
    Em = jnp.where(mask, E, 0.0)
    qw = jnp.exp(eq * lg)
    kw = jnp.exp(ek * lg)
    qw2 = jnp.concatenate([qw, qw], axis=1)
    return Dm, Em, eq, ek, qw, kw, qw2, jnp.exp(C * lg)


def _dot(a, b, dims):
    return lax.dot_general(a.astype(bf16), b.astype(bf16), (dims, ((), ())), preferred_element_type=f32)


NN = ((1,), (0,))
NT = ((1,), (1,))
TN = ((0,), (0,))


def _ret_dir_fwd(zr, lg, cos, sinm, *, reverse, name):
    S = zr.shape[0]
    C = RET_CHUNK
    TB = min(S, 512)
    nc = TB // C
    NB = S // TB
    d = 1 if reverse else 0
    scale = RET_DK ** -0.5

    def tb(b):
        return (NB - 1 - b) if reverse else b

    def body(lg_ref, q_ref, k_ref, v_ref, cos_ref, sin_ref, y_ref, st_ref, state):
        h = pl.program_id(0)
        b = pl.program_id(1)

        @pl.when(b == 0)
        def _():
            state[...] = jnp.zeros_like(state)

        Dm, _, _, _, _, kw, qw2, gC = _ret_tables(lg_ref[d, h], reverse)
        order = range(nc - 1, -1, -1) if reverse else range(nc)
        for c in order:
            rows = pl.ds(c * C, C)
            q = _rope(q_ref[rows, :], cos_ref[rows, :], sin_ref[rows, :], 64) * scale
            k = _rope(k_ref[rows, :], cos_ref[rows, :], sin_ref[rows, :], 64)
            v = v_ref[rows, :]
            st = state[...]
            st_ref[0, c] = st
            s = _dot(q, k, NT) * Dm
            o = _dot(s, v, NN) + _dot(q, st, NN) * qw2
            y_ref[rows, :] = o
            state[...] = gC * st + _dot(k * kw, v, TN)

    return pl.pallas_call(
        body, grid=(RET_HEADS, NB),
        in_specs=[pl.BlockSpec(memory_space=pltpu.SMEM),
                  pl.BlockSpec((TB, 128), lambda h, b: (tb(b), h)),
                  pl.BlockSpec((TB, 128), lambda h, b: (tb(b), 4 + h)),
                  pl.BlockSpec((TB, 256), lambda h, b: (tb(b), 4 + h)),
                  pl.BlockSpec((TB, 128), lambda h, b: (tb(b), 0)),
                  pl.BlockSpec((TB, 128), lambda h, b: (tb(b), 0))],
        out_specs=[pl.BlockSpec((TB, 256), lambda h, b: (tb(b), h)),
                   pl.BlockSpec((1, nc, 128, 256), lambda h, b: (h, tb(b), 0, 0))],
        out_shape=[SDS((S, 1024), f32), SDS((RET_HEADS, S // C, 128, 256), f32)],
        scratch_shapes=[pltpu.VMEM((128, 256), f32)],
        compiler_params=_params(("parallel", "arbitrary")), name=name)(lg, zr, zr, zr, cos, sinm)


def _ret_dir_bwd(zr, lg, cos, sinm, dy, states, *, reverse, name):
    S = zr.shape[0]
    C = RET_CHUNK
    TB = min(S, 512)
    nc = TB // C
    NB = S // TB
    d = 1 if reverse else 0
    scale = RET_DK ** -0.5

    def tb(b):
        return b if reverse else (NB - 1 - b)

    def body(lg_ref, q_ref, k_ref, v_ref, cos_ref, sin_ref, dy_ref, st_ref, dq_ref, dk_ref, dv_ref, dlg_ref, dstate):
        h = pl.program_id(0)
        b = pl.program_id(1)

        @pl.when(b == 0)
        def _():
            dstate[...] = jnp.zeros_like(dstate)
            dlg_ref[...] = jnp.zeros_like(dlg_ref)

        Dm, Em, eq, ek, qw, kw, qw2, gC = _ret_tables(lg_ref[d, h], reverse)
        order = range(nc) if reverse else range(nc - 1, -1, -1)
        dlg = jnp.zeros((), f32)
        for c in order:
            rows = pl.ds(c * C, C)
            cs, sn = cos_ref[rows, :], sin_ref[rows, :]
            q = _rope(q_ref[rows, :], cs, sn, 64) * scale
            k = _rope(k_ref[rows, :], cs, sn, 64)
            v = v_ref[rows, :]
            do = dy_ref[rows, :]
            st = st_ref[0, c]
            ds = dstate[...]
            p = _dot(q, k, NT)
            a = p * Dm
            dp = _dot(do, v, NT) * Dm
            dq_cross = _dot(do, st, NT) * qw
            dk_cross = _dot(v, ds, NT) * kw
            dq = _dot(dp, k, NN) + dq_cross
            dk = _dot(dp, q, TN) + dk_cross
            dv = _dot(a, do, TN) + _dot(k * kw, ds, NN)
            dlg = dlg + jnp.sum(dp * p * Em) + jnp.sum(dq_cross * q * eq) + jnp.sum(dk_cross * k * ek) \
                + C * gC * jnp.sum(ds * st)
            dstate[...] = gC * ds + _dot(q * qw, do, TN)
            dq_ref[rows, :] = _rope_t(dq, cs, sn, 64) * scale
            dk_ref[rows, :] = _rope_t(dk, cs, sn, 64)
            dv_ref[rows, :] = dv
        dlg_ref[...] += jnp.full(dlg_ref.shape, dlg, f32)

    return pl.pallas_call(
        body, grid=(RET_HEADS, NB),
        in_specs=[pl.BlockSpec(memory_space=pltpu.SMEM),
                  pl.BlockSpec((TB, 128), lambda h, b: (tb(b), h)),
                  pl.BlockSpec((TB, 128), lambda h, b: (tb(b), 4 + h)),
                  pl.BlockSpec((TB, 256), lambda h, b: (tb(b), 4 + h)),
                  pl.BlockSpec((TB, 128), lambda h, b: (tb(b), 0)),
                  pl.BlockSpec((TB, 128), lambda h, b: (tb(b), 0)),
                  pl.BlockSpec((TB, 256), lambda h, b: (tb(b), h)),
                  pl.BlockSpec((1, nc, 128, 256), lambda h, b: (h, tb(b), 0, 0))],
        out_specs=[pl.BlockSpec((TB, 128), lambda h, b: (tb(b), h)),
                   pl.BlockSpec((TB, 128), lambda h, b: (tb(b), h)),
                   pl.BlockSpec((TB, 256), lambda h, b: (tb(b), h)),
                   pl.BlockSpec((1, 1, 128), lambda h, b: (h, 0, 0))],
        out_shape=[SDS((S, 512), f32), SDS((S, 512), f32), SDS((S, 1024), f32), SDS((RET_HEADS, 1, 128), f32)],
        scratch_shapes=[pltpu.VMEM((128, 256), f32)],
        compiler_params=_params(("parallel", "arbitrary")), name=name)(lg, zr, zr, zr, cos, sinm, dy, states)


def _gn_gate(yf, yb, g, gn):
    y = yf + yb
    mu = jnp.mean(y, axis=-1, keepdims=True)
    var = jnp.mean(jnp.square(y - mu), axis=-1, keepdims=True)
    yn = (y - mu) * lax.rsqrt(var + GN_EPS)
    return jax.nn.silu(g) * (yn * gn)


def _flash_fwd(Q, K, kv, *, name):
    S = Q.shape[0]
    hq = min(S, 256)
    nh = 4 if S % 1024 == 0 else 1
    tq = nh * hq
    tk = min(S, 8192)
    nk = S // tk

    def body(q_ref, k_ref, v_ref, o_ref, l_ref, m_s, l_s, acc):
        kk = pl.program_id(2)

        @pl.when(kk == 0)
        def _():
            m_s[...] = jnp.full_like(m_s, -jnp.inf)
            l_s[...] = jnp.zeros_like(l_s)
            acc[...] = jnp.zeros_like(acc)

        k = k_ref[...]
        v = v_ref[...]
        sts = [lax.dot_general(k, q_ref[hf * hq:(hf + 1) * hq, :], (NT, ((), ())), preferred_element_type=f32)
               for hf in range(nh)]
        for hf in range(nh):
            st = sts[hf]
            m_prev = m_s[hf]
            m_new = jnp.maximum(m_prev, jnp.max(st, axis=0, keepdims=True))
            pt = jnp.exp2(st - m_new)
            alpha = jnp.exp2(m_prev - m_new)
            l_s[hf] = alpha * l_s[hf] + jnp.sum(pt, axis=0, keepdims=True)
            acc[hf] = alpha * acc[hf] + lax.dot_general(v, pt.astype(bf16), (TN, ((), ())), preferred_element_type=f32)
            m_s[hf] = m_new

        @pl.when(kk == nk - 1)
        def _():
            for hf in range(nh):
                o_ref[hf * hq:(hf + 1) * hq, :] = jnp.transpose(acc[hf] / l_s[hf]).astype(bf16)
                l_ref[0, :, hf * hq:(hf + 1) * hq] = m_s[hf] + jnp.log2(l_s[hf])

    return pl.pallas_call(
        body, grid=(MLA_HEADS, S // tq, nk),
        in_specs=[pl.BlockSpec((tq, 256), lambda h, i, k: (i, h)),
                  pl.BlockSpec((tk, 256), lambda h, i, k: (k, h)),
                  pl.BlockSpec((tk, 128), lambda h, i, k: (k, 2 * h + 1))],
        out_specs=[pl.BlockSpec((tq, 128), lambda h, i, k: (i, h)), pl.BlockSpec((1, 1, tq), lambda h, i, k: (h, 0, i))],
        out_shape=[SDS((S, 1024), bf16), SDS((MLA_HEADS, 1, S), f32)],
        scratch_shapes=[pltpu.VMEM((nh, 1, hq), f32), pltpu.VMEM((nh, 1, hq), f32), pltpu.VMEM((nh, 128, hq), f32)],
        compiler_params=_params(("parallel", "parallel", "arbitrary"), VMEM_BIG), name=name)(Q, K, kv)


def _attn_delta(dO, O, *, name):
    S = dO.shape[0]
    T = min(S, 512)

    def body(do_ref, o_ref, d_ref):
        ones = jnp.ones((8, 128), bf16)
        for h in range(MLA_HEADS):
            cols = slice(128 * h, 128 * h + 128)
            prod = do_ref[:, cols].astype(f32) * o_ref[:, cols].astype(f32)
            hi = prod.astype(bf16)
            lo = (prod - hi.astype(f32)).astype(bf16)
            row = lax.dot_general(ones, hi, (NT, ((), ())), preferred_element_type=f32) \
                + lax.dot_general(ones, lo, (NT, ((), ())), preferred_element_type=f32)
            d_ref[h] = row[0:1, :]

    return pl.pallas_call(
        body, grid=(S // T,),
        in_specs=[pl.BlockSpec((T, 1024), lambda i: (i, 0)), pl.BlockSpec((T, 1024), lambda i: (i, 0))],
        out_specs=pl.BlockSpec((MLA_HEADS, 1, T), lambda i: (0, 0, i)), out_shape=SDS((MLA_HEADS, 1, S), f32),
        compiler_params=_params(("parallel",)), name=name)(dO, O)


def _flash_bwd(Q, K, kv, delta, L, dO, *, name):
    S = Q.shape[0]
    hq = min(S, 512)
    nh = 2 if S % 1024 == 0 else 1
    tq = nh * hq
    tk = min(S, 2048)
    nq = S // tq
    ln2 = math.log(2.0)

    def body(q_ref, k_ref, v_ref, dl_ref, l_ref, do_ref, dq_ref, dk_ref, dv_ref, dk_acc, dv_acc):
        kk = pl.program_id(1)
        i = pl.program_id(2)

        @pl.when((kk == 0) & (i == 0))
        def _():
            dq_ref[...] = jnp.zeros_like(dq_ref)

        @pl.when(i == 0)
        def _():
            dk_acc[...] = jnp.zeros_like(dk_acc)
            dv_acc[...] = jnp.zeros_like(dv_acc)

        k = k_ref[...]
        v = v_ref[...]
        dk_new = dk_acc[...]
        dv_new = dv_acc[...]
        for hf in range(nh):
            sl = slice(hf * hq, (hf + 1) * hq)
            q = q_ref[sl, :]
            st = lax.dot_general(k, q, (NT, ((), ())), preferred_element_type=f32)
            pt = jnp.exp2(st - l_ref[0, :, sl])
            delta = dl_ref[0, :, sl]
            dob = do_ref[sl, :].astype(bf16)
            dv_new = dv_new + lax.dot_general(pt.astype(bf16), dob, (NN, ((), ())), preferred_element_type=f32)
            dpt = lax.dot_general(v, dob, (NT, ((), ())), preferred_element_type=f32)
            dst = (pt * (dpt - delta)).astype(bf16)
            dk_new = dk_new + lax.dot_general(dst, q, (NN, ((), ())), preferred_element_type=f32)
            dq_ref[0, i * nh + hf] += lax.dot_general(k, dst, (TN, ((), ())), preferred_element_type=f32)
        dk_acc[...] = dk_new
        dv_acc[...] = dv_new

        @pl.when(i == nq - 1)
        def _():
            dk_ref[...] = dk_acc[...] * ln2
            dv_ref[...] = dv_acc[...]

    return pl.pallas_call(
        body, grid=(MLA_HEADS, S // tk, nq),
        in_specs=[pl.BlockSpec((tq, 256), lambda h, k, i: (i, h)),
                  pl.BlockSpec((tk, 256), lambda h, k, i: (k, h)),
                  pl.BlockSpec((tk, 128), lambda h, k, i: (k, 2 * h + 1)),
                  pl.BlockSpec((1, 1, tq), lambda h, k, i: (h, 0, i)),
                  pl.BlockSpec((1, 1, tq), lambda h, k, i: (h, 0, i)),
                  pl.BlockSpec((tq, 128), lambda h, k, i: (i, h))],
        out_specs=[pl.BlockSpec((1, S // hq, 256, hq), lambda h, k, i: (h, 0, 0, 0)),
                   pl.BlockSpec((tk, 256), lambda h, k, i: (k, h)),
                   pl.BlockSpec((tk, 128), lambda h, k, i: (k, h))],
        out_shape=[SDS((MLA_HEADS, S // hq, 256, hq), f32), SDS((S, 2048), f32), SDS((S, 1024), f32)],
        scratch_shapes=[pltpu.VMEM((tk, 256), f32), pltpu.VMEM((tk, 128), f32)],
        compiler_params=_params(("parallel", "arbitrary", "arbitrary"), VMEM_BIG), name=name)(Q, K, kv, delta, L, dO)


def _mla_qk_prep(q, kv, zm, cosm, sinm, *, name):
    S = q.shape[0]
    T = min(S, 256)
    scale = (MLA_NOPE + MLA_ROPE) ** -0.5 * math.log2(math.e)

    def body(q_ref, kv_ref, kr_ref, cos_ref, sin_ref, oq_ref, ok_ref):
        cs, sn = cos_ref[...], sin_ref[...]
        kr = _rope(kr_ref[...], cs, sn, 32).astype(bf16)
        for h in range(MLA_HEADS):
            a = 256 * h
            oq_ref[:, a:a + 128] = (q_ref[:, a:a + 128].astype(f32) * scale).astype(bf16)
            oq_ref[:, a + 128:a + 256] = (_rope(q_ref[:, a + 128:a + 256].astype(f32), cs, sn, 32) * scale).astype(bf16)
            ok_ref[:, a:a + 128] = kv_ref[:, a:a + 128]
            ok_ref[:, a + 128:a + 256] = kr

    row = lambda w, col=0: pl.BlockSpec((T, w), lambda i: (i, col))
    return pl.pallas_call(
        body, grid=(S // T,), in_specs=[row(2048), row(2048), row(128, 6), row(128), row(128)],
        out_specs=[row(2048), row(2048)], out_shape=[SDS((S, 2048), bf16), SDS((S, 2048), bf16)],
        compiler_params=_params(("parallel",), VMEM_BIG), name=name)(q, kv, zm, cosm, sinm)


def _mla_bwd_prep(dQ, dK, dV, cosm, sinm, *, name):
    S = dK.shape[0]
    T = dQ.shape[3]
    scale = (MLA_NOPE + MLA_ROPE) ** -0.5

    def body(dq_ref, dk_ref, dv_ref, cos_ref, sin_ref, oq_ref, okv_ref, okr_ref):
        cs, sn = cos_ref[...], sin_ref[...]
        kr = jnp.zeros((T, 128), f32)
        for h in range(MLA_HEADS):
            a = 256 * h
            dq = jnp.transpose(dq_ref[h, 0])
            oq_ref[:, a:a + 128] = (dq[:, 0:128] * scale).astype(bf16)
            oq_ref[:, a + 128:a + 256] = (_rope_t(dq[:, 128:256], cs, sn, 32) * scale).astype(bf16)
            okv_ref[:, a:a + 128] = dk_ref[:, a:a + 128].astype(bf16)
            okv_ref[:, a + 128:a + 256] = dv_ref[:, 128 * h:128 * h + 128].astype(bf16)
            kr = kr + dk_ref[:, a + 128:a + 256]
        okr_ref[...] = _rope_t(kr, cs, sn, 32)

    return pl.pallas_call(
        body, grid=(S // T,),
        in_specs=[pl.BlockSpec((MLA_HEADS, 1, 256, T), lambda i: (0, i, 0, 0)), pl.BlockSpec((T, 2048), lambda i: (i, 0)),
                  pl.BlockSpec((T, 1024), lambda i: (i, 0)), pl.BlockSpec((T, 128), lambda i: (i, 0)),
                  pl.BlockSpec((T, 128), lambda i: (i, 0))],
        out_specs=[pl.BlockSpec((T, 2048), lambda i: (i, 0)), pl.BlockSpec((T, 2048), lambda i: (i, 0)),
                   pl.BlockSpec((T, 128), lambda i: (i, 0))],
        out_shape=[SDS((S, 2048), bf16), SDS((S, 2048), bf16), SDS((S, 128), f32)],
        compiler_params=_params(("parallel",), VMEM_BIG), name=name)(dQ, dK, dV, cosm, sinm)


def _mla_norm_bwd(zm, qg, kvg, dcqn, dckvn, dkr, *, name):
    S = zm.shape[0]
    T = min(S, 512)

    def body(cq_ref, ckv_ref, qg_ref, kvg_ref, dcq_ref, dckv_ref, dkr_ref, o_ref, dqg_ref, dkvg_ref):
        i = pl.program_id(0)
        _, vjp = jax.vjp(_rms, cq_ref[...], qg_ref[...])
        dcq, dqg = vjp(dcq_ref[...])
        _, vjp2 = jax.vjp(_rms, ckv_ref[...], kvg_ref[...])
        dckv, dkvg = vjp2(dckv_ref[...])
        o_ref[:, 0:384] = dcq.astype(bf16)
        o_ref[:, 384:512] = jnp.zeros((T, 128), bf16)
        o_ref[:, 512:768] = dckv.astype(bf16)
        o_ref[:, 768:896] = dkr_ref[...].astype(bf16)

        @pl.when(i == 0)
        def _():
            dqg_ref[...] = dqg
            dkvg_ref[...] = dkvg

        @pl.when(i > 0)
        def _():
            dqg_ref[...] += dqg
            dkvg_ref[...] += dkvg

    return pl.pallas_call(
        body, grid=(S // T,),
        in_specs=[pl.BlockSpec((T, 384), lambda i: (i, 0)), pl.BlockSpec((T, 256), lambda i: (i, 2)),
                  pl.BlockSpec((1, 384), lambda i: (0, 0)), pl.BlockSpec((1, 256), lambda i: (0, 0)),
                  pl.BlockSpec((T, 384), lambda i: (i, 0)), pl.BlockSpec((T, 256), lambda i: (i, 0)),
                  pl.BlockSpec((T, 128), lambda i: (i, 0))],
        out_specs=[pl.BlockSpec((T, 896), lambda i: (i, 0)), pl.BlockSpec((1, 384), lambda i: (0, 0)),
                   pl.BlockSpec((1, 256), lambda i: (0, 0))],
        out_shape=[SDS((S, 896), bf16), SDS((1, 384), f32), SDS((1, 256), f32)],
        compiler_params=_params(("arbitrary",)), name=name)(zm, zm, qg, kvg, dcqn, dckvn, dkr)


def _s5_disc(a_re, a_im, ldt, b_re, b_im):
    dt = jnp.exp(ldt)
    ar = jnp.minimum(a_re, -1e-4)
    mag = jnp.exp(dt * ar)
    abr = mag * jnp.cos(dt * a_im)
    abi = mag * jnp.sin(dt * a_im)
    den = ar * ar + a_im * a_im
    nr = abr - 1.0
    ni = abi
    cr = (nr * ar + ni * a_im) / den
    ci = (ni * ar - nr * a_im) / den
    return abr, abi, cr * b_re - ci * b_im, cr * b_im + ci * b_re


def _s5_param_fwd(a_re, a_im, ldt, b_re, b_im, *, name):
    R = SDS((1, 8192), f32)
    M = SDS((16, 8192), f32)
    Pw = SDS((64, 8192), f32)

    def body(a_re_r, a_im_r, ldt_r, b_re_r, b_im_r, o1, o2, o3, o4, p_re, p_im):
        abr, abi, bbr, bbi = _s5_disc(a_re_r[...], a_im_r[...], ldt_r[...], b_re_r[...], b_im_r[...])
        o1[...] = abr
        o2[...] = abi
        o3[...] = bbr
        o4[...] = bbi
        dt = jnp.exp(ldt_r[...])
        ar = jnp.minimum(a_re_r[...], -1e-4)
        n = lax.broadcasted_iota(jnp.int32, (64, 8192), 0).astype(f32) + 1.0
        mag = jnp.exp(n * (dt * ar))
        ang = n * (dt * a_im_r[...])
        p_re[...] = mag * jnp.cos(ang)
        p_im[...] = mag * jnp.sin(ang)

    return pl.pallas_call(body, out_shape=[R, R, M, M, Pw, Pw], name=name)(a_re, a_im, ldt, b_re, b_im)


def _s5_param_bwd(a_re, a_im, ldt, b_re, b_im, d_abr, d_abi, d_bbr, d_bbi, *, name):
    R = SDS((1, 8192), f32)
    M = SDS((16, 8192), f32)

    def body(a_re_r, a_im_r, ldt_r, b_re_r, b_im_r, c1, c2, c3, c4, o1, o2, o3, o4, o5):
        _, vjp = jax.vjp(_s5_disc, a_re_r[...], a_im_r[...], ldt_r[...], b_re_r[...], b_im_r[...])
        g = vjp((c1[...], c2[...], c3[...], c4[...]))
        for o, v in zip((o1, o2, o3, o4, o5), g):
            o[...] = v

    return pl.pallas_call(body, out_shape=[R, R, R, M, M], name=name)(a_re, a_im, ldt, b_re, b_im, d_abr, d_abi, d_bbr, d_bbi)


def _seg_perm(T, inverse):
    L = T // S5_SEG
    i = jnp.arange(T)
    src = (i % S5_SEG) * L + i // S5_SEG
    P = (src[:, None] == jnp.arange(T)[None, :]).astype(bf16)
    return P.T if inverse else P


def _perm_rows(a, P, *, name):
    S, W = a.shape
    T = P.shape[0]

    def body(p_ref, a_ref, o_ref):
        o_ref[...] = lax.dot_general(p_ref[...], a_ref[...], (NN, ((), ())), preferred_element_type=f32).astype(o_ref.dtype)

    return pl.pallas_call(
        body, grid=(S // T,), in_specs=[pl.BlockSpec((T, T), lambda i: (0, 0)), pl.BlockSpec((T, W), lambda i: (i, 0))],
        out_specs=pl.BlockSpec((T, W), lambda i: (i, 0)), out_shape=SDS((S, W), a.dtype),
        compiler_params=_params(("parallel",)), name=name)(P, a)


def _scan_core(xr, xi, ar, ai, pwr_ref, pwi_ref, a64r, a64i, carry, *, reverse, T, conj):
    L = T // S5_SEG
    sg = -1.0 if conj else 1.0
    arb = jnp.broadcast_to(ar, (8, 512))
    aib = jnp.broadcast_to(ai, (8, 512))
    UN = 4

    def step(r4, c):
        cr, ci = c
        for u in range(UN):
            r0 = r4 * UN + u
            r = (L - 1 - r0) if reverse else r0
            rows = pl.ds(pl.multiple_of(r * 8, 8), 8)
            nr = arb * cr - aib * ci + xr[rows, :]
            ni = arb * ci + aib * cr + xi[rows, :]
            xr[rows, :] = nr
            xi[rows, :] = ni
            cr, ci = nr, ni
        return cr, ci

    lr, li = lax.fori_loop(0, L // UN, step, (jnp.zeros((8, 512), f32), jnp.zeros((8, 512), f32)))
    row8 = lax.broadcasted_iota(jnp.int32, (8, 512), 0)
    cr = carry[0, 0:1, :]
    ci = carry[1, 0:1, :]
    a6i = sg * a64i
    cin_r = jnp.zeros((8, 512), f32)
    cin_i = jnp.zeros((8, 512), f32)
    for seg in (range(S5_SEG - 1, -1, -1) if reverse else range(S5_SEG)):
        cin_r = jnp.where(row8 == seg, cr, cin_r)
        cin_i = jnp.where(row8 == seg, ci, cin_i)
        ncr = lr[seg:seg + 1, :] + a64r * cr - a6i * ci
        nci = li[seg:seg + 1, :] + a64r * ci + a6i * cr
        cr, ci = ncr, nci
    carry[0, 0:1, :] = cr
    carry[1, 0:1, :] = ci

    def fix(r4, _):
        for u in range(UN):
            r = r4 * UN + u
            rows = pl.ds(pl.multiple_of(r * 8, 8), 8)
            pr = pwr_ref[pl.ds(r, 1), :]
            pi = sg * pwi_ref[pl.ds(r, 1), :]
            xr[rows, :] += pr * cin_r - pi * cin_i
            xi[rows, :] += pr * cin_i + pi * cin_r
        return 0

    lax.fori_loop(0, L // UN, fix, 0)


def _s5_scan_fwd(u, BBr, BBi, CCr, CCi, abr, abi, pwr, pwi, *, reverse, name):
    S = u.shape[0]
    T = min(S, 512)
    NB = S // T
    L = T // S5_SEG
    d = 1 if reverse else 0

    def tb(b):
        return (NB - 1 - b) if reverse else b

    def body(u_ref, bbr_ref, bbi_ref, ccr_ref, cci_ref, ar_ref, ai_ref, pwr_ref, pwi_ref, y_ref, xr_ref, xi_ref, carry):
        b = pl.program_id(1)

        @pl.when(b == 0)
        def _():
            carry[...] = jnp.zeros_like(carry)

        ub = u_ref[...].astype(bf16)
        xr_ref[...] = lax.dot_general(ub, bbr_ref[0, 0], (NN, ((), ())), preferred_element_type=f32)
        xi_ref[...] = lax.dot_general(ub, bbi_ref[0, 0], (NN, ((), ())), preferred_element_type=f32)
        a6 = (0 if reverse else L - 1)
        _scan_core(xr_ref, xi_ref, ar_ref[...], ai_ref[...], pwr_ref, pwi_ref, pwr_ref[a6:a6 + 1, :], pwi_ref[a6:a6 + 1, :],
                   carry, reverse=reverse, T=T, conj=False)
        y_ref[...] = _dot(xr_ref[...], ccr_ref[0, 0], NN) - _dot(xi_ref[...], cci_ref[0, 0], NN)

    mat = lambda shp: pl.BlockSpec((1, 1) + shp, lambda j, b: (d, j, 0, 0))
    vec = lambda r: pl.BlockSpec((r, 512), lambda j, b: (0, d * S5_NJ + j))
    return pl.pallas_call(
        body, grid=(S5_NJ, NB),
        in_specs=[pl.BlockSpec((T, 128), lambda j, b: (tb(b), j)), mat((128, 512)), mat((128, 512)), mat((512, 128)),
                  mat((512, 128)), vec(1), vec(1), vec(L), vec(L)],
        out_specs=[pl.BlockSpec((T, 128), lambda j, b: (tb(b), j)), pl.BlockSpec((T, 512), lambda j, b: (tb(b), j)),
                   pl.BlockSpec((T, 512), lambda j, b: (tb(b), j))],
        out_shape=[SDS((S, 1024), f32), SDS((S, 4096), f32), SDS((S, 4096), f32)],
        scratch_shapes=[pltpu.VMEM((2, 8, 512), f32)],
        compiler_params=_params(("parallel", "arbitrary")), name=name)(u, BBr, BBi, CCr, CCi, abr, abi, pwr, pwi)


def _s5_scan_bwd(u, dy, xr, xi, BBr, BBi, CCr, CCi, abr, abi, pwr, pwi, *, reverse, name):
    S = u.shape[0]
    T = min(S, 512)
    NB = S // T
    L = T // S5_SEG
    d = 1 if reverse else 0
    adj_rev = not reverse

    def tb(b):
        return b if reverse else (NB - 1 - b)

    def bnd(b):
        t = tb(b)
        if reverse:
            return jnp.minimum((t + 1) * (T // 8), S // 8 - 1)
        return jnp.maximum(t * (T // 8) - 1, 0)

    def body(u_ref, dy_ref, xr_ref, xi_ref, xbr_ref, xbi_ref, bbr_ref, bbi_ref, ccr_ref, cci_ref, ar_ref, ai_ref,
             pwr_ref, pwi_ref, du_ref, dbbr_ref, dbbi_ref, dccr_ref, dcci_ref, dar_ref, dai_ref, carry, lam):
        b = pl.program_id(1)

        @pl.when(b == 0)
        def _():
            carry[...] = jnp.zeros_like(carry)
            for r in (dbbr_ref, dbbi_ref, dccr_ref, dcci_ref, dar_ref, dai_ref):
                r[...] = jnp.zeros_like(r)

        dyb = dy_ref[...]
        lam[0] = lax.dot_general(dyb, ccr_ref[0, 0], (NT, ((), ())), preferred_element_type=f32)
        lam[1] = -lax.dot_general(dyb, cci_ref[0, 0], (NT, ((), ())), preferred_element_type=f32)
        a6 = (0 if adj_rev else L - 1)
        _scan_core(lam.at[0], lam.at[1], ar_ref[...], -ai_ref[...], pwr_ref, pwi_ref, pwr_ref[a6:a6 + 1, :],
                   pwi_ref[a6:a6 + 1, :], carry, reverse=adj_rev, T=T, conj=True)
        ub = u_ref[...].astype(bf16)
        first = (b == NB - 1)
        lrb = lam[0].astype(bf16)
        lib = lam[1].astype(bf16)
        du_ref[...] = lax.dot_general(lrb, bbr_ref[0, 0], (NT, ((), ())), preferred_element_type=f32) \
            + lax.dot_general(lib, bbi_ref[0, 0], (NT, ((), ())), preferred_element_type=f32)
        dbbr_ref[0, 0] += lax.dot_general(ub, lrb, (TN, ((), ())), preferred_element_type=f32)
        dbbi_ref[0, 0] += lax.dot_general(ub, lib, (TN, ((), ())), preferred_element_type=f32)
        dccr_ref[0, 0] += lax.dot_general(dyb, xr_ref[...].astype(bf16), (TN, ((), ())), preferred_element_type=f32)
        dcci_ref[0, 0] -= lax.dot_general(dyb, xi_ref[...].astype(bf16), (TN, ((), ())), preferred_element_type=f32)
        row8 = lax.broadcasted_iota(jnp.int32, (8, 512), 0)
        if reverse:
            body_x, body_l, edge_l = slice(8, T), slice(0, T - 8), slice(T - 8, T)
            sp_r = jnp.where(row8 == 7, jnp.where(first, 0.0, xbr_ref[0:1, :]), pltpu.roll(xr_ref[0:8, :], 7, axis=0))
            sp_i = jnp.where(row8 == 7, jnp.where(first, 0.0, xbi_ref[0:1, :]), pltpu.roll(xi_ref[0:8, :], 7, axis=0))
        else:
            body_x, body_l, edge_l = slice(0, T - 8), slice(8, T), slice(0, 8)
            sp_r = jnp.where(row8 == 0, jnp.where(first, 0.0, xbr_ref[7:8, :]), pltpu.roll(xr_ref[T - 8:T, :], 1, axis=0))
            sp_i = jnp.where(row8 == 0, jnp.where(first, 0.0, xbi_ref[7:8, :]), pltpu.roll(xi_ref[T - 8:T, :], 1, axis=0))
        xpr, xpi = xr_ref[body_x, :], xi_ref[body_x, :]
        lr, li = lam[0, body_l, :], lam[1, body_l, :]
        er, ei = lam[0, edge_l, :], lam[1, edge_l, :]
        dar_ref[...] += jnp.sum(xpr * lr + xpi * li, axis=0, keepdims=True) + jnp.sum(sp_r * er + sp_i * ei, axis=0, keepdims=True)
        dai_ref[...] += jnp.sum(xpr * li - xpi * lr, axis=0, keepdims=True) + jnp.sum(sp_r * ei - sp_i * er, axis=0, keepdims=True)

    mat = lambda shp: pl.BlockSpec((1, 1) + shp, lambda j, b: (d, j, 0, 0))
    omat = lambda shp: pl.BlockSpec((1, 1) + shp, lambda j, b: (0, j, 0, 0))
    vec = lambda r: pl.BlockSpec((r, 512), lambda j, b: (0, d * S5_NJ + j))
    blk = lambda w: pl.BlockSpec((T, w), lambda j, b: (tb(b), j))
    return pl.pallas_call(
        body, grid=(S5_NJ, NB),
        in_specs=[blk(128), blk(128), blk(512), blk(512),
                  pl.BlockSpec((8, 512), lambda j, b: (bnd(b), j)), pl.BlockSpec((8, 512), lambda j, b: (bnd(b), j)),
                  mat((128, 512)), mat((128, 512)), mat((512, 128)), mat((512, 128)), vec(1), vec(1), vec(L), vec(L)],
        out_specs=[blk(128), omat((128, 512)), omat((128, 512)), omat((128, 512)), omat((128, 512)),
                   pl.BlockSpec((1, 512), lambda j, b: (0, j)), pl.BlockSpec((1, 512), lambda j, b: (0, j))],
        out_shape=[SDS((S, 1024), f32), SDS((1, 8, 128, 512), f32), SDS((1, 8, 128, 512), f32), SDS((1, 8, 128, 512), f32),
                   SDS((1, 8, 128, 512), f32), SDS((1, 4096), f32), SDS((1, 4096), f32)],
        scratch_shapes=[pltpu.VMEM((2, 8, 512), f32), pltpu.VMEM((2, T, 512), f32)],
        compiler_params=_params(("parallel", "arbitrary"), VMEM_BIG), name=name)(
            u, dy, xr, xi, xr, xi, BBr, BBi, CCr, CCi, abr, abi, pwr, pwi)


def _silu_mul(g, u):
    return jax.nn.silu(g) * u


def _mixf(p0, p1, p2, z0, z1, z2):
    return jax.nn.sigmoid(z0) * p0 + jax.nn.sigmoid(z1) * p1 + jax.nn.sigmoid(z2) * p2


def _s5_act(yf, yb, u, dd):
    return jax.nn.gelu(yf + yb + dd * u)


def _glu(a, b):
    return a * jax.nn.sigmoid(b)


def _layer_fwd(x, w, tabs, l):
    S = x.shape[0]
    T = min(S, 512)
    I = S // T
    nm = lambda s: f"L{l}_{s}"
    sv = {'x': x}
    h = _rmsnorm_fwd(x, w['norm1_g'], name=nm("norm1"))
    zr = _mm(h, w['W_ret'], name=nm("in_ret"))
    zm = _mm(h, w['W_mla'], name=nm("in_mla"))
    h_seg = _perm_rows(h, tabs['seg_perm'], name=nm("s5_perm_h"))
    zs = _mm(h_seg, w['W_s5'], name=nm("in_s5"))
    zg = _mm(h, w['W_gate'], out_dtype=bf16, name=nm("in_gate"))
    sv.update(h=h, h_seg=h_seg, zr=zr, zm=zm, zs=zs, zg=zg)

    yf, stf = _ret_dir_fwd(zr, w['lg'], tabs['cos_r'], tabs['sin_r'], reverse=False, name=nm("ret_f"))
    yb, stb = _ret_dir_fwd(zr, w['lg'], tabs['cos_r'], tabs['sin_r'], reverse=True, name=nm("ret_b"))
    hd = lambda j: j
    y_ret = _pw(_gn_gate, [yf, yb, zr, w['ret_gn_g']],
                [_row(T, 256, hd), _row(T, 256, hd), _row(T, 256, lambda j: 8 + j), _par(256, hd)],
                [SDS((S, 1024), bf16)], [_row(T, 256, hd)], (RET_HEADS, I), name=nm("ret_gn"))[0]
    sv.update(yf=yf, yb=yb, stf=stf, stb=stb, y_ret=y_ret)

    cqn, ckvn = _pw(lambda a, b, g1, g2: (_rms(a, g1), _rms(b, g2)), [zm, zm, w['mla_q_norm_g'], w['mla_kv_norm_g']],
                    [_row(T, 384), _row(T, 256, lambda j: 2), _par(384), _par(256)],
                    [SDS((S, 384), bf16), SDS((S, 256), bf16)], [_row(T, 384), _row(T, 256)], (1, I), name=nm("mla_norm"))
    q = _mm(cqn, w['W_uq'], out_dtype=bf16, name=nm("mla_uq"))
    kv = _mm(ckvn, w['W_ukv'], out_dtype=bf16, name=nm("mla_ukv"))
    Q, K = _mla_qk_prep(q, kv, zm, tabs['cos_m'], tabs['sin_m'], name=nm("mla_qkprep"))
    O, Lse = _flash_fwd(Q, K, kv, name=nm("mla_attn"))
    sv.update(cqn=cqn, ckvn=ckvn, kv=kv, Q=Q, K=K, O=O, Lse=Lse)

    s5 = w['s5']
    ysf, xrf, xif = _s5_scan_fwd(zs, s5['BBr'], s5['BBi'], s5['CCr'], s5['CCi'], s5['abr'], s5['abi'], s5['pwr_f'], s5['pwi_f'],
                                 reverse=False, name=nm("s5_f"))
    ysb, xrb, xib = _s5_scan_fwd(zs, s5['BBr'], s5['BBi'], s5['CCr'], s5['CCi'], s5['abr'], s5['abi'], s5['pwr_f'], s5['pwi_f'],
                                 reverse=True, name=nm("s5_b"))
    gact = _pw(_s5_act, [ysf, ysb, zs, w['s5_d']], [_row(T, D), _row(T, D), _row(T, D), _par(D)],
               [SDS((S, D), bf16)], [_row(T, D)], (1, I), name=nm("s5_act"))[0]
    gg = _mm(gact, w['W_glu'], out_dtype=bf16, name=nm("s5_glu_mm"))
    y_s5 = _pw(_glu, [gg, gg], [_row(T, D), _row(T, D, lambda j: 1)], [SDS((S, D), bf16)], [_row(T, D)], (1, I),
               name=nm("s5_glu"))[0]
    y_s5 = _perm_rows(y_s5, tabs['seg_unperm'], name=nm("s5_unperm_y"))
    sv.update(ysf=ysf, ysb=ysb, xrf=xrf, xif=xif, xrb=xrb, xib=xib, gact=gact, gg=gg, y_s5=y_s5)

    ys = [y_ret, O, y_s5]
    pr = [_mm(ys[i], w['W_br'][i], out_dtype=bf16, name=nm(f"branch{i}")) for i in range(3)]
    mix = _pw(_mixf, pr + [zg, zg, zg],
              [_row(T, D)] * 3 + [_row(T, D), _row(T, D, lambda j: 1), _row(T, D, lambda j: 2)],
              [SDS((S, D), bf16)], [_row(T, D)], (1, I), name=nm("mix"))[0]
    x1 = _mm(mix, w['W_out'], res=x, name=nm("out_proj"))
    h2 = _rmsnorm_fwd(x1, w['norm2_g'], name=nm("norm2"))
    fgu = _mm(h2, w['W_gu'], out_dtype=bf16, name=nm("ffn_gu"))
    act = _pw(_silu_mul, [fgu, fgu], [_row(T, 1408, lambda j: j), _row(T, 1408, lambda j: 2 + j)],
              [SDS((S, FFN_H), bf16)], [_row(T, 1408, lambda j: j)], (2, I), name=nm("ffn_act"))[0]
    x2 = _mm(act, w['W_down'], res=x1, name=nm("ffn_down"))
    sv.update(pr=pr, mix=mix, x1=x1, h2=h2, fgu=fgu, act=act)
    return x2, sv


def _vjp_fn(fn, n_primal, cast=None):
    def g(*args):
        _, vjp = jax.vjp(fn, *args[:n_primal])
        return vjp(args[n_primal].astype(f32))
    return g


def _layer_bwd(dx2, w, tabs, sv, l):
    S = dx2.shape[0]
    T = min(S, 512)
    I = S // T
    nm = lambda s: f"L{l}_b_{s}"
    g = {}
    hd = lambda j: j

    dact = _mm(dx2, w['W_down'], tb=True, out_dtype=bf16, name=nm("ffn_down_dx"))
    g['W_down'] = _mmT(sv['act'], dx2, name=nm("ffn_down_dw"))
    dfg, dfu = _pw(_vjp_fn(_silu_mul, 2), [sv['fgu'], sv['fgu'], dact],
                   [_row(T, 1408, lambda j: j), _row(T, 1408, lambda j: 2 + j), _row(T, 1408, lambda j: j)],
                   [SDS((S, FFN_H), bf16), SDS((S, FFN_H), bf16)], [_row(T, 1408, lambda j: j)] * 2, (2, I), name=nm("ffn_act"))
    dfgu = jnp.concatenate([dfg, dfu], axis=1)
    g['W_gu'] = _mmT(sv['h2'], dfgu, name=nm("ffn_gu_dw"))
    dh2 = _mm(dfgu, w['W_gu'], tb=True, name=nm("ffn_gu_dx"))
    dx1, g['norm2_g'] = _rmsnorm_bwd(sv['x1'], w['norm2_g'], dh2, dx2, name=nm("norm2"))

    dmix = _mm(dx1, w['W_out'], tb=True, out_dtype=bf16, name=nm("out_dx"))
    g['W_out'] = _mmT(sv['mix'], dx1, name=nm("out_dw"))
    zg = sv['zg']
    outs = _pw(_vjp_fn(_mixf, 6), sv['pr'] + [zg, zg, zg, dmix],
               [_row(T, D)] * 3 + [_row(T, D), _row(T, D, lambda j: 1), _row(T, D, lambda j: 2), _row(T, D)],
               [SDS((S, D), bf16)] * 6, [_row(T, D)] * 6, (1, I), name=nm("mix"))
    dpr, dzg = outs[:3], jnp.concatenate(outs[3:], axis=1)
    ys = [sv['y_ret'], sv['O'], sv['y_s5']]
    g['W_br'] = [_mmT(ys[i], dpr[i], name=nm(f"branch{i}_dw")) for i in range(3)]
    dpr_seg = _perm_rows(dpr[2], tabs['seg_perm'], name=nm("s5_perm_dy"))
    dys = [_mm(dpr[i] if i < 2 else dpr_seg, w['W_br'][i], tb=True, out_dtype=bf16,
               name=nm(f"branch{i}_dx")) for i in range(3)]

    gg = sv['gg']
    dga, dgb = _pw(_vjp_fn(_glu, 2), [gg, gg, dys[2]], [_row(T, D), _row(T, D, lambda j: 1), _row(T, D)],
                   [SDS((S, D), bf16)] * 2, [_row(T, D)] * 2, (1, I), name=nm("s5_glu"))
    dgg = jnp.concatenate([dga, dgb], axis=1)
    g['W_glu'] = _mmT(sv['gact'], dgg, name=nm("s5_glu_dw"))
    dgact = _mm(dgg, w['W_glu'], tb=True, out_dtype=bf16, name=nm("s5_glu_dx"))

    def act_bwd(yf, yb, u, dd, ct):
        _, vjp = jax.vjp(_s5_act, yf, yb, u, dd)
        dyf, _, du, ddd = vjp(ct)
        return dyf, du, ddd

    dys5, du_direct, g['s5_d'] = _pw(act_bwd, [sv['ysf'], sv['ysb'], sv['zs'], w['s5_d'], dgact],
                                     [_row(T, D)] * 3 + [_par(D), _row(T, D)],
                                     [SDS((S, D), bf16), SDS((S, D), f32), SDS((1, D), f32)],
                                     [_row(T, D), _row(T, D), _par(D)], (1, I), n_acc=1, name=nm("s5_act"))
    s5 = w['s5']
    rf = _s5_scan_bwd(sv['zs'], dys5, sv['xrf'], sv['xif'], s5['BBr'], s5['BBi'], s5['CCr'], s5['CCi'], s5['abr'], s5['abi'],
                      s5['pwr_a'], s5['pwi_a'], reverse=False, name=nm("s5_f"))
    rb = _s5_scan_bwd(sv['zs'], dys5, sv['xrb'], sv['xib'], s5['BBr'], s5['BBi'], s5['CCr'], s5['CCi'], s5['abr'], s5['abi'],
                      s5['pwr_a'], s5['pwi_a'], reverse=True, name=nm("s5_b"))
    g['s5'] = (rf[1:], rb[1:])
    dzs_seg = _pw(lambda a, b, c: a + b + c, [du_direct, rf[0], rb[0]], [_row(T, D)] * 3, [SDS((S, D), bf16)], [_row(T, D)],
                  (1, I), name=nm("s5_du"))[0]
    dzs = _perm_rows(dzs_seg, tabs['seg_unperm'], name=nm("s5_unperm_dz"))

    delta = _attn_delta(dys[1], sv['O'], name=nm("mla_delta"))
    dQ, dK, dV = _flash_bwd(sv['Q'], sv['K'], sv['kv'], delta, sv['Lse'], dys[1], name=nm("mla_attn"))
    dq_lin, dkv, dkr = _mla_bwd_prep(dQ, dK, dV, tabs['cos_m'], tabs['sin_m'], name=nm("mla_prep"))
    g['W_uq'] = _mmT(sv['cqn'], dq_lin, name=nm("mla_uq_dw"))
    dcqn = _mm(dq_lin, w['W_uq'], tb=True, name=nm("mla_uq_dx"))
    g['W_ukv'] = _mmT(sv['ckvn'], dkv, name=nm("mla_ukv_dw"))
    dckvn = _mm(dkv, w['W_ukv'], tb=True, name=nm("mla_ukv_dx"))
    dzm, g['mla_q_norm_g'], g['mla_kv_norm_g'] = _mla_norm_bwd(sv['zm'], w['mla_q_norm_g'], w['mla_kv_norm_g'], dcqn, dckvn, dkr,
                                                               name=nm("mla_norm"))

    zr = sv['zr']

    def gn_bwd(yf, yb, gt, gn, ct):
        _, vjp = jax.vjp(_gn_gate, yf, yb, gt, gn)
        dyf, _, dgt, dgn = vjp(ct)
        return dyf, dgt, dgn

    dyr, dgate, g['ret_gn_g'] = _pw(gn_bwd, [sv['yf'], sv['yb'], zr, w['ret_gn_g'], dys[0]],
                                    [_row(T, 256, hd), _row(T, 256, hd), _row(T, 256, lambda j: 8 + j), _par(256, hd),
                                     _row(T, 256, hd)],
                                    [SDS((S, 1024), bf16), SDS((S, 1024), bf16), SDS((1, 1024), f32)],
                                    [_row(T, 256, hd), _row(T, 256, hd), _par(256, hd)], (RET_HEADS, I), n_acc=1, name=nm("ret_gn"))
    qf, kf, vf, lgf = _ret_dir_bwd(zr, w['lg'], tabs['cos_r'], tabs['sin_r'], dyr, sv['stf'], reverse=False, name=nm("ret_f"))
    qb, kb, vb, lgb = _ret_dir_bwd(zr, w['lg'], tabs['cos_r'], tabs['sin_r'], dyr, sv['stb'], reverse=True, name=nm("ret_b"))
    g['lg'] = jnp.stack([lgf[:, 0, 0], lgb[:, 0, 0]])
    dzr = _pw(lambda a, b, c, d, e, f, gt: jnp.concatenate([a + b, c + d, e + f, gt], axis=1),
              [qf, qb, kf, kb, vf, vb, dgate], [_row(256, 512)] * 4 + [_row(256, D)] * 3,
              [SDS((S, 3072), bf16)], [_row(256, 3072)], (1, S // 256), name=nm("ret_dz"))[0]

    h = sv['h']
    g['W_ret'] = _mmT(h, dzr, name=nm("in_ret_dw"))
    g['W_mla'] = _mmT(h, dzm, name=nm("in_mla_dw"))
    g['W_s5'] = _mmT(sv['h_seg'], dzs_seg, name=nm("in_s5_dw"))
    g['W_gate'] = _mmT(h, dzg, name=nm("in_gate_dw"))
    dh = _mm(dzr, w['W_ret'], tb=True, name=nm("in_ret_dx"))
    dh = _mm(dzm, w['W_mla'], tb=True, res=dh, name=nm("in_mla_dx"))
    dh = _mm(dzs, w['W_s5'], tb=True, res=dh, name=nm("in_s5_dx"))
    dh = _mm(dzg, w['W_gate'], tb=True, res=dh, name=nm("in_gate_dx"))
    dx, g['norm1_g'] = _rmsnorm_bwd(sv['x'], w['norm1_g'], dh, dx1, name=nm("norm1"))
    return dx, g


def _loss_head(x, tgt, gain, *, name):
    S, W = x.shape
    T = min(S, 512)

    def loss_fn(xv, gv, tv):
        return 0.5 * jnp.sum(jnp.mean(jnp.square(_rms(xv, gv) - tv), axis=-1, keepdims=True), axis=0, keepdims=True)

    def fn(xv, gv, tv):
        lv, vjp = jax.vjp(lambda a, b: loss_fn(a, b, tv), xv, gv)
        dx, dg = vjp(jnp.ones((1, 1), f32))
        return dx, jnp.broadcast_to(lv, (1, 128)), dg

    return _pw(fn, [x, gain, tgt], [_row(T, W), _par(W), _row(T, W)],
               [SDS((S, W), f32), SDS((1, 128), f32), SDS((1, W), f32)], [_row(T, W), _par(128), _par(W)],
               (1, S // T), n_acc=2, name=name)


def _rope_tabs(S):
    def tab(dim):
        inv = 1.0 / (ROPE_THETA ** (jnp.arange(0, dim, 2, dtype=f32) / dim))
        ang = jnp.arange(S, dtype=f32)[:, None] * inv[None, :]
        return jnp.cos(ang), jnp.sin(ang)

    cr, sr = tab(RET_DK)
    cm, sm = tab(MLA_ROPE)
    z = jnp.zeros((S, 64), f32)
    return {'cos_r': jnp.concatenate([cr, cr], axis=1), 'sin_r': jnp.concatenate([-sr, sr], axis=1),
            'cos_m': jnp.concatenate([cm, cm, z], axis=1), 'sin_m': jnp.concatenate([-sm, sm, z], axis=1),
            'seg_perm': _seg_perm(512, False), 'seg_unperm': _seg_perm(512, True)}


def _bd_B(bb):
    b5 = bb.reshape(16, 2, 8, 8, 64)
    return jnp.einsum('cdjgp,gh->djgchp', b5, jnp.eye(8, dtype=bb.dtype)).reshape(2, 8, 128, 512)


def _bd_B_t(dBB):
    return jnp.einsum('djgcgp->cdjgp', dBB.reshape(2, 8, 8, 16, 8, 64)).reshape(16, 8192)


def _bd_C(c):
    c5 = c.reshape(2, 8, 8, 16, 64)
    return jnp.einsum('djgcp,gh->djgphc', c5, jnp.eye(8, dtype=c.dtype)).reshape(2, 8, 512, 128)


def _s5_rows(p, l):
    a_re = p['s5_a_re'][l].reshape(1, 8192)
    a_im = p['s5_a_im'][l].reshape(1, 8192)
    ldt = jnp.broadcast_to(p['s5_log_dt'][l][:, :, None], (2, S5_G, S5_P)).reshape(1, 8192)
    b_re = p['s5_b_re'][l].transpose(3, 0, 1, 2).reshape(16, 8192)
    b_im = p['s5_b_im'][l].transpose(3, 0, 1, 2).reshape(16, 8192)
    return a_re, a_im, ldt, b_re, b_im


def _layer_weights(big, p, l):
    w_in = big['w_in'][l]
    z = lambda n: jnp.zeros((D, n), w_in.dtype)
    w = {
        'W_ret': w_in[:, 0:3072],
        'W_mla': jnp.concatenate([w_in[:, 3072:3456], z(128), w_in[:, 3456:3712], w_in[:, 3712:3776], z(64)], axis=1),
        'W_s5': w_in[:, 3776:4800],
        'W_gate': w_in[:, 4800:7872],
        'W_uq': jnp.pad(big['mla_w_uq'][l].reshape(MLA_Q_LORA, MLA_HEADS, 192), ((0, 0), (0, 0), (0, 64))).reshape(MLA_Q_LORA, 2048),
        'W_ukv': big['mla_w_ukv'][l],
        'W_glu': big['s5_w_glu'][l],
        'W_br': [big['w_branch'][l, i] for i in range(3)],
        'W_out': big['w_out'][l],
        'W_gu': big['ffn_w_gu'][l],
        'W_down': big['ffn_w_down'][l],
    }
    for n in ('norm1_g', 'ret_gn_g', 'mla_q_norm_g', 'mla_kv_norm_g', 's5_d', 'norm2_g'):
        w[n] = p[n][l][None, :]
    w['lg'] = jax.nn.log_sigmoid(p['ret_decay'][l])
    rows = _s5_rows(p, l)
    abr, abi, bbr, bbi, pwr, pwi = _s5_param_fwd(*rows, name=f"L{l}_s5_param")
    flip = lambda t, first: jnp.concatenate([t[::-1, :4096], t[:, 4096:]] if first else [t[:, :4096], t[::-1, 4096:]], axis=1)
    w['s5'] = {'abr': abr, 'abi': abi, 'BBr': _bd_B(bbr).astype(bf16), 'BBi': _bd_B(bbi).astype(bf16),
               'CCr': _bd_C(p['s5_c_re'][l]).astype(bf16), 'CCi': _bd_C(p['s5_c_im'][l]).astype(bf16),
               'pwr_f': flip(pwr, False), 'pwi_f': flip(pwi, False), 'pwr_a': flip(pwr, True), 'pwi_a': flip(pwi, True),
               'rows': rows}
    return w


def _layer_grads(g, w, p, l):
    out = {}
    m = g['W_mla']
    out['w_in'] = jnp.concatenate([g['W_ret'], m[:, 0:384], m[:, 512:768], m[:, 768:832], g['W_s5'], g['W_gate']], axis=1)
    out['mla_w_uq'] = g['W_uq'].reshape(MLA_Q_LORA, MLA_HEADS, 256)[:, :, :192].reshape(MLA_Q_LORA, 1536)
    out['mla_w_ukv'] = g['W_ukv']
    out['s5_w_glu'] = g['W_glu']
    out['w_branch'] = jnp.stack(g['W_br'])
    out['w_out'] = g['W_out']
    out['ffn_w_gu'] = g['W_gu']
    out['ffn_w_down'] = g['W_down']
    for n in ('norm1_g', 'ret_gn_g', 'mla_q_norm_g', 'mla_kv_norm_g', 's5_d', 'norm2_g'):
        out[n] = g[n][0]
    out['ret_decay'] = g['lg'] * jax.nn.sigmoid(-p['ret_decay'][l])
    (fB_r, fB_i, fC_r, fC_i, fa_r, fa_i), (bB_r, bB_i, bC_r, bC_i, ba_r, ba_i) = g['s5']
    cat = lambda a, b: jnp.concatenate([a, b], axis=0)
    d_bbr = _bd_B_t(cat(fB_r, bB_r))
    d_bbi = _bd_B_t(cat(fB_i, bB_i))
    to_c = lambda t: _bd_B_t(t).reshape(16, 2, S5_G, S5_P).transpose(1, 2, 0, 3)
    out['s5_c_re'] = to_c(cat(fC_r, bC_r))
    out['s5_c_im'] = to_c(cat(fC_i, bC_i))
    d_abr = jnp.concatenate([fa_r, ba_r], axis=1)
    d_abi = jnp.concatenate([fa_i, ba_i], axis=1)
    da_re, da_im, dldt, db_re, db_im = _s5_param_bwd(*w['s5']['rows'], d_abr, d_abi, d_bbr, d_bbi, name=f"L{l}_b_s5_param")
    out['s5_a_re'] = da_re.reshape(2, S5_G, S5_P)
    out['s5_a_im'] = da_im.reshape(2, S5_G, S5_P)
    out['s5_log_dt'] = dldt.reshape(2, S5_G, S5_P).sum(axis=-1)
    out['s5_b_re'] = db_re.reshape(16, 2, S5_G, S5_P).transpose(1, 2, 3, 0)
    out['s5_b_im'] = db_im.reshape(16, 2, S5_G, S5_P).transpose(1, 2, 3, 0)
    return out


def _local_step(x, tgt, big, p):
    S = x.shape[0]
    assert S % 512 == 0
    tabs = _rope_tabs(S)
    ws, svs = [], []
    h = x
    for l in range(DEPTH):
        w = _layer_weights(big, p, l)
        h, sv = _layer_fwd(h, w, tabs, l)
        ws.append(w)
        svs.append(sv)
    dx, lossv, dfinal = _loss_head(h, tgt, p['final_g'][None, :], name="loss_head")
    per_layer = [None] * DEPTH
    for l in reversed(range(DEPTH)):
        dx, g = _layer_bwd(dx, ws[l], tabs, svs[l], l)
        per_layer[l] = _layer_grads(g, ws[l], p, l)
    return lossv[0, 0], dx, per_layer, dfinal[0]


_ANY = pl.BlockSpec(memory_space=pl.ANY)


def _place():
    x, y, c = lax.axis_index("x"), lax.axis_index("y"), lax.axis_index("c")
    return x, y, c, [(1 - x, y), (x, 1 - y), (1 - x, 1 - y)]


def _allgather4(arrs, *, name):
    n = len(arrs)

    def body(*refs):
        ins, outs = refs[:n], refs[n:2 * n]
        send, recv, loc = refs[2 * n:]
        x, y, c, chips = _place()
        me = 2 * x + y

        def remote(a, k, slot):
            px, py = chips[k]
            return pltpu.make_async_remote_copy(src_ref=ins[a], dst_ref=outs[a].at[slot], send_sem=send.at[a, k],
                                                recv_sem=recv.at[a, k], device_id=(px, py, c), device_id_type=MESH)

        mine = [pltpu.make_async_copy(ins[a], outs[a].at[me], loc.at[a]) for a in range(n)]
        for cp in mine:
            cp.start()
        sends = [remote(a, k, me) for a in range(n) for k in range(3)]
        for cp in sends:
            cp.start()
        for a in range(n):
            for k, (px, py) in enumerate(chips):
                remote(a, k, 2 * px + py).wait_recv()
        for cp in sends:
            cp.wait_send()
        for cp in mine:
            cp.wait()

    return pl.pallas_call(
        body, in_specs=[_ANY] * n, out_specs=[_ANY] * n, out_shape=[SDS((4,) + a.shape, a.dtype) for a in arrs],
        scratch_shapes=[pltpu.SemaphoreType.DMA((n, 3)), pltpu.SemaphoreType.DMA((n, 3)), pltpu.SemaphoreType.DMA((n,))],
        name=name)(*arrs)


def _rs_exchange(parts, *, name):
    n = len(parts)

    def body(*refs):
        ins, gots = refs[:n], refs[n:2 * n]
        send, recv = refs[2 * n:]
        x, y, c, chips = _place()

        def remote(a, k):
            px, py = chips[k]
            return pltpu.make_async_remote_copy(src_ref=ins[a].at[2 * px + py], dst_ref=gots[a].at[k], send_sem=send.at[a, k],
                                                recv_sem=recv.at[a, k], device_id=(px, py, c), device_id_type=MESH)

        sends = [remote(a, k) for a in range(n) for k in range(3)]
        for cp in sends:
            cp.start()
        for cp in sends:
            cp.wait_recv()
        for cp in sends:
            cp.wait_send()

    return pl.pallas_call(
        body, in_specs=[_ANY] * n, out_specs=[_ANY] * n, out_shape=[SDS((3,) + a.shape[1:], a.dtype) for a in parts],
        scratch_shapes=[pltpu.SemaphoreType.DMA((n, 3)), pltpu.SemaphoreType.DMA((n, 3))], name=name)(*parts)


def _gather_split(arrs, *, name):
    n = len(arrs)

    def body(*refs):
        ins, outs = refs[:n], refs[n:2 * n]
        s_ici, r_ici, s_sib, r_sib, loc = refs[2 * n:]
        x, y, c, chips = _place()
        me = 2 * x + y
        ids = [2 * px + py for px, py in chips] + [me]

        def over_ici(a, k, slot):
            px, py = chips[k]
            return pltpu.make_async_remote_copy(src_ref=ins[a].at[c], dst_ref=outs[a].at[slot, c], send_sem=s_ici.at[a, k],
                                                recv_sem=r_ici.at[a, k], device_id=(px, py, c), device_id_type=MESH)

        def to_sibling(a, k, half, src=None):
            blk = outs[a].at[ids[k], half]
            return pltpu.make_async_remote_copy(src_ref=blk if src is None else src, dst_ref=blk, send_sem=s_sib.at[a, k],
                                                recv_sem=r_sib.at[a, k], device_id=(x, y, 1 - c), device_id_type=MESH)

        sends = [over_ici(a, k, me) for a in range(n) for k in range(3)]
        sends += [to_sibling(a, 3, c, src=ins[a].at[c]) for a in range(n)]
        for cp in sends:
            cp.start()
        mine = [pltpu.make_async_copy(ins[a].at[c], outs[a].at[me, c], loc.at[a]) for a in range(n)]
        for cp in mine:
            cp.start()
        for a in range(n):
            for k in range(3):
                over_ici(a, k, ids[k]).wait_recv()
                fwd = to_sibling(a, k, c)
                fwd.start()
                sends.append(fwd)
        for a in range(n):
            for k in range(4):
                to_sibling(a, k, 1 - c).wait_recv()
        for cp in sends:
            cp.wait_send()
        for cp in mine:
            cp.wait()

    dma = pltpu.SemaphoreType.DMA
    return pl.pallas_call(
        body, in_specs=[_ANY] * n, out_specs=[_ANY] * n, out_shape=[SDS((4,) + a.shape, a.dtype) for a in arrs],
        scratch_shapes=[dma((n, 3)), dma((n, 3)), dma((n, 4)), dma((n, 4)), dma((n,))], name=name)(*arrs)


def _swap_halves(parts, *, name):
    n = len(parts)

    def body(*refs):
        ins, gots = refs[:n], refs[n:2 * n]
        send, recv = refs[2 * n:]
        x, y, c, _ = _place()
        cps = [pltpu.make_async_remote_copy(src_ref=ins[a].at[q, 1 - c], dst_ref=gots[a].at[q], send_sem=send.at[a, q],
                                            recv_sem=recv.at[a, q], device_id=(x, y, 1 - c), device_id_type=MESH)
               for a in range(n) for q in range(4)]
        for cp in cps:
            cp.start()
        for cp in cps:
            cp.wait_recv()
        for cp in cps:
            cp.wait_send()

    dma = pltpu.SemaphoreType.DMA
    return pl.pallas_call(
        body, in_specs=[_ANY] * n, out_specs=[_ANY] * n, out_shape=[SDS((4,) + a.shape[2:], a.dtype) for a in parts],
        scratch_shapes=[dma((n, 4)), dma((n, 4))], name=name)(*parts)


def _sibling_copy(arrs, *, name):
    n = len(arrs)

    def body(*refs):
        ins, outs = refs[:n], refs[n:2 * n]
        send, recv = refs[2 * n:]
        x, y, c, _ = _place()
        cps = [pltpu.make_async_remote_copy(src_ref=ins[a], dst_ref=outs[a], send_sem=send.at[a], recv_sem=recv.at[a],
                                            device_id=(x, y, 1 - c), device_id_type=MESH) for a in range(n)]
        for cp in cps:
            cp.start()
        for cp in cps:
            cp.wait_recv()
        for cp in cps:
            cp.wait_send()

    dma = pltpu.SemaphoreType.DMA
    return pl.pallas_call(
        body, in_specs=[_ANY] * n, out_specs=[_ANY] * n, out_shape=[SDS(a.shape, a.dtype) for a in arrs],
        scratch_shapes=[dma((n,)), dma((n,))], name=name)(*arrs)


def _row_tile(R):
    return R if R <= 256 else next(t for t in (256, 128, 64, 32, 16) if R % t == 0)


def _add2(a, b, *, name):
    R, W = a.shape
    tr = _row_tile(R)
    return _pw(lambda p, q: p.astype(f32) + q.astype(f32), [a, b], [_row(tr, W)] * 2, [SDS((R, W), a.dtype)], [_row(tr, W)],
               (1, R // tr), name=name)[0]


def _sum4(own, got, *, name):
    R, W = own.shape
    tr = _row_tile(R)
    g3 = lambda k: pl.BlockSpec((None, tr, W), lambda j, i: (k, i, 0))
    up = lambda t: t.astype(f32)
    return _pw(lambda a, b, c, d: ((up(a) + up(b)) + up(c)) + up(d), [own, got, got, got], [_row(tr, W), g3(0), g3(1), g3(2)],
               [SDS((R, W), f32)], [_row(tr, W)], (1, R // tr), name=name)[0]


def _adamw(g, w, m, v, *, name):
    R, W = w.shape
    tr = _row_tile(R)

    def fn(gv, wv, mv, vv):
        m2 = ADAM_B1 * mv + (1.0 - ADAM_B1) * gv
        v2 = ADAM_B2 * vv + (1.0 - ADAM_B2) * jnp.square(gv)
        m_hat = m2 / (1.0 - ADAM_B1 ** ADAM_STEP)
        v_hat = v2 / (1.0 - ADAM_B2 ** ADAM_STEP)
        return -ADAM_LR * (m_hat / (jnp.sqrt(v_hat) + ADAM_EPS) + ADAM_WD * wv), m2, v2

    return _pw(fn, [g, w, m, v], [_row(tr, W)] * 4, [SDS((R, W), f32)] * 3, [_row(tr, W)] * 3, (1, R // tr), name=name)


def _to_parts(g, axis):
    shp = g.shape
    g = g.reshape(shp[:axis] + (4, shp[axis] // 4) + shp[axis + 1:])
    return jnp.moveaxis(g, axis, 0)


def _from_parts(pt, axis):
    g = jnp.moveaxis(pt, 0, axis)
    shp = g.shape
    return g.reshape(shp[:axis] + (4 * shp[axis + 1],) + shp[axis + 2:])


def kernel(x, norm1_g, w_in, ret_decay, ret_gn_g, mla_q_norm_g, mla_w_uq, mla_kv_norm_g, mla_w_ukv, s5_a_re, s5_a_im, s5_log_dt, s5_b_re, s5_b_im, s5_c_re, s5_c_im, s5_d, s5_w_glu, w_branch, w_out, norm2_g, ffn_w_gu, ffn_w_down, final_g, loss_target, m_norm1_g, m_w_in, m_ret_decay, m_ret_gn_g, m_mla_q_norm_g, m_mla_w_uq, m_mla_kv_norm_g, m_mla_w_ukv, m_s5_a_re, m_s5_a_im, m_s5_log_dt, m_s5_b_re, m_s5_b_im, m_s5_c_re, m_s5_c_im, m_s5_d, m_s5_w_glu, m_w_branch, m_w_out, m_norm2_g, m_ffn_w_gu, m_ffn_w_down, m_final_g, v_norm1_g, v_w_in, v_ret_decay, v_ret_gn_g, v_mla_q_norm_g, v_mla_w_uq, v_mla_kv_norm_g, v_mla_w_ukv, v_s5_a_re, v_s5_a_im, v_s5_log_dt, v_s5_b_re, v_s5_b_im, v_s5_c_re, v_s5_c_im, v_s5_d, v_s5_w_glu, v_w_branch, v_w_out, v_norm2_g, v_ffn_w_gu, v_ffn_w_down, v_final_g):
    wv = dict(zip(W_NAMES, (norm1_g, w_in, ret_decay, ret_gn_g, mla_q_norm_g, mla_w_uq, mla_kv_norm_g, mla_w_ukv, s5_a_re, s5_a_im,
                            s5_log_dt, s5_b_re, s5_b_im, s5_c_re, s5_c_im, s5_d, s5_w_glu, w_branch, w_out, norm2_g, ffn_w_gu,
                            ffn_w_down, final_g)))
    mv = dict(zip(W_NAMES, (m_norm1_g, m_w_in, m_ret_decay, m_ret_gn_g, m_mla_q_norm_g, m_mla_w_uq, m_mla_kv_norm_g, m_mla_w_ukv,
                            m_s5_a_re, m_s5_a_im, m_s5_log_dt, m_s5_b_re, m_s5_b_im, m_s5_c_re, m_s5_c_im, m_s5_d, m_s5_w_glu,
                            m_w_branch, m_w_out, m_norm2_g, m_ffn_w_gu, m_ffn_w_down, m_final_g)))
    vv = dict(zip(W_NAMES, (v_norm1_g, v_w_in, v_ret_decay, v_ret_gn_g, v_mla_q_norm_g, v_mla_w_uq, v_mla_kv_norm_g, v_mla_w_ukv,
                            v_s5_a_re, v_s5_a_im, v_s5_log_dt, v_s5_b_re, v_s5_b_im, v_s5_c_re, v_s5_c_im, v_s5_d, v_s5_w_glu,
                            v_w_branch, v_w_out, v_norm2_g, v_ffn_w_gu, v_ffn_w_down, v_final_g)))
    big_names = list(BIG)

    my_c = lax.axis_index("c")
    my_chip = 2 * lax.axis_index("x") + lax.axis_index("y")
    shards = [wv[n].astype(bf16) for n in big_names]
    gathered = _gather_split(shards, name="gather_weights")
    big = {n: _from_parts(gt, BIG[n]) for n, gt in zip(big_names, gathered)}
    small = {n: wv[n] for n in SMALL}

    loss_local, dx, layer_grads, d_final = _local_step(x[0], loss_target[0], big, small)
    grads = {n: jnp.stack([layer_grads[l][n] for l in range(DEPTH)]) for n in SMALL if n != 'final_g'}
    grads['final_g'] = d_final

    n_rows = {n: -(-math.prod(wv[n].shape) // 1024) * 8 for n in SMALL}
    used = sum(n_rows.values())
    rows_q = -(-(used + 8) // (4 * 128)) * 128

    def as_rows(d, tail=None):
        blocks = [jnp.pad(d[n].reshape(-1), (0, n_rows[n] * 128 - math.prod(wv[n].shape))).reshape(n_rows[n], 128) for n in SMALL]
        blocks.append(jnp.zeros((8, 128), f32) if tail is None else tail)
        blocks.append(jnp.zeros((4 * rows_q - used - 8, 128), f32))
        return jnp.concatenate(blocks, axis=0)

    loss_rows = jnp.full((8, 128), loss_local, f32)
    parts = [jnp.stack([_to_parts(layer_grads[l][n].astype(bf16), BIG[n] - 1) for l in range(DEPTH)], axis=1) for n in big_names]
    parts.append(as_rows(grads, loss_rows).reshape(4, 2, rows_q // 2, 128))
    n_arr = len(parts)
    two_d = lambda a: a.reshape(-1, a.shape[-1])
    theirs = _swap_halves(parts, name="grad_swap_halves")
    mine = [jnp.where(my_c == 0, p[:, 0], p[:, 1]) for p in parts]
    chip_sums = [_add2(two_d(mine[a]), two_d(theirs[a]), name=f"grad_add2_{a}").reshape(theirs[a].shape) for a in range(n_arr)]
    got = _rs_exchange(chip_sums, name="grad_exchange")

    def pick_chip(s):
        r = s[0]
        for q in range(1, 4):
            r = jnp.where(my_chip == q, s[q], r)
        return r

    own = [pick_chip(s) for s in chip_sums]
    sums = [_sum4(two_d(own[a]), got[a].reshape(3, -1, got[a].shape[-1]), name=f"grad_sum4_{a}") for a in range(n_arr)]
    other = _sibling_copy(sums, name="grad_sibling")
    full = [jnp.stack([jnp.where(my_c == 0, sums[a], other[a]), jnp.where(my_c == 0, other[a], sums[a])]) for a in range(n_arr)]

    out_g, out_d, out_m, out_v = {}, {}, {}, {}
    for a, n in enumerate(big_names):
        shp = wv[n].shape
        res = _adamw(two_d(full[a]), two_d(wv[n]), two_d(mv[n]), two_d(vv[n]), name=f"adamw_{n}")
        out_g[n] = full[a].reshape(shp)
        out_d[n], out_m[n], out_v[n] = [r.reshape(shp) for r in res]
    g_small = _allgather4([full[-1].reshape(rows_q, 128)], name="gather_small_grads")[0].reshape(4 * rows_q, 128)
    loss = g_small[used, 0]
    off = 0
    for n in SMALL:
        shp = wv[n].shape
        k = math.prod(shp)
        flat2 = (k // 128, 128) if k % 128 == 0 else (1, k)
        g_n = g_small[off:off + n_rows[n]].reshape(-1)[:k].reshape(flat2)
        res = _adamw(g_n, wv[n].reshape(flat2), mv[n].reshape(flat2), vv[n].reshape(flat2), name=f"adamw_{n}")
        out_g[n] = g_n.reshape(shp)
        out_d[n], out_m[n], out_v[n] = [r.reshape(shp) for r in res]
        off += n_rows[n]
    return (loss, dx[None], *[out_g[n] for n in W_NAMES], *[out_d[n] for n in W_NAMES], *[out_m[n] for n in W_NAMES],
            *[out_v[n] for n in W_NAMES])
```

```python
import functools
import math

import jax
import jax.numpy as jnp
from jax import lax
from jax.experimental import pallas as pl
from jax.experimental.pallas import tpu as pltpu

f32 = jnp.float32
bf16 = jnp.bfloat16
SDS = jax.ShapeDtypeStruct
MESH = pl.DeviceIdType.MESH

D = 1024
DEPTH = 2
RMS_EPS = 1e-6
GN_EPS = 1e-5
ROPE_THETA = 10000.0
RET_HEADS = 4
RET_DK = 128
RET_DV = 256
RET_CHUNK = 128
MLA_HEADS = 8
MLA_Q_LORA = 384
MLA_KV_LORA = 256
MLA_NOPE = 128
MLA_ROPE = 64
MLA_V = 128
MLA_QW = 256
S5_G = 64
S5_P = 64
S5_C = 16
S5_NJ = 8
S5_SEG = 8
S5_T = 1024
FFN_H = 2816
ADAM_LR = 0.001
ADAM_B1 = 0.9
ADAM_B2 = 0.999
ADAM_EPS = 1e-08
ADAM_WD = 0.01
ADAM_STEP = 10
VMEM_BIG = 56 * 1024 * 1024

W_NAMES = ['norm1_g', 'w_in', 'ret_decay', 'ret_gn_g', 'mla_q_norm_g', 'mla_w_uq', 'mla_kv_norm_g', 'mla_w_ukv',
           's5_a_re', 's5_a_im', 's5_log_dt', 's5_b_re', 's5_b_im', 's5_c_re', 's5_c_im', 's5_d', 's5_w_glu',
           'w_branch', 'w_out', 'norm2_g', 'ffn_w_gu', 'ffn_w_down', 'final_g']
BIG = {'w_in': 2, 'mla_w_uq': 2, 'mla_w_ukv': 2, 's5_w_glu': 2, 'w_branch': 2, 'w_out': 1, 'ffn_w_gu': 2, 'ffn_w_down': 1}
SMALL = [n for n in W_NAMES if n not in BIG]


TILE_BYTES = 6 * 1024 * 1024


def _pick(n, cands=(512, 384, 256, 128), cap=None):
    if n <= 1024 and (cap is None or n <= cap):
        return n
    for c in cands:
        if n % c == 0 and (cap is None or c <= cap):
            return c
    raise ValueError(n)


WIDE = (1408, 1024, 768, 512, 384, 256, 128)


def _params(sem, vmem=None):
    return pltpu.CompilerParams(dimension_semantics=sem, vmem_limit_bytes=vmem)


def _mm(a, b, *, tb=False, res=None, out_dtype=f32, name):
    M, K = a.shape
    N = b.shape[0] if tb else b.shape[1]
    tk = K if K <= 3072 else _pick(K, (1408, 1024, 512))
    nk = K // tk
    tn = _pick(N, WIDE, cap=TILE_BYTES // (tk * b.dtype.itemsize))
    tm = _pick(M)
    if M % 1024 == 0 and 1024 * tk * a.dtype.itemsize <= 4 * 1024 * 1024 and 1024 * tn * 4 <= TILE_BYTES:
        tm = 1024
    assert M % tm == 0 and N % tn == 0 and K % tk == 0

    def body(*refs):
        if res is None:
            a_ref, b_ref, o_ref, acc = refs
        else:
            a_ref, b_ref, r_ref, o_ref, acc = refs
        k = pl.program_id(2)
        dn = (((1,), (1 if tb else 0,)), ((), ()))
        part = lax.dot_general(a_ref[...].astype(bf16), b_ref[...].astype(bf16), dn, preferred_element_type=f32)

        @pl.when(k == 0)
        def _():
            acc[...] = part

        @pl.when(k > 0)
        def _():
            acc[...] += part

        @pl.when(k == nk - 1)
        def _():
            v = acc[...]
            if res is not None:
                v = v + r_ref[...]
            o_ref[...] = v.astype(out_dtype)

    in_specs = [pl.BlockSpec((tm, tk), lambda i, j, k: (i, k)),
                pl.BlockSpec((tn, tk), lambda i, j, k: (j, k)) if tb else pl.BlockSpec((tk, tn), lambda i, j, k: (k, j))]
    args = [a, b]
    if res is not None:
        in_specs.append(pl.BlockSpec((tm, tn), lambda i, j, k: (i, j)))
        args.append(res)
    return pl.pallas_call(
        body, grid=(M // tm, N // tn, nk), in_specs=in_specs,
        out_specs=pl.BlockSpec((tm, tn), lambda i, j, k: (i, j)),
        out_shape=SDS((M, N), out_dtype), scratch_shapes=[pltpu.VMEM((tm, tn), f32)],
        compiler_params=_params(("parallel", "parallel", "arbitrary"), VMEM_BIG), name=name)(*args)


def _mmT(a, b, *, name):
    S, M = a.shape
    N = b.shape[1]
    tn = _pick(N, WIDE)
    tm = _pick(M, WIDE, cap=TILE_BYTES // (tn * 4))
    tk = min(S, 1024)
    nk = S // tk

    def body(a_ref, b_ref, o_ref):
        k = pl.program_id(2)
        part = lax.dot_general(a_ref[...].astype(bf16), b_ref[...].astype(bf16), (((0,), (0,)), ((), ())),
                               preferred_element_type=f32)

        @pl.when(k == 0)
        def _():
            o_ref[...] = part

        @pl.when(k > 0)
        def _():
            o_ref[...] += part

    return pl.pallas_call(
        body, grid=(M // tm, N // tn, nk),
        in_specs=[pl.BlockSpec((tk, tm), lambda i, j, k: (k, i)), pl.BlockSpec((tk, tn), lambda i, j, k: (k, j))],
        out_specs=pl.BlockSpec((tm, tn), lambda i, j, k: (i, j)),
        out_shape=SDS((M, N), f32),
        compiler_params=_params(("parallel", "parallel", "arbitrary"), VMEM_BIG), name=name)(a, b)


def _pw(fn, ins, in_specs, outs, out_specs, grid, *, n_acc=0, name):
    n_in = len(ins)
    n_out = len(outs)

    def body(*refs):
        vals = fn(*[r[...].astype(f32) if r.dtype == bf16 else r[...] for r in refs[:n_in]])
        if not isinstance(vals, (tuple, list)):
            vals = (vals,)
        orefs = refs[n_in:]
        for r, v in zip(orefs[:n_out - n_acc], vals[:n_out - n_acc]):
            r[...] = v.astype(r.dtype)
        if n_acc:
            i = pl.program_id(1)

            @pl.when(i == 0)
            def _():
                for r, v in zip(orefs[n_out - n_acc:], vals[n_out - n_acc:]):
                    r[...] = v.astype(r.dtype)

            @pl.when(i > 0)
            def _():
                for r, v in zip(orefs[n_out - n_acc:], vals[n_out - n_acc:]):
                    r[...] += v.astype(r.dtype)

    res = pl.pallas_call(
        body, grid=grid, in_specs=in_specs, out_specs=out_specs, out_shape=outs,
        compiler_params=_params(("parallel", "arbitrary"), VMEM_BIG), name=name)(*ins)
    return res


def _row(T, w, col=None):
    if col is None:
        return pl.BlockSpec((T, w), lambda j, i: (i, 0))
    return pl.BlockSpec((T, w), lambda j, i: (i, col(j)))


def _par(w, col=None):
    if col is None:
        return pl.BlockSpec((1, w), lambda j, i: (0, 0))
    return pl.BlockSpec((1, w), lambda j, i: (0, col(j)))


def _rms(x, g):
    return x * lax.rsqrt(jnp.mean(x * x, axis=-1, keepdims=True) + RMS_EPS) * g


def _rope(x, cos, sinm, half):
    if half == 64:
        partner = pltpu.roll(x, 64, axis=1)
    else:
        lane = lax.broadcasted_iota(jnp.int32, x.shape, 1)
        partner = jnp.where((lane % (2 * half)) < half, pltpu.roll(x, 128 - half, axis=1), pltpu.roll(x, half, axis=1))
    return x * cos + partner * sinm


def _rope_t(x, cos, sinm, half):
    return _rope(x, cos, -sinm, half)


def _rmsnorm_fwd(x, g, *, name):
    S, W = x.shape
    T = min(S, 1024)
    return _pw(lambda xv, gv: _rms(xv, gv), [x, g], [_row(T, W), _par(W)], [SDS((S, W), bf16)], [_row(T, W)],
               (1, S // T), name=name)[0]


def _rmsnorm_bwd(x, g, dh, dres, *, name):
    S, W = x.shape
    T = min(S, 512)

    def fn(xv, gv, dhv, drv):
        _, vjp = jax.vjp(_rms, xv, gv)
        dx, dg = vjp(dhv)
        return dx + drv, dg

    return _pw(fn, [x, g, dh, dres], [_row(T, W), _par(W), _row(T, W), _row(T, W)],
               [SDS((S, W), f32), SDS((1, W), f32)], [_row(T, W), _par(W)], (1, S // T), n_acc=1, name=name)


def _ret_tables(lg, reverse):
    C = RET_CHUNK
    ii = lax.broadcasted_iota(jnp.int32, (C, C), 0).astype(f32)
    jj = lax.broadcasted_iota(jnp.int32, (C, C), 1).astype(f32)
    if not reverse:
        E = ii - jj
        mask = E >= 0
        eq = ii + 1.0
        ek = (C - 1.0) - ii
    else:
        E = jj - ii
        mask = E > 0
        eq = C - ii
        ek = ii
    Dm = jnp.where(mask, jnp.exp(jnp.where(mask, E, 0.0) * lg), 0.0)
    Em = jnp.where(mask, E, 0.0)
    qw = jnp.exp(eq * lg)
    kw = jnp.exp(ek * lg)
    qw2 = jnp.concatenate([qw, qw], axis=1)
    return Dm, Em, eq, ek, qw, kw, qw2, jnp.exp(C * lg)


def _dot(a, b, dims):
    return lax.dot_general(a.astype(bf16), b.astype(bf16), (dims, ((), ())), preferred_element_type=f32)


NN = ((1,), (0,))
NT = ((1,), (1,))
TN = ((0,), (0,))


def _ret_dir_fwd(zr, lg, cos, sinm, *, reverse, name):
    S = zr.shape[0]
    C = RET_CHUNK
    TB = min(S, 1024)
    nc = TB // C
    NB = S // TB
    d = 1 if reverse else 0
    scale = RET_DK ** -0.5

    def tb(b):
        return (NB - 1 - b) if reverse else b

    def body(lg_ref, q_ref, k_ref, v_ref, cos_ref, sin_ref, y_ref, st_ref, state):
        h = pl.program_id(0)
        b = pl.program_id(1)

        @pl.when(b == 0)
        def _():
            state[...] = jnp.zeros_like(state)

        Dm, _, _, _, _, kw, qw2, gC = _ret_tables(lg_ref[d, h], reverse)
        order = range(nc - 1, -1, -1) if reverse else range(nc)
        for c in order:
            rows = pl.ds(c * C, C)
            q = _rope(q_ref[rows, :], cos_ref[rows, :], sin_ref[rows, :], 64) * scale
            k = _rope(k_ref[rows, :], cos_ref[rows, :], sin_ref[rows, :], 64)
            v = v_ref[rows, :]
            st = state[...]
            st_ref[0, c] = st
            s = _dot(q, k, NT) * Dm
            o = _dot(s, v, NN) + _dot(q, st, NN) * qw2
            y_ref[rows, :] = o
            state[...] = gC * st + _dot(k * kw, v, TN)

    return pl.pallas_call(
        body, grid=(RET_HEADS, NB),
        in_specs=[pl.BlockSpec(memory_space=pltpu.SMEM),
                  pl.BlockSpec((TB, 128), lambda h, b: (tb(b), h)),
                  pl.BlockSpec((TB, 128), lambda h, b: (tb(b), 4 + h)),
                  pl.BlockSpec((TB, 256), lambda h, b: (tb(b), 4 + h)),
                  pl.BlockSpec((TB, 128), lambda h, b: (tb(b), 0)),
                  pl.BlockSpec((TB, 128), lambda h, b: (tb(b), 0))],
        out_specs=[pl.BlockSpec((TB, 256), lambda h, b: (tb(b), h)),
                   pl.BlockSpec((1, nc, 128, 256), lambda h, b: (h, tb(b), 0, 0))],
        out_shape=[SDS((S, 1024), f32), SDS((RET_HEADS, S // C, 128, 256), f32)],
        scratch_shapes=[pltpu.VMEM((128, 256), f32)],
        compiler_params=_params(("parallel", "arbitrary")), name=name)(lg, zr, zr, zr, cos, sinm)


def _ret_dir_bwd(zr, lg, cos, sinm, dy, states, *, reverse, name):
    S = zr.shape[0]
    C = RET_CHUNK
    TB = min(S, 1024)
    nc = TB // C
    NB = S // TB
    d = 1 if reverse else 0
    scale = RET_DK ** -0.5

    def tb(b):
        return b if reverse else (NB - 1 - b)

    def body(lg_ref, q_ref, k_ref, v_ref, cos_ref, sin_ref, dy_ref, st_ref, dq_ref, dk_ref, dv_ref, dlg_ref, dstate):
        h = pl.program_id(0)
        b = pl.program_id(1)

        @pl.when(b == 0)
        def _():
            dstate[...] = jnp.zeros_like(dstate)
            dlg_ref[...] = jnp.zeros_like(dlg_ref)

        Dm, Em, eq, ek, qw, kw, qw2, gC = _ret_tables(lg_ref[d, h], reverse)
        order = range(nc) if reverse else range(nc - 1, -1, -1)
        dlg = jnp.zeros((), f32)
        for c in order:
            rows = pl.ds(c * C, C)
            cs, sn = cos_ref[rows, :], sin_ref[rows, :]
            q = _rope(q_ref[rows, :], cs, sn, 64) * scale
            k = _rope(k_ref[rows, :], cs, sn, 64)
            v = v_ref[rows, :]
            do = dy_ref[rows, :]
            st = st_ref[0, c]
            ds = dstate[...]
            p = _dot(q, k, NT)
            a = p * Dm
            dp = _dot(do, v, NT) * Dm
            dq_cross = _dot(do, st, NT) * qw
            dk_cross = _dot(v, ds, NT) * kw
            dq = _dot(dp, k, NN) + dq_cross
            dk = _dot(dp, q, TN) + dk_cross
            dv = _dot(a, do, TN) + _dot(k * kw, ds, NN)
            dlg = dlg + jnp.sum(dp * p * Em) + jnp.sum(dq_cross * q * eq) + jnp.sum(dk_cross * k * ek) \
                + C * gC * jnp.sum(ds * st)
            dstate[...] = gC * ds + _dot(q * qw, do, TN)
            dq_ref[rows, :] = _rope_t(dq, cs, sn, 64) * scale
            dk_ref[rows, :] = _rope_t(dk, cs, sn, 64)
            dv_ref[rows, :] = dv
        dlg_ref[...] += jnp.full(dlg_ref.shape, dlg, f32)

    return pl.pallas_call(
        body, grid=(RET_HEADS, NB),
        in_specs=[pl.BlockSpec(memory_space=pltpu.SMEM),
                  pl.BlockSpec((TB, 128), lambda h, b: (tb(b), h)),
                  pl.BlockSpec((TB, 128), lambda h, b: (tb(b), 4 + h)),
                  pl.BlockSpec((TB, 256), lambda h, b: (tb(b), 4 + h)),
                  pl.BlockSpec((TB, 128), lambda h, b: (tb(b), 0)),
                  pl.BlockSpec((TB, 128), lambda h, b: (tb(b), 0)),
                  pl.BlockSpec((TB, 256), lambda h, b: (tb(b), h)),
                  pl.BlockSpec((1, nc, 128, 256), lambda h, b: (h, tb(b), 0, 0))],
        out_specs=[pl.BlockSpec((TB, 128), lambda h, b: (tb(b), h)),
                   pl.BlockSpec((TB, 128), lambda h, b: (tb(b), h)),
                   pl.BlockSpec((TB, 256), lambda h, b: (tb(b), h)),
                   pl.BlockSpec((1, 1, 128), lambda h, b: (h, 0, 0))],
        out_shape=[SDS((S, 512), f32), SDS((S, 512), f32), SDS((S, 1024), f32), SDS((RET_HEADS, 1, 128), f32)],
        scratch_shapes=[pltpu.VMEM((128, 256), f32)],
        compiler_params=_params(("parallel", "arbitrary")), name=name)(lg, zr, zr, zr, cos, sinm, dy, states)


def _gn_gate(yf, yb, g, gn):
    y = yf + yb
    mu = jnp.mean(y, axis=-1, keepdims=True)
    var = jnp.mean(jnp.square(y - mu), axis=-1, keepdims=True)
    yn = (y - mu) * lax.rsqrt(var + GN_EPS)
    return jax.nn.silu(g) * (yn * gn)


def _flash_fwd(Q, K, kv, *, name):
    S = Q.shape[0]
    hq = min(S, 256)
    nh = 4 if S % 1024 == 0 else 1
    tq = nh * hq
    tk = min(S, 8192)
    nk = S // tk

    def body(q_ref, k_ref, v_ref, o_ref, l_ref, m_s, l_s, acc):
        kk = pl.program_id(2)

        @pl.when(kk == 0)
        def _():
            m_s[...] = jnp.full_like(m_s, -jnp.inf)
            l_s[...] = jnp.zeros_like(l_s)
            acc[...] = jnp.zeros_like(acc)

        k = k_ref[...]
        v = v_ref[...]
        sts = [lax.dot_general(k, q_ref[hf * hq:(hf + 1) * hq, :], (NT, ((), ())), preferred_element_type=f32)
               for hf in range(nh)]
        for hf in range(nh):
            st = sts[hf]
            m_prev = m_s[hf]
            m_new = jnp.maximum(m_prev, jnp.max(st, axis=0, keepdims=True))
            pt = jnp.exp2(st - m_new)
            alpha = jnp.exp2(m_prev - m_new)
            l_s[hf] = alpha * l_s[hf] + jnp.sum(pt, axis=0, keepdims=True)
            acc[hf] = alpha * acc[hf] + lax.dot_general(v, pt.astype(bf16), (TN, ((), ())), preferred_element_type=f32)
            m_s[hf] = m_new

        @pl.when(kk == nk - 1)
        def _():
            for hf in range(nh):
                o_ref[hf * hq:(hf + 1) * hq, :] = jnp.transpose(acc[hf] / l_s[hf]).astype(bf16)
                l_ref[0, :, hf * hq:(hf + 1) * hq] = m_s[hf] + jnp.log2(l_s[hf])

    return pl.pallas_call(
        body, grid=(MLA_HEADS, S // tq, nk),
        in_specs=[pl.BlockSpec((tq, 256), lambda h, i, k: (i, h)),
                  pl.BlockSpec((tk, 256), lambda h, i, k: (k, h)),
                  pl.BlockSpec((tk, 128), lambda h, i, k: (k, 2 * h + 1))],
        out_specs=[pl.BlockSpec((tq, 128), lambda h, i, k: (i, h)), pl.BlockSpec((1, 1, tq), lambda h, i, k: (h, 0, i))],
        out_shape=[SDS((S, 1024), bf16), SDS((MLA_HEADS, 1, S), f32)],
        scratch_shapes=[pltpu.VMEM((nh, 1, hq), f32), pltpu.VMEM((nh, 1, hq), f32), pltpu.VMEM((nh, 128, hq), f32)],
        compiler_params=_params(("parallel", "parallel", "arbitrary"), VMEM_BIG), name=name)(Q, K, kv)


def _attn_delta(dO, O, *, name):
    S = dO.shape[0]
    T = min(S, 512)

    def body(do_ref, o_ref, d_ref):
        ones = jnp.ones((8, 128), bf16)
        for h in range(MLA_HEADS):
            cols = slice(128 * h, 128 * h + 128)
            prod = do_ref[:, cols].astype(f32) * o_ref[:, cols].astype(f32)
            hi = prod.astype(bf16)
            lo = (prod - hi.astype(f32)).astype(bf16)
            row = lax.dot_general(ones, hi, (NT, ((), ())), preferred_element_type=f32) \
                + lax.dot_general(ones, lo, (NT, ((), ())), preferred_element_type=f32)
            d_ref[h] = row[0:1, :]

    return pl.pallas_call(
        body, grid=(S // T,),
        in_specs=[pl.BlockSpec((T, 1024), lambda i: (i, 0)), pl.BlockSpec((T, 1024), lambda i: (i, 0))],
        out_specs=pl.BlockSpec((MLA_HEADS, 1, T), lambda i: (0, 0, i)), out_shape=SDS((MLA_HEADS, 1, S), f32),
        compiler_params=_params(("parallel",)), name=name)(dO, O)


def _flash_bwd(Q, K, kv, delta, L, dO, *, name):
    S = Q.shape[0]
    hq = min(S, 512)
    nh = 2 if S % 1024 == 0 else 1
    tq = nh * hq
    tk = min(S, 2048)
    nq = S // tq
    ln2 = math.log(2.0)

    def body(q_ref, k_ref, v_ref, dl_ref, l_ref, do_ref, dq_ref, dk_ref, dv_ref, dk_acc, dv_acc):
        kk = pl.program_id(1)
        i = pl.program_id(2)

        @pl.when((kk == 0) & (i == 0))
        def _():
            dq_ref[...] = jnp.zeros_like(dq_ref)

        @pl.when(i == 0)
        def _():
            dk_acc[...] = jnp.zeros_like(dk_acc)
            dv_acc[...] = jnp.zeros_like(dv_acc)

        k = k_ref[...]
        v = v_ref[...]
        dk_new = dk_acc[...]
        dv_new = dv_acc[...]
        for hf in range(nh):
            sl = slice(hf * hq, (hf + 1) * hq)
            q = q_ref[sl, :]
            st = lax.dot_general(k, q, (NT, ((), ())), preferred_element_type=f32)
            pt = jnp.exp2(st - l_ref[0, :, sl])
            delta = dl_ref[0, :, sl]
            dob = do_ref[sl, :].astype(bf16)
            dv_new = dv_new + lax.dot_general(pt.astype(bf16), dob, (NN, ((), ())), preferred_element_type=f32)
            dpt = lax.dot_general(v, dob, (NT, ((), ())), preferred_element_type=f32)
            dst = (pt * (dpt - delta)).astype(bf16)
            dk_new = dk_new + lax.dot_general(dst, q, (NN, ((), ())), preferred_element_type=f32)
            dq_ref[0, i * nh + hf] += lax.dot_general(k, dst, (TN, ((), ())), preferred_element_type=f32)
        dk_acc[...] = dk_new
        dv_acc[...] = dv_new

        @pl.when(i == nq - 1)
        def _():
            dk_ref[...] = dk_acc[...] * ln2
            dv_ref[...] = dv_acc[...]

    return pl.pallas_call(
        body, grid=(MLA_HEADS, S // tk, nq),
        in_specs=[pl.BlockSpec((tq, 256), lambda h, k, i: (i, h)),
                  pl.BlockSpec((tk, 256), lambda h, k, i: (k, h)),
                  pl.BlockSpec((tk, 128), lambda h, k, i: (k, 2 * h + 1)),
                  pl.BlockSpec((1, 1, tq), lambda h, k, i: (h, 0, i)),
                  pl.BlockSpec((1, 1, tq), lambda h, k, i: (h, 0, i)),
                  pl.BlockSpec((tq, 128), lambda h, k, i: (i, h))],
        out_specs=[pl.BlockSpec((1, S // hq, 256, hq), lambda h, k, i: (h, 0, 0, 0)),
                   pl.BlockSpec((tk, 256), lambda h, k, i: (k, h)),
                   pl.BlockSpec((tk, 128), lambda h, k, i: (k, h))],
        out_shape=[SDS((MLA_HEADS, S // hq, 256, hq), f32), SDS((S, 2048), f32), SDS((S, 1024), f32)],
        scratch_shapes=[pltpu.VMEM((tk, 256), f32), pltpu.VMEM((tk, 128), f32)],
        compiler_params=_params(("parallel", "arbitrary", "arbitrary"), VMEM_BIG), name=name)(Q, K, kv, delta, L, dO)


def _mla_qk_prep(q, kv, zm, cosm, sinm, *, name):
    S = q.shape[0]
    T = min(S, 512)
    scale = (MLA_NOPE + MLA_ROPE) ** -0.5 * math.log2(math.e)

    def body(q_ref, kv_ref, kr_ref, cos_ref, sin_ref, oq_ref, ok_ref):
        cs, sn = cos_ref[...], sin_ref[...]
        kr = _rope(kr_ref[...], cs, sn, 32).astype(bf16)
        for h in range(MLA_HEADS):
            a = 256 * h
            oq_ref[:, a:a + 128] = (q_ref[:, a:a + 128].astype(f32) * scale).astype(bf16)
            oq_ref[:, a + 128:a + 256] = (_rope(q_ref[:, a + 128:a + 256].astype(f32), cs, sn, 32) * scale).astype(bf16)
            ok_ref[:, a:a + 128] = kv_ref[:, a:a + 128]
            ok_ref[:, a + 128:a + 256] = kr

    row = lambda w, col=0: pl.BlockSpec((T, w), lambda i: (i, col))
    return pl.pallas_call(
        body, grid=(S // T,), in_specs=[row(2048), row(2048), row(128, 6), row(128), row(128)],
        out_specs=[row(2048), row(2048)], out_shape=[SDS((S, 2048), bf16), SDS((S, 2048), bf16)],
        compiler_params=_params(("parallel",), VMEM_BIG), name=name)(q, kv, zm, cosm, sinm)


def _mla_bwd_prep(dQ, dK, dV, cosm, sinm, *, name):
    S = dK.shape[0]
    T = dQ.shape[3]
    scale = (MLA_NOPE + MLA_ROPE) ** -0.5

    def body(dq_ref, dk_ref, dv_ref, cos_ref, sin_ref, oq_ref, okv_ref, okr_ref):
        cs, sn = cos_ref[...], sin_ref[...]
        kr = jnp.zeros((T, 128), f32)
        for h in range(MLA_HEADS):
            a = 256 * h
            dq = jnp.transpose(dq_ref[h, 0])
            oq_ref[:, a:a + 128] = (dq[:, 0:128] * scale).astype(bf16)
            oq_ref[:, a + 128:a + 256] = (_rope_t(dq[:, 128:256], cs, sn, 32) * scale).astype(bf16)
            okv_ref[:, a:a + 128] = dk_ref[:, a:a + 128].astype(bf16)
            okv_ref[:, a + 128:a + 256] = dv_ref[:, 128 * h:128 * h + 128].astype(bf16)
            kr = kr + dk_ref[:, a + 128:a + 256]
        okr_ref[...] = _rope_t(kr, cs, sn, 32)

    return pl.pallas_call(
        body, grid=(S // T,),
        in_specs=[pl.BlockSpec((MLA_HEADS, 1, 256, T), lambda i: (0, i, 0, 0)), pl.BlockSpec((T, 2048), lambda i: (i, 0)),
                  pl.BlockSpec((T, 1024), lambda i: (i, 0)), pl.BlockSpec((T, 128), lambda i: (i, 0)),
                  pl.BlockSpec((T, 128), lambda i: (i, 0))],
        out_specs=[pl.BlockSpec((T, 2048), lambda i: (i, 0)), pl.BlockSpec((T, 2048), lambda i: (i, 0)),
                   pl.BlockSpec((T, 128), lambda i: (i, 0))],
        out_shape=[SDS((S, 2048), bf16), SDS((S, 2048), bf16), SDS((S, 128), f32)],
        compiler_params=_params(("parallel",), VMEM_BIG), name=name)(dQ, dK, dV, cosm, sinm)


def _mla_norm_bwd(zm, qg, kvg, dcqn, dckvn, dkr, *, name):
    S = zm.shape[0]
    T = min(S, 512)

    def body(cq_ref, ckv_ref, qg_ref, kvg_ref, dcq_ref, dckv_ref, dkr_ref, o_ref, dqg_ref, dkvg_ref):
        i = pl.program_id(0)
        _, vjp = jax.vjp(_rms, cq_ref[...], qg_ref[...])
        dcq, dqg = vjp(dcq_ref[...])
        _, vjp2 = jax.vjp(_rms, ckv_ref[...], kvg_ref[...])
        dckv, dkvg = vjp2(dckv_ref[...])
        o_ref[:, 0:384] = dcq.astype(bf16)
        o_ref[:, 384:512] = jnp.zeros((T, 128), bf16)
        o_ref[:, 512:768] = dckv.astype(bf16)
        o_ref[:, 768:896] = dkr_ref[...].astype(bf16)

        @pl.when(i == 0)
        def _():
            dqg_ref[...] = dqg
            dkvg_ref[...] = dkvg

        @pl.when(i > 0)
        def _():
            dqg_ref[...] += dqg
            dkvg_ref[...] += dkvg

    return pl.pallas_call(
        body, grid=(S // T,),
        in_specs=[pl.BlockSpec((T, 384), lambda i: (i, 0)), pl.BlockSpec((T, 256), lambda i: (i, 2)),
                  pl.BlockSpec((1, 384), lambda i: (0, 0)), pl.BlockSpec((1, 256), lambda i: (0, 0)),
                  pl.BlockSpec((T, 384), lambda i: (i, 0)), pl.BlockSpec((T, 256), lambda i: (i, 0)),
                  pl.BlockSpec((T, 128), lambda i: (i, 0))],
        out_specs=[pl.BlockSpec((T, 896), lambda i: (i, 0)), pl.BlockSpec((1, 384), lambda i: (0, 0)),
                   pl.BlockSpec((1, 256), lambda i: (0, 0))],
        out_shape=[SDS((S, 896), bf16), SDS((1, 384), f32), SDS((1, 256), f32)],
        compiler_params=_params(("arbitrary",)), name=name)(zm, zm, qg, kvg, dcqn, dckvn, dkr)


def _s5_disc(a_re, a_im, ldt, b_re, b_im):
    dt = jnp.exp(ldt)
    ar = jnp.minimum(a_re, -1e-4)
    mag = jnp.exp(dt * ar)
    abr = mag * jnp.cos(dt * a_im)
    abi = mag * jnp.sin(dt * a_im)
    den = ar * ar + a_im * a_im
    nr = abr - 1.0
    ni = abi
    cr = (nr * ar + ni * a_im) / den
    ci = (ni * ar - nr * a_im) / den
    return abr, abi, cr * b_re - ci * b_im, cr * b_im + ci * b_re


def _s5_param_fwd(a_re, a_im, ldt, b_re, b_im, *, name):
    R = SDS((1, 8192), f32)
    M = SDS((16, 8192), f32)
    LP = S5_T // S5_SEG
    Pw = SDS((LP, 8192), f32)

    def body(a_re_r, a_im_r, ldt_r, b_re_r, b_im_r, o1, o2, o3, o4, p_re, p_im):
        abr, abi, bbr, bbi = _s5_disc(a_re_r[...], a_im_r[...], ldt_r[...], b_re_r[...], b_im_r[...])
        o1[...] = abr
        o2[...] = abi
        o3[...] = bbr
        o4[...] = bbi
        dt = jnp.exp(ldt_r[...])
        ar = jnp.minimum(a_re_r[...], -1e-4)
        n = lax.broadcasted_iota(jnp.int32, (LP, 8192), 0).astype(f32) + 1.0
        mag = jnp.exp(n * (dt * ar))
        ang = n * (dt * a_im_r[...])
        p_re[...] = mag * jnp.cos(ang)
        p_im[...] = mag * jnp.sin(ang)

    return pl.pallas_call(body, out_shape=[R, R, M, M, Pw, Pw], name=name)(a_re, a_im, ldt, b_re, b_im)


def _s5_param_bwd(a_re, a_im, ldt, b_re, b_im, d_abr, d_abi, d_bbr, d_bbi, *, name):
    R = SDS((1, 8192), f32)
    M = SDS((16, 8192), f32)

    def body(a_re_r, a_im_r, ldt_r, b_re_r, b_im_r, c1, c2, c3, c4, o1, o2, o3, o4, o5):
        _, vjp = jax.vjp(_s5_disc, a_re_r[...], a_im_r[...], ldt_r[...], b_re_r[...], b_im_r[...])
        g = vjp((c1[...], c2[...], c3[...], c4[...]))
        for o, v in zip((o1, o2, o3, o4, o5), g):
            o[...] = v

    return pl.pallas_call(body, out_shape=[R, R, R, M, M], name=name)(a_re, a_im, ldt, b_re, b_im, d_abr, d_abi, d_bbr, d_bbi)


def _seg_perm(T, inverse):
    L = T // S5_SEG
    i = jnp.arange(T)
    src = (i % S5_SEG) * L + i // S5_SEG
    P = (src[:, None] == jnp.arange(T)[None, :]).astype(bf16)
    return P.T if inverse else P


def _perm_rows(a, P, *, name):
    S, W = a.shape
    T = P.shape[0]

    def body(p_ref, a_ref, o_ref):
        o_ref[...] = lax.dot_general(p_ref[...], a_ref[...], (NN, ((), ())), preferred_element_type=f32).astype(o_ref.dtype)

    return pl.pallas_call(
        body, grid=(S // T,), in_specs=[pl.BlockSpec((T, T), lambda i: (0, 0)), pl.BlockSpec((T, W), lambda i: (i, 0))],
        out_specs=pl.BlockSpec((T, W), lambda i: (i, 0)), out_shape=SDS((S, W), a.dtype),
        compiler_params=_params(("parallel",)), name=name)(P, a)


def _scan_core(xr, xi, ar, ai, pwr_ref, pwi_ref, a64r, a64i, carry, *, reverse, T, conj):
    L = T // S5_SEG
    sg = -1.0 if conj else 1.0
    arb = jnp.broadcast_to(ar, (8, 512))
    aib = jnp.broadcast_to(ai, (8, 512))
    UN = 4

    def step(r4, c):
        cr, ci = c
        for u in range(UN):
            r0 = r4 * UN + u
            r = (L - 1 - r0) if reverse else r0
            rows = pl.ds(pl.multiple_of(r * 8, 8), 8)
            nr = arb * cr - aib * ci + xr[rows, :]
            ni = arb * ci + aib * cr + xi[rows, :]
            xr[rows, :] = nr
            xi[rows, :] = ni
            cr, ci = nr, ni
        return cr, ci

    lr, li = lax.fori_loop(0, L // UN, step, (jnp.zeros((8, 512), f32), jnp.zeros((8, 512), f32)))
    row8 = lax.broadcasted_iota(jnp.int32, (8, 512), 0)
    cr = carry[0, 0:1, :]
    ci = carry[1, 0:1, :]
    a6i = sg * a64i
    cin_r = jnp.zeros((8, 512), f32)
    cin_i = jnp.zeros((8, 512), f32)
    for seg in (range(S5_SEG - 1, -1, -1) if reverse else range(S5_SEG)):
        cin_r = jnp.where(row8 == seg, cr, cin_r)
        cin_i = jnp.where(row8 == seg, ci, cin_i)
        ncr = lr[seg:seg + 1, :] + a64r * cr - a6i * ci
        nci = li[seg:seg + 1, :] + a64r * ci + a6i * cr
        cr, ci = ncr, nci
    carry[0, 0:1, :] = cr
    carry[1, 0:1, :] = ci

    def fix(r4, _):
        for u in range(UN):
            r = r4 * UN + u
            rows = pl.ds(pl.multiple_of(r * 8, 8), 8)
            pr = pwr_ref[pl.ds(r, 1), :]
            pi = sg * pwi_ref[pl.ds(r, 1), :]
            xr[rows, :] += pr * cin_r - pi * cin_i
            xi[rows, :] += pr * cin_i + pi * cin_r
        return 0

    lax.fori_loop(0, L // UN, fix, 0)


def _s5_scan_fwd(u, BBr, BBi, CCr, CCi, abr, abi, pwr, pwi, *, reverse, name):
    S = u.shape[0]
    T = S5_T
    NB = S // T
    L = T // S5_SEG
    d = 1 if reverse else 0

    def tb(b):
        return (NB - 1 - b) if reverse else b

    def body(u_ref, bbr_ref, bbi_ref, ccr_ref, cci_ref, ar_ref, ai_ref, pwr_ref, pwi_ref, y_ref, xr_ref, xi_ref, carry):
        b = pl.program_id(1)

        @pl.when(b == 0)
        def _():
            carry[...] = jnp.zeros_like(carry)

        ub = u_ref[...].astype(bf16)
        xr_ref[...] = lax.dot_general(ub, bbr_ref[0, 0], (NN, ((), ())), preferred_element_type=f32)
        xi_ref[...] = lax.dot_general(ub, bbi_ref[0, 0], (NN, ((), ())), preferred_element_type=f32)
        a6 = (0 if reverse else L - 1)
        _scan_core(xr_ref, xi_ref, ar_ref[...], ai_ref[...], pwr_ref, pwi_ref, pwr_ref[a6:a6 + 1, :], pwi_ref[a6:a6 + 1, :],
                   carry, reverse=reverse, T=T, conj=False)
        y_ref[...] = _dot(xr_ref[...], ccr_ref[0, 0], NN) - _dot(xi_ref[...], cci_ref[0, 0], NN)

    mat = lambda shp: pl.BlockSpec((1, 1) + shp, lambda j, b: (d, j, 0, 0))
    vec = lambda r: pl.BlockSpec((r, 512), lambda j, b: (0, d * S5_NJ + j))
    return pl.pallas_call(
        body, grid=(S5_NJ, NB),
        in_specs=[pl.BlockSpec((T, 128), lambda j, b: (tb(b), j)), mat((128, 512)), mat((128, 512)), mat((512, 128)),
                  mat((512, 128)), vec(1), vec(1), vec(L), vec(L)],
        out_specs=[pl.BlockSpec((T, 128), lambda j, b: (tb(b), j)), pl.BlockSpec((T, 512), lambda j, b: (tb(b), j)),
                   pl.BlockSpec((T, 512), lambda j, b: (tb(b), j))],
        out_shape=[SDS((S, 1024), f32), SDS((S, 4096), f32), SDS((S, 4096), f32)],
        scratch_shapes=[pltpu.VMEM((2, 8, 512), f32)],
        compiler_params=_params(("parallel", "arbitrary")), name=name)(u, BBr, BBi, CCr, CCi, abr, abi, pwr, pwi)


def _s5_scan_bwd(u, dy, xr, xi, BBr, BBi, CCr, CCi, abr, abi, pwr, pwi, *, reverse, name):
    S = u.shape[0]
    T = S5_T
    NB = S // T
    L = T // S5_SEG
    d = 1 if reverse else 0
    adj_rev = not reverse

    def tb(b):
        return b if reverse else (NB - 1 - b)

    def bnd(b):
        t = tb(b)
        if reverse:
            return jnp.minimum((t + 1) * (T // 8), S // 8 - 1)
        return jnp.maximum(t * (T // 8) - 1, 0)

    def body(u_ref, dy_ref, xr_ref, xi_ref, xbr_ref, xbi_ref, bbr_ref, bbi_ref, ccr_ref, cci_ref, ar_ref, ai_ref,
             pwr_ref, pwi_ref, du_ref, dbbr_ref, dbbi_ref, dccr_ref, dcci_ref, dar_ref, dai_ref, carry, lam):
        b = pl.program_id(1)

        @pl.when(b == 0)
        def _():
            carry[...] = jnp.zeros_like(carry)
            for r in (dbbr_ref, dbbi_ref, dccr_ref, dcci_ref, dar_ref, dai_ref):
                r[...] = jnp.zeros_like(r)

        dyb = dy_ref[...]
        lam[0] = lax.dot_general(dyb, ccr_ref[0, 0], (NT, ((), ())), preferred_element_type=f32)
        lam[1] = -lax.dot_general(dyb, cci_ref[0, 0], (NT, ((), ())), preferred_element_type=f32)
        a6 = (0 if adj_rev else L - 1)
        _scan_core(lam.at[0], lam.at[1], ar_ref[...], -ai_ref[...], pwr_ref, pwi_ref, pwr_ref[a6:a6 + 1, :],
                   pwi_ref[a6:a6 + 1, :], carry, reverse=adj_rev, T=T, conj=True)
        ub = u_ref[...].astype(bf16)
        first = (b == NB - 1)
        lrb = lam[0].astype(bf16)
        lib = lam[1].astype(bf16)
        du_ref[...] = lax.dot_general(lrb, bbr_ref[0, 0], (NT, ((), ())), preferred_element_type=f32) \
            + lax.dot_general(lib, bbi_ref[0, 0], (NT, ((), ())), preferred_element_type=f32)
        dbbr_ref[0, 0] += lax.dot_general(ub, lrb, (TN, ((), ())), preferred_element_type=f32)
        dbbi_ref[0, 0] += lax.dot_general(ub, lib, (TN, ((), ())), preferred_element_type=f32)
        dccr_ref[0, 0] += lax.dot_general(dyb, xr_ref[...].astype(bf16), (TN, ((), ())), preferred_element_type=f32)
        dcci_ref[0, 0] -= lax.dot_general(dyb, xi_ref[...].astype(bf16), (TN, ((), ())), preferred_element_type=f32)
        row8 = lax.broadcasted_iota(jnp.int32, (8, 512), 0)
        if reverse:
            body_x, body_l, edge_l = slice(8, T), slice(0, T - 8), slice(T - 8, T)
            sp_r = jnp.where(row8 == 7, jnp.where(first, 0.0, xbr_ref[0:1, :]), pltpu.roll(xr_ref[0:8, :], 7, axis=0))
            sp_i = jnp.where(row8 == 7, jnp.where(first, 0.0, xbi_ref[0:1, :]), pltpu.roll(xi_ref[0:8, :], 7, axis=0))
        else:
            body_x, body_l, edge_l = slice(0, T - 8), slice(8, T), slice(0, 8)
            sp_r = jnp.where(row8 == 0, jnp.where(first, 0.0, xbr_ref[7:8, :]), pltpu.roll(xr_ref[T - 8:T, :], 1, axis=0))
            sp_i = jnp.where(row8 == 0, jnp.where(first, 0.0, xbi_ref[7:8, :]), pltpu.roll(xi_ref[T - 8:T, :], 1, axis=0))
        xpr, xpi = xr_ref[body_x, :], xi_ref[body_x, :]
        lr, li = lam[0, body_l, :], lam[1, body_l, :]
        er, ei = lam[0, edge_l, :], lam[1, edge_l, :]
        dar_ref[...] += jnp.sum(xpr * lr + xpi * li, axis=0, keepdims=True) + jnp.sum(sp_r * er + sp_i * ei, axis=0, keepdims=True)
        dai_ref[...] += jnp.sum(xpr * li - xpi * lr, axis=0, keepdims=True) + jnp.sum(sp_r * ei - sp_i * er, axis=0, keepdims=True)

    mat = lambda shp: pl.BlockSpec((1, 1) + shp, lambda j, b: (d, j, 0, 0))
    omat = lambda shp: pl.BlockSpec((1, 1) + shp, lambda j, b: (0, j, 0, 0))
    vec = lambda r: pl.BlockSpec((r, 512), lambda j, b: (0, d * S5_NJ + j))
    blk = lambda w: pl.BlockSpec((T, w), lambda j, b: (tb(b), j))
    return pl.pallas_call(
        body, grid=(S5_NJ, NB),
        in_specs=[blk(128), blk(128), blk(512), blk(512),
                  pl.BlockSpec((8, 512), lambda j, b: (bnd(b), j)), pl.BlockSpec((8, 512), lambda j, b: (bnd(b), j)),
                  mat((128, 512)), mat((128, 512)), mat((512, 128)), mat((512, 128)), vec(1), vec(1), vec(L), vec(L)],
        out_specs=[blk(128), omat((128, 512)), omat((128, 512)), omat((128, 512)), omat((128, 512)),
                   pl.BlockSpec((1, 512), lambda j, b: (0, j)), pl.BlockSpec((1, 512), lambda j, b: (0, j))],
        out_shape=[SDS((S, 1024), f32), SDS((1, 8, 128, 512), f32), SDS((1, 8, 128, 512), f32), SDS((1, 8, 128, 512), f32),
                   SDS((1, 8, 128, 512), f32), SDS((1, 4096), f32), SDS((1, 4096), f32)],
        scratch_shapes=[pltpu.VMEM((2, 8, 512), f32), pltpu.VMEM((2, T, 512), f32)],
        compiler_params=_params(("parallel", "arbitrary"), VMEM_BIG), name=name)(
            u, dy, xr, xi, xr, xi, BBr, BBi, CCr, CCi, abr, abi, pwr, pwi)


def _silu_mul(g, u):
    return jax.nn.silu(g) * u


def _mixf(p0, p1, p2, z0, z1, z2):
    return jax.nn.sigmoid(z0) * p0 + jax.nn.sigmoid(z1) * p1 + jax.nn.sigmoid(z2) * p2


def _s5_act(yf, yb, u, dd):
    return jax.nn.gelu(yf + yb + dd * u)


def _glu(a, b):
    return a * jax.nn.sigmoid(b)


def _layer_fwd(x, w, tabs, l):
    S = x.shape[0]
    T = min(S, 512)
    I = S // T
    nm = lambda s: f"L{l}_{s}"
    sv = {'x': x}
    h = _rmsnorm_fwd(x, w['norm1_g'], name=nm("norm1"))
    zr = _mm(h, w['W_ret'], name=nm("in_ret"))
    zm = _mm(h, w['W_mla'], name=nm("in_mla"))
    h_seg = _perm_rows(h, tabs['seg_perm'], name=nm("s5_perm_h"))
    zs = _mm(h_seg, w['W_s5'], name=nm("in_s5"))
    zg = _mm(h, w['W_gate'], out_dtype=bf16, name=nm("in_gate"))
    sv.update(h=h, h_seg=h_seg, zr=zr, zm=zm, zs=zs, zg=zg)

    yf, stf = _ret_dir_fwd(zr, w['lg'], tabs['cos_r'], tabs['sin_r'], reverse=False, name=nm("ret_f"))
    yb, stb = _ret_dir_fwd(zr, w['lg'], tabs['cos_r'], tabs['sin_r'], reverse=True, name=nm("ret_b"))
    hd = lambda j: j
    y_ret = _pw(_gn_gate, [yf, yb, zr, w['ret_gn_g']],
                [_row(T, 256, hd), _row(T, 256, hd), _row(T, 256, lambda j: 8 + j), _par(256, hd)],
                [SDS((S, 1024), bf16)], [_row(T, 256, hd)], (RET_HEADS, I), name=nm("ret_gn"))[0]
    sv.update(yf=yf, yb=yb, stf=stf, stb=stb, y_ret=y_ret)

    cqn, ckvn = _pw(lambda a, b, g1, g2: (_rms(a, g1), _rms(b, g2)), [zm, zm, w['mla_q_norm_g'], w['mla_kv_norm_g']],
                    [_row(T, 384), _row(T, 256, lambda j: 2), _par(384), _par(256)],
                    [SDS((S, 384), bf16), SDS((S, 256), bf16)], [_row(T, 384), _row(T, 256)], (1, I), name=nm("mla_norm"))
    q = _mm(cqn, w['W_uq'], out_dtype=bf16, name=nm("mla_uq"))
    kv = _mm(ckvn, w['W_ukv'], out_dtype=bf16, name=nm("mla_ukv"))
    Q, K = _mla_qk_prep(q, kv, zm, tabs['cos_m'], tabs['sin_m'], name=nm("mla_qkprep"))
    O, Lse = _flash_fwd(Q, K, kv, name=nm("mla_attn"))
    sv.update(cqn=cqn, ckvn=ckvn, kv=kv, Q=Q, K=K, O=O, Lse=Lse)

    s5 = w['s5']
    ysf, xrf, xif = _s5_scan_fwd(zs, s5['BBr'], s5['BBi'], s5['CCr'], s5['CCi'], s5['abr'], s5['abi'], s5['pwr_f'], s5['pwi_f'],
                                 reverse=False, name=nm("s5_f"))
    ysb, xrb, xib = _s5_scan_fwd(zs, s5['BBr'], s5['BBi'], s5['CCr'], s5['CCi'], s5['abr'], s5['abi'], s5['pwr_f'], s5['pwi_f'],
                                 reverse=True, name=nm("s5_b"))
    gact = _pw(_s5_act, [ysf, ysb, zs, w['s5_d']], [_row(T, D), _row(T, D), _row(T, D), _par(D)],
               [SDS((S, D), bf16)], [_row(T, D)], (1, I), name=nm("s5_act"))[0]
    gg = _mm(gact, w['W_glu'], out_dtype=bf16, name=nm("s5_glu_mm"))
    y_s5 = _pw(_glu, [gg, gg], [_row(T, D), _row(T, D, lambda j: 1)], [SDS((S, D), bf16)], [_row(T, D)], (1, I),
               name=nm("s5_glu"))[0]
    y_s5 = _perm_rows(y_s5, tabs['seg_unperm'], name=nm("s5_unperm_y"))
    sv.update(ysf=ysf, ysb=ysb, xrf=xrf, xif=xif, xrb=xrb, xib=xib, gact=gact, gg=gg, y_s5=y_s5)

    ys = [y_ret, O, y_s5]
    pr = [_mm(ys[i], w['W_br'][i], out_dtype=bf16, name=nm(f"branch{i}")) for i in range(3)]
    mix = _pw(_mixf, pr + [zg, zg, zg],
              [_row(T, D)] * 3 + [_row(T, D), _row(T, D, lambda j: 1), _row(T, D, lambda j: 2)],
              [SDS((S, D), bf16)], [_row(T, D)], (1, I), name=nm("mix"))[0]
    x1 = _mm(mix, w['W_out'], res=x, name=nm("out_proj"))
    h2 = _rmsnorm_fwd(x1, w['norm2_g'], name=nm("norm2"))
    fgu = _mm(h2, w['W_gu'], out_dtype=bf16, name=nm("ffn_gu"))
    act = _pw(_silu_mul, [fgu, fgu], [_row(T, 1408, lambda j: j), _row(T, 1408, lambda j: 2 + j)],
              [SDS((S, FFN_H), bf16)], [_row(T, 1408, lambda j: j)], (2, I), name=nm("ffn_act"))[0]
    x2 = _mm(act, w['W_down'], res=x1, name=nm("ffn_down"))
    sv.update(pr=pr, mix=mix, x1=x1, h2=h2, fgu=fgu, act=act)
    return x2, sv


def _vjp_fn(fn, n_primal, cast=None):
    def g(*args):
        _, vjp = jax.vjp(fn, *args[:n_primal])
        return vjp(args[n_primal].astype(f32))
    return g


def _layer_bwd(dx2, w, tabs, sv, l):
    S = dx2.shape[0]
    T = min(S, 512)
    I = S // T
    nm = lambda s: f"L{l}_b_{s}"
    g = {}
    hd = lambda j: j

    dact = _mm(dx2, w['W_down'], tb=True, out_dtype=bf16, name=nm("ffn_down_dx"))
    g['W_down'] = _mmT(sv['act'], dx2, name=nm("ffn_down_dw"))
    dfg, dfu = _pw(_vjp_fn(_silu_mul, 2), [sv['fgu'], sv['fgu'], dact],
                   [_row(T, 1408, lambda j: j), _row(T, 1408, lambda j: 2 + j), _row(T, 1408, lambda j: j)],
                   [SDS((S, FFN_H), bf16), SDS((S, FFN_H), bf16)], [_row(T, 1408, lambda j: j)] * 2, (2, I), name=nm("ffn_act"))
    dfgu = jnp.concatenate([dfg, dfu], axis=1)
    g['W_gu'] = _mmT(sv['h2'], dfgu, name=nm("ffn_gu_dw"))
    dh2 = _mm(dfgu, w['W_gu'], tb=True, name=nm("ffn_gu_dx"))
    dx1, g['norm2_g'] = _rmsnorm_bwd(sv['x1'], w['norm2_g'], dh2, dx2, name=nm("norm2"))

    dmix = _mm(dx1, w['W_out'], tb=True, out_dtype=bf16, name=nm("out_dx"))
    g['W_out'] = _mmT(sv['mix'], dx1, name=nm("out_dw"))
    zg = sv['zg']
    outs = _pw(_vjp_fn(_mixf, 6), sv['pr'] + [zg, zg, zg, dmix],
               [_row(T, D)] * 3 + [_row(T, D), _row(T, D, lambda j: 1), _row(T, D, lambda j: 2), _row(T, D)],
               [SDS((S, D), bf16)] * 6, [_row(T, D)] * 6, (1, I), name=nm("mix"))
    dpr, dzg = outs[:3], jnp.concatenate(outs[3:], axis=1)
    ys = [sv['y_ret'], sv['O'], sv['y_s5']]
    g['W_br'] = [_mmT(ys[i], dpr[i], name=nm(f"branch{i}_dw")) for i in range(3)]
    dpr_seg = _perm_rows(dpr[2], tabs['seg_perm'], name=nm("s5_perm_dy"))
    dys = [_mm(dpr[i] if i < 2 else dpr_seg, w['W_br'][i], tb=True, out_dtype=bf16,
               name=nm(f"branch{i}_dx")) for i in range(3)]

    gg = sv['gg']
    dga, dgb = _pw(_vjp_fn(_glu, 2), [gg, gg, dys[2]], [_row(T, D), _row(T, D, lambda j: 1), _row(T, D)],
                   [SDS((S, D), bf16)] * 2, [_row(T, D)] * 2, (1, I), name=nm("s5_glu"))
    dgg = jnp.concatenate([dga, dgb], axis=1)
    g['W_glu'] = _mmT(sv['gact'], dgg, name=nm("s5_glu_dw"))
    dgact = _mm(dgg, w['W_glu'], tb=True, out_dtype=bf16, name=nm("s5_glu_dx"))

    def act_bwd(yf, yb, u, dd, ct):
        _, vjp = jax.vjp(_s5_act, yf, yb, u, dd)
        dyf, _, du, ddd = vjp(ct)
        return dyf, du, ddd

    dys5, du_direct, g['s5_d'] = _pw(act_bwd, [sv['ysf'], sv['ysb'], sv['zs'], w['s5_d'], dgact],
                                     [_row(T, D)] * 3 + [_par(D), _row(T, D)],
                                     [SDS((S, D), bf16), SDS((S, D), f32), SDS((1, D), f32)],
                                     [_row(T, D), _row(T, D), _par(D)], (1, I), n_acc=1, name=nm("s5_act"))
    s5 = w['s5']
    rf = _s5_scan_bwd(sv['zs'], dys5, sv['xrf'], sv['xif'], s5['BBr'], s5['BBi'], s5['CCr'], s5['CCi'], s5['abr'], s5['abi'],
                      s5['pwr_a'], s5['pwi_a'], reverse=False, name=nm("s5_f"))
    rb = _s5_scan_bwd(sv['zs'], dys5, sv['xrb'], sv['xib'], s5['BBr'], s5['BBi'], s5['CCr'], s5['CCi'], s5['abr'], s5['abi'],
                      s5['pwr_a'], s5['pwi_a'], reverse=True, name=nm("s5_b"))
    g['s5'] = (rf[1:], rb[1:])
    dzs_seg = _pw(lambda a, b, c: a + b + c, [du_direct, rf[0], rb[0]], [_row(T, D)] * 3, [SDS((S, D), bf16)], [_row(T, D)],
                  (1, I), name=nm("s5_du"))[0]
    dzs = _perm_rows(dzs_seg, tabs['seg_unperm'], name=nm("s5_unperm_dz"))

    delta = _attn_delta(dys[1], sv['O'], name=nm("mla_delta"))
    dQ, dK, dV = _flash_bwd(sv['Q'], sv['K'], sv['kv'], delta, sv['Lse'], dys[1], name=nm("mla_attn"))
    dq_lin, dkv, dkr = _mla_bwd_prep(dQ, dK, dV, tabs['cos_m'], tabs['sin_m'], name=nm("mla_prep"))
    g['W_uq'] = _mmT(sv['cqn'], dq_lin, name=nm("mla_uq_dw"))
    dcqn = _mm(dq_lin, w['W_uq'], tb=True, name=nm("mla_uq_dx"))
    g['W_ukv'] = _mmT(sv['ckvn'], dkv, name=nm("mla_ukv_dw"))
    dckvn = _mm(dkv, w['W_ukv'], tb=True, name=nm("mla_ukv_dx"))
    dzm, g['mla_q_norm_g'], g['mla_kv_norm_g'] = _mla_norm_bwd(sv['zm'], w['mla_q_norm_g'], w['mla_kv_norm_g'], dcqn, dckvn, dkr,
                                                               name=nm("mla_norm"))

    zr = sv['zr']

    def gn_bwd(yf, yb, gt, gn, ct):
        _, vjp = jax.vjp(_gn_gate, yf, yb, gt, gn)
        dyf, _, dgt, dgn = vjp(ct)
        return dyf, dgt, dgn

    dyr, dgate, g['ret_gn_g'] = _pw(gn_bwd, [sv['yf'], sv['yb'], zr, w['ret_gn_g'], dys[0]],
                                    [_row(T, 256, hd), _row(T, 256, hd), _row(T, 256, lambda j: 8 + j), _par(256, hd),
                                     _row(T, 256, hd)],
                                    [SDS((S, 1024), bf16), SDS((S, 1024), bf16), SDS((1, 1024), f32)],
                                    [_row(T, 256, hd), _row(T, 256, hd), _par(256, hd)], (RET_HEADS, I), n_acc=1, name=nm("ret_gn"))
    qf, kf, vf, lgf = _ret_dir_bwd(zr, w['lg'], tabs['cos_r'], tabs['sin_r'], dyr, sv['stf'], reverse=False, name=nm("ret_f"))
    qb, kb, vb, lgb = _ret_dir_bwd(zr, w['lg'], tabs['cos_r'], tabs['sin_r'], dyr, sv['stb'], reverse=True, name=nm("ret_b"))
    g['lg'] = jnp.stack([lgf[:, 0, 0], lgb[:, 0, 0]])
    dzr = _pw(lambda a, b, c, d, e, f, gt: jnp.concatenate([a + b, c + d, e + f, gt], axis=1),
              [qf, qb, kf, kb, vf, vb, dgate], [_row(256, 512)] * 4 + [_row(256, D)] * 3,
              [SDS((S, 3072), bf16)], [_row(256, 3072)], (1, S // 256), name=nm("ret_dz"))[0]

    h = sv['h']
    g['W_ret'] = _mmT(h, dzr, name=nm("in_ret_dw"))
    g['W_mla'] = _mmT(h, dzm, name=nm("in_mla_dw"))
    g['W_s5'] = _mmT(sv['h_seg'], dzs_seg, name=nm("in_s5_dw"))
    g['W_gate'] = _mmT(h, dzg, name=nm("in_gate_dw"))
    dh = _mm(dzr, w['W_ret'], tb=True, name=nm("in_ret_dx"))
    dh = _mm(dzm, w['W_mla'], tb=True, res=dh, name=nm("in_mla_dx"))
    dh = _mm(dzs, w['W_s5'], tb=True, res=dh, name=nm("in_s5_dx"))
    dh = _mm(dzg, w['W_gate'], tb=True, res=dh, name=nm("in_gate_dx"))
    dx, g['norm1_g'] = _rmsnorm_bwd(sv['x'], w['norm1_g'], dh, dx1, name=nm("norm1"))
    return dx, g


def _loss_head(x, tgt, gain, *, name):
    S, W = x.shape
    T = min(S, 512)

    def loss_fn(xv, gv, tv):
        return 0.5 * jnp.sum(jnp.mean(jnp.square(_rms(xv, gv) - tv), axis=-1, keepdims=True), axis=0, keepdims=True)

    def fn(xv, gv, tv):
        lv, vjp = jax.vjp(lambda a, b: loss_fn(a, b, tv), xv, gv)
        dx, dg = vjp(jnp.ones((1, 1), f32))
        return dx, jnp.broadcast_to(lv, (1, 128)), dg

    return _pw(fn, [x, gain, tgt], [_row(T, W), _par(W), _row(T, W)],
               [SDS((S, W), f32), SDS((1, 128), f32), SDS((1, W), f32)], [_row(T, W), _par(128), _par(W)],
               (1, S // T), n_acc=2, name=name)


def _rope_tabs(S):
    def tab(dim):
        inv = 1.0 / (ROPE_THETA ** (jnp.arange(0, dim, 2, dtype=f32) / dim))
        ang = jnp.arange(S, dtype=f32)[:, None] * inv[None, :]
        return jnp.cos(ang), jnp.sin(ang)

    cr, sr = tab(RET_DK)
    cm, sm = tab(MLA_ROPE)
    z = jnp.zeros((S, 64), f32)
    return {'cos_r': jnp.concatenate([cr, cr], axis=1), 'sin_r': jnp.concatenate([-sr, sr], axis=1),
            'cos_m': jnp.concatenate([cm, cm, z], axis=1), 'sin_m': jnp.concatenate([-sm, sm, z], axis=1),
            'seg_perm': _seg_perm(S5_T, False), 'seg_unperm': _seg_perm(S5_T, True)}


def _bd_B(bb):
    b5 = bb.reshape(16, 2, 8, 8, 64)
    return jnp.einsum('cdjgp,gh->djgchp', b5, jnp.eye(8, dtype=bb.dtype)).reshape(2, 8, 128, 512)


def _bd_B_t(dBB):
    return jnp.einsum('djgcgp->cdjgp', dBB.reshape(2, 8, 8, 16, 8, 64)).reshape(16, 8192)


def _bd_C(c):
    c5 = c.reshape(2, 8, 8, 16, 64)
    return jnp.einsum('djgcp,gh->djgphc', c5, jnp.eye(8, dtype=c.dtype)).reshape(2, 8, 512, 128)


def _s5_rows(p, l):
    a_re = p['s5_a_re'][l].reshape(1, 8192)
    a_im = p['s5_a_im'][l].reshape(1, 8192)
    ldt = jnp.broadcast_to(p['s5_log_dt'][l][:, :, None], (2, S5_G, S5_P)).reshape(1, 8192)
    b_re = p['s5_b_re'][l].transpose(3, 0, 1, 2).reshape(16, 8192)
    b_im = p['s5_b_im'][l].transpose(3, 0, 1, 2).reshape(16, 8192)
    return a_re, a_im, ldt, b_re, b_im


def _layer_weights(big, p, l):
    w_in = big['w_in'][l]
    z = lambda n: jnp.zeros((D, n), w_in.dtype)
    w = {
        'W_ret': w_in[:, 0:3072],
        'W_mla': jnp.concatenate([w_in[:, 3072:3456], z(128), w_in[:, 3456:3712], w_in[:, 3712:3776], z(64)], axis=1),
        'W_s5': w_in[:, 3776:4800],
        'W_gate': w_in[:, 4800:7872],
        'W_uq': jnp.pad(big['mla_w_uq'][l].reshape(MLA_Q_LORA, MLA_HEADS, 192), ((0, 0), (0, 0), (0, 64))).reshape(MLA_Q_LORA, 2048),
        'W_ukv': big['mla_w_ukv'][l],
        'W_glu': big['s5_w_glu'][l],
        'W_br': [big['w_branch'][l, i] for i in range(3)],
        'W_out': big['w_out'][l],
        'W_gu': big['ffn_w_gu'][l],
        'W_down': big['ffn_w_down'][l],
    }
    for n in ('norm1_g', 'ret_gn_g', 'mla_q_norm_g', 'mla_kv_norm_g', 's5_d', 'norm2_g'):
        w[n] = p[n][l][None, :]
    w['lg'] = jax.nn.log_sigmoid(p['ret_decay'][l])
    rows = _s5_rows(p, l)
    abr, abi, bbr, bbi, pwr, pwi = _s5_param_fwd(*rows, name=f"L{l}_s5_param")
    flip = lambda t, first: jnp.concatenate([t[::-1, :4096], t[:, 4096:]] if first else [t[:, :4096], t[::-1, 4096:]], axis=1)
    w['s5'] = {'abr': abr, 'abi': abi, 'BBr': _bd_B(bbr).astype(bf16), 'BBi': _bd_B(bbi).astype(bf16),
               'CCr': _bd_C(p['s5_c_re'][l]).astype(bf16), 'CCi': _bd_C(p['s5_c_im'][l]).astype(bf16),
               'pwr_f': flip(pwr, False), 'pwi_f': flip(pwi, False), 'pwr_a': flip(pwr, True), 'pwi_a': flip(pwi, True),
               'rows': rows}
    return w


def _layer_grads(g, w, p, l):
    out = {}
    m = g['W_mla']
    out['w_in'] = jnp.concatenate([g['W_ret'], m[:, 0:384], m[:, 512:768], m[:, 768:832], g['W_s5'], g['W_gate']], axis=1)
    out['mla_w_uq'] = g['W_uq'].reshape(MLA_Q_LORA, MLA_HEADS, 256)[:, :, :192].reshape(MLA_Q_LORA, 1536)
    out['mla_w_ukv'] = g['W_ukv']
    out['s5_w_glu'] = g['W_glu']
    out['w_branch'] = jnp.stack(g['W_br'])
    out['w_out'] = g['W_out']
    out['ffn_w_gu'] = g['W_gu']
    out['ffn_w_down'] = g['W_down']
    for n in ('norm1_g', 'ret_gn_g', 'mla_q_norm_g', 'mla_kv_norm_g', 's5_d', 'norm2_g'):
        out[n] = g[n][0]
    out['ret_decay'] = g['lg'] * jax.nn.sigmoid(-p['ret_decay'][l])
    (fB_r, fB_i, fC_r, fC_i, fa_r, fa_i), (bB_r, bB_i, bC_r, bC_i, ba_r, ba_i) = g['s5']
    cat = lambda a, b: jnp.concatenate([a, b], axis=0)
    d_bbr = _bd_B_t(cat(fB_r, bB_r))
    d_bbi = _bd_B_t(cat(fB_i, bB_i))
    to_c = lambda t: _bd_B_t(t).reshape(16, 2, S5_G, S5_P).transpose(1, 2, 0, 3)
    out['s5_c_re'] = to_c(cat(fC_r, bC_r))
    out['s5_c_im'] = to_c(cat(fC_i, bC_i))
    d_abr = jnp.concatenate([fa_r, ba_r], axis=1)
    d_abi = jnp.concatenate([fa_i, ba_i], axis=1)
    da_re, da_im, dldt, db_re, db_im = _s5_param_bwd(*w['s5']['rows'], d_abr, d_abi, d_bbr, d_bbi, name=f"L{l}_b_s5_param")
    out['s5_a_re'] = da_re.reshape(2, S5_G, S5_P)
    out['s5_a_im'] = da_im.reshape(2, S5_G, S5_P)
    out['s5_log_dt'] = dldt.reshape(2, S5_G, S5_P).sum(axis=-1)
    out['s5_b_re'] = db_re.reshape(16, 2, S5_G, S5_P).transpose(1, 2, 3, 0)
    out['s5_b_im'] = db_im.reshape(16, 2, S5_G, S5_P).transpose(1, 2, 3, 0)
    return out


def _local_step(x, tgt, big, p):
    S = x.shape[0]
    assert S % S5_T == 0
    tabs = _rope_tabs(S)
    ws, svs = [], []
    h = x
    for l in range(DEPTH):
        w = _layer_weights(big, p, l)
        h, sv = _layer_fwd(h, w, tabs, l)
        ws.append(w)
        svs.append(sv)
    dx, lossv, dfinal = _loss_head(h, tgt, p['final_g'][None, :], name="loss_head")
    per_layer = [None] * DEPTH
    for l in reversed(range(DEPTH)):
        dx, g = _layer_bwd(dx, ws[l], tabs, svs[l], l)
        per_layer[l] = _layer_grads(g, ws[l], p, l)
    return lossv[0, 0], dx, per_layer, dfinal[0]


_ANY = pl.BlockSpec(memory_space=pl.ANY)


def _place():
    x, y, c = lax.axis_index("x"), lax.axis_index("y"), lax.axis_index("c")
    return x, y, c, [(1 - x, y), (x, 1 - y), (1 - x, 1 - y)]


def _allgather4(arrs, *, name):
    n = len(arrs)

    def body(*refs):
        ins, outs = refs[:n], refs[n:2 * n]
        send, recv, loc = refs[2 * n:]
        x, y, c, chips = _place()
        me = 2 * x + y

        def remote(a, k, slot):
            px, py = chips[k]
            return pltpu.make_async_remote_copy(src_ref=ins[a], dst_ref=outs[a].at[slot], send_sem=send.at[a, k],
                                                recv_sem=recv.at[a, k], device_id=(px, py, c), device_id_type=MESH)

        mine = [pltpu.make_async_copy(ins[a], outs[a].at[me], loc.at[a]) for a in range(n)]
        for cp in mine:
            cp.start()
        sends = [remote(a, k, me) for a in range(n) for k in range(3)]
        for cp in sends:
            cp.start()
        for a in range(n):
            for k, (px, py) in enumerate(chips):
                remote(a, k, 2 * px + py).wait_recv()
        for cp in sends:
            cp.wait_send()
        for cp in mine:
            cp.wait()

    return pl.pallas_call(
        body, in_specs=[_ANY] * n, out_specs=[_ANY] * n, out_shape=[SDS((4,) + a.shape, a.dtype) for a in arrs],
        scratch_shapes=[pltpu.SemaphoreType.DMA((n, 3)), pltpu.SemaphoreType.DMA((n, 3)), pltpu.SemaphoreType.DMA((n,))],
        name=name)(*arrs)


def _rs_exchange(parts, *, name):
    n = len(parts)

    def body(*refs):
        ins, gots = refs[:n], refs[n:2 * n]
        send, recv = refs[2 * n:]
        x, y, c, chips = _place()

        def remote(a, k):
            px, py = chips[k]
            return pltpu.make_async_remote_copy(src_ref=ins[a].at[2 * px + py], dst_ref=gots[a].at[k], send_sem=send.at[a, k],
                                                recv_sem=recv.at[a, k], device_id=(px, py, c), device_id_type=MESH)

        sends = [remote(a, k) for a in range(n) for k in range(3)]
        for cp in sends:
            cp.start()
        for cp in sends:
            cp.wait_recv()
        for cp in sends:
            cp.wait_send()

    return pl.pallas_call(
        body, in_specs=[_ANY] * n, out_specs=[_ANY] * n, out_shape=[SDS((3,) + a.shape[1:], a.dtype) for a in parts],
        scratch_shapes=[pltpu.SemaphoreType.DMA((n, 3)), pltpu.SemaphoreType.DMA((n, 3))], name=name)(*parts)


def _gather_split(arrs, *, name):
    n = len(arrs)

    def body(*refs):
        ins, outs = refs[:n], refs[n:2 * n]
        s_ici, r_ici, s_sib, r_sib, loc = refs[2 * n:]
        x, y, c, chips = _place()
        me = 2 * x + y
        ids = [2 * px + py for px, py in chips] + [me]

        def over_ici(a, k, slot):
            px, py = chips[k]
            return pltpu.make_async_remote_copy(src_ref=ins[a].at[c], dst_ref=outs[a].at[slot, c], send_sem=s_ici.at[a, k],
                                                recv_sem=r_ici.at[a, k], device_id=(px, py, c), device_id_type=MESH)

        def to_sibling(a, k, half, src=None):
            blk = outs[a].at[ids[k], half]
            return pltpu.make_async_remote_copy(src_ref=blk if src is None else src, dst_ref=blk, send_sem=s_sib.at[a, k],
                                                recv_sem=r_sib.at[a, k], device_id=(x, y, 1 - c), device_id_type=MESH)

        sends = [over_ici(a, k, me) for a in range(n) for k in range(3)]
        sends += [to_sibling(a, 3, c, src=ins[a].at[c]) for a in range(n)]
        for cp in sends:
            cp.start()
        mine = [pltpu.make_async_copy(ins[a].at[c], outs[a].at[me, c], loc.at[a]) for a in range(n)]
        for cp in mine:
            cp.start()
        for a in range(n):
            for k in range(3):
                over_ici(a, k, ids[k]).wait_recv()
                fwd = to_sibling(a, k, c)
                fwd.start()
                sends.append(fwd)
        for a in range(n):
            for k in range(4):
                to_sibling(a, k, 1 - c).wait_recv()
        for cp in sends:
            cp.wait_send()
        for cp in mine:
            cp.wait()

    dma = pltpu.SemaphoreType.DMA
    return pl.pallas_call(
        body, in_specs=[_ANY] * n, out_specs=[_ANY] * n, out_shape=[SDS((4,) + a.shape, a.dtype) for a in arrs],
        scratch_shapes=[dma((n, 3)), dma((n, 3)), dma((n, 4)), dma((n, 4)), dma((n,))], name=name)(*arrs)


def _swap_halves(parts, *, name):
    n = len(parts)

    def body(*refs):
        ins, gots = refs[:n], refs[n:2 * n]
        send, recv = refs[2 * n:]
        x, y, c, _ = _place()
        cps = [pltpu.make_async_remote_copy(src_ref=ins[a].at[q, 1 - c], dst_ref=gots[a].at[q], send_sem=send.at[a, q],
                                            recv_sem=recv.at[a, q], device_id=(x, y, 1 - c), device_id_type=MESH)
               for a in range(n) for q in range(4)]
        for cp in cps:
            cp.start()
        for cp in cps:
            cp.wait_recv()
        for cp in cps:
            cp.wait_send()

    dma = pltpu.SemaphoreType.DMA
    return pl.pallas_call(
        body, in_specs=[_ANY] * n, out_specs=[_ANY] * n, out_shape=[SDS((4,) + a.shape[2:], a.dtype) for a in parts],
        scratch_shapes=[dma((n, 4)), dma((n, 4))], name=name)(*parts)


def _sibling_copy(arrs, *, name):
    n = len(arrs)

    def body(*refs):
        ins, outs = refs[:n], refs[n:2 * n]
        send, recv = refs[2 * n:]
        x, y, c, _ = _place()
        cps = [pltpu.make_async_remote_copy(src_ref=ins[a], dst_ref=outs[a], send_sem=send.at[a], recv_sem=recv.at[a],
                                            device_id=(x, y, 1 - c), device_id_type=MESH) for a in range(n)]
        for cp in cps:
            cp.start()
        for cp in cps:
            cp.wait_recv()
        for cp in cps:
            cp.wait_send()

    dma = pltpu.SemaphoreType.DMA
    return pl.pallas_call(
        body, in_specs=[_ANY] * n, out_specs=[_ANY] * n, out_shape=[SDS(a.shape, a.dtype) for a in arrs],
        scratch_shapes=[dma((n,)), dma((n,))], name=name)(*arrs)


def _row_tile(R):
    return R if R <= 256 else next(t for t in (256, 128, 64, 32, 16) if R % t == 0)


def _add2(a, b, *, name):
    R, W = a.shape
    tr = _row_tile(R)
    return _pw(lambda p, q: p.astype(f32) + q.astype(f32), [a, b], [_row(tr, W)] * 2, [SDS((R, W), a.dtype)], [_row(tr, W)],
               (1, R // tr), name=name)[0]


def _sum4(own, got, *, name):
    R, W = own.shape
    tr = _row_tile(R)
    g3 = lambda k: pl.BlockSpec((None, tr, W), lambda j, i: (k, i, 0))
    up = lambda t: t.astype(f32)
    return _pw(lambda a, b, c, d: ((up(a) + up(b)) + up(c)) + up(d), [own, got, got, got], [_row(tr, W), g3(0), g3(1), g3(2)],
               [SDS((R, W), f32)], [_row(tr, W)], (1, R // tr), name=name)[0]


def _adamw(g, w, m, v, *, name):
    R, W = w.shape
    tr = _row_tile(R)

    def fn(gv, wv, mv, vv):
        m2 = ADAM_B1 * mv + (1.0 - ADAM_B1) * gv
        v2 = ADAM_B2 * vv + (1.0 - ADAM_B2) * jnp.square(gv)
        m_hat = m2 / (1.0 - ADAM_B1 ** ADAM_STEP)
        v_hat = v2 / (1.0 - ADAM_B2 ** ADAM_STEP)
        return -ADAM_LR * (m_hat / (jnp.sqrt(v_hat) + ADAM_EPS) + ADAM_WD * wv), m2, v2

    return _pw(fn, [g, w, m, v], [_row(tr, W)] * 4, [SDS((R, W), f32)] * 3, [_row(tr, W)] * 3, (1, R // tr), name=name)


def _to_parts(g, axis):
    shp = g.shape
    g = g.reshape(shp[:axis] + (4, shp[axis] // 4) + shp[axis + 1:])
    return jnp.moveaxis(g, axis, 0)


def _from_parts(pt, axis):
    g = jnp.moveaxis(pt, 0, axis)
    shp = g.shape
    return g.reshape(shp[:axis] + (4 * shp[axis + 1],) + shp[axis + 2:])


def kernel(x, norm1_g, w_in, ret_decay, ret_gn_g, mla_q_norm_g, mla_w_uq, mla_kv_norm_g, mla_w_ukv, s5_a_re, s5_a_im, s5_log_dt, s5_b_re, s5_b_im, s5_c_re, s5_c_im, s5_d, s5_w_glu, w_branch, w_out, norm2_g, ffn_w_gu, ffn_w_down, final_g, loss_target, m_norm1_g, m_w_in, m_ret_decay, m_ret_gn_g, m_mla_q_norm_g, m_mla_w_uq, m_mla_kv_norm_g, m_mla_w_ukv, m_s5_a_re, m_s5_a_im, m_s5_log_dt, m_s5_b_re, m_s5_b_im, m_s5_c_re, m_s5_c_im, m_s5_d, m_s5_w_glu, m_w_branch, m_w_out, m_norm2_g, m_ffn_w_gu, m_ffn_w_down, m_final_g, v_norm1_g, v_w_in, v_ret_decay, v_ret_gn_g, v_mla_q_norm_g, v_mla_w_uq, v_mla_kv_norm_g, v_mla_w_ukv, v_s5_a_re, v_s5_a_im, v_s5_log_dt, v_s5_b_re, v_s5_b_im, v_s5_c_re, v_s5_c_im, v_s5_d, v_s5_w_glu, v_w_branch, v_w_out, v_norm2_g, v_ffn_w_gu, v_ffn_w_down, v_final_g):
    wv = dict(zip(W_NAMES, (norm1_g, w_in, ret_decay, ret_gn_g, mla_q_norm_g, mla_w_uq, mla_kv_norm_g, mla_w_ukv, s5_a_re, s5_a_im,
                            s5_log_dt, s5_b_re, s5_b_im, s5_c_re, s5_c_im, s5_d, s5_w_glu, w_branch, w_out, norm2_g, ffn_w_gu,
                            ffn_w_down, final_g)))
    mv = dict(zip(W_NAMES, (m_norm1_g, m_w_in, m_ret_decay, m_ret_gn_g, m_mla_q_norm_g, m_mla_w_uq, m_mla_kv_norm_g, m_mla_w_ukv,
                            m_s5_a_re, m_s5_a_im, m_s5_log_dt, m_s5_b_re, m_s5_b_im, m_s5_c_re, m_s5_c_im, m_s5_d, m_s5_w_glu,
                            m_w_branch, m_w_out, m_norm2_g, m_ffn_w_gu, m_ffn_w_down, m_final_g)))
    vv = dict(zip(W_NAMES, (v_norm1_g, v_w_in, v_ret_decay, v_ret_gn_g, v_mla_q_norm_g, v_mla_w_uq, v_mla_kv_norm_g, v_mla_w_ukv,
                            v_s5_a_re, v_s5_a_im, v_s5_log_dt, v_s5_b_re, v_s5_b_im, v_s5_c_re, v_s5_c_im, v_s5_d, v_s5_w_glu,
                            v_w_branch, v_w_out, v_norm2_g, v_ffn_w_gu, v_ffn_w_down, v_final_g)))
    big_names = list(BIG)

    my_c = lax.axis_index("c")
    my_chip = 2 * lax.axis_index("x") + lax.axis_index("y")
    shards = [wv[n].astype(bf16) for n in big_names]
    gathered = _gather_split(shards, name="gather_weights")
    big = {n: _from_parts(gt, BIG[n]) for n, gt in zip(big_names, gathered)}
    small = {n: wv[n] for n in SMALL}

    loss_local, dx, layer_grads, d_final = _local_step(x[0], loss_target[0], big, small)
    grads = {n: jnp.stack([layer_grads[l][n] for l in range(DEPTH)]) for n in SMALL if n != 'final_g'}
    grads['final_g'] = d_final

    n_rows = {n: -(-math.prod(wv[n].shape) // 1024) * 8 for n in SMALL}
    used = sum(n_rows.values())
    rows_q = -(-(used + 8) // (4 * 128)) * 128

    def as_rows(d, tail=None):
        blocks = [jnp.pad(d[n].reshape(-1), (0, n_rows[n] * 128 - math.prod(wv[n].shape))).reshape(n_rows[n], 128) for n in SMALL]
        blocks.append(jnp.zeros((8, 128), f32) if tail is None else tail)
        blocks.append(jnp.zeros((4 * rows_q - used - 8, 128), f32))
        return jnp.concatenate(blocks, axis=0)

    loss_rows = jnp.full((8, 128), loss_local, f32)
    parts = [jnp.stack([_to_parts(layer_grads[l][n].astype(bf16), BIG[n] - 1) for l in range(DEPTH)], axis=1) for n in big_names]
    parts.append(as_rows(grads, loss_rows).reshape(4, 2, rows_q // 2, 128))
    n_arr = len(parts)
    two_d = lambda a: a.reshape(-1, a.shape[-1])
    theirs = _swap_halves(parts, name="grad_swap_halves")
    mine = [jnp.where(my_c == 0, p[:, 0], p[:, 1]) for p in parts]
    chip_sums = [_add2(two_d(mine[a]), two_d(theirs[a]), name=f"grad_add2_{a}").reshape(theirs[a].shape) for a in range(n_arr)]
    got = _rs_exchange(chip_sums, name="grad_exchange")

    def pick_chip(s):
        r = s[0]
        for q in range(1, 4):
            r = jnp.where(my_chip == q, s[q], r)
        return r

    own = [pick_chip(s) for s in chip_sums]
    sums = [_sum4(two_d(own[a]), got[a].reshape(3, -1, got[a].shape[-1]), name=f"grad_sum4_{a}") for a in range(n_arr)]
    other = _sibling_copy(sums, name="grad_sibling")
    full = [jnp.stack([jnp.where(my_c == 0, sums[a], other[a]), jnp.where(my_c == 0, other[a], sums[a])]) for a in range(n_arr)]

    out_g, out_d, out_m, out_v = {}, {}, {}, {}
    for a, n in enumerate(big_names):
        shp = wv[n].shape
        res = _adamw(two_d(full[a]), two_d(wv[n]), two_d(mv[n]), two_d(vv[n]), name=f"adamw_{n}")
        out_g[n] = full[a].reshape(shp)
        out_d[n], out_m[n], out_v[n] = [r.reshape(shp) for r in res]
    g_small = _allgather4([full[-1].reshape(rows_q, 128)], name="gather_small_grads")[0].reshape(4 * rows_q, 128)
    loss = g_small[used, 0]
    off = 0
    for n in SMALL:
        shp = wv[n].shape
        k = math.prod(shp)
        flat2 = (k // 128, 128) if k % 128 == 0 else (1, k)
        g_n = g_small[off:off + n_rows[n]].reshape(-1)[:k].reshape(flat2)
        res = _adamw(g_n, wv[n].reshape(flat2), mv[n].reshape(flat2), vv[n].reshape(flat2), name=f"adamw_{n}")
        out_g[n] = g_n.reshape(shp)
        out_d[n], out_m[n], out_v[n] = [r.reshape(shp) for r in res]
        off += n_rows[n]
    return (loss, dx[None], *[out_g[n] for n in W_NAMES], *[out_d[n] for n in W_NAMES], *[out_m[n] for n in W_NAMES],
            *[out_v[n] for n in W_NAMES])
```

```python
import functools
import math

import jax
import jax.numpy as jnp
from jax import lax
from jax.experimental import pallas as pl
from jax.experimental.pallas import tpu as pltpu

f32 = jnp.float32
bf16 = jnp.bfloat16
SDS = jax.ShapeDtypeStruct
MESH = pl.DeviceIdType.MESH

D = 1024
DEPTH = 2
RMS_EPS = 1e-6
GN_EPS = 1e-5
ROPE_THETA = 10000.0
RET_HEADS = 4
RET_DK = 128
RET_DV = 256
RET_CHUNK = 128
MLA_HEADS = 8
MLA_Q_LORA = 384
MLA_KV_LORA = 256
MLA_NOPE = 128
MLA_ROPE = 64
MLA_V = 128
MLA_QW = 256
S5_G = 64
S5_P = 64
S5_C = 16
S5_NJ = 8
S5_SEG = 8
S5_T = 1024
FFN_H = 2816
ADAM_LR = 0.001
ADAM_B1 = 0.9
ADAM_B2 = 0.999
ADAM_EPS = 1e-08
ADAM_WD = 0.01
ADAM_STEP = 10
VMEM_BIG = 56 * 1024 * 1024

W_NAMES = ['norm1_g', 'w_in', 'ret_decay', 'ret_gn_g', 'mla_q_norm_g', 'mla_w_uq', 'mla_kv_norm_g', 'mla_w_ukv',
           's5_a_re', 's5_a_im', 's5_log_dt', 's5_b_re', 's5_b_im', 's5_c_re', 's5_c_im', 's5_d', 's5_w_glu',
           'w_branch', 'w_out', 'norm2_g', 'ffn_w_gu', 'ffn_w_down', 'final_g']
BIG = {'w_in': 2, 'mla_w_uq': 2, 'mla_w_ukv': 2, 's5_w_glu': 2, 'w_branch': 2, 'w_out': 1, 'ffn_w_gu': 2, 'ffn_w_down': 1}
SMALL = [n for n in W_NAMES if n not in BIG]


TILE_BYTES = 6 * 1024 * 1024


def _pick(n, cands=(512, 384, 256, 128), cap=None):
    if n <= 1024 and (cap is None or n <= cap):
        return n
    for c in cands:
        if n % c == 0 and (cap is None or c <= cap):
            return c
    raise ValueError(n)


WIDE = (1408, 1024, 768, 512, 384, 256, 128)


def _params(sem, vmem=None):
    return pltpu.CompilerParams(dimension_semantics=sem, vmem_limit_bytes=vmem)


def _mm(a, b, *, tb=False, res=None, out_dtype=f32, name):
    M, K = a.shape
    N = b.shape[0] if tb else b.shape[1]
    tk = K if K <= 3072 else _pick(K, (1408, 1024, 512))
    nk = K // tk
    tn = _pick(N, WIDE, cap=TILE_BYTES // (tk * b.dtype.itemsize))
    tm = _pick(M)
    if M % 1024 == 0 and 1024 * tk * a.dtype.itemsize <= 4 * 1024 * 1024 and 1024 * tn * 4 <= TILE_BYTES:
        tm = 1024
    assert M % tm == 0 and N % tn == 0 and K % tk == 0

    def body(*refs):
        if res is None:
            a_ref, b_ref, o_ref, acc = refs
        else:
            a_ref, b_ref, r_ref, o_ref, acc = refs
        k = pl.program_id(2)
        dn = (((1,), (1 if tb else 0,)), ((), ()))
        part = lax.dot_general(a_ref[...].astype(bf16), b_ref[...].astype(bf16), dn, preferred_element_type=f32)

        @pl.when(k == 0)
        def _():
            acc[...] = part

        @pl.when(k > 0)
        def _():
            acc[...] += part

        @pl.when(k == nk - 1)
        def _():
            v = acc[...]
            if res is not None:
                v = v + r_ref[...]
            o_ref[...] = v.astype(out_dtype)

    in_specs = [pl.BlockSpec((tm, tk), lambda i, j, k: (i, k)),
                pl.BlockSpec((tn, tk), lambda i, j, k: (j, k)) if tb else pl.BlockSpec((tk, tn), lambda i, j, k: (k, j))]
    args = [a, b]
    if res is not None:
        in_specs.append(pl.BlockSpec((tm, tn), lambda i, j, k: (i, j)))
        args.append(res)
    return pl.pallas_call(
        body, grid=(M // tm, N // tn, nk), in_specs=in_specs,
        out_specs=pl.BlockSpec((tm, tn), lambda i, j, k: (i, j)),
        out_shape=SDS((M, N), out_dtype), scratch_shapes=[pltpu.VMEM((tm, tn), f32)],
        compiler_params=_params(("parallel", "parallel", "arbitrary"), VMEM_BIG), name=name)(*args)


def _mmT(a, b, *, name):
    S, M = a.shape
    N = b.shape[1]
    tn = _pick(N, WIDE)
    tm = _pick(M, WIDE, cap=TILE_BYTES // (tn * 4))
    tk = min(S, 1024)
    nk = S // tk

    def body(a_ref, b_ref, o_ref):
        k = pl.program_id(2)
        part = lax.dot_general(a_ref[...].astype(bf16), b_ref[...].astype(bf16), (((0,), (0,)), ((), ())),
                               preferred_element_type=f32)

        @pl.when(k == 0)
        def _():
            o_ref[...] = part

        @pl.when(k > 0)
        def _():
            o_ref[...] += part

    return pl.pallas_call(
        body, grid=(M // tm, N // tn, nk),
        in_specs=[pl.BlockSpec((tk, tm), lambda i, j, k: (k, i)), pl.BlockSpec((tk, tn), lambda i, j, k: (k, j))],
        out_specs=pl.BlockSpec((tm, tn), lambda i, j, k: (i, j)),
        out_shape=SDS((M, N), f32),
        compiler_params=_params(("parallel", "parallel", "arbitrary"), VMEM_BIG), name=name)(a, b)


def _pw(fn, ins, in_specs, outs, out_specs, grid, *, n_acc=0, name):
    n_in = len(ins)
    n_out = len(outs)

    def body(*refs):
        vals = fn(*[r[...].astype(f32) if r.dtype == bf16 else r[...] for r in refs[:n_in]])
        if not isinstance(vals, (tuple, list)):
            vals = (vals,)
        orefs = refs[n_in:]
        for r, v in zip(orefs[:n_out - n_acc], vals[:n_out - n_acc]):
            r[...] = v.astype(r.dtype)
        if n_acc:
            i = pl.program_id(1)

            @pl.when(i == 0)
            def _():
                for r, v in zip(orefs[n_out - n_acc:], vals[n_out - n_acc:]):
                    r[...] = v.astype(r.dtype)

            @pl.when(i > 0)
            def _():
                for r, v in zip(orefs[n_out - n_acc:], vals[n_out - n_acc:]):
                    r[...] += v.astype(r.dtype)

    res = pl.pallas_call(
        body, grid=grid, in_specs=in_specs, out_specs=out_specs, out_shape=outs,
        compiler_params=_params(("parallel", "arbitrary"), VMEM_BIG), name=name)(*ins)
    return res


def _row(T, w, col=None):
    if col is None:
        return pl.BlockSpec((T, w), lambda j, i: (i, 0))
    return pl.BlockSpec((T, w), lambda j, i: (i, col(j)))


def _par(w, col=None):
    if col is None:
        return pl.BlockSpec((1, w), lambda j, i: (0, 0))
    return pl.BlockSpec((1, w), lambda j, i: (0, col(j)))


def _rms(x, g):
    return x * lax.rsqrt(jnp.mean(x * x, axis=-1, keepdims=True) + RMS_EPS) * g


def _rope(x, cos, sinm, half):
    if half == 64:
        partner = pltpu.roll(x, 64, axis=1)
    else:
        lane = lax.broadcasted_iota(jnp.int32, x.shape, 1)
        partner = jnp.where((lane % (2 * half)) < half, pltpu.roll(x, 128 - half, axis=1), pltpu.roll(x, half, axis=1))
    return x * cos + partner * sinm


def _rope_t(x, cos, sinm, half):
    return _rope(x, cos, -sinm, half)


def _rmsnorm_fwd(x, g, *, name):
    S, W = x.shape
    T = min(S, 1024)
    return _pw(lambda xv, gv: _rms(xv, gv), [x, g], [_row(T, W), _par(W)], [SDS((S, W), bf16)], [_row(T, W)],
               (1, S // T), name=name)[0]


def _rmsnorm_bwd(x, g, dh, dres, *, name):
    S, W = x.shape
    T = min(S, 512)

    def fn(xv, gv, dhv, drv):
        _, vjp = jax.vjp(_rms, xv, gv)
        dx, dg = vjp(dhv)
        return dx + drv, dg

    return _pw(fn, [x, g, dh, dres], [_row(T, W), _par(W), _row(T, W), _row(T, W)],
               [SDS((S, W), f32), SDS((1, W), f32)], [_row(T, W), _par(W)], (1, S // T), n_acc=1, name=name)


def _ret_tables(lg, reverse):
    C = RET_CHUNK
    ii = lax.broadcasted_iota(jnp.int32, (C, C), 0).astype(f32)
    jj = lax.broadcasted_iota(jnp.int32, (C, C), 1).astype(f32)
    if not reverse:
        E = ii - jj
        mask = E >= 0
        eq = ii + 1.0
        ek = (C - 1.0) - ii
    else:
        E = jj - ii
        mask = E > 0
        eq = C - ii
        ek = ii
    Dm = jnp.where(mask, jnp.exp(jnp.where(mask, E, 0.0) * lg), 0.0)
    Em = jnp.where(mask, E, 0.0)
    qw = jnp.exp(eq * lg)
    kw = jnp.exp(ek * lg)
    qw2 = jnp.concatenate([qw, qw], axis=1)
    return Dm, Em, eq, ek, qw, kw, qw2, jnp.exp(C * lg)


def _dot(a, b, dims):
    return lax.dot_general(a.astype(bf16), b.astype(bf16), (dims, ((), ())), preferred_element_type=f32)


NN = ((1,), (0,))
NT = ((1,), (1,))
TN = ((0,), (0,))


def _ret_dir_fwd(zr, lg, cos, sinm, *, reverse, name):
    S = zr.shape[0]
    C = RET_CHUNK
    TB = min(S, 1024)
    nc = TB // C
    NB = S // TB
    d = 1 if reverse else 0
    scale = RET_DK ** -0.5

    def tb(b):
        return (NB - 1 - b) if reverse else b

    def body(lg_ref, q_ref, k_ref, v_ref, cos_ref, sin_ref, y_ref, st_ref, state):
        h = pl.program_id(0)
        b = pl.program_id(1)

        @pl.when(b == 0)
        def _():
            state[...] = jnp.zeros_like(state)

        Dm, _, _, _, _, kw, qw2, gC = _ret_tables(lg_ref[d, h], reverse)
        order = range(nc - 1, -1, -1) if reverse else range(nc)
        for c in order:
            rows = pl.ds(c * C, C)
            q = _rope(q_ref[rows, :], cos_ref[rows, :], sin_ref[rows, :], 64) * scale
            k = _rope(k_ref[rows, :], cos_ref[rows, :], sin_ref[rows, :], 64)
            v = v_ref[rows, :]
            st = state[...]
            st_ref[0, c] = st
            s = _dot(q, k, NT) * Dm
            o = _dot(s, v, NN) + _dot(q, st, NN) * qw2
            y_ref[rows, :] = o
            state[...] = gC * st + _dot(k * kw, v, TN)

    return pl.pallas_call(
        body, grid=(RET_HEADS, NB),
        in_specs=[pl.BlockSpec(memory_space=pltpu.SMEM),
                  pl.BlockSpec((TB, 128), lambda h, b: (tb(b), h)),
                  pl.BlockSpec((TB, 128), lambda h, b: (tb(b), 4 + h)),
                  pl.BlockSpec((TB, 256), lambda h, b: (tb(b), 4 + h)),
                  pl.BlockSpec((TB, 128), lambda h, b: (tb(b), 0)),
                  pl.BlockSpec((TB, 128), lambda h, b: (tb(b), 0))],
        out_specs=[pl.BlockSpec((TB, 256), lambda h, b: (tb(b), h)),
                   pl.BlockSpec((1, nc, 128, 256), lambda h, b: (h, tb(b), 0, 0))],
        out_shape=[SDS((S, 1024), f32), SDS((RET_HEADS, S // C, 128, 256), f32)],
        scratch_shapes=[pltpu.VMEM((128, 256), f32)],
        compiler_params=_params(("parallel", "arbitrary")), name=name)(lg, zr, zr, zr, cos, sinm)


def _ret_dir_bwd(zr, lg, cos, sinm, dy, states, *, reverse, name):
    S = zr.shape[0]
    C = RET_CHUNK
    TB = min(S, 1024)
    nc = TB // C
    NB = S // TB
    d = 1 if reverse else 0
    scale = RET_DK ** -0.5

    def tb(b):
        return b if reverse else (NB - 1 - b)

    def body(lg_ref, q_ref, k_ref, v_ref, cos_ref, sin_ref, dy_ref, st_ref, dq_ref, dk_ref, dv_ref, dlg_ref, dstate):
        h = pl.program_id(0)
        b = pl.program_id(1)

        @pl.when(b == 0)
        def _():
            dstate[...] = jnp.zeros_like(dstate)
            dlg_ref[...] = jnp.zeros_like(dlg_ref)

        Dm, Em, eq, ek, qw, kw, qw2, gC = _ret_tables(lg_ref[d, h], reverse)
        order = range(nc) if reverse else range(nc - 1, -1, -1)
        dlg = jnp.zeros((), f32)
        for c in order:
            rows = pl.ds(c * C, C)
            cs, sn = cos_ref[rows, :], sin_ref[rows, :]
            q = _rope(q_ref[rows, :], cs, sn, 64) * scale
            k = _rope(k_ref[rows, :], cs, sn, 64)
            v = v_ref[rows, :]
            do = dy_ref[rows, :]
            st = st_ref[0, c]
            ds = dstate[...]
            p = _dot(q, k, NT)
            a = p * Dm
            dp = _dot(do, v, NT) * Dm
            dq_cross = _dot(do, st, NT) * qw
            dk_cross = _dot(v, ds, NT) * kw
            dq = _dot(dp, k, NN) + dq_cross
            dk = _dot(dp, q, TN) + dk_cross
            dv = _dot(a, do, TN) + _dot(k * kw, ds, NN)
            dlg = dlg + jnp.sum(dp * p * Em) + jnp.sum(dq_cross * q * eq) + jnp.sum(dk_cross * k * ek) \
                + C * gC * jnp.sum(ds * st)
            dstate[...] = gC * ds + _dot(q * qw, do, TN)
            dq_ref[rows, :] = _rope_t(dq, cs, sn, 64) * scale
            dk_ref[rows, :] = _rope_t(dk, cs, sn, 64)
            dv_ref[rows, :] = dv
        dlg_ref[...] += jnp.full(dlg_ref.shape, dlg, f32)

    return pl.pallas_call(
        body, grid=(RET_HEADS, NB),
        in_specs=[pl.BlockSpec(memory_space=pltpu.SMEM),
                  pl.BlockSpec((TB, 128), lambda h, b: (tb(b), h)),
                  pl.BlockSpec((TB, 128), lambda h, b: (tb(b), 4 + h)),
                  pl.BlockSpec((TB, 256), lambda h, b: (tb(b), 4 + h)),
                  pl.BlockSpec((TB, 128), lambda h, b: (tb(b), 0)),
                  pl.BlockSpec((TB, 128), lambda h, b: (tb(b), 0)),
                  pl.BlockSpec((TB, 256), lambda h, b: (tb(b), h)),
                  pl.BlockSpec((1, nc, 128, 256), lambda h, b: (h, tb(b), 0, 0))],
        out_specs=[pl.BlockSpec((TB, 128), lambda h, b: (tb(b), h)),
                   pl.BlockSpec((TB, 128), lambda h, b: (tb(b), h)),
                   pl.BlockSpec((TB, 256), lambda h, b: (tb(b), h)),
                   pl.BlockSpec((1, 1, 128), lambda h, b: (h, 0, 0))],
        out_shape=[SDS((S, 512), f32), SDS((S, 512), f32), SDS((S, 1024), f32), SDS((RET_HEADS, 1, 128), f32)],
        scratch_shapes=[pltpu.VMEM((128, 256), f32)],
        compiler_params=_params(("parallel", "arbitrary")), name=name)(lg, zr, zr, zr, cos, sinm, dy, states)


def _gn_gate(yf, yb, g, gn):
    y = yf + yb
    mu = jnp.mean(y, axis=-1, keepdims=True)
    var = jnp.mean(jnp.square(y - mu), axis=-1, keepdims=True)
    yn = (y - mu) * lax.rsqrt(var + GN_EPS)
    return jax.nn.silu(g) * (yn * gn)


def _flash_fwd(Q, K, kv, *, name):
    S = Q.shape[0]
    hq = min(S, 256)
    nh = 4 if S % 1024 == 0 else 1
    tq = nh * hq
    tk = min(S, 8192)
    nk = S // tk

    def body(q_ref, k_ref, v_ref, o_ref, l_ref, m_s, l_s, acc):
        kk = pl.program_id(2)

        @pl.when(kk == 0)
        def _():
            m_s[...] = jnp.full_like(m_s, -jnp.inf)
            l_s[...] = jnp.zeros_like(l_s)
            acc[...] = jnp.zeros_like(acc)

        k = k_ref[...]
        v = v_ref[...]
        sts = [lax.dot_general(k, q_ref[hf * hq:(hf + 1) * hq, :], (NT, ((), ())), preferred_element_type=f32)
               for hf in range(nh)]
        for hf in range(nh):
            st = sts[hf]
            m_prev = m_s[hf]
            m_new = jnp.maximum(m_prev, jnp.max(st, axis=0, keepdims=True))
            pt = jnp.exp2(st - m_new)
            alpha = jnp.exp2(m_prev - m_new)
            l_s[hf] = alpha * l_s[hf] + jnp.sum(pt, axis=0, keepdims=True)
            acc[hf] = alpha * acc[hf] + lax.dot_general(v, pt.astype(bf16), (TN, ((), ())), preferred_element_type=f32)
            m_s[hf] = m_new

        @pl.when(kk == nk - 1)
        def _():
            for hf in range(nh):
                o_ref[hf * hq:(hf + 1) * hq, :] = jnp.transpose(acc[hf] / l_s[hf]).astype(bf16)
                l_ref[0, :, hf * hq:(hf + 1) * hq] = m_s[hf] + jnp.log2(l_s[hf])

    return pl.pallas_call(
        body, grid=(MLA_HEADS, S // tq, nk),
        in_specs=[pl.BlockSpec((tq, 256), lambda h, i, k: (i, h)),
                  pl.BlockSpec((tk, 256), lambda h, i, k: (k, h)),
                  pl.BlockSpec((tk, 128), lambda h, i, k: (k, 2 * h + 1))],
        out_specs=[pl.BlockSpec((tq, 128), lambda h, i, k: (i, h)), pl.BlockSpec((1, 1, tq), lambda h, i, k: (h, 0, i))],
        out_shape=[SDS((S, 1024), bf16), SDS((MLA_HEADS, 1, S), f32)],
        scratch_shapes=[pltpu.VMEM((nh, 1, hq), f32), pltpu.VMEM((nh, 1, hq), f32), pltpu.VMEM((nh, 128, hq), f32)],
        compiler_params=_params(("parallel", "parallel", "arbitrary"), VMEM_BIG), name=name)(Q, K, kv)


def _attn_delta(dO, O, *, name):
    S = dO.shape[0]
    T = min(S, 512)

    def body(do_ref, o_ref, d_ref):
        ones = jnp.ones((8, 128), bf16)
        for h in range(MLA_HEADS):
            cols = slice(128 * h, 128 * h + 128)
            prod = do_ref[:, cols].astype(f32) * o_ref[:, cols].astype(f32)
            hi = prod.astype(bf16)
            lo = (prod - hi.astype(f32)).astype(bf16)
            row = lax.dot_general(ones, hi, (NT, ((), ())), preferred_element_type=f32) \
                + lax.dot_general(ones, lo, (NT, ((), ())), preferred_element_type=f32)
            d_ref[h] = row[0:1, :]

    return pl.pallas_call(
        body, grid=(S // T,),
        in_specs=[pl.BlockSpec((T, 1024), lambda i: (i, 0)), pl.BlockSpec((T, 1024), lambda i: (i, 0))],
        out_specs=pl.BlockSpec((MLA_HEADS, 1, T), lambda i: (0, 0, i)), out_shape=SDS((MLA_HEADS, 1, S), f32),
        compiler_params=_params(("parallel",)), name=name)(dO, O)


def _flash_bwd(Q, K, kv, delta, L, dO, *, name):
    S = Q.shape[0]
    hq = min(S, 512)
    nh = 2 if S % 1024 == 0 else 1
    tq = nh * hq
    tk = min(S, 2048)
    nq = S // tq
    ln2 = math.log(2.0)

    def body(q_ref, k_ref, v_ref, dl_ref, l_ref, do_ref, dq_ref, dk_ref, dv_ref, dk_acc, dv_acc):
        kk = pl.program_id(1)
        i = pl.program_id(2)

        @pl.when((kk == 0) & (i == 0))
        def _():
            dq_ref[...] = jnp.zeros_like(dq_ref)

        @pl.when(i == 0)
        def _():
            dk_acc[...] = jnp.zeros_like(dk_acc)
            dv_acc[...] = jnp.zeros_like(dv_acc)

        k = k_ref[...]
        v = v_ref[...]
        dk_new = dk_acc[...]
        dv_new = dv_acc[...]
        for hf in range(nh):
            sl = slice(hf * hq, (hf + 1) * hq)
            q = q_ref[sl, :]
            st = lax.dot_general(k, q, (NT, ((), ())), preferred_element_type=f32)
            pt = jnp.exp2(st - l_ref[0, :, sl])
            delta = dl_ref[0, :, sl]
            dob = do_ref[sl, :].astype(bf16)
            dv_new = dv_new + lax.dot_general(pt.astype(bf16), dob, (NN, ((), ())), preferred_element_type=f32)
            dpt = lax.dot_general(v, dob, (NT, ((), ())), preferred_element_type=f32)
            dst = (pt * (dpt - delta)).astype(bf16)
            dk_new = dk_new + lax.dot_general(dst, q, (NN, ((), ())), preferred_element_type=f32)
            dq_ref[0, i * nh + hf] += lax.dot_general(k, dst, (TN, ((), ())), preferred_element_type=f32)
        dk_acc[...] = dk_new
        dv_acc[...] = dv_new

        @pl.when(i == nq - 1)
        def _():
            dk_ref[...] = dk_acc[...] * ln2
            dv_ref[...] = dv_acc[...]

    return pl.pallas_call(
        body, grid=(MLA_HEADS, S // tk, nq),
        in_specs=[pl.BlockSpec((tq, 256), lambda h, k, i: (i, h)),
                  pl.BlockSpec((tk, 256), lambda h, k, i: (k, h)),
                  pl.BlockSpec((tk, 128), lambda h, k, i: (k, 2 * h + 1)),
                  pl.BlockSpec((1, 1, tq), lambda h, k, i: (h, 0, i)),
                  pl.BlockSpec((1, 1, tq), lambda h, k, i: (h, 0, i)),
                  pl.BlockSpec((tq, 128), lambda h, k, i: (i, h))],
        out_specs=[pl.BlockSpec((1, S // hq, 256, hq), lambda h, k, i: (h, 0, 0, 0)),
                   pl.BlockSpec((tk, 256), lambda h, k, i: (k, h)),
                   pl.BlockSpec((tk, 128), lambda h, k, i: (k, h))],
        out_shape=[SDS((MLA_HEADS, S // hq, 256, hq), f32), SDS((S, 2048), f32), SDS((S, 1024), f32)],
        scratch_shapes=[pltpu.VMEM((tk, 256), f32), pltpu.VMEM((tk, 128), f32)],
        compiler_params=_params(("parallel", "arbitrary", "arbitrary"), VMEM_BIG), name=name)(Q, K, kv, delta, L, dO)


def _mla_qk_prep(q, kv, zm, cosm, sinm, *, name):
    S = q.shape[0]
    T = min(S, 512)
    scale = (MLA_NOPE + MLA_ROPE) ** -0.5 * math.log2(math.e)

    def body(q_ref, kv_ref, kr_ref, cos_ref, sin_ref, oq_ref, ok_ref):
        cs, sn = cos_ref[...], sin_ref[...]
        kr = _rope(kr_ref[...], cs, sn, 32).astype(bf16)
        for h in range(MLA_HEADS):
            a = 256 * h
            oq_ref[:, a:a + 128] = (q_ref[:, a:a + 128].astype(f32) * scale).astype(bf16)
            oq_ref[:, a + 128:a + 256] = (_rope(q_ref[:, a + 128:a + 256].astype(f32), cs, sn, 32) * scale).astype(bf16)
            ok_ref[:, a:a + 128] = kv_ref[:, a:a + 128]
            ok_ref[:, a + 128:a + 256] = kr

    row = lambda w, col=0: pl.BlockSpec((T, w), lambda i: (i, col))
    return pl.pallas_call(
        body, grid=(S // T,), in_specs=[row(2048), row(2048), row(128, 6), row(128), row(128)],
        out_specs=[row(2048), row(2048)], out_shape=[SDS((S, 2048), bf16), SDS((S, 2048), bf16)],
        compiler_params=_params(("parallel",), VMEM_BIG), name=name)(q, kv, zm, cosm, sinm)


def _mla_bwd_prep(dQ, dK, dV, cosm, sinm, *, name):
    S = dK.shape[0]
    T = dQ.shape[3]
    scale = (MLA_NOPE + MLA_ROPE) ** -0.5

    def body(dq_ref, dk_ref, dv_ref, cos_ref, sin_ref, oq_ref, okv_ref, okr_ref):
        cs, sn = cos_ref[...], sin_ref[...]
        kr = jnp.zeros((T, 128), f32)
        for h in range(MLA_HEADS):
            a = 256 * h
            dq = jnp.transpose(dq_ref[h, 0])
            oq_ref[:, a:a + 128] = (dq[:, 0:128] * scale).astype(bf16)
            oq_ref[:, a + 128:a + 256] = (_rope_t(dq[:, 128:256], cs, sn, 32) * scale).astype(bf16)
            okv_ref[:, a:a + 128] = dk_ref[:, a:a + 128].astype(bf16)
            okv_ref[:, a + 128:a + 256] = dv_ref[:, 128 * h:128 * h + 128].astype(bf16)
            kr = kr + dk_ref[:, a + 128:a + 256]
        okr_ref[...] = _rope_t(kr, cs, sn, 32)

    return pl.pallas_call(
        body, grid=(S // T,),
        in_specs=[pl.BlockSpec((MLA_HEADS, 1, 256, T), lambda i: (0, i, 0, 0)), pl.BlockSpec((T, 2048), lambda i: (i, 0)),
                  pl.BlockSpec((T, 1024), lambda i: (i, 0)), pl.BlockSpec((T, 128), lambda i: (i, 0)),
                  pl.BlockSpec((T, 128), lambda i: (i, 0))],
        out_specs=[pl.BlockSpec((T, 2048), lambda i: (i, 0)), pl.BlockSpec((T, 2048), lambda i: (i, 0)),
                   pl.BlockSpec((T, 128), lambda i: (i, 0))],
        out_shape=[SDS((S, 2048), bf16), SDS((S, 2048), bf16), SDS((S, 128), f32)],
        compiler_params=_params(("parallel",), VMEM_BIG), name=name)(dQ, dK, dV, cosm, sinm)


def _mla_norm_bwd(zm, qg, kvg, dcqn, dckvn, dkr, *, name):
    S = zm.shape[0]
    T = min(S, 512)

    def body(cq_ref, ckv_ref, qg_ref, kvg_ref, dcq_ref, dckv_ref, dkr_ref, o_ref, dqg_ref, dkvg_ref):
        i = pl.program_id(0)
        _, vjp = jax.vjp(_rms, cq_ref[...], qg_ref[...])
        dcq, dqg = vjp(dcq_ref[...])
        _, vjp2 = jax.vjp(_rms, ckv_ref[...], kvg_ref[...])
        dckv, dkvg = vjp2(dckv_ref[...])
        o_ref[:, 0:384] = dcq.astype(bf16)
        o_ref[:, 384:512] = jnp.zeros((T, 128), bf16)
        o_ref[:, 512:768] = dckv.astype(bf16)
        o_ref[:, 768:896] = dkr_ref[...].astype(bf16)

        @pl.when(i == 0)
        def _():
            dqg_ref[...] = dqg
            dkvg_ref[...] = dkvg

        @pl.when(i > 0)
        def _():
            dqg_ref[...] += dqg
            dkvg_ref[...] += dkvg

    return pl.pallas_call(
        body, grid=(S // T,),
        in_specs=[pl.BlockSpec((T, 384), lambda i: (i, 0)), pl.BlockSpec((T, 256), lambda i: (i, 2)),
                  pl.BlockSpec((1, 384), lambda i: (0, 0)), pl.BlockSpec((1, 256), lambda i: (0, 0)),
                  pl.BlockSpec((T, 384), lambda i: (i, 0)), pl.BlockSpec((T, 256), lambda i: (i, 0)),
                  pl.BlockSpec((T, 128), lambda i: (i, 0))],
        out_specs=[pl.BlockSpec((T, 896), lambda i: (i, 0)), pl.BlockSpec((1, 384), lambda i: (0, 0)),
                   pl.BlockSpec((1, 256), lambda i: (0, 0))],
        out_shape=[SDS((S, 896), bf16), SDS((1, 384), f32), SDS((1, 256), f32)],
        compiler_params=_params(("arbitrary",)), name=name)(zm, zm, qg, kvg, dcqn, dckvn, dkr)


def _s5_disc(a_re, a_im, ldt, b_re, b_im):
    dt = jnp.exp(ldt)
    ar = jnp.minimum(a_re, -1e-4)
    mag = jnp.exp(dt * ar)
    abr = mag * jnp.cos(dt * a_im)
    abi = mag * jnp.sin(dt * a_im)
    den = ar * ar + a_im * a_im
    nr = abr - 1.0
    ni = abi
    cr = (nr * ar + ni * a_im) / den
    ci = (ni * ar - nr * a_im) / den
    return abr, abi, cr * b_re - ci * b_im, cr * b_im + ci * b_re


def _s5_param_fwd(a_re, a_im, ldt, b_re, b_im, *, name):
    R = SDS((1, 8192), f32)
    M = SDS((16, 8192), f32)
    LP = S5_T // S5_SEG
    Pw = SDS((LP, 8192), f32)

    def body(a_re_r, a_im_r, ldt_r, b_re_r, b_im_r, o1, o2, o3, o4, p_re, p_im):
        abr, abi, bbr, bbi = _s5_disc(a_re_r[...], a_im_r[...], ldt_r[...], b_re_r[...], b_im_r[...])
        o1[...] = abr
        o2[...] = abi
        o3[...] = bbr
        o4[...] = bbi
        dt = jnp.exp(ldt_r[...])
        ar = jnp.minimum(a_re_r[...], -1e-4)
        n = lax.broadcasted_iota(jnp.int32, (LP, 8192), 0).astype(f32) + 1.0
        mag = jnp.exp(n * (dt * ar))
        ang = n * (dt * a_im_r[...])
        p_re[...] = mag * jnp.cos(ang)
        p_im[...] = mag * jnp.sin(ang)

    return pl.pallas_call(body, out_shape=[R, R, M, M, Pw, Pw], name=name)(a_re, a_im, ldt, b_re, b_im)


def _s5_param_bwd(a_re, a_im, ldt, b_re, b_im, d_abr, d_abi, d_bbr, d_bbi, *, name):
    R = SDS((1, 8192), f32)
    M = SDS((16, 8192), f32)

    def body(a_re_r, a_im_r, ldt_r, b_re_r, b_im_r, c1, c2, c3, c4, o1, o2, o3, o4, o5):
        _, vjp = jax.vjp(_s5_disc, a_re_r[...], a_im_r[...], ldt_r[...], b_re_r[...], b_im_r[...])
        g = vjp((c1[...], c2[...], c3[...], c4[...]))
        for o, v in zip((o1, o2, o3, o4, o5), g):
            o[...] = v

    return pl.pallas_call(body, out_shape=[R, R, R, M, M], name=name)(a_re, a_im, ldt, b_re, b_im, d_abr, d_abi, d_bbr, d_bbi)


def _seg_perm(T, inverse):
    L = T // S5_SEG
    i = jnp.arange(T)
    src = (i % S5_SEG) * L + i // S5_SEG
    P = (src[:, None] == jnp.arange(T)[None, :]).astype(bf16)
    return P.T if inverse else P


def _perm_rows(a, P, *, name):
    S, W = a.shape
    T = P.shape[0]

    def body(p_ref, a_ref, o_ref):
        o_ref[...] = lax.dot_general(p_ref[...], a_ref[...], (NN, ((), ())), preferred_element_type=f32).astype(o_ref.dtype)

    return pl.pallas_call(
        body, grid=(S // T,), in_specs=[pl.BlockSpec((T, T), lambda i: (0, 0)), pl.BlockSpec((T, W), lambda i: (i, 0))],
        out_specs=pl.BlockSpec((T, W), lambda i: (i, 0)), out_shape=SDS((S, W), a.dtype),
        compiler_params=_params(("parallel",)), name=name)(P, a)


def _scan_core(xr, xi, ar, ai, pwr_ref, pwi_ref, a64r, a64i, carry, *, reverse, T, conj):
    L = T // S5_SEG
    sg = -1.0 if conj else 1.0
    arb = jnp.broadcast_to(ar, (8, 512))
    aib = jnp.broadcast_to(ai, (8, 512))
    UN = 4

    def step(r4, c):
        cr, ci = c
        for u in range(UN):
            r0 = r4 * UN + u
            r = (L - 1 - r0) if reverse else r0
            rows = pl.ds(pl.multiple_of(r * 8, 8), 8)
            nr = arb * cr - aib * ci + xr[rows, :]
            ni = arb * ci + aib * cr + xi[rows, :]
            xr[rows, :] = nr
            xi[rows, :] = ni
            cr, ci = nr, ni
        return cr, ci

    lr, li = lax.fori_loop(0, L // UN, step, (jnp.zeros((8, 512), f32), jnp.zeros((8, 512), f32)))
    row8 = lax.broadcasted_iota(jnp.int32, (8, 512), 0)
    cr = carry[0, 0:1, :]
    ci = carry[1, 0:1, :]
    a6i = sg * a64i
    cin_r = jnp.zeros((8, 512), f32)
    cin_i = jnp.zeros((8, 512), f32)
    for seg in (range(S5_SEG - 1, -1, -1) if reverse else range(S5_SEG)):
        cin_r = jnp.where(row8 == seg, cr, cin_r)
        cin_i = jnp.where(row8 == seg, ci, cin_i)
        ncr = lr[seg:seg + 1, :] + a64r * cr - a6i * ci
        nci = li[seg:seg + 1, :] + a64r * ci + a6i * cr
        cr, ci = ncr, nci
    carry[0, 0:1, :] = cr
    carry[1, 0:1, :] = ci

    def fix(r4, _):
        for u in range(UN):
            r = r4 * UN + u
            rows = pl.ds(pl.multiple_of(r * 8, 8), 8)
            pr = pwr_ref[pl.ds(r, 1), :]
            pi = sg * pwi_ref[pl.ds(r, 1), :]
            xr[rows, :] += pr * cin_r - pi * cin_i
            xi[rows, :] += pr * cin_i + pi * cin_r
        return 0

    lax.fori_loop(0, L // UN, fix, 0)


def _s5_scan_fwd(u, BBr, BBi, CCr, CCi, abr, abi, pwr, pwi, *, reverse, name):
    S = u.shape[0]
    T = S5_T
    NB = S // T
    L = T // S5_SEG
    d = 1 if reverse else 0

    def tb(b):
        return (NB - 1 - b) if reverse else b

    def body(u_ref, bbr_ref, bbi_ref, ccr_ref, cci_ref, ar_ref, ai_ref, pwr_ref, pwi_ref, y_ref, xr_ref, xi_ref, carry):
        b = pl.program_id(1)

        @pl.when(b == 0)
        def _():
            carry[...] = jnp.zeros_like(carry)

        ub = u_ref[...].astype(bf16)
        xr_ref[...] = lax.dot_general(ub, bbr_ref[0, 0], (NN, ((), ())), preferred_element_type=f32)
        xi_ref[...] = lax.dot_general(ub, bbi_ref[0, 0], (NN, ((), ())), preferred_element_type=f32)
        a6 = (0 if reverse else L - 1)
        _scan_core(xr_ref, xi_ref, ar_ref[...], ai_ref[...], pwr_ref, pwi_ref, pwr_ref[a6:a6 + 1, :], pwi_ref[a6:a6 + 1, :],
                   carry, reverse=reverse, T=T, conj=False)
        y_ref[...] = _dot(xr_ref[...], ccr_ref[0, 0], NN) - _dot(xi_ref[...], cci_ref[0, 0], NN)

    mat = lambda shp: pl.BlockSpec((1, 1) + shp, lambda j, b: (d, j, 0, 0))
    vec = lambda r: pl.BlockSpec((r, 512), lambda j, b: (0, d * S5_NJ + j))
    return pl.pallas_call(
        body, grid=(S5_NJ, NB),
        in_specs=[pl.BlockSpec((T, 128), lambda j, b: (tb(b), j)), mat((128, 512)), mat((128, 512)), mat((512, 128)),
                  mat((512, 128)), vec(1), vec(1), vec(L), vec(L)],
        out_specs=[pl.BlockSpec((T, 128), lambda j, b: (tb(b), j)), pl.BlockSpec((T, 512), lambda j, b: (tb(b), j)),
                   pl.BlockSpec((T, 512), lambda j, b: (tb(b), j))],
        out_shape=[SDS((S, 1024), f32), SDS((S, 4096), f32), SDS((S, 4096), f32)],
        scratch_shapes=[pltpu.VMEM((2, 8, 512), f32)],
        compiler_params=_params(("parallel", "arbitrary")), name=name)(u, BBr, BBi, CCr, CCi, abr, abi, pwr, pwi)


def _s5_scan_bwd(u, dy, xr, xi, BBr, BBi, CCr, CCi, abr, abi, pwr, pwi, *, reverse, name):
    S = u.shape[0]
    T = S5_T
    NB = S // T
    L = T // S5_SEG
    d = 1 if reverse else 0
    adj_rev = not reverse

    def tb(b):
        return b if reverse else (NB - 1 - b)

    def bnd(b):
        t = tb(b)
        if reverse:
            return jnp.minimum((t + 1) * (T // 8), S // 8 - 1)
        return jnp.maximum(t * (T // 8) - 1, 0)

    def body(u_ref, dy_ref, xr_ref, xi_ref, xbr_ref, xbi_ref, bbr_ref, bbi_ref, ccr_ref, cci_ref, ar_ref, ai_ref,
             pwr_ref, pwi_ref, du_ref, dbbr_ref, dbbi_ref, dccr_ref, dcci_ref, dar_ref, dai_ref, carry, lam):
        b = pl.program_id(1)

        @pl.when(b == 0)
        def _():
            carry[...] = jnp.zeros_like(carry)
            for r in (dbbr_ref, dbbi_ref, dccr_ref, dcci_ref, dar_ref, dai_ref):
                r[...] = jnp.zeros_like(r)

        dyb = dy_ref[...]
        lam[0] = lax.dot_general(dyb, ccr_ref[0, 0], (NT, ((), ())), preferred_element_type=f32)
        lam[1] = -lax.dot_general(dyb, cci_ref[0, 0], (NT, ((), ())), preferred_element_type=f32)
        a6 = (0 if adj_rev else L - 1)
        _scan_core(lam.at[0], lam.at[1], ar_ref[...], -ai_ref[...], pwr_ref, pwi_ref, pwr_ref[a6:a6 + 1, :],
                   pwi_ref[a6:a6 + 1, :], carry, reverse=adj_rev, T=T, conj=True)
        ub = u_ref[...].astype(bf16)
        first = (b == NB - 1)
        lrb = lam[0].astype(bf16)
        lib = lam[1].astype(bf16)
        du_ref[...] = lax.dot_general(lrb, bbr_ref[0, 0], (NT, ((), ())), preferred_element_type=f32) \
            + lax.dot_general(lib, bbi_ref[0, 0], (NT, ((), ())), preferred_element_type=f32)
        dbbr_ref[0, 0] += lax.dot_general(ub, lrb, (TN, ((), ())), preferred_element_type=f32)
        dbbi_ref[0, 0] += lax.dot_general(ub, lib, (TN, ((), ())), preferred_element_type=f32)
        dccr_ref[0, 0] += lax.dot_general(dyb, xr_ref[...].astype(bf16), (TN, ((), ())), preferred_element_type=f32)
        dcci_ref[0, 0] -= lax.dot_general(dyb, xi_ref[...].astype(bf16), (TN, ((), ())), preferred_element_type=f32)
        row8 = lax.broadcasted_iota(jnp.int32, (8, 512), 0)
        if reverse:
            body_x, body_l, edge_l = slice(8, T), slice(0, T - 8), slice(T - 8, T)
            sp_r = jnp.where(row8 == 7, jnp.where(first, 0.0, xbr_ref[0:1, :]), pltpu.roll(xr_ref[0:8, :], 7, axis=0))
            sp_i = jnp.where(row8 == 7, jnp.where(first, 0.0, xbi_ref[0:1, :]), pltpu.roll(xi_ref[0:8, :], 7, axis=0))
        else:
            body_x, body_l, edge_l = slice(0, T - 8), slice(8, T), slice(0, 8)
            sp_r = jnp.where(row8 == 0, jnp.where(first, 0.0, xbr_ref[7:8, :]), pltpu.roll(xr_ref[T - 8:T, :], 1, axis=0))
            sp_i = jnp.where(row8 == 0, jnp.where(first, 0.0, xbi_ref[7:8, :]), pltpu.roll(xi_ref[T - 8:T, :], 1, axis=0))
        xpr, xpi = xr_ref[body_x, :], xi_ref[body_x, :]
        lr, li = lam[0, body_l, :], lam[1, body_l, :]
        er, ei = lam[0, edge_l, :], lam[1, edge_l, :]
        dar_ref[...] += jnp.sum(xpr * lr + xpi * li, axis=0, keepdims=True) + jnp.sum(sp_r * er + sp_i * ei, axis=0, keepdims=True)
        dai_ref[...] += jnp.sum(xpr * li - xpi * lr, axis=0, keepdims=True) + jnp.sum(sp_r * ei - sp_i * er, axis=0, keepdims=True)

    mat = lambda shp: pl.BlockSpec((1, 1) + shp, lambda j, b: (d, j, 0, 0))
    omat = lambda shp: pl.BlockSpec((1, 1) + shp, lambda j, b: (0, j, 0, 0))
    vec = lambda r: pl.BlockSpec((r, 512), lambda j, b: (0, d * S5_NJ + j))
    blk = lambda w: pl.BlockSpec((T, w), lambda j, b: (tb(b), j))
    return pl.pallas_call(
        body, grid=(S5_NJ, NB),
        in_specs=[blk(128), blk(128), blk(512), blk(512),
                  pl.BlockSpec((8, 512), lambda j, b: (bnd(b), j)), pl.BlockSpec((8, 512), lambda j, b: (bnd(b), j)),
                  mat((128, 512)), mat((128, 512)), mat((512, 128)), mat((512, 128)), vec(1), vec(1), vec(L), vec(L)],
        out_specs=[blk(128), omat((128, 512)), omat((128, 512)), omat((128, 512)), omat((128, 512)),
                   pl.BlockSpec((1, 512), lambda j, b: (0, j)), pl.BlockSpec((1, 512), lambda j, b: (0, j))],
        out_shape=[SDS((S, 1024), f32), SDS((1, 8, 128, 512), f32), SDS((1, 8, 128, 512), f32), SDS((1, 8, 128, 512), f32),
                   SDS((1, 8, 128, 512), f32), SDS((1, 4096), f32), SDS((1, 4096), f32)],
        scratch_shapes=[pltpu.VMEM((2, 8, 512), f32), pltpu.VMEM((2, T, 512), f32)],
        compiler_params=_params(("parallel", "arbitrary"), VMEM_BIG), name=name)(
            u, dy, xr, xi, xr, xi, BBr, BBi, CCr, CCi, abr, abi, pwr, pwi)


def _silu_mul(g, u):
    return jax.nn.silu(g) * u


def _mixf(p0, p1, p2, z0, z1, z2):
    return jax.nn.sigmoid(z0) * p0 + jax.nn.sigmoid(z1) * p1 + jax.nn.sigmoid(z2) * p2


def _s5_act(yf, yb, u, dd):
    return jax.nn.gelu(yf + yb + dd * u)


def _glu(a, b):
    return a * jax.nn.sigmoid(b)


def _layer_fwd(x, w, tabs, l):
    S = x.shape[0]
    T = min(S, 1024)
    I = S // T
    nm = lambda s: f"L{l}_{s}"
    sv = {'x': x}
    h = _rmsnorm_fwd(x, w['norm1_g'], name=nm("norm1"))
    zr = _mm(h, w['W_ret'], name=nm("in_ret"))
    zm = _mm(h, w['W_mla'], name=nm("in_mla"))
    h_seg = _perm_rows(h, tabs['seg_perm'], name=nm("s5_perm_h"))
    zs = _mm(h_seg, w['W_s5'], name=nm("in_s5"))
    zg = _mm(h, w['W_gate'], out_dtype=bf16, name=nm("in_gate"))
    sv.update(h=h, h_seg=h_seg, zr=zr, zm=zm, zs=zs, zg=zg)

    yf, stf = _ret_dir_fwd(zr, w['lg'], tabs['cos_r'], tabs['sin_r'], reverse=False, name=nm("ret_f"))
    yb, stb = _ret_dir_fwd(zr, w['lg'], tabs['cos_r'], tabs['sin_r'], reverse=True, name=nm("ret_b"))
    hd = lambda j: j
    y_ret = _pw(_gn_gate, [yf, yb, zr, w['ret_gn_g']],
                [_row(T, 256, hd), _row(T, 256, hd), _row(T, 256, lambda j: 8 + j), _par(256, hd)],
                [SDS((S, 1024), bf16)], [_row(T, 256, hd)], (RET_HEADS, I), name=nm("ret_gn"))[0]
    sv.update(yf=yf, yb=yb, stf=stf, stb=stb, y_ret=y_ret)

    cqn, ckvn = _pw(lambda a, b, g1, g2: (_rms(a, g1), _rms(b, g2)), [zm, zm, w['mla_q_norm_g'], w['mla_kv_norm_g']],
                    [_row(T, 384), _row(T, 256, lambda j: 2), _par(384), _par(256)],
                    [SDS((S, 384), bf16), SDS((S, 256), bf16)], [_row(T, 384), _row(T, 256)], (1, I), name=nm("mla_norm"))
    q = _mm(cqn, w['W_uq'], out_dtype=bf16, name=nm("mla_uq"))
    kv = _mm(ckvn, w['W_ukv'], out_dtype=bf16, name=nm("mla_ukv"))
    Q, K = _mla_qk_prep(q, kv, zm, tabs['cos_m'], tabs['sin_m'], name=nm("mla_qkprep"))
    O, Lse = _flash_fwd(Q, K, kv, name=nm("mla_attn"))
    sv.update(cqn=cqn, ckvn=ckvn, kv=kv, Q=Q, K=K, O=O, Lse=Lse)

    s5 = w['s5']
    ysf, xrf, xif = _s5_scan_fwd(zs, s5['BBr'], s5['BBi'], s5['CCr'], s5['CCi'], s5['abr'], s5['abi'], s5['pwr_f'], s5['pwi_f'],
                                 reverse=False, name=nm("s5_f"))
    ysb, xrb, xib = _s5_scan_fwd(zs, s5['BBr'], s5['BBi'], s5['CCr'], s5['CCi'], s5['abr'], s5['abi'], s5['pwr_f'], s5['pwi_f'],
                                 reverse=True, name=nm("s5_b"))
    gact = _pw(_s5_act, [ysf, ysb, zs, w['s5_d']], [_row(T, D), _row(T, D), _row(T, D), _par(D)],
               [SDS((S, D), bf16)], [_row(T, D)], (1, I), name=nm("s5_act"))[0]
    gg = _mm(gact, w['W_glu'], out_dtype=bf16, name=nm("s5_glu_mm"))
    y_s5 = _pw(_glu, [gg, gg], [_row(T, D), _row(T, D, lambda j: 1)], [SDS((S, D), bf16)], [_row(T, D)], (1, I),
               name=nm("s5_glu"))[0]
    y_s5 = _perm_rows(y_s5, tabs['seg_unperm'], name=nm("s5_unperm_y"))
    sv.update(ysf=ysf, ysb=ysb, xrf=xrf, xif=xif, xrb=xrb, xib=xib, gact=gact, gg=gg, y_s5=y_s5)

    ys = [y_ret, O, y_s5]
    pr = [_mm(ys[i], w['W_br'][i], out_dtype=bf16, name=nm(f"branch{i}")) for i in range(3)]
    mix = _pw(_mixf, pr + [zg, zg, zg],
              [_row(T, D)] * 3 + [_row(T, D), _row(T, D, lambda j: 1), _row(T, D, lambda j: 2)],
              [SDS((S, D), bf16)], [_row(T, D)], (1, I), name=nm("mix"))[0]
    x1 = _mm(mix, w['W_out'], res=x, name=nm("out_proj"))
    h2 = _rmsnorm_fwd(x1, w['norm2_g'], name=nm("norm2"))
    fgu = _mm(h2, w['W_gu'], out_dtype=bf16, name=nm("ffn_gu"))
    act = _pw(_silu_mul, [fgu, fgu], [_row(T, 1408, lambda j: j), _row(T, 1408, lambda j: 2 + j)],
              [SDS((S, FFN_H), bf16)], [_row(T, 1408, lambda j: j)], (2, I), name=nm("ffn_act"))[0]
    x2 = _mm(act, w['W_down'], res=x1, name=nm("ffn_down"))
    sv.update(pr=pr, mix=mix, x1=x1, h2=h2, fgu=fgu, act=act)
    return x2, sv


def _vjp_fn(fn, n_primal, cast=None):
    def g(*args):
        _, vjp = jax.vjp(fn, *args[:n_primal])
        return vjp(args[n_primal].astype(f32))
    return g


def _layer_bwd(dx2, w, tabs, sv, l):
    S = dx2.shape[0]
    T = min(S, 1024)
    I = S // T
    nm = lambda s: f"L{l}_b_{s}"
    g = {}
    hd = lambda j: j

    dact = _mm(dx2, w['W_down'], tb=True, out_dtype=bf16, name=nm("ffn_down_dx"))
    g['W_down'] = _mmT(sv['act'], dx2, name=nm("ffn_down_dw"))
    dfg, dfu = _pw(_vjp_fn(_silu_mul, 2), [sv['fgu'], sv['fgu'], dact],
                   [_row(T, 1408, lambda j: j), _row(T, 1408, lambda j: 2 + j), _row(T, 1408, lambda j: j)],
                   [SDS((S, FFN_H), bf16), SDS((S, FFN_H), bf16)], [_row(T, 1408, lambda j: j)] * 2, (2, I), name=nm("ffn_act"))
    dfgu = jnp.concatenate([dfg, dfu], axis=1)
    g['W_gu'] = _mmT(sv['h2'], dfgu, name=nm("ffn_gu_dw"))
    dh2 = _mm(dfgu, w['W_gu'], tb=True, name=nm("ffn_gu_dx"))
    dx1, g['norm2_g'] = _rmsnorm_bwd(sv['x1'], w['norm2_g'], dh2, dx2, name=nm("norm2"))

    dmix = _mm(dx1, w['W_out'], tb=True, out_dtype=bf16, name=nm("out_dx"))
    g['W_out'] = _mmT(sv['mix'], dx1, name=nm("out_dw"))
    zg = sv['zg']
    Th = min(S, 512)
    outs = _pw(_vjp_fn(_mixf, 6), sv['pr'] + [zg, zg, zg, dmix],
               [_row(Th, D)] * 3 + [_row(Th, D), _row(Th, D, lambda j: 1), _row(Th, D, lambda j: 2), _row(Th, D)],
               [SDS((S, D), bf16)] * 6, [_row(Th, D)] * 6, (1, S // Th), name=nm("mix"))
    dpr, dzg = outs[:3], jnp.concatenate(outs[3:], axis=1)
    ys = [sv['y_ret'], sv['O'], sv['y_s5']]
    g['W_br'] = [_mmT(ys[i], dpr[i], name=nm(f"branch{i}_dw")) for i in range(3)]
    dpr_seg = _perm_rows(dpr[2], tabs['seg_perm'], name=nm("s5_perm_dy"))
    dys = [_mm(dpr[i] if i < 2 else dpr_seg, w['W_br'][i], tb=True, out_dtype=bf16,
               name=nm(f"branch{i}_dx")) for i in range(3)]

    gg = sv['gg']
    dga, dgb = _pw(_vjp_fn(_glu, 2), [gg, gg, dys[2]], [_row(T, D), _row(T, D, lambda j: 1), _row(T, D)],
                   [SDS((S, D), bf16)] * 2, [_row(T, D)] * 2, (1, I), name=nm("s5_glu"))
    dgg = jnp.concatenate([dga, dgb], axis=1)
    g['W_glu'] = _mmT(sv['gact'], dgg, name=nm("s5_glu_dw"))
    dgact = _mm(dgg, w['W_glu'], tb=True, out_dtype=bf16, name=nm("s5_glu_dx"))

    def act_bwd(yf, yb, u, dd, ct):
        _, vjp = jax.vjp(_s5_act, yf, yb, u, dd)
        dyf, _, du, ddd = vjp(ct)
        return dyf, du, ddd

    dys5, du_direct, g['s5_d'] = _pw(act_bwd, [sv['ysf'], sv['ysb'], sv['zs'], w['s5_d'], dgact],
                                     [_row(T, D)] * 3 + [_par(D), _row(T, D)],
                                     [SDS((S, D), bf16), SDS((S, D), f32), SDS((1, D), f32)],
                                     [_row(T, D), _row(T, D), _par(D)], (1, I), n_acc=1, name=nm("s5_act"))
    s5 = w['s5']
    rf = _s5_scan_bwd(sv['zs'], dys5, sv['xrf'], sv['xif'], s5['BBr'], s5['BBi'], s5['CCr'], s5['CCi'], s5['abr'], s5['abi'],
                      s5['pwr_a'], s5['pwi_a'], reverse=False, name=nm("s5_f"))
    rb = _s5_scan_bwd(sv['zs'], dys5, sv['xrb'], sv['xib'], s5['BBr'], s5['BBi'], s5['CCr'], s5['CCi'], s5['abr'], s5['abi'],
                      s5['pwr_a'], s5['pwi_a'], reverse=True, name=nm("s5_b"))
    g['s5'] = (rf[1:], rb[1:])
    dzs_seg = _pw(lambda a, b, c: a + b + c, [du_direct, rf[0], rb[0]], [_row(T, D)] * 3, [SDS((S, D), bf16)], [_row(T, D)],
                  (1, I), name=nm("s5_du"))[0]
    dzs = _perm_rows(dzs_seg, tabs['seg_unperm'], name=nm("s5_unperm_dz"))

    delta = _attn_delta(dys[1], sv['O'], name=nm("mla_delta"))
    dQ, dK, dV = _flash_bwd(sv['Q'], sv['K'], sv['kv'], delta, sv['Lse'], dys[1], name=nm("mla_attn"))
    dq_lin, dkv, dkr = _mla_bwd_prep(dQ, dK, dV, tabs['cos_m'], tabs['sin_m'], name=nm("mla_prep"))
    g['W_uq'] = _mmT(sv['cqn'], dq_lin, name=nm("mla_uq_dw"))
    dcqn = _mm(dq_lin, w['W_uq'], tb=True, name=nm("mla_uq_dx"))
    g['W_ukv'] = _mmT(sv['ckvn'], dkv, name=nm("mla_ukv_dw"))
    dckvn = _mm(dkv, w['W_ukv'], tb=True, name=nm("mla_ukv_dx"))
    dzm, g['mla_q_norm_g'], g['mla_kv_norm_g'] = _mla_norm_bwd(sv['zm'], w['mla_q_norm_g'], w['mla_kv_norm_g'], dcqn, dckvn, dkr,
                                                               name=nm("mla_norm"))

    zr = sv['zr']

    def gn_bwd(yf, yb, gt, gn, ct):
        _, vjp = jax.vjp(_gn_gate, yf, yb, gt, gn)
        dyf, _, dgt, dgn = vjp(ct)
        return dyf, dgt, dgn

    dyr, dgate, g['ret_gn_g'] = _pw(gn_bwd, [sv['yf'], sv['yb'], zr, w['ret_gn_g'], dys[0]],
                                    [_row(T, 256, hd), _row(T, 256, hd), _row(T, 256, lambda j: 8 + j), _par(256, hd),
                                     _row(T, 256, hd)],
                                    [SDS((S, 1024), bf16), SDS((S, 1024), bf16), SDS((1, 1024), f32)],
                                    [_row(T, 256, hd), _row(T, 256, hd), _par(256, hd)], (RET_HEADS, I), n_acc=1, name=nm("ret_gn"))
    qf, kf, vf, lgf = _ret_dir_bwd(zr, w['lg'], tabs['cos_r'], tabs['sin_r'], dyr, sv['stf'], reverse=False, name=nm("ret_f"))
    qb, kb, vb, lgb = _ret_dir_bwd(zr, w['lg'], tabs['cos_r'], tabs['sin_r'], dyr, sv['stb'], reverse=True, name=nm("ret_b"))
    g['lg'] = jnp.stack([lgf[:, 0, 0], lgb[:, 0, 0]])
    dzr = _pw(lambda a, b, c, d, e, f, gt: jnp.concatenate([a + b, c + d, e + f, gt], axis=1),
              [qf, qb, kf, kb, vf, vb, dgate], [_row(256, 512)] * 4 + [_row(256, D)] * 3,
              [SDS((S, 3072), bf16)], [_row(256, 3072)], (1, S // 256), name=nm("ret_dz"))[0]

    h = sv['h']
    g['W_ret'] = _mmT(h, dzr, name=nm("in_ret_dw"))
    g['W_mla'] = _mmT(h, dzm, name=nm("in_mla_dw"))
    g['W_s5'] = _mmT(sv['h_seg'], dzs_seg, name=nm("in_s5_dw"))
    g['W_gate'] = _mmT(h, dzg, name=nm("in_gate_dw"))
    dh = _mm(dzr, w['W_ret'], tb=True, name=nm("in_ret_dx"))
    dh = _mm(dzm, w['W_mla'], tb=True, res=dh, name=nm("in_mla_dx"))
    dh = _mm(dzs, w['W_s5'], tb=True, res=dh, name=nm("in_s5_dx"))
    dh = _mm(dzg, w['W_gate'], tb=True, res=dh, name=nm("in_gate_dx"))
    dx, g['norm1_g'] = _rmsnorm_bwd(sv['x'], w['norm1_g'], dh, dx1, name=nm("norm1"))
    return dx, g


def _loss_head(x, tgt, gain, *, name):
    S, W = x.shape
    T = min(S, 512)

    def loss_fn(xv, gv, tv):
        return 0.5 * jnp.sum(jnp.mean(jnp.square(_rms(xv, gv) - tv), axis=-1, keepdims=True), axis=0, keepdims=True)

    def fn(xv, gv, tv):
        lv, vjp = jax.vjp(lambda a, b: loss_fn(a, b, tv), xv, gv)
        dx, dg = vjp(jnp.ones((1, 1), f32))
        return dx, jnp.broadcast_to(lv, (1, 128)), dg

    return _pw(fn, [x, gain, tgt], [_row(T, W), _par(W), _row(T, W)],
               [SDS((S, W), f32), SDS((1, 128), f32), SDS((1, W), f32)], [_row(T, W), _par(128), _par(W)],
               (1, S // T), n_acc=2, name=name)


def _rope_tabs(S):
    def tab(dim):
        inv = 1.0 / (ROPE_THETA ** (jnp.arange(0, dim, 2, dtype=f32) / dim))
        ang = jnp.arange(S, dtype=f32)[:, None] * inv[None, :]
        return jnp.cos(ang), jnp.sin(ang)

    cr, sr = tab(RET_DK)
    cm, sm = tab(MLA_ROPE)
    z = jnp.zeros((S, 64), f32)
    return {'cos_r': jnp.concatenate([cr, cr], axis=1), 'sin_r': jnp.concatenate([-sr, sr], axis=1),
            'cos_m': jnp.concatenate([cm, cm, z], axis=1), 'sin_m': jnp.concatenate([-sm, sm, z], axis=1),
            'seg_perm': _seg_perm(S5_T, False), 'seg_unperm': _seg_perm(S5_T, True)}


def _bd_B(bb):
    b5 = bb.reshape(16, 2, 8, 8, 64)
    return jnp.einsum('cdjgp,gh->djgchp', b5, jnp.eye(8, dtype=bb.dtype)).reshape(2, 8, 128, 512)


def _bd_B_t(dBB):
    return jnp.einsum('djgcgp->cdjgp', dBB.reshape(2, 8, 8, 16, 8, 64)).reshape(16, 8192)


def _bd_C(c):
    c5 = c.reshape(2, 8, 8, 16, 64)
    return jnp.einsum('djgcp,gh->djgphc', c5, jnp.eye(8, dtype=c.dtype)).reshape(2, 8, 512, 128)


def _s5_rows(p, l):
    a_re = p['s5_a_re'][l].reshape(1, 8192)
    a_im = p['s5_a_im'][l].reshape(1, 8192)
    ldt = jnp.broadcast_to(p['s5_log_dt'][l][:, :, None], (2, S5_G, S5_P)).reshape(1, 8192)
    b_re = p['s5_b_re'][l].transpose(3, 0, 1, 2).reshape(16, 8192)
    b_im = p['s5_b_im'][l].transpose(3, 0, 1, 2).reshape(16, 8192)
    return a_re, a_im, ldt, b_re, b_im


def _layer_weights(big, p, l):
    w_in = big['w_in'][l]
    z = lambda n: jnp.zeros((D, n), w_in.dtype)
    w = {
        'W_ret': w_in[:, 0:3072],
        'W_mla': jnp.concatenate([w_in[:, 3072:3456], z(128), w_in[:, 3456:3712], w_in[:, 3712:3776], z(64)], axis=1),
        'W_s5': w_in[:, 3776:4800],
        'W_gate': w_in[:, 4800:7872],
        'W_uq': jnp.pad(big['mla_w_uq'][l].reshape(MLA_Q_LORA, MLA_HEADS, 192), ((0, 0), (0, 0), (0, 64))).reshape(MLA_Q_LORA, 2048),
        'W_ukv': big['mla_w_ukv'][l],
        'W_glu': big['s5_w_glu'][l],
        'W_br': [big['w_branch'][l, i] for i in range(3)],
        'W_out': big['w_out'][l],
        'W_gu': big['ffn_w_gu'][l],
        'W_down': big['ffn_w_down'][l],
    }
    for n in ('norm1_g', 'ret_gn_g', 'mla_q_norm_g', 'mla_kv_norm_g', 's5_d', 'norm2_g'):
        w[n] = p[n][l][None, :]
    w['lg'] = jax.nn.log_sigmoid(p['ret_decay'][l])
    rows = _s5_rows(p, l)
    abr, abi, bbr, bbi, pwr, pwi = _s5_param_fwd(*rows, name=f"L{l}_s5_param")
    flip = lambda t, first: jnp.concatenate([t[::-1, :4096], t[:, 4096:]] if first else [t[:, :4096], t[::-1, 4096:]], axis=1)
    w['s5'] = {'abr': abr, 'abi': abi, 'BBr': _bd_B(bbr).astype(bf16), 'BBi': _bd_B(bbi).astype(bf16),
               'CCr': _bd_C(p['s5_c_re'][l]).astype(bf16), 'CCi': _bd_C(p['s5_c_im'][l]).astype(bf16),
               'pwr_f': flip(pwr, False), 'pwi_f': flip(pwi, False), 'pwr_a': flip(pwr, True), 'pwi_a': flip(pwi, True),
               'rows': rows}
    return w


def _layer_grads(g, w, p, l):
    out = {}
    m = g['W_mla']
    out['w_in'] = jnp.concatenate([g['W_ret'], m[:, 0:384], m[:, 512:768], m[:, 768:832], g['W_s5'], g['W_gate']], axis=1)
    out['mla_w_uq'] = g['W_uq'].reshape(MLA_Q_LORA, MLA_HEADS, 256)[:, :, :192].reshape(MLA_Q_LORA, 1536)
    out['mla_w_ukv'] = g['W_ukv']
    out['s5_w_glu'] = g['W_glu']
    out['w_branch'] = jnp.stack(g['W_br'])
    out['w_out'] = g['W_out']
    out['ffn_w_gu'] = g['W_gu']
    out['ffn_w_down'] = g['W_down']
    for n in ('norm1_g', 'ret_gn_g', 'mla_q_norm_g', 'mla_kv_norm_g', 's5_d', 'norm2_g'):
        out[n] = g[n][0]
    out['ret_decay'] = g['lg'] * jax.nn.sigmoid(-p['ret_decay'][l])
    (fB_r, fB_i, fC_r, fC_i, fa_r, fa_i), (bB_r, bB_i, bC_r, bC_i, ba_r, ba_i) = g['s5']
    cat = lambda a, b: jnp.concatenate([a, b], axis=0)
    d_bbr = _bd_B_t(cat(fB_r, bB_r))
    d_bbi = _bd_B_t(cat(fB_i, bB_i))
    to_c = lambda t: _bd_B_t(t).reshape(16, 2, S5_G, S5_P).transpose(1, 2, 0, 3)
    out['s5_c_re'] = to_c(cat(fC_r, bC_r))
    out['s5_c_im'] = to_c(cat(fC_i, bC_i))
    d_abr = jnp.concatenate([fa_r, ba_r], axis=1)
    d_abi = jnp.concatenate([fa_i, ba_i], axis=1)
    da_re, da_im, dldt, db_re, db_im = _s5_param_bwd(*w['s5']['rows'], d_abr, d_abi, d_bbr, d_bbi, name=f"L{l}_b_s5_param")
    out['s5_a_re'] = da_re.reshape(2, S5_G, S5_P)
    out['s5_a_im'] = da_im.reshape(2, S5_G, S5_P)
    out['s5_log_dt'] = dldt.reshape(2, S5_G, S5_P).sum(axis=-1)
    out['s5_b_re'] = db_re.reshape(16, 2, S5_G, S5_P).transpose(1, 2, 3, 0)
    out['s5_b_im'] = db_im.reshape(16, 2, S5_G, S5_P).transpose(1, 2, 3, 0)
    return out


def _local_step(x, tgt, big, p):
    S = x.shape[0]
    assert S % S5_T == 0
    tabs = _rope_tabs(S)
    ws, svs = [], []
    h = x
    for l in range(DEPTH):
        w = _layer_weights(big, p, l)
        h, sv = _layer_fwd(h, w, tabs, l)
        ws.append(w)
        svs.append(sv)
    dx, lossv, dfinal = _loss_head(h, tgt, p['final_g'][None, :], name="loss_head")
    per_layer = [None] * DEPTH
    for l in reversed(range(DEPTH)):
        dx, g = _layer_bwd(dx, ws[l], tabs, svs[l], l)
        per_layer[l] = _layer_grads(g, ws[l], p, l)
    return lossv[0, 0], dx, per_layer, dfinal[0]


_ANY = pl.BlockSpec(memory_space=pl.ANY)


def _place():
    x, y, c = lax.axis_index("x"), lax.axis_index("y"), lax.axis_index("c")
    return x, y, c, [(1 - x, y), (x, 1 - y), (1 - x, 1 - y)]


def _allgather4(arrs, *, name):
    n = len(arrs)

    def body(*refs):
        ins, outs = refs[:n], refs[n:2 * n]
        send, recv, loc = refs[2 * n:]
        x, y, c, chips = _place()
        me = 2 * x + y

        def remote(a, k, slot):
            px, py = chips[k]
            return pltpu.make_async_remote_copy(src_ref=ins[a], dst_ref=outs[a].at[slot], send_sem=send.at[a, k],
                                                recv_sem=recv.at[a, k], device_id=(px, py, c), device_id_type=MESH)

        mine = [pltpu.make_async_copy(ins[a], outs[a].at[me], loc.at[a]) for a in range(n)]
        for cp in mine:
            cp.start()
        sends = [remote(a, k, me) for a in range(n) for k in range(3)]
        for cp in sends:
            cp.start()
        for a in range(n):
            for k, (px, py) in enumerate(chips):
                remote(a, k, 2 * px + py).wait_recv()
        for cp in sends:
            cp.wait_send()
        for cp in mine:
            cp.wait()

    return pl.pallas_call(
        body, in_specs=[_ANY] * n, out_specs=[_ANY] * n, out_shape=[SDS((4,) + a.shape, a.dtype) for a in arrs],
        scratch_shapes=[pltpu.SemaphoreType.DMA((n, 3)), pltpu.SemaphoreType.DMA((n, 3)), pltpu.SemaphoreType.DMA((n,))],
        name=name)(*arrs)


def _rs_exchange(parts, *, name):
    n = len(parts)

    def body(*refs):
        ins, gots = refs[:n], refs[n:2 * n]
        send, recv = refs[2 * n:]
        x, y, c, chips = _place()

        def remote(a, k):
            px, py = chips[k]
            return pltpu.make_async_remote_copy(src_ref=ins[a].at[2 * px + py], dst_ref=gots[a].at[k], send_sem=send.at[a, k],
                                                recv_sem=recv.at[a, k], device_id=(px, py, c), device_id_type=MESH)

        sends = [remote(a, k) for a in range(n) for k in range(3)]
        for cp in sends:
            cp.start()
        for cp in sends:
            cp.wait_recv()
        for cp in sends:
            cp.wait_send()

    return pl.pallas_call(
        body, in_specs=[_ANY] * n, out_specs=[_ANY] * n, out_shape=[SDS((3,) + a.shape[1:], a.dtype) for a in parts],
        scratch_shapes=[pltpu.SemaphoreType.DMA((n, 3)), pltpu.SemaphoreType.DMA((n, 3))], name=name)(*parts)


def _gather_split(arrs, *, name):
    n = len(arrs)

    def body(*refs):
        ins, outs = refs[:n], refs[n:2 * n]
        s_ici, r_ici, s_sib, r_sib, loc = refs[2 * n:]
        x, y, c, chips = _place()
        me = 2 * x + y
        ids = [2 * px + py for px, py in chips] + [me]

        def over_ici(a, k, slot):
            px, py = chips[k]
            return pltpu.make_async_remote_copy(src_ref=ins[a].at[c], dst_ref=outs[a].at[slot, c], send_sem=s_ici.at[a, k],
                                                recv_sem=r_ici.at[a, k], device_id=(px, py, c), device_id_type=MESH)

        def to_sibling(a, k, half, src=None):
            blk = outs[a].at[ids[k], half]
            return pltpu.make_async_remote_copy(src_ref=blk if src is None else src, dst_ref=blk, send_sem=s_sib.at[a, k],
                                                recv_sem=r_sib.at[a, k], device_id=(x, y, 1 - c), device_id_type=MESH)

        sends = [over_ici(a, k, me) for a in range(n) for k in range(3)]
        sends += [to_sibling(a, 3, c, src=ins[a].at[c]) for a in range(n)]
        for cp in sends:
            cp.start()
        mine = [pltpu.make_async_copy(ins[a].at[c], outs[a].at[me, c], loc.at[a]) for a in range(n)]
        for cp in mine:
            cp.start()
        for a in range(n):
            for k in range(3):
                over_ici(a, k, ids[k]).wait_recv()
                fwd = to_sibling(a, k, c)
                fwd.start()
                sends.append(fwd)
        for a in range(n):
            for k in range(4):
                to_sibling(a, k, 1 - c).wait_recv()
        for cp in sends:
            cp.wait_send()
        for cp in mine:
            cp.wait()

    dma = pltpu.SemaphoreType.DMA
    return pl.pallas_call(
        body, in_specs=[_ANY] * n, out_specs=[_ANY] * n, out_shape=[SDS((4,) + a.shape, a.dtype) for a in arrs],
        scratch_shapes=[dma((n, 3)), dma((n, 3)), dma((n, 4)), dma((n, 4)), dma((n,))], name=name)(*arrs)


def _swap_halves(parts, *, name):
    n = len(parts)

    def body(*refs):
        ins, gots = refs[:n], refs[n:2 * n]
        send, recv = refs[2 * n:]
        x, y, c, _ = _place()
        cps = [pltpu.make_async_remote_copy(src_ref=ins[a].at[q, 1 - c], dst_ref=gots[a].at[q], send_sem=send.at[a, q],
                                            recv_sem=recv.at[a, q], device_id=(x, y, 1 - c), device_id_type=MESH)
               for a in range(n) for q in range(4)]
        for cp in cps:
            cp.start()
        for cp in cps:
            cp.wait_recv()
        for cp in cps:
            cp.wait_send()

    dma = pltpu.SemaphoreType.DMA
    return pl.pallas_call(
        body, in_specs=[_ANY] * n, out_specs=[_ANY] * n, out_shape=[SDS((4,) + a.shape[2:], a.dtype) for a in parts],
        scratch_shapes=[dma((n, 4)), dma((n, 4))], name=name)(*parts)


def _sibling_copy(arrs, *, name):
    n = len(arrs)

    def body(*refs):
        ins, outs = refs[:n], refs[n:2 * n]
        send, recv = refs[2 * n:]
        x, y, c, _ = _place()
        cps = [pltpu.make_async_remote_copy(src_ref=ins[a], dst_ref=outs[a], send_sem=send.at[a], recv_sem=recv.at[a],
                                            device_id=(x, y, 1 - c), device_id_type=MESH) for a in range(n)]
        for cp in cps:
            cp.start()
        for cp in cps:
            cp.wait_recv()
        for cp in cps:
            cp.wait_send()

    dma = pltpu.SemaphoreType.DMA
    return pl.pallas_call(
        body, in_specs=[_ANY] * n, out_specs=[_ANY] * n, out_shape=[SDS(a.shape, a.dtype) for a in arrs],
        scratch_shapes=[dma((n,)), dma((n,))], name=name)(*arrs)


def _row_tile(R):
    return R if R <= 256 else next(t for t in (256, 128, 64, 32, 16) if R % t == 0)


def _add2(a, b, *, name):
    R, W = a.shape
    tr = _row_tile(R)
    return _pw(lambda p, q: p.astype(f32) + q.astype(f32), [a, b], [_row(tr, W)] * 2, [SDS((R, W), a.dtype)], [_row(tr, W)],
               (1, R // tr), name=name)[0]


def _sum4(own, got, *, name):
    R, W = own.shape
    tr = _row_tile(R)
    g3 = lambda k: pl.BlockSpec((None, tr, W), lambda j, i: (k, i, 0))
    up = lambda t: t.astype(f32)
    return _pw(lambda a, b, c, d: ((up(a) + up(b)) + up(c)) + up(d), [own, got, got, got], [_row(tr, W), g3(0), g3(1), g3(2)],
               [SDS((R, W), f32)], [_row(tr, W)], (1, R // tr), name=name)[0]


def _adamw(g, w, m, v, *, name):
    R, W = w.shape
    tr = _row_tile(R)

    def fn(gv, wv, mv, vv):
        m2 = ADAM_B1 * mv + (1.0 - ADAM_B1) * gv
        v2 = ADAM_B2 * vv + (1.0 - ADAM_B2) * jnp.square(gv)
        m_hat = m2 / (1.0 - ADAM_B1 ** ADAM_STEP)
        v_hat = v2 / (1.0 - ADAM_B2 ** ADAM_STEP)
        return -ADAM_LR * (m_hat / (jnp.sqrt(v_hat) + ADAM_EPS) + ADAM_WD * wv), m2, v2

    return _pw(fn, [g, w, m, v], [_row(tr, W)] * 4, [SDS((R, W), f32)] * 3, [_row(tr, W)] * 3, (1, R // tr), name=name)


def _to_parts(g, axis):
    shp = g.shape
    g = g.reshape(shp[:axis] + (4, shp[axis] // 4) + shp[axis + 1:])
    return jnp.moveaxis(g, axis, 0)


def _from_parts(pt, axis):
    g = jnp.moveaxis(pt, 0, axis)
    shp = g.shape
    return g.reshape(shp[:axis] + (4 * shp[axis + 1],) + shp[axis + 2:])


def kernel(x, norm1_g, w_in, ret_decay, ret_gn_g, mla_q_norm_g, mla_w_uq, mla_kv_norm_g, mla_w_ukv, s5_a_re, s5_a_im, s5_log_dt, s5_b_re, s5_b_im, s5_c_re, s5_c_im, s5_d, s5_w_glu, w_branch, w_out, norm2_g, ffn_w_gu, ffn_w_down, final_g, loss_target, m_norm1_g, m_w_in, m_ret_decay, m_ret_gn_g, m_mla_q_norm_g, m_mla_w_uq, m_mla_kv_norm_g, m_mla_w_ukv, m_s5_a_re, m_s5_a_im, m_s5_log_dt, m_s5_b_re, m_s5_b_im, m_s5_c_re, m_s5_c_im, m_s5_d, m_s5_w_glu, m_w_branch, m_w_out, m_norm2_g, m_ffn_w_gu, m_ffn_w_down, m_final_g, v_norm1_g, v_w_in, v_ret_decay, v_ret_gn_g, v_mla_q_norm_g, v_mla_w_uq, v_mla_kv_norm_g, v_mla_w_ukv, v_s5_a_re, v_s5_a_im, v_s5_log_dt, v_s5_b_re, v_s5_b_im, v_s5_c_re, v_s5_c_im, v_s5_d, v_s5_w_glu, v_w_branch, v_w_out, v_norm2_g, v_ffn_w_gu, v_ffn_w_down, v_final_g):
    wv = dict(zip(W_NAMES, (norm1_g, w_in, ret_decay, ret_gn_g, mla_q_norm_g, mla_w_uq, mla_kv_norm_g, mla_w_ukv, s5_a_re, s5_a_im,
                            s5_log_dt, s5_b_re, s5_b_im, s5_c_re, s5_c_im, s5_d, s5_w_glu, w_branch, w_out, norm2_g, ffn_w_gu,
                            ffn_w_down, final_g)))
    mv = dict(zip(W_NAMES, (m_norm1_g, m_w_in, m_ret_decay, m_ret_gn_g, m_mla_q_norm_g, m_mla_w_uq, m_mla_kv_norm_g, m_mla_w_ukv,
                            m_s5_a_re, m_s5_a_im, m_s5_log_dt, m_s5_b_re, m_s5_b_im, m_s5_c_re, m_s5_c_im, m_s5_d, m_s5_w_glu,
                            m_w_branch, m_w_out, m_norm2_g, m_ffn_w_gu, m_ffn_w_down, m_final_g)))
    vv = dict(zip(W_NAMES, (v_norm1_g, v_w_in, v_ret_decay, v_ret_gn_g, v_mla_q_norm_g, v_mla_w_uq, v_mla_kv_norm_g, v_mla_w_ukv,
                            v_s5_a_re, v_s5_a_im, v_s5_log_dt, v_s5_b_re, v_s5_b_im, v_s5_c_re, v_s5_c_im, v_s5_d, v_s5_w_glu,
                            v_w_branch, v_w_out, v_norm2_g, v_ffn_w_gu, v_ffn_w_down, v_final_g)))
    big_names = list(BIG)

    my_c = lax.axis_index("c")
    my_chip = 2 * lax.axis_index("x") + lax.axis_index("y")
    shards = [wv[n].astype(bf16) for n in big_names]
    gathered = _gather_split(shards, name="gather_weights")
    big = {n: _from_parts(gt, BIG[n]) for n, gt in zip(big_names, gathered)}
    small = {n: wv[n] for n in SMALL}

    loss_local, dx, layer_grads, d_final = _local_step(x[0], loss_target[0], big, small)
    grads = {n: jnp.stack([layer_grads[l][n] for l in range(DEPTH)]) for n in SMALL if n != 'final_g'}
    grads['final_g'] = d_final

    n_rows = {n: -(-math.prod(wv[n].shape) // 1024) * 8 for n in SMALL}
    used = sum(n_rows.values())
    rows_q = -(-(used + 8) // (4 * 128)) * 128

    def as_rows(d, tail=None):
        blocks = [jnp.pad(d[n].reshape(-1), (0, n_rows[n] * 128 - math.prod(wv[n].shape))).reshape(n_rows[n], 128) for n in SMALL]
        blocks.append(jnp.zeros((8, 128), f32) if tail is None else tail)
        blocks.append(jnp.zeros((4 * rows_q - used - 8, 128), f32))
        return jnp.concatenate(blocks, axis=0)

    loss_rows = jnp.full((8, 128), loss_local, f32)
    parts = [jnp.stack([_to_parts(layer_grads[l][n].astype(bf16), BIG[n] - 1) for l in range(DEPTH)], axis=1) for n in big_names]
    parts.append(as_rows(grads, loss_rows).reshape(4, 2, rows_q // 2, 128))
    n_arr = len(parts)
    two_d = lambda a: a.reshape(-1, a.shape[-1])
    theirs = _swap_halves(parts, name="grad_swap_halves")
    mine = [jnp.where(my_c == 0, p[:, 0], p[:, 1]) for p in parts]
    chip_sums = [_add2(two_d(mine[a]), two_d(theirs[a]), name=f"grad_add2_{a}").reshape(theirs[a].shape) for a in range(n_arr)]
    got = _rs_exchange(chip_sums, name="grad_exchange")

    def pick_chip(s):
        r = s[0]
        for q in range(1, 4):
            r = jnp.where(my_chip == q, s[q], r)
        return r

    own = [pick_chip(s) for s in chip_sums]
    sums = [_sum4(two_d(own[a]), got[a].reshape(3, -1, got[a].shape[-1]), name=f"grad_sum4_{a}") for a in range(n_arr)]
    other = _sibling_copy(sums, name="grad_sibling")
    full = [jnp.stack([jnp.where(my_c == 0, sums[a], other[a]), jnp.where(my_c == 0, other[a], sums[a])]) for a in range(n_arr)]

    out_g, out_d, out_m, out_v = {}, {}, {}, {}
    for a, n in enumerate(big_names):
        shp = wv[n].shape
        res = _adamw(two_d(full[a]), two_d(wv[n]), two_d(mv[n]), two_d(vv[n]), name=f"adamw_{n}")
        out_g[n] = full[a].reshape(shp)
        out_d[n], out_m[n], out_v[n] = [r.reshape(shp) for r in res]
    g_small = _allgather4([full[-1].reshape(rows_q, 128)], name="gather_small_grads")[0].reshape(4 * rows_q, 128)
    loss = g_small[used, 0]
    off = 0
    for n in SMALL:
        shp = wv[n].shape
        k = math.prod(shp)
        flat2 = (k // 128, 128) if k % 128 == 0 else (1, k)
        g_n = g_small[off:off + n_rows[n]].reshape(-1)[:k].reshape(flat2)
        res = _adamw(g_n, wv[n].reshape(flat2), mv[n].reshape(flat2), vv[n].reshape(flat2), name=f"adamw_{n}")
        out_g[n] = g_n.reshape(shp)
        out_d[n], out_m[n], out_v[n] = [r.reshape(shp) for r in res]
        off += n_rows[n]
    return (loss, dx[None], *[out_g[n] for n in W_NAMES], *[out_d[n] for n in W_NAMES], *[out_m[n] for n in W_NAMES],
            *[out_v[n] for n in W_NAMES])
```

```python
import functools
import math

import jax
import jax.numpy as jnp
from jax import lax
from jax.experimental import pallas as pl
from jax.experimental.pallas import tpu as pltpu

f32 = jnp.float32
bf16 = jnp.bfloat16
SDS = jax.ShapeDtypeStruct
MESH = pl.DeviceIdType.MESH

D = 1024
DEPTH = 2
RMS_EPS = 1e-6
GN_EPS = 1e-5
ROPE_THETA = 10000.0
RET_HEADS = 4
RET_DK = 128
RET_DV = 256
RET_CHUNK = 128
MLA_HEADS = 8
MLA_Q_LORA = 384
MLA_KV_LORA = 256
MLA_NOPE = 128
MLA_ROPE = 64
MLA_V = 128
MLA_QW = 256
S5_G = 64
S5_P = 64
S5_C = 16
S5_NJ = 8
S5_SEG = 8
S5_T = 1024
FFN_H = 2816
ADAM_LR = 0.001
ADAM_B1 = 0.9
ADAM_B2 = 0.999
ADAM_EPS = 1e-08
ADAM_WD = 0.01
ADAM_STEP = 10
VMEM_BIG = 56 * 1024 * 1024

W_NAMES = ['norm1_g', 'w_in', 'ret_decay', 'ret_gn_g', 'mla_q_norm_g', 'mla_w_uq', 'mla_kv_norm_g', 'mla_w_ukv',
           's5_a_re', 's5_a_im', 's5_log_dt', 's5_b_re', 's5_b_im', 's5_c_re', 's5_c_im', 's5_d', 's5_w_glu',
           'w_branch', 'w_out', 'norm2_g', 'ffn_w_gu', 'ffn_w_down', 'final_g']
BIG = {'w_in': 2, 'mla_w_uq': 2, 'mla_w_ukv': 2, 's5_w_glu': 2, 'w_branch': 2, 'w_out': 1, 'ffn_w_gu': 2, 'ffn_w_down': 1}
SMALL = [n for n in W_NAMES if n not in BIG]


TILE_BYTES = 6 * 1024 * 1024


def _pick(n, cands=(512, 384, 256, 128), cap=None):
    if n <= 1024 and (cap is None or n <= cap):
        return n
    for c in cands:
        if n % c == 0 and (cap is None or c <= cap):
            return c
    raise ValueError(n)


WIDE = (1408, 1024, 768, 512, 384, 256, 128)


def _params(sem, vmem=None):
    return pltpu.CompilerParams(dimension_semantics=sem, vmem_limit_bytes=vmem)


def _mm(a, b, *, tb=False, res=None, out_dtype=f32, name):
    M, K = a.shape
    N = b.shape[0] if tb else b.shape[1]
    tk = K if K <= 3072 else _pick(K, (1408, 1024, 512))
    nk = K // tk
    tn = _pick(N, WIDE, cap=TILE_BYTES // (tk * b.dtype.itemsize))
    tm = _pick(M)
    if M % 1024 == 0 and 1024 * tk * a.dtype.itemsize <= 4 * 1024 * 1024 and 1024 * tn * 4 <= TILE_BYTES:
        tm = 1024
    assert M % tm == 0 and N % tn == 0 and K % tk == 0

    def body(*refs):
        if res is None:
            a_ref, b_ref, o_ref, acc = refs
        else:
            a_ref, b_ref, r_ref, o_ref, acc = refs
        k = pl.program_id(2)
        dn = (((1,), (1 if tb else 0,)), ((), ()))
        part = lax.dot_general(a_ref[...].astype(bf16), b_ref[...].astype(bf16), dn, preferred_element_type=f32)

        @pl.when(k == 0)
        def _():
            acc[...] = part

        @pl.when(k > 0)
        def _():
            acc[...] += part

        @pl.when(k == nk - 1)
        def _():
            v = acc[...]
            if res is not None:
                v = v + r_ref[...]
            o_ref[...] = v.astype(out_dtype)

    in_specs = [pl.BlockSpec((tm, tk), lambda i, j, k: (i, k)),
                pl.BlockSpec((tn, tk), lambda i, j, k: (j, k)) if tb else pl.BlockSpec((tk, tn), lambda i, j, k: (k, j))]
    args = [a, b]
    if res is not None:
        in_specs.append(pl.BlockSpec((tm, tn), lambda i, j, k: (i, j)))
        args.append(res)
    return pl.pallas_call(
        body, grid=(M // tm, N // tn, nk), in_specs=in_specs,
        out_specs=pl.BlockSpec((tm, tn), lambda i, j, k: (i, j)),
        out_shape=SDS((M, N), out_dtype), scratch_shapes=[pltpu.VMEM((tm, tn), f32)],
        compiler_params=_params(("parallel", "parallel", "arbitrary"), VMEM_BIG), name=name)(*args)


def _mmT(a, b, *, name):
    S, M = a.shape
    N = b.shape[1]
    tn = _pick(N, WIDE)
    tm = _pick(M, WIDE, cap=TILE_BYTES // (tn * 4))
    tk = min(S, 1024)
    nk = S // tk

    def body(a_ref, b_ref, o_ref):
        k = pl.program_id(2)
        part = lax.dot_general(a_ref[...].astype(bf16), b_ref[...].astype(bf16), (((0,), (0,)), ((), ())),
                               preferred_element_type=f32)

        @pl.when(k == 0)
        def _():
            o_ref[...] = part

        @pl.when(k > 0)
        def _():
            o_ref[...] += part

    return pl.pallas_call(
        body, grid=(M // tm, N // tn, nk),
        in_specs=[pl.BlockSpec((tk, tm), lambda i, j, k: (k, i)), pl.BlockSpec((tk, tn), lambda i, j, k: (k, j))],
        out_specs=pl.BlockSpec((tm, tn), lambda i, j, k: (i, j)),
        out_shape=SDS((M, N), f32),
        compiler_params=_params(("parallel", "parallel", "arbitrary"), VMEM_BIG), name=name)(a, b)


def _pw(fn, ins, in_specs, outs, out_specs, grid, *, n_acc=0, name):
    n_in = len(ins)
    n_out = len(outs)

    def body(*refs):
        vals = fn(*[r[...].astype(f32) if r.dtype == bf16 else r[...] for r in refs[:n_in]])
        if not isinstance(vals, (tuple, list)):
            vals = (vals,)
        orefs = refs[n_in:]
        for r, v in zip(orefs[:n_out - n_acc], vals[:n_out - n_acc]):
            r[...] = v.astype(r.dtype)
        if n_acc:
            i = pl.program_id(1)

            @pl.when(i == 0)
            def _():
                for r, v in zip(orefs[n_out - n_acc:], vals[n_out - n_acc:]):
                    r[...] = v.astype(r.dtype)

            @pl.when(i > 0)
            def _():
                for r, v in zip(orefs[n_out - n_acc:], vals[n_out - n_acc:]):
                    r[...] += v.astype(r.dtype)

    res = pl.pallas_call(
        body, grid=grid, in_specs=in_specs, out_specs=out_specs, out_shape=outs,
        compiler_params=_params(("parallel", "arbitrary"), VMEM_BIG), name=name)(*ins)
    return res


def _row(T, w, col=None):
    if col is None:
        return pl.BlockSpec((T, w), lambda j, i: (i, 0))
    return pl.BlockSpec((T, w), lambda j, i: (i, col(j)))


def _par(w, col=None):
    if col is None:
        return pl.BlockSpec((1, w), lambda j, i: (0, 0))
    return pl.BlockSpec((1, w), lambda j, i: (0, col(j)))


def _rms(x, g):
    return x * lax.rsqrt(jnp.mean(x * x, axis=-1, keepdims=True) + RMS_EPS) * g


def _rope(x, cos, sinm, half):
    if half == 64:
        partner = pltpu.roll(x, 64, axis=1)
    else:
        lane = lax.broadcasted_iota(jnp.int32, x.shape, 1)
        partner = jnp.where((lane % (2 * half)) < half, pltpu.roll(x, 128 - half, axis=1), pltpu.roll(x, half, axis=1))
    return x * cos + partner * sinm


def _rope_t(x, cos, sinm, half):
    return _rope(x, cos, -sinm, half)


def _rmsnorm_fwd(x, g, *, name):
    S, W = x.shape
    T = min(S, 1024)
    return _pw(lambda xv, gv: _rms(xv, gv), [x, g], [_row(T, W), _par(W)], [SDS((S, W), bf16)], [_row(T, W)],
               (1, S // T), name=name)[0]


def _rmsnorm_bwd(x, g, dh, dres, *, name):
    S, W = x.shape
    T = min(S, 512)

    def fn(xv, gv, dhv, drv):
        _, vjp = jax.vjp(_rms, xv, gv)
        dx, dg = vjp(dhv)
        return dx + drv, dg

    return _pw(fn, [x, g, dh, dres], [_row(T, W), _par(W), _row(T, W), _row(T, W)],
               [SDS((S, W), f32), SDS((1, W), f32)], [_row(T, W), _par(W)], (1, S // T), n_acc=1, name=name)


def _ret_tables(lg, reverse):
    C = RET_CHUNK
    ii = lax.broadcasted_iota(jnp.int32, (C, C), 0).astype(f32)
    jj = lax.broadcasted_iota(jnp.int32, (C, C), 1).astype(f32)
    if not reverse:
        E = ii - jj
        mask = E >= 0
        eq = ii + 1.0
        ek = (C - 1.0) - ii
    else:
        E = jj - ii
        mask = E > 0
        eq = C - ii
        ek = ii
    Dm = jnp.where(mask, jnp.exp(jnp.where(mask, E, 0.0) * lg), 0.0)
    Em = jnp.where(mask, E, 0.0)
    qw = jnp.exp(eq * lg)
    kw = jnp.exp(ek * lg)
    qw2 = jnp.concatenate([qw, qw], axis=1)
    return Dm, Em, eq, ek, qw, kw, qw2, jnp.exp(C * lg)


def _dot(a, b, dims):
    return lax.dot_general(a.astype(bf16), b.astype(bf16), (dims, ((), ())), preferred_element_type=f32)


NN = ((1,), (0,))
NT = ((1,), (1,))
TN = ((0,), (0,))


def _ret_dir_fwd(zr, lg, cos, sinm, *, reverse, name):
    S = zr.shape[0]
    C = RET_CHUNK
    TB = min(S, 1024)
    nc = TB // C
    NB = S // TB
    d = 1 if reverse else 0
    scale = RET_DK ** -0.5

    def tb(b):
        return (NB - 1 - b) if reverse else b

    def body(lg_ref, q_ref, k_ref, v_ref, cos_ref, sin_ref, y_ref, st_ref, state):
        h = pl.program_id(0)
        b = pl.program_id(1)

        @pl.when(b == 0)
        def _():
            state[...] = jnp.zeros_like(state)

        Dm, _, _, _, _, kw, qw2, gC = _ret_tables(lg_ref[d, h], reverse)
        order = range(nc - 1, -1, -1) if reverse else range(nc)
        for c in order:
            rows = pl.ds(c * C, C)
            q = _rope(q_ref[rows, :], cos_ref[rows, :], sin_ref[rows, :], 64) * scale
            k = _rope(k_ref[rows, :], cos_ref[rows, :], sin_ref[rows, :], 64)
            v = v_ref[rows, :]
            st = state[...]
            st_ref[0, c] = st
            s = _dot(q, k, NT) * Dm
            o = _dot(s, v, NN) + _dot(q, st, NN) * qw2
            y_ref[rows, :] = o
            state[...] = gC * st + _dot(k * kw, v, TN)

    return pl.pallas_call(
        body, grid=(RET_HEADS, NB),
        in_specs=[pl.BlockSpec(memory_space=pltpu.SMEM),
                  pl.BlockSpec((TB, 128), lambda h, b: (tb(b), h)),
                  pl.BlockSpec((TB, 128), lambda h, b: (tb(b), 4 + h)),
                  pl.BlockSpec((TB, 256), lambda h, b: (tb(b), 4 + h)),
                  pl.BlockSpec((TB, 128), lambda h, b: (tb(b), 0)),
                  pl.BlockSpec((TB, 128), lambda h, b: (tb(b), 0))],
        out_specs=[pl.BlockSpec((TB, 256), lambda h, b: (tb(b), h)),
                   pl.BlockSpec((1, nc, 128, 256), lambda h, b: (h, tb(b), 0, 0))],
        out_shape=[SDS((S, 1024), f32), SDS((RET_HEADS, S // C, 128, 256), f32)],
        scratch_shapes=[pltpu.VMEM((128, 256), f32)],
        compiler_params=_params(("parallel", "arbitrary")), name=name)(lg, zr, zr, zr, cos, sinm)


def _ret_dir_bwd(zr, lg, cos, sinm, dy, states, *, reverse, name):
    S = zr.shape[0]
    C = RET_CHUNK
    TB = min(S, 1024)
    nc = TB // C
    NB = S // TB
    d = 1 if reverse else 0
    scale = RET_DK ** -0.5

    def tb(b):
        return b if reverse else (NB - 1 - b)

    def body(lg_ref, q_ref, k_ref, v_ref, cos_ref, sin_ref, dy_ref, st_ref, dq_ref, dk_ref, dv_ref, dlg_ref, dstate):
        h = pl.program_id(0)
        b = pl.program_id(1)

        @pl.when(b == 0)
        def _():
            dstate[...] = jnp.zeros_like(dstate)
            dlg_ref[...] = jnp.zeros_like(dlg_ref)

        Dm, Em, eq, ek, qw, kw, qw2, gC = _ret_tables(lg_ref[d, h], reverse)
        order = range(nc) if reverse else range(nc - 1, -1, -1)
        dlg = jnp.zeros((), f32)
        for c in order:
            rows = pl.ds(c * C, C)
            cs, sn = cos_ref[rows, :], sin_ref[rows, :]
            q = _rope(q_ref[rows, :], cs, sn, 64) * scale
            k = _rope(k_ref[rows, :], cs, sn, 64)
            v = v_ref[rows, :]
            do = dy_ref[rows, :]
            st = st_ref[0, c]
            ds = dstate[...]
            p = _dot(q, k, NT)
            a = p * Dm
            dp = _dot(do, v, NT) * Dm
            dq_cross = _dot(do, st, NT) * qw
            dk_cross = _dot(v, ds, NT) * kw
            dq = _dot(dp, k, NN) + dq_cross
            dk = _dot(dp, q, TN) + dk_cross
            dv = _dot(a, do, TN) + _dot(k * kw, ds, NN)
            dlg = dlg + jnp.sum(dp * p * Em) + jnp.sum(dq_cross * q * eq) + jnp.sum(dk_cross * k * ek) \
                + C * gC * jnp.sum(ds * st)
            dstate[...] = gC * ds + _dot(q * qw, do, TN)
            dq_ref[rows, :] = _rope_t(dq, cs, sn, 64) * scale
            dk_ref[rows, :] = _rope_t(dk, cs, sn, 64)
            dv_ref[rows, :] = dv
        dlg_ref[...] += jnp.full(dlg_ref.shape, dlg, f32)

    return pl.pallas_call(
        body, grid=(RET_HEADS, NB),
        in_specs=[pl.BlockSpec(memory_space=pltpu.SMEM),
                  pl.BlockSpec((TB, 128), lambda h, b: (tb(b), h)),
                  pl.BlockSpec((TB, 128), lambda h, b: (tb(b), 4 + h)),
                  pl.BlockSpec((TB, 256), lambda h, b: (tb(b), 4 + h)),
                  pl.BlockSpec((TB, 128), lambda h, b: (tb(b), 0)),
                  pl.BlockSpec((TB, 128), lambda h, b: (tb(b), 0)),
                  pl.BlockSpec((TB, 256), lambda h, b: (tb(b), h)),
                  pl.BlockSpec((1, nc, 128, 256), lambda h, b: (h, tb(b), 0, 0))],
        out_specs=[pl.BlockSpec((TB, 128), lambda h, b: (tb(b), h)),
                   pl.BlockSpec((TB, 128), lambda h, b: (tb(b), h)),
                   pl.BlockSpec((TB, 256), lambda h, b: (tb(b), h)),
                   pl.BlockSpec((1, 1, 128), lambda h, b: (h, 0, 0))],
        out_shape=[SDS((S, 512), f32), SDS((S, 512), f32), SDS((S, 1024), f32), SDS((RET_HEADS, 1, 128), f32)],
        scratch_shapes=[pltpu.VMEM((128, 256), f32)],
        compiler_params=_params(("parallel", "arbitrary")), name=name)(lg, zr, zr, zr, cos, sinm, dy, states)


def _gn_gate(yf, yb, g, gn):
    y = yf + yb
    mu = jnp.mean(y, axis=-1, keepdims=True)
    var = jnp.mean(jnp.square(y - mu), axis=-1, keepdims=True)
    yn = (y - mu) * lax.rsqrt(var + GN_EPS)
    return jax.nn.silu(g) * (yn * gn)


def _flash_fwd(Q, K, kv, *, name):
    S = Q.shape[0]
    hq = min(S, 256)
    nh = 4 if S % 1024 == 0 else 1
    tq = nh * hq
    tk = min(S, 8192)
    nk = S // tk

    def body(q_ref, k_ref, v_ref, o_ref, l_ref, m_s, l_s, acc):
        kk = pl.program_id(2)

        @pl.when(kk == 0)
        def _():
            m_s[...] = jnp.full_like(m_s, -jnp.inf)
            l_s[...] = jnp.zeros_like(l_s)
            acc[...] = jnp.zeros_like(acc)

        k = k_ref[...]
        v = v_ref[...]
        sts = [lax.dot_general(k, q_ref[hf * hq:(hf + 1) * hq, :], (NT, ((), ())), preferred_element_type=f32)
               for hf in range(nh)]
        for hf in range(nh):
            st = sts[hf]
            m_prev = m_s[hf]
            m_new = jnp.maximum(m_prev, jnp.max(st, axis=0, keepdims=True))
            pt = jnp.exp2(st - m_new)
            alpha = jnp.exp2(m_prev - m_new)
            l_s[hf] = alpha * l_s[hf] + jnp.sum(pt, axis=0, keepdims=True)
            acc[hf] = alpha * acc[hf] + lax.dot_general(v, pt.astype(bf16), (TN, ((), ())), preferred_element_type=f32)
            m_s[hf] = m_new

        @pl.when(kk == nk - 1)
        def _():
            for hf in range(nh):
                o_ref[hf * hq:(hf + 1) * hq, :] = jnp.transpose(acc[hf] / l_s[hf]).astype(bf16)
                l_ref[0, :, hf * hq:(hf + 1) * hq] = m_s[hf] + jnp.log2(l_s[hf])

    return pl.pallas_call(
        body, grid=(MLA_HEADS, S // tq, nk),
        in_specs=[pl.BlockSpec((tq, 256), lambda h, i, k: (i, h)),
                  pl.BlockSpec((tk, 256), lambda h, i, k: (k, h)),
                  pl.BlockSpec((tk, 128), lambda h, i, k: (k, 2 * h + 1))],
        out_specs=[pl.BlockSpec((tq, 128), lambda h, i, k: (i, h)), pl.BlockSpec((1, 1, tq), lambda h, i, k: (h, 0, i))],
        out_shape=[SDS((S, 1024), bf16), SDS((MLA_HEADS, 1, S), f32)],
        scratch_shapes=[pltpu.VMEM((nh, 1, hq), f32), pltpu.VMEM((nh, 1, hq), f32), pltpu.VMEM((nh, 128, hq), f32)],
        compiler_params=_params(("parallel", "parallel", "arbitrary"), VMEM_BIG), name=name)(Q, K, kv)


def _attn_delta(dO, O, *, name):
    S = dO.shape[0]
    T = min(S, 512)

    def body(do_ref, o_ref, d_ref):
        ones = jnp.ones((8, 128), bf16)
        for h in range(MLA_HEADS):
            cols = slice(128 * h, 128 * h + 128)
            prod = do_ref[:, cols].astype(f32) * o_ref[:, cols].astype(f32)
            hi = prod.astype(bf16)
            lo = (prod - hi.astype(f32)).astype(bf16)
            row = lax.dot_general(ones, hi, (NT, ((), ())), preferred_element_type=f32) \
                + lax.dot_general(ones, lo, (NT, ((), ())), preferred_element_type=f32)
            d_ref[h] = row[0:1, :]

    return pl.pallas_call(
        body, grid=(S // T,),
        in_specs=[pl.BlockSpec((T, 1024), lambda i: (i, 0)), pl.BlockSpec((T, 1024), lambda i: (i, 0))],
        out_specs=pl.BlockSpec((MLA_HEADS, 1, T), lambda i: (0, 0, i)), out_shape=SDS((MLA_HEADS, 1, S), f32),
        compiler_params=_params(("parallel",)), name=name)(dO, O)


def _flash_bwd(Q, K, kv, delta, L, dO, *, name):
    S = Q.shape[0]
    hq = min(S, 512)
    nh = 2 if S % 1024 == 0 else 1
    tq = nh * hq
    tk = min(S, 2048)
    nq = S // tq
    ln2 = math.log(2.0)

    def body(q_ref, k_ref, v_ref, dl_ref, l_ref, do_ref, dq_ref, dk_ref, dv_ref, dk_acc, dv_acc):
        kk = pl.program_id(1)
        i = pl.program_id(2)

        @pl.when((kk == 0) & (i == 0))
        def _():
            dq_ref[...] = jnp.zeros_like(dq_ref)

        @pl.when(i == 0)
        def _():
            dk_acc[...] = jnp.zeros_like(dk_acc)
            dv_acc[...] = jnp.zeros_like(dv_acc)

        k = k_ref[...]
        v = v_ref[...]
        dk_new = dk_acc[...]
        dv_new = dv_acc[...]
        for hf in range(nh):
            sl = slice(hf * hq, (hf + 1) * hq)
            q = q_ref[sl, :]
            st = lax.dot_general(k, q, (NT, ((), ())), preferred_element_type=f32)
            pt = jnp.exp2(st - l_ref[0, :, sl])
            delta = dl_ref[0, :, sl]
            dob = do_ref[sl, :].astype(bf16)
            dv_new = dv_new + lax.dot_general(pt.astype(bf16), dob, (NN, ((), ())), preferred_element_type=f32)
            dpt = lax.dot_general(v, dob, (NT, ((), ())), preferred_element_type=f32)
            dst = (pt * (dpt - delta)).astype(bf16)
            dk_new = dk_new + lax.dot_general(dst, q, (NN, ((), ())), preferred_element_type=f32)
            dq_ref[0, i * nh + hf] += lax.dot_general(k, dst, (TN, ((), ())), preferred_element_type=f32)
        dk_acc[...] = dk_new
        dv_acc[...] = dv_new

        @pl.when(i == nq - 1)
        def _():
            dk_ref[...] = dk_acc[...] * ln2
            dv_ref[...] = dv_acc[...]

    return pl.pallas_call(
        body, grid=(MLA_HEADS, S // tk, nq),
        in_specs=[pl.BlockSpec((tq, 256), lambda h, k, i: (i, h)),
                  pl.BlockSpec((tk, 256), lambda h, k, i: (k, h)),
                  pl.BlockSpec((tk, 128), lambda h, k, i: (k, 2 * h + 1)),
                  pl.BlockSpec((1, 1, tq), lambda h, k, i: (h, 0, i)),
                  pl.BlockSpec((1, 1, tq), lambda h, k, i: (h, 0, i)),
                  pl.BlockSpec((tq, 128), lambda h, k, i: (i, h))],
        out_specs=[pl.BlockSpec((1, S // hq, 256, hq), lambda h, k, i: (h, 0, 0, 0)),
                   pl.BlockSpec((tk, 256), lambda h, k, i: (k, h)),
                   pl.BlockSpec((tk, 128), lambda h, k, i: (k, h))],
        out_shape=[SDS((MLA_HEADS, S // hq, 256, hq), f32), SDS((S, 2048), f32), SDS((S, 1024), f32)],
        scratch_shapes=[pltpu.VMEM((tk, 256), f32), pltpu.VMEM((tk, 128), f32)],
        compiler_params=_params(("parallel", "arbitrary", "arbitrary"), VMEM_BIG), name=name)(Q, K, kv, delta, L, dO)


def _mla_qk_prep(q, kv, zm, cosm, sinm, *, name):
    S = q.shape[0]
    T = min(S, 512)
    scale = (MLA_NOPE + MLA_ROPE) ** -0.5 * math.log2(math.e)

    def body(q_ref, kv_ref, kr_ref, cos_ref, sin_ref, oq_ref, ok_ref):
        cs, sn = cos_ref[...], sin_ref[...]
        kr = _rope(kr_ref[...], cs, sn, 32).astype(bf16)
        for h in range(MLA_HEADS):
            a = 256 * h
            oq_ref[:, a:a + 128] = (q_ref[:, a:a + 128].astype(f32) * scale).astype(bf16)
            oq_ref[:, a + 128:a + 256] = (_rope(q_ref[:, a + 128:a + 256].astype(f32), cs, sn, 32) * scale).astype(bf16)
            ok_ref[:, a:a + 128] = kv_ref[:, a:a + 128]
            ok_ref[:, a + 128:a + 256] = kr

    row = lambda w, col=0: pl.BlockSpec((T, w), lambda i: (i, col))
    return pl.pallas_call(
        body, grid=(S // T,), in_specs=[row(2048), row(2048), row(128, 6), row(128), row(128)],
        out_specs=[row(2048), row(2048)], out_shape=[SDS((S, 2048), bf16), SDS((S, 2048), bf16)],
        compiler_params=_params(("parallel",), VMEM_BIG), name=name)(q, kv, zm, cosm, sinm)


def _mla_bwd_prep(dQ, dK, dV, cosm, sinm, *, name):
    S = dK.shape[0]
    T = dQ.shape[3]
    scale = (MLA_NOPE + MLA_ROPE) ** -0.5

    def body(dq_ref, dk_ref, dv_ref, cos_ref, sin_ref, oq_ref, okv_ref, okr_ref):
        cs, sn = cos_ref[...], sin_ref[...]
        kr = jnp.zeros((T, 128), f32)
        for h in range(MLA_HEADS):
            a = 256 * h
            dq = jnp.transpose(dq_ref[h, 0])
            oq_ref[:, a:a + 128] = (dq[:, 0:128] * scale).astype(bf16)
            oq_ref[:, a + 128:a + 256] = (_rope_t(dq[:, 128:256], cs, sn, 32) * scale).astype(bf16)
            okv_ref[:, a:a + 128] = dk_ref[:, a:a + 128].astype(bf16)
            okv_ref[:, a + 128:a + 256] = dv_ref[:, 128 * h:128 * h + 128].astype(bf16)
            kr = kr + dk_ref[:, a + 128:a + 256]
        okr_ref[...] = _rope_t(kr, cs, sn, 32)

    return pl.pallas_call(
        body, grid=(S // T,),
        in_specs=[pl.BlockSpec((MLA_HEADS, 1, 256, T), lambda i: (0, i, 0, 0)), pl.BlockSpec((T, 2048), lambda i: (i, 0)),
                  pl.BlockSpec((T, 1024), lambda i: (i, 0)), pl.BlockSpec((T, 128), lambda i: (i, 0)),
                  pl.BlockSpec((T, 128), lambda i: (i, 0))],
        out_specs=[pl.BlockSpec((T, 2048), lambda i: (i, 0)), pl.BlockSpec((T, 2048), lambda i: (i, 0)),
                   pl.BlockSpec((T, 128), lambda i: (i, 0))],
        out_shape=[SDS((S, 2048), bf16), SDS((S, 2048), bf16), SDS((S, 128), f32)],
        compiler_params=_params(("parallel",), VMEM_BIG), name=name)(dQ, dK, dV, cosm, sinm)


def _mla_norm_bwd(zm, qg, kvg, dcqn, dckvn, dkr, *, name):
    S = zm.shape[0]
    T = min(S, 512)

    def body(cq_ref, ckv_ref, qg_ref, kvg_ref, dcq_ref, dckv_ref, dkr_ref, o_ref, dqg_ref, dkvg_ref):
        i = pl.program_id(0)
        _, vjp = jax.vjp(_rms, cq_ref[...], qg_ref[...])
        dcq, dqg = vjp(dcq_ref[...])
        _, vjp2 = jax.vjp(_rms, ckv_ref[...], kvg_ref[...])
        dckv, dkvg = vjp2(dckv_ref[...])
        o_ref[:, 0:384] = dcq.astype(bf16)
        o_ref[:, 384:512] = jnp.zeros((T, 128), bf16)
        o_ref[:, 512:768] = dckv.astype(bf16)
        o_ref[:, 768:896] = dkr_ref[...].astype(bf16)

        @pl.when(i == 0)
        def _():
            dqg_ref[...] = dqg
            dkvg_ref[...] = dkvg

        @pl.when(i > 0)
        def _():
            dqg_ref[...] += dqg
            dkvg_ref[...] += dkvg

    return pl.pallas_call(
        body, grid=(S // T,),
        in_specs=[pl.BlockSpec((T, 384), lambda i: (i, 0)), pl.BlockSpec((T, 256), lambda i: (i, 2)),
                  pl.BlockSpec((1, 384), lambda i: (0, 0)), pl.BlockSpec((1, 256), lambda i: (0, 0)),
                  pl.BlockSpec((T, 384), lambda i: (i, 0)), pl.BlockSpec((T, 256), lambda i: (i, 0)),
                  pl.BlockSpec((T, 128), lambda i: (i, 0))],
        out_specs=[pl.BlockSpec((T, 896), lambda i: (i, 0)), pl.BlockSpec((1, 384), lambda i: (0, 0)),
                   pl.BlockSpec((1, 256), lambda i: (0, 0))],
        out_shape=[SDS((S, 896), bf16), SDS((1, 384), f32), SDS((1, 256), f32)],
        compiler_params=_params(("arbitrary",)), name=name)(zm, zm, qg, kvg, dcqn, dckvn, dkr)


def _s5_disc(a_re, a_im, ldt, b_re, b_im):
    dt = jnp.exp(ldt)
    ar = jnp.minimum(a_re, -1e-4)
    mag = jnp.exp(dt * ar)
    abr = mag * jnp.cos(dt * a_im)
    abi = mag * jnp.sin(dt * a_im)
    den = ar * ar + a_im * a_im
    nr = abr - 1.0
    ni = abi
    cr = (nr * ar + ni * a_im) / den
    ci = (ni * ar - nr * a_im) / den
    return abr, abi, cr * b_re - ci * b_im, cr * b_im + ci * b_re


def _s5_param_fwd(a_re, a_im, ldt, b_re, b_im, *, name):
    R = SDS((1, 8192), f32)
    M = SDS((16, 8192), f32)
    LP = S5_T // S5_SEG
    Pw = SDS((LP, 8192), f32)

    def body(a_re_r, a_im_r, ldt_r, b_re_r, b_im_r, o1, o2, o3, o4, p_re, p_im):
        abr, abi, bbr, bbi = _s5_disc(a_re_r[...], a_im_r[...], ldt_r[...], b_re_r[...], b_im_r[...])
        o1[...] = abr
        o2[...] = abi
        o3[...] = bbr
        o4[...] = bbi
        dt = jnp.exp(ldt_r[...])
        ar = jnp.minimum(a_re_r[...], -1e-4)
        n = lax.broadcasted_iota(jnp.int32, (LP, 8192), 0).astype(f32) + 1.0
        mag = jnp.exp(n * (dt * ar))
        ang = n * (dt * a_im_r[...])
        p_re[...] = mag * jnp.cos(ang)
        p_im[...] = mag * jnp.sin(ang)

    return pl.pallas_call(body, out_shape=[R, R, M, M, Pw, Pw], name=name)(a_re, a_im, ldt, b_re, b_im)


def _s5_param_bwd(a_re, a_im, ldt, b_re, b_im, d_abr, d_abi, d_bbr, d_bbi, *, name):
    R = SDS((1, 8192), f32)
    M = SDS((16, 8192), f32)

    def body(a_re_r, a_im_r, ldt_r, b_re_r, b_im_r, c1, c2, c3, c4, o1, o2, o3, o4, o5):
        _, vjp = jax.vjp(_s5_disc, a_re_r[...], a_im_r[...], ldt_r[...], b_re_r[...], b_im_r[...])
        g = vjp((c1[...], c2[...], c3[...], c4[...]))
        for o, v in zip((o1, o2, o3, o4, o5), g):
            o[...] = v

    return pl.pallas_call(body, out_shape=[R, R, R, M, M], name=name)(a_re, a_im, ldt, b_re, b_im, d_abr, d_abi, d_bbr, d_bbi)


def _seg_perm(T, inverse):
    L = T // S5_SEG
    i = jnp.arange(T)
    src = (i % S5_SEG) * L + i // S5_SEG
    P = (src[:, None] == jnp.arange(T)[None, :]).astype(bf16)
    return P.T if inverse else P


def _perm_rows(a, P, *, name):
    S, W = a.shape
    T = P.shape[0]

    def body(p_ref, a_ref, o_ref):
        o_ref[...] = lax.dot_general(p_ref[...], a_ref[...].astype(bf16), (NN, ((), ())),
                                     preferred_element_type=f32).astype(o_ref.dtype)

    return pl.pallas_call(
        body, grid=(S // T,), in_specs=[pl.BlockSpec((T, T), lambda i: (0, 0)), pl.BlockSpec((T, W), lambda i: (i, 0))],
        out_specs=pl.BlockSpec((T, W), lambda i: (i, 0)), out_shape=SDS((S, W), bf16),
        compiler_params=_params(("parallel",), VMEM_BIG), name=name)(P, a)


def _scan_core(xr, xi, ar, ai, pwr_ref, pwi_ref, a64r, a64i, carry, *, reverse, T, conj):
    L = T // S5_SEG
    sg = -1.0 if conj else 1.0
    arb = jnp.broadcast_to(ar, (8, 512))
    aib = jnp.broadcast_to(ai, (8, 512))
    UN = 4

    def step(r4, c):
        cr, ci = c
        for u in range(UN):
            r0 = r4 * UN + u
            r = (L - 1 - r0) if reverse else r0
            rows = pl.ds(pl.multiple_of(r * 8, 8), 8)
            nr = arb * cr - aib * ci + xr[rows, :]
            ni = arb * ci + aib * cr + xi[rows, :]
            xr[rows, :] = nr
            xi[rows, :] = ni
            cr, ci = nr, ni
        return cr, ci

    lr, li = lax.fori_loop(0, L // UN, step, (jnp.zeros((8, 512), f32), jnp.zeros((8, 512), f32)))
    row8 = lax.broadcasted_iota(jnp.int32, (8, 512), 0)
    cr = carry[0, 0:1, :]
    ci = carry[1, 0:1, :]
    a6i = sg * a64i
    cin_r = jnp.zeros((8, 512), f32)
    cin_i = jnp.zeros((8, 512), f32)
    for seg in (range(S5_SEG - 1, -1, -1) if reverse else range(S5_SEG)):
        cin_r = jnp.where(row8 == seg, cr, cin_r)
        cin_i = jnp.where(row8 == seg, ci, cin_i)
        ncr = lr[seg:seg + 1, :] + a64r * cr - a6i * ci
        nci = li[seg:seg + 1, :] + a64r * ci + a6i * cr
        cr, ci = ncr, nci
    carry[0, 0:1, :] = cr
    carry[1, 0:1, :] = ci

    def fix(r4, _):
        for u in range(UN):
            r = r4 * UN + u
            rows = pl.ds(pl.multiple_of(r * 8, 8), 8)
            pr = pwr_ref[pl.ds(r, 1), :]
            pi = sg * pwi_ref[pl.ds(r, 1), :]
            xr[rows, :] += pr * cin_r - pi * cin_i
            xi[rows, :] += pr * cin_i + pi * cin_r
        return 0

    lax.fori_loop(0, L // UN, fix, 0)


def _s5_scan_fwd(u, BBr, BBi, CCr, CCi, abr, abi, pwr, pwi, *, reverse, name):
    S = u.shape[0]
    T = S5_T
    NB = S // T
    L = T // S5_SEG
    d = 1 if reverse else 0

    def tb(b):
        return (NB - 1 - b) if reverse else b

    def body(u_ref, bbr_ref, bbi_ref, ccr_ref, cci_ref, ar_ref, ai_ref, pwr_ref, pwi_ref, y_ref, xr_ref, xi_ref, carry):
        b = pl.program_id(1)

        @pl.when(b == 0)
        def _():
            carry[...] = jnp.zeros_like(carry)

        ub = u_ref[...].astype(bf16)
        xr_ref[...] = lax.dot_general(ub, bbr_ref[0, 0], (NN, ((), ())), preferred_element_type=f32)
        xi_ref[...] = lax.dot_general(ub, bbi_ref[0, 0], (NN, ((), ())), preferred_element_type=f32)
        a6 = (0 if reverse else L - 1)
        _scan_core(xr_ref, xi_ref, ar_ref[...], ai_ref[...], pwr_ref, pwi_ref, pwr_ref[a6:a6 + 1, :], pwi_ref[a6:a6 + 1, :],
                   carry, reverse=reverse, T=T, conj=False)
        y_ref[...] = _dot(xr_ref[...], ccr_ref[0, 0], NN) - _dot(xi_ref[...], cci_ref[0, 0], NN)

    mat = lambda shp: pl.BlockSpec((1, 1) + shp, lambda j, b: (d, j, 0, 0))
    vec = lambda r: pl.BlockSpec((r, 512), lambda j, b: (0, d * S5_NJ + j))
    return pl.pallas_call(
        body, grid=(S5_NJ, NB),
        in_specs=[pl.BlockSpec((T, 128), lambda j, b: (tb(b), j)), mat((128, 512)), mat((128, 512)), mat((512, 128)),
                  mat((512, 128)), vec(1), vec(1), vec(L), vec(L)],
        out_specs=[pl.BlockSpec((T, 128), lambda j, b: (tb(b), j)), pl.BlockSpec((T, 512), lambda j, b: (tb(b), j)),
                   pl.BlockSpec((T, 512), lambda j, b: (tb(b), j))],
        out_shape=[SDS((S, 1024), f32), SDS((S, 4096), f32), SDS((S, 4096), f32)],
        scratch_shapes=[pltpu.VMEM((2, 8, 512), f32)],
        compiler_params=_params(("parallel", "arbitrary")), name=name)(u, BBr, BBi, CCr, CCi, abr, abi, pwr, pwi)


def _s5_scan_bwd(u, dy, xr, xi, prev, BBr, BBi, CCr, CCi, abr, abi, pwr, pwi, *, reverse, name):
    S = u.shape[0]
    T = S5_T
    NB = S // T
    L = T // S5_SEG
    d = 1 if reverse else 0
    adj_rev = not reverse

    def tb(b):
        return b if reverse else (NB - 1 - b)

    def bnd(b):
        t = tb(b)
        if reverse:
            return jnp.minimum((t + 1) * (T // 8), S // 8 - 1)
        return jnp.maximum(t * (T // 8) - 1, 0)

    def body(u_ref, dy_ref, xr_ref, xi_ref, prev_ref, xbr_ref, xbi_ref, bbr_ref, bbi_ref, ccr_ref, cci_ref, ar_ref, ai_ref,
             pwr_ref, pwi_ref, du_ref, dbbr_ref, dbbi_ref, dccr_ref, dcci_ref, dar_ref, dai_ref, carry, lam):
        b = pl.program_id(1)

        @pl.when(b == 0)
        def _():
            carry[...] = jnp.zeros_like(carry)
            for r in (dbbr_ref, dbbi_ref, dccr_ref, dcci_ref, dar_ref, dai_ref):
                r[...] = jnp.zeros_like(r)

        dyb = dy_ref[...]
        lam[0] = lax.dot_general(dyb, ccr_ref[0, 0], (NT, ((), ())), preferred_element_type=f32)
        lam[1] = -lax.dot_general(dyb, cci_ref[0, 0], (NT, ((), ())), preferred_element_type=f32)
        a6 = (0 if adj_rev else L - 1)
        _scan_core(lam.at[0], lam.at[1], ar_ref[...], -ai_ref[...], pwr_ref, pwi_ref, pwr_ref[a6:a6 + 1, :],
                   pwi_ref[a6:a6 + 1, :], carry, reverse=adj_rev, T=T, conj=True)
        ub = u_ref[...].astype(bf16)
        first = (b == NB - 1)
        lrb = lam[0].astype(bf16)
        lib = lam[1].astype(bf16)
        du_ref[...] = prev_ref[...] + lax.dot_general(lrb, bbr_ref[0, 0], (NT, ((), ())), preferred_element_type=f32) \
            + lax.dot_general(lib, bbi_ref[0, 0], (NT, ((), ())), preferred_element_type=f32)
        dbbr_ref[0, 0] += lax.dot_general(ub, lrb, (TN, ((), ())), preferred_element_type=f32)
        dbbi_ref[0, 0] += lax.dot_general(ub, lib, (TN, ((), ())), preferred_element_type=f32)
        dccr_ref[0, 0] += lax.dot_general(dyb, xr_ref[...].astype(bf16), (TN, ((), ())), preferred_element_type=f32)
        dcci_ref[0, 0] -= lax.dot_general(dyb, xi_ref[...].astype(bf16), (TN, ((), ())), preferred_element_type=f32)
        row8 = lax.broadcasted_iota(jnp.int32, (8, 512), 0)
        if reverse:
            body_x, body_l, edge_l = slice(8, T), slice(0, T - 8), slice(T - 8, T)
            sp_r = jnp.where(row8 == 7, jnp.where(first, 0.0, xbr_ref[0:1, :]), pltpu.roll(xr_ref[0:8, :], 7, axis=0))
            sp_i = jnp.where(row8 == 7, jnp.where(first, 0.0, xbi_ref[0:1, :]), pltpu.roll(xi_ref[0:8, :], 7, axis=0))
        else:
            body_x, body_l, edge_l = slice(0, T - 8), slice(8, T), slice(0, 8)
            sp_r = jnp.where(row8 == 0, jnp.where(first, 0.0, xbr_ref[7:8, :]), pltpu.roll(xr_ref[T - 8:T, :], 1, axis=0))
            sp_i = jnp.where(row8 == 0, jnp.where(first, 0.0, xbi_ref[7:8, :]), pltpu.roll(xi_ref[T - 8:T, :], 1, axis=0))
        xpr, xpi = xr_ref[body_x, :], xi_ref[body_x, :]
        lr, li = lam[0, body_l, :], lam[1, body_l, :]
        er, ei = lam[0, edge_l, :], lam[1, edge_l, :]
        dar_ref[...] += jnp.sum(xpr * lr + xpi * li, axis=0, keepdims=True) + jnp.sum(sp_r * er + sp_i * ei, axis=0, keepdims=True)
        dai_ref[...] += jnp.sum(xpr * li - xpi * lr, axis=0, keepdims=True) + jnp.sum(sp_r * ei - sp_i * er, axis=0, keepdims=True)

    mat = lambda shp: pl.BlockSpec((1, 1) + shp, lambda j, b: (d, j, 0, 0))
    omat = lambda shp: pl.BlockSpec((1, 1) + shp, lambda j, b: (0, j, 0, 0))
    vec = lambda r: pl.BlockSpec((r, 512), lambda j, b: (0, d * S5_NJ + j))
    blk = lambda w: pl.BlockSpec((T, w), lambda j, b: (tb(b), j))
    return pl.pallas_call(
        body, grid=(S5_NJ, NB),
        in_specs=[blk(128), blk(128), blk(512), blk(512), blk(128),
                  pl.BlockSpec((8, 512), lambda j, b: (bnd(b), j)), pl.BlockSpec((8, 512), lambda j, b: (bnd(b), j)),
                  mat((128, 512)), mat((128, 512)), mat((512, 128)), mat((512, 128)), vec(1), vec(1), vec(L), vec(L)],
        out_specs=[blk(128), omat((128, 512)), omat((128, 512)), omat((128, 512)), omat((128, 512)),
                   pl.BlockSpec((1, 512), lambda j, b: (0, j)), pl.BlockSpec((1, 512), lambda j, b: (0, j))],
        out_shape=[SDS((S, 1024), f32), SDS((1, 8, 128, 512), f32), SDS((1, 8, 128, 512), f32), SDS((1, 8, 128, 512), f32),
                   SDS((1, 8, 128, 512), f32), SDS((1, 4096), f32), SDS((1, 4096), f32)],
        scratch_shapes=[pltpu.VMEM((2, 8, 512), f32), pltpu.VMEM((2, T, 512), f32)],
        compiler_params=_params(("parallel", "arbitrary"), VMEM_BIG), name=name)(
            u, dy, xr, xi, prev, xr, xi, BBr, BBi, CCr, CCi, abr, abi, pwr, pwi)


def _silu_mul(g, u):
    return jax.nn.silu(g) * u


def _mixf(p0, p1, p2, z0, z1, z2):
    return jax.nn.sigmoid(z0) * p0 + jax.nn.sigmoid(z1) * p1 + jax.nn.sigmoid(z2) * p2


def _s5_act(yf, yb, u, dd):
    return jax.nn.gelu(yf + yb + dd * u)


def _glu(a, b):
    return a * jax.nn.sigmoid(b)


def _layer_fwd(x, w, tabs, l):
    S = x.shape[0]
    T = min(S, 1024)
    I = S // T
    nm = lambda s: f"L{l}_{s}"
    sv = {'x': x}
    h = _rmsnorm_fwd(x, w['norm1_g'], name=nm("norm1"))
    zr = _mm(h, w['W_ret'], name=nm("in_ret"))
    zm = _mm(h, w['W_mla'], name=nm("in_mla"))
    h_seg = _perm_rows(h, tabs['seg_perm'], name=nm("s5_perm_h"))
    zs = _mm(h_seg, w['W_s5'], name=nm("in_s5"))
    zg = _mm(h, w['W_gate'], out_dtype=bf16, name=nm("in_gate"))
    sv.update(h=h, h_seg=h_seg, zr=zr, zm=zm, zs=zs, zg=zg)

    yf, stf = _ret_dir_fwd(zr, w['lg'], tabs['cos_r'], tabs['sin_r'], reverse=False, name=nm("ret_f"))
    yb, stb = _ret_dir_fwd(zr, w['lg'], tabs['cos_r'], tabs['sin_r'], reverse=True, name=nm("ret_b"))
    hd = lambda j: j
    y_ret = _pw(_gn_gate, [yf, yb, zr, w['ret_gn_g']],
                [_row(T, 256, hd), _row(T, 256, hd), _row(T, 256, lambda j: 8 + j), _par(256, hd)],
                [SDS((S, 1024), bf16)], [_row(T, 256, hd)], (RET_HEADS, I), name=nm("ret_gn"))[0]
    sv.update(yf=yf, yb=yb, stf=stf, stb=stb, y_ret=y_ret)

    cqn, ckvn = _pw(lambda a, b, g1, g2: (_rms(a, g1), _rms(b, g2)), [zm, zm, w['mla_q_norm_g'], w['mla_kv_norm_g']],
                    [_row(T, 384), _row(T, 256, lambda j: 2), _par(384), _par(256)],
                    [SDS((S, 384), bf16), SDS((S, 256), bf16)], [_row(T, 384), _row(T, 256)], (1, I), name=nm("mla_norm"))
    q = _mm(cqn, w['W_uq'], out_dtype=bf16, name=nm("mla_uq"))
    kv = _mm(ckvn, w['W_ukv'], out_dtype=bf16, name=nm("mla_ukv"))
    Q, K = _mla_qk_prep(q, kv, zm, tabs['cos_m'], tabs['sin_m'], name=nm("mla_qkprep"))
    O, Lse = _flash_fwd(Q, K, kv, name=nm("mla_attn"))
    sv.update(cqn=cqn, ckvn=ckvn, kv=kv, Q=Q, K=K, O=O, Lse=Lse)

    s5 = w['s5']
    ysf, xrf, xif = _s5_scan_fwd(zs, s5['BBr'], s5['BBi'], s5['CCr'], s5['CCi'], s5['abr'], s5['abi'], s5['pwr_f'], s5['pwi_f'],
                                 reverse=False, name=nm("s5_f"))
    ysb, xrb, xib = _s5_scan_fwd(zs, s5['BBr'], s5['BBi'], s5['CCr'], s5['CCi'], s5['abr'], s5['abi'], s5['pwr_f'], s5['pwi_f'],
                                 reverse=True, name=nm("s5_b"))
    gact = _pw(_s5_act, [ysf, ysb, zs, w['s5_d']], [_row(T, D), _row(T, D), _row(T, D), _par(D)],
               [SDS((S, D), bf16)], [_row(T, D)], (1, I), name=nm("s5_act"))[0]
    gg = _mm(gact, w['W_glu'], out_dtype=bf16, name=nm("s5_glu_mm"))
    y_s5 = _pw(_glu, [gg, gg], [_row(T, D), _row(T, D, lambda j: 1)], [SDS((S, D), bf16)], [_row(T, D)], (1, I),
               name=nm("s5_glu"))[0]
    y_s5 = _perm_rows(y_s5, tabs['seg_unperm'], name=nm("s5_unperm_y"))
    sv.update(ysf=ysf, ysb=ysb, xrf=xrf, xif=xif, xrb=xrb, xib=xib, gact=gact, gg=gg, y_s5=y_s5)

    ys = [y_ret, O, y_s5]
    pr = [_mm(ys[i], w['W_br'][i], out_dtype=bf16, name=nm(f"branch{i}")) for i in range(3)]
    mix = _pw(_mixf, pr + [zg, zg, zg],
              [_row(T, D)] * 3 + [_row(T, D), _row(T, D, lambda j: 1), _row(T, D, lambda j: 2)],
              [SDS((S, D), bf16)], [_row(T, D)], (1, I), name=nm("mix"))[0]
    x1 = _mm(mix, w['W_out'], res=x, name=nm("out_proj"))
    h2 = _rmsnorm_fwd(x1, w['norm2_g'], name=nm("norm2"))
    fgu = _mm(h2, w['W_gu'], out_dtype=bf16, name=nm("ffn_gu"))
    act = _pw(_silu_mul, [fgu, fgu], [_row(T, 1408, lambda j: j), _row(T, 1408, lambda j: 2 + j)],
              [SDS((S, FFN_H), bf16)], [_row(T, 1408, lambda j: j)], (2, I), name=nm("ffn_act"))[0]
    x2 = _mm(act, w['W_down'], res=x1, name=nm("ffn_down"))
    sv.update(pr=pr, mix=mix, x1=x1, h2=h2, fgu=fgu, act=act)
    return x2, sv


def _vjp_fn(fn, n_primal, cast=None):
    def g(*args):
        _, vjp = jax.vjp(fn, *args[:n_primal])
        return vjp(args[n_primal].astype(f32))
    return g


def _layer_bwd(dx2, w, tabs, sv, l):
    S = dx2.shape[0]
    T = min(S, 1024)
    I = S // T
    nm = lambda s: f"L{l}_b_{s}"
    g = {}
    hd = lambda j: j

    dact = _mm(dx2, w['W_down'], tb=True, out_dtype=bf16, name=nm("ffn_down_dx"))
    g['W_down'] = _mmT(sv['act'], dx2, name=nm("ffn_down_dw"))
    dfg, dfu = _pw(_vjp_fn(_silu_mul, 2), [sv['fgu'], sv['fgu'], dact],
                   [_row(T, 1408, lambda j: j), _row(T, 1408, lambda j: 2 + j), _row(T, 1408, lambda j: j)],
                   [SDS((S, FFN_H), bf16), SDS((S, FFN_H), bf16)], [_row(T, 1408, lambda j: j)] * 2, (2, I), name=nm("ffn_act"))
    dfgu = jnp.concatenate([dfg, dfu], axis=1)
    g['W_gu'] = _mmT(sv['h2'], dfgu, name=nm("ffn_gu_dw"))
    dh2 = _mm(dfgu, w['W_gu'], tb=True, name=nm("ffn_gu_dx"))
    dx1, g['norm2_g'] = _rmsnorm_bwd(sv['x1'], w['norm2_g'], dh2, dx2, name=nm("norm2"))

    dmix = _mm(dx1, w['W_out'], tb=True, out_dtype=bf16, name=nm("out_dx"))
    g['W_out'] = _mmT(sv['mix'], dx1, name=nm("out_dw"))
    zg = sv['zg']
    Th = min(S, 512)
    outs = _pw(_vjp_fn(_mixf, 6), sv['pr'] + [zg, zg, zg, dmix],
               [_row(Th, D)] * 3 + [_row(Th, D), _row(Th, D, lambda j: 1), _row(Th, D, lambda j: 2), _row(Th, D)],
               [SDS((S, D), bf16)] * 6, [_row(Th, D)] * 6, (1, S // Th), name=nm("mix"))
    dpr, dzg = outs[:3], jnp.concatenate(outs[3:], axis=1)
    ys = [sv['y_ret'], sv['O'], sv['y_s5']]
    g['W_br'] = [_mmT(ys[i], dpr[i], name=nm(f"branch{i}_dw")) for i in range(3)]
    dpr_seg = _perm_rows(dpr[2], tabs['seg_perm'], name=nm("s5_perm_dy"))
    dys = [_mm(dpr[i] if i < 2 else dpr_seg, w['W_br'][i], tb=True, out_dtype=bf16,
               name=nm(f"branch{i}_dx")) for i in range(3)]

    gg = sv['gg']
    dga, dgb = _pw(_vjp_fn(_glu, 2), [gg, gg, dys[2]], [_row(T, D), _row(T, D, lambda j: 1), _row(T, D)],
                   [SDS((S, D), bf16)] * 2, [_row(T, D)] * 2, (1, I), name=nm("s5_glu"))
    dgg = jnp.concatenate([dga, dgb], axis=1)
    g['W_glu'] = _mmT(sv['gact'], dgg, name=nm("s5_glu_dw"))
    dgact = _mm(dgg, w['W_glu'], tb=True, out_dtype=bf16, name=nm("s5_glu_dx"))

    def act_bwd(yf, yb, u, dd, ct):
        _, vjp = jax.vjp(_s5_act, yf, yb, u, dd)
        dyf, _, du, ddd = vjp(ct)
        return dyf, du, ddd

    dys5, du_direct, g['s5_d'] = _pw(act_bwd, [sv['ysf'], sv['ysb'], sv['zs'], w['s5_d'], dgact],
                                     [_row(T, D)] * 3 + [_par(D), _row(T, D)],
                                     [SDS((S, D), bf16), SDS((S, D), f32), SDS((1, D), f32)],
                                     [_row(T, D), _row(T, D), _par(D)], (1, I), n_acc=1, name=nm("s5_act"))
    s5 = w['s5']
    rf = _s5_scan_bwd(sv['zs'], dys5, sv['xrf'], sv['xif'], du_direct, s5['BBr'], s5['BBi'], s5['CCr'], s5['CCi'], s5['abr'],
                      s5['abi'], s5['pwr_a'], s5['pwi_a'], reverse=False, name=nm("s5_f"))
    rb = _s5_scan_bwd(sv['zs'], dys5, sv['xrb'], sv['xib'], rf[0], s5['BBr'], s5['BBi'], s5['CCr'], s5['CCi'], s5['abr'],
                      s5['abi'], s5['pwr_a'], s5['pwi_a'], reverse=True, name=nm("s5_b"))
    g['s5'] = (rf[1:], rb[1:])
    dzs_seg = rb[0]
    dzs = _perm_rows(dzs_seg, tabs['seg_unperm'], name=nm("s5_unperm_dz"))

    delta = _attn_delta(dys[1], sv['O'], name=nm("mla_delta"))
    dQ, dK, dV = _flash_bwd(sv['Q'], sv['K'], sv['kv'], delta, sv['Lse'], dys[1], name=nm("mla_attn"))
    dq_lin, dkv, dkr = _mla_bwd_prep(dQ, dK, dV, tabs['cos_m'], tabs['sin_m'], name=nm("mla_prep"))
    g['W_uq'] = _mmT(sv['cqn'], dq_lin, name=nm("mla_uq_dw"))
    dcqn = _mm(dq_lin, w['W_uq'], tb=True, name=nm("mla_uq_dx"))
    g['W_ukv'] = _mmT(sv['ckvn'], dkv, name=nm("mla_ukv_dw"))
    dckvn = _mm(dkv, w['W_ukv'], tb=True, name=nm("mla_ukv_dx"))
    dzm, g['mla_q_norm_g'], g['mla_kv_norm_g'] = _mla_norm_bwd(sv['zm'], w['mla_q_norm_g'], w['mla_kv_norm_g'], dcqn, dckvn, dkr,
                                                               name=nm("mla_norm"))

    zr = sv['zr']

    def gn_bwd(yf, yb, gt, gn, ct):
        _, vjp = jax.vjp(_gn_gate, yf, yb, gt, gn)
        dyf, _, dgt, dgn = vjp(ct)
        return dyf, dgt, dgn

    dyr, dgate, g['ret_gn_g'] = _pw(gn_bwd, [sv['yf'], sv['yb'], zr, w['ret_gn_g'], dys[0]],
                                    [_row(T, 256, hd), _row(T, 256, hd), _row(T, 256, lambda j: 8 + j), _par(256, hd),
                                     _row(T, 256, hd)],
                                    [SDS((S, 1024), bf16), SDS((S, 1024), bf16), SDS((1, 1024), f32)],
                                    [_row(T, 256, hd), _row(T, 256, hd), _par(256, hd)], (RET_HEADS, I), n_acc=1, name=nm("ret_gn"))
    qf, kf, vf, lgf = _ret_dir_bwd(zr, w['lg'], tabs['cos_r'], tabs['sin_r'], dyr, sv['stf'], reverse=False, name=nm("ret_f"))
    qb, kb, vb, lgb = _ret_dir_bwd(zr, w['lg'], tabs['cos_r'], tabs['sin_r'], dyr, sv['stb'], reverse=True, name=nm("ret_b"))
    g['lg'] = jnp.stack([lgf[:, 0, 0], lgb[:, 0, 0]])
    dzr = _pw(lambda a, b, c, d, e, f, gt: jnp.concatenate([a + b, c + d, e + f, gt], axis=1),
              [qf, qb, kf, kb, vf, vb, dgate], [_row(256, 512)] * 4 + [_row(256, D)] * 3,
              [SDS((S, 3072), bf16)], [_row(256, 3072)], (1, S // 256), name=nm("ret_dz"))[0]

    h = sv['h']
    g['W_ret'] = _mmT(h, dzr, name=nm("in_ret_dw"))
    g['W_mla'] = _mmT(h, dzm, name=nm("in_mla_dw"))
    g['W_s5'] = _mmT(sv['h_seg'], dzs_seg, name=nm("in_s5_dw"))
    g['W_gate'] = _mmT(h, dzg, name=nm("in_gate_dw"))
    dh = _mm(dzr, w['W_ret'], tb=True, name=nm("in_ret_dx"))
    dh = _mm(dzm, w['W_mla'], tb=True, res=dh, name=nm("in_mla_dx"))
    dh = _mm(dzs, w['W_s5'], tb=True, res=dh, name=nm("in_s5_dx"))
    dh = _mm(dzg, w['W_gate'], tb=True, res=dh, name=nm("in_gate_dx"))
    dx, g['norm1_g'] = _rmsnorm_bwd(sv['x'], w['norm1_g'], dh, dx1, name=nm("norm1"))
    return dx, g


def _loss_head(x, tgt, gain, *, name):
    S, W = x.shape
    T = min(S, 512)

    def loss_fn(xv, gv, tv):
        return 0.5 * jnp.sum(jnp.mean(jnp.square(_rms(xv, gv) - tv), axis=-1, keepdims=True), axis=0, keepdims=True)

    def fn(xv, gv, tv):
        lv, vjp = jax.vjp(lambda a, b: loss_fn(a, b, tv), xv, gv)
        dx, dg = vjp(jnp.ones((1, 1), f32))
        return dx, jnp.broadcast_to(lv, (1, 128)), dg

    return _pw(fn, [x, gain, tgt], [_row(T, W), _par(W), _row(T, W)],
               [SDS((S, W), f32), SDS((1, 128), f32), SDS((1, W), f32)], [_row(T, W), _par(128), _par(W)],
               (1, S // T), n_acc=2, name=name)


def _rope_tabs(S):
    def tab(dim):
        inv = 1.0 / (ROPE_THETA ** (jnp.arange(0, dim, 2, dtype=f32) / dim))
        ang = jnp.arange(S, dtype=f32)[:, None] * inv[None, :]
        return jnp.cos(ang), jnp.sin(ang)

    cr, sr = tab(RET_DK)
    cm, sm = tab(MLA_ROPE)
    z = jnp.zeros((S, 64), f32)
    return {'cos_r': jnp.concatenate([cr, cr], axis=1), 'sin_r': jnp.concatenate([-sr, sr], axis=1),
            'cos_m': jnp.concatenate([cm, cm, z], axis=1), 'sin_m': jnp.concatenate([-sm, sm, z], axis=1),
            'seg_perm': _seg_perm(S5_T, False), 'seg_unperm': _seg_perm(S5_T, True)}


def _bd_B(bb):
    b5 = bb.reshape(16, 2, 8, 8, 64)
    return jnp.einsum('cdjgp,gh->djgchp', b5, jnp.eye(8, dtype=bb.dtype)).reshape(2, 8, 128, 512)


def _bd_B_t(dBB):
    return jnp.einsum('djgcgp->cdjgp', dBB.reshape(2, 8, 8, 16, 8, 64)).reshape(16, 8192)


def _bd_C(c):
    c5 = c.reshape(2, 8, 8, 16, 64)
    return jnp.einsum('djgcp,gh->djgphc', c5, jnp.eye(8, dtype=c.dtype)).reshape(2, 8, 512, 128)


def _s5_rows(p, l):
    a_re = p['s5_a_re'][l].reshape(1, 8192)
    a_im = p['s5_a_im'][l].reshape(1, 8192)
    ldt = jnp.broadcast_to(p['s5_log_dt'][l][:, :, None], (2, S5_G, S5_P)).reshape(1, 8192)
    b_re = p['s5_b_re'][l].transpose(3, 0, 1, 2).reshape(16, 8192)
    b_im = p['s5_b_im'][l].transpose(3, 0, 1, 2).reshape(16, 8192)
    return a_re, a_im, ldt, b_re, b_im


def _layer_weights(big, p, l):
    w_in = big['w_in'][l]
    z = lambda n: jnp.zeros((D, n), w_in.dtype)
    w = {
        'W_ret': w_in[:, 0:3072],
        'W_mla': jnp.concatenate([w_in[:, 3072:3456], z(128), w_in[:, 3456:3712], w_in[:, 3712:3776], z(64)], axis=1),
        'W_s5': w_in[:, 3776:4800],
        'W_gate': w_in[:, 4800:7872],
        'W_uq': jnp.pad(big['mla_w_uq'][l].reshape(MLA_Q_LORA, MLA_HEADS, 192), ((0, 0), (0, 0), (0, 64))).reshape(MLA_Q_LORA, 2048),
        'W_ukv': big['mla_w_ukv'][l],
        'W_glu': big['s5_w_glu'][l],
        'W_br': [big['w_branch'][l, i] for i in range(3)],
        'W_out': big['w_out'][l],
        'W_gu': big['ffn_w_gu'][l],
        'W_down': big['ffn_w_down'][l],
    }
    for n in ('norm1_g', 'ret_gn_g', 'mla_q_norm_g', 'mla_kv_norm_g', 's5_d', 'norm2_g'):
        w[n] = p[n][l][None, :]
    w['lg'] = jax.nn.log_sigmoid(p['ret_decay'][l])
    rows = _s5_rows(p, l)
    abr, abi, bbr, bbi, pwr, pwi = _s5_param_fwd(*rows, name=f"L{l}_s5_param")
    flip = lambda t, first: jnp.concatenate([t[::-1, :4096], t[:, 4096:]] if first else [t[:, :4096], t[::-1, 4096:]], axis=1)
    w['s5'] = {'abr': abr, 'abi': abi, 'BBr': _bd_B(bbr).astype(bf16), 'BBi': _bd_B(bbi).astype(bf16),
               'CCr': _bd_C(p['s5_c_re'][l]).astype(bf16), 'CCi': _bd_C(p['s5_c_im'][l]).astype(bf16),
               'pwr_f': flip(pwr, False), 'pwi_f': flip(pwi, False), 'pwr_a': flip(pwr, True), 'pwi_a': flip(pwi, True),
               'rows': rows}
    return w


def _layer_grads(g, w, p, l):
    out = {}
    m = g['W_mla']
    out['w_in'] = jnp.concatenate([g['W_ret'], m[:, 0:384], m[:, 512:768], m[:, 768:832], g['W_s5'], g['W_gate']], axis=1)
    out['mla_w_uq'] = g['W_uq'].reshape(MLA_Q_LORA, MLA_HEADS, 256)[:, :, :192].reshape(MLA_Q_LORA, 1536)
    out['mla_w_ukv'] = g['W_ukv']
    out['s5_w_glu'] = g['W_glu']
    out['w_branch'] = jnp.stack(g['W_br'])
    out['w_out'] = g['W_out']
    out['ffn_w_gu'] = g['W_gu']
    out['ffn_w_down'] = g['W_down']
    for n in ('norm1_g', 'ret_gn_g', 'mla_q_norm_g', 'mla_kv_norm_g', 's5_d', 'norm2_g'):
        out[n] = g[n][0]
    out['ret_decay'] = g['lg'] * jax.nn.sigmoid(-p['ret_decay'][l])
    (fB_r, fB_i, fC_r, fC_i, fa_r, fa_i), (bB_r, bB_i, bC_r, bC_i, ba_r, ba_i) = g['s5']
    cat = lambda a, b: jnp.concatenate([a, b], axis=0)
    d_bbr = _bd_B_t(cat(fB_r, bB_r))
    d_bbi = _bd_B_t(cat(fB_i, bB_i))
    to_c = lambda t: _bd_B_t(t).reshape(16, 2, S5_G, S5_P).transpose(1, 2, 0, 3)
    out['s5_c_re'] = to_c(cat(fC_r, bC_r))
    out['s5_c_im'] = to_c(cat(fC_i, bC_i))
    d_abr = jnp.concatenate([fa_r, ba_r], axis=1)
    d_abi = jnp.concatenate([fa_i, ba_i], axis=1)
    da_re, da_im, dldt, db_re, db_im = _s5_param_bwd(*w['s5']['rows'], d_abr, d_abi, d_bbr, d_bbi, name=f"L{l}_b_s5_param")
    out['s5_a_re'] = da_re.reshape(2, S5_G, S5_P)
    out['s5_a_im'] = da_im.reshape(2, S5_G, S5_P)
    out['s5_log_dt'] = dldt.reshape(2, S5_G, S5_P).sum(axis=-1)
    out['s5_b_re'] = db_re.reshape(16, 2, S5_G, S5_P).transpose(1, 2, 3, 0)
    out['s5_b_im'] = db_im.reshape(16, 2, S5_G, S5_P).transpose(1, 2, 3, 0)
    return out


def _local_step(x, tgt, big, p):
    S = x.shape[0]
    assert S % S5_T == 0
    tabs = _rope_tabs(S)
    ws, svs = [], []
    h = x
    for l in range(DEPTH):
        w = _layer_weights(big, p, l)
        h, sv = _layer_fwd(h, w, tabs, l)
        ws.append(w)
        svs.append(sv)
    dx, lossv, dfinal = _loss_head(h, tgt, p['final_g'][None, :], name="loss_head")
    per_layer = [None] * DEPTH
    for l in reversed(range(DEPTH)):
        dx, g = _layer_bwd(dx, ws[l], tabs, svs[l], l)
        per_layer[l] = _layer_grads(g, ws[l], p, l)
    return lossv[0, 0], dx, per_layer, dfinal[0]


_ANY = pl.BlockSpec(memory_space=pl.ANY)


def _place():
    x, y, c = lax.axis_index("x"), lax.axis_index("y"), lax.axis_index("c")
    return x, y, c, [(1 - x, y), (x, 1 - y), (1 - x, 1 - y)]


def _allgather4(arrs, *, name):
    n = len(arrs)

    def body(*refs):
        ins, outs = refs[:n], refs[n:2 * n]
        send, recv, loc = refs[2 * n:]
        x, y, c, chips = _place()
        me = 2 * x + y

        def remote(a, k, slot):
            px, py = chips[k]
            return pltpu.make_async_remote_copy(src_ref=ins[a], dst_ref=outs[a].at[slot], send_sem=send.at[a, k],
                                                recv_sem=recv.at[a, k], device_id=(px, py, c), device_id_type=MESH)

        mine = [pltpu.make_async_copy(ins[a], outs[a].at[me], loc.at[a]) for a in range(n)]
        for cp in mine:
            cp.start()
        sends = [remote(a, k, me) for a in range(n) for k in range(3)]
        for cp in sends:
            cp.start()
        for a in range(n):
            for k, (px, py) in enumerate(chips):
                remote(a, k, 2 * px + py).wait_recv()
        for cp in sends:
            cp.wait_send()
        for cp in mine:
            cp.wait()

    return pl.pallas_call(
        body, in_specs=[_ANY] * n, out_specs=[_ANY] * n, out_shape=[SDS((4,) + a.shape, a.dtype) for a in arrs],
        scratch_shapes=[pltpu.SemaphoreType.DMA((n, 3)), pltpu.SemaphoreType.DMA((n, 3)), pltpu.SemaphoreType.DMA((n,))],
        name=name)(*arrs)


def _rs_exchange(parts, *, name):
    n = len(parts)

    def body(*refs):
        ins, gots = refs[:n], refs[n:2 * n]
        send, recv = refs[2 * n:]
        x, y, c, chips = _place()

        def remote(a, k):
            px, py = chips[k]
            return pltpu.make_async_remote_copy(src_ref=ins[a].at[2 * px + py], dst_ref=gots[a].at[k], send_sem=send.at[a, k],
                                                recv_sem=recv.at[a, k], device_id=(px, py, c), device_id_type=MESH)

        sends = [remote(a, k) for a in range(n) for k in range(3)]
        for cp in sends:
            cp.start()
        for cp in sends:
            cp.wait_recv()
        for cp in sends:
            cp.wait_send()

    return pl.pallas_call(
        body, in_specs=[_ANY] * n, out_specs=[_ANY] * n, out_shape=[SDS((3,) + a.shape[1:], a.dtype) for a in parts],
        scratch_shapes=[pltpu.SemaphoreType.DMA((n, 3)), pltpu.SemaphoreType.DMA((n, 3))], name=name)(*parts)


def _gather_split(arrs, *, name):
    n = len(arrs)

    def body(*refs):
        ins, outs = refs[:n], refs[n:2 * n]
        s_ici, r_ici, s_sib, r_sib, loc = refs[2 * n:]
        x, y, c, chips = _place()
        me = 2 * x + y
        ids = [2 * px + py for px, py in chips] + [me]

        def over_ici(a, k, slot):
            px, py = chips[k]
            return pltpu.make_async_remote_copy(src_ref=ins[a].at[c], dst_ref=outs[a].at[slot, c], send_sem=s_ici.at[a, k],
                                                recv_sem=r_ici.at[a, k], device_id=(px, py, c), device_id_type=MESH)

        def to_sibling(a, k, half, src=None):
            blk = outs[a].at[ids[k], half]
            return pltpu.make_async_remote_copy(src_ref=blk if src is None else src, dst_ref=blk, send_sem=s_sib.at[a, k],
                                                recv_sem=r_sib.at[a, k], device_id=(x, y, 1 - c), device_id_type=MESH)

        sends = [over_ici(a, k, me) for a in range(n) for k in range(3)]
        sends += [to_sibling(a, 3, c, src=ins[a].at[c]) for a in range(n)]
        for cp in sends:
            cp.start()
        mine = [pltpu.make_async_copy(ins[a].at[c], outs[a].at[me, c], loc.at[a]) for a in range(n)]
        for cp in mine:
            cp.start()
        for a in range(n):
            for k in range(3):
                over_ici(a, k, ids[k]).wait_recv()
                fwd = to_sibling(a, k, c)
                fwd.start()
                sends.append(fwd)
        for a in range(n):
            for k in range(4):
                to_sibling(a, k, 1 - c).wait_recv()
        for cp in sends:
            cp.wait_send()
        for cp in mine:
            cp.wait()

    dma = pltpu.SemaphoreType.DMA
    return pl.pallas_call(
        body, in_specs=[_ANY] * n, out_specs=[_ANY] * n, out_shape=[SDS((4,) + a.shape, a.dtype) for a in arrs],
        scratch_shapes=[dma((n, 3)), dma((n, 3)), dma((n, 4)), dma((n, 4)), dma((n,))], name=name)(*arrs)


def _swap_halves(parts, *, name):
    n = len(parts)

    def body(*refs):
        ins, gots = refs[:n], refs[n:2 * n]
        send, recv = refs[2 * n:]
        x, y, c, _ = _place()
        cps = [pltpu.make_async_remote_copy(src_ref=ins[a].at[q, 1 - c], dst_ref=gots[a].at[q], send_sem=send.at[a, q],
                                            recv_sem=recv.at[a, q], device_id=(x, y, 1 - c), device_id_type=MESH)
               for a in range(n) for q in range(4)]
        for cp in cps:
            cp.start()
        for cp in cps:
            cp.wait_recv()
        for cp in cps:
            cp.wait_send()

    dma = pltpu.SemaphoreType.DMA
    return pl.pallas_call(
        body, in_specs=[_ANY] * n, out_specs=[_ANY] * n, out_shape=[SDS((4,) + a.shape[2:], a.dtype) for a in parts],
        scratch_shapes=[dma((n, 4)), dma((n, 4))], name=name)(*parts)


def _sibling_copy(arrs, *, name):
    n = len(arrs)

    def body(*refs):
        ins, outs = refs[:n], refs[n:2 * n]
        send, recv = refs[2 * n:]
        x, y, c, _ = _place()
        cps = [pltpu.make_async_remote_copy(src_ref=ins[a], dst_ref=outs[a], send_sem=send.at[a], recv_sem=recv.at[a],
                                            device_id=(x, y, 1 - c), device_id_type=MESH) for a in range(n)]
        for cp in cps:
            cp.start()
        for cp in cps:
            cp.wait_recv()
        for cp in cps:
            cp.wait_send()

    dma = pltpu.SemaphoreType.DMA
    return pl.pallas_call(
        body, in_specs=[_ANY] * n, out_specs=[_ANY] * n, out_shape=[SDS(a.shape, a.dtype) for a in arrs],
        scratch_shapes=[dma((n,)), dma((n,))], name=name)(*arrs)


def _row_tile(R):
    return R if R <= 256 else next(t for t in (256, 128, 64, 32, 16) if R % t == 0)


def _add2(a, b, *, name):
    R, W = a.shape
    tr = _row_tile(R)
    return _pw(lambda p, q: p.astype(f32) + q.astype(f32), [a, b], [_row(tr, W)] * 2, [SDS((R, W), a.dtype)], [_row(tr, W)],
               (1, R // tr), name=name)[0]


def _sum4(own, got, *, name):
    R, W = own.shape
    tr = _row_tile(R)
    g3 = lambda k: pl.BlockSpec((None, tr, W), lambda j, i: (k, i, 0))
    up = lambda t: t.astype(f32)
    return _pw(lambda a, b, c, d: ((up(a) + up(b)) + up(c)) + up(d), [own, got, got, got], [_row(tr, W), g3(0), g3(1), g3(2)],
               [SDS((R, W), f32)], [_row(tr, W)], (1, R // tr), name=name)[0]


def _adamw(g, w, m, v, *, name):
    R, W = w.shape
    tr = _row_tile(R)

    def fn(gv, wv, mv, vv):
        m2 = ADAM_B1 * mv + (1.0 - ADAM_B1) * gv
        v2 = ADAM_B2 * vv + (1.0 - ADAM_B2) * jnp.square(gv)
        m_hat = m2 / (1.0 - ADAM_B1 ** ADAM_STEP)
        v_hat = v2 / (1.0 - ADAM_B2 ** ADAM_STEP)
        return -ADAM_LR * (m_hat / (jnp.sqrt(v_hat) + ADAM_EPS) + ADAM_WD * wv), m2, v2

    return _pw(fn, [g, w, m, v], [_row(tr, W)] * 4, [SDS((R, W), f32)] * 3, [_row(tr, W)] * 3, (1, R // tr), name=name)


def _to_parts(g, axis):
    shp = g.shape
    g = g.reshape(shp[:axis] + (4, shp[axis] // 4) + shp[axis + 1:])
    return jnp.moveaxis(g, axis, 0)


def _from_parts(pt, axis):
    g = jnp.moveaxis(pt, 0, axis)
    shp = g.shape
    return g.reshape(shp[:axis] + (4 * shp[axis + 1],) + shp[axis + 2:])


def kernel(x, norm1_g, w_in, ret_decay, ret_gn_g, mla_q_norm_g, mla_w_uq, mla_kv_norm_g, mla_w_ukv, s5_a_re, s5_a_im, s5_log_dt, s5_b_re, s5_b_im, s5_c_re, s5_c_im, s5_d, s5_w_glu, w_branch, w_out, norm2_g, ffn_w_gu, ffn_w_down, final_g, loss_target, m_norm1_g, m_w_in, m_ret_decay, m_ret_gn_g, m_mla_q_norm_g, m_mla_w_uq, m_mla_kv_norm_g, m_mla_w_ukv, m_s5_a_re, m_s5_a_im, m_s5_log_dt, m_s5_b_re, m_s5_b_im, m_s5_c_re, m_s5_c_im, m_s5_d, m_s5_w_glu, m_w_branch, m_w_out, m_norm2_g, m_ffn_w_gu, m_ffn_w_down, m_final_g, v_norm1_g, v_w_in, v_ret_decay, v_ret_gn_g, v_mla_q_norm_g, v_mla_w_uq, v_mla_kv_norm_g, v_mla_w_ukv, v_s5_a_re, v_s5_a_im, v_s5_log_dt, v_s5_b_re, v_s5_b_im, v_s5_c_re, v_s5_c_im, v_s5_d, v_s5_w_glu, v_w_branch, v_w_out, v_norm2_g, v_ffn_w_gu, v_ffn_w_down, v_final_g):
    wv = dict(zip(W_NAMES, (norm1_g, w_in, ret_decay, ret_gn_g, mla_q_norm_g, mla_w_uq, mla_kv_norm_g, mla_w_ukv, s5_a_re, s5_a_im,
                            s5_log_dt, s5_b_re, s5_b_im, s5_c_re, s5_c_im, s5_d, s5_w_glu, w_branch, w_out, norm2_g, ffn_w_gu,
                            ffn_w_down, final_g)))
    mv = dict(zip(W_NAMES, (m_norm1_g, m_w_in, m_ret_decay, m_ret_gn_g, m_mla_q_norm_g, m_mla_w_uq, m_mla_kv_norm_g, m_mla_w_ukv,
                            m_s5_a_re, m_s5_a_im, m_s5_log_dt, m_s5_b_re, m_s5_b_im, m_s5_c_re, m_s5_c_im, m_s5_d, m_s5_w_glu,
                            m_w_branch, m_w_out, m_norm2_g, m_ffn_w_gu, m_ffn_w_down, m_final_g)))
    vv = dict(zip(W_NAMES, (v_norm1_g, v_w_in, v_ret_decay, v_ret_gn_g, v_mla_q_norm_g, v_mla_w_uq, v_mla_kv_norm_g, v_mla_w_ukv,
                            v_s5_a_re, v_s5_a_im, v_s5_log_dt, v_s5_b_re, v_s5_b_im, v_s5_c_re, v_s5_c_im, v_s5_d, v_s5_w_glu,
                            v_w_branch, v_w_out, v_norm2_g, v_ffn_w_gu, v_ffn_w_down, v_final_g)))
    big_names = list(BIG)

    my_c = lax.axis_index("c")
    my_chip = 2 * lax.axis_index("x") + lax.axis_index("y")
    shards = [wv[n].astype(bf16) for n in big_names]
    gathered = _gather_split(shards, name="gather_weights")
    big = {n: _from_parts(gt, BIG[n]) for n, gt in zip(big_names, gathered)}
    small = {n: wv[n] for n in SMALL}

    loss_local, dx, layer_grads, d_final = _local_step(x[0], loss_target[0], big, small)
    grads = {n: jnp.stack([layer_grads[l][n] for l in range(DEPTH)]) for n in SMALL if n != 'final_g'}
    grads['final_g'] = d_final

    n_rows = {n: -(-math.prod(wv[n].shape) // 1024) * 8 for n in SMALL}
    used = sum(n_rows.values())
    rows_q = -(-(used + 8) // (4 * 128)) * 128

    def as_rows(d, tail=None):
        blocks = [jnp.pad(d[n].reshape(-1), (0, n_rows[n] * 128 - math.prod(wv[n].shape))).reshape(n_rows[n], 128) for n in SMALL]
        blocks.append(jnp.zeros((8, 128), f32) if tail is None else tail)
        blocks.append(jnp.zeros((4 * rows_q - used - 8, 128), f32))
        return jnp.concatenate(blocks, axis=0)

    loss_rows = jnp.full((8, 128), loss_local, f32)
    parts = [jnp.stack([_to_parts(layer_grads[l][n].astype(bf16), BIG[n] - 1) for l in range(DEPTH)], axis=1) for n in big_names]
    parts.append(as_rows(grads, loss_rows).reshape(4, 2, rows_q // 2, 128))
    n_arr = len(parts)
    two_d = lambda a: a.reshape(-1, a.shape[-1])
    theirs = _swap_halves(parts, name="grad_swap_halves")
    mine = [jnp.where(my_c == 0, p[:, 0], p[:, 1]) for p in parts]
    chip_sums = [_add2(two_d(mine[a]), two_d(theirs[a]), name=f"grad_add2_{a}").reshape(theirs[a].shape) for a in range(n_arr)]
    got = _rs_exchange(chip_sums, name="grad_exchange")

    def pick_chip(s):
        r = s[0]
        for q in range(1, 4):
            r = jnp.where(my_chip == q, s[q], r)
        return r

    own = [pick_chip(s) for s in chip_sums]
    sums = [_sum4(two_d(own[a]), got[a].reshape(3, -1, got[a].shape[-1]), name=f"grad_sum4_{a}") for a in range(n_arr)]
    other = _sibling_copy(sums, name="grad_sibling")
    full = [jnp.stack([jnp.where(my_c == 0, sums[a], other[a]), jnp.where(my_c == 0, other[a], sums[a])]) for a in range(n_arr)]

    out_g, out_d, out_m, out_v = {}, {}, {}, {}
    for a, n in enumerate(big_names):
        shp = wv[n].shape
        res = _adamw(two_d(full[a]), two_d(wv[n]), two_d(mv[n]), two_d(vv[n]), name=f"adamw_{n}")
        out_g[n] = full[a].reshape(shp)
        out_d[n], out_m[n], out_v[n] = [r.reshape(shp) for r in res]
    g_small = _allgather4([full[-1].reshape(rows_q, 128)], name="gather_small_grads")[0].reshape(4 * rows_q, 128)
    loss = g_small[used, 0]
    off = 0
    for n in SMALL:
        shp = wv[n].shape
        k = math.prod(shp)
        flat2 = (k // 128, 128) if k % 128 == 0 else (1, k)
        g_n = g_small[off:off + n_rows[n]].reshape(-1)[:k].reshape(flat2)
        res = _adamw(g_n, wv[n].reshape(flat2), mv[n].reshape(flat2), vv[n].reshape(flat2), name=f"adamw_{n}")
        out_g[n] = g_n.reshape(shp)
        out_d[n], out_m[n], out_v[n] = [r.reshape(shp) for r in res]
        off += n_rows[n]
    return (loss, dx[None], *[out_g[n] for n in W_NAMES], *[out_d[n] for n in W_NAMES], *[out_m[n] for n in W_NAMES],
            *[out_v[n] for n in W_NAMES])
```
